```python
import math
import jax, jax.numpy as jnp
from jax import lax
import numpy as np

D_MODEL = 2048
BATCH = 8
SEQ = 2048
DEPTH = 1

MIX_WIDTH = D_MODEL
ATTN_WIDTH = MIX_WIDTH // 2
DELTA_WIDTH = MIX_WIDTH - ATTN_WIDTH

ATTN_HEAD_DIM = 64
N_ATTN_HEADS = ATTN_WIDTH // ATTN_HEAD_DIM
N_KV_HEADS = N_ATTN_HEADS // 4
GQA_GROUP = N_ATTN_HEADS // N_KV_HEADS
WINDOW = 128
ATTN_BLOCK = 128
NEG_INF = -1e30

N_BUCKETS = 32
MAX_DISTANCE = 128

DELTA_HEAD_DIM = 128
N_DELTA_HEADS = DELTA_WIDTH // DELTA_HEAD_DIM
CONV_WIDTH = 4
CHUNK = 64

D_FF = 4 * D_MODEL

DN_ALPHA = (2.0 * DEPTH) ** 0.25
DN_BETA = (8.0 * DEPTH) ** -0.25
LN_EPS = 1e-5
RMS_EPS = 1e-6

COL_ATTN_Q = N_ATTN_HEADS * ATTN_HEAD_DIM
COL_ATTN_KV = N_KV_HEADS * ATTN_HEAD_DIM
COL_DELTA_QKV = 3 * DELTA_WIDTH
COL_DELTA_SCALAR = N_DELTA_HEADS
COL_DELTA_Z = DELTA_WIDTH
N_IN_COLS = COL_ATTN_Q + 2 * COL_ATTN_KV + COL_DELTA_QKV + 2 * COL_DELTA_SCALAR + COL_DELTA_Z
SPLIT_POINTS = [int(s) for s in np.cumsum([COL_ATTN_Q, COL_ATTN_KV, COL_ATTN_KV, COL_DELTA_QKV,
                                              COL_DELTA_SCALAR, COL_DELTA_SCALAR])]

kernel_name = "hymba_swa_sink_gdn_deepnorm"


def layer_norm(x, g, b):
    xf = x.astype(jnp.float32)
    mu = jnp.mean(xf, axis=-1, keepdims=True)
    xc = xf - mu
    var = jnp.mean(xc * xc, axis=-1, keepdims=True)
    y = xc * lax.rsqrt(var + LN_EPS) * g.astype(jnp.float32) + b.astype(jnp.float32)
    return y.astype(x.dtype)


def t5_causal_bucket(dist):
    n = jnp.maximum(dist, 0)
    max_exact = N_BUCKETS // 2
    nf = jnp.maximum(n, 1).astype(jnp.float32)
    large = max_exact + (jnp.log(nf / max_exact) / math.log(MAX_DISTANCE / max_exact)
                         * (N_BUCKETS - max_exact)).astype(jnp.int32)
    large = jnp.minimum(large, N_BUCKETS - 1)
    return jnp.where(n < max_exact, n, large)


def sliding_window_attention(q, k, v, sinks, rel_bias):
    B, S = q.shape[0], q.shape[1]
    nb = S // ATTN_BLOCK
    qb = q.reshape(B, nb, ATTN_BLOCK, N_KV_HEADS, GQA_GROUP, ATTN_HEAD_DIM)
    kb = k.reshape(B, nb, ATTN_BLOCK, N_KV_HEADS, ATTN_HEAD_DIM)
    vb = v.reshape(B, nb, ATTN_BLOCK, N_KV_HEADS, ATTN_HEAD_DIM)
    pad = ((0, 0), (1, 0), (0, 0), (0, 0), (0, 0))
    kc = jnp.concatenate([jnp.pad(kb, pad)[:, :-1], kb], axis=2)
    vc = jnp.concatenate([jnp.pad(vb, pad)[:, :-1], vb], axis=2)
    logits = jnp.einsum('bnqhgd,bnkhd->bnhgqk', qb, kc).astype(jnp.float32) * (ATTN_HEAD_DIM ** -0.5)
    qi = jnp.arange(ATTN_BLOCK, dtype=jnp.int32)[:, None]
    kj = jnp.arange(2 * ATTN_BLOCK, dtype=jnp.int32)[None, :]
    dist = qi + ATTN_BLOCK - kj
    band = (dist >= 0) & (dist < WINDOW)
    blk = jnp.arange(nb, dtype=jnp.int32)[:, None, None]
    valid = band[None] & ~((blk == 0) & (kj[None] < ATTN_BLOCK))
    bias = rel_bias.astype(jnp.float32)[t5_causal_bucket(dist)]
    bias = bias.transpose(2, 0, 1).reshape(N_KV_HEADS, GQA_GROUP, ATTN_BLOCK, 2 * ATTN_BLOCK)
    logits = jnp.where(valid[None, :, None, None], logits + bias, NEG_INF)
    sink = jnp.broadcast_to(sinks.astype(jnp.float32).reshape(1, 1, N_KV_HEADS, GQA_GROUP, 1, 1),
                            logits.shape[:-1] + (1,))
    probs = jax.nn.softmax(jnp.concatenate([logits, sink], axis=-1), axis=-1)[..., :-1]
    out = jnp.einsum('bnhgqk,bnkhd->bnqhgd', probs.astype(v.dtype), vc)
    return out.reshape(B, S, N_ATTN_HEADS * ATTN_HEAD_DIM)


def chunked_gated_delta_rule(q, k, v, g, beta):
    B, S, H, dk = q.shape
    dv = v.shape[-1]
    nc = S // CHUNK

    def chunkify(t):
        return t.reshape(B, nc, CHUNK, H, -1).transpose(0, 3, 1, 2, 4)

    qc, kc, vc = chunkify(q), chunkify(k), chunkify(v)
    gc = g.reshape(B, nc, CHUNK, H).transpose(0, 3, 1, 2)
    bc = beta.reshape(B, nc, CHUNK, H).transpose(0, 3, 1, 2)
    G = jnp.cumsum(gc, axis=-1)
    tril = jnp.tril(jnp.ones((CHUNK, CHUNK), dtype=bool))
    strict = jnp.tril(jnp.ones((CHUNK, CHUNK), dtype=bool), -1)
    decay_mat = jnp.exp(jnp.where(tril, G[..., :, None] - G[..., None, :], -jnp.inf))
    kbeta = kc * bc[..., None]
    A = jnp.where(strict, jnp.einsum('bhnid,bhnjd->bhnij', kbeta, kc) * decay_mat, 0.0)
    eye = jnp.eye(CHUNK, dtype=jnp.float32)
    rhs = jnp.concatenate([vc * bc[..., None], kbeta * jnp.exp(G)[..., None]], axis=-1)
    sol = lax.linalg.triangular_solve(A + eye, rhs, left_side=True, lower=True, unit_diagonal=True)
    u, w = sol[..., :dv], sol[..., dv:]
    attn_intra = jnp.einsum('bhnid,bhnjd->bhnij', qc, kc) * decay_mat
    q_dec = qc * jnp.exp(G)[..., None]
    k_dec = kc * jnp.exp(G[..., -1:] - G)[..., None]
    g_last = jnp.exp(G[..., -1])

    def step(state, inp):
        q_d, k_d, u_c, w_c, a_c, gl = inp
        v_new = u_c - jnp.einsum('bhcd,bhde->bhce', w_c, state)
        o = jnp.einsum('bhcd,bhde->bhce', q_d, state) + jnp.einsum('bhij,bhje->bhie', a_c, v_new)
        state = state * gl[..., None, None] + jnp.einsum('bhcd,bhce->bhde', k_d, v_new)
        return state, o

    xs = (jnp.moveaxis(q_dec, 2, 0), jnp.moveaxis(k_dec, 2, 0), jnp.moveaxis(u, 2, 0),
          jnp.moveaxis(w, 2, 0), jnp.moveaxis(attn_intra, 2, 0), jnp.moveaxis(g_last, 2, 0))
    s0 = jnp.zeros((B, H, dk, dv), dtype=jnp.float32)
    _, o = lax.scan(step, s0, xs)
    return o.transpose(1, 0, 3, 2, 4).reshape(B, S, H, dv)


def l2_normalise(t):
    return t * lax.rsqrt(jnp.sum(t * t, axis=-1, keepdims=True) + RMS_EPS)


def hybrid_mixer(h, w_in, conv_w, a_log, dt_bias, delta_norm_w, sinks, rel_bias, w_o):
    B, S, _ = h.shape
    proj = h @ w_in
    q_a, k_a, v_a, qkv_d, a_raw, b_raw, z = jnp.split(proj, SPLIT_POINTS, axis=-1)

    attn_out = sliding_window_attention(
        q_a.reshape(B, S, N_ATTN_HEADS, ATTN_HEAD_DIM),
        k_a.reshape(B, S, N_KV_HEADS, ATTN_HEAD_DIM),
        v_a.reshape(B, S, N_KV_HEADS, ATTN_HEAD_DIM), sinks, rel_bias)

    qkv_d = lax.conv_general_dilated(qkv_d, conv_w, window_strides=(1,), padding=[(CONV_WIDTH - 1, 0)],
                                     dimension_numbers=('NWC', 'WIO', 'NWC'),
                                     feature_group_count=COL_DELTA_QKV)
    qkv_d = jax.nn.silu(qkv_d).astype(jnp.float32)
    q_d, k_d, v_d = jnp.split(qkv_d, 3, axis=-1)
    q_d = l2_normalise(q_d.reshape(B, S, N_DELTA_HEADS, DELTA_HEAD_DIM)) * (DELTA_HEAD_DIM ** -0.5)
    k_d = l2_normalise(k_d.reshape(B, S, N_DELTA_HEADS, DELTA_HEAD_DIM))
    v_d = v_d.reshape(B, S, N_DELTA_HEADS, DELTA_HEAD_DIM)
    g = -jnp.exp(a_log.astype(jnp.float32)) * jax.nn.softplus(a_raw.astype(jnp.float32) + dt_bias.astype(jnp.float32))
    beta = jax.nn.sigmoid(b_raw.astype(jnp.float32))
    o_d = chunked_gated_delta_rule(q_d, k_d, v_d, g, beta)
    o_d = o_d * lax.rsqrt(jnp.mean(o_d * o_d, axis=-1, keepdims=True) + RMS_EPS) * delta_norm_w.astype(jnp.float32)
    o_d = o_d * jax.nn.silu(z.astype(jnp.float32)).reshape(B, S, N_DELTA_HEADS, DELTA_HEAD_DIM)
    delta_out = o_d.reshape(B, S, DELTA_WIDTH).astype(h.dtype)

    mix = jnp.concatenate([attn_out, delta_out], axis=-1)
    return mix @ w_o


def squared_relu_mlp(h, w_up, w_down):
    a = jax.nn.relu(h @ w_up)
    return (a * a) @ w_down


def _fwd_setup_inputs(seed: int = 0) -> dict:
    key = jax.random.key(seed)
    ks = jax.random.split(key, 16)
    f32 = jnp.float32
    x = jax.random.normal(ks[0], (BATCH, SEQ, D_MODEL), f32)
    w_in = jax.random.normal(ks[1], (DEPTH, D_MODEL, N_IN_COLS), f32) * D_MODEL ** -0.5
    conv_w = jax.random.normal(ks[2], (DEPTH, CONV_WIDTH, 1, COL_DELTA_QKV), f32) * CONV_WIDTH ** -0.5
    a_log = jnp.log(jax.random.uniform(ks[3], (DEPTH, N_DELTA_HEADS), f32, 1.0, 16.0))
    dt = jnp.exp(jax.random.uniform(ks[4], (DEPTH, N_DELTA_HEADS), f32, math.log(1e-3), math.log(1e-1)))
    dt_bias = dt + jnp.log(-jnp.expm1(-dt))
    delta_norm_w = 1.0 + 0.02 * jax.random.normal(ks[5], (DEPTH, DELTA_HEAD_DIM), f32)
    attn_sinks = 0.5 * jax.random.normal(ks[6], (DEPTH, N_ATTN_HEADS), f32)
    rel_bias = 0.5 * jax.random.normal(ks[7], (N_BUCKETS, N_ATTN_HEADS), f32)
    w_o = jax.random.normal(ks[8], (DEPTH, MIX_WIDTH, D_MODEL), f32) * (MIX_WIDTH ** -0.5 * DN_BETA)
    ln1_g = 1.0 + 0.02 * jax.random.normal(ks[9], (DEPTH, D_MODEL), f32)
    ln1_b = 0.02 * jax.random.normal(ks[10], (DEPTH, D_MODEL), f32)
    w_up = jax.random.normal(ks[11], (DEPTH, D_MODEL, D_FF), f32) * D_MODEL ** -0.5
    w_down = jax.random.normal(ks[12], (DEPTH, D_FF, D_MODEL), f32) * (D_FF ** -0.5 * DN_BETA)
    ln2_g = 1.0 + 0.02 * jax.random.normal(ks[13], (DEPTH, D_MODEL), f32)
    ln2_b = 0.02 * jax.random.normal(ks[14], (DEPTH, D_MODEL), f32)
    return {"x": x, "w_in": w_in, "conv_w": conv_w, "a_log": a_log, "dt_bias": dt_bias,
            "delta_norm_w": delta_norm_w, "attn_sinks": attn_sinks, "rel_bias": rel_bias,
            "w_o": w_o, "ln1_g": ln1_g, "ln1_b": ln1_b, "w_up": w_up, "w_down": w_down,
            "ln2_g": ln2_g, "ln2_b": ln2_b}


def _fwd_reference(x, w_in, conv_w, a_log, dt_bias, delta_norm_w, attn_sinks, rel_bias,
              w_o, ln1_g, ln1_b, w_up, w_down, ln2_g, ln2_b):
    for l in range(DEPTH):
        mixed = hybrid_mixer(x, w_in[l], conv_w[l], a_log[l], dt_bias[l], delta_norm_w[l],
                             attn_sinks[l], rel_bias, w_o[l])
        x = layer_norm(DN_ALPHA * x + mixed, ln1_g[l], ln1_b[l])
        x = layer_norm(DN_ALPHA * x + squared_relu_mlp(x, w_up[l], w_down[l]), ln2_g[l], ln2_b[l])
    return x


import jax as _jax
import jax.numpy as _jnp

TWIN_FORMAT = 'train_step'
FWD_PARAMS = ['x', 'w_in', 'conv_w', 'a_log', 'dt_bias', 'delta_norm_w', 'attn_sinks', 'rel_bias', 'w_o', 'ln1_g', 'ln1_b', 'w_up', 'w_down', 'ln2_g', 'ln2_b']
TWIN_WEIGHTS = ['w_in', 'conv_w', 'a_log', 'dt_bias', 'delta_norm_w', 'attn_sinks', 'rel_bias', 'w_o', 'ln1_g', 'ln1_b', 'w_up', 'w_down', 'ln2_g', 'ln2_b']
TWIN_DIFF_INPUT = 'x'
TWIN_INPUTS = ['x', 'w_in', 'conv_w', 'a_log', 'dt_bias', 'delta_norm_w', 'attn_sinks', 'rel_bias', 'w_o', 'ln1_g', 'ln1_b', 'w_up', 'w_down', 'ln2_g', 'ln2_b', 'loss_target', 'm_w_in', 'm_conv_w', 'm_a_log', 'm_dt_bias', 'm_delta_norm_w', 'm_attn_sinks', 'm_rel_bias', 'm_w_o', 'm_ln1_g', 'm_ln1_b', 'm_w_up', 'm_w_down', 'm_ln2_g', 'm_ln2_b', 'v_w_in', 'v_conv_w', 'v_a_log', 'v_dt_bias', 'v_delta_norm_w', 'v_attn_sinks', 'v_rel_bias', 'v_w_o', 'v_ln1_g', 'v_ln1_b', 'v_w_up', 'v_w_down', 'v_ln2_g', 'v_ln2_b']
TWIN_OUTPUTS = ['loss', 'grad_x', 'grad_w_in', 'grad_conv_w', 'grad_a_log', 'grad_dt_bias', 'grad_delta_norm_w', 'grad_attn_sinks', 'grad_rel_bias', 'grad_w_o', 'grad_ln1_g', 'grad_ln1_b', 'grad_w_up', 'grad_w_down', 'grad_ln2_g', 'grad_ln2_b', 'delta_w_in', 'delta_conv_w', 'delta_a_log', 'delta_dt_bias', 'delta_delta_norm_w', 'delta_attn_sinks', 'delta_rel_bias', 'delta_w_o', 'delta_ln1_g', 'delta_ln1_b', 'delta_w_up', 'delta_w_down', 'delta_ln2_g', 'delta_ln2_b', 'new_m_w_in', 'new_m_conv_w', 'new_m_a_log', 'new_m_dt_bias', 'new_m_delta_norm_w', 'new_m_attn_sinks', 'new_m_rel_bias', 'new_m_w_o', 'new_m_ln1_g', 'new_m_ln1_b', 'new_m_w_up', 'new_m_w_down', 'new_m_ln2_g', 'new_m_ln2_b', 'new_v_w_in', 'new_v_conv_w', 'new_v_a_log', 'new_v_dt_bias', 'new_v_delta_norm_w', 'new_v_attn_sinks', 'new_v_rel_bias', 'new_v_w_o', 'new_v_ln1_g', 'new_v_ln1_b', 'new_v_w_up', 'new_v_w_down', 'new_v_ln2_g', 'new_v_ln2_b']
TWIN_LEAF_KINDS = {'loss': 'loss', 'grad_x': 'grad_x', 'grad_w_in': 'grad_w', 'grad_conv_w': 'grad_w', 'grad_a_log': 'grad_w', 'grad_dt_bias': 'grad_w', 'grad_delta_norm_w': 'grad_w', 'grad_attn_sinks': 'grad_w', 'grad_rel_bias': 'grad_w', 'grad_w_o': 'grad_w', 'grad_ln1_g': 'grad_w', 'grad_ln1_b': 'grad_w', 'grad_w_up': 'grad_w', 'grad_w_down': 'grad_w', 'grad_ln2_g': 'grad_w', 'grad_ln2_b': 'grad_w', 'delta_w_in': 'delta_w', 'delta_conv_w': 'delta_w', 'delta_a_log': 'delta_w', 'delta_dt_bias': 'delta_w', 'delta_delta_norm_w': 'delta_w', 'delta_attn_sinks': 'delta_w', 'delta_rel_bias': 'delta_w', 'delta_w_o': 'delta_w', 'delta_ln1_g': 'delta_w', 'delta_ln1_b': 'delta_w', 'delta_w_up': 'delta_w', 'delta_w_down': 'delta_w', 'delta_ln2_g': 'delta_w', 'delta_ln2_b': 'delta_w', 'new_m_w_in': 'new_m', 'new_m_conv_w': 'new_m', 'new_m_a_log': 'new_m', 'new_m_dt_bias': 'new_m', 'new_m_delta_norm_w': 'new_m', 'new_m_attn_sinks': 'new_m', 'new_m_rel_bias': 'new_m', 'new_m_w_o': 'new_m', 'new_m_ln1_g': 'new_m', 'new_m_ln1_b': 'new_m', 'new_m_w_up': 'new_m', 'new_m_w_down': 'new_m', 'new_m_ln2_g': 'new_m', 'new_m_ln2_b': 'new_m', 'new_v_w_in': 'new_v', 'new_v_conv_w': 'new_v', 'new_v_a_log': 'new_v', 'new_v_dt_bias': 'new_v', 'new_v_delta_norm_w': 'new_v', 'new_v_attn_sinks': 'new_v', 'new_v_rel_bias': 'new_v', 'new_v_w_o': 'new_v', 'new_v_ln1_g': 'new_v', 'new_v_ln1_b': 'new_v', 'new_v_w_up': 'new_v', 'new_v_w_down': 'new_v', 'new_v_ln2_g': 'new_v', 'new_v_ln2_b': 'new_v'}


def _forward(args):
    return _fwd_reference(*[args[k] for k in FWD_PARAMS])


def _output_shape():
    out = _jax.eval_shape(lambda: _forward(_fwd_setup_inputs(0)))
    return out.shape, out.dtype

N_MICROBATCH = 1
ADAM_LR = 0.001
ADAM_B1 = 0.9
ADAM_B2 = 0.999
ADAM_EPS = 1e-08
ADAM_WD = 0.01
ADAM_STEP = 10
PER_EXAMPLE_BATCH_AXIS = {'x': 0, 'loss_target': 0}
SHARED_INPUTS = []
_WEIGHT_DTYPES = {'w_in': _jnp.float32, 'conv_w': _jnp.float32, 'a_log': _jnp.float32, 'dt_bias': _jnp.float32, 'delta_norm_w': _jnp.float32, 'attn_sinks': _jnp.float32, 'rel_bias': _jnp.float32, 'w_o': _jnp.float32, 'ln1_g': _jnp.float32, 'ln1_b': _jnp.float32, 'w_up': _jnp.float32, 'w_down': _jnp.float32, 'ln2_g': _jnp.float32, 'ln2_b': _jnp.float32}
MOMENT_SCALE = {'w_in': 1.427188e-02, 'conv_w': 1.542499e-02, 'a_log': 7.406751e-02, 'dt_bias': 7.227197e-02, 'delta_norm_w': 5.277221e-02, 'attn_sinks': 6.001929e-03, 'rel_bias': 7.754851e-03, 'w_o': 2.393826e-02, 'ln1_g': 2.068632e-01, 'ln1_b': 1.482209e-01, 'w_up': 1.892035e-02, 'w_down': 7.110121e-02, 'ln2_g': 8.025760e+00, 'ln2_b': 1.736908e+00}


def _to_microbatches(a, axis):
    t = _jnp.moveaxis(a, axis, 0)
    t = t.reshape((N_MICROBATCH, t.shape[0] // N_MICROBATCH) + t.shape[1:])
    return _jnp.moveaxis(t, 1, axis + 1)


def setup_inputs(seed: int = 0) -> dict:
    inp = _fwd_setup_inputs(seed)
    key = _jax.random.fold_in(_jax.random.key(seed), 7919)
    shape, _ = _output_shape()
    out = dict(inp)
    out["loss_target"] = _jax.random.normal(_jax.random.fold_in(key, 0), shape, _jnp.float32)
    for i, name in enumerate(TWIN_WEIGHTS):
        w = inp[name].astype(_jnp.float32)
        if MOMENT_SCALE is None:
            s = _jnp.sqrt(_jnp.mean(_jnp.square(w)) + 1e-30)
        else:
            s = MOMENT_SCALE[name]
        km, kv = _jax.random.split(_jax.random.fold_in(key, i + 1))
        out[name] = w
        out["m_" + name] = s * _jax.random.normal(km, w.shape, _jnp.float32)
        out["v_" + name] = (s * s) * _jax.random.uniform(kv, w.shape, _jnp.float32, 0.5, 1.5)
    if N_MICROBATCH > 1:
        for name, axis in PER_EXAMPLE_BATCH_AXIS.items():
            out[name] = _to_microbatches(out[name], axis)
    return {'x': out['x'], 'w_in': out['w_in'], 'conv_w': out['conv_w'], 'a_log': out['a_log'], 'dt_bias': out['dt_bias'], 'delta_norm_w': out['delta_norm_w'], 'attn_sinks': out['attn_sinks'], 'rel_bias': out['rel_bias'], 'w_o': out['w_o'], 'ln1_g': out['ln1_g'], 'ln1_b': out['ln1_b'], 'w_up': out['w_up'], 'w_down': out['w_down'], 'ln2_g': out['ln2_g'], 'ln2_b': out['ln2_b'], 'loss_target': out['loss_target'], 'm_w_in': out['m_w_in'], 'm_conv_w': out['m_conv_w'], 'm_a_log': out['m_a_log'], 'm_dt_bias': out['m_dt_bias'], 'm_delta_norm_w': out['m_delta_norm_w'], 'm_attn_sinks': out['m_attn_sinks'], 'm_rel_bias': out['m_rel_bias'], 'm_w_o': out['m_w_o'], 'm_ln1_g': out['m_ln1_g'], 'm_ln1_b': out['m_ln1_b'], 'm_w_up': out['m_w_up'], 'm_w_down': out['m_w_down'], 'm_ln2_g': out['m_ln2_g'], 'm_ln2_b': out['m_ln2_b'], 'v_w_in': out['v_w_in'], 'v_conv_w': out['v_conv_w'], 'v_a_log': out['v_a_log'], 'v_dt_bias': out['v_dt_bias'], 'v_delta_norm_w': out['v_delta_norm_w'], 'v_attn_sinks': out['v_attn_sinks'], 'v_rel_bias': out['v_rel_bias'], 'v_w_o': out['v_w_o'], 'v_ln1_g': out['v_ln1_g'], 'v_ln1_b': out['v_ln1_b'], 'v_w_up': out['v_w_up'], 'v_w_down': out['v_w_down'], 'v_ln2_g': out['v_ln2_g'], 'v_ln2_b': out['v_ln2_b']}


def _loss(weights, diff, rest, loss_target):
    with _jax.named_scope("forward"):
        args = {**rest, TWIN_DIFF_INPUT: diff, **{k: w.astype(_WEIGHT_DTYPES[k]) for k, w in weights.items()}}
        y = _forward(args)
    with _jax.named_scope("loss_head"):
        err = _jnp.square(y.astype(_jnp.float32) - loss_target)
        return 0.5 * _jnp.sum(_jnp.mean(err, axis=-1)) if err.ndim else 0.5 * err


def _adamw(w, g, m, v):
    m = ADAM_B1 * m + (1.0 - ADAM_B1) * g
    v = ADAM_B2 * v + (1.0 - ADAM_B2) * _jnp.square(g)
    m_hat = m / (1.0 - ADAM_B1 ** ADAM_STEP)
    v_hat = v / (1.0 - ADAM_B2 ** ADAM_STEP)
    delta = -ADAM_LR * (m_hat / (_jnp.sqrt(v_hat) + ADAM_EPS) + ADAM_WD * w)
    return delta, m, v


def reference(x, w_in, conv_w, a_log, dt_bias, delta_norm_w, attn_sinks, rel_bias, w_o, ln1_g, ln1_b, w_up, w_down, ln2_g, ln2_b, loss_target, m_w_in, m_conv_w, m_a_log, m_dt_bias, m_delta_norm_w, m_attn_sinks, m_rel_bias, m_w_o, m_ln1_g, m_ln1_b, m_w_up, m_w_down, m_ln2_g, m_ln2_b, v_w_in, v_conv_w, v_a_log, v_dt_bias, v_delta_norm_w, v_attn_sinks, v_rel_bias, v_w_o, v_ln1_g, v_ln1_b, v_w_up, v_w_down, v_ln2_g, v_ln2_b):
    given = dict(x=x, w_in=w_in, conv_w=conv_w, a_log=a_log, dt_bias=dt_bias, delta_norm_w=delta_norm_w, attn_sinks=attn_sinks, rel_bias=rel_bias, w_o=w_o, ln1_g=ln1_g, ln1_b=ln1_b, w_up=w_up, w_down=w_down, ln2_g=ln2_g, ln2_b=ln2_b, loss_target=loss_target, m_w_in=m_w_in, m_conv_w=m_conv_w, m_a_log=m_a_log, m_dt_bias=m_dt_bias, m_delta_norm_w=m_delta_norm_w, m_attn_sinks=m_attn_sinks, m_rel_bias=m_rel_bias, m_w_o=m_w_o, m_ln1_g=m_ln1_g, m_ln1_b=m_ln1_b, m_w_up=m_w_up, m_w_down=m_w_down, m_ln2_g=m_ln2_g, m_ln2_b=m_ln2_b, v_w_in=v_w_in, v_conv_w=v_conv_w, v_a_log=v_a_log, v_dt_bias=v_dt_bias, v_delta_norm_w=v_delta_norm_w, v_attn_sinks=v_attn_sinks, v_rel_bias=v_rel_bias, v_w_o=v_w_o, v_ln1_g=v_ln1_g, v_ln1_b=v_ln1_b, v_w_up=v_w_up, v_w_down=v_w_down, v_ln2_g=v_ln2_g, v_ln2_b=v_ln2_b)
    weights = {n: given[n] for n in TWIN_WEIGHTS}
    shared = {n: given[n] for n in SHARED_INPUTS}
    per_example = {n: given[n] for n in ['x']}
    grad_fn = _jax.value_and_grad(_loss, argnums=(0, 1))

    def one_microbatch(ex, loss_target):
        ex = dict(ex)
        diff = ex.pop(TWIN_DIFF_INPUT)
        return grad_fn(weights, diff, {**shared, **ex}, loss_target)

    if N_MICROBATCH == 1:
        loss, (grad_w, grad_x) = one_microbatch(per_example, given["loss_target"])
    else:
        def body(carry, xs):
            loss_sum, grad_sum = carry
            l_k, (gw_k, gx_k) = one_microbatch(xs[0], xs[1])
            with _jax.named_scope("update"):
                return (loss_sum + l_k, _jax.tree.map(_jnp.add, grad_sum, gw_k)), gx_k

        init = (_jnp.zeros((), _jnp.float32), _jax.tree.map(_jnp.zeros_like, weights))
        (loss, grad_w), grad_x = _jax.lax.scan(body, init, (per_example, given["loss_target"]))
    with _jax.named_scope("update"):
        delta_w, new_m, new_v = {}, {}, {}
        for n in TWIN_WEIGHTS:
            delta_w[n], new_m[n], new_v[n] = _adamw(weights[n], grad_w[n], given["m_" + n], given["v_" + n])
    return (loss, grad_x, *[grad_w[n] for n in TWIN_WEIGHTS], *[delta_w[n] for n in TWIN_WEIGHTS],
            *[new_m[n] for n in TWIN_WEIGHTS], *[new_v[n] for n in TWIN_WEIGHTS])
```

```python
import functools
import math

import numpy as np
import jax
import jax.numpy as jnp
from jax import lax
from jax.experimental import pallas as pl
from jax.experimental.pallas import tpu as pltpu

F32 = jnp.float32
BF16 = jnp.bfloat16
HIGHEST = lax.Precision.HIGHEST

N_DEV = 8
HEAD_A = 64
GQA = 4
BLK = 128
N_BUCKETS = 32
MAX_DISTANCE = 128
HEAD_D = 128
CONV_W = 4
CHUNK = 64
NEG_INF = -1e30
LN_EPS = 1e-5
RMS_EPS = 1e-6
DN_ALPHA = 2.0 ** 0.25
ADAM_LR, ADAM_B1, ADAM_B2, ADAM_EPS, ADAM_WD, ADAM_STEP = 0.001, 0.9, 0.999, 1e-08, 0.01, 10

LANE = 128
VMEM_LIMIT = 56 * 1024 * 1024

NN = ((1,), (0,))
NT = ((1,), (1,))
TN = ((0,), (0,))


def _dot(a, b, dims, prec=None):
    return lax.dot_general(a, b, (dims, ((), ())), precision=prec, preferred_element_type=F32)


def _tile(dim, pref):
    if dim <= pref:
        return dim
    t = (pref // LANE) * LANE
    while t > LANE and dim % t:
        t -= LANE
    assert dim % t == 0, (dim, pref)
    return t


def _params(sem):
    return pltpu.CompilerParams(dimension_semantics=sem, vmem_limit_bytes=VMEM_LIMIT)


def _matmul(a, b, dims, *, name, out_dtype=F32, tm=1024, tn=1024, tk=512, a_fn=None, epi=None, epi_in=(),
            b_groups=None, out_groups=None):
    (ca,), (cb,) = dims
    M, K = a.shape[1 - ca], a.shape[ca]
    if b_groups:
        G, R, C = b.shape
        bshape = (R, G * C)
    else:
        bshape = b.shape
    N = bshape[1 - cb]
    assert bshape[cb] == K, (a.shape, b.shape, dims)
    tm, tk = _tile(M, tm), _tile(K, tk)
    if b_groups:
        lim = C if cb == 0 else tn
        tn = _tile(N, min(tn, lim))
        if cb == 1:
            tk = _tile(K, min(tk, C))
    else:
        tn = _tile(N, tn)
    if out_groups:
        tn = _tile(N, min(tn, N // out_groups))
    nk = K // tk

    def body(*refs):
        a_ref, b_ref = refs[0], refs[1]
        e_refs = refs[2:2 + len(epi_in)]
        o_ref, acc_ref = refs[2 + len(epi_in)], refs[3 + len(epi_in)]
        k = pl.program_id(2)

        @pl.when(k == 0)
        def _():
            acc_ref[...] = jnp.zeros_like(acc_ref)

        av = a_ref[...]
        if a_fn is not None:
            av = a_fn(av)
        acc_ref[...] += _dot(av.astype(BF16), b_ref[...].astype(BF16), dims)

        @pl.when(k == nk - 1)
        def _():
            r = acc_ref[...]
            if epi is not None:
                r = epi(r, *[e[...] for e in e_refs])
            o_ref[...] = r.astype(out_dtype)

    a_spec = (pl.BlockSpec((tm, tk), lambda i, j, k: (i, k)) if ca == 1
              else pl.BlockSpec((tk, tm), lambda i, j, k: (k, i)))
    if b_groups:
        if cb == 0:
            per = C // tn
            b_spec = pl.BlockSpec((None, tk, tn), lambda i, j, k: (j // per, k, j % per))
        else:
            per = C // tk
            b_spec = pl.BlockSpec((None, tn, tk), lambda i, j, k: (k // per, j, k % per))
    else:
        b_spec = (pl.BlockSpec((tk, tn), lambda i, j, k: (k, j)) if cb == 0
                  else pl.BlockSpec((tn, tk), lambda i, j, k: (j, k)))
    e_specs = [pl.BlockSpec((tm, tn), lambda i, j, k: (i, j)) for _ in epi_in]
    if out_groups:
        per_o = (N // out_groups) // tn
        o_spec = pl.BlockSpec((None, tm, tn), lambda i, j, k: (j // per_o, i, j % per_o))
        o_shape = jax.ShapeDtypeStruct((out_groups, M, N // out_groups), out_dtype)
    else:
        o_spec = pl.BlockSpec((tm, tn), lambda i, j, k: (i, j))
        o_shape = jax.ShapeDtypeStruct((M, N), out_dtype)
    return pl.pallas_call(
        body, grid=(M // tm, N // tn, nk), in_specs=[a_spec, b_spec] + e_specs, out_specs=o_spec,
        out_shape=o_shape, scratch_shapes=[pltpu.VMEM((tm, tn), F32)],
        compiler_params=_params(("parallel", "parallel", "arbitrary")), name=name)(a, b, *epi_in)


def _relu_sq(u):
    r = jnp.maximum(u, 0.0)
    return r * r


def _relu_sq_grad(acc, u):
    return acc * (2.0 * jnp.maximum(u, 0.0))


def _ln_stats(r):
    mu = jnp.mean(r, axis=-1, keepdims=True)
    xc = r - mu
    var = jnp.mean(xc * xc, axis=-1, keepdims=True)
    rstd = lax.rsqrt(var + LN_EPS)
    return xc * rstd, rstd


def _ln_bwd(dy, xhat, rstd, g):
    dxh = dy * g
    m1 = jnp.mean(dxh, axis=-1, keepdims=True)
    m2 = jnp.mean(dxh * xhat, axis=-1, keepdims=True)
    return rstd * (dxh - m1 - xhat * m2)


def _row_call(body, ins, row_ins, outs, acc_outs, name, tr=256):
    S = ins[0].shape[0]
    tr = min(tr, S)
    n_in, n_row, n_out = len(ins), len(row_ins), len(outs)

    def wrapped(*refs):
        i = pl.program_id(0)
        acc_refs = refs[n_in + n_row + n_out:]

        @pl.when(i == 0)
        def _():
            for r in acc_refs:
                r[...] = jnp.zeros_like(r)

        body(*refs)

    in_specs = [pl.BlockSpec((tr, a.shape[1]), lambda i: (i, 0)) for a in ins]
    in_specs += [pl.BlockSpec(a.shape, lambda i: (0, 0)) for a in row_ins]
    out_specs = [pl.BlockSpec((tr, s.shape[1]), lambda i: (i, 0)) for s in outs]
    out_specs += [pl.BlockSpec(s.shape, lambda i: (0, 0)) for s in acc_outs]
    return pl.pallas_call(wrapped, grid=(S // tr,), in_specs=in_specs, out_specs=out_specs,
                          out_shape=list(outs) + list(acc_outs),
                          compiler_params=_params(("arbitrary",)), name=name)(*ins, *row_ins)


def _ln1_fwd(x, mixed, g, b):
    def body(x_ref, m_ref, g_ref, b_ref, h_ref):
        xhat, _ = _ln_stats(DN_ALPHA * x_ref[...] + m_ref[...])
        h_ref[...] = xhat * g_ref[...] + b_ref[...]
    return _row_call(body, [x, mixed], [g, b], [jax.ShapeDtypeStruct(x.shape, F32)], [], "ln1_fwd")[0]


def _ln2_loss(h1, mlp, g, b, target):
    S, D = h1.shape
    sds = jax.ShapeDtypeStruct

    def body(h_ref, m_ref, t_ref, g_ref, b_ref, dr_ref, loss_ref, dg_ref, db_ref):
        xhat, rstd = _ln_stats(DN_ALPHA * h_ref[...] + m_ref[...])
        gv = g_ref[...]
        err = xhat * gv + b_ref[...] - t_ref[...]
        loss_ref[...] += jnp.sum(jnp.sum(err * err, axis=0, keepdims=True), axis=1, keepdims=True) * (0.5 / D)
        dy = err * (1.0 / D)
        dg_ref[...] += jnp.sum(dy * xhat, axis=0, keepdims=True)
        db_ref[...] += jnp.sum(dy, axis=0, keepdims=True)
        dr_ref[...] = _ln_bwd(dy, xhat, rstd, gv)

    return _row_call(body, [h1, mlp, target], [g, b], [sds((S, D), F32)],
                     [sds((1, LANE), F32), sds((1, D), F32), sds((1, D), F32)], "ln2_loss")


def _ln1_bwd(x, mixed, g, dr2, dh_mlp):
    S, D = x.shape
    sds = jax.ShapeDtypeStruct

    def body(x_ref, m_ref, dr2_ref, dh_ref, g_ref, dr_ref, dg_ref, db_ref):
        xhat, rstd = _ln_stats(DN_ALPHA * x_ref[...] + m_ref[...])
        dy = DN_ALPHA * dr2_ref[...] + dh_ref[...]
        dg_ref[...] += jnp.sum(dy * xhat, axis=0, keepdims=True)
        db_ref[...] += jnp.sum(dy, axis=0, keepdims=True)
        dr_ref[...] = _ln_bwd(dy, xhat, rstd, g_ref[...])

    return _row_call(body, [x, mixed, dr2, dh_mlp], [g], [sds((S, D), F32)],
                     [sds((1, D), F32), sds((1, D), F32)], "ln1_bwd")


def _grad_x(dr1, dx_proj):
    def body(a_ref, b_ref, o_ref):
        o_ref[...] = DN_ALPHA * a_ref[...] + b_ref[...]
    return _row_call(body, [dr1, dx_proj], [], [jax.ShapeDtypeStruct(dr1.shape, F32)], [], "grad_x")[0]


def _bucket_table():
    qi = np.arange(BLK, dtype=np.int32)[:, None]
    kj = np.arange(2 * BLK, dtype=np.int32)[None, :]
    dist = qi + BLK - kj
    n = np.maximum(dist, 0)
    max_exact = N_BUCKETS // 2
    nf = np.maximum(n, 1).astype(np.float32)
    large = max_exact + (np.log(nf / np.float32(max_exact)) / np.float32(math.log(MAX_DISTANCE / max_exact))
                         * np.float32(N_BUCKETS - max_exact)).astype(np.int32)
    large = np.minimum(large, N_BUCKETS - 1)
    bucket = np.where(n < max_exact, n, large)
    return np.where((dist >= 0) & (dist < BLK), bucket, -1).astype(np.int32)


def _attn_bias(rel_bias_t):
    hq = rel_bias_t.shape[0]
    bucket = jnp.asarray(_bucket_table())

    def body(rb_ref, bk_ref, o_ref):
        h = pl.program_id(0)
        bk = bk_ref[...]
        acc = jnp.zeros((BLK, 2 * BLK), F32)
        for b in range(N_BUCKETS):
            acc = jnp.where(bk == b, rb_ref[h, b], acc)
        o_ref[...] = acc

    return pl.pallas_call(
        body, grid=(hq,),
        in_specs=[pl.BlockSpec(memory_space=pltpu.SMEM), pl.BlockSpec((BLK, 2 * BLK), lambda h: (0, 0))],
        out_specs=pl.BlockSpec((BLK, 2 * BLK), lambda h: (h, 0)),
        out_shape=jax.ShapeDtypeStruct((hq * BLK, 2 * BLK), F32),
        compiler_params=_params(("arbitrary",)), name="attn_bias")(rel_bias_t, bucket)


def _attn_probs(q, kc, kp, bias, sink, mask_c, mask_p):
    lc = jnp.where(mask_c, _dot(q, kc, NT) + bias[:, BLK:], NEG_INF)
    lp = jnp.where(mask_p, _dot(q, kp, NT) + bias[:, :BLK], NEG_INF)
    m = jnp.maximum(jnp.maximum(jnp.max(lc, axis=1, keepdims=True), jnp.max(lp, axis=1, keepdims=True)), sink)
    pc, pp, ps = jnp.exp(lc - m), jnp.exp(lp - m), jnp.exp(sink - m)
    inv = 1.0 / (jnp.sum(pc, axis=1, keepdims=True) + jnp.sum(pp, axis=1, keepdims=True) + ps)
    return pc, pp, ps, inv


def _attn_masks(n):
    qi = lax.broadcasted_iota(jnp.int32, (BLK, BLK), 0)
    kj = lax.broadcasted_iota(jnp.int32, (BLK, BLK), 1)
    return kj <= qi, (kj > qi) & (n > 0)


def _attn_fwd(proj, bias, sinks, hq, q_blk, k_blk, v_blk):
    S = proj.shape[0]
    hkv = hq // GQA
    wq, wk = hq * HEAD_A, hkv * HEAD_A

    def body(q_ref, k_ref, v_ref, bias_ref, sink_ref, o_ref):
        n = pl.program_id(0)
        cur = pl.multiple_of(n * BLK, BLK)
        prev = pl.multiple_of(jnp.maximum(n - 1, 0) * BLK, BLK)
        mask_c, mask_p = _attn_masks(n)
        for h4 in range(hkv):
            cs = slice(h4 * HEAD_A, (h4 + 1) * HEAD_A)
            kc, kp = k_ref[pl.ds(cur, BLK), cs].astype(BF16), k_ref[pl.ds(prev, BLK), cs].astype(BF16)
            vc, vp = v_ref[pl.ds(cur, BLK), cs].astype(BF16), v_ref[pl.ds(prev, BLK), cs].astype(BF16)
            for g in range(GQA):
                h = h4 * GQA + g
                hs = slice(h * HEAD_A, (h + 1) * HEAD_A)
                q = (q_ref[:, hs] * (HEAD_A ** -0.5)).astype(BF16)
                pc, pp, _, inv = _attn_probs(q, kc, kp, bias_ref[h * BLK:(h + 1) * BLK, :], sink_ref[h], mask_c, mask_p)
                o_ref[:, hs] = (_dot(pc.astype(BF16), vc, NN) + _dot(pp.astype(BF16), vp, NN)) * inv

    return pl.pallas_call(
        body, grid=(S // BLK,),
        in_specs=[pl.BlockSpec((BLK, wq), lambda n: (n, q_blk)), pl.BlockSpec((S, wk), lambda n: (0, k_blk)),
                  pl.BlockSpec((S, wk), lambda n: (0, v_blk)), pl.BlockSpec((hq * BLK, 2 * BLK), lambda n: (0, 0)),
                  pl.BlockSpec(memory_space=pltpu.SMEM)],
        out_specs=pl.BlockSpec((BLK, wq), lambda n: (n, 0)),
        out_shape=jax.ShapeDtypeStruct((S, wq), F32),
        compiler_params=_params(("arbitrary",)), name="attn_fwd")(proj, proj, proj, bias, sinks)


def _attn_bwd(proj, bias, sinks, out, dmix, hq, q_blk, k_blk, v_blk):
    S = proj.shape[0]
    hkv = hq // GQA
    wq, wk = hq * HEAD_A, hkv * HEAD_A
    sds = jax.ShapeDtypeStruct

    def body(q_ref, k_ref, v_ref, bias_ref, sink_ref, o_ref, do_ref, dq_ref, dk_ref, dv_ref, dbias_ref, dsink_ref):
        n = pl.program_id(0)

        @pl.when(n == 0)
        def _():
            dk_ref[...] = jnp.zeros_like(dk_ref)
            dv_ref[...] = jnp.zeros_like(dv_ref)
            dbias_ref[...] = jnp.zeros_like(dbias_ref)
            dsink_ref[...] = jnp.zeros_like(dsink_ref)

        cur = pl.multiple_of(n * BLK, BLK)
        prev = pl.multiple_of(jnp.maximum(n - 1, 0) * BLK, BLK)
        mask_c, mask_p = _attn_masks(n)
        for h4 in range(hkv):
            cs = slice(h4 * HEAD_A, (h4 + 1) * HEAD_A)
            kc, kp = k_ref[pl.ds(cur, BLK), cs].astype(BF16), k_ref[pl.ds(prev, BLK), cs].astype(BF16)
            vc, vp = v_ref[pl.ds(cur, BLK), cs].astype(BF16), v_ref[pl.ds(prev, BLK), cs].astype(BF16)
            dkc = jnp.zeros((BLK, HEAD_A), F32)
            dkp = jnp.zeros((BLK, HEAD_A), F32)
            dvc = jnp.zeros((BLK, HEAD_A), F32)
            dvp = jnp.zeros((BLK, HEAD_A), F32)
            for g in range(GQA):
                h = h4 * GQA + g
                hs = slice(h * HEAD_A, (h + 1) * HEAD_A)
                rows = slice(h * BLK, (h + 1) * BLK)
                q = (q_ref[:, hs] * (HEAD_A ** -0.5)).astype(BF16)
                pc, pp, ps, inv = _attn_probs(q, kc, kp, bias_ref[rows, :], sink_ref[h], mask_c, mask_p)
                pc, pp, ps = pc * inv, pp * inv, ps * inv
                do = do_ref[:, hs]
                delta = jnp.sum(do * o_ref[:, hs], axis=1, keepdims=True)
                dob = do.astype(BF16)
                dsc = pc * (_dot(dob, vc, NT) - delta)
                dsp = pp * (_dot(dob, vp, NT) - delta)
                dsink_ref[h:h + 1, :] += jnp.broadcast_to(jnp.sum(-ps * delta, axis=0, keepdims=True), (1, LANE))
                dbias_ref[rows, BLK:] += dsc
                dbias_ref[rows, :BLK] += dsp
                dscb, dspb = dsc.astype(BF16), dsp.astype(BF16)
                dq_ref[:, hs] = (_dot(dscb, kc, NN) + _dot(dspb, kp, NN)) * (HEAD_A ** -0.5)
                dkc += _dot(dscb, q, TN)
                dkp += _dot(dspb, q, TN)
                dvc += _dot(pc.astype(BF16), dob, TN)
                dvp += _dot(pp.astype(BF16), dob, TN)
            dk_ref[pl.ds(cur, BLK), cs] += dkc
            dk_ref[pl.ds(prev, BLK), cs] += dkp
            dv_ref[pl.ds(cur, BLK), cs] += dvc
            dv_ref[pl.ds(prev, BLK), cs] += dvp

    return pl.pallas_call(
        body, grid=(S // BLK,),
        in_specs=[pl.BlockSpec((BLK, wq), lambda n: (n, q_blk)), pl.BlockSpec((S, wk), lambda n: (0, k_blk)),
                  pl.BlockSpec((S, wk), lambda n: (0, v_blk)), pl.BlockSpec((hq * BLK, 2 * BLK), lambda n: (0, 0)),
                  pl.BlockSpec(memory_space=pltpu.SMEM),
                  pl.BlockSpec((BLK, wq), lambda n: (n, 0)), pl.BlockSpec((BLK, wq), lambda n: (n, 0))],
        out_specs=[pl.BlockSpec((BLK, wq), lambda n: (n, 0)), pl.BlockSpec((S, wk), lambda n: (0, 0)),
                   pl.BlockSpec((S, wk), lambda n: (0, 0)), pl.BlockSpec((hq * BLK, 2 * BLK), lambda n: (0, 0)),
                   pl.BlockSpec((hq, LANE), lambda n: (0, 0))],
        out_shape=[sds((S, wq), F32), sds((S, wk), F32), sds((S, wk), F32), sds((hq * BLK, 2 * BLK), F32),
                   sds((hq, LANE), F32)],
        compiler_params=_params(("arbitrary",)), name="attn_bwd")(proj, proj, proj, bias, sinks, out, dmix)


def _rel_bias_grad(dbias, hq):
    bucket = jnp.asarray(_bucket_table())

    def body(d_ref, bk_ref, o_ref):
        d = d_ref[...]
        bk = bk_ref[...]
        rows = [jnp.sum(jnp.where(bk == b, d, 0.0), axis=0, keepdims=True) for b in range(N_BUCKETS)]
        tot = jnp.sum(jnp.concatenate(rows, axis=0), axis=1, keepdims=True)
        o_ref[...] = jnp.broadcast_to(tot, (N_BUCKETS, LANE))

    return pl.pallas_call(
        body, grid=(hq,),
        in_specs=[pl.BlockSpec((BLK, 2 * BLK), lambda h: (h, 0)), pl.BlockSpec((BLK, 2 * BLK), lambda h: (0, 0))],
        out_specs=pl.BlockSpec((None, N_BUCKETS, LANE), lambda h: (h, 0, 0)),
        out_shape=jax.ShapeDtypeStruct((hq, N_BUCKETS, LANE), F32),
        compiler_params=_params(("arbitrary",)), name="rel_bias_grad")(dbias, bucket)


def _sigmoid(x):
    return 1.0 / (1.0 + jnp.exp(-x))


def _shift_rows(x, s):
    n = x.shape[0]
    row = lax.broadcasted_iota(jnp.int32, x.shape, 0)
    if s > 0:
        return jnp.where(row >= s, pltpu.roll(x, s, 0), 0.0)
    return jnp.where(row < n + s, pltpu.roll(x, n + s, 0), 0.0)


def _conv_silu_norm(xv, w, j, nh):
    c = w[CONV_W - 1:CONV_W, :] * xv
    for s in range(1, CONV_W):
        c = c + w[CONV_W - 1 - s:CONV_W - s, :] * _shift_rows(xv, s)
    sg = _sigmoid(c)
    a = c * sg
    r = lax.rsqrt(jnp.sum(a * a, axis=1, keepdims=True) + RMS_EPS)
    scale = jnp.where(j < nh, HEAD_D ** -0.5, 1.0)
    is_norm = j < 2 * nh
    y = jnp.where(is_norm, a * (r * scale), a)
    return c, sg, a, r, scale, is_norm, y


def _gdn_prep_fwd(proj, conv_w, nh, blk0):
    S = proj.shape[0]

    def body(x_ref, w_ref, o_ref):
        j = pl.program_id(0)
        o_ref[...] = _conv_silu_norm(x_ref[...], w_ref[...], j, nh)[-1]

    return pl.pallas_call(
        body, grid=(3 * nh,),
        in_specs=[pl.BlockSpec((S, HEAD_D), lambda j: (0, blk0 + j)), pl.BlockSpec((CONV_W, HEAD_D), lambda j: (0, j))],
        out_specs=pl.BlockSpec((S, HEAD_D), lambda j: (0, j)),
        out_shape=jax.ShapeDtypeStruct((S, 3 * nh * HEAD_D), F32),
        compiler_params=_params(("parallel",)), name="gdn_prep_fwd")(proj, conv_w)


def _gdn_prep_bwd(proj, conv_w, dqkv, nh, blk0):
    S = proj.shape[0]
    sds = jax.ShapeDtypeStruct

    def body(x_ref, w_ref, dy_ref, dx_ref, dw_ref):
        j = pl.program_id(0)
        xv, w = x_ref[...], w_ref[...]
        c, sg, a, r, scale, is_norm, _ = _conv_silu_norm(xv, w, j, nh)
        dy = dy_ref[...]
        rs = r * scale
        da_n = rs * dy - a * (r * r * rs) * jnp.sum(dy * a, axis=1, keepdims=True)
        da = jnp.where(is_norm, da_n, dy)
        dc = da * (sg * (1.0 + c * (1.0 - sg)))
        dx = w[CONV_W - 1:CONV_W, :] * dc
        dws = [jnp.sum(dc * xv, axis=0, keepdims=True)]
        for s in range(1, CONV_W):
            dx = dx + w[CONV_W - 1 - s:CONV_W - s, :] * _shift_rows(dc, -s)
            dws.insert(0, jnp.sum(dc * _shift_rows(xv, s), axis=0, keepdims=True))
        dx_ref[...] = dx
        dw_ref[...] = jnp.concatenate(dws, axis=0)

    return pl.pallas_call(
        body, grid=(3 * nh,),
        in_specs=[pl.BlockSpec((S, HEAD_D), lambda j: (0, blk0 + j)), pl.BlockSpec((CONV_W, HEAD_D), lambda j: (0, j)),
                  pl.BlockSpec((S, HEAD_D), lambda j: (0, j))],
        out_specs=[pl.BlockSpec((S, HEAD_D), lambda j: (0, j)), pl.BlockSpec((CONV_W, HEAD_D), lambda j: (0, j))],
        out_shape=[sds((S, 3 * nh * HEAD_D), F32), sds((CONV_W, 3 * nh * HEAD_D), F32)],
        compiler_params=_params(("parallel",)), name="gdn_prep_bwd")(proj, conv_w, dqkv)


def _softplus(x):
    return jnp.maximum(x, 0.0) + jnp.log(1.0 + jnp.exp(-jnp.abs(x)))


def _gates_fwd(ab, al, dt, nh):
    S = ab.shape[0]

    def body(ab_ref, al_ref, dt_ref, o_ref):
        v = ab_ref[...]
        lane = lax.broadcasted_iota(jnp.int32, v.shape, 1)
        g = -jnp.exp(al_ref[...]) * _softplus(v + dt_ref[...])
        o_ref[...] = jnp.where(lane < nh, g, jnp.where(lane < 2 * nh, _sigmoid(v), 0.0))

    row = pl.BlockSpec((1, LANE), lambda i: (0, 0))
    full = pl.BlockSpec((S, LANE), lambda i: (0, 0))
    return pl.pallas_call(body, grid=(1,), in_specs=[full, row, row], out_specs=full,
                          out_shape=jax.ShapeDtypeStruct((S, LANE), F32),
                          compiler_params=_params(("arbitrary",)), name="gates_fwd")(ab, al, dt)


def _gates_bwd(ab, al, dt, dgb, nh):
    S = ab.shape[0]
    sds = jax.ShapeDtypeStruct

    def body(ab_ref, al_ref, dt_ref, d_ref, dab_ref, dal_ref, ddt_ref):
        v, d = ab_ref[...], d_ref[...]
        lane = lax.broadcasted_iota(jnp.int32, v.shape, 1)
        is_a = lane < nh
        z = v + dt_ref[...]
        dsp = jnp.where(is_a, d * (-jnp.exp(al_ref[...])), 0.0)
        dz = dsp * _sigmoid(z)
        beta = _sigmoid(v)
        dab_ref[...] = jnp.where(is_a, dz, jnp.where(lane < 2 * nh, d * beta * (1.0 - beta), 0.0))
        dal_ref[...] = jnp.sum(dsp * _softplus(z), axis=0, keepdims=True)
        ddt_ref[...] = jnp.sum(dz, axis=0, keepdims=True)

    row = pl.BlockSpec((1, LANE), lambda i: (0, 0))
    full = pl.BlockSpec((S, LANE), lambda i: (0, 0))
    return pl.pallas_call(body, grid=(1,), in_specs=[full, row, row, full], out_specs=[full, row, row],
                          out_shape=[sds((S, LANE), F32), sds((1, LANE), F32), sds((1, LANE), F32)],
                          compiler_params=_params(("arbitrary",)), name="gates_bwd")(ab, al, dt, dgb)


def _col_of(tile, h):
    lane = lax.broadcasted_iota(jnp.int32, tile.shape, 1)
    return jnp.sum(jnp.where(lane == h, tile, 0.0), axis=1, keepdims=True)


def _to_row(col, eye):
    return jnp.sum(jnp.where(eye, col, 0.0), axis=0, keepdims=True)


def _to_col(row, eye):
    return jnp.sum(jnp.where(eye, row, 0.0), axis=1, keepdims=True)


def _chunk_local(q, k, v, gcol, bcol):
    C = CHUNK
    row = lax.broadcasted_iota(jnp.int32, (C, C), 0)
    col = lax.broadcasted_iota(jnp.int32, (C, C), 1)
    tril, strict, eye = col <= row, col < row, col == row
    grow = _to_row(gcol, eye)
    G_row = jnp.sum(jnp.where(row <= col, gcol, 0.0), axis=0, keepdims=True)
    G_col = jnp.sum(jnp.where(tril, grow, 0.0), axis=1, keepdims=True)
    decay = jnp.exp(jnp.where(tril, G_col - G_row, NEG_INF))
    G_last = G_col[C - 1:C, :]
    eG = jnp.exp(G_col)
    eGr = jnp.exp(G_last - G_col)
    gl = jnp.exp(G_last)
    kb = k * bcol
    M = _dot(kb, k, NT, HIGHEST)
    A = jnp.where(strict, M * decay, 0.0)
    T = jnp.where(eye, 1.0, 0.0) - A
    P = A
    for _ in range(int(math.log2(C)) - 1):
        P = _dot(P, P, NN, HIGHEST)
        T = T + _dot(T, P, NN, HIGHEST)
    rhs_v = v * bcol
    rhs_k = kb * eG
    u = _dot(T, rhs_v, NN, HIGHEST)
    w = _dot(T, rhs_k, NN, HIGHEST)
    N = _dot(q, k, NT, HIGHEST)
    attn = N * decay
    return dict(tril=tril, strict=strict, eye=eye, row=row, col=col, decay=decay, eG=eG, eGr=eGr, gl=gl, kb=kb, M=M, A=A,
                T=T, rhs_k=rhs_k, u=u, w=w, N=N, attn=attn, q_dec=q * eG, k_dec=k * eGr)


def _gdn_fwd(qkv, gb, nh):
    S = qkv.shape[0]
    nc = S // CHUNK
    sds = jax.ShapeDtypeStruct

    def body(q_ref, k_ref, v_ref, gb_ref, o_ref, st_ref, s_ref):
        h = pl.program_id(0)
        s_ref[...] = jnp.zeros_like(s_ref)

        def step(c, carry):
            t0 = pl.multiple_of(c * CHUNK, CHUNK)
            rows = pl.ds(t0, CHUNK)
            q, k, v = q_ref[rows, :], k_ref[rows, :], v_ref[rows, :]
            gbt = gb_ref[rows, :]
            L = _chunk_local(q, k, v, _col_of(gbt, h), _col_of(gbt, nh + h))
            state = s_ref[...]
            st_ref[c] = state
            v_new = L["u"] - _dot(L["w"], state, NN, HIGHEST)
            o_ref[rows, :] = _dot(L["q_dec"], state, NN, HIGHEST) + _dot(L["attn"], v_new, NN, HIGHEST)
            s_ref[...] = state * L["gl"] + _dot(L["k_dec"], v_new, TN, HIGHEST)
            return carry

        lax.fori_loop(0, nc, step, 0)

    return pl.pallas_call(
        body, grid=(nh,),
        in_specs=[pl.BlockSpec((S, HEAD_D), lambda h: (0, h)), pl.BlockSpec((S, HEAD_D), lambda h: (0, nh + h)),
                  pl.BlockSpec((S, HEAD_D), lambda h: (0, 2 * nh + h)), pl.BlockSpec((S, LANE), lambda h: (0, 0))],
        out_specs=[pl.BlockSpec((S, HEAD_D), lambda h: (0, h)),
                   pl.BlockSpec((None, nc, HEAD_D, HEAD_D), lambda h: (h, 0, 0, 0))],
        out_shape=[sds((S, nh * HEAD_D), F32), sds((nh, nc, HEAD_D, HEAD_D), F32)],
        scratch_shapes=[pltpu.VMEM((HEAD_D, HEAD_D), F32)],
        compiler_params=_params(("parallel",)), name="gdn_fwd")(qkv, qkv, qkv, gb)


def _gdn_bwd(qkv, gb, states, do, nh):
    S = qkv.shape[0]
    nc = S // CHUNK
    sds = jax.ShapeDtypeStruct

    def body(q_ref, k_ref, v_ref, gb_ref, st_ref, do_ref, dq_ref, dk_ref, dv_ref, dg_ref, db_ref, ds_ref):
        h = pl.program_id(0)
        ds_ref[...] = jnp.zeros_like(ds_ref)

        def step(i, carry):
            c = nc - 1 - i
            t0 = pl.multiple_of(c * CHUNK, CHUNK)
            rows = pl.ds(t0, CHUNK)
            q, k, v = q_ref[rows, :], k_ref[rows, :], v_ref[rows, :]
            gbt = gb_ref[rows, :]
            bcol = _col_of(gbt, nh + h)
            L = _chunk_local(q, k, v, _col_of(gbt, h), bcol)
            tril, strict, eye = L["tril"], L["strict"], L["eye"]
            decay, eG, eGr, gl, kb, T = L["decay"], L["eG"], L["eGr"], L["gl"], L["kb"], L["T"]
            u, w, attn, q_dec, k_dec = L["u"], L["w"], L["attn"], L["q_dec"], L["k_dec"]
            state = st_ref[c]
            dS = ds_ref[...]
            dov = do_ref[rows, :]
            v_new = u - _dot(w, state, NN, HIGHEST)
            dv_new = _dot(attn, dov, TN, HIGHEST) + _dot(k_dec, dS, NN, HIGHEST)
            dattn = jnp.where(tril, _dot(dov, v_new, NT, HIGHEST), 0.0)
            dq_dec = _dot(dov, state, NT, HIGHEST)
            dk_dec = _dot(v_new, dS, NT, HIGHEST)
            dgl = jnp.sum(jnp.sum(state * dS, axis=0, keepdims=True), axis=1, keepdims=True)
            ds_ref[...] = dS * gl + _dot(q_dec, dov, TN, HIGHEST) - _dot(w, dv_new, TN, HIGHEST)
            dw = -_dot(dv_new, state, NT, HIGHEST)
            drv = _dot(T, dv_new, TN, HIGHEST)
            drk = _dot(T, dw, TN, HIGHEST)
            dA = jnp.where(strict, -(_dot(drv, u, NT, HIGHEST) + _dot(drk, w, NT, HIGHEST)), 0.0)
            dM = dA * decay
            dN = dattn * decay
            dkb = _dot(dM, k, NN, HIGHEST)
            dq = _dot(dN, k, NN, HIGHEST) + dq_dec * eG
            dk = (drk * (bcol * eG) + _dot(dM, kb, TN, HIGHEST) + dkb * bcol + _dot(dN, q, TN, HIGHEST) + dk_dec * eGr)
            dvv = drv * bcol
            dbeta = (jnp.sum(drv * v, axis=1, keepdims=True) + jnp.sum(drk * k, axis=1, keepdims=True) * eG
                     + jnp.sum(dkb * k, axis=1, keepdims=True))
            E = dA * L["A"] + dattn * attn
            kd = jnp.sum(dk_dec * k_dec, axis=1, keepdims=True)
            dG = (jnp.sum(dq_dec * q_dec, axis=1, keepdims=True) - kd + jnp.sum(drk * L["rhs_k"], axis=1, keepdims=True)
                  + jnp.sum(E, axis=1, keepdims=True) - _to_col(jnp.sum(E, axis=0, keepdims=True), eye))
            d_last = jnp.sum(kd, axis=0, keepdims=True) + dgl * gl
            dG = dG + jnp.where(L["row"][:, :1] == CHUNK - 1, d_last, 0.0)
            dgc = jnp.sum(jnp.where(L["col"] >= L["row"], _to_row(dG, eye), 0.0), axis=1, keepdims=True)
            dq_ref[rows, :] = dq
            dk_ref[rows, :] = dk
            dv_ref[rows, :] = dvv
            dg_ref[rows, :] = dgc
            db_ref[rows, :] = dbeta
            return carry

        lax.fori_loop(0, nc, step, 0)

    head_spec = [pl.BlockSpec((S, HEAD_D), lambda h: (0, h)), pl.BlockSpec((S, HEAD_D), lambda h: (0, nh + h)),
                 pl.BlockSpec((S, HEAD_D), lambda h: (0, 2 * nh + h))]
    outs = pl.pallas_call(
        body, grid=(nh,),
        in_specs=head_spec + [pl.BlockSpec((S, LANE), lambda h: (0, 0)),
                              pl.BlockSpec((None, nc, HEAD_D, HEAD_D), lambda h: (h, 0, 0, 0)),
                              pl.BlockSpec((S, HEAD_D), lambda h: (0, h))],
        out_specs=[pl.BlockSpec((S, HEAD_D), lambda h: (0, h))] * 3
        + [pl.BlockSpec((None, S, 1), lambda h: (h, 0, 0))] * 2,
        out_shape=[sds((S, nh * HEAD_D), F32)] * 3 + [sds((nh, S, 1), F32)] * 2,
        scratch_shapes=[pltpu.VMEM((HEAD_D, HEAD_D), F32)],
        compiler_params=_params(("parallel",)), name="gdn_bwd")(qkv, qkv, qkv, gb, states, do)
    return outs


def _gated_norm_fwd(o, proj, norm_w, nh, z_blk0):
    S = o.shape[0]

    def body(o_ref, z_ref, w_ref, y_ref):
        ov, z = o_ref[...], z_ref[...]
        r = lax.rsqrt(jnp.mean(ov * ov, axis=1, keepdims=True) + RMS_EPS)
        y_ref[...] = ov * r * w_ref[...] * (z * _sigmoid(z))

    return pl.pallas_call(
        body, grid=(nh,),
        in_specs=[pl.BlockSpec((S, HEAD_D), lambda h: (0, h)), pl.BlockSpec((S, HEAD_D), lambda h: (0, z_blk0 + h)),
                  pl.BlockSpec((1, HEAD_D), lambda h: (0, 0))],
        out_specs=pl.BlockSpec((S, HEAD_D), lambda h: (0, h)),
        out_shape=jax.ShapeDtypeStruct((S, nh * HEAD_D), F32),
        compiler_params=_params(("parallel",)), name="gated_norm_fwd")(o, proj, norm_w)


def _gated_norm_bwd(o, proj, norm_w, dmix, nh, z_blk0, d_blk0):
    S = o.shape[0]
    sds = jax.ShapeDtypeStruct

    def body(o_ref, z_ref, w_ref, dy_ref, do_ref, dz_ref, dw_ref):
        ov, z, w, dy = o_ref[...], z_ref[...], w_ref[...], dy_ref[...]
        r = lax.rsqrt(jnp.mean(ov * ov, axis=1, keepdims=True) + RMS_EPS)
        oh = ov * r
        sg = _sigmoid(z)
        dz_ref[...] = dy * (oh * w) * (sg * (1.0 + z * (1.0 - sg)))
        don = dy * (z * sg)
        @pl.when(pl.program_id(0) == 0)
        def _():
            dw_ref[...] = jnp.zeros_like(dw_ref)

        dw_ref[...] += jnp.sum(don * oh, axis=0, keepdims=True)
        doh = don * w
        do_ref[...] = r * (doh - oh * jnp.mean(doh * oh, axis=1, keepdims=True))

    return pl.pallas_call(
        body, grid=(nh,),
        in_specs=[pl.BlockSpec((S, HEAD_D), lambda h: (0, h)), pl.BlockSpec((S, HEAD_D), lambda h: (0, z_blk0 + h)),
                  pl.BlockSpec((1, HEAD_D), lambda h: (0, 0)), pl.BlockSpec((S, HEAD_D), lambda h: (0, d_blk0 + h))],
        out_specs=[pl.BlockSpec((S, HEAD_D), lambda h: (0, h)), pl.BlockSpec((S, HEAD_D), lambda h: (0, h)),
                   pl.BlockSpec((1, HEAD_D), lambda h: (0, 0))],
        out_shape=[sds((S, nh * HEAD_D), F32), sds((S, nh * HEAD_D), F32), sds((1, HEAD_D), F32)],
        compiler_params=_params(("arbitrary",)), name="gated_norm_bwd")(o, proj, norm_w, dmix)


def _adamw_math(w, g, m, v):
    m = ADAM_B1 * m + (1.0 - ADAM_B1) * g
    v = ADAM_B2 * v + (1.0 - ADAM_B2) * (g * g)
    m_hat = m / (1.0 - ADAM_B1 ** ADAM_STEP)
    v_hat = v / (1.0 - ADAM_B2 ** ADAM_STEP)
    delta = -ADAM_LR * (m_hat / (jnp.sqrt(v_hat) + ADAM_EPS) + ADAM_WD * w)
    return delta, m, v


def _adamw_big(terms, w, m, v, name, tr=256):
    R, C = w.shape
    tr = _tile(R, tr)
    sds = jax.ShapeDtypeStruct

    def body(t_ref, w_ref, m_ref, v_ref, g_ref, d_ref, nm_ref, nv_ref):
        g = ((t_ref[0].astype(F32) + t_ref[1].astype(F32)) + t_ref[2].astype(F32)) + t_ref[3].astype(F32)
        g_ref[...] = g
        d_ref[...], nm_ref[...], nv_ref[...] = _adamw_math(w_ref[...], g, m_ref[...], v_ref[...])

    spec = pl.BlockSpec((tr, C), lambda i: (i, 0))
    return pl.pallas_call(
        body, grid=(R // tr,), in_specs=[pl.BlockSpec((4, tr, C), lambda i: (0, i, 0)), spec, spec, spec],
        out_specs=[spec] * 4, out_shape=[sds((R, C), F32)] * 4,
        compiler_params=_params(("parallel",)), name=name)(terms, w, m, v)


def _adamw_small(ws, gs, ms, vs):
    n = len(ws)

    def body(*refs):
        for i in range(n):
            w, g, m, v = (refs[k * n + i][...] for k in range(4))
            d, nm, nv = _adamw_math(w, g, m, v)
            refs[4 * n + i][...] = d
            refs[5 * n + i][...] = nm
            refs[6 * n + i][...] = nv

    shapes = [jax.ShapeDtypeStruct(w.shape, F32) for w in ws]
    vm = pl.BlockSpec(memory_space=pltpu.VMEM)
    outs = pl.pallas_call(body, in_specs=[vm] * (4 * n), out_specs=[vm] * (3 * n), out_shape=shapes * 3,
                          name="adamw_small")(*ws, *gs, *ms, *vs)
    return outs[:n], outs[n:2 * n], outs[2 * n:]


MESH = pl.DeviceIdType.MESH
ANY = pl.BlockSpec(memory_space=pl.ANY)


def _place():
    x, y, c = lax.axis_index("x"), lax.axis_index("y"), lax.axis_index("c")
    return x, y, c, [(1 - x, y), (x, 1 - y), (1 - x, 1 - y)]


def _all_gather(shards):
    n = len(shards)

    def body(*refs):
        ins, outs = refs[:n], refs[n:2 * n]
        send_sems, recv_sems, local_sems = refs[2 * n:]
        x, y, c, chips = _place()
        me, sibling = (x, y, c), (x, y, 1 - c)

        def slot(px, py, pc):
            return 4 * px + 2 * py + pc

        def copy(w, k, block, to, src=None):
            dst = outs[w].at[slot(*block)]
            return pltpu.make_async_remote_copy(src_ref=dst if src is None else src, dst_ref=dst,
                                                send_sem=send_sems.at[w * 7 + k], recv_sem=recv_sems.at[w * 7 + k],
                                                device_id=to, device_id_type=MESH)

        mine = [pltpu.make_async_copy(ins[w], outs[w].at[slot(*me)], local_sems.at[w]) for w in range(n)]
        for cp in mine:
            cp.start()
        first = []
        for w in range(n):
            first.append(copy(w, 0, me, sibling, src=ins[w]))
            first += [copy(w, 1 + j, me, (*chip, c), src=ins[w]) for j, chip in enumerate(chips)]
        for cp in first:
            cp.start()
        passed = []
        for w in range(n):
            for j, chip in enumerate(chips):
                copy(w, 1 + j, (*chip, c), me).wait_recv()
                cp = copy(w, 4 + j, (*chip, c), sibling)
                cp.start()
                passed.append(cp)
        for w in range(n):
            copy(w, 0, sibling, me).wait_recv()
            for j, chip in enumerate(chips):
                copy(w, 4 + j, (*chip, 1 - c), me).wait_recv()
        for cp in first + passed:
            cp.wait_send()
        for cp in mine:
            cp.wait()

    return pl.pallas_call(
        body, in_specs=[ANY] * n, out_specs=[ANY] * n,
        out_shape=[jax.ShapeDtypeStruct((N_DEV,) + s.shape, s.dtype) for s in shards],
        scratch_shapes=[pltpu.SemaphoreType.DMA((7 * n,)), pltpu.SemaphoreType.DMA((7 * n,)),
                        pltpu.SemaphoreType.DMA((n,))],
        name="all_gather_weights")(*shards)


def _sibling_exchange(grads):
    n = len(grads)

    def body(*refs):
        ins, outs = refs[:n], refs[n:2 * n]
        send_sems, recv_sems = refs[2 * n:]
        x, y, c, _ = _place()
        sibling = (x, y, 1 - c)
        copies = []
        for w in range(n):
            for q in range(4):
                copies.append(pltpu.make_async_remote_copy(
                    src_ref=ins[w].at[2 * q + (1 - c)], dst_ref=outs[w].at[q], send_sem=send_sems.at[4 * w + q],
                    recv_sem=recv_sems.at[4 * w + q], device_id=sibling, device_id_type=MESH))
        for cp in copies:
            cp.start()
        for cp in copies:
            cp.wait()

    return pl.pallas_call(
        body, in_specs=[ANY] * n, out_specs=[ANY] * n,
        out_shape=[jax.ShapeDtypeStruct((4,) + g.shape[1:], g.dtype) for g in grads],
        scratch_shapes=[pltpu.SemaphoreType.DMA((4 * n,)), pltpu.SemaphoreType.DMA((4 * n,))],
        name="rs_sibling_exchange")(*grads)


def _chip_sum(grad, recv, core, name, tr=256):
    _, R, C = grad.shape
    tr = _tile(R, tr)

    def body(c_ref, g_ref, r_ref, o_ref):
        o_ref[...] = (g_ref[...].astype(F32) + r_ref[...].astype(F32)).astype(o_ref.dtype)

    grid_spec = pltpu.PrefetchScalarGridSpec(
        num_scalar_prefetch=1, grid=(4, R // tr),
        in_specs=[pl.BlockSpec((None, tr, C), lambda q, i, c_ref: (2 * q + c_ref[0], i, 0)),
                  pl.BlockSpec((None, tr, C), lambda q, i, c_ref: (q, i, 0))],
        out_specs=pl.BlockSpec((None, tr, C), lambda q, i, c_ref: (q, i, 0)))
    return pl.pallas_call(body, grid_spec=grid_spec, out_shape=jax.ShapeDtypeStruct((4, R, C), BF16),
                          compiler_params=_params(("parallel", "parallel")), name=name)(core, grad, recv)


def _chip_exchange(parts):
    n = len(parts)

    def body(*refs):
        ins, outs = refs[:n], refs[n:2 * n]
        send_sems, recv_sems, local_sems = refs[2 * n:]
        x, y, c, chips = _place()
        my_q = 2 * x + y
        mine = [pltpu.make_async_copy(ins[w].at[my_q], outs[w].at[my_q], local_sems.at[w]) for w in range(n)]
        for cp in mine:
            cp.start()
        copies = []
        for w in range(n):
            for j, (px, py) in enumerate(chips):
                copies.append(pltpu.make_async_remote_copy(
                    src_ref=ins[w].at[2 * px + py], dst_ref=outs[w].at[my_q], send_sem=send_sems.at[3 * w + j],
                    recv_sem=recv_sems.at[3 * w + j], device_id=(px, py, c), device_id_type=MESH))
        for cp in copies:
            cp.start()
        for cp in copies:
            cp.wait()
        for cp in mine:
            cp.wait()

    return pl.pallas_call(
        body, in_specs=[ANY] * n, out_specs=[ANY] * n,
        out_shape=[jax.ShapeDtypeStruct(p.shape, p.dtype) for p in parts],
        scratch_shapes=[pltpu.SemaphoreType.DMA((3 * n,)), pltpu.SemaphoreType.DMA((3 * n,)),
                        pltpu.SemaphoreType.DMA((n,))],
        name="rs_chip_exchange")(*parts)


def _all_reduce_small(buf):
    R = buf.shape[0]

    def body(x_ref, o_ref, g_ref, send_sems, recv_sems):
        x, y, c, chips = _place()
        me, sibling = (x, y, c), (x, y, 1 - c)

        def slot(px, py, pc):
            return 4 * px + 2 * py + pc

        def copy(k, block, to, src=None):
            dst = g_ref.at[slot(*block)]
            return pltpu.make_async_remote_copy(src_ref=dst if src is None else src, dst_ref=dst,
                                                send_sem=send_sems.at[k], recv_sem=recv_sems.at[k],
                                                device_id=to, device_id_type=MESH)

        first = [copy(0, me, sibling, src=x_ref)]
        first += [copy(1 + j, me, (*chip, c), src=x_ref) for j, chip in enumerate(chips)]
        for cp in first:
            cp.start()
        g_ref[slot(*me)] = x_ref[...]
        passed = [copy(4 + j, (*chip, c), sibling) for j, chip in enumerate(chips)]
        for j, chip in enumerate(chips):
            copy(1 + j, (*chip, c), me).wait_recv()
            passed[j].start()
        copy(0, sibling, me).wait_recv()
        for j, chip in enumerate(chips):
            copy(4 + j, (*chip, 1 - c), me).wait_recv()
        for cp in first + passed:
            cp.wait_send()
        acc = g_ref[0]
        for s in range(1, N_DEV):
            acc = acc + g_ref[s]
        o_ref[...] = acc

    vm = pl.BlockSpec(memory_space=pltpu.VMEM)
    return pl.pallas_call(
        body, in_specs=[vm], out_specs=vm, out_shape=jax.ShapeDtypeStruct((R, LANE), F32),
        scratch_shapes=[pltpu.VMEM((N_DEV, R, LANE), F32), pltpu.SemaphoreType.DMA((7,)), pltpu.SemaphoreType.DMA((7,))],
        name="all_reduce_small")(buf)


def _pad_cols(a, width):
    return jnp.pad(a, ((0, 0), (0, width - a.shape[1])))


def _local_step(x, target, w_in, conv_w, a_log, dt_bias, delta_norm_w, sinks, rel_bias, w_o, ln1_g, ln1_b,
                w_up_g, w_down, ln2_g, ln2_b):
    S, D = x.shape
    aw = D // 2
    hq, hkv, nh = aw // HEAD_A, aw // HEAD_A // GQA, aw // HEAD_D
    kvw = hkv * HEAD_A
    c_q, c_k, c_v, c_d = 0, aw, aw + kvw, aw + 2 * kvw
    c_ab = c_d + 3 * aw
    c_z = c_ab + 2 * nh
    n_in = c_z + aw
    assert w_in.shape == (D, n_in), (w_in.shape, n_in)
    w_p = jnp.concatenate([w_in[:, :c_ab], w_in[:, c_z:], _pad_cols(w_in[:, c_ab:c_z], LANE)], axis=1)
    p_z, p_ab = c_ab, c_ab + aw
    n_p = p_ab + LANE

    proj = _matmul(x, w_p, NN, name="proj", tn=1152)
    bias = _attn_bias(rel_bias.T)
    attn_out = _attn_fwd(proj, bias, sinks.reshape(-1), hq, 0, c_k // kvw, c_v // kvw)
    conv2 = conv_w.reshape(CONV_W, 3 * aw)
    qkv = _gdn_prep_fwd(proj, conv2, nh, c_d // HEAD_D)
    ab = proj[:, p_ab:]
    al, dt = _pad_cols(a_log, LANE), _pad_cols(dt_bias, LANE)
    gb = _gates_fwd(ab, al, dt, nh)
    o_d, states = _gdn_fwd(qkv, gb, nh)
    delta_out = _gated_norm_fwd(o_d, proj, delta_norm_w, nh, p_z // HEAD_D)
    mix = jnp.concatenate([attn_out, delta_out], axis=1)
    mixed = _matmul(mix, w_o, NN, name="out_proj")
    h1 = _ln1_fwd(x, mixed, ln1_g, ln1_b)
    u = _matmul(h1, w_up_g, NN, name="mlp_up", b_groups=True)
    mlp = _matmul(u, w_down, NN, name="mlp_down", a_fn=_relu_sq)
    dr2, loss_row, dln2_g, dln2_b = _ln2_loss(h1, mlp, ln2_g, ln2_b, target)

    du = _matmul(dr2, w_down, NT, name="d_mlp_act", epi=_relu_sq_grad, epi_in=(u,))
    dw_down = _matmul(u, dr2, TN, name="dw_down", a_fn=_relu_sq, out_dtype=BF16)
    dw_up = _matmul(h1, du, TN, name="dw_up", out_dtype=BF16, out_groups=N_DEV)
    dh_mlp = _matmul(du, w_up_g, NT, name="d_h1", b_groups=True)
    dr1, dln1_g, dln1_b = _ln1_bwd(x, mixed, ln1_g, dr2, dh_mlp)
    dw_o = _matmul(mix, dr1, TN, name="dw_o", out_dtype=BF16)
    dmix = _matmul(dr1, w_o, NT, name="d_mix")
    dq_a, dk_a, dv_a, dbias, dsink = _attn_bwd(proj, bias, sinks.reshape(-1), attn_out, dmix, hq, 0, c_k // kvw, c_v // kvw)
    drel = _rel_bias_grad(dbias, hq)
    do_d, dz, dnw = _gated_norm_bwd(o_d, proj, delta_norm_w, dmix, nh, p_z // HEAD_D, aw // HEAD_D)
    dqn, dkn, dvn, dg, dbeta = _gdn_bwd(qkv, gb, states, do_d, nh)
    dgb = _pad_cols(jnp.concatenate([dg.reshape(nh, S).T, dbeta.reshape(nh, S).T], axis=1), LANE)
    dab, da_log, ddt_bias = _gates_bwd(ab, al, dt, dgb, nh)
    dqkv_d, dconv = _gdn_prep_bwd(proj, conv2, jnp.concatenate([dqn, dkn, dvn], axis=1), nh, c_d // HEAD_D)
    dproj = jnp.concatenate([dq_a, dk_a, dv_a, dqkv_d, dz, dab], axis=1)
    dw_p = _matmul(x, dproj, TN, name="dw_in", tn=1152)
    dx_proj = _matmul(dproj, w_p, NT, name="d_x", tk=1152)
    grad_x = _grad_x(dr1, dx_proj)
    dw_in = jnp.concatenate([dw_p[:, :p_z], dw_p[:, p_ab:p_ab + 2 * nh], dw_p[:, p_z:p_ab]], axis=1)

    small = dict(conv_w=dconv, a_log=da_log[:, :nh], dt_bias=ddt_bias[:, :nh], delta_norm_w=dnw,
                 attn_sinks=dsink[:, 0].reshape(1, hq), rel_bias=drel[:, :, 0].T,
                 ln1_g=dln1_g, ln1_b=dln1_b, ln2_g=dln2_g, ln2_b=dln2_b)
    return loss_row, grad_x, dw_in, small, dw_o, dw_up, dw_down


SMALL_ORDER = ("conv_w", "a_log", "dt_bias", "delta_norm_w", "attn_sinks", "rel_bias", "ln1_g", "ln1_b", "ln2_g", "ln2_b")


def _pack_small(loss_row, small):
    parts = [loss_row.reshape(-1)]
    for k in SMALL_ORDER:
        flat = small[k].reshape(-1)
        parts.append(jnp.pad(flat, (0, (-flat.shape[0]) % LANE)))
    flat = jnp.concatenate(parts)
    flat = jnp.pad(flat, (0, (-flat.shape[0]) % (8 * LANE)))
    return flat.reshape(-1, LANE)


def _unpack_small(buf, small_shapes):
    flat = buf.reshape(-1)
    loss = flat[0]
    off = LANE
    out = {}
    for k in SMALL_ORDER:
        n = int(np.prod(small_shapes[k]))
        out[k] = flat[off:off + n].reshape(small_shapes[k])
        off += n + (-n) % LANE
    return loss, out


def kernel(x, w_in, conv_w, a_log, dt_bias, delta_norm_w, attn_sinks, rel_bias, w_o, ln1_g, ln1_b, w_up, w_down, ln2_g, ln2_b, loss_target, m_w_in, m_conv_w, m_a_log, m_dt_bias, m_delta_norm_w, m_attn_sinks, m_rel_bias, m_w_o, m_ln1_g, m_ln1_b, m_w_up, m_w_down, m_ln2_g, m_ln2_b, v_w_in, v_conv_w, v_a_log, v_dt_bias, v_delta_norm_w, v_attn_sinks, v_rel_bias, v_w_o, v_ln1_g, v_ln1_b, v_w_up, v_w_down, v_ln2_g, v_ln2_b):
    S, D = x.shape[1], x.shape[2]
    core = lax.axis_index("c")
    dev = 4 * lax.axis_index("x") + 2 * lax.axis_index("y") + core

    w_in_g, w_o_g, w_up_g, w_down_g = _all_gather(
        [w_in[0].astype(BF16), w_o[0].astype(BF16), w_up[0].astype(BF16), w_down[0].astype(BF16)])
    n_in = N_DEV * w_in.shape[2]
    w_in_full = jnp.transpose(w_in_g, (1, 0, 2)).reshape(D, n_in)
    w_o_full = w_o_g.reshape(D, D)
    w_down_full = w_down_g.reshape(-1, D)

    cw_sh = conv_w.shape[3]
    conv_place = lax.dynamic_update_slice(jnp.zeros((CONV_W, N_DEV * cw_sh), F32), conv_w[0, :, 0, :], (0, dev * cw_sh))
    conv_full = _all_reduce_small(jnp.pad(conv_place.reshape(-1, LANE), ((0, (-conv_place.size // LANE) % 8), (0, 0))))
    conv_full = conv_full[:conv_place.size // LANE].reshape(CONV_W, N_DEV * cw_sh)

    loss_row, grad_x, dw_in, small, dw_o, dw_up, dw_down = _local_step(
        x[0], loss_target[0], w_in_full, conv_full, a_log, dt_bias, delta_norm_w, attn_sinks, rel_bias, w_o_full,
        ln1_g, ln1_b, w_up_g, w_down_full, ln2_g, ln2_b)

    dw_in_g = jnp.transpose(dw_in.reshape(D, N_DEV, -1), (1, 0, 2)).astype(BF16)
    grads = [dw_in_g, dw_o.reshape(N_DEV, -1, D), dw_up, dw_down.reshape(N_DEV, -1, D)]
    recv = _sibling_exchange(grads)
    core_arr = core.reshape(1).astype(jnp.int32)
    parts = [_chip_sum(gr, rc, core_arr, "rs_chip_sum_%d" % i) for i, (gr, rc) in enumerate(zip(grads, recv))]
    terms = _chip_exchange(parts)
    big = {}
    for name, t, w, m, v in (("w_in", terms[0], w_in, m_w_in, v_w_in), ("w_o", terms[1], w_o, m_w_o, v_w_o),
                             ("w_up", terms[2], w_up, m_w_up, v_w_up), ("w_down", terms[3], w_down, m_w_down, v_w_down)):
        outs = _adamw_big(t, w[0], m[0], v[0], "adamw_" + name)
        big[name] = [o[None] for o in outs]

    small_shapes = {k: v.shape for k, v in small.items()}
    loss, small = _unpack_small(_all_reduce_small(_pack_small(loss_row, small)), small_shapes)
    small["conv_w"] = lax.dynamic_slice(small["conv_w"], (0, dev * cw_sh), (CONV_W, cw_sh))
    small["rel_bias"] = small["rel_bias"].reshape(rel_bias.shape)
    p2 = dict(conv_w=(conv_w, m_conv_w, v_conv_w), a_log=(a_log, m_a_log, v_a_log), dt_bias=(dt_bias, m_dt_bias, v_dt_bias),
              delta_norm_w=(delta_norm_w, m_delta_norm_w, v_delta_norm_w), attn_sinks=(attn_sinks, m_attn_sinks, v_attn_sinks),
              rel_bias=(rel_bias, m_rel_bias, v_rel_bias), ln1_g=(ln1_g, m_ln1_g, v_ln1_g), ln1_b=(ln1_b, m_ln1_b, v_ln1_b),
              ln2_g=(ln2_g, m_ln2_g, v_ln2_g), ln2_b=(ln2_b, m_ln2_b, v_ln2_b))
    two_d = lambda a: a.reshape(-1, a.shape[-1])
    ws = [two_d(p2[k][0]) for k in SMALL_ORDER]
    gs = [two_d(small[k]) for k in SMALL_ORDER]
    ms = [two_d(p2[k][1]) for k in SMALL_ORDER]
    vs = [two_d(p2[k][2]) for k in SMALL_ORDER]
    ds, nms, nvs = _adamw_small(ws, gs, ms, vs)
    res = {}
    for i, k in enumerate(SMALL_ORDER):
        shp = p2[k][0].shape
        res[k] = [gs[i].reshape(shp), ds[i].reshape(shp), nms[i].reshape(shp), nvs[i].reshape(shp)]
    res.update(big)
    order = ("w_in", "conv_w", "a_log", "dt_bias", "delta_norm_w", "attn_sinks", "rel_bias", "w_o", "ln1_g", "ln1_b",
             "w_up", "w_down", "ln2_g", "ln2_b")
    return (loss, grad_x[None], *[res[k][0] for k in order], *[res[k][1] for k in order],
            *[res[k][2] for k in order], *[res[k][3] for k in order])
```

```python
import functools
import math

import numpy as np
import jax
import jax.numpy as jnp
from jax import lax
from jax.experimental import pallas as pl
from jax.experimental.pallas import tpu as pltpu

F32 = jnp.float32
BF16 = jnp.bfloat16
HIGHEST = lax.Precision.HIGHEST

N_DEV = 8
HEAD_A = 64
GQA = 4
BLK = 128
N_BUCKETS = 32
MAX_DISTANCE = 128
HEAD_D = 128
CONV_W = 4
CHUNK = 64
NEG_INF = -1e30
LN_EPS = 1e-5
RMS_EPS = 1e-6
DN_ALPHA = 2.0 ** 0.25
ADAM_LR, ADAM_B1, ADAM_B2, ADAM_EPS, ADAM_WD, ADAM_STEP = 0.001, 0.9, 0.999, 1e-08, 0.01, 10

LANE = 128
VMEM_LIMIT = 56 * 1024 * 1024

NN = ((1,), (0,))
NT = ((1,), (1,))
TN = ((0,), (0,))


def _dot(a, b, dims, prec=None):
    return lax.dot_general(a, b, (dims, ((), ())), precision=prec, preferred_element_type=F32)


def _tile(dim, pref):
    if dim <= pref:
        return dim
    t = (pref // LANE) * LANE
    while t > LANE and dim % t:
        t -= LANE
    assert dim % t == 0, (dim, pref)
    return t


def _params(sem):
    return pltpu.CompilerParams(dimension_semantics=sem, vmem_limit_bytes=VMEM_LIMIT)


def _matmul(a, b, dims, *, name, out_dtype=F32, tm=1024, tn=1024, tk=2048, a_fn=None, epi=None, epi_in=(),
            b_groups=None, out_groups=None):
    (ca,), (cb,) = dims
    M, K = a.shape[1 - ca], a.shape[ca]
    if b_groups:
        G, R, C = b.shape
        bshape = (R, G * C)
    else:
        bshape = b.shape
    N = bshape[1 - cb]
    assert bshape[cb] == K, (a.shape, b.shape, dims)
    tm, tk = _tile(M, tm), _tile(K, tk)
    if b_groups:
        lim = C if cb == 0 else tn
        tn = _tile(N, min(tn, lim))
        if cb == 1:
            tk = _tile(K, min(tk, C))
    else:
        tn = _tile(N, tn)
    if out_groups:
        tn = _tile(N, min(tn, N // out_groups))
    nk = K // tk

    def body(*refs):
        a_ref, b_ref = refs[0], refs[1]
        e_refs = refs[2:2 + len(epi_in)]
        o_ref = refs[2 + len(epi_in)]
        acc_ref = refs[3 + len(epi_in)] if nk > 1 else None
        k = pl.program_id(2)
        av = a_ref[...]
        if a_fn is not None:
            av = a_fn(av)
        prod = _dot(av.astype(BF16), b_ref[...].astype(BF16), dims)

        def finish(r):
            if epi is not None:
                r = epi(r, *[e[...] for e in e_refs])
            o_ref[...] = r.astype(out_dtype)

        if nk == 1:
            finish(prod)
            return

        @pl.when(k == 0)
        def _():
            acc_ref[...] = prod

        @pl.when(k > 0)
        def _():
            acc_ref[...] += prod

        @pl.when(k == nk - 1)
        def _():
            finish(acc_ref[...])

    a_spec = (pl.BlockSpec((tm, tk), lambda i, j, k: (i, k)) if ca == 1
              else pl.BlockSpec((tk, tm), lambda i, j, k: (k, i)))
    if b_groups:
        if cb == 0:
            per = C // tn
            b_spec = pl.BlockSpec((None, tk, tn), lambda i, j, k: (j // per, k, j % per))
        else:
            per = C // tk
            b_spec = pl.BlockSpec((None, tn, tk), lambda i, j, k: (k // per, j, k % per))
    else:
        b_spec = (pl.BlockSpec((tk, tn), lambda i, j, k: (k, j)) if cb == 0
                  else pl.BlockSpec((tn, tk), lambda i, j, k: (j, k)))
    e_specs = [pl.BlockSpec((tm, tn), lambda i, j, k: (i, j)) for _ in epi_in]
    if out_groups:
        per_o = (N // out_groups) // tn
        o_spec = pl.BlockSpec((None, tm, tn), lambda i, j, k: (j // per_o, i, j % per_o))
        o_shape = jax.ShapeDtypeStruct((out_groups, M, N // out_groups), out_dtype)
    else:
        o_spec = pl.BlockSpec((tm, tn), lambda i, j, k: (i, j))
        o_shape = jax.ShapeDtypeStruct((M, N), out_dtype)
    return pl.pallas_call(
        body, grid=(M // tm, N // tn, nk), in_specs=[a_spec, b_spec] + e_specs, out_specs=o_spec,
        out_shape=o_shape, scratch_shapes=[pltpu.VMEM((tm, tn), F32)] if nk > 1 else [],
        compiler_params=_params(("parallel", "parallel", "arbitrary")), name=name)(a, b, *epi_in)


def _relu_sq(u):
    r = jnp.maximum(u, 0.0)
    return r * r


def _relu_sq_grad(acc, u):
    return acc * (2.0 * jnp.maximum(u, 0.0))


def _ln_stats(r):
    mu = jnp.mean(r, axis=-1, keepdims=True)
    xc = r - mu
    var = jnp.mean(xc * xc, axis=-1, keepdims=True)
    rstd = lax.rsqrt(var + LN_EPS)
    return xc * rstd, rstd


def _ln_bwd(dy, xhat, rstd, g):
    dxh = dy * g
    m1 = jnp.mean(dxh, axis=-1, keepdims=True)
    m2 = jnp.mean(dxh * xhat, axis=-1, keepdims=True)
    return rstd * (dxh - m1 - xhat * m2)


def _row_call(body, ins, row_ins, outs, acc_outs, name, tr=256):
    S = ins[0].shape[0]
    tr = min(tr, S)
    n_in, n_row, n_out = len(ins), len(row_ins), len(outs)

    def wrapped(*refs):
        i = pl.program_id(0)
        acc_refs = refs[n_in + n_row + n_out:]

        @pl.when(i == 0)
        def _():
            for r in acc_refs:
                r[...] = jnp.zeros_like(r)

        body(*refs)

    in_specs = [pl.BlockSpec((tr, a.shape[1]), lambda i: (i, 0)) for a in ins]
    in_specs += [pl.BlockSpec(a.shape, lambda i: (0, 0)) for a in row_ins]
    out_specs = [pl.BlockSpec((tr, s.shape[1]), lambda i: (i, 0)) for s in outs]
    out_specs += [pl.BlockSpec(s.shape, lambda i: (0, 0)) for s in acc_outs]
    return pl.pallas_call(wrapped, grid=(S // tr,), in_specs=in_specs, out_specs=out_specs,
                          out_shape=list(outs) + list(acc_outs),
                          compiler_params=_params(("arbitrary",)), name=name)(*ins, *row_ins)


def _ln1_fwd(x, mixed, g, b):
    def body(x_ref, m_ref, g_ref, b_ref, h_ref):
        xhat, _ = _ln_stats(DN_ALPHA * x_ref[...] + m_ref[...])
        h_ref[...] = xhat * g_ref[...] + b_ref[...]
    return _row_call(body, [x, mixed], [g, b], [jax.ShapeDtypeStruct(x.shape, F32)], [], "ln1_fwd")[0]


def _ln2_loss(h1, mlp, g, b, target):
    S, D = h1.shape
    sds = jax.ShapeDtypeStruct

    def body(h_ref, m_ref, t_ref, g_ref, b_ref, dr_ref, loss_ref, dg_ref, db_ref):
        xhat, rstd = _ln_stats(DN_ALPHA * h_ref[...] + m_ref[...])
        gv = g_ref[...]
        err = xhat * gv + b_ref[...] - t_ref[...]
        loss_ref[...] += jnp.sum(jnp.sum(err * err, axis=0, keepdims=True), axis=1, keepdims=True) * (0.5 / D)
        dy = err * (1.0 / D)
        dg_ref[...] += jnp.sum(dy * xhat, axis=0, keepdims=True)
        db_ref[...] += jnp.sum(dy, axis=0, keepdims=True)
        dr_ref[...] = _ln_bwd(dy, xhat, rstd, gv)

    return _row_call(body, [h1, mlp, target], [g, b], [sds((S, D), F32)],
                     [sds((1, LANE), F32), sds((1, D), F32), sds((1, D), F32)], "ln2_loss")


def _ln1_bwd(x, mixed, g, dr2, dh_mlp):
    S, D = x.shape
    sds = jax.ShapeDtypeStruct

    def body(x_ref, m_ref, dr2_ref, dh_ref, g_ref, dr_ref, dg_ref, db_ref):
        xhat, rstd = _ln_stats(DN_ALPHA * x_ref[...] + m_ref[...])
        dy = DN_ALPHA * dr2_ref[...] + dh_ref[...]
        dg_ref[...] += jnp.sum(dy * xhat, axis=0, keepdims=True)
        db_ref[...] += jnp.sum(dy, axis=0, keepdims=True)
        dr_ref[...] = _ln_bwd(dy, xhat, rstd, g_ref[...])

    return _row_call(body, [x, mixed, dr2, dh_mlp], [g], [sds((S, D), F32)],
                     [sds((1, D), F32), sds((1, D), F32)], "ln1_bwd")


def _grad_x(dr1, dx_proj):
    def body(a_ref, b_ref, o_ref):
        o_ref[...] = DN_ALPHA * a_ref[...] + b_ref[...]
    return _row_call(body, [dr1, dx_proj], [], [jax.ShapeDtypeStruct(dr1.shape, F32)], [], "grad_x")[0]


def _bucket_table():
    qi = np.arange(BLK, dtype=np.int32)[:, None]
    kj = np.arange(2 * BLK, dtype=np.int32)[None, :]
    dist = qi + BLK - kj
    n = np.maximum(dist, 0)
    max_exact = N_BUCKETS // 2
    nf = np.maximum(n, 1).astype(np.float32)
    large = max_exact + (np.log(nf / np.float32(max_exact)) / np.float32(math.log(MAX_DISTANCE / max_exact))
                         * np.float32(N_BUCKETS - max_exact)).astype(np.int32)
    large = np.minimum(large, N_BUCKETS - 1)
    bucket = np.where(n < max_exact, n, large)
    return np.where((dist >= 0) & (dist < BLK), bucket, -1).astype(np.int32)


def _attn_bias(rel_bias_t):
    hq = rel_bias_t.shape[0]
    bucket = jnp.asarray(_bucket_table())

    def body(rb_ref, bk_ref, o_ref):
        h = pl.program_id(0)
        bk = bk_ref[...]
        acc = jnp.zeros((BLK, 2 * BLK), F32)
        for b in range(N_BUCKETS):
            acc = jnp.where(bk == b, rb_ref[h, b], acc)
        o_ref[...] = acc

    return pl.pallas_call(
        body, grid=(hq,),
        in_specs=[pl.BlockSpec(memory_space=pltpu.SMEM), pl.BlockSpec((BLK, 2 * BLK), lambda h: (0, 0))],
        out_specs=pl.BlockSpec((BLK, 2 * BLK), lambda h: (h, 0)),
        out_shape=jax.ShapeDtypeStruct((hq * BLK, 2 * BLK), F32),
        compiler_params=_params(("arbitrary",)), name="attn_bias")(rel_bias_t, bucket)


def _attn_probs(q, kc, kp, bias, sink, mask_c, mask_p):
    lc = jnp.where(mask_c, _dot(q, kc, NT) + bias[:, BLK:], NEG_INF)
    lp = jnp.where(mask_p, _dot(q, kp, NT) + bias[:, :BLK], NEG_INF)
    m = jnp.maximum(jnp.maximum(jnp.max(lc, axis=1, keepdims=True), jnp.max(lp, axis=1, keepdims=True)), sink)
    pc, pp, ps = jnp.exp(lc - m), jnp.exp(lp - m), jnp.exp(sink - m)
    inv = 1.0 / (jnp.sum(pc, axis=1, keepdims=True) + jnp.sum(pp, axis=1, keepdims=True) + ps)
    return pc, pp, ps, inv


def _attn_masks(n):
    qi = lax.broadcasted_iota(jnp.int32, (BLK, BLK), 0)
    kj = lax.broadcasted_iota(jnp.int32, (BLK, BLK), 1)
    return kj <= qi, (kj > qi) & (n > 0)


def _attn_fwd(proj, bias, sinks, hq, q_blk, k_blk, v_blk):
    S = proj.shape[0]
    hkv = hq // GQA
    wq, wk = hq * HEAD_A, hkv * HEAD_A

    def body(q_ref, k_ref, v_ref, bias_ref, sink_ref, o_ref):
        n = pl.program_id(0)
        cur = pl.multiple_of(n * BLK, BLK)
        prev = pl.multiple_of(jnp.maximum(n - 1, 0) * BLK, BLK)
        mask_c, mask_p = _attn_masks(n)
        for h4 in range(hkv):
            cs = slice(h4 * HEAD_A, (h4 + 1) * HEAD_A)
            kc, kp = k_ref[pl.ds(cur, BLK), cs].astype(BF16), k_ref[pl.ds(prev, BLK), cs].astype(BF16)
            vc, vp = v_ref[pl.ds(cur, BLK), cs].astype(BF16), v_ref[pl.ds(prev, BLK), cs].astype(BF16)
            for g in range(GQA):
                h = h4 * GQA + g
                hs = slice(h * HEAD_A, (h + 1) * HEAD_A)
                q = (q_ref[:, hs] * (HEAD_A ** -0.5)).astype(BF16)
                pc, pp, _, inv = _attn_probs(q, kc, kp, bias_ref[h * BLK:(h + 1) * BLK, :], sink_ref[h], mask_c, mask_p)
                o_ref[:, hs] = (_dot(pc.astype(BF16), vc, NN) + _dot(pp.astype(BF16), vp, NN)) * inv

    return pl.pallas_call(
        body, grid=(S // BLK,),
        in_specs=[pl.BlockSpec((BLK, wq), lambda n: (n, q_blk)), pl.BlockSpec((S, wk), lambda n: (0, k_blk)),
                  pl.BlockSpec((S, wk), lambda n: (0, v_blk)), pl.BlockSpec((hq * BLK, 2 * BLK), lambda n: (0, 0)),
                  pl.BlockSpec(memory_space=pltpu.SMEM)],
        out_specs=pl.BlockSpec((BLK, wq), lambda n: (n, 0)),
        out_shape=jax.ShapeDtypeStruct((S, wq), F32),
        compiler_params=_params(("arbitrary",)), name="attn_fwd")(proj, proj, proj, bias, sinks)


def _attn_bwd(proj, bias, sinks, out, dmix, hq, q_blk, k_blk, v_blk):
    S = proj.shape[0]
    hkv = hq // GQA
    wq, wk = hq * HEAD_A, hkv * HEAD_A
    sds = jax.ShapeDtypeStruct

    def body(q_ref, k_ref, v_ref, bias_ref, sink_ref, o_ref, do_ref, dq_ref, dk_ref, dv_ref, dbias_ref, dsink_ref):
        n = pl.program_id(0)

        @pl.when(n == 0)
        def _():
            dk_ref[...] = jnp.zeros_like(dk_ref)
            dv_ref[...] = jnp.zeros_like(dv_ref)
            dbias_ref[...] = jnp.zeros_like(dbias_ref)
            dsink_ref[...] = jnp.zeros_like(dsink_ref)

        cur = pl.multiple_of(n * BLK, BLK)
        prev = pl.multiple_of(jnp.maximum(n - 1, 0) * BLK, BLK)
        mask_c, mask_p = _attn_masks(n)
        for h4 in range(hkv):
            cs = slice(h4 * HEAD_A, (h4 + 1) * HEAD_A)
            kc, kp = k_ref[pl.ds(cur, BLK), cs].astype(BF16), k_ref[pl.ds(prev, BLK), cs].astype(BF16)
            vc, vp = v_ref[pl.ds(cur, BLK), cs].astype(BF16), v_ref[pl.ds(prev, BLK), cs].astype(BF16)
            dkc = jnp.zeros((BLK, HEAD_A), F32)
            dkp = jnp.zeros((BLK, HEAD_A), F32)
            dvc = jnp.zeros((BLK, HEAD_A), F32)
            dvp = jnp.zeros((BLK, HEAD_A), F32)
            for g in range(GQA):
                h = h4 * GQA + g
                hs = slice(h * HEAD_A, (h + 1) * HEAD_A)
                rows = slice(h * BLK, (h + 1) * BLK)
                q = (q_ref[:, hs] * (HEAD_A ** -0.5)).astype(BF16)
                pc, pp, ps, inv = _attn_probs(q, kc, kp, bias_ref[rows, :], sink_ref[h], mask_c, mask_p)
                pc, pp, ps = pc * inv, pp * inv, ps * inv
                do = do_ref[:, hs]
                delta = jnp.sum(do * o_ref[:, hs], axis=1, keepdims=True)
                dob = do.astype(BF16)
                dsc = pc * (_dot(dob, vc, NT) - delta)
                dsp = pp * (_dot(dob, vp, NT) - delta)
                dsink_ref[h:h + 1, :] += jnp.broadcast_to(jnp.sum(-ps * delta, axis=0, keepdims=True), (1, LANE))
                dbias_ref[rows, BLK:] += dsc
                dbias_ref[rows, :BLK] += dsp
                dscb, dspb = dsc.astype(BF16), dsp.astype(BF16)
                dq_ref[:, hs] = (_dot(dscb, kc, NN) + _dot(dspb, kp, NN)) * (HEAD_A ** -0.5)
                dkc += _dot(dscb, q, TN)
                dkp += _dot(dspb, q, TN)
                dvc += _dot(pc.astype(BF16), dob, TN)
                dvp += _dot(pp.astype(BF16), dob, TN)
            dk_ref[pl.ds(cur, BLK), cs] += dkc
            dk_ref[pl.ds(prev, BLK), cs] += dkp
            dv_ref[pl.ds(cur, BLK), cs] += dvc
            dv_ref[pl.ds(prev, BLK), cs] += dvp

    return pl.pallas_call(
        body, grid=(S // BLK,),
        in_specs=[pl.BlockSpec((BLK, wq), lambda n: (n, q_blk)), pl.BlockSpec((S, wk), lambda n: (0, k_blk)),
                  pl.BlockSpec((S, wk), lambda n: (0, v_blk)), pl.BlockSpec((hq * BLK, 2 * BLK), lambda n: (0, 0)),
                  pl.BlockSpec(memory_space=pltpu.SMEM),
                  pl.BlockSpec((BLK, wq), lambda n: (n, 0)), pl.BlockSpec((BLK, wq), lambda n: (n, 0))],
        out_specs=[pl.BlockSpec((BLK, wq), lambda n: (n, 0)), pl.BlockSpec((S, wk), lambda n: (0, 0)),
                   pl.BlockSpec((S, wk), lambda n: (0, 0)), pl.BlockSpec((hq * BLK, 2 * BLK), lambda n: (0, 0)),
                   pl.BlockSpec((hq, LANE), lambda n: (0, 0))],
        out_shape=[sds((S, wq), F32), sds((S, wk), F32), sds((S, wk), F32), sds((hq * BLK, 2 * BLK), F32),
                   sds((hq, LANE), F32)],
        compiler_params=_params(("arbitrary",)), name="attn_bwd")(proj, proj, proj, bias, sinks, out, dmix)


def _rel_bias_grad(dbias, hq):
    bucket = jnp.asarray(_bucket_table())

    def body(d_ref, bk_ref, o_ref):
        d = d_ref[...]
        bk = bk_ref[...]
        rows = [jnp.sum(jnp.where(bk == b, d, 0.0), axis=0, keepdims=True) for b in range(N_BUCKETS)]
        tot = jnp.sum(jnp.concatenate(rows, axis=0), axis=1, keepdims=True)
        o_ref[...] = jnp.broadcast_to(tot, (N_BUCKETS, LANE))

    return pl.pallas_call(
        body, grid=(hq,),
        in_specs=[pl.BlockSpec((BLK, 2 * BLK), lambda h: (h, 0)), pl.BlockSpec((BLK, 2 * BLK), lambda h: (0, 0))],
        out_specs=pl.BlockSpec((None, N_BUCKETS, LANE), lambda h: (h, 0, 0)),
        out_shape=jax.ShapeDtypeStruct((hq, N_BUCKETS, LANE), F32),
        compiler_params=_params(("arbitrary",)), name="rel_bias_grad")(dbias, bucket)


def _sigmoid(x):
    return 1.0 / (1.0 + jnp.exp(-x))


def _shift_rows(x, s):
    n = x.shape[0]
    row = lax.broadcasted_iota(jnp.int32, x.shape, 0)
    if s > 0:
        return jnp.where(row >= s, pltpu.roll(x, s, 0), 0.0)
    return jnp.where(row < n + s, pltpu.roll(x, n + s, 0), 0.0)


def _conv_silu_norm(xv, w, j, nh):
    c = w[CONV_W - 1:CONV_W, :] * xv
    for s in range(1, CONV_W):
        c = c + w[CONV_W - 1 - s:CONV_W - s, :] * _shift_rows(xv, s)
    sg = _sigmoid(c)
    a = c * sg
    r = lax.rsqrt(jnp.sum(a * a, axis=1, keepdims=True) + RMS_EPS)
    scale = jnp.where(j < nh, HEAD_D ** -0.5, 1.0)
    is_norm = j < 2 * nh
    y = jnp.where(is_norm, a * (r * scale), a)
    return c, sg, a, r, scale, is_norm, y


def _gdn_prep_fwd(proj, conv_w, nh, blk0):
    S = proj.shape[0]

    def body(x_ref, w_ref, o_ref):
        j = pl.program_id(0)
        o_ref[...] = _conv_silu_norm(x_ref[...], w_ref[...], j, nh)[-1]

    return pl.pallas_call(
        body, grid=(3 * nh,),
        in_specs=[pl.BlockSpec((S, HEAD_D), lambda j: (0, blk0 + j)), pl.BlockSpec((CONV_W, HEAD_D), lambda j: (0, j))],
        out_specs=pl.BlockSpec((S, HEAD_D), lambda j: (0, j)),
        out_shape=jax.ShapeDtypeStruct((S, 3 * nh * HEAD_D), F32),
        compiler_params=_params(("parallel",)), name="gdn_prep_fwd")(proj, conv_w)


def _gdn_prep_bwd(proj, conv_w, dqkv, nh, blk0):
    S = proj.shape[0]
    sds = jax.ShapeDtypeStruct

    def body(x_ref, w_ref, dy_ref, dx_ref, dw_ref):
        j = pl.program_id(0)
        xv, w = x_ref[...], w_ref[...]
        c, sg, a, r, scale, is_norm, _ = _conv_silu_norm(xv, w, j, nh)
        dy = dy_ref[...]
        rs = r * scale
        da_n = rs * dy - a * (r * r * rs) * jnp.sum(dy * a, axis=1, keepdims=True)
        da = jnp.where(is_norm, da_n, dy)
        dc = da * (sg * (1.0 + c * (1.0 - sg)))
        dx = w[CONV_W - 1:CONV_W, :] * dc
        dws = [jnp.sum(dc * xv, axis=0, keepdims=True)]
        for s in range(1, CONV_W):
            dx = dx + w[CONV_W - 1 - s:CONV_W - s, :] * _shift_rows(dc, -s)
            dws.insert(0, jnp.sum(dc * _shift_rows(xv, s), axis=0, keepdims=True))
        dx_ref[...] = dx
        dw_ref[...] = jnp.concatenate(dws, axis=0)

    return pl.pallas_call(
        body, grid=(3 * nh,),
        in_specs=[pl.BlockSpec((S, HEAD_D), lambda j: (0, blk0 + j)), pl.BlockSpec((CONV_W, HEAD_D), lambda j: (0, j)),
                  pl.BlockSpec((S, HEAD_D), lambda j: (0, j))],
        out_specs=[pl.BlockSpec((S, HEAD_D), lambda j: (0, j)), pl.BlockSpec((CONV_W, HEAD_D), lambda j: (0, j))],
        out_shape=[sds((S, 3 * nh * HEAD_D), F32), sds((CONV_W, 3 * nh * HEAD_D), F32)],
        compiler_params=_params(("parallel",)), name="gdn_prep_bwd")(proj, conv_w, dqkv)


def _softplus(x):
    return jnp.maximum(x, 0.0) + jnp.log(1.0 + jnp.exp(-jnp.abs(x)))


def _gates_fwd(ab, al, dt, nh):
    S = ab.shape[0]

    def body(ab_ref, al_ref, dt_ref, o_ref):
        v = ab_ref[...]
        lane = lax.broadcasted_iota(jnp.int32, v.shape, 1)
        g = -jnp.exp(al_ref[...]) * _softplus(v + dt_ref[...])
        o_ref[...] = jnp.where(lane < nh, g, jnp.where(lane < 2 * nh, _sigmoid(v), 0.0))

    row = pl.BlockSpec((1, LANE), lambda i: (0, 0))
    full = pl.BlockSpec((S, LANE), lambda i: (0, 0))
    return pl.pallas_call(body, grid=(1,), in_specs=[full, row, row], out_specs=full,
                          out_shape=jax.ShapeDtypeStruct((S, LANE), F32),
                          compiler_params=_params(("arbitrary",)), name="gates_fwd")(ab, al, dt)


def _gates_bwd(ab, al, dt, dgb, nh):
    S = ab.shape[0]
    sds = jax.ShapeDtypeStruct

    def body(ab_ref, al_ref, dt_ref, d_ref, dab_ref, dal_ref, ddt_ref):
        v, d = ab_ref[...], d_ref[...]
        lane = lax.broadcasted_iota(jnp.int32, v.shape, 1)
        is_a = lane < nh
        z = v + dt_ref[...]
        dsp = jnp.where(is_a, d * (-jnp.exp(al_ref[...])), 0.0)
        dz = dsp * _sigmoid(z)
        beta = _sigmoid(v)
        dab_ref[...] = jnp.where(is_a, dz, jnp.where(lane < 2 * nh, d * beta * (1.0 - beta), 0.0))
        dal_ref[...] = jnp.sum(dsp * _softplus(z), axis=0, keepdims=True)
        ddt_ref[...] = jnp.sum(dz, axis=0, keepdims=True)

    row = pl.BlockSpec((1, LANE), lambda i: (0, 0))
    full = pl.BlockSpec((S, LANE), lambda i: (0, 0))
    return pl.pallas_call(body, grid=(1,), in_specs=[full, row, row, full], out_specs=[full, row, row],
                          out_shape=[sds((S, LANE), F32), sds((1, LANE), F32), sds((1, LANE), F32)],
                          compiler_params=_params(("arbitrary",)), name="gates_bwd")(ab, al, dt, dgb)


def _col_of(tile, h):
    lane = lax.broadcasted_iota(jnp.int32, tile.shape, 1)
    return jnp.sum(jnp.where(lane == h, tile, 0.0), axis=1, keepdims=True)


def _to_row(col, eye):
    return jnp.sum(jnp.where(eye, col, 0.0), axis=0, keepdims=True)


def _to_col(row, eye):
    return jnp.sum(jnp.where(eye, row, 0.0), axis=1, keepdims=True)


def _split(a):
    hi = a.astype(BF16)
    return hi, (a - hi.astype(F32)).astype(BF16)


def _gdot(a, b, dims):
    ah, al = _split(a)
    bh, bl = _split(b)
    return _dot(ah, bh, dims) + (_dot(ah, bl, dims) + _dot(al, bh, dims))


def _chunk_local(q, k, v, gcol, bcol, T=None):
    C = CHUNK
    row = lax.broadcasted_iota(jnp.int32, (C, C), 0)
    col = lax.broadcasted_iota(jnp.int32, (C, C), 1)
    tril, strict, eye = col <= row, col < row, col == row
    grow = _to_row(gcol, eye)
    G_row = jnp.sum(jnp.where(row <= col, gcol, 0.0), axis=0, keepdims=True)
    G_col = jnp.sum(jnp.where(tril, grow, 0.0), axis=1, keepdims=True)
    decay = jnp.exp(jnp.where(tril, G_col - G_row, NEG_INF))
    G_last = G_col[C - 1:C, :]
    eG = jnp.exp(G_col)
    eGr = jnp.exp(G_last - G_col)
    gl = jnp.exp(G_last)
    kb = k * bcol
    A = jnp.where(strict, _gdot(kb, k, NT) * decay, 0.0)
    attn = _gdot(q, k, NT) * decay
    out = dict(strict=strict, eye=eye, row=row, col=col, decay=decay, eG=eG, eGr=eGr, gl=gl, kb=kb, A=A,
               rhs_k=kb * eG, attn=attn, q_dec=q * eG, k_dec=k * eGr)
    if T is None:
        T = jnp.where(eye, 1.0, 0.0) - A
        P = A
        for _ in range(int(math.log2(C)) - 1):
            P = _gdot(P, P, NN)
            T = T + _gdot(T, P, NN)
        out.update(T=T, u=_gdot(T, v * bcol, NN), w=_gdot(T, out["rhs_k"], NN))
    return out


GDN_ROWS = 256
WQK = 3 * CHUNK


def _gdn_local_fwd(qkv, gb, nh):
    S = qkv.shape[0]
    nc = S // CHUNK
    rb = min(GDN_ROWS, S)
    cpb = rb // CHUNK
    sds = jax.ShapeDtypeStruct

    def body(q_ref, k_ref, v_ref, gb_ref, u_ref, wqk_ref, attn_ref, t_ref):
        h = pl.program_id(0)
        for ci in range(cpb):
            rows = slice(ci * CHUNK, (ci + 1) * CHUNK)
            gbt = gb_ref[rows, :]
            L = _chunk_local(q_ref[rows, :], k_ref[rows, :], v_ref[rows, :], _col_of(gbt, h), _col_of(gbt, nh + h))
            u_ref[rows, :] = L["u"]
            base = ci * WQK
            wqk_ref[base:base + CHUNK, :] = L["w"]
            wqk_ref[base + CHUNK:base + 2 * CHUNK, :] = L["q_dec"]
            wqk_ref[base + 2 * CHUNK:base + WQK, :] = L["k_dec"]
            attn_ref[ci] = L["attn"]
            t_ref[ci] = L["T"]

    cc = pl.BlockSpec((None, cpb, CHUNK, CHUNK), lambda h, i: (h, i, 0, 0))
    return pl.pallas_call(
        body, grid=(nh, S // rb),
        in_specs=[pl.BlockSpec((rb, HEAD_D), lambda h, i: (i, h)), pl.BlockSpec((rb, HEAD_D), lambda h, i: (i, nh + h)),
                  pl.BlockSpec((rb, HEAD_D), lambda h, i: (i, 2 * nh + h)), pl.BlockSpec((rb, LANE), lambda h, i: (i, 0))],
        out_specs=[pl.BlockSpec((rb, HEAD_D), lambda h, i: (i, h)),
                   pl.BlockSpec((None, 3 * rb, HEAD_D), lambda h, i: (h, i, 0)), cc, cc],
        out_shape=[sds((S, nh * HEAD_D), F32), sds((nh, 3 * S, HEAD_D), F32), sds((nh, nc, CHUNK, CHUNK), F32),
                   sds((nh, nc, CHUNK, CHUNK), F32)],
        compiler_params=_params(("parallel", "parallel")), name="gdn_local_fwd")(qkv, qkv, qkv, gb)


def _gdn_scan_fwd(u, wqk, attn, gb, nh):
    S = u.shape[0]
    nc = S // CHUNK
    rb = min(GDN_ROWS, S)
    cpb = rb // CHUNK
    sds = jax.ShapeDtypeStruct

    def body(u_ref, wqk_ref, attn_ref, gb_ref, o_ref, vn_ref, st_ref, s_ref):
        @pl.when(pl.program_id(0) == 0)
        def _():
            s_ref[...] = jnp.zeros_like(s_ref)

        for ci in range(cpb):
            rows = slice(ci * CHUNK, (ci + 1) * CHUNK)
            glv = jnp.exp(jnp.sum(gb_ref[rows, :], axis=0, keepdims=True))
            base = ci * WQK
            for h in range(nh):
                cols = slice(h * HEAD_D, (h + 1) * HEAD_D)
                state = s_ref[h]
                st_ref[h, ci] = state
                r = _gdot(wqk_ref[h, base:base + 2 * CHUNK, :], state, NN)
                vb = u_ref[rows, cols] - r[:CHUNK]
                o_ref[rows, cols] = r[CHUNK:] + _gdot(attn_ref[h, ci], vb, NN)
                vn_ref[rows, cols] = vb
                s_ref[h] = state * glv[:, h:h + 1] + _gdot(wqk_ref[h, base + 2 * CHUNK:base + WQK, :], vb, TN)

    return pl.pallas_call(
        body, grid=(S // rb,),
        in_specs=[pl.BlockSpec((rb, nh * HEAD_D), lambda i: (i, 0)), pl.BlockSpec((nh, 3 * rb, HEAD_D), lambda i: (0, i, 0)),
                  pl.BlockSpec((nh, cpb, CHUNK, CHUNK), lambda i: (0, i, 0, 0)), pl.BlockSpec((rb, LANE), lambda i: (i, 0))],
        out_specs=[pl.BlockSpec((rb, nh * HEAD_D), lambda i: (i, 0)), pl.BlockSpec((rb, nh * HEAD_D), lambda i: (i, 0)),
                   pl.BlockSpec((nh, cpb, HEAD_D, HEAD_D), lambda i: (0, i, 0, 0))],
        out_shape=[sds((S, nh * HEAD_D), F32), sds((S, nh * HEAD_D), F32), sds((nh, nc, HEAD_D, HEAD_D), F32)],
        scratch_shapes=[pltpu.VMEM((nh, HEAD_D, HEAD_D), F32)],
        compiler_params=_params(("arbitrary",)), name="gdn_scan_fwd")(u, wqk, attn, gb)


def _gdn_scan_bwd(wqk, attn, gb, states, vn, do, nh):
    S = vn.shape[0]
    nc = S // CHUNK
    rb = min(GDN_ROWS, S)
    cpb = rb // CHUNK
    last = S // rb - 1
    sds = jax.ShapeDtypeStruct

    def body(wqk_ref, attn_ref, gb_ref, st_ref, vn_ref, do_ref, dvn_ref, dw_ref, dqd_ref, dkd_ref, da_ref, dgl_ref, ds_ref):
        @pl.when(pl.program_id(0) == 0)
        def _():
            ds_ref[...] = jnp.zeros_like(ds_ref)

        row = lax.broadcasted_iota(jnp.int32, (CHUNK, CHUNK), 0)
        col = lax.broadcasted_iota(jnp.int32, (CHUNK, CHUNK), 1)
        for ci in reversed(range(cpb)):
            rows = slice(ci * CHUNK, (ci + 1) * CHUNK)
            glv = jnp.exp(jnp.sum(gb_ref[rows, :], axis=0, keepdims=True))
            base = ci * WQK
            for h in range(nh):
                cols = slice(h * HEAD_D, (h + 1) * HEAD_D)
                state, dS = st_ref[h, ci], ds_ref[h]
                wq = wqk_ref[h, base:base + 2 * CHUNK, :]
                kd = wqk_ref[h, base + 2 * CHUNK:base + WQK, :]
                vb = vn_ref[rows, cols]
                dob = do_ref[rows, cols]
                dvb = _gdot(attn_ref[h, ci], dob, TN) + _gdot(kd, dS, NN)
                x = _gdot(jnp.concatenate([dob, dvb], axis=0), state, NT)
                dqd_ref[rows, cols] = x[:CHUNK]
                dw_ref[rows, cols] = -x[CHUNK:]
                dvn_ref[rows, cols] = dvb
                da_ref[h, ci] = jnp.where(col <= row, _gdot(dob, vb, NT), 0.0)
                dkd_ref[rows, cols] = _gdot(vb, dS, NT)
                gl = glv[:, h:h + 1]
                dgl = jnp.sum(jnp.sum(state * dS, axis=0, keepdims=True), axis=1, keepdims=True)
                dgl_ref[h, ci] = jnp.broadcast_to(dgl * gl, (1, LANE))
                ds_ref[h] = dS * gl + _gdot(wq, jnp.concatenate([-dvb, dob], axis=0), TN)

    rv = lambda i: last - i
    wide = pl.BlockSpec((rb, nh * HEAD_D), lambda i: (rv(i), 0))
    return pl.pallas_call(
        body, grid=(S // rb,),
        in_specs=[pl.BlockSpec((nh, 3 * rb, HEAD_D), lambda i: (0, rv(i), 0)),
                  pl.BlockSpec((nh, cpb, CHUNK, CHUNK), lambda i: (0, rv(i), 0, 0)),
                  pl.BlockSpec((rb, LANE), lambda i: (rv(i), 0)),
                  pl.BlockSpec((nh, cpb, HEAD_D, HEAD_D), lambda i: (0, rv(i), 0, 0)), wide, wide],
        out_specs=[wide, wide, wide, wide, pl.BlockSpec((nh, cpb, CHUNK, CHUNK), lambda i: (0, rv(i), 0, 0)),
                   pl.BlockSpec((nh, cpb, 1, LANE), lambda i: (0, rv(i), 0, 0))],
        out_shape=[sds((S, nh * HEAD_D), F32), sds((S, nh * HEAD_D), F32), sds((S, nh * HEAD_D), F32),
                   sds((S, nh * HEAD_D), F32), sds((nh, nc, CHUNK, CHUNK), F32), sds((nh, nc, 1, LANE), F32)],
        scratch_shapes=[pltpu.VMEM((nh, HEAD_D, HEAD_D), F32)],
        compiler_params=_params(("arbitrary",)), name="gdn_scan_bwd")(wqk, attn, gb, states, vn, do)


def _gdn_local_bwd(qkv, gb, T, u, wqk, dvn, dw, dqd, dkd, dattn, dgl, nh):
    S = qkv.shape[0]
    rb = min(GDN_ROWS, S)
    cpb = rb // CHUNK
    sds = jax.ShapeDtypeStruct

    def body(q_ref, k_ref, v_ref, gb_ref, t_ref, u_ref, wqk_ref, dvn_ref, dw_ref, dqd_ref, dkd_ref, da_ref, dgl_ref,
             dq_ref, dk_ref, dv_ref, dg_ref, db_ref):
        h = pl.program_id(0)
        for ci in range(cpb):
            rows = slice(ci * CHUNK, (ci + 1) * CHUNK)
            q, k, v = q_ref[rows, :], k_ref[rows, :], v_ref[rows, :]
            gbt = gb_ref[rows, :]
            bcol = _col_of(gbt, nh + h)
            T = t_ref[ci]
            L = _chunk_local(q, k, v, _col_of(gbt, h), bcol, T=T)
            strict, eye, decay, eG, eGr, kb = L["strict"], L["eye"], L["decay"], L["eG"], L["eGr"], L["kb"]
            attn, q_dec, k_dec = L["attn"], L["q_dec"], L["k_dec"]
            w = wqk_ref[ci * WQK:ci * WQK + CHUNK, :]
            dq_dec, dk_dec, dattn_c = dqd_ref[rows, :], dkd_ref[rows, :], da_ref[ci]
            drv = _gdot(T, dvn_ref[rows, :], TN)
            drk = _gdot(T, dw_ref[rows, :], TN)
            dA = jnp.where(strict, -(_gdot(drv, u_ref[rows, :], NT) + _gdot(drk, w, NT)), 0.0)
            dM = dA * decay
            dN = dattn_c * decay
            dkb = _gdot(dM, k, NN)
            dq_ref[rows, :] = _gdot(dN, k, NN) + dq_dec * eG
            dk_ref[rows, :] = (drk * (bcol * eG) + _gdot(dM, kb, TN) + dkb * bcol + _gdot(dN, q, TN) + dk_dec * eGr)
            dv_ref[rows, :] = drv * bcol
            db_ref[rows, :] = (jnp.sum(drv * v, axis=1, keepdims=True) + jnp.sum(drk * k, axis=1, keepdims=True) * eG
                               + jnp.sum(dkb * k, axis=1, keepdims=True))
            E = dA * L["A"] + dattn_c * attn
            kd = jnp.sum(dk_dec * k_dec, axis=1, keepdims=True)
            dG = (jnp.sum(dq_dec * q_dec, axis=1, keepdims=True) - kd + jnp.sum(drk * L["rhs_k"], axis=1, keepdims=True)
                  + jnp.sum(E, axis=1, keepdims=True) - _to_col(jnp.sum(E, axis=0, keepdims=True), eye))
            d_last = jnp.sum(kd, axis=0, keepdims=True) + dgl_ref[ci][:, :1]
            dG = dG + jnp.where(L["row"][:, :1] == CHUNK - 1, d_last, 0.0)
            dg_ref[rows, :] = jnp.sum(jnp.where(L["col"] >= L["row"], _to_row(dG, eye), 0.0), axis=1, keepdims=True)

    hd = pl.BlockSpec((rb, HEAD_D), lambda h, i: (i, h))
    cc = pl.BlockSpec((None, cpb, CHUNK, CHUNK), lambda h, i: (h, i, 0, 0))
    col1 = pl.BlockSpec((None, rb, 1), lambda h, i: (h, i, 0))
    return pl.pallas_call(
        body, grid=(nh, S // rb),
        in_specs=[hd, pl.BlockSpec((rb, HEAD_D), lambda h, i: (i, nh + h)),
                  pl.BlockSpec((rb, HEAD_D), lambda h, i: (i, 2 * nh + h)), pl.BlockSpec((rb, LANE), lambda h, i: (i, 0)),
                  cc, hd, pl.BlockSpec((None, 3 * rb, HEAD_D), lambda h, i: (h, i, 0)), hd, hd, hd, hd, cc,
                  pl.BlockSpec((None, cpb, 1, LANE), lambda h, i: (h, i, 0, 0))],
        out_specs=[hd, hd, hd, col1, col1],
        out_shape=[sds((S, nh * HEAD_D), F32)] * 3 + [sds((nh, S, 1), F32)] * 2,
        compiler_params=_params(("parallel", "parallel")), name="gdn_local_bwd")(
            qkv, qkv, qkv, gb, T, u, wqk, dvn, dw, dqd, dkd, dattn, dgl)


def _gated_norm_fwd(o, proj, norm_w, nh, z_blk0):
    S = o.shape[0]

    def body(o_ref, z_ref, w_ref, y_ref):
        ov, z = o_ref[...], z_ref[...]
        r = lax.rsqrt(jnp.mean(ov * ov, axis=1, keepdims=True) + RMS_EPS)
        y_ref[...] = ov * r * w_ref[...] * (z * _sigmoid(z))

    return pl.pallas_call(
        body, grid=(nh,),
        in_specs=[pl.BlockSpec((S, HEAD_D), lambda h: (0, h)), pl.BlockSpec((S, HEAD_D), lambda h: (0, z_blk0 + h)),
                  pl.BlockSpec((1, HEAD_D), lambda h: (0, 0))],
        out_specs=pl.BlockSpec((S, HEAD_D), lambda h: (0, h)),
        out_shape=jax.ShapeDtypeStruct((S, nh * HEAD_D), F32),
        compiler_params=_params(("parallel",)), name="gated_norm_fwd")(o, proj, norm_w)


def _gated_norm_bwd(o, proj, norm_w, dmix, nh, z_blk0, d_blk0):
    S = o.shape[0]
    sds = jax.ShapeDtypeStruct

    def body(o_ref, z_ref, w_ref, dy_ref, do_ref, dz_ref, dw_ref):
        ov, z, w, dy = o_ref[...], z_ref[...], w_ref[...], dy_ref[...]
        r = lax.rsqrt(jnp.mean(ov * ov, axis=1, keepdims=True) + RMS_EPS)
        oh = ov * r
        sg = _sigmoid(z)
        dz_ref[...] = dy * (oh * w) * (sg * (1.0 + z * (1.0 - sg)))
        don = dy * (z * sg)
        @pl.when(pl.program_id(0) == 0)
        def _():
            dw_ref[...] = jnp.zeros_like(dw_ref)

        dw_ref[...] += jnp.sum(don * oh, axis=0, keepdims=True)
        doh = don * w
        do_ref[...] = r * (doh - oh * jnp.mean(doh * oh, axis=1, keepdims=True))

    return pl.pallas_call(
        body, grid=(nh,),
        in_specs=[pl.BlockSpec((S, HEAD_D), lambda h: (0, h)), pl.BlockSpec((S, HEAD_D), lambda h: (0, z_blk0 + h)),
                  pl.BlockSpec((1, HEAD_D), lambda h: (0, 0)), pl.BlockSpec((S, HEAD_D), lambda h: (0, d_blk0 + h))],
        out_specs=[pl.BlockSpec((S, HEAD_D), lambda h: (0, h)), pl.BlockSpec((S, HEAD_D), lambda h: (0, h)),
                   pl.BlockSpec((1, HEAD_D), lambda h: (0, 0))],
        out_shape=[sds((S, nh * HEAD_D), F32), sds((S, nh * HEAD_D), F32), sds((1, HEAD_D), F32)],
        compiler_params=_params(("arbitrary",)), name="gated_norm_bwd")(o, proj, norm_w, dmix)


def _adamw_math(w, g, m, v):
    m = ADAM_B1 * m + (1.0 - ADAM_B1) * g
    v = ADAM_B2 * v + (1.0 - ADAM_B2) * (g * g)
    m_hat = m / (1.0 - ADAM_B1 ** ADAM_STEP)
    v_hat = v / (1.0 - ADAM_B2 ** ADAM_STEP)
    delta = -ADAM_LR * (m_hat / (jnp.sqrt(v_hat) + ADAM_EPS) + ADAM_WD * w)
    return delta, m, v


def _adamw_big(terms, w, m, v, name, tr=256):
    R, C = w.shape
    tr = _tile(R, tr)
    sds = jax.ShapeDtypeStruct

    def body(t_ref, w_ref, m_ref, v_ref, g_ref, d_ref, nm_ref, nv_ref):
        g = ((t_ref[0].astype(F32) + t_ref[1].astype(F32)) + t_ref[2].astype(F32)) + t_ref[3].astype(F32)
        g_ref[...] = g
        d_ref[...], nm_ref[...], nv_ref[...] = _adamw_math(w_ref[...], g, m_ref[...], v_ref[...])

    spec = pl.BlockSpec((tr, C), lambda i: (i, 0))
    return pl.pallas_call(
        body, grid=(R // tr,), in_specs=[pl.BlockSpec((4, tr, C), lambda i: (0, i, 0)), spec, spec, spec],
        out_specs=[spec] * 4, out_shape=[sds((R, C), F32)] * 4,
        compiler_params=_params(("parallel",)), name=name)(terms, w, m, v)


def _adamw_small(ws, gs, ms, vs):
    n = len(ws)

    def body(*refs):
        for i in range(n):
            w, g, m, v = (refs[k * n + i][...] for k in range(4))
            d, nm, nv = _adamw_math(w, g, m, v)
            refs[4 * n + i][...] = d
            refs[5 * n + i][...] = nm
            refs[6 * n + i][...] = nv

    shapes = [jax.ShapeDtypeStruct(w.shape, F32) for w in ws]
    vm = pl.BlockSpec(memory_space=pltpu.VMEM)
    outs = pl.pallas_call(body, in_specs=[vm] * (4 * n), out_specs=[vm] * (3 * n), out_shape=shapes * 3,
                          name="adamw_small")(*ws, *gs, *ms, *vs)
    return outs[:n], outs[n:2 * n], outs[2 * n:]


MESH = pl.DeviceIdType.MESH
ANY = pl.BlockSpec(memory_space=pl.ANY)


def _place():
    x, y, c = lax.axis_index("x"), lax.axis_index("y"), lax.axis_index("c")
    return x, y, c, [(1 - x, y), (x, 1 - y), (1 - x, 1 - y)]


def _all_gather(shards):
    n = len(shards)

    def body(*refs):
        ins, outs = refs[:n], refs[n:2 * n]
        send_sems, recv_sems, local_sems = refs[2 * n:]
        x, y, c, chips = _place()
        me, sibling = (x, y, c), (x, y, 1 - c)

        def slot(px, py, pc):
            return 4 * px + 2 * py + pc

        def copy(w, k, block, to, src=None):
            dst = outs[w].at[slot(*block)]
            return pltpu.make_async_remote_copy(src_ref=dst if src is None else src, dst_ref=dst,
                                                send_sem=send_sems.at[w * 7 + k], recv_sem=recv_sems.at[w * 7 + k],
                                                device_id=to, device_id_type=MESH)

        mine = [pltpu.make_async_copy(ins[w], outs[w].at[slot(*me)], local_sems.at[w]) for w in range(n)]
        for cp in mine:
            cp.start()
        first = []
        for w in range(n):
            first.append(copy(w, 0, me, sibling, src=ins[w]))
            first += [copy(w, 1 + j, me, (*chip, c), src=ins[w]) for j, chip in enumerate(chips)]
        for cp in first:
            cp.start()
        passed = []
        for w in range(n):
            for j, chip in enumerate(chips):
                copy(w, 1 + j, (*chip, c), me).wait_recv()
                cp = copy(w, 4 + j, (*chip, c), sibling)
                cp.start()
                passed.append(cp)
        for w in range(n):
            copy(w, 0, sibling, me).wait_recv()
            for j, chip in enumerate(chips):
                copy(w, 4 + j, (*chip, 1 - c), me).wait_recv()
        for cp in first + passed:
            cp.wait_send()
        for cp in mine:
            cp.wait()

    return pl.pallas_call(
        body, in_specs=[ANY] * n, out_specs=[ANY] * n,
        out_shape=[jax.ShapeDtypeStruct((N_DEV,) + s.shape, s.dtype) for s in shards],
        scratch_shapes=[pltpu.SemaphoreType.DMA((7 * n,)), pltpu.SemaphoreType.DMA((7 * n,)),
                        pltpu.SemaphoreType.DMA((n,))],
        name="all_gather_weights")(*shards)


def _sibling_exchange(grads):
    n = len(grads)

    def body(*refs):
        ins, outs = refs[:n], refs[n:2 * n]
        send_sems, recv_sems = refs[2 * n:]
        x, y, c, _ = _place()
        sibling = (x, y, 1 - c)
        copies = []
        for w in range(n):
            for q in range(4):
                copies.append(pltpu.make_async_remote_copy(
                    src_ref=ins[w].at[2 * q + (1 - c)], dst_ref=outs[w].at[q], send_sem=send_sems.at[4 * w + q],
                    recv_sem=recv_sems.at[4 * w + q], device_id=sibling, device_id_type=MESH))
        for cp in copies:
            cp.start()
        for cp in copies:
            cp.wait()

    return pl.pallas_call(
        body, in_specs=[ANY] * n, out_specs=[ANY] * n,
        out_shape=[jax.ShapeDtypeStruct((4,) + g.shape[1:], g.dtype) for g in grads],
        scratch_shapes=[pltpu.SemaphoreType.DMA((4 * n,)), pltpu.SemaphoreType.DMA((4 * n,))],
        name="rs_sibling_exchange")(*grads)


def _chip_sum(grad, recv, core, name, tr=256):
    _, R, C = grad.shape
    tr = _tile(R, tr)

    def body(c_ref, g_ref, r_ref, o_ref):
        o_ref[...] = (g_ref[...].astype(F32) + r_ref[...].astype(F32)).astype(o_ref.dtype)

    grid_spec = pltpu.PrefetchScalarGridSpec(
        num_scalar_prefetch=1, grid=(4, R // tr),
        in_specs=[pl.BlockSpec((None, tr, C), lambda q, i, c_ref: (2 * q + c_ref[0], i, 0)),
                  pl.BlockSpec((None, tr, C), lambda q, i, c_ref: (q, i, 0))],
        out_specs=pl.BlockSpec((None, tr, C), lambda q, i, c_ref: (q, i, 0)))
    return pl.pallas_call(body, grid_spec=grid_spec, out_shape=jax.ShapeDtypeStruct((4, R, C), BF16),
                          compiler_params=_params(("parallel", "parallel")), name=name)(core, grad, recv)


def _chip_exchange(parts):
    n = len(parts)

    def body(*refs):
        ins, outs = refs[:n], refs[n:2 * n]
        send_sems, recv_sems, local_sems = refs[2 * n:]
        x, y, c, chips = _place()
        my_q = 2 * x + y
        mine = [pltpu.make_async_copy(ins[w].at[my_q], outs[w].at[my_q], local_sems.at[w]) for w in range(n)]
        for cp in mine:
            cp.start()
        copies = []
        for w in range(n):
            for j, (px, py) in enumerate(chips):
                copies.append(pltpu.make_async_remote_copy(
                    src_ref=ins[w].at[2 * px + py], dst_ref=outs[w].at[my_q], send_sem=send_sems.at[3 * w + j],
                    recv_sem=recv_sems.at[3 * w + j], device_id=(px, py, c), device_id_type=MESH))
        for cp in copies:
            cp.start()
        for cp in copies:
            cp.wait()
        for cp in mine:
            cp.wait()

    return pl.pallas_call(
        body, in_specs=[ANY] * n, out_specs=[ANY] * n,
        out_shape=[jax.ShapeDtypeStruct(p.shape, p.dtype) for p in parts],
        scratch_shapes=[pltpu.SemaphoreType.DMA((3 * n,)), pltpu.SemaphoreType.DMA((3 * n,)),
                        pltpu.SemaphoreType.DMA((n,))],
        name="rs_chip_exchange")(*parts)


def _all_reduce_small(buf):
    R = buf.shape[0]

    def body(x_ref, o_ref, g_ref, send_sems, recv_sems):
        x, y, c, chips = _place()
        me, sibling = (x, y, c), (x, y, 1 - c)

        def slot(px, py, pc):
            return 4 * px + 2 * py + pc

        def copy(k, block, to, src=None):
            dst = g_ref.at[slot(*block)]
            return pltpu.make_async_remote_copy(src_ref=dst if src is None else src, dst_ref=dst,
                                                send_sem=send_sems.at[k], recv_sem=recv_sems.at[k],
                                                device_id=to, device_id_type=MESH)

        first = [copy(0, me, sibling, src=x_ref)]
        first += [copy(1 + j, me, (*chip, c), src=x_ref) for j, chip in enumerate(chips)]
        for cp in first:
            cp.start()
        g_ref[slot(*me)] = x_ref[...]
        passed = [copy(4 + j, (*chip, c), sibling) for j, chip in enumerate(chips)]
        for j, chip in enumerate(chips):
            copy(1 + j, (*chip, c), me).wait_recv()
            passed[j].start()
        copy(0, sibling, me).wait_recv()
        for j, chip in enumerate(chips):
            copy(4 + j, (*chip, 1 - c), me).wait_recv()
        for cp in first + passed:
            cp.wait_send()
        acc = g_ref[0]
        for s in range(1, N_DEV):
            acc = acc + g_ref[s]
        o_ref[...] = acc

    vm = pl.BlockSpec(memory_space=pltpu.VMEM)
    return pl.pallas_call(
        body, in_specs=[vm], out_specs=vm, out_shape=jax.ShapeDtypeStruct((R, LANE), F32),
        scratch_shapes=[pltpu.VMEM((N_DEV, R, LANE), F32), pltpu.SemaphoreType.DMA((7,)), pltpu.SemaphoreType.DMA((7,))],
        name="all_reduce_small")(buf)


def _pad_cols(a, width):
    return jnp.pad(a, ((0, 0), (0, width - a.shape[1])))


def _local_step(x, target, w_in, conv_w, a_log, dt_bias, delta_norm_w, sinks, rel_bias, w_o, ln1_g, ln1_b,
                w_up_g, w_down, ln2_g, ln2_b):
    S, D = x.shape
    aw = D // 2
    hq, hkv, nh = aw // HEAD_A, aw // HEAD_A // GQA, aw // HEAD_D
    kvw = hkv * HEAD_A
    c_q, c_k, c_v, c_d = 0, aw, aw + kvw, aw + 2 * kvw
    c_ab = c_d + 3 * aw
    c_z = c_ab + 2 * nh
    n_in = c_z + aw
    assert w_in.shape == (D, n_in), (w_in.shape, n_in)
    w_p = jnp.concatenate([w_in[:, :c_ab], w_in[:, c_z:], _pad_cols(w_in[:, c_ab:c_z], LANE)], axis=1)
    p_z, p_ab = c_ab, c_ab + aw
    n_p = p_ab + LANE

    proj = _matmul(x, w_p, NN, name="proj", tn=1152)
    bias = _attn_bias(rel_bias.T)
    attn_out = _attn_fwd(proj, bias, sinks.reshape(-1), hq, 0, c_k // kvw, c_v // kvw)
    conv2 = conv_w.reshape(CONV_W, 3 * aw)
    qkv = _gdn_prep_fwd(proj, conv2, nh, c_d // HEAD_D)
    ab = proj[:, p_ab:]
    al, dt = _pad_cols(a_log, LANE), _pad_cols(dt_bias, LANE)
    gb = _gates_fwd(ab, al, dt, nh)
    u_d, wqk, attn_d, t_d = _gdn_local_fwd(qkv, gb, nh)
    o_d, vn, states = _gdn_scan_fwd(u_d, wqk, attn_d, gb, nh)
    delta_out = _gated_norm_fwd(o_d, proj, delta_norm_w, nh, p_z // HEAD_D)
    mix = jnp.concatenate([attn_out, delta_out], axis=1)
    mixed = _matmul(mix, w_o, NN, name="out_proj")
    h1 = _ln1_fwd(x, mixed, ln1_g, ln1_b)
    u = _matmul(h1, w_up_g, NN, name="mlp_up", b_groups=True)
    mlp = _matmul(u, w_down, NN, name="mlp_down", a_fn=_relu_sq)
    dr2, loss_row, dln2_g, dln2_b = _ln2_loss(h1, mlp, ln2_g, ln2_b, target)

    du = _matmul(dr2, w_down, NT, name="d_mlp_act", epi=_relu_sq_grad, epi_in=(u,))
    dw_down = _matmul(u, dr2, TN, name="dw_down", a_fn=_relu_sq, out_dtype=BF16)
    dw_up = _matmul(h1, du, TN, name="dw_up", out_dtype=BF16, out_groups=N_DEV)
    dh_mlp = _matmul(du, w_up_g, NT, name="d_h1", b_groups=True)
    dr1, dln1_g, dln1_b = _ln1_bwd(x, mixed, ln1_g, dr2, dh_mlp)
    dw_o = _matmul(mix, dr1, TN, name="dw_o", out_dtype=BF16)
    dmix = _matmul(dr1, w_o, NT, name="d_mix")
    dq_a, dk_a, dv_a, dbias, dsink = _attn_bwd(proj, bias, sinks.reshape(-1), attn_out, dmix, hq, 0, c_k // kvw, c_v // kvw)
    drel = _rel_bias_grad(dbias, hq)
    do_d, dz, dnw = _gated_norm_bwd(o_d, proj, delta_norm_w, dmix, nh, p_z // HEAD_D, aw // HEAD_D)
    dvn_s, dw_s, dqd, dkd, dattn_d, dgl = _gdn_scan_bwd(wqk, attn_d, gb, states, vn, do_d, nh)
    dqn, dkn, dvn, dg, dbeta = _gdn_local_bwd(qkv, gb, t_d, u_d, wqk, dvn_s, dw_s, dqd, dkd, dattn_d, dgl, nh)
    dgb = _pad_cols(jnp.concatenate([dg.reshape(nh, S).T, dbeta.reshape(nh, S).T], axis=1), LANE)
    dab, da_log, ddt_bias = _gates_bwd(ab, al, dt, dgb, nh)
    dqkv_d, dconv = _gdn_prep_bwd(proj, conv2, jnp.concatenate([dqn, dkn, dvn], axis=1), nh, c_d // HEAD_D)
    dproj = jnp.concatenate([dq_a, dk_a, dv_a, dqkv_d, dz, dab], axis=1)
    dw_p = _matmul(x, dproj, TN, name="dw_in", tn=640)
    dx_proj = _matmul(dproj, w_p, NT, name="d_x", tk=1920)
    grad_x = _grad_x(dr1, dx_proj)
    dw_in = jnp.concatenate([dw_p[:, :p_z], dw_p[:, p_ab:p_ab + 2 * nh], dw_p[:, p_z:p_ab]], axis=1)

    small = dict(conv_w=dconv, a_log=da_log[:, :nh], dt_bias=ddt_bias[:, :nh], delta_norm_w=dnw,
                 attn_sinks=dsink[:, 0].reshape(1, hq), rel_bias=drel[:, :, 0].T,
                 ln1_g=dln1_g, ln1_b=dln1_b, ln2_g=dln2_g, ln2_b=dln2_b)
    return loss_row, grad_x, dw_in, small, dw_o, dw_up, dw_down


SMALL_ORDER = ("conv_w", "a_log", "dt_bias", "delta_norm_w", "attn_sinks", "rel_bias", "ln1_g", "ln1_b", "ln2_g", "ln2_b")


def _pack_small(loss_row, small):
    parts = [loss_row.reshape(-1)]
    for k in SMALL_ORDER:
        flat = small[k].reshape(-1)
        parts.append(jnp.pad(flat, (0, (-flat.shape[0]) % LANE)))
    flat = jnp.concatenate(parts)
    flat = jnp.pad(flat, (0, (-flat.shape[0]) % (8 * LANE)))
    return flat.reshape(-1, LANE)


def _unpack_small(buf, small_shapes):
    flat = buf.reshape(-1)
    loss = flat[0]
    off = LANE
    out = {}
    for k in SMALL_ORDER:
        n = int(np.prod(small_shapes[k]))
        out[k] = flat[off:off + n].reshape(small_shapes[k])
        off += n + (-n) % LANE
    return loss, out


def kernel(x, w_in, conv_w, a_log, dt_bias, delta_norm_w, attn_sinks, rel_bias, w_o, ln1_g, ln1_b, w_up, w_down, ln2_g, ln2_b, loss_target, m_w_in, m_conv_w, m_a_log, m_dt_bias, m_delta_norm_w, m_attn_sinks, m_rel_bias, m_w_o, m_ln1_g, m_ln1_b, m_w_up, m_w_down, m_ln2_g, m_ln2_b, v_w_in, v_conv_w, v_a_log, v_dt_bias, v_delta_norm_w, v_attn_sinks, v_rel_bias, v_w_o, v_ln1_g, v_ln1_b, v_w_up, v_w_down, v_ln2_g, v_ln2_b):
    S, D = x.shape[1], x.shape[2]
    core = lax.axis_index("c")
    dev = 4 * lax.axis_index("x") + 2 * lax.axis_index("y") + core

    w_in_g, w_o_g, w_up_g, w_down_g = _all_gather(
        [w_in[0].astype(BF16), w_o[0].astype(BF16), w_up[0].astype(BF16), w_down[0].astype(BF16)])
    n_in = N_DEV * w_in.shape[2]
    w_in_full = jnp.transpose(w_in_g, (1, 0, 2)).reshape(D, n_in)
    w_o_full = w_o_g.reshape(D, D)
    w_down_full = w_down_g.reshape(-1, D)

    cw_sh = conv_w.shape[3]
    conv_place = lax.dynamic_update_slice(jnp.zeros((CONV_W, N_DEV * cw_sh), F32), conv_w[0, :, 0, :], (0, dev * cw_sh))
    conv_full = _all_reduce_small(jnp.pad(conv_place.reshape(-1, LANE), ((0, (-conv_place.size // LANE) % 8), (0, 0))))
    conv_full = conv_full[:conv_place.size // LANE].reshape(CONV_W, N_DEV * cw_sh)

    loss_row, grad_x, dw_in, small, dw_o, dw_up, dw_down = _local_step(
        x[0], loss_target[0], w_in_full, conv_full, a_log, dt_bias, delta_norm_w, attn_sinks, rel_bias, w_o_full,
        ln1_g, ln1_b, w_up_g, w_down_full, ln2_g, ln2_b)

    dw_in_g = jnp.transpose(dw_in.reshape(D, N_DEV, -1), (1, 0, 2)).astype(BF16)
    grads = [dw_in_g, dw_o.reshape(N_DEV, -1, D), dw_up, dw_down.reshape(N_DEV, -1, D)]
    recv = _sibling_exchange(grads)
    core_arr = core.reshape(1).astype(jnp.int32)
    parts = [_chip_sum(gr, rc, core_arr, "rs_chip_sum_%d" % i) for i, (gr, rc) in enumerate(zip(grads, recv))]
    terms = _chip_exchange(parts)
    big = {}
    for name, t, w, m, v in (("w_in", terms[0], w_in, m_w_in, v_w_in), ("w_o", terms[1], w_o, m_w_o, v_w_o),
                             ("w_up", terms[2], w_up, m_w_up, v_w_up), ("w_down", terms[3], w_down, m_w_down, v_w_down)):
        outs = _adamw_big(t, w[0], m[0], v[0], "adamw_" + name)
        big[name] = [o[None] for o in outs]

    small_shapes = {k: v.shape for k, v in small.items()}
    loss, small = _unpack_small(_all_reduce_small(_pack_small(loss_row, small)), small_shapes)
    small["conv_w"] = lax.dynamic_slice(small["conv_w"], (0, dev * cw_sh), (CONV_W, cw_sh))
    small["rel_bias"] = small["rel_bias"].reshape(rel_bias.shape)
    p2 = dict(conv_w=(conv_w, m_conv_w, v_conv_w), a_log=(a_log, m_a_log, v_a_log), dt_bias=(dt_bias, m_dt_bias, v_dt_bias),
              delta_norm_w=(delta_norm_w, m_delta_norm_w, v_delta_norm_w), attn_sinks=(attn_sinks, m_attn_sinks, v_attn_sinks),
              rel_bias=(rel_bias, m_rel_bias, v_rel_bias), ln1_g=(ln1_g, m_ln1_g, v_ln1_g), ln1_b=(ln1_b, m_ln1_b, v_ln1_b),
              ln2_g=(ln2_g, m_ln2_g, v_ln2_g), ln2_b=(ln2_b, m_ln2_b, v_ln2_b))
    two_d = lambda a: a.reshape(-1, a.shape[-1])
    ws = [two_d(p2[k][0]) for k in SMALL_ORDER]
    gs = [two_d(small[k]) for k in SMALL_ORDER]
    ms = [two_d(p2[k][1]) for k in SMALL_ORDER]
    vs = [two_d(p2[k][2]) for k in SMALL_ORDER]
    ds, nms, nvs = _adamw_small(ws, gs, ms, vs)
    res = {}
    for i, k in enumerate(SMALL_ORDER):
        shp = p2[k][0].shape
        res[k] = [gs[i].reshape(shp), ds[i].reshape(shp), nms[i].reshape(shp), nvs[i].reshape(shp)]
    res.update(big)
    order = ("w_in", "conv_w", "a_log", "dt_bias", "delta_norm_w", "attn_sinks", "rel_bias", "w_o", "ln1_g", "ln1_b",
             "w_up", "w_down", "ln2_g", "ln2_b")
    return (loss, grad_x[None], *[res[k][0] for k in order], *[res[k][1] for k in order],
            *[res[k][2] for k in order], *[res[k][3] for k in order])
```

```python
import functools
import math

import numpy as np
import jax
import jax.numpy as jnp
from jax import lax
from jax.experimental import pallas as pl
from jax.experimental.pallas import tpu as pltpu

F32 = jnp.float32
BF16 = jnp.bfloat16
HIGHEST = lax.Precision.HIGHEST

N_DEV = 8
HEAD_A = 64
GQA = 4
BLK = 128
N_BUCKETS = 32
MAX_DISTANCE = 128
HEAD_D = 128
CONV_W = 4
CHUNK = 64
NEG_INF = -1e30
LN_EPS = 1e-5
RMS_EPS = 1e-6
DN_ALPHA = 2.0 ** 0.25
ADAM_LR, ADAM_B1, ADAM_B2, ADAM_EPS, ADAM_WD, ADAM_STEP = 0.001, 0.9, 0.999, 1e-08, 0.01, 10

LANE = 128
VMEM_LIMIT = 56 * 1024 * 1024

NN = ((1,), (0,))
NT = ((1,), (1,))
TN = ((0,), (0,))


def _dot(a, b, dims, prec=None):
    return lax.dot_general(a, b, (dims, ((), ())), precision=prec, preferred_element_type=F32)


def _tile(dim, pref):
    if dim <= pref:
        return dim
    t = (pref // LANE) * LANE
    while t > LANE and dim % t:
        t -= LANE
    assert dim % t == 0, (dim, pref)
    return t


def _params(sem):
    return pltpu.CompilerParams(dimension_semantics=sem, vmem_limit_bytes=VMEM_LIMIT)


def _matmul(a, b, dims, *, name, out_dtype=F32, tm=1024, tn=1024, tk=2048, a_fn=None, epi=None, epi_in=(),
            b_groups=None, out_groups=None):
    (ca,), (cb,) = dims
    M, K = a.shape[1 - ca], a.shape[ca]
    if b_groups:
        G, R, C = b.shape
        bshape = (R, G * C)
    else:
        bshape = b.shape
    N = bshape[1 - cb]
    assert bshape[cb] == K, (a.shape, b.shape, dims)
    tm, tk = _tile(M, tm), _tile(K, tk)
    if b_groups:
        lim = C if cb == 0 else tn
        tn = _tile(N, min(tn, lim))
        if cb == 1:
            tk = _tile(K, min(tk, C))
    else:
        tn = _tile(N, tn)
    if out_groups:
        tn = _tile(N, min(tn, N // out_groups))
    nk = K // tk

    def body(*refs):
        a_ref, b_ref = refs[0], refs[1]
        e_refs = refs[2:2 + len(epi_in)]
        o_ref = refs[2 + len(epi_in)]
        acc_ref = refs[3 + len(epi_in)] if nk > 1 else None
        k = pl.program_id(2)
        av = a_ref[...]
        if a_fn is not None:
            av = a_fn(av)
        prod = _dot(av.astype(BF16), b_ref[...].astype(BF16), dims)

        def finish(r):
            if epi is not None:
                r = epi(r, *[e[...] for e in e_refs])
            o_ref[...] = r.astype(out_dtype)

        if nk == 1:
            finish(prod)
            return

        @pl.when(k == 0)
        def _():
            acc_ref[...] = prod

        @pl.when(k > 0)
        def _():
            acc_ref[...] += prod

        @pl.when(k == nk - 1)
        def _():
            finish(acc_ref[...])

    a_spec = (pl.BlockSpec((tm, tk), lambda i, j, k: (i, k)) if ca == 1
              else pl.BlockSpec((tk, tm), lambda i, j, k: (k, i)))
    if b_groups:
        if cb == 0:
            per = C // tn
            b_spec = pl.BlockSpec((None, tk, tn), lambda i, j, k: (j // per, k, j % per))
        else:
            per = C // tk
            b_spec = pl.BlockSpec((None, tn, tk), lambda i, j, k: (k // per, j, k % per))
    else:
        b_spec = (pl.BlockSpec((tk, tn), lambda i, j, k: (k, j)) if cb == 0
                  else pl.BlockSpec((tn, tk), lambda i, j, k: (j, k)))
    e_specs = [pl.BlockSpec((tm, tn), lambda i, j, k: (i, j)) for _ in epi_in]
    if out_groups:
        per_o = (N // out_groups) // tn
        o_spec = pl.BlockSpec((None, tm, tn), lambda i, j, k: (j // per_o, i, j % per_o))
        o_shape = jax.ShapeDtypeStruct((out_groups, M, N // out_groups), out_dtype)
    else:
        o_spec = pl.BlockSpec((tm, tn), lambda i, j, k: (i, j))
        o_shape = jax.ShapeDtypeStruct((M, N), out_dtype)
    return pl.pallas_call(
        body, grid=(M // tm, N // tn, nk), in_specs=[a_spec, b_spec] + e_specs, out_specs=o_spec,
        out_shape=o_shape, scratch_shapes=[pltpu.VMEM((tm, tn), F32)] if nk > 1 else [],
        compiler_params=_params(("parallel", "parallel", "arbitrary")), name=name)(a, b, *epi_in)


def _relu_sq(u):
    r = jnp.maximum(u, 0.0)
    return r * r


def _relu_sq_grad(acc, u):
    return acc * (2.0 * jnp.maximum(u, 0.0))


def _ln_stats(r):
    mu = jnp.mean(r, axis=-1, keepdims=True)
    xc = r - mu
    var = jnp.mean(xc * xc, axis=-1, keepdims=True)
    rstd = lax.rsqrt(var + LN_EPS)
    return xc * rstd, rstd


def _ln_bwd(dy, xhat, rstd, g):
    dxh = dy * g
    m1 = jnp.mean(dxh, axis=-1, keepdims=True)
    m2 = jnp.mean(dxh * xhat, axis=-1, keepdims=True)
    return rstd * (dxh - m1 - xhat * m2)


def _row_call(body, ins, row_ins, outs, acc_outs, name, tr=256):
    S = ins[0].shape[0]
    tr = min(tr, S)
    n_in, n_row, n_out = len(ins), len(row_ins), len(outs)

    def wrapped(*refs):
        i = pl.program_id(0)
        acc_refs = refs[n_in + n_row + n_out:]

        @pl.when(i == 0)
        def _():
            for r in acc_refs:
                r[...] = jnp.zeros_like(r)

        body(*refs)

    in_specs = [pl.BlockSpec((tr, a.shape[1]), lambda i: (i, 0)) for a in ins]
    in_specs += [pl.BlockSpec(a.shape, lambda i: (0, 0)) for a in row_ins]
    out_specs = [pl.BlockSpec((tr, s.shape[1]), lambda i: (i, 0)) for s in outs]
    out_specs += [pl.BlockSpec(s.shape, lambda i: (0, 0)) for s in acc_outs]
    return pl.pallas_call(wrapped, grid=(S // tr,), in_specs=in_specs, out_specs=out_specs,
                          out_shape=list(outs) + list(acc_outs),
                          compiler_params=_params(("arbitrary",)), name=name)(*ins, *row_ins)


def _ln1_fwd(x, mixed, g, b):
    def body(x_ref, m_ref, g_ref, b_ref, h_ref):
        xhat, _ = _ln_stats(DN_ALPHA * x_ref[...] + m_ref[...])
        h_ref[...] = xhat * g_ref[...] + b_ref[...]
    return _row_call(body, [x, mixed], [g, b], [jax.ShapeDtypeStruct(x.shape, F32)], [], "ln1_fwd")[0]


def _ln2_loss(h1, mlp, g, b, target):
    S, D = h1.shape
    sds = jax.ShapeDtypeStruct

    def body(h_ref, m_ref, t_ref, g_ref, b_ref, dr_ref, loss_ref, dg_ref, db_ref):
        xhat, rstd = _ln_stats(DN_ALPHA * h_ref[...] + m_ref[...])
        gv = g_ref[...]
        err = xhat * gv + b_ref[...] - t_ref[...]
        loss_ref[...] += jnp.sum(jnp.sum(err * err, axis=0, keepdims=True), axis=1, keepdims=True) * (0.5 / D)
        dy = err * (1.0 / D)
        dg_ref[...] += jnp.sum(dy * xhat, axis=0, keepdims=True)
        db_ref[...] += jnp.sum(dy, axis=0, keepdims=True)
        dr_ref[...] = _ln_bwd(dy, xhat, rstd, gv)

    return _row_call(body, [h1, mlp, target], [g, b], [sds((S, D), F32)],
                     [sds((1, LANE), F32), sds((1, D), F32), sds((1, D), F32)], "ln2_loss")


def _ln1_bwd(x, mixed, g, dr2, dh_mlp):
    S, D = x.shape
    sds = jax.ShapeDtypeStruct

    def body(x_ref, m_ref, dr2_ref, dh_ref, g_ref, dr_ref, dg_ref, db_ref):
        xhat, rstd = _ln_stats(DN_ALPHA * x_ref[...] + m_ref[...])
        dy = DN_ALPHA * dr2_ref[...] + dh_ref[...]
        dg_ref[...] += jnp.sum(dy * xhat, axis=0, keepdims=True)
        db_ref[...] += jnp.sum(dy, axis=0, keepdims=True)
        dr_ref[...] = _ln_bwd(dy, xhat, rstd, g_ref[...])

    return _row_call(body, [x, mixed, dr2, dh_mlp], [g], [sds((S, D), F32)],
                     [sds((1, D), F32), sds((1, D), F32)], "ln1_bwd")


def _grad_x(dr1, dx_proj):
    def body(a_ref, b_ref, o_ref):
        o_ref[...] = DN_ALPHA * a_ref[...] + b_ref[...]
    return _row_call(body, [dr1, dx_proj], [], [jax.ShapeDtypeStruct(dr1.shape, F32)], [], "grad_x")[0]


def _bucket_table():
    qi = np.arange(BLK, dtype=np.int32)[:, None]
    kj = np.arange(2 * BLK, dtype=np.int32)[None, :]
    dist = qi + BLK - kj
    n = np.maximum(dist, 0)
    max_exact = N_BUCKETS // 2
    nf = np.maximum(n, 1).astype(np.float32)
    large = max_exact + (np.log(nf / np.float32(max_exact)) / np.float32(math.log(MAX_DISTANCE / max_exact))
                         * np.float32(N_BUCKETS - max_exact)).astype(np.int32)
    large = np.minimum(large, N_BUCKETS - 1)
    bucket = np.where(n < max_exact, n, large)
    return np.where((dist >= 0) & (dist < BLK), bucket, -1).astype(np.int32)


def _attn_bias(rel_bias_t):
    hq = rel_bias_t.shape[0]
    bucket = jnp.asarray(_bucket_table())

    def body(rb_ref, bk_ref, o_ref):
        h = pl.program_id(0)
        bk = bk_ref[...]
        acc = jnp.zeros((BLK, 2 * BLK), F32)
        for b in range(N_BUCKETS):
            acc = jnp.where(bk == b, rb_ref[h, b], acc)
        o_ref[...] = acc

    return pl.pallas_call(
        body, grid=(hq,),
        in_specs=[pl.BlockSpec(memory_space=pltpu.SMEM), pl.BlockSpec((BLK, 2 * BLK), lambda h: (0, 0))],
        out_specs=pl.BlockSpec((BLK, 2 * BLK), lambda h: (h, 0)),
        out_shape=jax.ShapeDtypeStruct((hq * BLK, 2 * BLK), F32),
        compiler_params=_params(("arbitrary",)), name="attn_bias")(rel_bias_t, bucket)


def _attn_probs(q, kc, kp, bias, sink, mask_c, mask_p):
    lc = jnp.where(mask_c, _dot(q, kc, NT) + bias[:, BLK:], NEG_INF)
    lp = jnp.where(mask_p, _dot(q, kp, NT) + bias[:, :BLK], NEG_INF)
    m = jnp.maximum(jnp.maximum(jnp.max(lc, axis=1, keepdims=True), jnp.max(lp, axis=1, keepdims=True)), sink)
    pc, pp, ps = jnp.exp(lc - m), jnp.exp(lp - m), jnp.exp(sink - m)
    inv = 1.0 / (jnp.sum(pc, axis=1, keepdims=True) + jnp.sum(pp, axis=1, keepdims=True) + ps)
    return pc, pp, ps, inv


def _attn_masks(n):
    qi = lax.broadcasted_iota(jnp.int32, (BLK, BLK), 0)
    kj = lax.broadcasted_iota(jnp.int32, (BLK, BLK), 1)
    return kj <= qi, (kj > qi) & (n > 0)


def _attn_fwd(proj, bias, sinks, hq, q_blk, k_blk, v_blk):
    S = proj.shape[0]
    hkv = hq // GQA
    wq, wk = hq * HEAD_A, hkv * HEAD_A

    def body(q_ref, k_ref, v_ref, bias_ref, sink_ref, o_ref):
        n = pl.program_id(0)
        cur = pl.multiple_of(n * BLK, BLK)
        prev = pl.multiple_of(jnp.maximum(n - 1, 0) * BLK, BLK)
        mask_c, mask_p = _attn_masks(n)
        for h4 in range(hkv):
            cs = slice(h4 * HEAD_A, (h4 + 1) * HEAD_A)
            kc, kp = k_ref[pl.ds(cur, BLK), cs].astype(BF16), k_ref[pl.ds(prev, BLK), cs].astype(BF16)
            vc, vp = v_ref[pl.ds(cur, BLK), cs].astype(BF16), v_ref[pl.ds(prev, BLK), cs].astype(BF16)
            for g in range(GQA):
                h = h4 * GQA + g
                hs = slice(h * HEAD_A, (h + 1) * HEAD_A)
                q = (q_ref[:, hs] * (HEAD_A ** -0.5)).astype(BF16)
                pc, pp, _, inv = _attn_probs(q, kc, kp, bias_ref[h * BLK:(h + 1) * BLK, :], sink_ref[h], mask_c, mask_p)
                o_ref[:, hs] = (_dot(pc.astype(BF16), vc, NN) + _dot(pp.astype(BF16), vp, NN)) * inv

    return pl.pallas_call(
        body, grid=(S // BLK,),
        in_specs=[pl.BlockSpec((BLK, wq), lambda n: (n, q_blk)), pl.BlockSpec((S, wk), lambda n: (0, k_blk)),
                  pl.BlockSpec((S, wk), lambda n: (0, v_blk)), pl.BlockSpec((hq * BLK, 2 * BLK), lambda n: (0, 0)),
                  pl.BlockSpec(memory_space=pltpu.SMEM)],
        out_specs=pl.BlockSpec((BLK, wq), lambda n: (n, 0)),
        out_shape=jax.ShapeDtypeStruct((S, wq), F32),
        compiler_params=_params(("arbitrary",)), name="attn_fwd")(proj, proj, proj, bias, sinks)


def _attn_bwd(proj, bias, sinks, out, dmix, hq, q_blk, k_blk, v_blk):
    S = proj.shape[0]
    hkv = hq // GQA
    wq, wk = hq * HEAD_A, hkv * HEAD_A
    sds = jax.ShapeDtypeStruct

    def body(q_ref, k_ref, v_ref, bias_ref, sink_ref, o_ref, do_ref, dq_ref, dk_ref, dv_ref, dbias_ref, dsink_ref):
        n = pl.program_id(0)

        @pl.when(n == 0)
        def _():
            dk_ref[...] = jnp.zeros_like(dk_ref)
            dv_ref[...] = jnp.zeros_like(dv_ref)
            dbias_ref[...] = jnp.zeros_like(dbias_ref)
            dsink_ref[...] = jnp.zeros_like(dsink_ref)

        cur = pl.multiple_of(n * BLK, BLK)
        prev = pl.multiple_of(jnp.maximum(n - 1, 0) * BLK, BLK)
        mask_c, mask_p = _attn_masks(n)
        for h4 in range(hkv):
            cs = slice(h4 * HEAD_A, (h4 + 1) * HEAD_A)
            kc, kp = k_ref[pl.ds(cur, BLK), cs].astype(BF16), k_ref[pl.ds(prev, BLK), cs].astype(BF16)
            vc, vp = v_ref[pl.ds(cur, BLK), cs].astype(BF16), v_ref[pl.ds(prev, BLK), cs].astype(BF16)
            dkc = jnp.zeros((BLK, HEAD_A), F32)
            dkp = jnp.zeros((BLK, HEAD_A), F32)
            dvc = jnp.zeros((BLK, HEAD_A), F32)
            dvp = jnp.zeros((BLK, HEAD_A), F32)
            for g in range(GQA):
                h = h4 * GQA + g
                hs = slice(h * HEAD_A, (h + 1) * HEAD_A)
                rows = slice(h * BLK, (h + 1) * BLK)
                q = (q_ref[:, hs] * (HEAD_A ** -0.5)).astype(BF16)
                pc, pp, ps, inv = _attn_probs(q, kc, kp, bias_ref[rows, :], sink_ref[h], mask_c, mask_p)
                pc, pp, ps = pc * inv, pp * inv, ps * inv
                do = do_ref[:, hs]
                delta = jnp.sum(do * o_ref[:, hs], axis=1, keepdims=True)
                dob = do.astype(BF16)
                dsc = pc * (_dot(dob, vc, NT) - delta)
                dsp = pp * (_dot(dob, vp, NT) - delta)
                dsink_ref[h:h + 1, :] += jnp.broadcast_to(jnp.sum(-ps * delta, axis=0, keepdims=True), (1, LANE))
                dbias_ref[rows, BLK:] += dsc
                dbias_ref[rows, :BLK] += dsp
                dscb, dspb = dsc.astype(BF16), dsp.astype(BF16)
                dq_ref[:, hs] = (_dot(dscb, kc, NN) + _dot(dspb, kp, NN)) * (HEAD_A ** -0.5)
                dkc += _dot(dscb, q, TN)
                dkp += _dot(dspb, q, TN)
                dvc += _dot(pc.astype(BF16), dob, TN)
                dvp += _dot(pp.astype(BF16), dob, TN)
            dk_ref[pl.ds(cur, BLK), cs] += dkc
            dk_ref[pl.ds(prev, BLK), cs] += dkp
            dv_ref[pl.ds(cur, BLK), cs] += dvc
            dv_ref[pl.ds(prev, BLK), cs] += dvp

    return pl.pallas_call(
        body, grid=(S // BLK,),
        in_specs=[pl.BlockSpec((BLK, wq), lambda n: (n, q_blk)), pl.BlockSpec((S, wk), lambda n: (0, k_blk)),
                  pl.BlockSpec((S, wk), lambda n: (0, v_blk)), pl.BlockSpec((hq * BLK, 2 * BLK), lambda n: (0, 0)),
                  pl.BlockSpec(memory_space=pltpu.SMEM),
                  pl.BlockSpec((BLK, wq), lambda n: (n, 0)), pl.BlockSpec((BLK, wq), lambda n: (n, 0))],
        out_specs=[pl.BlockSpec((BLK, wq), lambda n: (n, 0)), pl.BlockSpec((S, wk), lambda n: (0, 0)),
                   pl.BlockSpec((S, wk), lambda n: (0, 0)), pl.BlockSpec((hq * BLK, 2 * BLK), lambda n: (0, 0)),
                   pl.BlockSpec((hq, LANE), lambda n: (0, 0))],
        out_shape=[sds((S, wq), F32), sds((S, wk), F32), sds((S, wk), F32), sds((hq * BLK, 2 * BLK), F32),
                   sds((hq, LANE), F32)],
        compiler_params=_params(("arbitrary",)), name="attn_bwd")(proj, proj, proj, bias, sinks, out, dmix)


def _rel_bias_grad(dbias, hq):
    bucket = jnp.asarray(_bucket_table())

    def body(d_ref, bk_ref, o_ref):
        d = d_ref[...]
        bk = bk_ref[...]
        rows = [jnp.sum(jnp.where(bk == b, d, 0.0), axis=0, keepdims=True) for b in range(N_BUCKETS)]
        tot = jnp.sum(jnp.concatenate(rows, axis=0), axis=1, keepdims=True)
        o_ref[...] = jnp.broadcast_to(tot, (N_BUCKETS, LANE))

    return pl.pallas_call(
        body, grid=(hq,),
        in_specs=[pl.BlockSpec((BLK, 2 * BLK), lambda h: (h, 0)), pl.BlockSpec((BLK, 2 * BLK), lambda h: (0, 0))],
        out_specs=pl.BlockSpec((None, N_BUCKETS, LANE), lambda h: (h, 0, 0)),
        out_shape=jax.ShapeDtypeStruct((hq, N_BUCKETS, LANE), F32),
        compiler_params=_params(("arbitrary",)), name="rel_bias_grad")(dbias, bucket)


def _sigmoid(x):
    return 1.0 / (1.0 + jnp.exp(-x))


def _shift_rows(x, s):
    n = x.shape[0]
    row = lax.broadcasted_iota(jnp.int32, x.shape, 0)
    if s > 0:
        return jnp.where(row >= s, pltpu.roll(x, s, 0), 0.0)
    return jnp.where(row < n + s, pltpu.roll(x, n + s, 0), 0.0)


def _conv_silu_norm(xv, w, j, nh):
    c = w[CONV_W - 1:CONV_W, :] * xv
    for s in range(1, CONV_W):
        c = c + w[CONV_W - 1 - s:CONV_W - s, :] * _shift_rows(xv, s)
    sg = _sigmoid(c)
    a = c * sg
    r = lax.rsqrt(jnp.sum(a * a, axis=1, keepdims=True) + RMS_EPS)
    scale = jnp.where(j < nh, HEAD_D ** -0.5, 1.0)
    is_norm = j < 2 * nh
    y = jnp.where(is_norm, a * (r * scale), a)
    return c, sg, a, r, scale, is_norm, y


def _gdn_prep_fwd(proj, conv_w, nh, blk0):
    S = proj.shape[0]

    def body(x_ref, w_ref, o_ref):
        j = pl.program_id(0)
        o_ref[...] = _conv_silu_norm(x_ref[...], w_ref[...], j, nh)[-1]

    return pl.pallas_call(
        body, grid=(3 * nh,),
        in_specs=[pl.BlockSpec((S, HEAD_D), lambda j: (0, blk0 + j)), pl.BlockSpec((CONV_W, HEAD_D), lambda j: (0, j))],
        out_specs=pl.BlockSpec((S, HEAD_D), lambda j: (0, j)),
        out_shape=jax.ShapeDtypeStruct((S, 3 * nh * HEAD_D), F32),
        compiler_params=_params(("parallel",)), name="gdn_prep_fwd")(proj, conv_w)


def _gdn_prep_bwd(proj, conv_w, dqkv, nh, blk0):
    S = proj.shape[0]
    sds = jax.ShapeDtypeStruct

    def body(x_ref, w_ref, dy_ref, dx_ref, dw_ref):
        j = pl.program_id(0)
        xv, w = x_ref[...], w_ref[...]
        c, sg, a, r, scale, is_norm, _ = _conv_silu_norm(xv, w, j, nh)
        dy = dy_ref[...]
        rs = r * scale
        da_n = rs * dy - a * (r * r * rs) * jnp.sum(dy * a, axis=1, keepdims=True)
        da = jnp.where(is_norm, da_n, dy)
        dc = da * (sg * (1.0 + c * (1.0 - sg)))
        dx = w[CONV_W - 1:CONV_W, :] * dc
        dws = [jnp.sum(dc * xv, axis=0, keepdims=True)]
        for s in range(1, CONV_W):
            dx = dx + w[CONV_W - 1 - s:CONV_W - s, :] * _shift_rows(dc, -s)
            dws.insert(0, jnp.sum(dc * _shift_rows(xv, s), axis=0, keepdims=True))
        dx_ref[...] = dx
        dw_ref[...] = jnp.concatenate(dws, axis=0)

    return pl.pallas_call(
        body, grid=(3 * nh,),
        in_specs=[pl.BlockSpec((S, HEAD_D), lambda j: (0, blk0 + j)), pl.BlockSpec((CONV_W, HEAD_D), lambda j: (0, j)),
                  pl.BlockSpec((S, HEAD_D), lambda j: (0, j))],
        out_specs=[pl.BlockSpec((S, HEAD_D), lambda j: (0, j)), pl.BlockSpec((CONV_W, HEAD_D), lambda j: (0, j))],
        out_shape=[sds((S, 3 * nh * HEAD_D), F32), sds((CONV_W, 3 * nh * HEAD_D), F32)],
        compiler_params=_params(("parallel",)), name="gdn_prep_bwd")(proj, conv_w, dqkv)


def _softplus(x):
    return jnp.maximum(x, 0.0) + jnp.log(1.0 + jnp.exp(-jnp.abs(x)))


def _gates_fwd(ab, al, dt, nh):
    S = ab.shape[0]

    def body(ab_ref, al_ref, dt_ref, o_ref):
        v = ab_ref[...]
        lane = lax.broadcasted_iota(jnp.int32, v.shape, 1)
        g = -jnp.exp(al_ref[...]) * _softplus(v + dt_ref[...])
        o_ref[...] = jnp.where(lane < nh, g, jnp.where(lane < 2 * nh, _sigmoid(v), 0.0))

    row = pl.BlockSpec((1, LANE), lambda i: (0, 0))
    full = pl.BlockSpec((S, LANE), lambda i: (0, 0))
    return pl.pallas_call(body, grid=(1,), in_specs=[full, row, row], out_specs=full,
                          out_shape=jax.ShapeDtypeStruct((S, LANE), F32),
                          compiler_params=_params(("arbitrary",)), name="gates_fwd")(ab, al, dt)


def _gates_bwd(ab, al, dt, dgb, nh):
    S = ab.shape[0]
    sds = jax.ShapeDtypeStruct

    def body(ab_ref, al_ref, dt_ref, d_ref, dab_ref, dal_ref, ddt_ref):
        v, d = ab_ref[...], d_ref[...]
        lane = lax.broadcasted_iota(jnp.int32, v.shape, 1)
        is_a = lane < nh
        z = v + dt_ref[...]
        dsp = jnp.where(is_a, d * (-jnp.exp(al_ref[...])), 0.0)
        dz = dsp * _sigmoid(z)
        beta = _sigmoid(v)
        dab_ref[...] = jnp.where(is_a, dz, jnp.where(lane < 2 * nh, d * beta * (1.0 - beta), 0.0))
        dal_ref[...] = jnp.sum(dsp * _softplus(z), axis=0, keepdims=True)
        ddt_ref[...] = jnp.sum(dz, axis=0, keepdims=True)

    row = pl.BlockSpec((1, LANE), lambda i: (0, 0))
    full = pl.BlockSpec((S, LANE), lambda i: (0, 0))
    return pl.pallas_call(body, grid=(1,), in_specs=[full, row, row, full], out_specs=[full, row, row],
                          out_shape=[sds((S, LANE), F32), sds((1, LANE), F32), sds((1, LANE), F32)],
                          compiler_params=_params(("arbitrary",)), name="gates_bwd")(ab, al, dt, dgb)


def _col_of(tile, h):
    lane = lax.broadcasted_iota(jnp.int32, tile.shape, 1)
    return jnp.sum(jnp.where(lane == h, tile, 0.0), axis=1, keepdims=True)


def _to_row(col, eye):
    return jnp.sum(jnp.where(eye, col, 0.0), axis=0, keepdims=True)


def _to_col(row, eye):
    return jnp.sum(jnp.where(eye, row, 0.0), axis=1, keepdims=True)


def _split(a):
    hi = a.astype(BF16)
    return hi, (a - hi.astype(F32)).astype(BF16)


def _gdot(a, b, dims):
    ah, al = _split(a)
    bh, bl = _split(b)
    return _dot(ah, bh, dims) + (_dot(ah, bl, dims) + _dot(al, bh, dims))


def _chunk_local(q, k, v, gcol, bcol, T=None):
    C = CHUNK
    row = lax.broadcasted_iota(jnp.int32, (C, C), 0)
    col = lax.broadcasted_iota(jnp.int32, (C, C), 1)
    tril, strict, eye = col <= row, col < row, col == row
    grow = _to_row(gcol, eye)
    G_row = jnp.sum(jnp.where(row <= col, gcol, 0.0), axis=0, keepdims=True)
    G_col = jnp.sum(jnp.where(tril, grow, 0.0), axis=1, keepdims=True)
    decay = jnp.exp(jnp.where(tril, G_col - G_row, NEG_INF))
    G_last = G_col[C - 1:C, :]
    eG = jnp.exp(G_col)
    eGr = jnp.exp(G_last - G_col)
    gl = jnp.exp(G_last)
    kb = k * bcol
    A = jnp.where(strict, _gdot(kb, k, NT) * decay, 0.0)
    attn = _gdot(q, k, NT) * decay
    out = dict(strict=strict, eye=eye, row=row, col=col, decay=decay, eG=eG, eGr=eGr, gl=gl, kb=kb, A=A,
               rhs_k=kb * eG, attn=attn, q_dec=q * eG, k_dec=k * eGr)
    if T is None:
        T = jnp.where(eye, 1.0, 0.0) - A
        P = A
        for _ in range(int(math.log2(C)) - 1):
            P = _gdot(P, P, NN)
            T = T + _gdot(T, P, NN)
        out.update(T=T, u=_gdot(T, v * bcol, NN), w=_gdot(T, out["rhs_k"], NN))
    return out


GDN_ROWS = 256
WQK = 3 * CHUNK


def _gdn_local_fwd(qkv, gb, nh):
    S = qkv.shape[0]
    nc = S // CHUNK
    rb = min(GDN_ROWS, S)
    cpb = rb // CHUNK
    sds = jax.ShapeDtypeStruct

    def body(q_ref, k_ref, v_ref, gb_ref, u_ref, wqk_ref, attn_ref, t_ref):
        h = pl.program_id(0)
        for ci in range(cpb):
            rows = slice(ci * CHUNK, (ci + 1) * CHUNK)
            gbt = gb_ref[rows, :]
            L = _chunk_local(q_ref[rows, :], k_ref[rows, :], v_ref[rows, :], _col_of(gbt, h), _col_of(gbt, nh + h))
            u_ref[rows, :] = L["u"]
            base = ci * WQK
            wqk_ref[base:base + CHUNK, :] = L["w"]
            wqk_ref[base + CHUNK:base + 2 * CHUNK, :] = L["q_dec"]
            wqk_ref[base + 2 * CHUNK:base + WQK, :] = L["k_dec"]
            attn_ref[ci] = L["attn"]
            t_ref[ci] = L["T"]

    cc = pl.BlockSpec((None, cpb, CHUNK, CHUNK), lambda h, i: (h, i, 0, 0))
    return pl.pallas_call(
        body, grid=(nh, S // rb),
        in_specs=[pl.BlockSpec((rb, HEAD_D), lambda h, i: (i, h)), pl.BlockSpec((rb, HEAD_D), lambda h, i: (i, nh + h)),
                  pl.BlockSpec((rb, HEAD_D), lambda h, i: (i, 2 * nh + h)), pl.BlockSpec((rb, LANE), lambda h, i: (i, 0))],
        out_specs=[pl.BlockSpec((rb, HEAD_D), lambda h, i: (i, h)),
                   pl.BlockSpec((None, 3 * rb, HEAD_D), lambda h, i: (h, i, 0)), cc, cc],
        out_shape=[sds((S, nh * HEAD_D), F32), sds((nh, 3 * S, HEAD_D), F32), sds((nh, nc, CHUNK, CHUNK), F32),
                   sds((nh, nc, CHUNK, CHUNK), F32)],
        compiler_params=_params(("parallel", "parallel")), name="gdn_local_fwd")(qkv, qkv, qkv, gb)


def _gdn_scan_fwd(u, wqk, attn, gb, nh):
    S = u.shape[0]
    nc = S // CHUNK
    rb = min(GDN_ROWS, S)
    cpb = rb // CHUNK
    sds = jax.ShapeDtypeStruct

    def body(u_ref, wqk_ref, attn_ref, gb_ref, o_ref, vn_ref, st_ref, s_ref):
        @pl.when(pl.program_id(0) == 0)
        def _():
            s_ref[...] = jnp.zeros_like(s_ref)

        for ci in range(cpb):
            rows = slice(ci * CHUNK, (ci + 1) * CHUNK)
            glv = jnp.exp(jnp.sum(gb_ref[rows, :], axis=0, keepdims=True))
            base = ci * WQK
            for h in range(nh):
                cols = slice(h * HEAD_D, (h + 1) * HEAD_D)
                state = s_ref[h]
                st_ref[h, ci] = state
                r = _gdot(wqk_ref[h, base:base + 2 * CHUNK, :], state, NN)
                vb = u_ref[rows, cols] - r[:CHUNK]
                o_ref[rows, cols] = r[CHUNK:] + _gdot(attn_ref[h, ci], vb, NN)
                vn_ref[rows, cols] = vb
                s_ref[h] = state * glv[:, h:h + 1] + _gdot(wqk_ref[h, base + 2 * CHUNK:base + WQK, :], vb, TN)

    return pl.pallas_call(
        body, grid=(S // rb,),
        in_specs=[pl.BlockSpec((rb, nh * HEAD_D), lambda i: (i, 0)), pl.BlockSpec((nh, 3 * rb, HEAD_D), lambda i: (0, i, 0)),
                  pl.BlockSpec((nh, cpb, CHUNK, CHUNK), lambda i: (0, i, 0, 0)), pl.BlockSpec((rb, LANE), lambda i: (i, 0))],
        out_specs=[pl.BlockSpec((rb, nh * HEAD_D), lambda i: (i, 0)), pl.BlockSpec((rb, nh * HEAD_D), lambda i: (i, 0)),
                   pl.BlockSpec((nh, cpb, HEAD_D, HEAD_D), lambda i: (0, i, 0, 0))],
        out_shape=[sds((S, nh * HEAD_D), F32), sds((S, nh * HEAD_D), F32), sds((nh, nc, HEAD_D, HEAD_D), F32)],
        scratch_shapes=[pltpu.VMEM((nh, HEAD_D, HEAD_D), F32)],
        compiler_params=_params(("arbitrary",)), name="gdn_scan_fwd")(u, wqk, attn, gb)


def _gdn_scan_bwd(wqk, attn, gb, states, vn, do, nh):
    S = vn.shape[0]
    nc = S // CHUNK
    rb = min(GDN_ROWS, S)
    cpb = rb // CHUNK
    last = S // rb - 1
    sds = jax.ShapeDtypeStruct

    def body(wqk_ref, attn_ref, gb_ref, st_ref, vn_ref, do_ref, dvn_ref, dw_ref, dqd_ref, dkd_ref, da_ref, dgl_ref, ds_ref):
        @pl.when(pl.program_id(0) == 0)
        def _():
            ds_ref[...] = jnp.zeros_like(ds_ref)

        row = lax.broadcasted_iota(jnp.int32, (CHUNK, CHUNK), 0)
        col = lax.broadcasted_iota(jnp.int32, (CHUNK, CHUNK), 1)
        for ci in reversed(range(cpb)):
            rows = slice(ci * CHUNK, (ci + 1) * CHUNK)
            glv = jnp.exp(jnp.sum(gb_ref[rows, :], axis=0, keepdims=True))
            base = ci * WQK
            for h in range(nh):
                cols = slice(h * HEAD_D, (h + 1) * HEAD_D)
                state, dS = st_ref[h, ci], ds_ref[h]
                wq = wqk_ref[h, base:base + 2 * CHUNK, :]
                kd = wqk_ref[h, base + 2 * CHUNK:base + WQK, :]
                vb = vn_ref[rows, cols]
                dob = do_ref[rows, cols]
                dvb = _gdot(attn_ref[h, ci], dob, TN) + _gdot(kd, dS, NN)
                x = _gdot(jnp.concatenate([dob, dvb], axis=0), state, NT)
                dqd_ref[rows, cols] = x[:CHUNK]
                dw_ref[rows, cols] = -x[CHUNK:]
                dvn_ref[rows, cols] = dvb
                da_ref[h, ci] = jnp.where(col <= row, _gdot(dob, vb, NT), 0.0)
                dkd_ref[rows, cols] = _gdot(vb, dS, NT)
                gl = glv[:, h:h + 1]
                dgl = jnp.sum(jnp.sum(state * dS, axis=0, keepdims=True), axis=1, keepdims=True)
                dgl_ref[h, ci] = jnp.broadcast_to(dgl * gl, (1, LANE))
                ds_ref[h] = dS * gl + _gdot(wq, jnp.concatenate([-dvb, dob], axis=0), TN)

    rv = lambda i: last - i
    wide = pl.BlockSpec((rb, nh * HEAD_D), lambda i: (rv(i), 0))
    return pl.pallas_call(
        body, grid=(S // rb,),
        in_specs=[pl.BlockSpec((nh, 3 * rb, HEAD_D), lambda i: (0, rv(i), 0)),
                  pl.BlockSpec((nh, cpb, CHUNK, CHUNK), lambda i: (0, rv(i), 0, 0)),
                  pl.BlockSpec((rb, LANE), lambda i: (rv(i), 0)),
                  pl.BlockSpec((nh, cpb, HEAD_D, HEAD_D), lambda i: (0, rv(i), 0, 0)), wide, wide],
        out_specs=[wide, wide, wide, wide, pl.BlockSpec((nh, cpb, CHUNK, CHUNK), lambda i: (0, rv(i), 0, 0)),
                   pl.BlockSpec((nh, cpb, 1, LANE), lambda i: (0, rv(i), 0, 0))],
        out_shape=[sds((S, nh * HEAD_D), F32), sds((S, nh * HEAD_D), F32), sds((S, nh * HEAD_D), F32),
                   sds((S, nh * HEAD_D), F32), sds((nh, nc, CHUNK, CHUNK), F32), sds((nh, nc, 1, LANE), F32)],
        scratch_shapes=[pltpu.VMEM((nh, HEAD_D, HEAD_D), F32)],
        compiler_params=_params(("arbitrary",)), name="gdn_scan_bwd")(wqk, attn, gb, states, vn, do)


def _gdn_local_bwd(qkv, gb, T, u, wqk, dvn, dw, dqd, dkd, dattn, dgl, nh):
    S = qkv.shape[0]
    rb = min(GDN_ROWS, S)
    cpb = rb // CHUNK
    sds = jax.ShapeDtypeStruct

    def body(q_ref, k_ref, v_ref, gb_ref, t_ref, u_ref, wqk_ref, dvn_ref, dw_ref, dqd_ref, dkd_ref, da_ref, dgl_ref,
             dq_ref, dk_ref, dv_ref, dg_ref, db_ref):
        h = pl.program_id(0)
        for ci in range(cpb):
            rows = slice(ci * CHUNK, (ci + 1) * CHUNK)
            q, k, v = q_ref[rows, :], k_ref[rows, :], v_ref[rows, :]
            gbt = gb_ref[rows, :]
            bcol = _col_of(gbt, nh + h)
            T = t_ref[ci]
            L = _chunk_local(q, k, v, _col_of(gbt, h), bcol, T=T)
            strict, eye, decay, eG, eGr, kb = L["strict"], L["eye"], L["decay"], L["eG"], L["eGr"], L["kb"]
            attn, q_dec, k_dec = L["attn"], L["q_dec"], L["k_dec"]
            w = wqk_ref[ci * WQK:ci * WQK + CHUNK, :]
            dq_dec, dk_dec, dattn_c = dqd_ref[rows, :], dkd_ref[rows, :], da_ref[ci]
            drv = _gdot(T, dvn_ref[rows, :], TN)
            drk = _gdot(T, dw_ref[rows, :], TN)
            dA = jnp.where(strict, -(_gdot(drv, u_ref[rows, :], NT) + _gdot(drk, w, NT)), 0.0)
            dM = dA * decay
            dN = dattn_c * decay
            dkb = _gdot(dM, k, NN)
            dq_ref[rows, :] = _gdot(dN, k, NN) + dq_dec * eG
            dk_ref[rows, :] = (drk * (bcol * eG) + _gdot(dM, kb, TN) + dkb * bcol + _gdot(dN, q, TN) + dk_dec * eGr)
            dv_ref[rows, :] = drv * bcol
            db_ref[rows, :] = (jnp.sum(drv * v, axis=1, keepdims=True) + jnp.sum(drk * k, axis=1, keepdims=True) * eG
                               + jnp.sum(dkb * k, axis=1, keepdims=True))
            E = dA * L["A"] + dattn_c * attn
            kd = jnp.sum(dk_dec * k_dec, axis=1, keepdims=True)
            dG = (jnp.sum(dq_dec * q_dec, axis=1, keepdims=True) - kd + jnp.sum(drk * L["rhs_k"], axis=1, keepdims=True)
                  + jnp.sum(E, axis=1, keepdims=True) - _to_col(jnp.sum(E, axis=0, keepdims=True), eye))
            d_last = jnp.sum(kd, axis=0, keepdims=True) + dgl_ref[ci][:, :1]
            dG = dG + jnp.where(L["row"][:, :1] == CHUNK - 1, d_last, 0.0)
            dg_ref[rows, :] = jnp.sum(jnp.where(L["col"] >= L["row"], _to_row(dG, eye), 0.0), axis=1, keepdims=True)

    hd = pl.BlockSpec((rb, HEAD_D), lambda h, i: (i, h))
    cc = pl.BlockSpec((None, cpb, CHUNK, CHUNK), lambda h, i: (h, i, 0, 0))
    col1 = pl.BlockSpec((None, rb, 1), lambda h, i: (h, i, 0))
    return pl.pallas_call(
        body, grid=(nh, S // rb),
        in_specs=[hd, pl.BlockSpec((rb, HEAD_D), lambda h, i: (i, nh + h)),
                  pl.BlockSpec((rb, HEAD_D), lambda h, i: (i, 2 * nh + h)), pl.BlockSpec((rb, LANE), lambda h, i: (i, 0)),
                  cc, hd, pl.BlockSpec((None, 3 * rb, HEAD_D), lambda h, i: (h, i, 0)), hd, hd, hd, hd, cc,
                  pl.BlockSpec((None, cpb, 1, LANE), lambda h, i: (h, i, 0, 0))],
        out_specs=[hd, hd, hd, col1, col1],
        out_shape=[sds((S, nh * HEAD_D), F32)] * 3 + [sds((nh, S, 1), F32)] * 2,
        compiler_params=_params(("parallel", "parallel")), name="gdn_local_bwd")(
            qkv, qkv, qkv, gb, T, u, wqk, dvn, dw, dqd, dkd, dattn, dgl)


def _gated_norm_fwd(o, proj, norm_w, nh, z_blk0):
    S = o.shape[0]

    def body(o_ref, z_ref, w_ref, y_ref):
        ov, z = o_ref[...], z_ref[...]
        r = lax.rsqrt(jnp.mean(ov * ov, axis=1, keepdims=True) + RMS_EPS)
        y_ref[...] = ov * r * w_ref[...] * (z * _sigmoid(z))

    return pl.pallas_call(
        body, grid=(nh,),
        in_specs=[pl.BlockSpec((S, HEAD_D), lambda h: (0, h)), pl.BlockSpec((S, HEAD_D), lambda h: (0, z_blk0 + h)),
                  pl.BlockSpec((1, HEAD_D), lambda h: (0, 0))],
        out_specs=pl.BlockSpec((S, HEAD_D), lambda h: (0, h)),
        out_shape=jax.ShapeDtypeStruct((S, nh * HEAD_D), F32),
        compiler_params=_params(("parallel",)), name="gated_norm_fwd")(o, proj, norm_w)


def _gated_norm_bwd(o, proj, norm_w, dmix, nh, z_blk0, d_blk0):
    S = o.shape[0]
    sds = jax.ShapeDtypeStruct

    def body(o_ref, z_ref, w_ref, dy_ref, do_ref, dz_ref, dw_ref):
        ov, z, w, dy = o_ref[...], z_ref[...], w_ref[...], dy_ref[...]
        r = lax.rsqrt(jnp.mean(ov * ov, axis=1, keepdims=True) + RMS_EPS)
        oh = ov * r
        sg = _sigmoid(z)
        dz_ref[...] = dy * (oh * w) * (sg * (1.0 + z * (1.0 - sg)))
        don = dy * (z * sg)
        @pl.when(pl.program_id(0) == 0)
        def _():
            dw_ref[...] = jnp.zeros_like(dw_ref)

        dw_ref[...] += jnp.sum(don * oh, axis=0, keepdims=True)
        doh = don * w
        do_ref[...] = r * (doh - oh * jnp.mean(doh * oh, axis=1, keepdims=True))

    return pl.pallas_call(
        body, grid=(nh,),
        in_specs=[pl.BlockSpec((S, HEAD_D), lambda h: (0, h)), pl.BlockSpec((S, HEAD_D), lambda h: (0, z_blk0 + h)),
                  pl.BlockSpec((1, HEAD_D), lambda h: (0, 0)), pl.BlockSpec((S, HEAD_D), lambda h: (0, d_blk0 + h))],
        out_specs=[pl.BlockSpec((S, HEAD_D), lambda h: (0, h)), pl.BlockSpec((S, HEAD_D), lambda h: (0, h)),
                   pl.BlockSpec((1, HEAD_D), lambda h: (0, 0))],
        out_shape=[sds((S, nh * HEAD_D), F32), sds((S, nh * HEAD_D), F32), sds((1, HEAD_D), F32)],
        compiler_params=_params(("arbitrary",)), name="gated_norm_bwd")(o, proj, norm_w, dmix)


def _adamw_math(w, g, m, v):
    m = ADAM_B1 * m + (1.0 - ADAM_B1) * g
    v = ADAM_B2 * v + (1.0 - ADAM_B2) * (g * g)
    m_hat = m / (1.0 - ADAM_B1 ** ADAM_STEP)
    v_hat = v / (1.0 - ADAM_B2 ** ADAM_STEP)
    delta = -ADAM_LR * (m_hat / (jnp.sqrt(v_hat) + ADAM_EPS) + ADAM_WD * w)
    return delta, m, v


def _adamw_big(parts, terms, chip, w, m, v, name, tr=256):
    R, C = w.shape
    tr = _tile(R, tr)
    sds = jax.ShapeDtypeStruct

    def body(q_ref, p_ref, t_ref, w_ref, m_ref, v_ref, g_ref, d_ref, nm_ref, nv_ref):
        g = ((p_ref[...].astype(F32) + t_ref[0].astype(F32)) + t_ref[1].astype(F32)) + t_ref[2].astype(F32)
        g_ref[...] = g
        d_ref[...], nm_ref[...], nv_ref[...] = _adamw_math(w_ref[...], g, m_ref[...], v_ref[...])

    spec = pl.BlockSpec((tr, C), lambda i, q_ref: (i, 0))
    grid_spec = pltpu.PrefetchScalarGridSpec(
        num_scalar_prefetch=1, grid=(R // tr,),
        in_specs=[pl.BlockSpec((None, tr, C), lambda i, q_ref: (q_ref[0], i, 0)),
                  pl.BlockSpec((3, tr, C), lambda i, q_ref: (0, i, 0)), spec, spec, spec],
        out_specs=[spec] * 4)
    return pl.pallas_call(body, grid_spec=grid_spec, out_shape=[sds((R, C), F32)] * 4,
                          compiler_params=_params(("parallel",)), name=name)(chip, parts, terms, w, m, v)


def _adamw_small(ws, gs, ms, vs):
    n = len(ws)

    def body(*refs):
        for i in range(n):
            w, g, m, v = (refs[k * n + i][...] for k in range(4))
            d, nm, nv = _adamw_math(w, g, m, v)
            refs[4 * n + i][...] = d
            refs[5 * n + i][...] = nm
            refs[6 * n + i][...] = nv

    shapes = [jax.ShapeDtypeStruct(w.shape, F32) for w in ws]
    vm = pl.BlockSpec(memory_space=pltpu.VMEM)
    outs = pl.pallas_call(body, in_specs=[vm] * (4 * n), out_specs=[vm] * (3 * n), out_shape=shapes * 3,
                          name="adamw_small")(*ws, *gs, *ms, *vs)
    return outs[:n], outs[n:2 * n], outs[2 * n:]


MESH = pl.DeviceIdType.MESH
ANY = pl.BlockSpec(memory_space=pl.ANY)


def _place():
    x, y, c = lax.axis_index("x"), lax.axis_index("y"), lax.axis_index("c")
    return x, y, c, [(1 - x, y), (x, 1 - y), (1 - x, 1 - y)]


def _all_gather(shards):
    n = len(shards)

    def body(*refs):
        ins, outs = refs[:n], refs[n:2 * n]
        send_sems, recv_sems, local_sems = refs[2 * n:]
        x, y, c, chips = _place()
        me, sibling = (x, y, c), (x, y, 1 - c)

        def slot(px, py, pc):
            return 4 * px + 2 * py + pc

        def copy(w, k, block, to, src=None):
            dst = outs[w].at[slot(*block)]
            return pltpu.make_async_remote_copy(src_ref=dst if src is None else src, dst_ref=dst,
                                                send_sem=send_sems.at[w * 7 + k], recv_sem=recv_sems.at[w * 7 + k],
                                                device_id=to, device_id_type=MESH)

        mine = [pltpu.make_async_copy(ins[w], outs[w].at[slot(*me)], local_sems.at[w]) for w in range(n)]
        for cp in mine:
            cp.start()
        first = []
        for w in range(n):
            first.append(copy(w, 0, me, sibling, src=ins[w]))
            first += [copy(w, 1 + j, me, (*chip, c), src=ins[w]) for j, chip in enumerate(chips)]
        for cp in first:
            cp.start()
        passed = []
        for w in range(n):
            for j, chip in enumerate(chips):
                copy(w, 1 + j, (*chip, c), me).wait_recv()
                cp = copy(w, 4 + j, (*chip, c), sibling)
                cp.start()
                passed.append(cp)
        for w in range(n):
            copy(w, 0, sibling, me).wait_recv()
            for j, chip in enumerate(chips):
                copy(w, 4 + j, (*chip, 1 - c), me).wait_recv()
        for cp in first + passed:
            cp.wait_send()
        for cp in mine:
            cp.wait()

    return pl.pallas_call(
        body, in_specs=[ANY] * n, out_specs=[ANY] * n,
        out_shape=[jax.ShapeDtypeStruct((N_DEV,) + s.shape, s.dtype) for s in shards],
        scratch_shapes=[pltpu.SemaphoreType.DMA((7 * n,)), pltpu.SemaphoreType.DMA((7 * n,)),
                        pltpu.SemaphoreType.DMA((n,))],
        name="all_gather_weights")(*shards)


def _sibling_exchange(grads):
    n = len(grads)

    def body(*refs):
        ins, outs = refs[:n], refs[n:2 * n]
        send_sems, recv_sems = refs[2 * n:]
        x, y, c, _ = _place()
        sibling = (x, y, 1 - c)
        copies = []
        for w in range(n):
            for q in range(4):
                copies.append(pltpu.make_async_remote_copy(
                    src_ref=ins[w].at[2 * q + (1 - c)], dst_ref=outs[w].at[q], send_sem=send_sems.at[4 * w + q],
                    recv_sem=recv_sems.at[4 * w + q], device_id=sibling, device_id_type=MESH))
        for cp in copies:
            cp.start()
        for cp in copies:
            cp.wait()

    return pl.pallas_call(
        body, in_specs=[ANY] * n, out_specs=[ANY] * n,
        out_shape=[jax.ShapeDtypeStruct((4,) + g.shape[1:], g.dtype) for g in grads],
        scratch_shapes=[pltpu.SemaphoreType.DMA((4 * n,)), pltpu.SemaphoreType.DMA((4 * n,))],
        name="rs_sibling_exchange")(*grads)


def _chip_sum(grad, recv, core, name, tr=256):
    _, R, C = grad.shape
    tr = _tile(R, tr)

    def body(c_ref, g_ref, r_ref, o_ref):
        o_ref[...] = (g_ref[...].astype(F32) + r_ref[...].astype(F32)).astype(o_ref.dtype)

    grid_spec = pltpu.PrefetchScalarGridSpec(
        num_scalar_prefetch=1, grid=(4, R // tr),
        in_specs=[pl.BlockSpec((None, tr, C), lambda q, i, c_ref: (2 * q + c_ref[0], i, 0)),
                  pl.BlockSpec((None, tr, C), lambda q, i, c_ref: (q, i, 0))],
        out_specs=pl.BlockSpec((None, tr, C), lambda q, i, c_ref: (q, i, 0)))
    return pl.pallas_call(body, grid_spec=grid_spec, out_shape=jax.ShapeDtypeStruct((4, R, C), BF16),
                          compiler_params=_params(("parallel", "parallel")), name=name)(core, grad, recv)


def _chip_exchange(parts):
    n = len(parts)

    def body(*refs):
        ins, outs = refs[:n], refs[n:2 * n]
        send_sems, recv_sems = refs[2 * n:]
        x, y, c, chips = _place()
        copies = []
        for w in range(n):
            for j, (px, py) in enumerate(chips):
                copies.append(pltpu.make_async_remote_copy(
                    src_ref=ins[w].at[2 * px + py], dst_ref=outs[w].at[j], send_sem=send_sems.at[3 * w + j],
                    recv_sem=recv_sems.at[3 * w + j], device_id=(px, py, c), device_id_type=MESH))
        for cp in copies:
            cp.start()
        for cp in copies:
            cp.wait()

    return pl.pallas_call(
        body, in_specs=[ANY] * n, out_specs=[ANY] * n,
        out_shape=[jax.ShapeDtypeStruct((3,) + p.shape[1:], p.dtype) for p in parts],
        scratch_shapes=[pltpu.SemaphoreType.DMA((3 * n,)), pltpu.SemaphoreType.DMA((3 * n,))],
        name="rs_chip_exchange")(*parts)


HBM_SPEC = pl.BlockSpec(memory_space=pltpu.HBM)
SEM_SPEC = pl.BlockSpec(memory_space=pltpu.SEMAPHORE)
DATAFLOW = pltpu.SideEffectType.DATAFLOW_SIDE_EFFECTING


def _split_start(name, bufs, plan, n, after):
    nb = len(bufs)

    def body(*refs):
        send_sems, recv_sems, token = refs[nb + 1], refs[nb + 2], refs[-1]
        for k, (src, dst, to) in enumerate(plan(refs[:nb])):
            pltpu.make_async_remote_copy(src_ref=src, dst_ref=dst, send_sem=send_sems.at[k], recv_sem=recv_sems.at[k],
                                         device_id=to, device_id_type=MESH).start()
        token[...] = jnp.zeros_like(token)

    outs = pl.pallas_call(
        body, name=name,
        out_shape=(pltpu.SemaphoreType.DMA((n,)), pltpu.SemaphoreType.DMA((n,)),
                   *[pltpu.HBM(b.shape, b.dtype) for b in bufs], jax.ShapeDtypeStruct((8, LANE), F32)),
        in_specs=[HBM_SPEC] * nb + [ANY],
        out_specs=(SEM_SPEC, SEM_SPEC, *[HBM_SPEC] * nb, pl.BlockSpec(memory_space=pltpu.VMEM)),
        input_output_aliases={i: 2 + i for i in range(nb)},
        compiler_params=pltpu.CompilerParams(has_side_effects=DATAFLOW))(
            *[pltpu.with_memory_space_constraint(b, pltpu.HBM) for b in bufs], after)
    return outs[0], outs[1], list(outs[2:2 + nb]), outs[-1]


def _split_wait(name, send_sems, recv_sems, bufs, plan, n, after):
    nb = len(bufs)

    def body(*refs):
        send_s, recv_s = refs[nb], refs[nb + 1]
        for k, (src, dst, to) in enumerate(plan(refs[:nb])):
            cp = pltpu.make_async_remote_copy(src_ref=src, dst_ref=dst, send_sem=send_s.at[k], recv_sem=recv_s.at[k],
                                              device_id=to, device_id_type=MESH)
            cp.wait_send()
            cp.wait_recv()

    outs = pl.pallas_call(
        body, name=name, out_shape=tuple(pltpu.HBM(b.shape, b.dtype) for b in bufs),
        in_specs=[HBM_SPEC] * nb + [SEM_SPEC, SEM_SPEC, ANY], out_specs=tuple([HBM_SPEC] * nb),
        input_output_aliases={i: i for i in range(nb)},
        compiler_params=pltpu.CompilerParams(has_side_effects=DATAFLOW))(*bufs, send_sems, recv_sems, after)
    return list(outs)


def _tie(x, token):
    return lax.optimization_barrier((x, token))[0]


def _slot(px, py, pc):
    return 4 * px + 2 * py + pc


class _Exchanges:
    def __init__(self, shards, dev, core):
        self.shards, self.dev, self.core = shards, dev, core
        self.n = len(shards)

    def _gather_plan1(self, refs):
        n = self.n
        x, y, c, chips = _place()
        out = []
        for w in range(n):
            dst = refs[n + w].at[_slot(x, y, c)]
            out.append((refs[w], dst, (x, y, 1 - c)))
            out += [(refs[w], dst, (px, py, c)) for px, py in chips]
        return out

    def _gather_plan2(self, refs):
        x, y, c, chips = _place()
        return [(refs[w].at[_slot(px, py, c)],) * 2 + ((x, y, 1 - c),) for w in range(self.n) for px, py in chips]

    def weights_start(self, after):
        lands = [lax.dynamic_update_slice(lax.empty((N_DEV,) + s.shape, s.dtype), s[None], (self.dev, 0, 0))
                 for s in self.shards]
        self.w1 = _split_start("gather_start_1", list(self.shards) + lands, self._gather_plan1, 4 * self.n, after)
        return self.w1[3]

    def weights_mid(self, after):
        send, recv, bufs, _ = self.w1
        lands = _split_wait("gather_wait_1", send, recv, bufs, self._gather_plan1, 4 * self.n, after)[self.n:]
        self.w2 = _split_start("gather_start_2", lands, self._gather_plan2, 3 * self.n, lands[0])
        return self.w2[3]

    def weights_finish(self, after):
        send, recv, bufs, _ = self.w2
        return _split_wait("gather_wait_2", send, recv, bufs, self._gather_plan2, 3 * self.n, after)

    def _reduce_plan1(self, refs):
        n = self.n
        x, y, c, _ = _place()
        return [(refs[w].at[2 * q + (1 - c)], refs[n + w].at[q], (x, y, 1 - c)) for w in range(n) for q in range(4)]

    def _reduce_plan2(self, refs):
        n = self.n
        x, y, c, chips = _place()
        return [(refs[w].at[2 * px + py], refs[n + w].at[j], (px, py, c))
                for w in range(n) for j, (px, py) in enumerate(chips)]

    def grads_start(self, grads):
        lands = [lax.empty((4,) + g.shape[1:], g.dtype) for g in grads]
        self.g1 = _split_start("reduce_start_1", list(grads) + lands, self._reduce_plan1, 4 * self.n, grads[-1])
        return self.g1[3]

    def grads_mid(self, after):
        n = self.n
        send, recv, bufs, _ = self.g1
        bufs = _split_wait("reduce_wait_1", send, recv, bufs, self._reduce_plan1, 4 * n, after)
        core = self.core.reshape(1).astype(jnp.int32)
        self.parts = [_chip_sum(bufs[w], bufs[n + w], core, "reduce_chip_sum_%d" % w) for w in range(n)]
        lands = [lax.empty((3,) + p.shape[1:], p.dtype) for p in self.parts]
        self.g2 = _split_start("reduce_start_2", self.parts + lands, self._reduce_plan2, 3 * n, self.parts[-1])
        return self.g2[3]

    def grads_finish(self, after):
        n = self.n
        send, recv, bufs, _ = self.g2
        bufs = _split_wait("reduce_wait_2", send, recv, bufs, self._reduce_plan2, 3 * n, after)
        self.parts, self.terms = bufs[:n], bufs[n:]


def _all_reduce_small(buf):
    R = buf.shape[0]

    def body(x_ref, o_ref, g_ref, send_sems, recv_sems):
        x, y, c, chips = _place()
        me, sibling = (x, y, c), (x, y, 1 - c)

        def slot(px, py, pc):
            return 4 * px + 2 * py + pc

        def copy(k, block, to, src=None):
            dst = g_ref.at[slot(*block)]
            return pltpu.make_async_remote_copy(src_ref=dst if src is None else src, dst_ref=dst,
                                                send_sem=send_sems.at[k], recv_sem=recv_sems.at[k],
                                                device_id=to, device_id_type=MESH)

        first = [copy(0, me, sibling, src=x_ref)]
        first += [copy(1 + j, me, (*chip, c), src=x_ref) for j, chip in enumerate(chips)]
        for cp in first:
            cp.start()
        g_ref[slot(*me)] = x_ref[...]
        passed = [copy(4 + j, (*chip, c), sibling) for j, chip in enumerate(chips)]
        for j, chip in enumerate(chips):
            copy(1 + j, (*chip, c), me).wait_recv()
            passed[j].start()
        copy(0, sibling, me).wait_recv()
        for j, chip in enumerate(chips):
            copy(4 + j, (*chip, 1 - c), me).wait_recv()
        for cp in first + passed:
            cp.wait_send()
        acc = g_ref[0]
        for s in range(1, N_DEV):
            acc = acc + g_ref[s]
        o_ref[...] = acc

    vm = pl.BlockSpec(memory_space=pltpu.VMEM)
    return pl.pallas_call(
        body, in_specs=[vm], out_specs=vm, out_shape=jax.ShapeDtypeStruct((R, LANE), F32),
        scratch_shapes=[pltpu.VMEM((N_DEV, R, LANE), F32), pltpu.SemaphoreType.DMA((7,)), pltpu.SemaphoreType.DMA((7,))],
        name="all_reduce_small")(buf)


def _pad_cols(a, width):
    return jnp.pad(a, ((0, 0), (0, width - a.shape[1])))


def _local_step(x, target, w_in, conv_w, a_log, dt_bias, delta_norm_w, sinks, rel_bias, ln1_g, ln1_b, ln2_g, ln2_b, ex):
    S, D = x.shape
    aw = D // 2
    hq, hkv, nh = aw // HEAD_A, aw // HEAD_A // GQA, aw // HEAD_D
    kvw = hkv * HEAD_A
    c_q, c_k, c_v, c_d = 0, aw, aw + kvw, aw + 2 * kvw
    c_ab = c_d + 3 * aw
    c_z = c_ab + 2 * nh
    n_in = c_z + aw
    assert w_in.shape == (D, n_in), (w_in.shape, n_in)
    w_p = jnp.concatenate([w_in[:, :c_ab], w_in[:, c_z:], _pad_cols(w_in[:, c_ab:c_z], LANE)], axis=1)
    p_z, p_ab = c_ab, c_ab + aw
    n_p = p_ab + LANE

    proj = _matmul(_tie(x, ex.weights_start(w_p)), w_p, NN, name="proj", tn=1152)
    bias = _attn_bias(rel_bias.T)
    attn_out = _attn_fwd(proj, bias, sinks.reshape(-1), hq, 0, c_k // kvw, c_v // kvw)
    conv2 = conv_w.reshape(CONV_W, 3 * aw)
    qkv = _gdn_prep_fwd(proj, conv2, nh, c_d // HEAD_D)
    ab = proj[:, p_ab:]
    al, dt = _pad_cols(a_log, LANE), _pad_cols(dt_bias, LANE)
    gb = _gates_fwd(ab, al, dt, nh)
    u_d, wqk, attn_d, t_d = _gdn_local_fwd(qkv, gb, nh)
    o_d, vn, states = _gdn_scan_fwd(_tie(u_d, ex.weights_mid(u_d)), wqk, attn_d, gb, nh)
    delta_out = _gated_norm_fwd(o_d, proj, delta_norm_w, nh, p_z // HEAD_D)
    mix = jnp.concatenate([attn_out, delta_out], axis=1)
    w_o_g, w_up_g, w_down_g = ex.weights_finish(mix)
    w_o, w_down = w_o_g.reshape(D, D), w_down_g.reshape(-1, D)
    mixed = _matmul(mix, w_o, NN, name="out_proj")
    h1 = _ln1_fwd(x, mixed, ln1_g, ln1_b)
    u = _matmul(h1, w_up_g, NN, name="mlp_up", b_groups=True)
    mlp = _matmul(u, w_down, NN, name="mlp_down", a_fn=_relu_sq)
    dr2, loss_row, dln2_g, dln2_b = _ln2_loss(h1, mlp, ln2_g, ln2_b, target)

    du = _matmul(dr2, w_down, NT, name="d_mlp_act", epi=_relu_sq_grad, epi_in=(u,))
    dw_down = _matmul(u, dr2, TN, name="dw_down", a_fn=_relu_sq, out_dtype=BF16)
    dw_up = _matmul(h1, du, TN, name="dw_up", out_dtype=BF16, out_groups=N_DEV)
    dh_mlp = _matmul(du, w_up_g, NT, name="d_h1", b_groups=True)
    dr1, dln1_g, dln1_b = _ln1_bwd(x, mixed, ln1_g, dr2, dh_mlp)
    dw_o = _matmul(mix, dr1, TN, name="dw_o", out_dtype=BF16)
    tok = ex.grads_start([dw_o.reshape(N_DEV, -1, D), dw_up, dw_down.reshape(N_DEV, -1, D)])
    dmix = _matmul(_tie(dr1, tok), w_o, NT, name="d_mix")
    dq_a, dk_a, dv_a, dbias, dsink = _attn_bwd(proj, bias, sinks.reshape(-1), attn_out, dmix, hq, 0, c_k // kvw, c_v // kvw)
    drel = _rel_bias_grad(dbias, hq)
    do_d, dz, dnw = _gated_norm_bwd(o_d, proj, delta_norm_w, dmix, nh, p_z // HEAD_D, aw // HEAD_D)
    do_d = _tie(do_d, ex.grads_mid(dq_a))
    dvn_s, dw_s, dqd, dkd, dattn_d, dgl = _gdn_scan_bwd(wqk, attn_d, gb, states, vn, do_d, nh)
    dqn, dkn, dvn, dg, dbeta = _gdn_local_bwd(qkv, gb, t_d, u_d, wqk, dvn_s, dw_s, dqd, dkd, dattn_d, dgl, nh)
    dgb = _pad_cols(jnp.concatenate([dg.reshape(nh, S).T, dbeta.reshape(nh, S).T], axis=1), LANE)
    dab, da_log, ddt_bias = _gates_bwd(ab, al, dt, dgb, nh)
    dqkv_d, dconv = _gdn_prep_bwd(proj, conv2, jnp.concatenate([dqn, dkn, dvn], axis=1), nh, c_d // HEAD_D)
    dproj = jnp.concatenate([dq_a, dk_a, dv_a, dqkv_d, dz, dab], axis=1)
    dw_p = _matmul(x, dproj, TN, name="dw_in", tn=640)
    dx_proj = _matmul(dproj, w_p, NT, name="d_x", tk=1920)
    grad_x = _grad_x(dr1, dx_proj)
    ex.grads_finish(grad_x)
    dw_in = jnp.concatenate([dw_p[:, :p_z], dw_p[:, p_ab:p_ab + 2 * nh], dw_p[:, p_z:p_ab]], axis=1)

    small = dict(conv_w=dconv, a_log=da_log[:, :nh], dt_bias=ddt_bias[:, :nh], delta_norm_w=dnw,
                 attn_sinks=dsink[:, 0].reshape(1, hq), rel_bias=drel[:, :, 0].T,
                 ln1_g=dln1_g, ln1_b=dln1_b, ln2_g=dln2_g, ln2_b=dln2_b)
    return loss_row, grad_x, dw_in, small


SMALL_ORDER = ("conv_w", "a_log", "dt_bias", "delta_norm_w", "attn_sinks", "rel_bias", "ln1_g", "ln1_b", "ln2_g", "ln2_b")


def _pack_small(loss_row, small):
    parts = [loss_row.reshape(-1)]
    for k in SMALL_ORDER:
        flat = small[k].reshape(-1)
        parts.append(jnp.pad(flat, (0, (-flat.shape[0]) % LANE)))
    flat = jnp.concatenate(parts)
    flat = jnp.pad(flat, (0, (-flat.shape[0]) % (8 * LANE)))
    return flat.reshape(-1, LANE)


def _unpack_small(buf, small_shapes):
    flat = buf.reshape(-1)
    loss = flat[0]
    off = LANE
    out = {}
    for k in SMALL_ORDER:
        n = int(np.prod(small_shapes[k]))
        out[k] = flat[off:off + n].reshape(small_shapes[k])
        off += n + (-n) % LANE
    return loss, out


def kernel(x, w_in, conv_w, a_log, dt_bias, delta_norm_w, attn_sinks, rel_bias, w_o, ln1_g, ln1_b, w_up, w_down, ln2_g, ln2_b, loss_target, m_w_in, m_conv_w, m_a_log, m_dt_bias, m_delta_norm_w, m_attn_sinks, m_rel_bias, m_w_o, m_ln1_g, m_ln1_b, m_w_up, m_w_down, m_ln2_g, m_ln2_b, v_w_in, v_conv_w, v_a_log, v_dt_bias, v_delta_norm_w, v_attn_sinks, v_rel_bias, v_w_o, v_ln1_g, v_ln1_b, v_w_up, v_w_down, v_ln2_g, v_ln2_b):
    S, D = x.shape[1], x.shape[2]
    core = lax.axis_index("c")
    dev = 4 * lax.axis_index("x") + 2 * lax.axis_index("y") + core

    (w_in_g,) = _all_gather([w_in[0].astype(BF16)])
    n_in = N_DEV * w_in.shape[2]
    w_in_full = jnp.transpose(w_in_g, (1, 0, 2)).reshape(D, n_in)
    ex = _Exchanges([w_o[0].astype(BF16), w_up[0].astype(BF16), w_down[0].astype(BF16)], dev, core)

    cw_sh = conv_w.shape[3]
    conv_place = lax.dynamic_update_slice(jnp.zeros((CONV_W, N_DEV * cw_sh), F32), conv_w[0, :, 0, :], (0, dev * cw_sh))
    conv_full = _all_reduce_small(jnp.pad(conv_place.reshape(-1, LANE), ((0, (-conv_place.size // LANE) % 8), (0, 0))))
    conv_full = conv_full[:conv_place.size // LANE].reshape(CONV_W, N_DEV * cw_sh)

    loss_row, grad_x, dw_in, small = _local_step(
        x[0], loss_target[0], w_in_full, conv_full, a_log, dt_bias, delta_norm_w, attn_sinks, rel_bias,
        ln1_g, ln1_b, ln2_g, ln2_b, ex)

    dw_in_g = jnp.transpose(dw_in.reshape(D, N_DEV, -1), (1, 0, 2)).astype(BF16)
    core_arr = core.reshape(1).astype(jnp.int32)
    chip_arr = (dev // 2).reshape(1).astype(jnp.int32)
    (recv,) = _sibling_exchange([dw_in_g])
    part_in = _chip_sum(dw_in_g, recv, core_arr, "rs_chip_sum_in")
    (terms_in,) = _chip_exchange([part_in])
    parts = [part_in] + ex.parts
    terms = [terms_in] + ex.terms
    big = {}
    for i, (name, w, m, v) in enumerate((("w_in", w_in, m_w_in, v_w_in), ("w_o", w_o, m_w_o, v_w_o),
                                         ("w_up", w_up, m_w_up, v_w_up), ("w_down", w_down, m_w_down, v_w_down))):
        outs = _adamw_big(parts[i], terms[i], chip_arr, w[0], m[0], v[0], "adamw_" + name)
        big[name] = [o[None] for o in outs]

    small_shapes = {k: v.shape for k, v in small.items()}
    loss, small = _unpack_small(_all_reduce_small(_pack_small(loss_row, small)), small_shapes)
    small["conv_w"] = lax.dynamic_slice(small["conv_w"], (0, dev * cw_sh), (CONV_W, cw_sh))
    small["rel_bias"] = small["rel_bias"].reshape(rel_bias.shape)
    p2 = dict(conv_w=(conv_w, m_conv_w, v_conv_w), a_log=(a_log, m_a_log, v_a_log), dt_bias=(dt_bias, m_dt_bias, v_dt_bias),
              delta_norm_w=(delta_norm_w, m_delta_norm_w, v_delta_norm_w), attn_sinks=(attn_sinks, m_attn_sinks, v_attn_sinks),
              rel_bias=(rel_bias, m_rel_bias, v_rel_bias), ln1_g=(ln1_g, m_ln1_g, v_ln1_g), ln1_b=(ln1_b, m_ln1_b, v_ln1_b),
              ln2_g=(ln2_g, m_ln2_g, v_ln2_g), ln2_b=(ln2_b, m_ln2_b, v_ln2_b))
    two_d = lambda a: a.reshape(-1, a.shape[-1])
    ws = [two_d(p2[k][0]) for k in SMALL_ORDER]
    gs = [two_d(small[k]) for k in SMALL_ORDER]
    ms = [two_d(p2[k][1]) for k in SMALL_ORDER]
    vs = [two_d(p2[k][2]) for k in SMALL_ORDER]
    ds, nms, nvs = _adamw_small(ws, gs, ms, vs)
    res = {}
    for i, k in enumerate(SMALL_ORDER):
        shp = p2[k][0].shape
        res[k] = [gs[i].reshape(shp), ds[i].reshape(shp), nms[i].reshape(shp), nvs[i].reshape(shp)]
    res.update(big)
    order = ("w_in", "conv_w", "a_log", "dt_bias", "delta_norm_w", "attn_sinks", "rel_bias", "w_o", "ln1_g", "ln1_b",
             "w_up", "w_down", "ln2_g", "ln2_b")
    return (loss, grad_x[None], *[res[k][0] for k in order], *[res[k][1] for k in order],
            *[res[k][2] for k in order], *[res[k][3] for k in order])
```

```python
import functools
import math

import numpy as np
import jax
import jax.numpy as jnp
from jax import lax
from jax.experimental import pallas as pl
from jax.experimental.pallas import tpu as pltpu

F32 = jnp.float32
BF16 = jnp.bfloat16
HIGHEST = lax.Precision.HIGHEST

N_DEV = 8
HEAD_A = 64
GQA = 4
BLK = 128
N_BUCKETS = 32
MAX_DISTANCE = 128
HEAD_D = 128
CONV_W = 4
CHUNK = 64
NEG_INF = -1e30
LN_EPS = 1e-5
RMS_EPS = 1e-6
DN_ALPHA = 2.0 ** 0.25
ADAM_LR, ADAM_B1, ADAM_B2, ADAM_EPS, ADAM_WD, ADAM_STEP = 0.001, 0.9, 0.999, 1e-08, 0.01, 10

LANE = 128
VMEM_LIMIT = 56 * 1024 * 1024

NN = ((1,), (0,))
NT = ((1,), (1,))
TN = ((0,), (0,))


def _dot(a, b, dims, prec=None):
    return lax.dot_general(a, b, (dims, ((), ())), precision=prec, preferred_element_type=F32)


def _tile(dim, pref):
    if dim <= pref:
        return dim
    t = (pref // LANE) * LANE
    while t > LANE and dim % t:
        t -= LANE
    assert dim % t == 0, (dim, pref)
    return t


def _params(sem):
    return pltpu.CompilerParams(dimension_semantics=sem, vmem_limit_bytes=VMEM_LIMIT)


def _matmul(a, b, dims, *, name, out_dtype=F32, tm=1024, tn=1024, tk=2048, a_fn=None, epi=None, epi_in=(),
            b_groups=None, out_groups=None, deps=()):
    (ca,), (cb,) = dims
    M, K = a.shape[1 - ca], a.shape[ca]
    if b_groups:
        G, R, C = b.shape
        bshape = (R, G * C)
    else:
        bshape = b.shape
    N = bshape[1 - cb]
    assert bshape[cb] == K, (a.shape, b.shape, dims)
    tm, tk = _tile(M, tm), _tile(K, tk)
    if b_groups:
        lim = C if cb == 0 else tn
        tn = _tile(N, min(tn, lim))
        if cb == 1:
            tk = _tile(K, min(tk, C))
    else:
        tn = _tile(N, tn)
    if out_groups:
        tn = _tile(N, min(tn, N // out_groups))
    nk = K // tk

    def body(*refs):
        a_ref, b_ref = refs[0], refs[1]
        e_refs = refs[2:2 + len(epi_in)]
        o_ref = refs[2 + len(epi_in) + len(deps)]
        acc_ref = refs[3 + len(epi_in) + len(deps)] if nk > 1 else None
        k = pl.program_id(2)
        av = a_ref[...]
        if a_fn is not None:
            av = a_fn(av)
        prod = _dot(av.astype(BF16), b_ref[...].astype(BF16), dims)

        def finish(r):
            if epi is not None:
                r = epi(r, *[e[...] for e in e_refs])
            o_ref[...] = r.astype(out_dtype)

        if nk == 1:
            finish(prod)
            return

        @pl.when(k == 0)
        def _():
            acc_ref[...] = prod

        @pl.when(k > 0)
        def _():
            acc_ref[...] += prod

        @pl.when(k == nk - 1)
        def _():
            finish(acc_ref[...])

    a_spec = (pl.BlockSpec((tm, tk), lambda i, j, k: (i, k)) if ca == 1
              else pl.BlockSpec((tk, tm), lambda i, j, k: (k, i)))
    if b_groups:
        if cb == 0:
            per = C // tn
            b_spec = pl.BlockSpec((None, tk, tn), lambda i, j, k: (j // per, k, j % per))
        else:
            per = C // tk
            b_spec = pl.BlockSpec((None, tn, tk), lambda i, j, k: (k // per, j, k % per))
    else:
        b_spec = (pl.BlockSpec((tk, tn), lambda i, j, k: (k, j)) if cb == 0
                  else pl.BlockSpec((tn, tk), lambda i, j, k: (j, k)))
    e_specs = [pl.BlockSpec((tm, tn), lambda i, j, k: (i, j)) for _ in epi_in]
    if out_groups:
        per_o = (N // out_groups) // tn
        o_spec = pl.BlockSpec((None, tm, tn), lambda i, j, k: (j // per_o, i, j % per_o))
        o_shape = jax.ShapeDtypeStruct((out_groups, M, N // out_groups), out_dtype)
    else:
        o_spec = pl.BlockSpec((tm, tn), lambda i, j, k: (i, j))
        o_shape = jax.ShapeDtypeStruct((M, N), out_dtype)
    return pl.pallas_call(
        body, grid=(M // tm, N // tn, nk), out_specs=o_spec,
        in_specs=[a_spec, b_spec] + e_specs + [pl.BlockSpec(memory_space=pl.ANY)] * len(deps),
        out_shape=o_shape, scratch_shapes=[pltpu.VMEM((tm, tn), F32)] if nk > 1 else [],
        compiler_params=_params(("parallel", "parallel", "arbitrary")), name=name)(a, b, *epi_in, *deps)


def _relu_sq(u):
    r = jnp.maximum(u, 0.0)
    return r * r


def _relu_sq_grad(acc, u):
    return acc * (2.0 * jnp.maximum(u, 0.0))


def _ln_stats(r):
    mu = jnp.mean(r, axis=-1, keepdims=True)
    xc = r - mu
    var = jnp.mean(xc * xc, axis=-1, keepdims=True)
    rstd = lax.rsqrt(var + LN_EPS)
    return xc * rstd, rstd


def _ln_bwd(dy, xhat, rstd, g):
    dxh = dy * g
    m1 = jnp.mean(dxh, axis=-1, keepdims=True)
    m2 = jnp.mean(dxh * xhat, axis=-1, keepdims=True)
    return rstd * (dxh - m1 - xhat * m2)


def _row_call(body, ins, row_ins, outs, acc_outs, name, tr=256):
    S = ins[0].shape[0]
    tr = min(tr, S)
    n_in, n_row, n_out = len(ins), len(row_ins), len(outs)

    def wrapped(*refs):
        i = pl.program_id(0)
        acc_refs = refs[n_in + n_row + n_out:]

        @pl.when(i == 0)
        def _():
            for r in acc_refs:
                r[...] = jnp.zeros_like(r)

        body(*refs)

    in_specs = [pl.BlockSpec((tr, a.shape[1]), lambda i: (i, 0)) for a in ins]
    in_specs += [pl.BlockSpec(a.shape, lambda i: (0, 0)) for a in row_ins]
    out_specs = [pl.BlockSpec((tr, s.shape[1]), lambda i: (i, 0)) for s in outs]
    out_specs += [pl.BlockSpec(s.shape, lambda i: (0, 0)) for s in acc_outs]
    return pl.pallas_call(wrapped, grid=(S // tr,), in_specs=in_specs, out_specs=out_specs,
                          out_shape=list(outs) + list(acc_outs),
                          compiler_params=_params(("arbitrary",)), name=name)(*ins, *row_ins)


def _ln1_fwd(x, mixed, g, b):
    def body(x_ref, m_ref, g_ref, b_ref, h_ref):
        xhat, _ = _ln_stats(DN_ALPHA * x_ref[...] + m_ref[...])
        h_ref[...] = xhat * g_ref[...] + b_ref[...]
    return _row_call(body, [x, mixed], [g, b], [jax.ShapeDtypeStruct(x.shape, F32)], [], "ln1_fwd")[0]


def _ln2_loss(h1, mlp, g, b, target):
    S, D = h1.shape
    sds = jax.ShapeDtypeStruct

    def body(h_ref, m_ref, t_ref, g_ref, b_ref, dr_ref, loss_ref, dg_ref, db_ref):
        xhat, rstd = _ln_stats(DN_ALPHA * h_ref[...] + m_ref[...])
        gv = g_ref[...]
        err = xhat * gv + b_ref[...] - t_ref[...]
        loss_ref[...] += jnp.sum(jnp.sum(err * err, axis=0, keepdims=True), axis=1, keepdims=True) * (0.5 / D)
        dy = err * (1.0 / D)
        dg_ref[...] += jnp.sum(dy * xhat, axis=0, keepdims=True)
        db_ref[...] += jnp.sum(dy, axis=0, keepdims=True)
        dr_ref[...] = _ln_bwd(dy, xhat, rstd, gv)

    return _row_call(body, [h1, mlp, target], [g, b], [sds((S, D), F32)],
                     [sds((1, LANE), F32), sds((1, D), F32), sds((1, D), F32)], "ln2_loss")


def _ln1_bwd(x, mixed, g, dr2, dh_mlp):
    S, D = x.shape
    sds = jax.ShapeDtypeStruct

    def body(x_ref, m_ref, dr2_ref, dh_ref, g_ref, dr_ref, dg_ref, db_ref):
        xhat, rstd = _ln_stats(DN_ALPHA * x_ref[...] + m_ref[...])
        dy = DN_ALPHA * dr2_ref[...] + dh_ref[...]
        dg_ref[...] += jnp.sum(dy * xhat, axis=0, keepdims=True)
        db_ref[...] += jnp.sum(dy, axis=0, keepdims=True)
        dr_ref[...] = _ln_bwd(dy, xhat, rstd, g_ref[...])

    return _row_call(body, [x, mixed, dr2, dh_mlp], [g], [sds((S, D), F32)],
                     [sds((1, D), F32), sds((1, D), F32)], "ln1_bwd")


def _grad_x(dr1, dx_proj):
    def body(a_ref, b_ref, o_ref):
        o_ref[...] = DN_ALPHA * a_ref[...] + b_ref[...]
    return _row_call(body, [dr1, dx_proj], [], [jax.ShapeDtypeStruct(dr1.shape, F32)], [], "grad_x")[0]


def _bucket_table():
    qi = np.arange(BLK, dtype=np.int32)[:, None]
    kj = np.arange(2 * BLK, dtype=np.int32)[None, :]
    dist = qi + BLK - kj
    n = np.maximum(dist, 0)
    max_exact = N_BUCKETS // 2
    nf = np.maximum(n, 1).astype(np.float32)
    large = max_exact + (np.log(nf / np.float32(max_exact)) / np.float32(math.log(MAX_DISTANCE / max_exact))
                         * np.float32(N_BUCKETS - max_exact)).astype(np.int32)
    large = np.minimum(large, N_BUCKETS - 1)
    bucket = np.where(n < max_exact, n, large)
    return np.where((dist >= 0) & (dist < BLK), bucket, -1).astype(np.int32)


def _attn_bias(rel_bias_t):
    hq = rel_bias_t.shape[0]
    bucket = jnp.asarray(_bucket_table())

    def body(rb_ref, bk_ref, o_ref):
        h = pl.program_id(0)
        bk = bk_ref[...]
        acc = jnp.zeros((BLK, 2 * BLK), F32)
        for b in range(N_BUCKETS):
            acc = jnp.where(bk == b, rb_ref[h, b], acc)
        o_ref[...] = acc

    return pl.pallas_call(
        body, grid=(hq,),
        in_specs=[pl.BlockSpec(memory_space=pltpu.SMEM), pl.BlockSpec((BLK, 2 * BLK), lambda h: (0, 0))],
        out_specs=pl.BlockSpec((BLK, 2 * BLK), lambda h: (h, 0)),
        out_shape=jax.ShapeDtypeStruct((hq * BLK, 2 * BLK), F32),
        compiler_params=_params(("arbitrary",)), name="attn_bias")(rel_bias_t, bucket)


def _attn_probs(q, kc, kp, bias, sink, mask_c, mask_p):
    lc = jnp.where(mask_c, _dot(q, kc, NT) + bias[:, BLK:], NEG_INF)
    lp = jnp.where(mask_p, _dot(q, kp, NT) + bias[:, :BLK], NEG_INF)
    m = jnp.maximum(jnp.maximum(jnp.max(lc, axis=1, keepdims=True), jnp.max(lp, axis=1, keepdims=True)), sink)
    pc, pp, ps = jnp.exp(lc - m), jnp.exp(lp - m), jnp.exp(sink - m)
    inv = 1.0 / (jnp.sum(pc, axis=1, keepdims=True) + jnp.sum(pp, axis=1, keepdims=True) + ps)
    return pc, pp, ps, inv


def _attn_masks(n):
    qi = lax.broadcasted_iota(jnp.int32, (BLK, BLK), 0)
    kj = lax.broadcasted_iota(jnp.int32, (BLK, BLK), 1)
    return kj <= qi, (kj > qi) & (n > 0)


def _attn_fwd(proj, bias, sinks, hq, q_blk, k_blk, v_blk):
    S = proj.shape[0]
    hkv = hq // GQA
    wq, wk = hq * HEAD_A, hkv * HEAD_A

    def body(q_ref, k_ref, v_ref, bias_ref, sink_ref, o_ref):
        n = pl.program_id(0)
        cur = pl.multiple_of(n * BLK, BLK)
        prev = pl.multiple_of(jnp.maximum(n - 1, 0) * BLK, BLK)
        mask_c, mask_p = _attn_masks(n)
        for h4 in range(hkv):
            cs = slice(h4 * HEAD_A, (h4 + 1) * HEAD_A)
            kc, kp = k_ref[pl.ds(cur, BLK), cs].astype(BF16), k_ref[pl.ds(prev, BLK), cs].astype(BF16)
            vc, vp = v_ref[pl.ds(cur, BLK), cs].astype(BF16), v_ref[pl.ds(prev, BLK), cs].astype(BF16)
            for g in range(GQA):
                h = h4 * GQA + g
                hs = slice(h * HEAD_A, (h + 1) * HEAD_A)
                q = (q_ref[:, hs] * (HEAD_A ** -0.5)).astype(BF16)
                pc, pp, _, inv = _attn_probs(q, kc, kp, bias_ref[h * BLK:(h + 1) * BLK, :], sink_ref[h], mask_c, mask_p)
                o_ref[:, hs] = (_dot(pc.astype(BF16), vc, NN) + _dot(pp.astype(BF16), vp, NN)) * inv

    return pl.pallas_call(
        body, grid=(S // BLK,),
        in_specs=[pl.BlockSpec((BLK, wq), lambda n: (n, q_blk)), pl.BlockSpec((S, wk), lambda n: (0, k_blk)),
                  pl.BlockSpec((S, wk), lambda n: (0, v_blk)), pl.BlockSpec((hq * BLK, 2 * BLK), lambda n: (0, 0)),
                  pl.BlockSpec(memory_space=pltpu.SMEM)],
        out_specs=pl.BlockSpec((BLK, wq), lambda n: (n, 0)),
        out_shape=jax.ShapeDtypeStruct((S, wq), F32),
        compiler_params=_params(("arbitrary",)), name="attn_fwd")(proj, proj, proj, bias, sinks)


def _attn_bwd(proj, bias, sinks, out, dmix, hq, q_blk, k_blk, v_blk):
    S = proj.shape[0]
    hkv = hq // GQA
    wq, wk = hq * HEAD_A, hkv * HEAD_A
    sds = jax.ShapeDtypeStruct

    def body(q_ref, k_ref, v_ref, bias_ref, sink_ref, o_ref, do_ref, dq_ref, dk_ref, dv_ref, dbias_ref, dsink_ref):
        n = pl.program_id(0)

        @pl.when(n == 0)
        def _():
            dk_ref[...] = jnp.zeros_like(dk_ref)
            dv_ref[...] = jnp.zeros_like(dv_ref)
            dbias_ref[...] = jnp.zeros_like(dbias_ref)
            dsink_ref[...] = jnp.zeros_like(dsink_ref)

        cur = pl.multiple_of(n * BLK, BLK)
        prev = pl.multiple_of(jnp.maximum(n - 1, 0) * BLK, BLK)
        mask_c, mask_p = _attn_masks(n)
        for h4 in range(hkv):
            cs = slice(h4 * HEAD_A, (h4 + 1) * HEAD_A)
            kc, kp = k_ref[pl.ds(cur, BLK), cs].astype(BF16), k_ref[pl.ds(prev, BLK), cs].astype(BF16)
            vc, vp = v_ref[pl.ds(cur, BLK), cs].astype(BF16), v_ref[pl.ds(prev, BLK), cs].astype(BF16)
            dkc = jnp.zeros((BLK, HEAD_A), F32)
            dkp = jnp.zeros((BLK, HEAD_A), F32)
            dvc = jnp.zeros((BLK, HEAD_A), F32)
            dvp = jnp.zeros((BLK, HEAD_A), F32)
            for g in range(GQA):
                h = h4 * GQA + g
                hs = slice(h * HEAD_A, (h + 1) * HEAD_A)
                rows = slice(h * BLK, (h + 1) * BLK)
                q = (q_ref[:, hs] * (HEAD_A ** -0.5)).astype(BF16)
                pc, pp, ps, inv = _attn_probs(q, kc, kp, bias_ref[rows, :], sink_ref[h], mask_c, mask_p)
                pc, pp, ps = pc * inv, pp * inv, ps * inv
                do = do_ref[:, hs]
                delta = jnp.sum(do * o_ref[:, hs], axis=1, keepdims=True)
                dob = do.astype(BF16)
                dsc = pc * (_dot(dob, vc, NT) - delta)
                dsp = pp * (_dot(dob, vp, NT) - delta)
                dsink_ref[h:h + 1, :] += jnp.broadcast_to(jnp.sum(-ps * delta, axis=0, keepdims=True), (1, LANE))
                dbias_ref[rows, BLK:] += dsc
                dbias_ref[rows, :BLK] += dsp
                dscb, dspb = dsc.astype(BF16), dsp.astype(BF16)
                dq_ref[:, hs] = (_dot(dscb, kc, NN) + _dot(dspb, kp, NN)) * (HEAD_A ** -0.5)
                dkc += _dot(dscb, q, TN)
                dkp += _dot(dspb, q, TN)
                dvc += _dot(pc.astype(BF16), dob, TN)
                dvp += _dot(pp.astype(BF16), dob, TN)
            dk_ref[pl.ds(cur, BLK), cs] += dkc
            dk_ref[pl.ds(prev, BLK), cs] += dkp
            dv_ref[pl.ds(cur, BLK), cs] += dvc
            dv_ref[pl.ds(prev, BLK), cs] += dvp

    return pl.pallas_call(
        body, grid=(S // BLK,),
        in_specs=[pl.BlockSpec((BLK, wq), lambda n: (n, q_blk)), pl.BlockSpec((S, wk), lambda n: (0, k_blk)),
                  pl.BlockSpec((S, wk), lambda n: (0, v_blk)), pl.BlockSpec((hq * BLK, 2 * BLK), lambda n: (0, 0)),
                  pl.BlockSpec(memory_space=pltpu.SMEM),
                  pl.BlockSpec((BLK, wq), lambda n: (n, 0)), pl.BlockSpec((BLK, wq), lambda n: (n, 0))],
        out_specs=[pl.BlockSpec((BLK, wq), lambda n: (n, 0)), pl.BlockSpec((S, wk), lambda n: (0, 0)),
                   pl.BlockSpec((S, wk), lambda n: (0, 0)), pl.BlockSpec((hq * BLK, 2 * BLK), lambda n: (0, 0)),
                   pl.BlockSpec((hq, LANE), lambda n: (0, 0))],
        out_shape=[sds((S, wq), F32), sds((S, wk), F32), sds((S, wk), F32), sds((hq * BLK, 2 * BLK), F32),
                   sds((hq, LANE), F32)],
        compiler_params=_params(("arbitrary",)), name="attn_bwd")(proj, proj, proj, bias, sinks, out, dmix)


def _rel_bias_grad(dbias, hq):
    bucket = jnp.asarray(_bucket_table())

    def body(d_ref, bk_ref, o_ref):
        d = d_ref[...]
        bk = bk_ref[...]
        rows = [jnp.sum(jnp.where(bk == b, d, 0.0), axis=0, keepdims=True) for b in range(N_BUCKETS)]
        tot = jnp.sum(jnp.concatenate(rows, axis=0), axis=1, keepdims=True)
        o_ref[...] = jnp.broadcast_to(tot, (N_BUCKETS, LANE))

    return pl.pallas_call(
        body, grid=(hq,),
        in_specs=[pl.BlockSpec((BLK, 2 * BLK), lambda h: (h, 0)), pl.BlockSpec((BLK, 2 * BLK), lambda h: (0, 0))],
        out_specs=pl.BlockSpec((None, N_BUCKETS, LANE), lambda h: (h, 0, 0)),
        out_shape=jax.ShapeDtypeStruct((hq, N_BUCKETS, LANE), F32),
        compiler_params=_params(("arbitrary",)), name="rel_bias_grad")(dbias, bucket)


def _sigmoid(x):
    return 1.0 / (1.0 + jnp.exp(-x))


def _shift_rows(x, s):
    n = x.shape[0]
    row = lax.broadcasted_iota(jnp.int32, x.shape, 0)
    if s > 0:
        return jnp.where(row >= s, pltpu.roll(x, s, 0), 0.0)
    return jnp.where(row < n + s, pltpu.roll(x, n + s, 0), 0.0)


def _conv_silu_norm(xv, w, j, nh):
    c = w[CONV_W - 1:CONV_W, :] * xv
    for s in range(1, CONV_W):
        c = c + w[CONV_W - 1 - s:CONV_W - s, :] * _shift_rows(xv, s)
    sg = _sigmoid(c)
    a = c * sg
    r = lax.rsqrt(jnp.sum(a * a, axis=1, keepdims=True) + RMS_EPS)
    scale = jnp.where(j < nh, HEAD_D ** -0.5, 1.0)
    is_norm = j < 2 * nh
    y = jnp.where(is_norm, a * (r * scale), a)
    return c, sg, a, r, scale, is_norm, y


def _gdn_prep_fwd(proj, conv_w, nh, blk0):
    S = proj.shape[0]

    def body(x_ref, w_ref, o_ref):
        j = pl.program_id(0)
        o_ref[...] = _conv_silu_norm(x_ref[...], w_ref[...], j, nh)[-1]

    return pl.pallas_call(
        body, grid=(3 * nh,),
        in_specs=[pl.BlockSpec((S, HEAD_D), lambda j: (0, blk0 + j)), pl.BlockSpec((CONV_W, HEAD_D), lambda j: (0, j))],
        out_specs=pl.BlockSpec((S, HEAD_D), lambda j: (0, j)),
        out_shape=jax.ShapeDtypeStruct((S, 3 * nh * HEAD_D), F32),
        compiler_params=_params(("parallel",)), name="gdn_prep_fwd")(proj, conv_w)


def _gdn_prep_bwd(proj, conv_w, dqkv, nh, blk0):
    S = proj.shape[0]
    sds = jax.ShapeDtypeStruct

    def body(x_ref, w_ref, dy_ref, dx_ref, dw_ref):
        j = pl.program_id(0)
        xv, w = x_ref[...], w_ref[...]
        c, sg, a, r, scale, is_norm, _ = _conv_silu_norm(xv, w, j, nh)
        dy = dy_ref[...]
        rs = r * scale
        da_n = rs * dy - a * (r * r * rs) * jnp.sum(dy * a, axis=1, keepdims=True)
        da = jnp.where(is_norm, da_n, dy)
        dc = da * (sg * (1.0 + c * (1.0 - sg)))
        dx = w[CONV_W - 1:CONV_W, :] * dc
        dws = [jnp.sum(dc * xv, axis=0, keepdims=True)]
        for s in range(1, CONV_W):
            dx = dx + w[CONV_W - 1 - s:CONV_W - s, :] * _shift_rows(dc, -s)
            dws.insert(0, jnp.sum(dc * _shift_rows(xv, s), axis=0, keepdims=True))
        dx_ref[...] = dx
        dw_ref[...] = jnp.concatenate(dws, axis=0)

    return pl.pallas_call(
        body, grid=(3 * nh,),
        in_specs=[pl.BlockSpec((S, HEAD_D), lambda j: (0, blk0 + j)), pl.BlockSpec((CONV_W, HEAD_D), lambda j: (0, j)),
                  pl.BlockSpec((S, HEAD_D), lambda j: (0, j))],
        out_specs=[pl.BlockSpec((S, HEAD_D), lambda j: (0, j)), pl.BlockSpec((CONV_W, HEAD_D), lambda j: (0, j))],
        out_shape=[sds((S, 3 * nh * HEAD_D), F32), sds((CONV_W, 3 * nh * HEAD_D), F32)],
        compiler_params=_params(("parallel",)), name="gdn_prep_bwd")(proj, conv_w, dqkv)


def _softplus(x):
    return jnp.maximum(x, 0.0) + jnp.log(1.0 + jnp.exp(-jnp.abs(x)))


def _gates_fwd(ab, al, dt, nh):
    S = ab.shape[0]

    def body(ab_ref, al_ref, dt_ref, o_ref):
        v = ab_ref[...]
        lane = lax.broadcasted_iota(jnp.int32, v.shape, 1)
        g = -jnp.exp(al_ref[...]) * _softplus(v + dt_ref[...])
        o_ref[...] = jnp.where(lane < nh, g, jnp.where(lane < 2 * nh, _sigmoid(v), 0.0))

    row = pl.BlockSpec((1, LANE), lambda i: (0, 0))
    full = pl.BlockSpec((S, LANE), lambda i: (0, 0))
    return pl.pallas_call(body, grid=(1,), in_specs=[full, row, row], out_specs=full,
                          out_shape=jax.ShapeDtypeStruct((S, LANE), F32),
                          compiler_params=_params(("arbitrary",)), name="gates_fwd")(ab, al, dt)


def _gates_bwd(ab, al, dt, dgb, nh):
    S = ab.shape[0]
    sds = jax.ShapeDtypeStruct

    def body(ab_ref, al_ref, dt_ref, d_ref, dab_ref, dal_ref, ddt_ref):
        v, d = ab_ref[...], d_ref[...]
        lane = lax.broadcasted_iota(jnp.int32, v.shape, 1)
        is_a = lane < nh
        z = v + dt_ref[...]
        dsp = jnp.where(is_a, d * (-jnp.exp(al_ref[...])), 0.0)
        dz = dsp * _sigmoid(z)
        beta = _sigmoid(v)
        dab_ref[...] = jnp.where(is_a, dz, jnp.where(lane < 2 * nh, d * beta * (1.0 - beta), 0.0))
        dal_ref[...] = jnp.sum(dsp * _softplus(z), axis=0, keepdims=True)
        ddt_ref[...] = jnp.sum(dz, axis=0, keepdims=True)

    row = pl.BlockSpec((1, LANE), lambda i: (0, 0))
    full = pl.BlockSpec((S, LANE), lambda i: (0, 0))
    return pl.pallas_call(body, grid=(1,), in_specs=[full, row, row, full], out_specs=[full, row, row],
                          out_shape=[sds((S, LANE), F32), sds((1, LANE), F32), sds((1, LANE), F32)],
                          compiler_params=_params(("arbitrary",)), name="gates_bwd")(ab, al, dt, dgb)


def _col_of(tile, h):
    lane = lax.broadcasted_iota(jnp.int32, tile.shape, 1)
    return jnp.sum(jnp.where(lane == h, tile, 0.0), axis=1, keepdims=True)


def _to_row(col, eye):
    return jnp.sum(jnp.where(eye, col, 0.0), axis=0, keepdims=True)


def _to_col(row, eye):
    return jnp.sum(jnp.where(eye, row, 0.0), axis=1, keepdims=True)


def _split(a):
    hi = a.astype(BF16)
    return hi, (a - hi.astype(F32)).astype(BF16)


def _gdot(a, b, dims):
    ah, al = _split(a)
    bh, bl = _split(b)
    return _dot(ah, bh, dims) + (_dot(ah, bl, dims) + _dot(al, bh, dims))


def _chunk_local(q, k, v, gcol, bcol, T=None):
    C = CHUNK
    row = lax.broadcasted_iota(jnp.int32, (C, C), 0)
    col = lax.broadcasted_iota(jnp.int32, (C, C), 1)
    tril, strict, eye = col <= row, col < row, col == row
    grow = _to_row(gcol, eye)
    G_row = jnp.sum(jnp.where(row <= col, gcol, 0.0), axis=0, keepdims=True)
    G_col = jnp.sum(jnp.where(tril, grow, 0.0), axis=1, keepdims=True)
    decay = jnp.exp(jnp.where(tril, G_col - G_row, NEG_INF))
    G_last = G_col[C - 1:C, :]
    eG = jnp.exp(G_col)
    eGr = jnp.exp(G_last - G_col)
    gl = jnp.exp(G_last)
    kb = k * bcol
    A = jnp.where(strict, _gdot(kb, k, NT) * decay, 0.0)
    attn = _gdot(q, k, NT) * decay
    out = dict(strict=strict, eye=eye, row=row, col=col, decay=decay, eG=eG, eGr=eGr, gl=gl, kb=kb, A=A,
               rhs_k=kb * eG, attn=attn, q_dec=q * eG, k_dec=k * eGr)
    if T is None:
        T = jnp.where(eye, 1.0, 0.0) - A
        P = A
        for _ in range(int(math.log2(C)) - 1):
            P = _gdot(P, P, NN)
            T = T + _gdot(T, P, NN)
        out.update(T=T, u=_gdot(T, v * bcol, NN), w=_gdot(T, out["rhs_k"], NN))
    return out


GDN_ROWS = 256
WQK = 3 * CHUNK


def _gdn_local_fwd(qkv, gb, nh):
    S = qkv.shape[0]
    nc = S // CHUNK
    rb = min(GDN_ROWS, S)
    cpb = rb // CHUNK
    sds = jax.ShapeDtypeStruct

    def body(q_ref, k_ref, v_ref, gb_ref, u_ref, wqk_ref, attn_ref, t_ref):
        h = pl.program_id(0)
        for ci in range(cpb):
            rows = slice(ci * CHUNK, (ci + 1) * CHUNK)
            gbt = gb_ref[rows, :]
            L = _chunk_local(q_ref[rows, :], k_ref[rows, :], v_ref[rows, :], _col_of(gbt, h), _col_of(gbt, nh + h))
            u_ref[rows, :] = L["u"]
            base = ci * WQK
            wqk_ref[base:base + CHUNK, :] = L["w"]
            wqk_ref[base + CHUNK:base + 2 * CHUNK, :] = L["q_dec"]
            wqk_ref[base + 2 * CHUNK:base + WQK, :] = L["k_dec"]
            attn_ref[ci] = L["attn"]
            t_ref[ci] = L["T"]

    cc = pl.BlockSpec((None, cpb, CHUNK, CHUNK), lambda h, i: (h, i, 0, 0))
    return pl.pallas_call(
        body, grid=(nh, S // rb),
        in_specs=[pl.BlockSpec((rb, HEAD_D), lambda h, i: (i, h)), pl.BlockSpec((rb, HEAD_D), lambda h, i: (i, nh + h)),
                  pl.BlockSpec((rb, HEAD_D), lambda h, i: (i, 2 * nh + h)), pl.BlockSpec((rb, LANE), lambda h, i: (i, 0))],
        out_specs=[pl.BlockSpec((rb, HEAD_D), lambda h, i: (i, h)),
                   pl.BlockSpec((None, 3 * rb, HEAD_D), lambda h, i: (h, i, 0)), cc, cc],
        out_shape=[sds((S, nh * HEAD_D), F32), sds((nh, 3 * S, HEAD_D), F32), sds((nh, nc, CHUNK, CHUNK), F32),
                   sds((nh, nc, CHUNK, CHUNK), F32)],
        compiler_params=_params(("parallel", "parallel")), name="gdn_local_fwd")(qkv, qkv, qkv, gb)


def _gdn_scan_fwd(u, wqk, attn, gb, nh, dep):
    S = u.shape[0]
    nc = S // CHUNK
    rb = min(GDN_ROWS, S)
    cpb = rb // CHUNK
    sds = jax.ShapeDtypeStruct

    def body(u_ref, wqk_ref, attn_ref, gb_ref, dep_ref, o_ref, vn_ref, st_ref, s_ref):
        @pl.when(pl.program_id(0) == 0)
        def _():
            s_ref[...] = jnp.zeros_like(s_ref)

        for ci in range(cpb):
            rows = slice(ci * CHUNK, (ci + 1) * CHUNK)
            glv = jnp.exp(jnp.sum(gb_ref[rows, :], axis=0, keepdims=True))
            base = ci * WQK
            for h in range(nh):
                cols = slice(h * HEAD_D, (h + 1) * HEAD_D)
                state = s_ref[h]
                st_ref[h, ci] = state
                r = _gdot(wqk_ref[h, base:base + 2 * CHUNK, :], state, NN)
                vb = u_ref[rows, cols] - r[:CHUNK]
                o_ref[rows, cols] = r[CHUNK:] + _gdot(attn_ref[h, ci], vb, NN)
                vn_ref[rows, cols] = vb
                s_ref[h] = state * glv[:, h:h + 1] + _gdot(wqk_ref[h, base + 2 * CHUNK:base + WQK, :], vb, TN)

    return pl.pallas_call(
        body, grid=(S // rb,),
        in_specs=[pl.BlockSpec((rb, nh * HEAD_D), lambda i: (i, 0)), pl.BlockSpec((nh, 3 * rb, HEAD_D), lambda i: (0, i, 0)),
                  pl.BlockSpec((nh, cpb, CHUNK, CHUNK), lambda i: (0, i, 0, 0)), pl.BlockSpec((rb, LANE), lambda i: (i, 0)),
                  pl.BlockSpec(memory_space=pl.ANY)],
        out_specs=[pl.BlockSpec((rb, nh * HEAD_D), lambda i: (i, 0)), pl.BlockSpec((rb, nh * HEAD_D), lambda i: (i, 0)),
                   pl.BlockSpec((nh, cpb, HEAD_D, HEAD_D), lambda i: (0, i, 0, 0))],
        out_shape=[sds((S, nh * HEAD_D), F32), sds((S, nh * HEAD_D), F32), sds((nh, nc, HEAD_D, HEAD_D), F32)],
        scratch_shapes=[pltpu.VMEM((nh, HEAD_D, HEAD_D), F32)],
        compiler_params=_params(("arbitrary",)), name="gdn_scan_fwd")(u, wqk, attn, gb, dep)


def _gdn_scan_bwd(wqk, attn, gb, states, vn, do, nh, dep):
    S = vn.shape[0]
    nc = S // CHUNK
    rb = min(GDN_ROWS, S)
    cpb = rb // CHUNK
    last = S // rb - 1
    sds = jax.ShapeDtypeStruct

    def body(wqk_ref, attn_ref, gb_ref, st_ref, vn_ref, do_ref, dep_ref, dvn_ref, dw_ref, dqd_ref, dkd_ref, da_ref, dgl_ref,
             ds_ref):
        @pl.when(pl.program_id(0) == 0)
        def _():
            ds_ref[...] = jnp.zeros_like(ds_ref)

        row = lax.broadcasted_iota(jnp.int32, (CHUNK, CHUNK), 0)
        col = lax.broadcasted_iota(jnp.int32, (CHUNK, CHUNK), 1)
        for ci in reversed(range(cpb)):
            rows = slice(ci * CHUNK, (ci + 1) * CHUNK)
            glv = jnp.exp(jnp.sum(gb_ref[rows, :], axis=0, keepdims=True))
            base = ci * WQK
            for h in range(nh):
                cols = slice(h * HEAD_D, (h + 1) * HEAD_D)
                state, dS = st_ref[h, ci], ds_ref[h]
                wq = wqk_ref[h, base:base + 2 * CHUNK, :]
                kd = wqk_ref[h, base + 2 * CHUNK:base + WQK, :]
                vb = vn_ref[rows, cols]
                dob = do_ref[rows, cols]
                dvb = _gdot(attn_ref[h, ci], dob, TN) + _gdot(kd, dS, NN)
                x = _gdot(jnp.concatenate([dob, dvb], axis=0), state, NT)
                dqd_ref[rows, cols] = x[:CHUNK]
                dw_ref[rows, cols] = -x[CHUNK:]
                dvn_ref[rows, cols] = dvb
                da_ref[h, ci] = jnp.where(col <= row, _gdot(dob, vb, NT), 0.0)
                dkd_ref[rows, cols] = _gdot(vb, dS, NT)
                gl = glv[:, h:h + 1]
                dgl = jnp.sum(jnp.sum(state * dS, axis=0, keepdims=True), axis=1, keepdims=True)
                dgl_ref[h, ci] = jnp.broadcast_to(dgl * gl, (1, LANE))
                ds_ref[h] = dS * gl + _gdot(wq, jnp.concatenate([-dvb, dob], axis=0), TN)

    rv = lambda i: last - i
    wide = pl.BlockSpec((rb, nh * HEAD_D), lambda i: (rv(i), 0))
    return pl.pallas_call(
        body, grid=(S // rb,),
        in_specs=[pl.BlockSpec((nh, 3 * rb, HEAD_D), lambda i: (0, rv(i), 0)),
                  pl.BlockSpec((nh, cpb, CHUNK, CHUNK), lambda i: (0, rv(i), 0, 0)),
                  pl.BlockSpec((rb, LANE), lambda i: (rv(i), 0)),
                  pl.BlockSpec((nh, cpb, HEAD_D, HEAD_D), lambda i: (0, rv(i), 0, 0)), wide, wide,
                  pl.BlockSpec(memory_space=pl.ANY)],
        out_specs=[wide, wide, wide, wide, pl.BlockSpec((nh, cpb, CHUNK, CHUNK), lambda i: (0, rv(i), 0, 0)),
                   pl.BlockSpec((nh, cpb, 1, LANE), lambda i: (0, rv(i), 0, 0))],
        out_shape=[sds((S, nh * HEAD_D), F32), sds((S, nh * HEAD_D), F32), sds((S, nh * HEAD_D), F32),
                   sds((S, nh * HEAD_D), F32), sds((nh, nc, CHUNK, CHUNK), F32), sds((nh, nc, 1, LANE), F32)],
        scratch_shapes=[pltpu.VMEM((nh, HEAD_D, HEAD_D), F32)],
        compiler_params=_params(("arbitrary",)), name="gdn_scan_bwd")(wqk, attn, gb, states, vn, do, dep)


def _gdn_local_bwd(qkv, gb, T, u, wqk, dvn, dw, dqd, dkd, dattn, dgl, nh):
    S = qkv.shape[0]
    rb = min(GDN_ROWS, S)
    cpb = rb // CHUNK
    sds = jax.ShapeDtypeStruct

    def body(q_ref, k_ref, v_ref, gb_ref, t_ref, u_ref, wqk_ref, dvn_ref, dw_ref, dqd_ref, dkd_ref, da_ref, dgl_ref,
             dq_ref, dk_ref, dv_ref, dg_ref, db_ref):
        h = pl.program_id(0)
        for ci in range(cpb):
            rows = slice(ci * CHUNK, (ci + 1) * CHUNK)
            q, k, v = q_ref[rows, :], k_ref[rows, :], v_ref[rows, :]
            gbt = gb_ref[rows, :]
            bcol = _col_of(gbt, nh + h)
            T = t_ref[ci]
            L = _chunk_local(q, k, v, _col_of(gbt, h), bcol, T=T)
            strict, eye, decay, eG, eGr, kb = L["strict"], L["eye"], L["decay"], L["eG"], L["eGr"], L["kb"]
            attn, q_dec, k_dec = L["attn"], L["q_dec"], L["k_dec"]
            w = wqk_ref[ci * WQK:ci * WQK + CHUNK, :]
            dq_dec, dk_dec, dattn_c = dqd_ref[rows, :], dkd_ref[rows, :], da_ref[ci]
            drv = _gdot(T, dvn_ref[rows, :], TN)
            drk = _gdot(T, dw_ref[rows, :], TN)
            dA = jnp.where(strict, -(_gdot(drv, u_ref[rows, :], NT) + _gdot(drk, w, NT)), 0.0)
            dM = dA * decay
            dN = dattn_c * decay
            dkb = _gdot(dM, k, NN)
            dq_ref[rows, :] = _gdot(dN, k, NN) + dq_dec * eG
            dk_ref[rows, :] = (drk * (bcol * eG) + _gdot(dM, kb, TN) + dkb * bcol + _gdot(dN, q, TN) + dk_dec * eGr)
            dv_ref[rows, :] = drv * bcol
            db_ref[rows, :] = (jnp.sum(drv * v, axis=1, keepdims=True) + jnp.sum(drk * k, axis=1, keepdims=True) * eG
                               + jnp.sum(dkb * k, axis=1, keepdims=True))
            E = dA * L["A"] + dattn_c * attn
            kd = jnp.sum(dk_dec * k_dec, axis=1, keepdims=True)
            dG = (jnp.sum(dq_dec * q_dec, axis=1, keepdims=True) - kd + jnp.sum(drk * L["rhs_k"], axis=1, keepdims=True)
                  + jnp.sum(E, axis=1, keepdims=True) - _to_col(jnp.sum(E, axis=0, keepdims=True), eye))
            d_last = jnp.sum(kd, axis=0, keepdims=True) + dgl_ref[ci][:, :1]
            dG = dG + jnp.where(L["row"][:, :1] == CHUNK - 1, d_last, 0.0)
            dg_ref[rows, :] = jnp.sum(jnp.where(L["col"] >= L["row"], _to_row(dG, eye), 0.0), axis=1, keepdims=True)

    hd = pl.BlockSpec((rb, HEAD_D), lambda h, i: (i, h))
    cc = pl.BlockSpec((None, cpb, CHUNK, CHUNK), lambda h, i: (h, i, 0, 0))
    col1 = pl.BlockSpec((None, rb, 1), lambda h, i: (h, i, 0))
    return pl.pallas_call(
        body, grid=(nh, S // rb),
        in_specs=[hd, pl.BlockSpec((rb, HEAD_D), lambda h, i: (i, nh + h)),
                  pl.BlockSpec((rb, HEAD_D), lambda h, i: (i, 2 * nh + h)), pl.BlockSpec((rb, LANE), lambda h, i: (i, 0)),
                  cc, hd, pl.BlockSpec((None, 3 * rb, HEAD_D), lambda h, i: (h, i, 0)), hd, hd, hd, hd, cc,
                  pl.BlockSpec((None, cpb, 1, LANE), lambda h, i: (h, i, 0, 0))],
        out_specs=[hd, hd, hd, col1, col1],
        out_shape=[sds((S, nh * HEAD_D), F32)] * 3 + [sds((nh, S, 1), F32)] * 2,
        compiler_params=_params(("parallel", "parallel")), name="gdn_local_bwd")(
            qkv, qkv, qkv, gb, T, u, wqk, dvn, dw, dqd, dkd, dattn, dgl)


def _gated_norm_fwd(o, proj, norm_w, nh, z_blk0):
    S = o.shape[0]

    def body(o_ref, z_ref, w_ref, y_ref):
        ov, z = o_ref[...], z_ref[...]
        r = lax.rsqrt(jnp.mean(ov * ov, axis=1, keepdims=True) + RMS_EPS)
        y_ref[...] = ov * r * w_ref[...] * (z * _sigmoid(z))

    return pl.pallas_call(
        body, grid=(nh,),
        in_specs=[pl.BlockSpec((S, HEAD_D), lambda h: (0, h)), pl.BlockSpec((S, HEAD_D), lambda h: (0, z_blk0 + h)),
                  pl.BlockSpec((1, HEAD_D), lambda h: (0, 0))],
        out_specs=pl.BlockSpec((S, HEAD_D), lambda h: (0, h)),
        out_shape=jax.ShapeDtypeStruct((S, nh * HEAD_D), F32),
        compiler_params=_params(("parallel",)), name="gated_norm_fwd")(o, proj, norm_w)


def _gated_norm_bwd(o, proj, norm_w, dmix, nh, z_blk0, d_blk0):
    S = o.shape[0]
    sds = jax.ShapeDtypeStruct

    def body(o_ref, z_ref, w_ref, dy_ref, do_ref, dz_ref, dw_ref):
        ov, z, w, dy = o_ref[...], z_ref[...], w_ref[...], dy_ref[...]
        r = lax.rsqrt(jnp.mean(ov * ov, axis=1, keepdims=True) + RMS_EPS)
        oh = ov * r
        sg = _sigmoid(z)
        dz_ref[...] = dy * (oh * w) * (sg * (1.0 + z * (1.0 - sg)))
        don = dy * (z * sg)
        @pl.when(pl.program_id(0) == 0)
        def _():
            dw_ref[...] = jnp.zeros_like(dw_ref)

        dw_ref[...] += jnp.sum(don * oh, axis=0, keepdims=True)
        doh = don * w
        do_ref[...] = r * (doh - oh * jnp.mean(doh * oh, axis=1, keepdims=True))

    return pl.pallas_call(
        body, grid=(nh,),
        in_specs=[pl.BlockSpec((S, HEAD_D), lambda h: (0, h)), pl.BlockSpec((S, HEAD_D), lambda h: (0, z_blk0 + h)),
                  pl.BlockSpec((1, HEAD_D), lambda h: (0, 0)), pl.BlockSpec((S, HEAD_D), lambda h: (0, d_blk0 + h))],
        out_specs=[pl.BlockSpec((S, HEAD_D), lambda h: (0, h)), pl.BlockSpec((S, HEAD_D), lambda h: (0, h)),
                   pl.BlockSpec((1, HEAD_D), lambda h: (0, 0))],
        out_shape=[sds((S, nh * HEAD_D), F32), sds((S, nh * HEAD_D), F32), sds((1, HEAD_D), F32)],
        compiler_params=_params(("arbitrary",)), name="gated_norm_bwd")(o, proj, norm_w, dmix)


def _adamw_math(w, g, m, v):
    m = ADAM_B1 * m + (1.0 - ADAM_B1) * g
    v = ADAM_B2 * v + (1.0 - ADAM_B2) * (g * g)
    m_hat = m / (1.0 - ADAM_B1 ** ADAM_STEP)
    v_hat = v / (1.0 - ADAM_B2 ** ADAM_STEP)
    delta = -ADAM_LR * (m_hat / (jnp.sqrt(v_hat) + ADAM_EPS) + ADAM_WD * w)
    return delta, m, v


def _adamw_big(parts, terms, chip, w, m, v, name, tr=256):
    R, C = w.shape
    tr = _tile(R, tr)
    sds = jax.ShapeDtypeStruct

    def body(q_ref, p_ref, t_ref, w_ref, m_ref, v_ref, g_ref, d_ref, nm_ref, nv_ref):
        g = ((p_ref[...].astype(F32) + t_ref[0].astype(F32)) + t_ref[1].astype(F32)) + t_ref[2].astype(F32)
        g_ref[...] = g
        d_ref[...], nm_ref[...], nv_ref[...] = _adamw_math(w_ref[...], g, m_ref[...], v_ref[...])

    spec = pl.BlockSpec((tr, C), lambda i, q_ref: (i, 0))
    grid_spec = pltpu.PrefetchScalarGridSpec(
        num_scalar_prefetch=1, grid=(R // tr,),
        in_specs=[pl.BlockSpec((None, tr, C), lambda i, q_ref: (q_ref[0], i, 0)),
                  pl.BlockSpec((3, tr, C), lambda i, q_ref: (0, i, 0)), spec, spec, spec],
        out_specs=[spec] * 4)
    return pl.pallas_call(body, grid_spec=grid_spec, out_shape=[sds((R, C), F32)] * 4,
                          compiler_params=_params(("parallel",)), name=name)(chip, parts, terms, w, m, v)


def _adamw_small(ws, gs, ms, vs):
    n = len(ws)

    def body(*refs):
        for i in range(n):
            w, g, m, v = (refs[k * n + i][...] for k in range(4))
            d, nm, nv = _adamw_math(w, g, m, v)
            refs[4 * n + i][...] = d
            refs[5 * n + i][...] = nm
            refs[6 * n + i][...] = nv

    shapes = [jax.ShapeDtypeStruct(w.shape, F32) for w in ws]
    vm = pl.BlockSpec(memory_space=pltpu.VMEM)
    outs = pl.pallas_call(body, in_specs=[vm] * (4 * n), out_specs=[vm] * (3 * n), out_shape=shapes * 3,
                          name="adamw_small")(*ws, *gs, *ms, *vs)
    return outs[:n], outs[n:2 * n], outs[2 * n:]


MESH = pl.DeviceIdType.MESH
ANY = pl.BlockSpec(memory_space=pl.ANY)


def _place():
    x, y, c = lax.axis_index("x"), lax.axis_index("y"), lax.axis_index("c")
    return x, y, c, [(1 - x, y), (x, 1 - y), (1 - x, 1 - y)]


def _all_gather(shards):
    n = len(shards)

    def body(*refs):
        ins, outs = refs[:n], refs[n:2 * n]
        send_sems, recv_sems, local_sems = refs[2 * n:]
        x, y, c, chips = _place()
        me, sibling = (x, y, c), (x, y, 1 - c)

        def slot(px, py, pc):
            return 4 * px + 2 * py + pc

        def copy(w, k, block, to, src=None):
            dst = outs[w].at[slot(*block)]
            return pltpu.make_async_remote_copy(src_ref=dst if src is None else src, dst_ref=dst,
                                                send_sem=send_sems.at[w * 7 + k], recv_sem=recv_sems.at[w * 7 + k],
                                                device_id=to, device_id_type=MESH)

        mine = [pltpu.make_async_copy(ins[w], outs[w].at[slot(*me)], local_sems.at[w]) for w in range(n)]
        for cp in mine:
            cp.start()
        first = []
        for w in range(n):
            first.append(copy(w, 0, me, sibling, src=ins[w]))
            first += [copy(w, 1 + j, me, (*chip, c), src=ins[w]) for j, chip in enumerate(chips)]
        for cp in first:
            cp.start()
        passed = []
        for w in range(n):
            for j, chip in enumerate(chips):
                copy(w, 1 + j, (*chip, c), me).wait_recv()
                cp = copy(w, 4 + j, (*chip, c), sibling)
                cp.start()
                passed.append(cp)
        for w in range(n):
            copy(w, 0, sibling, me).wait_recv()
            for j, chip in enumerate(chips):
                copy(w, 4 + j, (*chip, 1 - c), me).wait_recv()
        for cp in first + passed:
            cp.wait_send()
        for cp in mine:
            cp.wait()

    return pl.pallas_call(
        body, in_specs=[ANY] * n, out_specs=[ANY] * n,
        out_shape=[jax.ShapeDtypeStruct((N_DEV,) + s.shape, s.dtype) for s in shards],
        scratch_shapes=[pltpu.SemaphoreType.DMA((7 * n,)), pltpu.SemaphoreType.DMA((7 * n,)),
                        pltpu.SemaphoreType.DMA((n,))],
        name="all_gather_weights")(*shards)


def _sibling_exchange(grads):
    n = len(grads)

    def body(*refs):
        ins, outs = refs[:n], refs[n:2 * n]
        send_sems, recv_sems = refs[2 * n:]
        x, y, c, _ = _place()
        sibling = (x, y, 1 - c)
        copies = []
        for w in range(n):
            for q in range(4):
                copies.append(pltpu.make_async_remote_copy(
                    src_ref=ins[w].at[2 * q + (1 - c)], dst_ref=outs[w].at[q], send_sem=send_sems.at[4 * w + q],
                    recv_sem=recv_sems.at[4 * w + q], device_id=sibling, device_id_type=MESH))
        for cp in copies:
            cp.start()
        for cp in copies:
            cp.wait()

    return pl.pallas_call(
        body, in_specs=[ANY] * n, out_specs=[ANY] * n,
        out_shape=[jax.ShapeDtypeStruct((4,) + g.shape[1:], g.dtype) for g in grads],
        scratch_shapes=[pltpu.SemaphoreType.DMA((4 * n,)), pltpu.SemaphoreType.DMA((4 * n,))],
        name="rs_sibling_exchange")(*grads)


def _chip_sum(grad, recv, core, name, tr=256):
    _, R, C = grad.shape
    tr = _tile(R, tr)

    def body(c_ref, g_ref, r_ref, o_ref):
        o_ref[...] = (g_ref[...].astype(F32) + r_ref[...].astype(F32)).astype(o_ref.dtype)

    grid_spec = pltpu.PrefetchScalarGridSpec(
        num_scalar_prefetch=1, grid=(4, R // tr),
        in_specs=[pl.BlockSpec((None, tr, C), lambda q, i, c_ref: (2 * q + c_ref[0], i, 0)),
                  pl.BlockSpec((None, tr, C), lambda q, i, c_ref: (q, i, 0))],
        out_specs=pl.BlockSpec((None, tr, C), lambda q, i, c_ref: (q, i, 0)))
    return pl.pallas_call(body, grid_spec=grid_spec, out_shape=jax.ShapeDtypeStruct((4, R, C), BF16),
                          compiler_params=_params(("parallel", "parallel")), name=name)(core, grad, recv)


def _chip_exchange(parts):
    n = len(parts)

    def body(*refs):
        ins, outs = refs[:n], refs[n:2 * n]
        send_sems, recv_sems = refs[2 * n:]
        x, y, c, chips = _place()
        copies = []
        for w in range(n):
            for j, (px, py) in enumerate(chips):
                copies.append(pltpu.make_async_remote_copy(
                    src_ref=ins[w].at[2 * px + py], dst_ref=outs[w].at[j], send_sem=send_sems.at[3 * w + j],
                    recv_sem=recv_sems.at[3 * w + j], device_id=(px, py, c), device_id_type=MESH))
        for cp in copies:
            cp.start()
        for cp in copies:
            cp.wait()

    return pl.pallas_call(
        body, in_specs=[ANY] * n, out_specs=[ANY] * n,
        out_shape=[jax.ShapeDtypeStruct((3,) + p.shape[1:], p.dtype) for p in parts],
        scratch_shapes=[pltpu.SemaphoreType.DMA((3 * n,)), pltpu.SemaphoreType.DMA((3 * n,))],
        name="rs_chip_exchange")(*parts)


HBM_SPEC = pl.BlockSpec(memory_space=pltpu.HBM)
SEM_SPEC = pl.BlockSpec(memory_space=pltpu.SEMAPHORE)
DATAFLOW = pltpu.SideEffectType.DATAFLOW_SIDE_EFFECTING


def _split_start(name, bufs, plan, n, after):
    nb = len(bufs)

    def body(*refs):
        send_sems, recv_sems, token = refs[nb + 1], refs[nb + 2], refs[-1]
        for k, (src, dst, to) in enumerate(plan(refs[:nb])):
            pltpu.make_async_remote_copy(src_ref=src, dst_ref=dst, send_sem=send_sems.at[k], recv_sem=recv_sems.at[k],
                                         device_id=to, device_id_type=MESH).start()
        token[...] = jnp.zeros_like(token)

    outs = pl.pallas_call(
        body, name=name,
        out_shape=(pltpu.SemaphoreType.DMA((n,)), pltpu.SemaphoreType.DMA((n,)),
                   *[pltpu.HBM(b.shape, b.dtype) for b in bufs], jax.ShapeDtypeStruct((8, LANE), F32)),
        in_specs=[HBM_SPEC] * nb + [ANY],
        out_specs=(SEM_SPEC, SEM_SPEC, *[HBM_SPEC] * nb, pl.BlockSpec(memory_space=pltpu.VMEM)),
        input_output_aliases={i: 2 + i for i in range(nb)},
        compiler_params=pltpu.CompilerParams(has_side_effects=DATAFLOW))(
            *[pltpu.with_memory_space_constraint(b, pltpu.HBM) for b in bufs], after)
    return outs[0], outs[1], list(outs[2:2 + nb]), outs[-1]


def _split_wait(name, send_sems, recv_sems, bufs, plan, n, after):
    nb = len(bufs)

    def body(*refs):
        send_s, recv_s = refs[nb], refs[nb + 1]
        for k, (src, dst, to) in enumerate(plan(refs[:nb])):
            cp = pltpu.make_async_remote_copy(src_ref=src, dst_ref=dst, send_sem=send_s.at[k], recv_sem=recv_s.at[k],
                                              device_id=to, device_id_type=MESH)
            cp.wait_send()
            cp.wait_recv()

    outs = pl.pallas_call(
        body, name=name, out_shape=tuple(pltpu.HBM(b.shape, b.dtype) for b in bufs),
        in_specs=[HBM_SPEC] * nb + [SEM_SPEC, SEM_SPEC, ANY], out_specs=tuple([HBM_SPEC] * nb),
        input_output_aliases={i: i for i in range(nb)},
        compiler_params=pltpu.CompilerParams(has_side_effects=DATAFLOW))(*bufs, send_sems, recv_sems, after)
    return list(outs)


def _slot(px, py, pc):
    return 4 * px + 2 * py + pc


class _Exchanges:
    def __init__(self, shards, dev, core):
        self.shards, self.dev, self.core = shards, dev, core
        self.n = len(shards)

    def _gather_plan1(self, refs):
        n = self.n
        x, y, c, chips = _place()
        out = []
        for w in range(n):
            dst = refs[n + w].at[_slot(x, y, c)]
            out.append((refs[w], dst, (x, y, 1 - c)))
            out += [(refs[w], dst, (px, py, c)) for px, py in chips]
        return out

    def _gather_plan2(self, refs):
        x, y, c, chips = _place()
        return [(refs[w].at[_slot(px, py, c)],) * 2 + ((x, y, 1 - c),) for w in range(self.n) for px, py in chips]

    def weights_start(self, after):
        lands = [lax.dynamic_update_slice(lax.empty((N_DEV,) + s.shape, s.dtype), s[None], (self.dev, 0, 0))
                 for s in self.shards]
        self.w1 = _split_start("gather_start_1", list(self.shards) + lands, self._gather_plan1, 4 * self.n, after)
        return self.w1[3]

    def weights_mid(self, after):
        send, recv, bufs, _ = self.w1
        lands = _split_wait("gather_wait_1", send, recv, bufs, self._gather_plan1, 4 * self.n, after)[self.n:]
        self.w2 = _split_start("gather_start_2", lands, self._gather_plan2, 3 * self.n, lands[0])
        return self.w2[3]

    def weights_finish(self, after):
        send, recv, bufs, _ = self.w2
        return _split_wait("gather_wait_2", send, recv, bufs, self._gather_plan2, 3 * self.n, after)

    def _reduce_plan1(self, refs):
        n = self.n
        x, y, c, _ = _place()
        return [(refs[w].at[2 * q + (1 - c)], refs[n + w].at[q], (x, y, 1 - c)) for w in range(n) for q in range(4)]

    def _reduce_plan2(self, refs):
        n = self.n
        x, y, c, chips = _place()
        return [(refs[w].at[2 * px + py], refs[n + w].at[j], (px, py, c))
                for w in range(n) for j, (px, py) in enumerate(chips)]

    def grads_start(self, grads):
        lands = [lax.empty((4,) + g.shape[1:], g.dtype) for g in grads]
        self.g1 = _split_start("reduce_start_1", list(grads) + lands, self._reduce_plan1, 4 * self.n, grads[-1])
        return self.g1[3]

    def grads_mid(self, after):
        n = self.n
        send, recv, bufs, _ = self.g1
        bufs = _split_wait("reduce_wait_1", send, recv, bufs, self._reduce_plan1, 4 * n, after)
        core = self.core.reshape(1).astype(jnp.int32)
        self.parts = [_chip_sum(bufs[w], bufs[n + w], core, "reduce_chip_sum_%d" % w) for w in range(n)]
        lands = [lax.empty((3,) + p.shape[1:], p.dtype) for p in self.parts]
        self.g2 = _split_start("reduce_start_2", self.parts + lands, self._reduce_plan2, 3 * n, self.parts[-1])
        return self.g2[3]

    def grads_finish(self, after):
        n = self.n
        send, recv, bufs, _ = self.g2
        bufs = _split_wait("reduce_wait_2", send, recv, bufs, self._reduce_plan2, 3 * n, after)
        self.parts, self.terms = bufs[:n], bufs[n:]


def _all_reduce_small(buf):
    R = buf.shape[0]

    def body(x_ref, o_ref, g_ref, send_sems, recv_sems):
        x, y, c, chips = _place()
        me, sibling = (x, y, c), (x, y, 1 - c)

        def slot(px, py, pc):
            return 4 * px + 2 * py + pc

        def copy(k, block, to, src=None):
            dst = g_ref.at[slot(*block)]
            return pltpu.make_async_remote_copy(src_ref=dst if src is None else src, dst_ref=dst,
                                                send_sem=send_sems.at[k], recv_sem=recv_sems.at[k],
                                                device_id=to, device_id_type=MESH)

        first = [copy(0, me, sibling, src=x_ref)]
        first += [copy(1 + j, me, (*chip, c), src=x_ref) for j, chip in enumerate(chips)]
        for cp in first:
            cp.start()
        g_ref[slot(*me)] = x_ref[...]
        passed = [copy(4 + j, (*chip, c), sibling) for j, chip in enumerate(chips)]
        for j, chip in enumerate(chips):
            copy(1 + j, (*chip, c), me).wait_recv()
            passed[j].start()
        copy(0, sibling, me).wait_recv()
        for j, chip in enumerate(chips):
            copy(4 + j, (*chip, 1 - c), me).wait_recv()
        for cp in first + passed:
            cp.wait_send()
        acc = g_ref[0]
        for s in range(1, N_DEV):
            acc = acc + g_ref[s]
        o_ref[...] = acc

    vm = pl.BlockSpec(memory_space=pltpu.VMEM)
    return pl.pallas_call(
        body, in_specs=[vm], out_specs=vm, out_shape=jax.ShapeDtypeStruct((R, LANE), F32),
        scratch_shapes=[pltpu.VMEM((N_DEV, R, LANE), F32), pltpu.SemaphoreType.DMA((7,)), pltpu.SemaphoreType.DMA((7,))],
        name="all_reduce_small")(buf)


def _pad_cols(a, width):
    return jnp.pad(a, ((0, 0), (0, width - a.shape[1])))


def _local_step(x, target, w_in, conv_w, a_log, dt_bias, delta_norm_w, sinks, rel_bias, ln1_g, ln1_b, ln2_g, ln2_b, ex):
    S, D = x.shape
    aw = D // 2
    hq, hkv, nh = aw // HEAD_A, aw // HEAD_A // GQA, aw // HEAD_D
    kvw = hkv * HEAD_A
    c_q, c_k, c_v, c_d = 0, aw, aw + kvw, aw + 2 * kvw
    c_ab = c_d + 3 * aw
    c_z = c_ab + 2 * nh
    n_in = c_z + aw
    assert w_in.shape == (D, n_in), (w_in.shape, n_in)
    w_p = jnp.concatenate([w_in[:, :c_ab], w_in[:, c_z:], _pad_cols(w_in[:, c_ab:c_z], LANE)], axis=1)
    p_z, p_ab = c_ab, c_ab + aw
    n_p = p_ab + LANE

    proj = _matmul(x, w_p, NN, name="proj", tn=1152, deps=(ex.weights_start(w_p),))
    bias = _attn_bias(rel_bias.T)
    attn_out = _attn_fwd(proj, bias, sinks.reshape(-1), hq, 0, c_k // kvw, c_v // kvw)
    conv2 = conv_w.reshape(CONV_W, 3 * aw)
    qkv = _gdn_prep_fwd(proj, conv2, nh, c_d // HEAD_D)
    ab = proj[:, p_ab:]
    al, dt = _pad_cols(a_log, LANE), _pad_cols(dt_bias, LANE)
    gb = _gates_fwd(ab, al, dt, nh)
    u_d, wqk, attn_d, t_d = _gdn_local_fwd(qkv, gb, nh)
    o_d, vn, states = _gdn_scan_fwd(u_d, wqk, attn_d, gb, nh, ex.weights_mid(u_d))
    delta_out = _gated_norm_fwd(o_d, proj, delta_norm_w, nh, p_z // HEAD_D)
    mix = jnp.concatenate([attn_out, delta_out], axis=1)
    w_o_g, w_up_g, w_down_g = ex.weights_finish(mix)
    w_o, w_down = w_o_g.reshape(D, D), w_down_g.reshape(-1, D)
    mixed = _matmul(mix, w_o, NN, name="out_proj")
    h1 = _ln1_fwd(x, mixed, ln1_g, ln1_b)
    u = _matmul(h1, w_up_g, NN, name="mlp_up", b_groups=True)
    mlp = _matmul(u, w_down, NN, name="mlp_down", a_fn=_relu_sq)
    dr2, loss_row, dln2_g, dln2_b = _ln2_loss(h1, mlp, ln2_g, ln2_b, target)

    du = _matmul(dr2, w_down, NT, name="d_mlp_act", epi=_relu_sq_grad, epi_in=(u,))
    dw_down = _matmul(u, dr2, TN, name="dw_down", a_fn=_relu_sq, out_dtype=BF16)
    dw_up = _matmul(h1, du, TN, name="dw_up", out_dtype=BF16, out_groups=N_DEV)
    dh_mlp = _matmul(du, w_up_g, NT, name="d_h1", b_groups=True)
    dr1, dln1_g, dln1_b = _ln1_bwd(x, mixed, ln1_g, dr2, dh_mlp)
    dw_o = _matmul(mix, dr1, TN, name="dw_o", out_dtype=BF16)
    tok = ex.grads_start([dw_o.reshape(N_DEV, -1, D), dw_up, dw_down.reshape(N_DEV, -1, D)])
    dmix = _matmul(dr1, w_o, NT, name="d_mix", deps=(tok,))
    dq_a, dk_a, dv_a, dbias, dsink = _attn_bwd(proj, bias, sinks.reshape(-1), attn_out, dmix, hq, 0, c_k // kvw, c_v // kvw)
    drel = _rel_bias_grad(dbias, hq)
    do_d, dz, dnw = _gated_norm_bwd(o_d, proj, delta_norm_w, dmix, nh, p_z // HEAD_D, aw // HEAD_D)
    dvn_s, dw_s, dqd, dkd, dattn_d, dgl = _gdn_scan_bwd(wqk, attn_d, gb, states, vn, do_d, nh, ex.grads_mid(dq_a))
    dqn, dkn, dvn, dg, dbeta = _gdn_local_bwd(qkv, gb, t_d, u_d, wqk, dvn_s, dw_s, dqd, dkd, dattn_d, dgl, nh)
    dgb = _pad_cols(jnp.concatenate([dg.reshape(nh, S).T, dbeta.reshape(nh, S).T], axis=1), LANE)
    dab, da_log, ddt_bias = _gates_bwd(ab, al, dt, dgb, nh)
    dqkv_d, dconv = _gdn_prep_bwd(proj, conv2, jnp.concatenate([dqn, dkn, dvn], axis=1), nh, c_d // HEAD_D)
    dproj = jnp.concatenate([dq_a, dk_a, dv_a, dqkv_d, dz, dab], axis=1)
    dw_p = _matmul(x, dproj, TN, name="dw_in", tn=640)
    dx_proj = _matmul(dproj, w_p, NT, name="d_x", tk=1920)
    grad_x = _grad_x(dr1, dx_proj)
    ex.grads_finish(grad_x)
    dw_in = jnp.concatenate([dw_p[:, :p_z], dw_p[:, p_ab:p_ab + 2 * nh], dw_p[:, p_z:p_ab]], axis=1)

    small = dict(conv_w=dconv, a_log=da_log[:, :nh], dt_bias=ddt_bias[:, :nh], delta_norm_w=dnw,
                 attn_sinks=dsink[:, 0].reshape(1, hq), rel_bias=drel[:, :, 0].T,
                 ln1_g=dln1_g, ln1_b=dln1_b, ln2_g=dln2_g, ln2_b=dln2_b)
    return loss_row, grad_x, dw_in, small


SMALL_ORDER = ("conv_w", "a_log", "dt_bias", "delta_norm_w", "attn_sinks", "rel_bias", "ln1_g", "ln1_b", "ln2_g", "ln2_b")


def _pack_small(loss_row, small):
    parts = [loss_row.reshape(-1)]
    for k in SMALL_ORDER:
        flat = small[k].reshape(-1)
        parts.append(jnp.pad(flat, (0, (-flat.shape[0]) % LANE)))
    flat = jnp.concatenate(parts)
    flat = jnp.pad(flat, (0, (-flat.shape[0]) % (8 * LANE)))
    return flat.reshape(-1, LANE)


def _unpack_small(buf, small_shapes):
    flat = buf.reshape(-1)
    loss = flat[0]
    off = LANE
    out = {}
    for k in SMALL_ORDER:
        n = int(np.prod(small_shapes[k]))
        out[k] = flat[off:off + n].reshape(small_shapes[k])
        off += n + (-n) % LANE
    return loss, out


def kernel(x, w_in, conv_w, a_log, dt_bias, delta_norm_w, attn_sinks, rel_bias, w_o, ln1_g, ln1_b, w_up, w_down, ln2_g, ln2_b, loss_target, m_w_in, m_conv_w, m_a_log, m_dt_bias, m_delta_norm_w, m_attn_sinks, m_rel_bias, m_w_o, m_ln1_g, m_ln1_b, m_w_up, m_w_down, m_ln2_g, m_ln2_b, v_w_in, v_conv_w, v_a_log, v_dt_bias, v_delta_norm_w, v_attn_sinks, v_rel_bias, v_w_o, v_ln1_g, v_ln1_b, v_w_up, v_w_down, v_ln2_g, v_ln2_b):
    S, D = x.shape[1], x.shape[2]
    core = lax.axis_index("c")
    dev = 4 * lax.axis_index("x") + 2 * lax.axis_index("y") + core

    (w_in_g,) = _all_gather([w_in[0].astype(BF16)])
    n_in = N_DEV * w_in.shape[2]
    w_in_full = jnp.transpose(w_in_g, (1, 0, 2)).reshape(D, n_in)
    ex = _Exchanges([w_o[0].astype(BF16), w_up[0].astype(BF16), w_down[0].astype(BF16)], dev, core)

    cw_sh = conv_w.shape[3]
    conv_place = lax.dynamic_update_slice(jnp.zeros((CONV_W, N_DEV * cw_sh), F32), conv_w[0, :, 0, :], (0, dev * cw_sh))
    conv_full = _all_reduce_small(jnp.pad(conv_place.reshape(-1, LANE), ((0, (-conv_place.size // LANE) % 8), (0, 0))))
    conv_full = conv_full[:conv_place.size // LANE].reshape(CONV_W, N_DEV * cw_sh)

    loss_row, grad_x, dw_in, small = _local_step(
        x[0], loss_target[0], w_in_full, conv_full, a_log, dt_bias, delta_norm_w, attn_sinks, rel_bias,
        ln1_g, ln1_b, ln2_g, ln2_b, ex)

    dw_in_g = jnp.transpose(dw_in.reshape(D, N_DEV, -1), (1, 0, 2)).astype(BF16)
    core_arr = core.reshape(1).astype(jnp.int32)
    chip_arr = (dev // 2).reshape(1).astype(jnp.int32)
    (recv,) = _sibling_exchange([dw_in_g])
    part_in = _chip_sum(dw_in_g, recv, core_arr, "rs_chip_sum_in")
    (terms_in,) = _chip_exchange([part_in])
    parts = [part_in] + ex.parts
    terms = [terms_in] + ex.terms
    big = {}
    for i, (name, w, m, v) in enumerate((("w_in", w_in, m_w_in, v_w_in), ("w_o", w_o, m_w_o, v_w_o),
                                         ("w_up", w_up, m_w_up, v_w_up), ("w_down", w_down, m_w_down, v_w_down))):
        outs = _adamw_big(parts[i], terms[i], chip_arr, w[0], m[0], v[0], "adamw_" + name)
        big[name] = [o[None] for o in outs]

    small_shapes = {k: v.shape for k, v in small.items()}
    loss, small = _unpack_small(_all_reduce_small(_pack_small(loss_row, small)), small_shapes)
    small["conv_w"] = lax.dynamic_slice(small["conv_w"], (0, dev * cw_sh), (CONV_W, cw_sh))
    small["rel_bias"] = small["rel_bias"].reshape(rel_bias.shape)
    p2 = dict(conv_w=(conv_w, m_conv_w, v_conv_w), a_log=(a_log, m_a_log, v_a_log), dt_bias=(dt_bias, m_dt_bias, v_dt_bias),
              delta_norm_w=(delta_norm_w, m_delta_norm_w, v_delta_norm_w), attn_sinks=(attn_sinks, m_attn_sinks, v_attn_sinks),
              rel_bias=(rel_bias, m_rel_bias, v_rel_bias), ln1_g=(ln1_g, m_ln1_g, v_ln1_g), ln1_b=(ln1_b, m_ln1_b, v_ln1_b),
              ln2_g=(ln2_g, m_ln2_g, v_ln2_g), ln2_b=(ln2_b, m_ln2_b, v_ln2_b))
    two_d = lambda a: a.reshape(-1, a.shape[-1])
    ws = [two_d(p2[k][0]) for k in SMALL_ORDER]
    gs = [two_d(small[k]) for k in SMALL_ORDER]
    ms = [two_d(p2[k][1]) for k in SMALL_ORDER]
    vs = [two_d(p2[k][2]) for k in SMALL_ORDER]
    ds, nms, nvs = _adamw_small(ws, gs, ms, vs)
    res = {}
    for i, k in enumerate(SMALL_ORDER):
        shp = p2[k][0].shape
        res[k] = [gs[i].reshape(shp), ds[i].reshape(shp), nms[i].reshape(shp), nvs[i].reshape(shp)]
    res.update(big)
    order = ("w_in", "conv_w", "a_log", "dt_bias", "delta_norm_w", "attn_sinks", "rel_bias", "w_o", "ln1_g", "ln1_b",
             "w_up", "w_down", "ln2_g", "ln2_b")
    return (loss, grad_x[None], *[res[k][0] for k in order], *[res[k][1] for k in order],
            *[res[k][2] for k in order], *[res[k][3] for k in order])
```

```python
import functools
import math

import numpy as np
import jax
import jax.numpy as jnp
from jax import lax
from jax.experimental import pallas as pl
from jax.experimental.pallas import tpu as pltpu

F32 = jnp.float32
BF16 = jnp.bfloat16
HIGHEST = lax.Precision.HIGHEST

N_DEV = 8
HEAD_A = 64
GQA = 4
BLK = 128
N_BUCKETS = 32
MAX_DISTANCE = 128
HEAD_D = 128
CONV_W = 4
CHUNK = 64
NEG_INF = -1e30
LN_EPS = 1e-5
RMS_EPS = 1e-6
DN_ALPHA = 2.0 ** 0.25
ADAM_LR, ADAM_B1, ADAM_B2, ADAM_EPS, ADAM_WD, ADAM_STEP = 0.001, 0.9, 0.999, 1e-08, 0.01, 10

LANE = 128
VMEM_LIMIT = 56 * 1024 * 1024

NN = ((1,), (0,))
NT = ((1,), (1,))
TN = ((0,), (0,))


def _dot(a, b, dims, prec=None):
    return lax.dot_general(a, b, (dims, ((), ())), precision=prec, preferred_element_type=F32)


def _tile(dim, pref):
    if dim <= pref:
        return dim
    t = (pref // LANE) * LANE
    while t > LANE and dim % t:
        t -= LANE
    assert dim % t == 0, (dim, pref)
    return t


def _params(sem):
    return pltpu.CompilerParams(dimension_semantics=sem, vmem_limit_bytes=VMEM_LIMIT)


def _matmul(a, b, dims, *, name, out_dtype=F32, tm=1024, tn=1024, tk=2048, a_fn=None, epi=None, epi_in=(),
            b_groups=None, out_groups=None, deps=()):
    (ca,), (cb,) = dims
    M, K = a.shape[1 - ca], a.shape[ca]
    if b_groups:
        G, R, C = b.shape
        bshape = (R, G * C)
    else:
        bshape = b.shape
    N = bshape[1 - cb]
    assert bshape[cb] == K, (a.shape, b.shape, dims)
    tm, tk = _tile(M, tm), _tile(K, tk)
    if b_groups:
        lim = C if cb == 0 else tn
        tn = _tile(N, min(tn, lim))
        if cb == 1:
            tk = _tile(K, min(tk, C))
    else:
        tn = _tile(N, tn)
    if out_groups:
        tn = _tile(N, min(tn, N // out_groups))
    nk = K // tk

    def body(*refs):
        a_ref, b_ref = refs[0], refs[1]
        e_refs = refs[2:2 + len(epi_in)]
        o_ref = refs[2 + len(epi_in) + len(deps)]
        acc_ref = refs[3 + len(epi_in) + len(deps)] if nk > 1 else None
        k = pl.program_id(2)
        av = a_ref[...]
        if a_fn is not None:
            av = a_fn(av)
        prod = _dot(av.astype(BF16), b_ref[...].astype(BF16), dims)

        def finish(r):
            if epi is not None:
                r = epi(r, *[e[...] for e in e_refs])
            o_ref[...] = r.astype(out_dtype)

        if nk == 1:
            finish(prod)
            return

        @pl.when(k == 0)
        def _():
            acc_ref[...] = prod

        @pl.when(k > 0)
        def _():
            acc_ref[...] += prod

        @pl.when(k == nk - 1)
        def _():
            finish(acc_ref[...])

    a_spec = (pl.BlockSpec((tm, tk), lambda i, j, k: (i, k)) if ca == 1
              else pl.BlockSpec((tk, tm), lambda i, j, k: (k, i)))
    if b_groups:
        if cb == 0:
            per = C // tn
            b_spec = pl.BlockSpec((None, tk, tn), lambda i, j, k: (j // per, k, j % per))
        else:
            per = C // tk
            b_spec = pl.BlockSpec((None, tn, tk), lambda i, j, k: (k // per, j, k % per))
    else:
        b_spec = (pl.BlockSpec((tk, tn), lambda i, j, k: (k, j)) if cb == 0
                  else pl.BlockSpec((tn, tk), lambda i, j, k: (j, k)))
    e_specs = [pl.BlockSpec((tm, tn), lambda i, j, k: (i, j)) for _ in epi_in]
    if out_groups:
        per_o = (N // out_groups) // tn
        o_spec = pl.BlockSpec((None, tm, tn), lambda i, j, k: (j // per_o, i, j % per_o))
        o_shape = jax.ShapeDtypeStruct((out_groups, M, N // out_groups), out_dtype)
    else:
        o_spec = pl.BlockSpec((tm, tn), lambda i, j, k: (i, j))
        o_shape = jax.ShapeDtypeStruct((M, N), out_dtype)
    return pl.pallas_call(
        body, grid=(M // tm, N // tn, nk), out_specs=o_spec,
        in_specs=[a_spec, b_spec] + e_specs + [pl.BlockSpec(memory_space=pl.ANY)] * len(deps),
        out_shape=o_shape, scratch_shapes=[pltpu.VMEM((tm, tn), F32)] if nk > 1 else [],
        compiler_params=_params(("parallel", "parallel", "arbitrary")), name=name)(a, b, *epi_in, *deps)


def _relu_sq(u):
    r = jnp.maximum(u, 0.0)
    return r * r


def _relu_sq_grad(acc, u):
    return acc * (2.0 * jnp.maximum(u, 0.0))


def _ln_stats(r):
    mu = jnp.mean(r, axis=-1, keepdims=True)
    xc = r - mu
    var = jnp.mean(xc * xc, axis=-1, keepdims=True)
    rstd = lax.rsqrt(var + LN_EPS)
    return xc * rstd, rstd


def _ln_bwd(dy, xhat, rstd, g):
    dxh = dy * g
    m1 = jnp.mean(dxh, axis=-1, keepdims=True)
    m2 = jnp.mean(dxh * xhat, axis=-1, keepdims=True)
    return rstd * (dxh - m1 - xhat * m2)


def _row_call(body, ins, row_ins, outs, acc_outs, name, tr=256):
    S = ins[0].shape[0]
    tr = min(tr, S)
    n_in, n_row, n_out = len(ins), len(row_ins), len(outs)

    def wrapped(*refs):
        i = pl.program_id(0)
        acc_refs = refs[n_in + n_row + n_out:]

        @pl.when(i == 0)
        def _():
            for r in acc_refs:
                r[...] = jnp.zeros_like(r)

        body(*refs)

    in_specs = [pl.BlockSpec((tr, a.shape[1]), lambda i: (i, 0)) for a in ins]
    in_specs += [pl.BlockSpec(a.shape, lambda i: (0, 0)) for a in row_ins]
    out_specs = [pl.BlockSpec((tr, s.shape[1]), lambda i: (i, 0)) for s in outs]
    out_specs += [pl.BlockSpec(s.shape, lambda i: (0, 0)) for s in acc_outs]
    return pl.pallas_call(wrapped, grid=(S // tr,), in_specs=in_specs, out_specs=out_specs,
                          out_shape=list(outs) + list(acc_outs),
                          compiler_params=_params(("arbitrary",)), name=name)(*ins, *row_ins)


def _ln1_fwd(x, mixed, g, b):
    def body(x_ref, m_ref, g_ref, b_ref, h_ref):
        xhat, _ = _ln_stats(DN_ALPHA * x_ref[...] + m_ref[...])
        h_ref[...] = xhat * g_ref[...] + b_ref[...]
    return _row_call(body, [x, mixed], [g, b], [jax.ShapeDtypeStruct(x.shape, F32)], [], "ln1_fwd")[0]


def _ln2_loss(h1, mlp, g, b, target):
    S, D = h1.shape
    sds = jax.ShapeDtypeStruct

    def body(h_ref, m_ref, t_ref, g_ref, b_ref, dr_ref, loss_ref, dg_ref, db_ref):
        xhat, rstd = _ln_stats(DN_ALPHA * h_ref[...] + m_ref[...])
        gv = g_ref[...]
        err = xhat * gv + b_ref[...] - t_ref[...]
        loss_ref[...] += jnp.sum(jnp.sum(err * err, axis=0, keepdims=True), axis=1, keepdims=True) * (0.5 / D)
        dy = err * (1.0 / D)
        dg_ref[...] += jnp.sum(dy * xhat, axis=0, keepdims=True)
        db_ref[...] += jnp.sum(dy, axis=0, keepdims=True)
        dr_ref[...] = _ln_bwd(dy, xhat, rstd, gv)

    return _row_call(body, [h1, mlp, target], [g, b], [sds((S, D), F32)],
                     [sds((1, LANE), F32), sds((1, D), F32), sds((1, D), F32)], "ln2_loss")


def _ln1_bwd(x, mixed, g, dr2, dh_mlp):
    S, D = x.shape
    sds = jax.ShapeDtypeStruct

    def body(x_ref, m_ref, dr2_ref, dh_ref, g_ref, dr_ref, dg_ref, db_ref):
        xhat, rstd = _ln_stats(DN_ALPHA * x_ref[...] + m_ref[...])
        dy = DN_ALPHA * dr2_ref[...] + dh_ref[...]
        dg_ref[...] += jnp.sum(dy * xhat, axis=0, keepdims=True)
        db_ref[...] += jnp.sum(dy, axis=0, keepdims=True)
        dr_ref[...] = _ln_bwd(dy, xhat, rstd, g_ref[...])

    return _row_call(body, [x, mixed, dr2, dh_mlp], [g], [sds((S, D), F32)],
                     [sds((1, D), F32), sds((1, D), F32)], "ln1_bwd")


def _grad_x(dr1, dx_proj, dep):
    def body(a_ref, b_ref, dep_ref, o_ref):
        o_ref[...] = DN_ALPHA * a_ref[...] + b_ref[...]
    return _row_call(body, [dr1, dx_proj], [dep], [jax.ShapeDtypeStruct(dr1.shape, F32)], [], "grad_x")[0]


def _bucket_table():
    qi = np.arange(BLK, dtype=np.int32)[:, None]
    kj = np.arange(2 * BLK, dtype=np.int32)[None, :]
    dist = qi + BLK - kj
    n = np.maximum(dist, 0)
    max_exact = N_BUCKETS // 2
    nf = np.maximum(n, 1).astype(np.float32)
    large = max_exact + (np.log(nf / np.float32(max_exact)) / np.float32(math.log(MAX_DISTANCE / max_exact))
                         * np.float32(N_BUCKETS - max_exact)).astype(np.int32)
    large = np.minimum(large, N_BUCKETS - 1)
    bucket = np.where(n < max_exact, n, large)
    return np.where((dist >= 0) & (dist < BLK), bucket, -1).astype(np.int32)


def _attn_bias(rel_bias_t):
    hq = rel_bias_t.shape[0]
    bucket = jnp.asarray(_bucket_table())

    def body(rb_ref, bk_ref, o_ref):
        h = pl.program_id(0)
        bk = bk_ref[...]
        acc = jnp.zeros((BLK, 2 * BLK), F32)
        for b in range(N_BUCKETS):
            acc = jnp.where(bk == b, rb_ref[h, b], acc)
        o_ref[...] = acc

    return pl.pallas_call(
        body, grid=(hq,),
        in_specs=[pl.BlockSpec(memory_space=pltpu.SMEM), pl.BlockSpec((BLK, 2 * BLK), lambda h: (0, 0))],
        out_specs=pl.BlockSpec((BLK, 2 * BLK), lambda h: (h, 0)),
        out_shape=jax.ShapeDtypeStruct((hq * BLK, 2 * BLK), F32),
        compiler_params=_params(("arbitrary",)), name="attn_bias")(rel_bias_t, bucket)


def _attn_probs(q, kc, kp, bias, sink, mask_c, mask_p):
    lc = jnp.where(mask_c, _dot(q, kc, NT) + bias[:, BLK:], NEG_INF)
    lp = jnp.where(mask_p, _dot(q, kp, NT) + bias[:, :BLK], NEG_INF)
    m = jnp.maximum(jnp.maximum(jnp.max(lc, axis=1, keepdims=True), jnp.max(lp, axis=1, keepdims=True)), sink)
    pc, pp, ps = jnp.exp(lc - m), jnp.exp(lp - m), jnp.exp(sink - m)
    inv = 1.0 / (jnp.sum(pc, axis=1, keepdims=True) + jnp.sum(pp, axis=1, keepdims=True) + ps)
    return pc, pp, ps, inv


def _attn_masks(n):
    qi = lax.broadcasted_iota(jnp.int32, (BLK, BLK), 0)
    kj = lax.broadcasted_iota(jnp.int32, (BLK, BLK), 1)
    return kj <= qi, (kj > qi) & (n > 0)


def _attn_fwd(proj, bias, sinks, hq, q_blk, k_blk, v_blk):
    S = proj.shape[0]
    hkv = hq // GQA
    wq, wk = hq * HEAD_A, hkv * HEAD_A

    def body(q_ref, k_ref, v_ref, bias_ref, sink_ref, o_ref):
        n = pl.program_id(0)
        cur = pl.multiple_of(n * BLK, BLK)
        prev = pl.multiple_of(jnp.maximum(n - 1, 0) * BLK, BLK)
        mask_c, mask_p = _attn_masks(n)
        for h4 in range(hkv):
            cs = slice(h4 * HEAD_A, (h4 + 1) * HEAD_A)
            kc, kp = k_ref[pl.ds(cur, BLK), cs].astype(BF16), k_ref[pl.ds(prev, BLK), cs].astype(BF16)
            vc, vp = v_ref[pl.ds(cur, BLK), cs].astype(BF16), v_ref[pl.ds(prev, BLK), cs].astype(BF16)
            for g in range(GQA):
                h = h4 * GQA + g
                hs = slice(h * HEAD_A, (h + 1) * HEAD_A)
                q = (q_ref[:, hs] * (HEAD_A ** -0.5)).astype(BF16)
                pc, pp, _, inv = _attn_probs(q, kc, kp, bias_ref[h * BLK:(h + 1) * BLK, :], sink_ref[h], mask_c, mask_p)
                o_ref[:, hs] = (_dot(pc.astype(BF16), vc, NN) + _dot(pp.astype(BF16), vp, NN)) * inv

    return pl.pallas_call(
        body, grid=(S // BLK,),
        in_specs=[pl.BlockSpec((BLK, wq), lambda n: (n, q_blk)), pl.BlockSpec((S, wk), lambda n: (0, k_blk)),
                  pl.BlockSpec((S, wk), lambda n: (0, v_blk)), pl.BlockSpec((hq * BLK, 2 * BLK), lambda n: (0, 0)),
                  pl.BlockSpec(memory_space=pltpu.SMEM)],
        out_specs=pl.BlockSpec((BLK, wq), lambda n: (n, 0)),
        out_shape=jax.ShapeDtypeStruct((S, wq), F32),
        compiler_params=_params(("arbitrary",)), name="attn_fwd")(proj, proj, proj, bias, sinks)


def _attn_bwd(proj, bias, sinks, out, dmix, hq, q_blk, k_blk, v_blk):
    S = proj.shape[0]
    hkv = hq // GQA
    wq, wk = hq * HEAD_A, hkv * HEAD_A
    sds = jax.ShapeDtypeStruct

    def body(q_ref, k_ref, v_ref, bias_ref, sink_ref, o_ref, do_ref, dq_ref, dk_ref, dv_ref, dbias_ref, dsink_ref):
        n = pl.program_id(0)

        @pl.when(n == 0)
        def _():
            dk_ref[...] = jnp.zeros_like(dk_ref)
            dv_ref[...] = jnp.zeros_like(dv_ref)
            dbias_ref[...] = jnp.zeros_like(dbias_ref)
            dsink_ref[...] = jnp.zeros_like(dsink_ref)

        cur = pl.multiple_of(n * BLK, BLK)
        prev = pl.multiple_of(jnp.maximum(n - 1, 0) * BLK, BLK)
        mask_c, mask_p = _attn_masks(n)
        for h4 in range(hkv):
            cs = slice(h4 * HEAD_A, (h4 + 1) * HEAD_A)
            kc, kp = k_ref[pl.ds(cur, BLK), cs].astype(BF16), k_ref[pl.ds(prev, BLK), cs].astype(BF16)
            vc, vp = v_ref[pl.ds(cur, BLK), cs].astype(BF16), v_ref[pl.ds(prev, BLK), cs].astype(BF16)
            dkc = jnp.zeros((BLK, HEAD_A), F32)
            dkp = jnp.zeros((BLK, HEAD_A), F32)
            dvc = jnp.zeros((BLK, HEAD_A), F32)
            dvp = jnp.zeros((BLK, HEAD_A), F32)
            for g in range(GQA):
                h = h4 * GQA + g
                hs = slice(h * HEAD_A, (h + 1) * HEAD_A)
                rows = slice(h * BLK, (h + 1) * BLK)
                q = (q_ref[:, hs] * (HEAD_A ** -0.5)).astype(BF16)
                pc, pp, ps, inv = _attn_probs(q, kc, kp, bias_ref[rows, :], sink_ref[h], mask_c, mask_p)
                pc, pp, ps = pc * inv, pp * inv, ps * inv
                do = do_ref[:, hs]
                delta = jnp.sum(do * o_ref[:, hs], axis=1, keepdims=True)
                dob = do.astype(BF16)
                dsc = pc * (_dot(dob, vc, NT) - delta)
                dsp = pp * (_dot(dob, vp, NT) - delta)
                dsink_ref[h:h + 1, :] += jnp.broadcast_to(jnp.sum(-ps * delta, axis=0, keepdims=True), (1, LANE))
                dbias_ref[rows, BLK:] += dsc
                dbias_ref[rows, :BLK] += dsp
                dscb, dspb = dsc.astype(BF16), dsp.astype(BF16)
                dq_ref[:, hs] = (_dot(dscb, kc, NN) + _dot(dspb, kp, NN)) * (HEAD_A ** -0.5)
                dkc += _dot(dscb, q, TN)
                dkp += _dot(dspb, q, TN)
                dvc += _dot(pc.astype(BF16), dob, TN)
                dvp += _dot(pp.astype(BF16), dob, TN)
            dk_ref[pl.ds(cur, BLK), cs] += dkc
            dk_ref[pl.ds(prev, BLK), cs] += dkp
            dv_ref[pl.ds(cur, BLK), cs] += dvc
            dv_ref[pl.ds(prev, BLK), cs] += dvp

    return pl.pallas_call(
        body, grid=(S // BLK,),
        in_specs=[pl.BlockSpec((BLK, wq), lambda n: (n, q_blk)), pl.BlockSpec((S, wk), lambda n: (0, k_blk)),
                  pl.BlockSpec((S, wk), lambda n: (0, v_blk)), pl.BlockSpec((hq * BLK, 2 * BLK), lambda n: (0, 0)),
                  pl.BlockSpec(memory_space=pltpu.SMEM),
                  pl.BlockSpec((BLK, wq), lambda n: (n, 0)), pl.BlockSpec((BLK, wq), lambda n: (n, 0))],
        out_specs=[pl.BlockSpec((BLK, wq), lambda n: (n, 0)), pl.BlockSpec((S, wk), lambda n: (0, 0)),
                   pl.BlockSpec((S, wk), lambda n: (0, 0)), pl.BlockSpec((hq * BLK, 2 * BLK), lambda n: (0, 0)),
                   pl.BlockSpec((hq, LANE), lambda n: (0, 0))],
        out_shape=[sds((S, wq), F32), sds((S, wk), F32), sds((S, wk), F32), sds((hq * BLK, 2 * BLK), F32),
                   sds((hq, LANE), F32)],
        compiler_params=_params(("arbitrary",)), name="attn_bwd")(proj, proj, proj, bias, sinks, out, dmix)


def _rel_bias_grad(dbias, hq):
    bucket = jnp.asarray(_bucket_table())

    def body(d_ref, bk_ref, o_ref):
        d = d_ref[...]
        bk = bk_ref[...]
        rows = [jnp.sum(jnp.where(bk == b, d, 0.0), axis=0, keepdims=True) for b in range(N_BUCKETS)]
        tot = jnp.sum(jnp.concatenate(rows, axis=0), axis=1, keepdims=True)
        o_ref[...] = jnp.broadcast_to(tot, (N_BUCKETS, LANE))

    return pl.pallas_call(
        body, grid=(hq,),
        in_specs=[pl.BlockSpec((BLK, 2 * BLK), lambda h: (h, 0)), pl.BlockSpec((BLK, 2 * BLK), lambda h: (0, 0))],
        out_specs=pl.BlockSpec((None, N_BUCKETS, LANE), lambda h: (h, 0, 0)),
        out_shape=jax.ShapeDtypeStruct((hq, N_BUCKETS, LANE), F32),
        compiler_params=_params(("arbitrary",)), name="rel_bias_grad")(dbias, bucket)


def _sigmoid(x):
    return 1.0 / (1.0 + jnp.exp(-x))


def _shift_rows(x, s):
    n = x.shape[0]
    row = lax.broadcasted_iota(jnp.int32, x.shape, 0)
    if s > 0:
        return jnp.where(row >= s, pltpu.roll(x, s, 0), 0.0)
    return jnp.where(row < n + s, pltpu.roll(x, n + s, 0), 0.0)


def _conv_silu_norm(xv, w, j, nh):
    c = w[CONV_W - 1:CONV_W, :] * xv
    for s in range(1, CONV_W):
        c = c + w[CONV_W - 1 - s:CONV_W - s, :] * _shift_rows(xv, s)
    sg = _sigmoid(c)
    a = c * sg
    r = lax.rsqrt(jnp.sum(a * a, axis=1, keepdims=True) + RMS_EPS)
    scale = jnp.where(j < nh, HEAD_D ** -0.5, 1.0)
    is_norm = j < 2 * nh
    y = jnp.where(is_norm, a * (r * scale), a)
    return c, sg, a, r, scale, is_norm, y


def _gdn_prep_fwd(proj, conv_w, nh, blk0):
    S = proj.shape[0]

    def body(x_ref, w_ref, o_ref):
        j = pl.program_id(0)
        o_ref[...] = _conv_silu_norm(x_ref[...], w_ref[...], j, nh)[-1]

    return pl.pallas_call(
        body, grid=(3 * nh,),
        in_specs=[pl.BlockSpec((S, HEAD_D), lambda j: (0, blk0 + j)), pl.BlockSpec((CONV_W, HEAD_D), lambda j: (0, j))],
        out_specs=pl.BlockSpec((S, HEAD_D), lambda j: (0, j)),
        out_shape=jax.ShapeDtypeStruct((S, 3 * nh * HEAD_D), F32),
        compiler_params=_params(("parallel",)), name="gdn_prep_fwd")(proj, conv_w)


def _gdn_prep_bwd(proj, conv_w, dqkv, nh, blk0):
    S = proj.shape[0]
    sds = jax.ShapeDtypeStruct

    def body(x_ref, w_ref, dy_ref, dx_ref, dw_ref):
        j = pl.program_id(0)
        xv, w = x_ref[...], w_ref[...]
        c, sg, a, r, scale, is_norm, _ = _conv_silu_norm(xv, w, j, nh)
        dy = dy_ref[...]
        rs = r * scale
        da_n = rs * dy - a * (r * r * rs) * jnp.sum(dy * a, axis=1, keepdims=True)
        da = jnp.where(is_norm, da_n, dy)
        dc = da * (sg * (1.0 + c * (1.0 - sg)))
        dx = w[CONV_W - 1:CONV_W, :] * dc
        dws = [jnp.sum(dc * xv, axis=0, keepdims=True)]
        for s in range(1, CONV_W):
            dx = dx + w[CONV_W - 1 - s:CONV_W - s, :] * _shift_rows(dc, -s)
            dws.insert(0, jnp.sum(dc * _shift_rows(xv, s), axis=0, keepdims=True))
        dx_ref[...] = dx
        dw_ref[...] = jnp.concatenate(dws, axis=0)

    return pl.pallas_call(
        body, grid=(3 * nh,),
        in_specs=[pl.BlockSpec((S, HEAD_D), lambda j: (0, blk0 + j)), pl.BlockSpec((CONV_W, HEAD_D), lambda j: (0, j)),
                  pl.BlockSpec((S, HEAD_D), lambda j: (0, j))],
        out_specs=[pl.BlockSpec((S, HEAD_D), lambda j: (0, j)), pl.BlockSpec((CONV_W, HEAD_D), lambda j: (0, j))],
        out_shape=[sds((S, 3 * nh * HEAD_D), F32), sds((CONV_W, 3 * nh * HEAD_D), F32)],
        compiler_params=_params(("parallel",)), name="gdn_prep_bwd")(proj, conv_w, dqkv)


def _softplus(x):
    return jnp.maximum(x, 0.0) + jnp.log(1.0 + jnp.exp(-jnp.abs(x)))


def _gates_fwd(ab, al, dt, nh):
    S = ab.shape[0]

    def body(ab_ref, al_ref, dt_ref, o_ref):
        v = ab_ref[...]
        lane = lax.broadcasted_iota(jnp.int32, v.shape, 1)
        g = -jnp.exp(al_ref[...]) * _softplus(v + dt_ref[...])
        o_ref[...] = jnp.where(lane < nh, g, jnp.where(lane < 2 * nh, _sigmoid(v), 0.0))

    row = pl.BlockSpec((1, LANE), lambda i: (0, 0))
    full = pl.BlockSpec((S, LANE), lambda i: (0, 0))
    return pl.pallas_call(body, grid=(1,), in_specs=[full, row, row], out_specs=full,
                          out_shape=jax.ShapeDtypeStruct((S, LANE), F32),
                          compiler_params=_params(("arbitrary",)), name="gates_fwd")(ab, al, dt)


def _gates_bwd(ab, al, dt, dgb, nh):
    S = ab.shape[0]
    sds = jax.ShapeDtypeStruct

    def body(ab_ref, al_ref, dt_ref, d_ref, dab_ref, dal_ref, ddt_ref):
        v, d = ab_ref[...], d_ref[...]
        lane = lax.broadcasted_iota(jnp.int32, v.shape, 1)
        is_a = lane < nh
        z = v + dt_ref[...]
        dsp = jnp.where(is_a, d * (-jnp.exp(al_ref[...])), 0.0)
        dz = dsp * _sigmoid(z)
        beta = _sigmoid(v)
        dab_ref[...] = jnp.where(is_a, dz, jnp.where(lane < 2 * nh, d * beta * (1.0 - beta), 0.0))
        dal_ref[...] = jnp.sum(dsp * _softplus(z), axis=0, keepdims=True)
        ddt_ref[...] = jnp.sum(dz, axis=0, keepdims=True)

    row = pl.BlockSpec((1, LANE), lambda i: (0, 0))
    full = pl.BlockSpec((S, LANE), lambda i: (0, 0))
    return pl.pallas_call(body, grid=(1,), in_specs=[full, row, row, full], out_specs=[full, row, row],
                          out_shape=[sds((S, LANE), F32), sds((1, LANE), F32), sds((1, LANE), F32)],
                          compiler_params=_params(("arbitrary",)), name="gates_bwd")(ab, al, dt, dgb)


def _col_of(tile, h):
    lane = lax.broadcasted_iota(jnp.int32, tile.shape, 1)
    return jnp.sum(jnp.where(lane == h, tile, 0.0), axis=1, keepdims=True)


def _to_row(col, eye):
    return jnp.sum(jnp.where(eye, col, 0.0), axis=0, keepdims=True)


def _to_col(row, eye):
    return jnp.sum(jnp.where(eye, row, 0.0), axis=1, keepdims=True)


def _split(a):
    hi = a.astype(BF16)
    return hi, (a - hi.astype(F32)).astype(BF16)


def _gdot(a, b, dims):
    ah, al = _split(a)
    bh, bl = _split(b)
    return _dot(ah, bh, dims) + (_dot(ah, bl, dims) + _dot(al, bh, dims))


def _chunk_local(q, k, v, gcol, bcol, T=None):
    C = CHUNK
    row = lax.broadcasted_iota(jnp.int32, (C, C), 0)
    col = lax.broadcasted_iota(jnp.int32, (C, C), 1)
    tril, strict, eye = col <= row, col < row, col == row
    grow = _to_row(gcol, eye)
    G_row = jnp.sum(jnp.where(row <= col, gcol, 0.0), axis=0, keepdims=True)
    G_col = jnp.sum(jnp.where(tril, grow, 0.0), axis=1, keepdims=True)
    decay = jnp.exp(jnp.where(tril, G_col - G_row, NEG_INF))
    G_last = G_col[C - 1:C, :]
    eG = jnp.exp(G_col)
    eGr = jnp.exp(G_last - G_col)
    gl = jnp.exp(G_last)
    kb = k * bcol
    A = jnp.where(strict, _gdot(kb, k, NT) * decay, 0.0)
    attn = _gdot(q, k, NT) * decay
    out = dict(strict=strict, eye=eye, row=row, col=col, decay=decay, eG=eG, eGr=eGr, gl=gl, kb=kb, A=A,
               rhs_k=kb * eG, attn=attn, q_dec=q * eG, k_dec=k * eGr)
    if T is None:
        T = jnp.where(eye, 1.0, 0.0) - A
        P = A
        for _ in range(int(math.log2(C)) - 1):
            P = _gdot(P, P, NN)
            T = T + _gdot(T, P, NN)
        out.update(T=T, u=_gdot(T, v * bcol, NN), w=_gdot(T, out["rhs_k"], NN))
    return out


GDN_ROWS = 256
WQK = 3 * CHUNK


def _gdn_local_fwd(qkv, gb, nh):
    S = qkv.shape[0]
    nc = S // CHUNK
    rb = min(GDN_ROWS, S)
    cpb = rb // CHUNK
    sds = jax.ShapeDtypeStruct

    def body(q_ref, k_ref, v_ref, gb_ref, u_ref, wqk_ref, attn_ref, t_ref):
        h = pl.program_id(0)
        for ci in range(cpb):
            rows = slice(ci * CHUNK, (ci + 1) * CHUNK)
            gbt = gb_ref[rows, :]
            L = _chunk_local(q_ref[rows, :], k_ref[rows, :], v_ref[rows, :], _col_of(gbt, h), _col_of(gbt, nh + h))
            u_ref[rows, :] = L["u"]
            base = ci * WQK
            wqk_ref[base:base + CHUNK, :] = L["w"]
            wqk_ref[base + CHUNK:base + 2 * CHUNK, :] = L["q_dec"]
            wqk_ref[base + 2 * CHUNK:base + WQK, :] = L["k_dec"]
            attn_ref[ci] = L["attn"]
            t_ref[ci] = L["T"]

    cc = pl.BlockSpec((None, cpb, CHUNK, CHUNK), lambda h, i: (h, i, 0, 0))
    return pl.pallas_call(
        body, grid=(nh, S // rb),
        in_specs=[pl.BlockSpec((rb, HEAD_D), lambda h, i: (i, h)), pl.BlockSpec((rb, HEAD_D), lambda h, i: (i, nh + h)),
                  pl.BlockSpec((rb, HEAD_D), lambda h, i: (i, 2 * nh + h)), pl.BlockSpec((rb, LANE), lambda h, i: (i, 0))],
        out_specs=[pl.BlockSpec((rb, HEAD_D), lambda h, i: (i, h)),
                   pl.BlockSpec((None, 3 * rb, HEAD_D), lambda h, i: (h, i, 0)), cc, cc],
        out_shape=[sds((S, nh * HEAD_D), F32), sds((nh, 3 * S, HEAD_D), F32), sds((nh, nc, CHUNK, CHUNK), F32),
                   sds((nh, nc, CHUNK, CHUNK), F32)],
        compiler_params=_params(("parallel", "parallel")), name="gdn_local_fwd")(qkv, qkv, qkv, gb)


def _gdn_scan_fwd(u, wqk, attn, gb, nh, dep):
    S = u.shape[0]
    nc = S // CHUNK
    rb = min(GDN_ROWS, S)
    cpb = rb // CHUNK
    sds = jax.ShapeDtypeStruct

    def body(u_ref, wqk_ref, attn_ref, gb_ref, dep_ref, o_ref, vn_ref, st_ref, s_ref):
        @pl.when(pl.program_id(0) == 0)
        def _():
            s_ref[...] = jnp.zeros_like(s_ref)

        for ci in range(cpb):
            rows = slice(ci * CHUNK, (ci + 1) * CHUNK)
            glv = jnp.exp(jnp.sum(gb_ref[rows, :], axis=0, keepdims=True))
            base = ci * WQK
            for h in range(nh):
                cols = slice(h * HEAD_D, (h + 1) * HEAD_D)
                state = s_ref[h]
                st_ref[h, ci] = state
                r = _gdot(wqk_ref[h, base:base + 2 * CHUNK, :], state, NN)
                vb = u_ref[rows, cols] - r[:CHUNK]
                o_ref[rows, cols] = r[CHUNK:] + _gdot(attn_ref[h, ci], vb, NN)
                vn_ref[rows, cols] = vb
                s_ref[h] = state * glv[:, h:h + 1] + _gdot(wqk_ref[h, base + 2 * CHUNK:base + WQK, :], vb, TN)

    return pl.pallas_call(
        body, grid=(S // rb,),
        in_specs=[pl.BlockSpec((rb, nh * HEAD_D), lambda i: (i, 0)), pl.BlockSpec((nh, 3 * rb, HEAD_D), lambda i: (0, i, 0)),
                  pl.BlockSpec((nh, cpb, CHUNK, CHUNK), lambda i: (0, i, 0, 0)), pl.BlockSpec((rb, LANE), lambda i: (i, 0)),
                  pl.BlockSpec(memory_space=pl.ANY)],
        out_specs=[pl.BlockSpec((rb, nh * HEAD_D), lambda i: (i, 0)), pl.BlockSpec((rb, nh * HEAD_D), lambda i: (i, 0)),
                   pl.BlockSpec((nh, cpb, HEAD_D, HEAD_D), lambda i: (0, i, 0, 0))],
        out_shape=[sds((S, nh * HEAD_D), F32), sds((S, nh * HEAD_D), F32), sds((nh, nc, HEAD_D, HEAD_D), F32)],
        scratch_shapes=[pltpu.VMEM((nh, HEAD_D, HEAD_D), F32)],
        compiler_params=_params(("arbitrary",)), name="gdn_scan_fwd")(u, wqk, attn, gb, dep)


def _gdn_scan_bwd(wqk, attn, gb, states, vn, do, nh, dep):
    S = vn.shape[0]
    nc = S // CHUNK
    rb = min(GDN_ROWS, S)
    cpb = rb // CHUNK
    last = S // rb - 1
    sds = jax.ShapeDtypeStruct

    def body(wqk_ref, attn_ref, gb_ref, st_ref, vn_ref, do_ref, dep_ref, dvn_ref, dw_ref, dqd_ref, dkd_ref, da_ref, dgl_ref,
             ds_ref):
        @pl.when(pl.program_id(0) == 0)
        def _():
            ds_ref[...] = jnp.zeros_like(ds_ref)

        row = lax.broadcasted_iota(jnp.int32, (CHUNK, CHUNK), 0)
        col = lax.broadcasted_iota(jnp.int32, (CHUNK, CHUNK), 1)
        for ci in reversed(range(cpb)):
            rows = slice(ci * CHUNK, (ci + 1) * CHUNK)
            glv = jnp.exp(jnp.sum(gb_ref[rows, :], axis=0, keepdims=True))
            base = ci * WQK
            for h in range(nh):
                cols = slice(h * HEAD_D, (h + 1) * HEAD_D)
                state, dS = st_ref[h, ci], ds_ref[h]
                wq = wqk_ref[h, base:base + 2 * CHUNK, :]
                kd = wqk_ref[h, base + 2 * CHUNK:base + WQK, :]
                vb = vn_ref[rows, cols]
                dob = do_ref[rows, cols]
                dvb = _gdot(attn_ref[h, ci], dob, TN) + _gdot(kd, dS, NN)
                x = _gdot(jnp.concatenate([dob, dvb], axis=0), state, NT)
                dqd_ref[rows, cols] = x[:CHUNK]
                dw_ref[rows, cols] = -x[CHUNK:]
                dvn_ref[rows, cols] = dvb
                da_ref[h, ci] = jnp.where(col <= row, _gdot(dob, vb, NT), 0.0)
                dkd_ref[rows, cols] = _gdot(vb, dS, NT)
                gl = glv[:, h:h + 1]
                dgl = jnp.sum(jnp.sum(state * dS, axis=0, keepdims=True), axis=1, keepdims=True)
                dgl_ref[h, ci] = jnp.broadcast_to(dgl * gl, (1, LANE))
                ds_ref[h] = dS * gl + _gdot(wq, jnp.concatenate([-dvb, dob], axis=0), TN)

    rv = lambda i: last - i
    wide = pl.BlockSpec((rb, nh * HEAD_D), lambda i: (rv(i), 0))
    return pl.pallas_call(
        body, grid=(S // rb,),
        in_specs=[pl.BlockSpec((nh, 3 * rb, HEAD_D), lambda i: (0, rv(i), 0)),
                  pl.BlockSpec((nh, cpb, CHUNK, CHUNK), lambda i: (0, rv(i), 0, 0)),
                  pl.BlockSpec((rb, LANE), lambda i: (rv(i), 0)),
                  pl.BlockSpec((nh, cpb, HEAD_D, HEAD_D), lambda i: (0, rv(i), 0, 0)), wide, wide,
                  pl.BlockSpec(memory_space=pl.ANY)],
        out_specs=[wide, wide, wide, wide, pl.BlockSpec((nh, cpb, CHUNK, CHUNK), lambda i: (0, rv(i), 0, 0)),
                   pl.BlockSpec((nh, cpb, 1, LANE), lambda i: (0, rv(i), 0, 0))],
        out_shape=[sds((S, nh * HEAD_D), F32), sds((S, nh * HEAD_D), F32), sds((S, nh * HEAD_D), F32),
                   sds((S, nh * HEAD_D), F32), sds((nh, nc, CHUNK, CHUNK), F32), sds((nh, nc, 1, LANE), F32)],
        scratch_shapes=[pltpu.VMEM((nh, HEAD_D, HEAD_D), F32)],
        compiler_params=_params(("arbitrary",)), name="gdn_scan_bwd")(wqk, attn, gb, states, vn, do, dep)


def _gdn_local_bwd(qkv, gb, T, u, wqk, dvn, dw, dqd, dkd, dattn, dgl, nh):
    S = qkv.shape[0]
    rb = min(GDN_ROWS, S)
    cpb = rb // CHUNK
    sds = jax.ShapeDtypeStruct

    def body(q_ref, k_ref, v_ref, gb_ref, t_ref, u_ref, wqk_ref, dvn_ref, dw_ref, dqd_ref, dkd_ref, da_ref, dgl_ref,
             dq_ref, dk_ref, dv_ref, dg_ref, db_ref):
        h = pl.program_id(0)
        for ci in range(cpb):
            rows = slice(ci * CHUNK, (ci + 1) * CHUNK)
            q, k, v = q_ref[rows, :], k_ref[rows, :], v_ref[rows, :]
            gbt = gb_ref[rows, :]
            bcol = _col_of(gbt, nh + h)
            T = t_ref[ci]
            L = _chunk_local(q, k, v, _col_of(gbt, h), bcol, T=T)
            strict, eye, decay, eG, eGr, kb = L["strict"], L["eye"], L["decay"], L["eG"], L["eGr"], L["kb"]
            attn, q_dec, k_dec = L["attn"], L["q_dec"], L["k_dec"]
            w = wqk_ref[ci * WQK:ci * WQK + CHUNK, :]
            dq_dec, dk_dec, dattn_c = dqd_ref[rows, :], dkd_ref[rows, :], da_ref[ci]
            drv = _gdot(T, dvn_ref[rows, :], TN)
            drk = _gdot(T, dw_ref[rows, :], TN)
            dA = jnp.where(strict, -(_gdot(drv, u_ref[rows, :], NT) + _gdot(drk, w, NT)), 0.0)
            dM = dA * decay
            dN = dattn_c * decay
            dkb = _gdot(dM, k, NN)
            dq_ref[rows, :] = _gdot(dN, k, NN) + dq_dec * eG
            dk_ref[rows, :] = (drk * (bcol * eG) + _gdot(dM, kb, TN) + dkb * bcol + _gdot(dN, q, TN) + dk_dec * eGr)
            dv_ref[rows, :] = drv * bcol
            db_ref[rows, :] = (jnp.sum(drv * v, axis=1, keepdims=True) + jnp.sum(drk * k, axis=1, keepdims=True) * eG
                               + jnp.sum(dkb * k, axis=1, keepdims=True))
            E = dA * L["A"] + dattn_c * attn
            kd = jnp.sum(dk_dec * k_dec, axis=1, keepdims=True)
            dG = (jnp.sum(dq_dec * q_dec, axis=1, keepdims=True) - kd + jnp.sum(drk * L["rhs_k"], axis=1, keepdims=True)
                  + jnp.sum(E, axis=1, keepdims=True) - _to_col(jnp.sum(E, axis=0, keepdims=True), eye))
            d_last = jnp.sum(kd, axis=0, keepdims=True) + dgl_ref[ci][:, :1]
            dG = dG + jnp.where(L["row"][:, :1] == CHUNK - 1, d_last, 0.0)
            dg_ref[rows, :] = jnp.sum(jnp.where(L["col"] >= L["row"], _to_row(dG, eye), 0.0), axis=1, keepdims=True)

    hd = pl.BlockSpec((rb, HEAD_D), lambda h, i: (i, h))
    cc = pl.BlockSpec((None, cpb, CHUNK, CHUNK), lambda h, i: (h, i, 0, 0))
    col1 = pl.BlockSpec((None, rb, 1), lambda h, i: (h, i, 0))
    return pl.pallas_call(
        body, grid=(nh, S // rb),
        in_specs=[hd, pl.BlockSpec((rb, HEAD_D), lambda h, i: (i, nh + h)),
                  pl.BlockSpec((rb, HEAD_D), lambda h, i: (i, 2 * nh + h)), pl.BlockSpec((rb, LANE), lambda h, i: (i, 0)),
                  cc, hd, pl.BlockSpec((None, 3 * rb, HEAD_D), lambda h, i: (h, i, 0)), hd, hd, hd, hd, cc,
                  pl.BlockSpec((None, cpb, 1, LANE), lambda h, i: (h, i, 0, 0))],
        out_specs=[hd, hd, hd, col1, col1],
        out_shape=[sds((S, nh * HEAD_D), F32)] * 3 + [sds((nh, S, 1), F32)] * 2,
        compiler_params=_params(("parallel", "parallel")), name="gdn_local_bwd")(
            qkv, qkv, qkv, gb, T, u, wqk, dvn, dw, dqd, dkd, dattn, dgl)


def _gated_norm_fwd(o, proj, norm_w, nh, z_blk0):
    S = o.shape[0]

    def body(o_ref, z_ref, w_ref, y_ref):
        ov, z = o_ref[...], z_ref[...]
        r = lax.rsqrt(jnp.mean(ov * ov, axis=1, keepdims=True) + RMS_EPS)
        y_ref[...] = ov * r * w_ref[...] * (z * _sigmoid(z))

    return pl.pallas_call(
        body, grid=(nh,),
        in_specs=[pl.BlockSpec((S, HEAD_D), lambda h: (0, h)), pl.BlockSpec((S, HEAD_D), lambda h: (0, z_blk0 + h)),
                  pl.BlockSpec((1, HEAD_D), lambda h: (0, 0))],
        out_specs=pl.BlockSpec((S, HEAD_D), lambda h: (0, h)),
        out_shape=jax.ShapeDtypeStruct((S, nh * HEAD_D), F32),
        compiler_params=_params(("parallel",)), name="gated_norm_fwd")(o, proj, norm_w)


def _gated_norm_bwd(o, proj, norm_w, dmix, nh, z_blk0, d_blk0):
    S = o.shape[0]
    sds = jax.ShapeDtypeStruct

    def body(o_ref, z_ref, w_ref, dy_ref, do_ref, dz_ref, dw_ref):
        ov, z, w, dy = o_ref[...], z_ref[...], w_ref[...], dy_ref[...]
        r = lax.rsqrt(jnp.mean(ov * ov, axis=1, keepdims=True) + RMS_EPS)
        oh = ov * r
        sg = _sigmoid(z)
        dz_ref[...] = dy * (oh * w) * (sg * (1.0 + z * (1.0 - sg)))
        don = dy * (z * sg)
        @pl.when(pl.program_id(0) == 0)
        def _():
            dw_ref[...] = jnp.zeros_like(dw_ref)

        dw_ref[...] += jnp.sum(don * oh, axis=0, keepdims=True)
        doh = don * w
        do_ref[...] = r * (doh - oh * jnp.mean(doh * oh, axis=1, keepdims=True))

    return pl.pallas_call(
        body, grid=(nh,),
        in_specs=[pl.BlockSpec((S, HEAD_D), lambda h: (0, h)), pl.BlockSpec((S, HEAD_D), lambda h: (0, z_blk0 + h)),
                  pl.BlockSpec((1, HEAD_D), lambda h: (0, 0)), pl.BlockSpec((S, HEAD_D), lambda h: (0, d_blk0 + h))],
        out_specs=[pl.BlockSpec((S, HEAD_D), lambda h: (0, h)), pl.BlockSpec((S, HEAD_D), lambda h: (0, h)),
                   pl.BlockSpec((1, HEAD_D), lambda h: (0, 0))],
        out_shape=[sds((S, nh * HEAD_D), F32), sds((S, nh * HEAD_D), F32), sds((1, HEAD_D), F32)],
        compiler_params=_params(("arbitrary",)), name="gated_norm_bwd")(o, proj, norm_w, dmix)


def _adamw_math(w, g, m, v):
    m = ADAM_B1 * m + (1.0 - ADAM_B1) * g
    v = ADAM_B2 * v + (1.0 - ADAM_B2) * (g * g)
    m_hat = m / (1.0 - ADAM_B1 ** ADAM_STEP)
    v_hat = v / (1.0 - ADAM_B2 ** ADAM_STEP)
    delta = -ADAM_LR * (m_hat / (jnp.sqrt(v_hat) + ADAM_EPS) + ADAM_WD * w)
    return delta, m, v


def _slab_tiles(R, C, rows=256, cols=256):
    if R % rows == 0:
        return (rows, C), R // rows, lambda i: (i, 0)
    tc = _tile(C, cols)
    return (R, tc), C // tc, lambda i: (0, i)


def _adamw_big(parts, terms, chip, w, m, v, name):
    R, C = w.shape
    blk, steps, at = _slab_tiles(R, C)
    sds = jax.ShapeDtypeStruct

    def body(q_ref, p_ref, t_ref, w_ref, m_ref, v_ref, g_ref, d_ref, nm_ref, nv_ref):
        g = ((p_ref[...].astype(F32) + t_ref[0].astype(F32)) + t_ref[1].astype(F32)) + t_ref[2].astype(F32)
        g_ref[...] = g
        d_ref[...], nm_ref[...], nv_ref[...] = _adamw_math(w_ref[...], g, m_ref[...], v_ref[...])

    spec = pl.BlockSpec(blk, lambda i, q_ref: at(i))
    grid_spec = pltpu.PrefetchScalarGridSpec(
        num_scalar_prefetch=1, grid=(steps,),
        in_specs=[pl.BlockSpec((None,) + blk, lambda i, q_ref: (q_ref[0],) + at(i)),
                  pl.BlockSpec((3,) + blk, lambda i, q_ref: (0,) + at(i)), spec, spec, spec],
        out_specs=[spec] * 4)
    return pl.pallas_call(body, grid_spec=grid_spec, out_shape=[sds((R, C), F32)] * 4,
                          compiler_params=_params(("parallel",)), name=name)(chip, parts, terms, w, m, v)


def _adamw_small(ws, gs, ms, vs):
    n = len(ws)

    def body(*refs):
        for i in range(n):
            w, g, m, v = (refs[k * n + i][...] for k in range(4))
            d, nm, nv = _adamw_math(w, g, m, v)
            refs[4 * n + i][...] = d
            refs[5 * n + i][...] = nm
            refs[6 * n + i][...] = nv

    shapes = [jax.ShapeDtypeStruct(w.shape, F32) for w in ws]
    vm = pl.BlockSpec(memory_space=pltpu.VMEM)
    outs = pl.pallas_call(body, in_specs=[vm] * (4 * n), out_specs=[vm] * (3 * n), out_shape=shapes * 3,
                          name="adamw_small")(*ws, *gs, *ms, *vs)
    return outs[:n], outs[n:2 * n], outs[2 * n:]


MESH = pl.DeviceIdType.MESH
ANY = pl.BlockSpec(memory_space=pl.ANY)


def _place():
    x, y, c = lax.axis_index("x"), lax.axis_index("y"), lax.axis_index("c")
    return x, y, c, [(1 - x, y), (x, 1 - y), (1 - x, 1 - y)]


def _all_gather(shards):
    n = len(shards)

    def body(*refs):
        ins, outs = refs[:n], refs[n:2 * n]
        send_sems, recv_sems, local_sems = refs[2 * n:]
        x, y, c, chips = _place()
        me, sibling = (x, y, c), (x, y, 1 - c)

        def slot(px, py, pc):
            return 4 * px + 2 * py + pc

        def copy(w, k, block, to, src=None):
            dst = outs[w].at[slot(*block)]
            return pltpu.make_async_remote_copy(src_ref=dst if src is None else src, dst_ref=dst,
                                                send_sem=send_sems.at[w * 7 + k], recv_sem=recv_sems.at[w * 7 + k],
                                                device_id=to, device_id_type=MESH)

        mine = [pltpu.make_async_copy(ins[w], outs[w].at[slot(*me)], local_sems.at[w]) for w in range(n)]
        for cp in mine:
            cp.start()
        first = []
        for w in range(n):
            first.append(copy(w, 0, me, sibling, src=ins[w]))
            first += [copy(w, 1 + j, me, (*chip, c), src=ins[w]) for j, chip in enumerate(chips)]
        for cp in first:
            cp.start()
        passed = []
        for w in range(n):
            for j, chip in enumerate(chips):
                copy(w, 1 + j, (*chip, c), me).wait_recv()
                cp = copy(w, 4 + j, (*chip, c), sibling)
                cp.start()
                passed.append(cp)
        for w in range(n):
            copy(w, 0, sibling, me).wait_recv()
            for j, chip in enumerate(chips):
                copy(w, 4 + j, (*chip, 1 - c), me).wait_recv()
        for cp in first + passed:
            cp.wait_send()
        for cp in mine:
            cp.wait()

    return pl.pallas_call(
        body, in_specs=[ANY] * n, out_specs=[ANY] * n,
        out_shape=[jax.ShapeDtypeStruct((N_DEV,) + s.shape, s.dtype) for s in shards],
        scratch_shapes=[pltpu.SemaphoreType.DMA((7 * n,)), pltpu.SemaphoreType.DMA((7 * n,)),
                        pltpu.SemaphoreType.DMA((n,))],
        name="all_gather_weights")(*shards)


def _sibling_exchange(grads):
    n = len(grads)

    def body(*refs):
        ins, outs = refs[:n], refs[n:2 * n]
        send_sems, recv_sems = refs[2 * n:]
        x, y, c, _ = _place()
        sibling = (x, y, 1 - c)
        copies = []
        for w in range(n):
            for q in range(4):
                copies.append(pltpu.make_async_remote_copy(
                    src_ref=ins[w].at[2 * q + (1 - c)], dst_ref=outs[w].at[q], send_sem=send_sems.at[4 * w + q],
                    recv_sem=recv_sems.at[4 * w + q], device_id=sibling, device_id_type=MESH))
        for cp in copies:
            cp.start()
        for cp in copies:
            cp.wait()

    return pl.pallas_call(
        body, in_specs=[ANY] * n, out_specs=[ANY] * n,
        out_shape=[jax.ShapeDtypeStruct((4,) + g.shape[1:], g.dtype) for g in grads],
        scratch_shapes=[pltpu.SemaphoreType.DMA((4 * n,)), pltpu.SemaphoreType.DMA((4 * n,))],
        name="rs_sibling_exchange")(*grads)


def _chip_sum(grad, recv, core, name):
    _, R, C = grad.shape
    blk, steps, at = _slab_tiles(R, C)

    def body(c_ref, g_ref, r_ref, o_ref):
        o_ref[...] = (g_ref[...].astype(F32) + r_ref[...].astype(F32)).astype(o_ref.dtype)

    grid_spec = pltpu.PrefetchScalarGridSpec(
        num_scalar_prefetch=1, grid=(4, steps),
        in_specs=[pl.BlockSpec((None,) + blk, lambda q, i, c_ref: (2 * q + c_ref[0],) + at(i)),
                  pl.BlockSpec((None,) + blk, lambda q, i, c_ref: (q,) + at(i))],
        out_specs=pl.BlockSpec((None,) + blk, lambda q, i, c_ref: (q,) + at(i)))
    return pl.pallas_call(body, grid_spec=grid_spec, out_shape=jax.ShapeDtypeStruct((4, R, C), BF16),
                          compiler_params=_params(("parallel", "parallel")), name=name)(core, grad, recv)


def _chip_exchange(parts):
    n = len(parts)

    def body(*refs):
        ins, outs = refs[:n], refs[n:2 * n]
        send_sems, recv_sems = refs[2 * n:]
        x, y, c, chips = _place()
        copies = []
        for w in range(n):
            for j, (px, py) in enumerate(chips):
                copies.append(pltpu.make_async_remote_copy(
                    src_ref=ins[w].at[2 * px + py], dst_ref=outs[w].at[j], send_sem=send_sems.at[3 * w + j],
                    recv_sem=recv_sems.at[3 * w + j], device_id=(px, py, c), device_id_type=MESH))
        for cp in copies:
            cp.start()
        for cp in copies:
            cp.wait()

    return pl.pallas_call(
        body, in_specs=[ANY] * n, out_specs=[ANY] * n,
        out_shape=[jax.ShapeDtypeStruct((3,) + p.shape[1:], p.dtype) for p in parts],
        scratch_shapes=[pltpu.SemaphoreType.DMA((3 * n,)), pltpu.SemaphoreType.DMA((3 * n,))],
        name="rs_chip_exchange")(*parts)


HBM_SPEC = pl.BlockSpec(memory_space=pltpu.HBM)
SEM_SPEC = pl.BlockSpec(memory_space=pltpu.SEMAPHORE)
DATAFLOW = pltpu.SideEffectType.DATAFLOW_SIDE_EFFECTING


def _split_start(name, bufs, plan, n, after):
    nb = len(bufs)

    def body(*refs):
        send_sems, recv_sems, token = refs[nb + 1], refs[nb + 2], refs[-1]
        for k, (src, dst, to) in enumerate(plan(refs[:nb])):
            pltpu.make_async_remote_copy(src_ref=src, dst_ref=dst, send_sem=send_sems.at[k], recv_sem=recv_sems.at[k],
                                         device_id=to, device_id_type=MESH).start()
        token[...] = jnp.zeros_like(token)

    outs = pl.pallas_call(
        body, name=name,
        out_shape=(pltpu.SemaphoreType.DMA((n,)), pltpu.SemaphoreType.DMA((n,)),
                   *[pltpu.HBM(b.shape, b.dtype) for b in bufs], jax.ShapeDtypeStruct((8, LANE), F32)),
        in_specs=[HBM_SPEC] * nb + [ANY],
        out_specs=(SEM_SPEC, SEM_SPEC, *[HBM_SPEC] * nb, pl.BlockSpec(memory_space=pltpu.VMEM)),
        input_output_aliases={i: 2 + i for i in range(nb)},
        compiler_params=pltpu.CompilerParams(has_side_effects=DATAFLOW))(
            *[pltpu.with_memory_space_constraint(b, pltpu.HBM) for b in bufs], after)
    return outs[0], outs[1], list(outs[2:2 + nb]), outs[-1]


def _split_wait(name, send_sems, recv_sems, bufs, plan, n, after):
    nb = len(bufs)

    def body(*refs):
        send_s, recv_s = refs[nb], refs[nb + 1]
        for k, (src, dst, to) in enumerate(plan(refs[:nb])):
            cp = pltpu.make_async_remote_copy(src_ref=src, dst_ref=dst, send_sem=send_s.at[k], recv_sem=recv_s.at[k],
                                              device_id=to, device_id_type=MESH)
            cp.wait_send()
            cp.wait_recv()

    outs = pl.pallas_call(
        body, name=name, out_shape=tuple(pltpu.HBM(b.shape, b.dtype) for b in bufs),
        in_specs=[HBM_SPEC] * nb + [SEM_SPEC, SEM_SPEC, ANY], out_specs=tuple([HBM_SPEC] * nb),
        input_output_aliases={i: i for i in range(nb)},
        compiler_params=pltpu.CompilerParams(has_side_effects=DATAFLOW))(*bufs, send_sems, recv_sems, after)
    return list(outs)


def _slot(px, py, pc):
    return 4 * px + 2 * py + pc


class _Exchanges:
    def __init__(self, tag, shards, dev, core):
        self.tag, self.shards, self.dev, self.core = tag, shards, dev, core
        self.n = len(shards)

    def _gather_plan1(self, refs):
        n = self.n
        x, y, c, chips = _place()
        out = []
        for w in range(n):
            dst = refs[n + w].at[_slot(x, y, c)]
            out.append((refs[w], dst, (x, y, 1 - c)))
            out += [(refs[w], dst, (px, py, c)) for px, py in chips]
        return out

    def _gather_plan2(self, refs):
        x, y, c, chips = _place()
        return [(refs[w].at[_slot(px, py, c)],) * 2 + ((x, y, 1 - c),) for w in range(self.n) for px, py in chips]

    def weights_start(self, after):
        lands = [lax.dynamic_update_slice(lax.empty((N_DEV,) + s.shape, s.dtype), s[None], (self.dev, 0, 0))
                 for s in self.shards]
        self.w1 = _split_start(self.tag + "gather_start_1", list(self.shards) + lands, self._gather_plan1, 4 * self.n, after)
        return self.w1[3]

    def weights_mid(self, after):
        send, recv, bufs, _ = self.w1
        lands = _split_wait(self.tag + "gather_wait_1", send, recv, bufs, self._gather_plan1, 4 * self.n, after)[self.n:]
        self.w2 = _split_start(self.tag + "gather_start_2", lands, self._gather_plan2, 3 * self.n, lands[0])
        return self.w2[3]

    def weights_finish(self, after):
        send, recv, bufs, _ = self.w2
        return _split_wait(self.tag + "gather_wait_2", send, recv, bufs, self._gather_plan2, 3 * self.n, after)

    def _reduce_plan1(self, refs):
        n = self.n
        x, y, c, _ = _place()
        return [(refs[w].at[2 * q + (1 - c)], refs[n + w].at[q], (x, y, 1 - c)) for w in range(n) for q in range(4)]

    def _reduce_plan2(self, refs):
        n = self.n
        x, y, c, chips = _place()
        return [(refs[w].at[2 * px + py], refs[n + w].at[j], (px, py, c))
                for w in range(n) for j, (px, py) in enumerate(chips)]

    def grads_start(self, grads):
        lands = [lax.empty((4,) + g.shape[1:], g.dtype) for g in grads]
        self.g1 = _split_start(self.tag + "reduce_start_1", list(grads) + lands, self._reduce_plan1, 4 * self.n, grads[-1])
        return self.g1[3]

    def grads_mid(self, after):
        n = self.n
        send, recv, bufs, _ = self.g1
        bufs = _split_wait(self.tag + "reduce_wait_1", send, recv, bufs, self._reduce_plan1, 4 * n, after)
        core = self.core.reshape(1).astype(jnp.int32)
        self.parts = [_chip_sum(bufs[w], bufs[n + w], core, self.tag + "reduce_chip_sum_%d" % w) for w in range(n)]
        lands = [lax.empty((3,) + p.shape[1:], p.dtype) for p in self.parts]
        self.g2 = _split_start(self.tag + "reduce_start_2", self.parts + lands, self._reduce_plan2, 3 * n, self.parts[-1])
        return self.g2[3]

    def grads_finish(self, after):
        n = self.n
        send, recv, bufs, _ = self.g2
        bufs = _split_wait(self.tag + "reduce_wait_2", send, recv, bufs, self._reduce_plan2, 3 * n, after)
        self.parts, self.terms = bufs[:n], bufs[n:]


def _all_reduce_small(buf):
    R = buf.shape[0]

    def body(x_ref, o_ref, g_ref, send_sems, recv_sems):
        x, y, c, chips = _place()
        me, sibling = (x, y, c), (x, y, 1 - c)

        def slot(px, py, pc):
            return 4 * px + 2 * py + pc

        def copy(k, block, to, src=None):
            dst = g_ref.at[slot(*block)]
            return pltpu.make_async_remote_copy(src_ref=dst if src is None else src, dst_ref=dst,
                                                send_sem=send_sems.at[k], recv_sem=recv_sems.at[k],
                                                device_id=to, device_id_type=MESH)

        first = [copy(0, me, sibling, src=x_ref)]
        first += [copy(1 + j, me, (*chip, c), src=x_ref) for j, chip in enumerate(chips)]
        for cp in first:
            cp.start()
        g_ref[slot(*me)] = x_ref[...]
        passed = [copy(4 + j, (*chip, c), sibling) for j, chip in enumerate(chips)]
        for j, chip in enumerate(chips):
            copy(1 + j, (*chip, c), me).wait_recv()
            passed[j].start()
        copy(0, sibling, me).wait_recv()
        for j, chip in enumerate(chips):
            copy(4 + j, (*chip, 1 - c), me).wait_recv()
        for cp in first + passed:
            cp.wait_send()
        acc = g_ref[0]
        for s in range(1, N_DEV):
            acc = acc + g_ref[s]
        o_ref[...] = acc

    vm = pl.BlockSpec(memory_space=pltpu.VMEM)
    return pl.pallas_call(
        body, in_specs=[vm], out_specs=vm, out_shape=jax.ShapeDtypeStruct((R, LANE), F32),
        scratch_shapes=[pltpu.VMEM((N_DEV, R, LANE), F32), pltpu.SemaphoreType.DMA((7,)), pltpu.SemaphoreType.DMA((7,))],
        name="all_reduce_small")(buf)


def _pad_cols(a, width):
    return jnp.pad(a, ((0, 0), (0, width - a.shape[1])))


def _local_step(x, target, w_in_t, conv_w, a_log, dt_bias, delta_norm_w, sinks, rel_bias, ln1_g, ln1_b, ln2_g, ln2_b,
                ex, ex_in):
    S, D = x.shape
    aw = D // 2
    hq, hkv, nh = aw // HEAD_A, aw // HEAD_A // GQA, aw // HEAD_D
    kvw = hkv * HEAD_A
    c_q, c_k, c_v, c_d = 0, aw, aw + kvw, aw + 2 * kvw
    c_ab = c_d + 3 * aw
    c_z = c_ab + 2 * nh
    n_in = c_z + aw
    assert w_in_t.shape == (n_in, D), (w_in_t.shape, n_in)
    w_pt = jnp.concatenate([w_in_t[:c_ab], w_in_t[c_z:], jnp.pad(w_in_t[c_ab:c_z], ((0, LANE - 2 * nh), (0, 0)))], axis=0)
    p_z, p_ab = c_ab, c_ab + aw
    n_p = p_ab + LANE

    proj = _matmul(x, w_pt, NT, name="proj", tn=1152, deps=(ex.weights_start(w_pt),))
    bias = _attn_bias(rel_bias.T)
    attn_out = _attn_fwd(proj, bias, sinks.reshape(-1), hq, 0, c_k // kvw, c_v // kvw)
    conv2 = conv_w.reshape(CONV_W, 3 * aw)
    qkv = _gdn_prep_fwd(proj, conv2, nh, c_d // HEAD_D)
    ab = proj[:, p_ab:]
    al, dt = _pad_cols(a_log, LANE), _pad_cols(dt_bias, LANE)
    gb = _gates_fwd(ab, al, dt, nh)
    u_d, wqk, attn_d, t_d = _gdn_local_fwd(qkv, gb, nh)
    o_d, vn, states = _gdn_scan_fwd(u_d, wqk, attn_d, gb, nh, ex.weights_mid(u_d))
    delta_out = _gated_norm_fwd(o_d, proj, delta_norm_w, nh, p_z // HEAD_D)
    mix = jnp.concatenate([attn_out, delta_out], axis=1)
    w_o_g, w_up_g, w_down_g = ex.weights_finish(mix)
    w_o, w_down = w_o_g.reshape(D, D), w_down_g.reshape(-1, D)
    mixed = _matmul(mix, w_o, NN, name="out_proj")
    h1 = _ln1_fwd(x, mixed, ln1_g, ln1_b)
    u = _matmul(h1, w_up_g, NN, name="mlp_up", b_groups=True)
    mlp = _matmul(u, w_down, NN, name="mlp_down", a_fn=_relu_sq)
    dr2, loss_row, dln2_g, dln2_b = _ln2_loss(h1, mlp, ln2_g, ln2_b, target)

    du = _matmul(dr2, w_down, NT, name="d_mlp_act", epi=_relu_sq_grad, epi_in=(u,))
    dw_down = _matmul(u, dr2, TN, name="dw_down", a_fn=_relu_sq, out_dtype=BF16)
    dw_up = _matmul(h1, du, TN, name="dw_up", out_dtype=BF16, out_groups=N_DEV)
    dh_mlp = _matmul(du, w_up_g, NT, name="d_h1", b_groups=True)
    dr1, dln1_g, dln1_b = _ln1_bwd(x, mixed, ln1_g, dr2, dh_mlp)
    dw_o = _matmul(mix, dr1, TN, name="dw_o", out_dtype=BF16)
    tok = ex.grads_start([dw_o.reshape(N_DEV, -1, D), dw_up, dw_down.reshape(N_DEV, -1, D)])
    dmix = _matmul(dr1, w_o, NT, name="d_mix", deps=(tok,))
    dq_a, dk_a, dv_a, dbias, dsink = _attn_bwd(proj, bias, sinks.reshape(-1), attn_out, dmix, hq, 0, c_k // kvw, c_v // kvw)
    drel = _rel_bias_grad(dbias, hq)
    do_d, dz, dnw = _gated_norm_bwd(o_d, proj, delta_norm_w, dmix, nh, p_z // HEAD_D, aw // HEAD_D)
    dvn_s, dw_s, dqd, dkd, dattn_d, dgl = _gdn_scan_bwd(wqk, attn_d, gb, states, vn, do_d, nh, ex.grads_mid(dq_a))
    dqn, dkn, dvn, dg, dbeta = _gdn_local_bwd(qkv, gb, t_d, u_d, wqk, dvn_s, dw_s, dqd, dkd, dattn_d, dgl, nh)
    dgb = _pad_cols(jnp.concatenate([dg.reshape(nh, S).T, dbeta.reshape(nh, S).T], axis=1), LANE)
    dab, da_log, ddt_bias = _gates_bwd(ab, al, dt, dgb, nh)
    dqkv_d, dconv = _gdn_prep_bwd(proj, conv2, jnp.concatenate([dqn, dkn, dvn], axis=1), nh, c_d // HEAD_D)
    dproj = jnp.concatenate([dq_a, dk_a, dv_a, dqkv_d, dz, dab], axis=1)
    dw_pt = _matmul(dproj, x, TN, name="dw_in", out_dtype=BF16, tm=1152)
    dw_in_t = jnp.concatenate([dw_pt[:p_z], dw_pt[p_ab:p_ab + 2 * nh], dw_pt[p_z:p_ab]], axis=0)
    tok = ex_in.grads_start([dw_in_t.reshape(N_DEV, -1, D)])
    dx_proj = _matmul(dproj, w_pt, NN, name="d_x", tk=1920, deps=(tok,))
    grad_x = _grad_x(dr1, dx_proj, ex_in.grads_mid(dx_proj))
    ex.grads_finish(grad_x)

    small = dict(conv_w=dconv, a_log=da_log[:, :nh], dt_bias=ddt_bias[:, :nh], delta_norm_w=dnw,
                 attn_sinks=dsink[:, 0].reshape(1, hq), rel_bias=drel[:, :, 0].T,
                 ln1_g=dln1_g, ln1_b=dln1_b, ln2_g=dln2_g, ln2_b=dln2_b)
    return loss_row, grad_x, small


SMALL_ORDER = ("conv_w", "a_log", "dt_bias", "delta_norm_w", "attn_sinks", "rel_bias", "ln1_g", "ln1_b", "ln2_g", "ln2_b")


def _pack_small(loss_row, small):
    parts = [loss_row.reshape(-1)]
    for k in SMALL_ORDER:
        flat = small[k].reshape(-1)
        parts.append(jnp.pad(flat, (0, (-flat.shape[0]) % LANE)))
    flat = jnp.concatenate(parts)
    flat = jnp.pad(flat, (0, (-flat.shape[0]) % (8 * LANE)))
    return flat.reshape(-1, LANE)


def _unpack_small(buf, small_shapes):
    flat = buf.reshape(-1)
    loss = flat[0]
    off = LANE
    out = {}
    for k in SMALL_ORDER:
        n = int(np.prod(small_shapes[k]))
        out[k] = flat[off:off + n].reshape(small_shapes[k])
        off += n + (-n) % LANE
    return loss, out


def kernel(x, w_in, conv_w, a_log, dt_bias, delta_norm_w, attn_sinks, rel_bias, w_o, ln1_g, ln1_b, w_up, w_down, ln2_g, ln2_b, loss_target, m_w_in, m_conv_w, m_a_log, m_dt_bias, m_delta_norm_w, m_attn_sinks, m_rel_bias, m_w_o, m_ln1_g, m_ln1_b, m_w_up, m_w_down, m_ln2_g, m_ln2_b, v_w_in, v_conv_w, v_a_log, v_dt_bias, v_delta_norm_w, v_attn_sinks, v_rel_bias, v_w_o, v_ln1_g, v_ln1_b, v_w_up, v_w_down, v_ln2_g, v_ln2_b):
    S, D = x.shape[1], x.shape[2]
    core = lax.axis_index("c")
    dev = 4 * lax.axis_index("x") + 2 * lax.axis_index("y") + core

    (w_in_g,) = _all_gather([w_in[0].T.astype(BF16)])
    w_in_t = w_in_g.reshape(-1, D)
    ex = _Exchanges("", [w_o[0].astype(BF16), w_up[0].astype(BF16), w_down[0].astype(BF16)], dev, core)
    ex_in = _Exchanges("in_", [w_in_g[0]], dev, core)

    cw_sh = conv_w.shape[3]
    conv_place = lax.dynamic_update_slice(jnp.zeros((CONV_W, N_DEV * cw_sh), F32), conv_w[0, :, 0, :], (0, dev * cw_sh))
    conv_full = _all_reduce_small(jnp.pad(conv_place.reshape(-1, LANE), ((0, (-conv_place.size // LANE) % 8), (0, 0))))
    conv_full = conv_full[:conv_place.size // LANE].reshape(CONV_W, N_DEV * cw_sh)

    loss_row, grad_x, small = _local_step(
        x[0], loss_target[0], w_in_t, conv_full, a_log, dt_bias, delta_norm_w, attn_sinks, rel_bias,
        ln1_g, ln1_b, ln2_g, ln2_b, ex, ex_in)

    chip_arr = (dev // 2).reshape(1).astype(jnp.int32)
    big = {}
    for i, (name, w, m, v) in enumerate((("w_o", w_o, m_w_o, v_w_o), ("w_up", w_up, m_w_up, v_w_up),
                                         ("w_down", w_down, m_w_down, v_w_down))):
        big[name] = [o[None] for o in _adamw_big(ex.parts[i], ex.terms[i], chip_arr, w[0], m[0], v[0], "adamw_" + name)]
    ex_in.grads_finish(big["w_down"][0])
    outs = _adamw_big(ex_in.parts[0], ex_in.terms[0], chip_arr, w_in[0].T, m_w_in[0].T, v_w_in[0].T, "adamw_w_in")
    big["w_in"] = [o.T[None] for o in outs]

    small_shapes = {k: v.shape for k, v in small.items()}
    loss, small = _unpack_small(_all_reduce_small(_pack_small(loss_row, small)), small_shapes)
    small["conv_w"] = lax.dynamic_slice(small["conv_w"], (0, dev * cw_sh), (CONV_W, cw_sh))
    small["rel_bias"] = small["rel_bias"].reshape(rel_bias.shape)
    p2 = dict(conv_w=(conv_w, m_conv_w, v_conv_w), a_log=(a_log, m_a_log, v_a_log), dt_bias=(dt_bias, m_dt_bias, v_dt_bias),
              delta_norm_w=(delta_norm_w, m_delta_norm_w, v_delta_norm_w), attn_sinks=(attn_sinks, m_attn_sinks, v_attn_sinks),
              rel_bias=(rel_bias, m_rel_bias, v_rel_bias), ln1_g=(ln1_g, m_ln1_g, v_ln1_g), ln1_b=(ln1_b, m_ln1_b, v_ln1_b),
              ln2_g=(ln2_g, m_ln2_g, v_ln2_g), ln2_b=(ln2_b, m_ln2_b, v_ln2_b))
    two_d = lambda a: a.reshape(-1, a.shape[-1])
    ws = [two_d(p2[k][0]) for k in SMALL_ORDER]
    gs = [two_d(small[k]) for k in SMALL_ORDER]
    ms = [two_d(p2[k][1]) for k in SMALL_ORDER]
    vs = [two_d(p2[k][2]) for k in SMALL_ORDER]
    ds, nms, nvs = _adamw_small(ws, gs, ms, vs)
    res = {}
    for i, k in enumerate(SMALL_ORDER):
        shp = p2[k][0].shape
        res[k] = [gs[i].reshape(shp), ds[i].reshape(shp), nms[i].reshape(shp), nvs[i].reshape(shp)]
    res.update(big)
    order = ("w_in", "conv_w", "a_log", "dt_bias", "delta_norm_w", "attn_sinks", "rel_bias", "w_o", "ln1_g", "ln1_b",
             "w_up", "w_down", "ln2_g", "ln2_b")
    return (loss, grad_x[None], *[res[k][0] for k in order], *[res[k][1] for k in order],
            *[res[k][2] for k in order], *[res[k][3] for k in order])
```

```python
import functools
import math

import numpy as np
import jax
import jax.numpy as jnp
from jax import lax
from jax.experimental import pallas as pl
from jax.experimental.pallas import tpu as pltpu

F32 = jnp.float32
BF16 = jnp.bfloat16
HIGHEST = lax.Precision.HIGHEST

N_DEV = 8
HEAD_A = 64
GQA = 4
BLK = 128
N_BUCKETS = 32
MAX_DISTANCE = 128
HEAD_D = 128
CONV_W = 4
CHUNK = 64
NEG_INF = -1e30
LN_EPS = 1e-5
RMS_EPS = 1e-6
DN_ALPHA = 2.0 ** 0.25
ADAM_LR, ADAM_B1, ADAM_B2, ADAM_EPS, ADAM_WD, ADAM_STEP = 0.001, 0.9, 0.999, 1e-08, 0.01, 10

LANE = 128
VMEM_LIMIT = 56 * 1024 * 1024

NN = ((1,), (0,))
NT = ((1,), (1,))
TN = ((0,), (0,))


def _dot(a, b, dims, prec=None):
    return lax.dot_general(a, b, (dims, ((), ())), precision=prec, preferred_element_type=F32)


def _tile(dim, pref):
    if dim <= pref:
        return dim
    t = (pref // LANE) * LANE
    while t > LANE and dim % t:
        t -= LANE
    assert dim % t == 0, (dim, pref)
    return t


def _params(sem):
    return pltpu.CompilerParams(dimension_semantics=sem, vmem_limit_bytes=VMEM_LIMIT)


def _matmul(a, b, dims, *, name, out_dtype=F32, tm=1024, tn=1024, tk=2048, a_fn=None, epi=None, epi_in=(),
            b_groups=None, out_groups=None, deps=()):
    (ca,), (cb,) = dims
    M, K = a.shape[1 - ca], a.shape[ca]
    if b_groups:
        G, R, C = b.shape
        bshape = (R, G * C)
    else:
        bshape = b.shape
    N = bshape[1 - cb]
    assert bshape[cb] == K, (a.shape, b.shape, dims)
    tm, tk = _tile(M, tm), _tile(K, tk)
    if b_groups:
        lim = C if cb == 0 else tn
        tn = _tile(N, min(tn, lim))
        if cb == 1:
            tk = _tile(K, min(tk, C))
    else:
        tn = _tile(N, tn)
    if out_groups:
        tn = _tile(N, min(tn, N // out_groups))
    nk = K // tk

    def body(*refs):
        a_ref, b_ref = refs[0], refs[1]
        e_refs = refs[2:2 + len(epi_in)]
        o_ref = refs[2 + len(epi_in) + len(deps)]
        acc_ref = refs[3 + len(epi_in) + len(deps)] if nk > 1 else None
        k = pl.program_id(2)
        av = a_ref[...]
        if a_fn is not None:
            av = a_fn(av)
        prod = _dot(av.astype(BF16), b_ref[...].astype(BF16), dims)

        def finish(r):
            if epi is not None:
                r = epi(r, *[e[...] for e in e_refs])
            o_ref[...] = r.astype(out_dtype)

        if nk == 1:
            finish(prod)
            return

        @pl.when(k == 0)
        def _():
            acc_ref[...] = prod

        @pl.when(k > 0)
        def _():
            acc_ref[...] += prod

        @pl.when(k == nk - 1)
        def _():
            finish(acc_ref[...])

    a_spec = (pl.BlockSpec((tm, tk), lambda i, j, k: (i, k)) if ca == 1
              else pl.BlockSpec((tk, tm), lambda i, j, k: (k, i)))
    if b_groups:
        if cb == 0:
            per = C // tn
            b_spec = pl.BlockSpec((None, tk, tn), lambda i, j, k: (j // per, k, j % per))
        else:
            per = C // tk
            b_spec = pl.BlockSpec((None, tn, tk), lambda i, j, k: (k // per, j, k % per))
    else:
        b_spec = (pl.BlockSpec((tk, tn), lambda i, j, k: (k, j)) if cb == 0
                  else pl.BlockSpec((tn, tk), lambda i, j, k: (j, k)))
    e_specs = [pl.BlockSpec((tm, tn), lambda i, j, k: (i, j)) for _ in epi_in]
    if out_groups:
        per_o = (N // out_groups) // tn
        o_spec = pl.BlockSpec((None, tm, tn), lambda i, j, k: (j // per_o, i, j % per_o))
        o_shape = jax.ShapeDtypeStruct((out_groups, M, N // out_groups), out_dtype)
    else:
        o_spec = pl.BlockSpec((tm, tn), lambda i, j, k: (i, j))
        o_shape = jax.ShapeDtypeStruct((M, N), out_dtype)
    return pl.pallas_call(
        body, grid=(M // tm, N // tn, nk), out_specs=o_spec,
        in_specs=[a_spec, b_spec] + e_specs + [pl.BlockSpec(memory_space=pl.ANY)] * len(deps),
        out_shape=o_shape, scratch_shapes=[pltpu.VMEM((tm, tn), F32)] if nk > 1 else [],
        compiler_params=_params(("parallel", "parallel", "arbitrary")), name=name)(a, b, *epi_in, *deps)


def _relu_sq(u):
    r = jnp.maximum(u, 0.0)
    return r * r


def _relu_sq_grad(acc, u):
    return acc * (2.0 * jnp.maximum(u, 0.0))


def _ln_stats(r):
    mu = jnp.mean(r, axis=-1, keepdims=True)
    xc = r - mu
    var = jnp.mean(xc * xc, axis=-1, keepdims=True)
    rstd = lax.rsqrt(var + LN_EPS)
    return xc * rstd, rstd


def _ln_bwd(dy, xhat, rstd, g):
    dxh = dy * g
    m1 = jnp.mean(dxh, axis=-1, keepdims=True)
    m2 = jnp.mean(dxh * xhat, axis=-1, keepdims=True)
    return rstd * (dxh - m1 - xhat * m2)


def _row_call(body, ins, row_ins, outs, acc_outs, name, tr=256):
    S = ins[0].shape[0]
    tr = min(tr, S)
    n_in, n_row, n_out = len(ins), len(row_ins), len(outs)

    def wrapped(*refs):
        i = pl.program_id(0)
        acc_refs = refs[n_in + n_row + n_out:]

        @pl.when(i == 0)
        def _():
            for r in acc_refs:
                r[...] = jnp.zeros_like(r)

        body(*refs)

    in_specs = [pl.BlockSpec((tr, a.shape[1]), lambda i: (i, 0)) for a in ins]
    in_specs += [pl.BlockSpec(a.shape, lambda i: (0, 0)) for a in row_ins]
    out_specs = [pl.BlockSpec((tr, s.shape[1]), lambda i: (i, 0)) for s in outs]
    out_specs += [pl.BlockSpec(s.shape, lambda i: (0, 0)) for s in acc_outs]
    return pl.pallas_call(wrapped, grid=(S // tr,), in_specs=in_specs, out_specs=out_specs,
                          out_shape=list(outs) + list(acc_outs),
                          compiler_params=_params(("arbitrary",)), name=name)(*ins, *row_ins)


def _ln1_fwd(x, mixed, g, b):
    def body(x_ref, m_ref, g_ref, b_ref, h_ref):
        xhat, _ = _ln_stats(DN_ALPHA * x_ref[...] + m_ref[...])
        h_ref[...] = xhat * g_ref[...] + b_ref[...]
    return _row_call(body, [x, mixed], [g, b], [jax.ShapeDtypeStruct(x.shape, F32)], [], "ln1_fwd")[0]


def _ln2_loss(h1, mlp, g, b, target):
    S, D = h1.shape
    sds = jax.ShapeDtypeStruct

    def body(h_ref, m_ref, t_ref, g_ref, b_ref, dr_ref, loss_ref, dg_ref, db_ref):
        xhat, rstd = _ln_stats(DN_ALPHA * h_ref[...] + m_ref[...])
        gv = g_ref[...]
        err = xhat * gv + b_ref[...] - t_ref[...]
        loss_ref[...] += jnp.sum(jnp.sum(err * err, axis=0, keepdims=True), axis=1, keepdims=True) * (0.5 / D)
        dy = err * (1.0 / D)
        dg_ref[...] += jnp.sum(dy * xhat, axis=0, keepdims=True)
        db_ref[...] += jnp.sum(dy, axis=0, keepdims=True)
        dr_ref[...] = _ln_bwd(dy, xhat, rstd, gv)

    return _row_call(body, [h1, mlp, target], [g, b], [sds((S, D), F32)],
                     [sds((1, LANE), F32), sds((1, D), F32), sds((1, D), F32)], "ln2_loss")


def _ln1_bwd(x, mixed, g, dr2, dh_mlp):
    S, D = x.shape
    sds = jax.ShapeDtypeStruct

    def body(x_ref, m_ref, dr2_ref, dh_ref, g_ref, dr_ref, dg_ref, db_ref):
        xhat, rstd = _ln_stats(DN_ALPHA * x_ref[...] + m_ref[...])
        dy = DN_ALPHA * dr2_ref[...] + dh_ref[...]
        dg_ref[...] += jnp.sum(dy * xhat, axis=0, keepdims=True)
        db_ref[...] += jnp.sum(dy, axis=0, keepdims=True)
        dr_ref[...] = _ln_bwd(dy, xhat, rstd, g_ref[...])

    return _row_call(body, [x, mixed, dr2, dh_mlp], [g], [sds((S, D), F32)],
                     [sds((1, D), F32), sds((1, D), F32)], "ln1_bwd")


def _grad_x(dr1, dx_proj, dep):
    def body(a_ref, b_ref, dep_ref, o_ref):
        o_ref[...] = DN_ALPHA * a_ref[...] + b_ref[...]
    return _row_call(body, [dr1, dx_proj], [dep], [jax.ShapeDtypeStruct(dr1.shape, F32)], [], "grad_x")[0]


def _bucket_table():
    qi = np.arange(BLK, dtype=np.int32)[:, None]
    kj = np.arange(2 * BLK, dtype=np.int32)[None, :]
    dist = qi + BLK - kj
    n = np.maximum(dist, 0)
    max_exact = N_BUCKETS // 2
    nf = np.maximum(n, 1).astype(np.float32)
    large = max_exact + (np.log(nf / np.float32(max_exact)) / np.float32(math.log(MAX_DISTANCE / max_exact))
                         * np.float32(N_BUCKETS - max_exact)).astype(np.int32)
    large = np.minimum(large, N_BUCKETS - 1)
    bucket = np.where(n < max_exact, n, large)
    return np.where((dist >= 0) & (dist < BLK), bucket, -1).astype(np.int32)


def _attn_bias(rel_bias_t):
    hq = rel_bias_t.shape[0]
    bucket = jnp.asarray(_bucket_table())

    def body(rb_ref, bk_ref, o_ref):
        h = pl.program_id(0)
        bk = bk_ref[...]
        acc = jnp.zeros((BLK, 2 * BLK), F32)
        for b in range(N_BUCKETS):
            acc = jnp.where(bk == b, rb_ref[h, b], acc)
        o_ref[...] = acc

    return pl.pallas_call(
        body, grid=(hq,),
        in_specs=[pl.BlockSpec(memory_space=pltpu.SMEM), pl.BlockSpec((BLK, 2 * BLK), lambda h: (0, 0))],
        out_specs=pl.BlockSpec((BLK, 2 * BLK), lambda h: (h, 0)),
        out_shape=jax.ShapeDtypeStruct((hq * BLK, 2 * BLK), F32),
        compiler_params=_params(("arbitrary",)), name="attn_bias")(rel_bias_t, bucket)


def _attn_probs(q, kc, kp, bias, sink, mask_c, mask_p):
    lc = jnp.where(mask_c, _dot(q, kc, NT) + bias[:, BLK:], NEG_INF)
    lp = jnp.where(mask_p, _dot(q, kp, NT) + bias[:, :BLK], NEG_INF)
    m = jnp.maximum(jnp.maximum(jnp.max(lc, axis=1, keepdims=True), jnp.max(lp, axis=1, keepdims=True)), sink)
    pc, pp, ps = jnp.exp(lc - m), jnp.exp(lp - m), jnp.exp(sink - m)
    inv = 1.0 / (jnp.sum(pc, axis=1, keepdims=True) + jnp.sum(pp, axis=1, keepdims=True) + ps)
    return pc, pp, ps, inv


def _attn_masks(n):
    qi = lax.broadcasted_iota(jnp.int32, (BLK, BLK), 0)
    kj = lax.broadcasted_iota(jnp.int32, (BLK, BLK), 1)
    return kj <= qi, (kj > qi) & (n > 0)


def _attn_fwd(proj, bias, sinks, hq, q_blk, k_blk, v_blk, out_width):
    S = proj.shape[0]
    hkv = hq // GQA
    wq, wk = hq * HEAD_A, hkv * HEAD_A

    def body(q_ref, k_ref, v_ref, bias_ref, sink_ref, o_ref):
        n = pl.program_id(0)
        cur = pl.multiple_of(n * BLK, BLK)
        prev = pl.multiple_of(jnp.maximum(n - 1, 0) * BLK, BLK)
        mask_c, mask_p = _attn_masks(n)
        for h4 in range(hkv):
            cs = slice(h4 * HEAD_A, (h4 + 1) * HEAD_A)
            kc, kp = k_ref[pl.ds(cur, BLK), cs].astype(BF16), k_ref[pl.ds(prev, BLK), cs].astype(BF16)
            vc, vp = v_ref[pl.ds(cur, BLK), cs].astype(BF16), v_ref[pl.ds(prev, BLK), cs].astype(BF16)
            for g in range(GQA):
                h = h4 * GQA + g
                hs = slice(h * HEAD_A, (h + 1) * HEAD_A)
                q = (q_ref[:, hs] * (HEAD_A ** -0.5)).astype(BF16)
                pc, pp, _, inv = _attn_probs(q, kc, kp, bias_ref[h * BLK:(h + 1) * BLK, :], sink_ref[h], mask_c, mask_p)
                o_ref[:, hs] = (_dot(pc.astype(BF16), vc, NN) + _dot(pp.astype(BF16), vp, NN)) * inv

    return pl.pallas_call(
        body, grid=(S // BLK,),
        in_specs=[pl.BlockSpec((BLK, wq), lambda n: (n, q_blk)), pl.BlockSpec((S, wk), lambda n: (0, k_blk)),
                  pl.BlockSpec((S, wk), lambda n: (0, v_blk)), pl.BlockSpec((hq * BLK, 2 * BLK), lambda n: (0, 0)),
                  pl.BlockSpec(memory_space=pltpu.SMEM)],
        out_specs=pl.BlockSpec((BLK, wq), lambda n: (n, 0)),
        out_shape=jax.ShapeDtypeStruct((S, out_width), F32),
        compiler_params=_params(("arbitrary",)), name="attn_fwd")(proj, proj, proj, bias, sinks)


def _attn_bwd(proj, bias, sinks, out, dmix, hq, q_blk, k_blk, v_blk):
    S = proj.shape[0]
    hkv = hq // GQA
    wq, wk = hq * HEAD_A, hkv * HEAD_A
    sds = jax.ShapeDtypeStruct

    def body(q_ref, k_ref, v_ref, bias_ref, sink_ref, o_ref, do_ref, dq_ref, dk_ref, dv_ref, dbias_ref, dsink_ref):
        n = pl.program_id(0)

        @pl.when(n == 0)
        def _():
            dk_ref[...] = jnp.zeros_like(dk_ref)
            dv_ref[...] = jnp.zeros_like(dv_ref)
            dbias_ref[...] = jnp.zeros_like(dbias_ref)
            dsink_ref[...] = jnp.zeros_like(dsink_ref)

        cur = pl.multiple_of(n * BLK, BLK)
        prev = pl.multiple_of(jnp.maximum(n - 1, 0) * BLK, BLK)
        mask_c, mask_p = _attn_masks(n)
        for h4 in range(hkv):
            cs = slice(h4 * HEAD_A, (h4 + 1) * HEAD_A)
            kc, kp = k_ref[pl.ds(cur, BLK), cs].astype(BF16), k_ref[pl.ds(prev, BLK), cs].astype(BF16)
            vc, vp = v_ref[pl.ds(cur, BLK), cs].astype(BF16), v_ref[pl.ds(prev, BLK), cs].astype(BF16)
            dkc = jnp.zeros((BLK, HEAD_A), F32)
            dkp = jnp.zeros((BLK, HEAD_A), F32)
            dvc = jnp.zeros((BLK, HEAD_A), F32)
            dvp = jnp.zeros((BLK, HEAD_A), F32)
            for g in range(GQA):
                h = h4 * GQA + g
                hs = slice(h * HEAD_A, (h + 1) * HEAD_A)
                rows = slice(h * BLK, (h + 1) * BLK)
                q = (q_ref[:, hs] * (HEAD_A ** -0.5)).astype(BF16)
                pc, pp, ps, inv = _attn_probs(q, kc, kp, bias_ref[rows, :], sink_ref[h], mask_c, mask_p)
                pc, pp, ps = pc * inv, pp * inv, ps * inv
                do = do_ref[:, hs]
                delta = jnp.sum(do * o_ref[:, hs], axis=1, keepdims=True)
                dob = do.astype(BF16)
                dsc = pc * (_dot(dob, vc, NT) - delta)
                dsp = pp * (_dot(dob, vp, NT) - delta)
                dsink_ref[h:h + 1, :] += jnp.broadcast_to(jnp.sum(-ps * delta, axis=0, keepdims=True), (1, LANE))
                dbias_ref[rows, BLK:] += dsc
                dbias_ref[rows, :BLK] += dsp
                dscb, dspb = dsc.astype(BF16), dsp.astype(BF16)
                dq_ref[:, hs] = (_dot(dscb, kc, NN) + _dot(dspb, kp, NN)) * (HEAD_A ** -0.5)
                dkc += _dot(dscb, q, TN)
                dkp += _dot(dspb, q, TN)
                dvc += _dot(pc.astype(BF16), dob, TN)
                dvp += _dot(pp.astype(BF16), dob, TN)
            dk_ref[pl.ds(cur, BLK), cs] += dkc
            dk_ref[pl.ds(prev, BLK), cs] += dkp
            dv_ref[pl.ds(cur, BLK), cs] += dvc
            dv_ref[pl.ds(prev, BLK), cs] += dvp

    return pl.pallas_call(
        body, grid=(S // BLK,),
        in_specs=[pl.BlockSpec((BLK, wq), lambda n: (n, q_blk)), pl.BlockSpec((S, wk), lambda n: (0, k_blk)),
                  pl.BlockSpec((S, wk), lambda n: (0, v_blk)), pl.BlockSpec((hq * BLK, 2 * BLK), lambda n: (0, 0)),
                  pl.BlockSpec(memory_space=pltpu.SMEM),
                  pl.BlockSpec((BLK, wq), lambda n: (n, 0)), pl.BlockSpec((BLK, wq), lambda n: (n, 0))],
        out_specs=[pl.BlockSpec((BLK, wq), lambda n: (n, 0)), pl.BlockSpec((S, wk), lambda n: (0, 0)),
                   pl.BlockSpec((S, wk), lambda n: (0, 0)), pl.BlockSpec((hq * BLK, 2 * BLK), lambda n: (0, 0)),
                   pl.BlockSpec((hq, LANE), lambda n: (0, 0))],
        out_shape=[sds((S, wq), F32), sds((S, wk), F32), sds((S, wk), F32), sds((hq * BLK, 2 * BLK), F32),
                   sds((hq, LANE), F32)],
        compiler_params=_params(("arbitrary",)), name="attn_bwd")(proj, proj, proj, bias, sinks, out, dmix)


def _rel_bias_grad(dbias, hq):
    bucket = jnp.asarray(_bucket_table())

    def body(d_ref, bk_ref, o_ref):
        d = d_ref[...]
        bk = bk_ref[...]
        rows = [jnp.sum(jnp.where(bk == b, d, 0.0), axis=0, keepdims=True) for b in range(N_BUCKETS)]
        tot = jnp.sum(jnp.concatenate(rows, axis=0), axis=1, keepdims=True)
        o_ref[...] = jnp.broadcast_to(tot, (N_BUCKETS, LANE))

    return pl.pallas_call(
        body, grid=(hq,),
        in_specs=[pl.BlockSpec((BLK, 2 * BLK), lambda h: (h, 0)), pl.BlockSpec((BLK, 2 * BLK), lambda h: (0, 0))],
        out_specs=pl.BlockSpec((None, N_BUCKETS, LANE), lambda h: (h, 0, 0)),
        out_shape=jax.ShapeDtypeStruct((hq, N_BUCKETS, LANE), F32),
        compiler_params=_params(("arbitrary",)), name="rel_bias_grad")(dbias, bucket)


def _sigmoid(x):
    return 1.0 / (1.0 + jnp.exp(-x))


def _shift_rows(x, s):
    n = x.shape[0]
    row = lax.broadcasted_iota(jnp.int32, x.shape, 0)
    if s > 0:
        return jnp.where(row >= s, pltpu.roll(x, s, 0), 0.0)
    return jnp.where(row < n + s, pltpu.roll(x, n + s, 0), 0.0)


def _conv_silu_norm(xv, w, j, nh):
    c = w[CONV_W - 1:CONV_W, :] * xv
    for s in range(1, CONV_W):
        c = c + w[CONV_W - 1 - s:CONV_W - s, :] * _shift_rows(xv, s)
    sg = _sigmoid(c)
    a = c * sg
    r = lax.rsqrt(jnp.sum(a * a, axis=1, keepdims=True) + RMS_EPS)
    scale = jnp.where(j < nh, HEAD_D ** -0.5, 1.0)
    is_norm = j < 2 * nh
    y = jnp.where(is_norm, a * (r * scale), a)
    return c, sg, a, r, scale, is_norm, y


def _gdn_prep_fwd(proj, conv_w, nh, blk0):
    S = proj.shape[0]

    def body(x_ref, w_ref, o_ref):
        j = pl.program_id(0)
        o_ref[...] = _conv_silu_norm(x_ref[...], w_ref[...], j, nh)[-1]

    return pl.pallas_call(
        body, grid=(3 * nh,),
        in_specs=[pl.BlockSpec((S, HEAD_D), lambda j: (0, blk0 + j)), pl.BlockSpec((CONV_W, HEAD_D), lambda j: (0, j))],
        out_specs=pl.BlockSpec((S, HEAD_D), lambda j: (0, 3 * (j % nh) + j // nh)),
        out_shape=jax.ShapeDtypeStruct((S, 3 * nh * HEAD_D), F32),
        compiler_params=_params(("parallel",)), name="gdn_prep_fwd")(proj, conv_w)


def _gdn_prep_bwd(proj, conv_w, dqkv, nh, blk0):
    S = proj.shape[0]
    sds = jax.ShapeDtypeStruct

    def body(x_ref, w_ref, dy_ref, dx_ref, dw_ref):
        j = pl.program_id(0)
        xv, w = x_ref[...], w_ref[...]
        c, sg, a, r, scale, is_norm, _ = _conv_silu_norm(xv, w, j, nh)
        dy = dy_ref[...]
        rs = r * scale
        da_n = rs * dy - a * (r * r * rs) * jnp.sum(dy * a, axis=1, keepdims=True)
        da = jnp.where(is_norm, da_n, dy)
        dc = da * (sg * (1.0 + c * (1.0 - sg)))
        dx = w[CONV_W - 1:CONV_W, :] * dc
        dws = [jnp.sum(dc * xv, axis=0, keepdims=True)]
        for s in range(1, CONV_W):
            dx = dx + w[CONV_W - 1 - s:CONV_W - s, :] * _shift_rows(dc, -s)
            dws.insert(0, jnp.sum(dc * _shift_rows(xv, s), axis=0, keepdims=True))
        dx_ref[...] = dx
        dw_ref[...] = jnp.concatenate(dws, axis=0)

    return pl.pallas_call(
        body, grid=(3 * nh,),
        in_specs=[pl.BlockSpec((S, HEAD_D), lambda j: (0, blk0 + j)), pl.BlockSpec((CONV_W, HEAD_D), lambda j: (0, j)),
                  pl.BlockSpec((S, HEAD_D), lambda j: (0, 3 * (j % nh) + j // nh))],
        out_specs=[pl.BlockSpec((S, HEAD_D), lambda j: (0, j)), pl.BlockSpec((CONV_W, HEAD_D), lambda j: (0, j))],
        out_shape=[sds((S, 3 * nh * HEAD_D), F32), sds((CONV_W, 3 * nh * HEAD_D), F32)],
        compiler_params=_params(("parallel",)), name="gdn_prep_bwd")(proj, conv_w, dqkv)


def _softplus(x):
    return jnp.maximum(x, 0.0) + jnp.log(1.0 + jnp.exp(-jnp.abs(x)))


def _gates_fwd(ab, al, dt, nh):
    S = ab.shape[0]

    def body(ab_ref, al_ref, dt_ref, o_ref):
        v = ab_ref[...]
        lane = lax.broadcasted_iota(jnp.int32, v.shape, 1)
        g = -jnp.exp(al_ref[...]) * _softplus(v + dt_ref[...])
        o_ref[...] = jnp.where(lane < nh, g, jnp.where(lane < 2 * nh, _sigmoid(v), 0.0))

    row = pl.BlockSpec((1, LANE), lambda i: (0, 0))
    full = pl.BlockSpec((S, LANE), lambda i: (0, 0))
    return pl.pallas_call(body, grid=(1,), in_specs=[full, row, row], out_specs=full,
                          out_shape=jax.ShapeDtypeStruct((S, LANE), F32),
                          compiler_params=_params(("arbitrary",)), name="gates_fwd")(ab, al, dt)


def _gates_bwd(ab, al, dt, dgb, nh):
    S = ab.shape[0]
    sds = jax.ShapeDtypeStruct

    def body(ab_ref, al_ref, dt_ref, d_ref, dab_ref, dal_ref, ddt_ref):
        v, d = ab_ref[...], d_ref[...]
        lane = lax.broadcasted_iota(jnp.int32, v.shape, 1)
        is_a = lane < nh
        z = v + dt_ref[...]
        dsp = jnp.where(is_a, d * (-jnp.exp(al_ref[...])), 0.0)
        dz = dsp * _sigmoid(z)
        beta = _sigmoid(v)
        dab_ref[...] = jnp.where(is_a, dz, jnp.where(lane < 2 * nh, d * beta * (1.0 - beta), 0.0))
        dal_ref[...] = jnp.sum(dsp * _softplus(z), axis=0, keepdims=True)
        ddt_ref[...] = jnp.sum(dz, axis=0, keepdims=True)

    row = pl.BlockSpec((1, LANE), lambda i: (0, 0))
    full = pl.BlockSpec((S, LANE), lambda i: (0, 0))
    return pl.pallas_call(body, grid=(1,), in_specs=[full, row, row, full], out_specs=[full, row, row],
                          out_shape=[sds((S, LANE), F32), sds((1, LANE), F32), sds((1, LANE), F32)],
                          compiler_params=_params(("arbitrary",)), name="gates_bwd")(ab, al, dt, dgb)


def _col_of(tile, h):
    lane = lax.broadcasted_iota(jnp.int32, tile.shape, 1)
    return jnp.sum(jnp.where(lane == h, tile, 0.0), axis=1, keepdims=True)


def _to_row(col, eye):
    return jnp.sum(jnp.where(eye, col, 0.0), axis=0, keepdims=True)


def _to_col(row, eye):
    return jnp.sum(jnp.where(eye, row, 0.0), axis=1, keepdims=True)


def _split(a):
    hi = a.astype(BF16)
    return hi, (a - hi.astype(F32)).astype(BF16)


def _gdot(a, b, dims):
    ah, al = _split(a)
    bh, bl = _split(b)
    return _dot(ah, bh, dims) + (_dot(ah, bl, dims) + _dot(al, bh, dims))


def _chunk_local(q, k, v, gcol, bcol, T=None):
    C = CHUNK
    row = lax.broadcasted_iota(jnp.int32, (C, C), 0)
    col = lax.broadcasted_iota(jnp.int32, (C, C), 1)
    tril, strict, eye = col <= row, col < row, col == row
    grow = _to_row(gcol, eye)
    G_row = jnp.sum(jnp.where(row <= col, gcol, 0.0), axis=0, keepdims=True)
    G_col = jnp.sum(jnp.where(tril, grow, 0.0), axis=1, keepdims=True)
    decay = jnp.exp(jnp.where(tril, G_col - G_row, NEG_INF))
    G_last = G_col[C - 1:C, :]
    eG = jnp.exp(G_col)
    eGr = jnp.exp(G_last - G_col)
    gl = jnp.exp(G_last)
    kb = k * bcol
    A = jnp.where(strict, _gdot(kb, k, NT) * decay, 0.0)
    attn = _gdot(q, k, NT) * decay
    out = dict(strict=strict, eye=eye, row=row, col=col, decay=decay, eG=eG, eGr=eGr, gl=gl, kb=kb, A=A,
               rhs_k=kb * eG, attn=attn, q_dec=q * eG, k_dec=k * eGr)
    if T is None:
        T = jnp.where(eye, 1.0, 0.0) - A
        P = A
        for _ in range(int(math.log2(C)) - 1):
            P = _gdot(P, P, NN)
            T = T + _gdot(T, P, NN)
        out.update(T=T, u=_gdot(T, v * bcol, NN), w=_gdot(T, out["rhs_k"], NN))
    return out


GDN_ROWS = 256
GDN_LOCAL_ROWS = 1024
WQK = 3 * CHUNK


def _gdn_local_fwd(qkv, gb, nh):
    S = qkv.shape[0]
    nc = S // CHUNK
    rb = min(GDN_LOCAL_ROWS, S)
    cpb = rb // CHUNK
    sds = jax.ShapeDtypeStruct

    def body(q_ref, k_ref, v_ref, gb_ref, u_ref, wqk_ref, attn_ref, t_ref):
        h = pl.program_id(0)
        for ci in range(cpb):
            rows = slice(ci * CHUNK, (ci + 1) * CHUNK)
            gbt = gb_ref[rows, :]
            L = _chunk_local(q_ref[rows, :], k_ref[rows, :], v_ref[rows, :], _col_of(gbt, h), _col_of(gbt, nh + h))
            u_ref[rows, :] = L["u"]
            base = ci * WQK
            wqk_ref[base:base + CHUNK, :] = L["w"]
            wqk_ref[base + CHUNK:base + 2 * CHUNK, :] = L["q_dec"]
            wqk_ref[base + 2 * CHUNK:base + WQK, :] = L["k_dec"]
            attn_ref[ci] = L["attn"]
            t_ref[ci] = L["T"]

    cc = pl.BlockSpec((None, cpb, CHUNK, CHUNK), lambda h, i: (h, i, 0, 0))
    return pl.pallas_call(
        body, grid=(nh, S // rb),
        in_specs=[pl.BlockSpec((rb, HEAD_D), lambda h, i: (i, 3 * h)), pl.BlockSpec((rb, HEAD_D), lambda h, i: (i, 3 * h + 1)),
                  pl.BlockSpec((rb, HEAD_D), lambda h, i: (i, 3 * h + 2)), pl.BlockSpec((rb, LANE), lambda h, i: (i, 0))],
        out_specs=[pl.BlockSpec((rb, HEAD_D), lambda h, i: (i, h)),
                   pl.BlockSpec((None, 3 * rb, HEAD_D), lambda h, i: (h, i, 0)), cc, cc],
        out_shape=[sds((S, nh * HEAD_D), F32), sds((nh, 3 * S, HEAD_D), F32), sds((nh, nc, CHUNK, CHUNK), F32),
                   sds((nh, nc, CHUNK, CHUNK), F32)],
        compiler_params=_params(("parallel", "parallel")), name="gdn_local_fwd")(qkv, qkv, qkv, gb)


def _gdn_scan_fwd(u, wqk, attn, gb, nh, dep):
    S = u.shape[0]
    nc = S // CHUNK
    rb = min(GDN_ROWS, S)
    cpb = rb // CHUNK
    sds = jax.ShapeDtypeStruct

    def body(u_ref, wqk_ref, attn_ref, gb_ref, dep_ref, o_ref, vn_ref, st_ref, s_ref):
        @pl.when(pl.program_id(0) == 0)
        def _():
            s_ref[...] = jnp.zeros_like(s_ref)

        for ci in range(cpb):
            rows = slice(ci * CHUNK, (ci + 1) * CHUNK)
            glv = jnp.exp(jnp.sum(gb_ref[rows, :], axis=0, keepdims=True))
            base = ci * WQK
            for h in range(nh):
                cols = slice(h * HEAD_D, (h + 1) * HEAD_D)
                state = s_ref[h]
                st_ref[h, ci] = state
                r = _gdot(wqk_ref[h, base:base + 2 * CHUNK, :], state, NN)
                vb = u_ref[rows, cols] - r[:CHUNK]
                o_ref[rows, cols] = r[CHUNK:] + _gdot(attn_ref[h, ci], vb, NN)
                vn_ref[rows, cols] = vb
                s_ref[h] = state * glv[:, h:h + 1] + _gdot(wqk_ref[h, base + 2 * CHUNK:base + WQK, :], vb, TN)

    return pl.pallas_call(
        body, grid=(S // rb,),
        in_specs=[pl.BlockSpec((rb, nh * HEAD_D), lambda i: (i, 0)), pl.BlockSpec((nh, 3 * rb, HEAD_D), lambda i: (0, i, 0)),
                  pl.BlockSpec((nh, cpb, CHUNK, CHUNK), lambda i: (0, i, 0, 0)), pl.BlockSpec((rb, LANE), lambda i: (i, 0)),
                  pl.BlockSpec(memory_space=pl.ANY)],
        out_specs=[pl.BlockSpec((rb, nh * HEAD_D), lambda i: (i, 0)), pl.BlockSpec((rb, nh * HEAD_D), lambda i: (i, 0)),
                   pl.BlockSpec((nh, cpb, HEAD_D, HEAD_D), lambda i: (0, i, 0, 0))],
        out_shape=[sds((S, nh * HEAD_D), F32), sds((S, nh * HEAD_D), F32), sds((nh, nc, HEAD_D, HEAD_D), F32)],
        scratch_shapes=[pltpu.VMEM((nh, HEAD_D, HEAD_D), F32)],
        compiler_params=_params(("arbitrary",)), name="gdn_scan_fwd")(u, wqk, attn, gb, dep)


def _gdn_scan_bwd(wqk, attn, gb, states, vn, do, nh, dep):
    S = vn.shape[0]
    nc = S // CHUNK
    rb = min(GDN_ROWS, S)
    cpb = rb // CHUNK
    last = S // rb - 1
    sds = jax.ShapeDtypeStruct

    def body(wqk_ref, attn_ref, gb_ref, st_ref, vn_ref, do_ref, dep_ref, dvn_ref, dw_ref, dqd_ref, dkd_ref, da_ref, dgl_ref,
             ds_ref):
        @pl.when(pl.program_id(0) == 0)
        def _():
            ds_ref[...] = jnp.zeros_like(ds_ref)

        row = lax.broadcasted_iota(jnp.int32, (CHUNK, CHUNK), 0)
        col = lax.broadcasted_iota(jnp.int32, (CHUNK, CHUNK), 1)
        for ci in reversed(range(cpb)):
            rows = slice(ci * CHUNK, (ci + 1) * CHUNK)
            glv = jnp.exp(jnp.sum(gb_ref[rows, :], axis=0, keepdims=True))
            base = ci * WQK
            for h in range(nh):
                cols = slice(h * HEAD_D, (h + 1) * HEAD_D)
                state, dS = st_ref[h, ci], ds_ref[h]
                wq = wqk_ref[h, base:base + 2 * CHUNK, :]
                kd = wqk_ref[h, base + 2 * CHUNK:base + WQK, :]
                vb = vn_ref[rows, cols]
                dob = do_ref[rows, cols]
                dvb = _gdot(attn_ref[h, ci], dob, TN) + _gdot(kd, dS, NN)
                x = _gdot(jnp.concatenate([dob, dvb], axis=0), state, NT)
                dqd_ref[rows, cols] = x[:CHUNK]
                dw_ref[rows, cols] = -x[CHUNK:]
                dvn_ref[rows, cols] = dvb
                da_ref[h, ci] = jnp.where(col <= row, _gdot(dob, vb, NT), 0.0)
                dkd_ref[rows, cols] = _gdot(vb, dS, NT)
                gl = glv[:, h:h + 1]
                dgl = jnp.sum(jnp.sum(state * dS, axis=0, keepdims=True), axis=1, keepdims=True)
                dgl_ref[h, ci] = jnp.broadcast_to(dgl * gl, (1, LANE))
                ds_ref[h] = dS * gl + _gdot(wq, jnp.concatenate([-dvb, dob], axis=0), TN)

    rv = lambda i: last - i
    wide = pl.BlockSpec((rb, nh * HEAD_D), lambda i: (rv(i), 0))
    return pl.pallas_call(
        body, grid=(S // rb,),
        in_specs=[pl.BlockSpec((nh, 3 * rb, HEAD_D), lambda i: (0, rv(i), 0)),
                  pl.BlockSpec((nh, cpb, CHUNK, CHUNK), lambda i: (0, rv(i), 0, 0)),
                  pl.BlockSpec((rb, LANE), lambda i: (rv(i), 0)),
                  pl.BlockSpec((nh, cpb, HEAD_D, HEAD_D), lambda i: (0, rv(i), 0, 0)), wide, wide,
                  pl.BlockSpec(memory_space=pl.ANY)],
        out_specs=[wide, wide, wide, wide, pl.BlockSpec((nh, cpb, CHUNK, CHUNK), lambda i: (0, rv(i), 0, 0)),
                   pl.BlockSpec((nh, cpb, 1, LANE), lambda i: (0, rv(i), 0, 0))],
        out_shape=[sds((S, nh * HEAD_D), F32), sds((S, nh * HEAD_D), F32), sds((S, nh * HEAD_D), F32),
                   sds((S, nh * HEAD_D), F32), sds((nh, nc, CHUNK, CHUNK), F32), sds((nh, nc, 1, LANE), F32)],
        scratch_shapes=[pltpu.VMEM((nh, HEAD_D, HEAD_D), F32)],
        compiler_params=_params(("arbitrary",)), name="gdn_scan_bwd")(wqk, attn, gb, states, vn, do, dep)


def _gdn_local_bwd(qkv, gb, T, u, wqk, dvn, dw, dqd, dkd, dattn, dgl, nh):
    S = qkv.shape[0]
    rb = min(GDN_LOCAL_ROWS, S)
    cpb = rb // CHUNK
    sds = jax.ShapeDtypeStruct

    def body(q_ref, k_ref, v_ref, gb_ref, t_ref, u_ref, wqk_ref, dvn_ref, dw_ref, dqd_ref, dkd_ref, da_ref, dgl_ref,
             dqkv_ref, dg_ref, db_ref):
        h = pl.program_id(0)
        for ci in range(cpb):
            rows = slice(ci * CHUNK, (ci + 1) * CHUNK)
            q, k, v = q_ref[rows, :], k_ref[rows, :], v_ref[rows, :]
            gbt = gb_ref[rows, :]
            bcol = _col_of(gbt, nh + h)
            T = t_ref[ci]
            L = _chunk_local(q, k, v, _col_of(gbt, h), bcol, T=T)
            strict, eye, decay, eG, eGr, kb = L["strict"], L["eye"], L["decay"], L["eG"], L["eGr"], L["kb"]
            attn, q_dec, k_dec = L["attn"], L["q_dec"], L["k_dec"]
            w = wqk_ref[ci * WQK:ci * WQK + CHUNK, :]
            dq_dec, dk_dec, dattn_c = dqd_ref[rows, :], dkd_ref[rows, :], da_ref[ci]
            drv = _gdot(T, dvn_ref[rows, :], TN)
            drk = _gdot(T, dw_ref[rows, :], TN)
            dA = jnp.where(strict, -(_gdot(drv, u_ref[rows, :], NT) + _gdot(drk, w, NT)), 0.0)
            dM = dA * decay
            dN = dattn_c * decay
            dkb = _gdot(dM, k, NN)
            dqkv_ref[rows, :HEAD_D] = _gdot(dN, k, NN) + dq_dec * eG
            dqkv_ref[rows, HEAD_D:2 * HEAD_D] = (drk * (bcol * eG) + _gdot(dM, kb, TN) + dkb * bcol + _gdot(dN, q, TN)
                                                 + dk_dec * eGr)
            dqkv_ref[rows, 2 * HEAD_D:] = drv * bcol
            db_ref[rows, :] = (jnp.sum(drv * v, axis=1, keepdims=True) + jnp.sum(drk * k, axis=1, keepdims=True) * eG
                               + jnp.sum(dkb * k, axis=1, keepdims=True))
            E = dA * L["A"] + dattn_c * attn
            kd = jnp.sum(dk_dec * k_dec, axis=1, keepdims=True)
            dG = (jnp.sum(dq_dec * q_dec, axis=1, keepdims=True) - kd + jnp.sum(drk * L["rhs_k"], axis=1, keepdims=True)
                  + jnp.sum(E, axis=1, keepdims=True) - _to_col(jnp.sum(E, axis=0, keepdims=True), eye))
            d_last = jnp.sum(kd, axis=0, keepdims=True) + dgl_ref[ci][:, :1]
            dG = dG + jnp.where(L["row"][:, :1] == CHUNK - 1, d_last, 0.0)
            dg_ref[rows, :] = jnp.sum(jnp.where(L["col"] >= L["row"], _to_row(dG, eye), 0.0), axis=1, keepdims=True)

    hd = pl.BlockSpec((rb, HEAD_D), lambda h, i: (i, h))
    cc = pl.BlockSpec((None, cpb, CHUNK, CHUNK), lambda h, i: (h, i, 0, 0))
    col1 = pl.BlockSpec((None, rb, 1), lambda h, i: (h, i, 0))
    return pl.pallas_call(
        body, grid=(nh, S // rb),
        in_specs=[pl.BlockSpec((rb, HEAD_D), lambda h, i: (i, 3 * h)), pl.BlockSpec((rb, HEAD_D), lambda h, i: (i, 3 * h + 1)),
                  pl.BlockSpec((rb, HEAD_D), lambda h, i: (i, 3 * h + 2)), pl.BlockSpec((rb, LANE), lambda h, i: (i, 0)),
                  cc, hd, pl.BlockSpec((None, 3 * rb, HEAD_D), lambda h, i: (h, i, 0)), hd, hd, hd, hd, cc,
                  pl.BlockSpec((None, cpb, 1, LANE), lambda h, i: (h, i, 0, 0))],
        out_specs=[pl.BlockSpec((rb, 3 * HEAD_D), lambda h, i: (i, h)), col1, col1],
        out_shape=[sds((S, 3 * nh * HEAD_D), F32)] + [sds((nh, S, 1), F32)] * 2,
        compiler_params=_params(("parallel", "parallel")), name="gdn_local_bwd")(
            qkv, qkv, qkv, gb, T, u, wqk, dvn, dw, dqd, dkd, dattn, dgl)


def _gated_norm_fwd(o, proj, norm_w, nh, z_blk0, mix, m_blk0):
    S = o.shape[0]

    def body(o_ref, z_ref, w_ref, mix_ref, y_ref):
        ov, z = o_ref[...], z_ref[...]
        r = lax.rsqrt(jnp.mean(ov * ov, axis=1, keepdims=True) + RMS_EPS)
        y_ref[...] = ov * r * w_ref[...] * (z * _sigmoid(z))

    return pl.pallas_call(
        body, grid=(nh,),
        in_specs=[pl.BlockSpec((S, HEAD_D), lambda h: (0, h)), pl.BlockSpec((S, HEAD_D), lambda h: (0, z_blk0 + h)),
                  pl.BlockSpec((1, HEAD_D), lambda h: (0, 0)), pl.BlockSpec(memory_space=pl.ANY)],
        out_specs=pl.BlockSpec((S, HEAD_D), lambda h: (0, m_blk0 + h)),
        out_shape=jax.ShapeDtypeStruct(mix.shape, F32), input_output_aliases={3: 0},
        compiler_params=_params(("parallel",)), name="gated_norm_fwd")(o, proj, norm_w, mix)


def _gated_norm_bwd(o, proj, norm_w, dmix, nh, z_blk0, d_blk0):
    S = o.shape[0]
    sds = jax.ShapeDtypeStruct

    def body(o_ref, z_ref, w_ref, dy_ref, do_ref, dz_ref, dw_ref):
        ov, z, w, dy = o_ref[...], z_ref[...], w_ref[...], dy_ref[...]
        r = lax.rsqrt(jnp.mean(ov * ov, axis=1, keepdims=True) + RMS_EPS)
        oh = ov * r
        sg = _sigmoid(z)
        dz_ref[...] = dy * (oh * w) * (sg * (1.0 + z * (1.0 - sg)))
        don = dy * (z * sg)
        @pl.when(pl.program_id(0) == 0)
        def _():
            dw_ref[...] = jnp.zeros_like(dw_ref)

        dw_ref[...] += jnp.sum(don * oh, axis=0, keepdims=True)
        doh = don * w
        do_ref[...] = r * (doh - oh * jnp.mean(doh * oh, axis=1, keepdims=True))

    return pl.pallas_call(
        body, grid=(nh,),
        in_specs=[pl.BlockSpec((S, HEAD_D), lambda h: (0, h)), pl.BlockSpec((S, HEAD_D), lambda h: (0, z_blk0 + h)),
                  pl.BlockSpec((1, HEAD_D), lambda h: (0, 0)), pl.BlockSpec((S, HEAD_D), lambda h: (0, d_blk0 + h))],
        out_specs=[pl.BlockSpec((S, HEAD_D), lambda h: (0, h)), pl.BlockSpec((S, HEAD_D), lambda h: (0, h)),
                   pl.BlockSpec((1, HEAD_D), lambda h: (0, 0))],
        out_shape=[sds((S, nh * HEAD_D), F32), sds((S, nh * HEAD_D), F32), sds((1, HEAD_D), F32)],
        compiler_params=_params(("arbitrary",)), name="gated_norm_bwd")(o, proj, norm_w, dmix)


def _adamw_math(w, g, m, v):
    m = ADAM_B1 * m + (1.0 - ADAM_B1) * g
    v = ADAM_B2 * v + (1.0 - ADAM_B2) * (g * g)
    m_hat = m / (1.0 - ADAM_B1 ** ADAM_STEP)
    v_hat = v / (1.0 - ADAM_B2 ** ADAM_STEP)
    delta = -ADAM_LR * (m_hat / (jnp.sqrt(v_hat) + ADAM_EPS) + ADAM_WD * w)
    return delta, m, v


def _slab_tiles(R, C, rows=256, cols=256):
    if R % rows == 0:
        return (rows, C), R // rows, lambda i: (i, 0)
    tc = _tile(C, cols)
    return (R, tc), C // tc, lambda i: (0, i)


def _adamw_big(parts, terms, chip, w, m, v, name):
    R, C = w.shape
    blk, steps, at = _slab_tiles(R, C)
    sds = jax.ShapeDtypeStruct

    def body(q_ref, p_ref, t_ref, w_ref, m_ref, v_ref, g_ref, d_ref, nm_ref, nv_ref):
        g = ((p_ref[...].astype(F32) + t_ref[0].astype(F32)) + t_ref[1].astype(F32)) + t_ref[2].astype(F32)
        g_ref[...] = g
        d_ref[...], nm_ref[...], nv_ref[...] = _adamw_math(w_ref[...], g, m_ref[...], v_ref[...])

    spec = pl.BlockSpec(blk, lambda i, q_ref: at(i))
    grid_spec = pltpu.PrefetchScalarGridSpec(
        num_scalar_prefetch=1, grid=(steps,),
        in_specs=[pl.BlockSpec((None,) + blk, lambda i, q_ref: (q_ref[0],) + at(i)),
                  pl.BlockSpec((3,) + blk, lambda i, q_ref: (0,) + at(i)), spec, spec, spec],
        out_specs=[spec] * 4)
    return pl.pallas_call(body, grid_spec=grid_spec, out_shape=[sds((R, C), F32)] * 4,
                          compiler_params=_params(("parallel",)), name=name)(chip, parts, terms, w, m, v)


def _adamw_small(ws, gs, ms, vs):
    n = len(ws)

    def body(*refs):
        for i in range(n):
            w, g, m, v = (refs[k * n + i][...] for k in range(4))
            d, nm, nv = _adamw_math(w, g, m, v)
            refs[4 * n + i][...] = d
            refs[5 * n + i][...] = nm
            refs[6 * n + i][...] = nv

    shapes = [jax.ShapeDtypeStruct(w.shape, F32) for w in ws]
    vm = pl.BlockSpec(memory_space=pltpu.VMEM)
    outs = pl.pallas_call(body, in_specs=[vm] * (4 * n), out_specs=[vm] * (3 * n), out_shape=shapes * 3,
                          name="adamw_small")(*ws, *gs, *ms, *vs)
    return outs[:n], outs[n:2 * n], outs[2 * n:]


MESH = pl.DeviceIdType.MESH
ANY = pl.BlockSpec(memory_space=pl.ANY)


def _place():
    x, y, c = lax.axis_index("x"), lax.axis_index("y"), lax.axis_index("c")
    return x, y, c, [(1 - x, y), (x, 1 - y), (1 - x, 1 - y)]


def _all_gather(shards):
    n = len(shards)

    def body(*refs):
        ins, outs = refs[:n], refs[n:2 * n]
        send_sems, recv_sems, local_sems = refs[2 * n:]
        x, y, c, chips = _place()
        me, sibling = (x, y, c), (x, y, 1 - c)

        def slot(px, py, pc):
            return 4 * px + 2 * py + pc

        def copy(w, k, block, to, src=None):
            dst = outs[w].at[slot(*block)]
            return pltpu.make_async_remote_copy(src_ref=dst if src is None else src, dst_ref=dst,
                                                send_sem=send_sems.at[w * 7 + k], recv_sem=recv_sems.at[w * 7 + k],
                                                device_id=to, device_id_type=MESH)

        mine = [pltpu.make_async_copy(ins[w], outs[w].at[slot(*me)], local_sems.at[w]) for w in range(n)]
        for cp in mine:
            cp.start()
        first = []
        for w in range(n):
            first.append(copy(w, 0, me, sibling, src=ins[w]))
            first += [copy(w, 1 + j, me, (*chip, c), src=ins[w]) for j, chip in enumerate(chips)]
        for cp in first:
            cp.start()
        passed = []
        for w in range(n):
            for j, chip in enumerate(chips):
                copy(w, 1 + j, (*chip, c), me).wait_recv()
                cp = copy(w, 4 + j, (*chip, c), sibling)
                cp.start()
                passed.append(cp)
        for w in range(n):
            copy(w, 0, sibling, me).wait_recv()
            for j, chip in enumerate(chips):
                copy(w, 4 + j, (*chip, 1 - c), me).wait_recv()
        for cp in first + passed:
            cp.wait_send()
        for cp in mine:
            cp.wait()

    return pl.pallas_call(
        body, in_specs=[ANY] * n, out_specs=[ANY] * n,
        out_shape=[jax.ShapeDtypeStruct((N_DEV,) + s.shape, s.dtype) for s in shards],
        scratch_shapes=[pltpu.SemaphoreType.DMA((7 * n,)), pltpu.SemaphoreType.DMA((7 * n,)),
                        pltpu.SemaphoreType.DMA((n,))],
        name="all_gather_weights")(*shards)


def _chip_sum(grad, recv, core, name):
    _, R, C = grad.shape
    blk, steps, at = _slab_tiles(R, C)

    def body(c_ref, g_ref, r_ref, o_ref):
        o_ref[...] = (g_ref[...].astype(F32) + r_ref[...].astype(F32)).astype(o_ref.dtype)

    grid_spec = pltpu.PrefetchScalarGridSpec(
        num_scalar_prefetch=1, grid=(4, steps),
        in_specs=[pl.BlockSpec((None,) + blk, lambda q, i, c_ref: (2 * q + c_ref[0],) + at(i)),
                  pl.BlockSpec((None,) + blk, lambda q, i, c_ref: (q,) + at(i))],
        out_specs=pl.BlockSpec((None,) + blk, lambda q, i, c_ref: (q,) + at(i)))
    return pl.pallas_call(body, grid_spec=grid_spec, out_shape=jax.ShapeDtypeStruct((4, R, C), BF16),
                          compiler_params=_params(("parallel", "parallel")), name=name)(core, grad, recv)


HBM_SPEC = pl.BlockSpec(memory_space=pltpu.HBM)
SEM_SPEC = pl.BlockSpec(memory_space=pltpu.SEMAPHORE)
DATAFLOW = pltpu.SideEffectType.DATAFLOW_SIDE_EFFECTING


def _split_start(name, bufs, plan, n, after=None):
    nb = len(bufs)
    extra = [] if after is None else [after]

    def body(*refs):
        send_sems, recv_sems, token = refs[nb + len(extra)], refs[nb + len(extra) + 1], refs[-1]
        for k, (src, dst, to) in enumerate(plan(refs[:nb])):
            pltpu.make_async_remote_copy(src_ref=src, dst_ref=dst, send_sem=send_sems.at[k], recv_sem=recv_sems.at[k],
                                         device_id=to, device_id_type=MESH).start()
        token[...] = jnp.zeros_like(token)

    outs = pl.pallas_call(
        body, name=name,
        out_shape=(pltpu.SemaphoreType.DMA((n,)), pltpu.SemaphoreType.DMA((n,)),
                   *[pltpu.HBM(b.shape, b.dtype) for b in bufs], jax.ShapeDtypeStruct((8, LANE), F32)),
        in_specs=[HBM_SPEC] * nb + [ANY] * len(extra),
        out_specs=(SEM_SPEC, SEM_SPEC, *[HBM_SPEC] * nb, pl.BlockSpec(memory_space=pltpu.VMEM)),
        input_output_aliases={i: 2 + i for i in range(nb)},
        compiler_params=pltpu.CompilerParams(has_side_effects=DATAFLOW))(
            *[pltpu.with_memory_space_constraint(b, pltpu.HBM) for b in bufs], *extra)
    return outs[0], outs[1], list(outs[2:2 + nb]), outs[-1]


def _split_wait(name, send_sems, recv_sems, bufs, plan, n, after):
    nb = len(bufs)

    def body(*refs):
        send_s, recv_s = refs[nb], refs[nb + 1]
        for k, (src, dst, to) in enumerate(plan(refs[:nb])):
            cp = pltpu.make_async_remote_copy(src_ref=src, dst_ref=dst, send_sem=send_s.at[k], recv_sem=recv_s.at[k],
                                              device_id=to, device_id_type=MESH)
            cp.wait_send()
            cp.wait_recv()

    outs = pl.pallas_call(
        body, name=name, out_shape=tuple(pltpu.HBM(b.shape, b.dtype) for b in bufs),
        in_specs=[HBM_SPEC] * nb + [SEM_SPEC, SEM_SPEC, ANY], out_specs=tuple([HBM_SPEC] * nb),
        input_output_aliases={i: i for i in range(nb)},
        compiler_params=pltpu.CompilerParams(has_side_effects=DATAFLOW))(*bufs, send_sems, recv_sems, after)
    return list(outs)


def _slot(px, py, pc):
    return 4 * px + 2 * py + pc


class _Exchanges:
    def __init__(self, tag, shards, dev, core):
        self.tag, self.shards, self.dev, self.core = tag, shards, dev, core
        self.n = len(shards)

    def _gather_plan1(self, refs):
        n = self.n
        x, y, c, chips = _place()
        out = []
        for w in range(n):
            dst = refs[n + w].at[_slot(x, y, c)]
            out.append((refs[w], dst, (x, y, 1 - c)))
            out += [(refs[w], dst, (px, py, c)) for px, py in chips]
        return out

    def _gather_plan2(self, refs):
        x, y, c, chips = _place()
        return [(refs[w].at[_slot(px, py, c)],) * 2 + ((x, y, 1 - c),) for w in range(self.n) for px, py in chips]

    def weights_start(self, after):
        lands = [lax.dynamic_update_slice(lax.empty((N_DEV,) + s.shape, s.dtype), s[None], (self.dev, 0, 0))
                 for s in self.shards]
        self.w1 = _split_start(self.tag + "gather_start_1", list(self.shards) + lands, self._gather_plan1, 4 * self.n, after)
        return self.w1[3]

    def weights_mid(self, after):
        send, recv, bufs, _ = self.w1
        lands = _split_wait(self.tag + "gather_wait_1", send, recv, bufs, self._gather_plan1, 4 * self.n, after)[self.n:]
        self.w2 = _split_start(self.tag + "gather_start_2", lands, self._gather_plan2, 3 * self.n)
        return self.w2[3]

    def weights_finish(self, after):
        send, recv, bufs, _ = self.w2
        return _split_wait(self.tag + "gather_wait_2", send, recv, bufs, self._gather_plan2, 3 * self.n, after)

    def _reduce_plan1(self, refs):
        n = self.n
        x, y, c, _ = _place()
        return [(refs[w].at[2 * q + (1 - c)], refs[n + w].at[q], (x, y, 1 - c)) for w in range(n) for q in range(4)]

    def _reduce_plan2(self, refs):
        n = self.n
        x, y, c, chips = _place()
        return [(refs[w].at[2 * px + py], refs[n + w].at[j], (px, py, c))
                for w in range(n) for j, (px, py) in enumerate(chips)]

    def grads_start(self, grads):
        lands = [lax.empty((4,) + g.shape[1:], g.dtype) for g in grads]
        self.g1 = _split_start(self.tag + "reduce_start_1", list(grads) + lands, self._reduce_plan1, 4 * self.n)
        return self.g1[3]

    def grads_mid(self, after):
        n = self.n
        send, recv, bufs, _ = self.g1
        bufs = _split_wait(self.tag + "reduce_wait_1", send, recv, bufs, self._reduce_plan1, 4 * n, after)
        core = self.core.reshape(1).astype(jnp.int32)
        self.parts = [_chip_sum(bufs[w], bufs[n + w], core, self.tag + "reduce_chip_sum_%d" % w) for w in range(n)]
        lands = [lax.empty((3,) + p.shape[1:], p.dtype) for p in self.parts]
        self.g2 = _split_start(self.tag + "reduce_start_2", self.parts + lands, self._reduce_plan2, 3 * n)
        return self.g2[3]

    def grads_finish(self, after):
        n = self.n
        send, recv, bufs, _ = self.g2
        bufs = _split_wait(self.tag + "reduce_wait_2", send, recv, bufs, self._reduce_plan2, 3 * n, after)
        self.parts, self.terms = bufs[:n], bufs[n:]


def _all_reduce_small(buf):
    R = buf.shape[0]

    def body(x_ref, o_ref, g_ref, send_sems, recv_sems):
        x, y, c, chips = _place()
        me, sibling = (x, y, c), (x, y, 1 - c)

        def slot(px, py, pc):
            return 4 * px + 2 * py + pc

        def copy(k, block, to, src=None):
            dst = g_ref.at[slot(*block)]
            return pltpu.make_async_remote_copy(src_ref=dst if src is None else src, dst_ref=dst,
                                                send_sem=send_sems.at[k], recv_sem=recv_sems.at[k],
                                                device_id=to, device_id_type=MESH)

        first = [copy(0, me, sibling, src=x_ref)]
        first += [copy(1 + j, me, (*chip, c), src=x_ref) for j, chip in enumerate(chips)]
        for cp in first:
            cp.start()
        g_ref[slot(*me)] = x_ref[...]
        passed = [copy(4 + j, (*chip, c), sibling) for j, chip in enumerate(chips)]
        for j, chip in enumerate(chips):
            copy(1 + j, (*chip, c), me).wait_recv()
            passed[j].start()
        copy(0, sibling, me).wait_recv()
        for j, chip in enumerate(chips):
            copy(4 + j, (*chip, 1 - c), me).wait_recv()
        for cp in first + passed:
            cp.wait_send()
        acc = g_ref[0]
        for s in range(1, N_DEV):
            acc = acc + g_ref[s]
        o_ref[...] = acc

    vm = pl.BlockSpec(memory_space=pltpu.VMEM)
    return pl.pallas_call(
        body, in_specs=[vm], out_specs=vm, out_shape=jax.ShapeDtypeStruct((R, LANE), F32),
        scratch_shapes=[pltpu.VMEM((N_DEV, R, LANE), F32), pltpu.SemaphoreType.DMA((7,)), pltpu.SemaphoreType.DMA((7,))],
        name="all_reduce_small")(buf)


def _pad_cols(a, width):
    return jnp.pad(a, ((0, 0), (0, width - a.shape[1])))


def _local_step(x, target, w_in_t, conv_w, a_log, dt_bias, delta_norm_w, sinks, rel_bias, ln1_g, ln1_b, ln2_g, ln2_b,
                ex, ex_in):
    S, D = x.shape
    aw = D // 2
    hq, hkv, nh = aw // HEAD_A, aw // HEAD_A // GQA, aw // HEAD_D
    kvw = hkv * HEAD_A
    c_q, c_k, c_v, c_d = 0, aw, aw + kvw, aw + 2 * kvw
    c_ab = c_d + 3 * aw
    c_z = c_ab + 2 * nh
    n_in = c_z + aw
    assert w_in_t.shape == (n_in, D), (w_in_t.shape, n_in)
    w_pt = jnp.concatenate([w_in_t[:c_ab], w_in_t[c_z:], jnp.pad(w_in_t[c_ab:c_z], ((0, LANE - 2 * nh), (0, 0)))], axis=0)
    p_z, p_ab = c_ab, c_ab + aw
    n_p = p_ab + LANE

    proj = _matmul(x, w_pt, NT, name="proj", tn=1152, deps=(ex.weights_start(w_pt),))
    bias = _attn_bias(rel_bias.T)
    attn_out = _attn_fwd(proj, bias, sinks.reshape(-1), hq, 0, c_k // kvw, c_v // kvw, D)
    conv2 = conv_w.reshape(CONV_W, 3 * aw)
    qkv = _gdn_prep_fwd(proj, conv2, nh, c_d // HEAD_D)
    ab = proj[:, p_ab:]
    al, dt = _pad_cols(a_log, LANE), _pad_cols(dt_bias, LANE)
    gb = _gates_fwd(ab, al, dt, nh)
    u_d, wqk, attn_d, t_d = _gdn_local_fwd(qkv, gb, nh)
    o_d, vn, states = _gdn_scan_fwd(u_d, wqk, attn_d, gb, nh, ex.weights_mid(u_d))
    mix = _gated_norm_fwd(o_d, proj, delta_norm_w, nh, p_z // HEAD_D, attn_out, aw // HEAD_D)
    w_o_g, w_up_g, w_down_g = ex.weights_finish(mix)
    w_o, w_down = w_o_g.reshape(D, D), w_down_g.reshape(-1, D)
    mixed = _matmul(mix, w_o, NN, name="out_proj")
    h1 = _ln1_fwd(x, mixed, ln1_g, ln1_b)
    u = _matmul(h1, w_up_g, NN, name="mlp_up", b_groups=True)
    mlp = _matmul(u, w_down, NN, name="mlp_down", a_fn=_relu_sq)
    dr2, loss_row, dln2_g, dln2_b = _ln2_loss(h1, mlp, ln2_g, ln2_b, target)

    du = _matmul(dr2, w_down, NT, name="d_mlp_act", epi=_relu_sq_grad, epi_in=(u,))
    dw_down = _matmul(u, dr2, TN, name="dw_down", a_fn=_relu_sq, out_dtype=BF16)
    dw_up = _matmul(h1, du, TN, name="dw_up", out_dtype=BF16, out_groups=N_DEV)
    dh_mlp = _matmul(du, w_up_g, NT, name="d_h1", b_groups=True)
    dr1, dln1_g, dln1_b = _ln1_bwd(x, mixed, ln1_g, dr2, dh_mlp)
    dw_o = _matmul(mix, dr1, TN, name="dw_o", out_dtype=BF16)
    tok = ex.grads_start([dw_o.reshape(N_DEV, -1, D), dw_up, dw_down.reshape(N_DEV, -1, D)])
    dmix = _matmul(dr1, w_o, NT, name="d_mix", deps=(tok,))
    dq_a, dk_a, dv_a, dbias, dsink = _attn_bwd(proj, bias, sinks.reshape(-1), mix, dmix, hq, 0, c_k // kvw, c_v // kvw)
    drel = _rel_bias_grad(dbias, hq)
    do_d, dz, dnw = _gated_norm_bwd(o_d, proj, delta_norm_w, dmix, nh, p_z // HEAD_D, aw // HEAD_D)
    dvn_s, dw_s, dqd, dkd, dattn_d, dgl = _gdn_scan_bwd(wqk, attn_d, gb, states, vn, do_d, nh, ex.grads_mid(dq_a))
    dqkv_n, dg, dbeta = _gdn_local_bwd(qkv, gb, t_d, u_d, wqk, dvn_s, dw_s, dqd, dkd, dattn_d, dgl, nh)
    dgb = _pad_cols(jnp.concatenate([dg.reshape(nh, S).T, dbeta.reshape(nh, S).T], axis=1), LANE)
    dab, da_log, ddt_bias = _gates_bwd(ab, al, dt, dgb, nh)
    dqkv_d, dconv = _gdn_prep_bwd(proj, conv2, dqkv_n, nh, c_d // HEAD_D)
    dproj = jnp.concatenate([dq_a, dk_a, dv_a, dqkv_d, dz, dab], axis=1)
    dw_pt = _matmul(dproj, x, TN, name="dw_in", out_dtype=BF16, tm=1152)
    dw_in_t = jnp.concatenate([dw_pt[:p_z], dw_pt[p_ab:p_ab + 2 * nh], dw_pt[p_z:p_ab]], axis=0)
    tok = ex_in.grads_start([dw_in_t.reshape(N_DEV, -1, D)])
    dx_proj = _matmul(dproj, w_pt, NN, name="d_x", tk=1920, deps=(tok,))
    grad_x = _grad_x(dr1, dx_proj, ex_in.grads_mid(dx_proj))
    ex.grads_finish(grad_x)

    small = dict(conv_w=dconv, a_log=da_log[:, :nh], dt_bias=ddt_bias[:, :nh], delta_norm_w=dnw,
                 attn_sinks=dsink[:, 0].reshape(1, hq), rel_bias=drel[:, :, 0].T,
                 ln1_g=dln1_g, ln1_b=dln1_b, ln2_g=dln2_g, ln2_b=dln2_b)
    return loss_row, grad_x, small


SMALL_ORDER = ("conv_w", "a_log", "dt_bias", "delta_norm_w", "attn_sinks", "rel_bias", "ln1_g", "ln1_b", "ln2_g", "ln2_b")


def _pack_small(loss_row, small):
    parts = [loss_row.reshape(-1)]
    for k in SMALL_ORDER:
        flat = small[k].reshape(-1)
        parts.append(jnp.pad(flat, (0, (-flat.shape[0]) % LANE)))
    flat = jnp.concatenate(parts)
    flat = jnp.pad(flat, (0, (-flat.shape[0]) % (8 * LANE)))
    return flat.reshape(-1, LANE)


def _unpack_small(buf, small_shapes):
    flat = buf.reshape(-1)
    loss = flat[0]
    off = LANE
    out = {}
    for k in SMALL_ORDER:
        n = int(np.prod(small_shapes[k]))
        out[k] = flat[off:off + n].reshape(small_shapes[k])
        off += n + (-n) % LANE
    return loss, out


def kernel(x, w_in, conv_w, a_log, dt_bias, delta_norm_w, attn_sinks, rel_bias, w_o, ln1_g, ln1_b, w_up, w_down, ln2_g, ln2_b, loss_target, m_w_in, m_conv_w, m_a_log, m_dt_bias, m_delta_norm_w, m_attn_sinks, m_rel_bias, m_w_o, m_ln1_g, m_ln1_b, m_w_up, m_w_down, m_ln2_g, m_ln2_b, v_w_in, v_conv_w, v_a_log, v_dt_bias, v_delta_norm_w, v_attn_sinks, v_rel_bias, v_w_o, v_ln1_g, v_ln1_b, v_w_up, v_w_down, v_ln2_g, v_ln2_b):
    S, D = x.shape[1], x.shape[2]
    core = lax.axis_index("c")
    dev = 4 * lax.axis_index("x") + 2 * lax.axis_index("y") + core

    (w_in_g,) = _all_gather([w_in[0].T.astype(BF16)])
    w_in_t = w_in_g.reshape(-1, D)
    ex = _Exchanges("", [w_o[0].astype(BF16), w_up[0].astype(BF16), w_down[0].astype(BF16)], dev, core)
    ex_in = _Exchanges("in_", [w_in_g[0]], dev, core)

    cw_sh = conv_w.shape[3]
    conv_place = lax.dynamic_update_slice(jnp.zeros((CONV_W, N_DEV * cw_sh), F32), conv_w[0, :, 0, :], (0, dev * cw_sh))
    conv_full = _all_reduce_small(jnp.pad(conv_place.reshape(-1, LANE), ((0, (-conv_place.size // LANE) % 8), (0, 0))))
    conv_full = conv_full[:conv_place.size // LANE].reshape(CONV_W, N_DEV * cw_sh)

    loss_row, grad_x, small = _local_step(
        x[0], loss_target[0], w_in_t, conv_full, a_log, dt_bias, delta_norm_w, attn_sinks, rel_bias,
        ln1_g, ln1_b, ln2_g, ln2_b, ex, ex_in)

    chip_arr = (dev // 2).reshape(1).astype(jnp.int32)
    big = {}
    for i, (name, w, m, v) in enumerate((("w_o", w_o, m_w_o, v_w_o), ("w_up", w_up, m_w_up, v_w_up),
                                         ("w_down", w_down, m_w_down, v_w_down))):
        big[name] = [o[None] for o in _adamw_big(ex.parts[i], ex.terms[i], chip_arr, w[0], m[0], v[0], "adamw_" + name)]
    ex_in.grads_finish(big["w_down"][0])
    outs = _adamw_big(ex_in.parts[0], ex_in.terms[0], chip_arr, w_in[0].T, m_w_in[0].T, v_w_in[0].T, "adamw_w_in")
    big["w_in"] = [o.T[None] for o in outs]

    small_shapes = {k: v.shape for k, v in small.items()}
    loss, small = _unpack_small(_all_reduce_small(_pack_small(loss_row, small)), small_shapes)
    small["conv_w"] = lax.dynamic_slice(small["conv_w"], (0, dev * cw_sh), (CONV_W, cw_sh))
    small["rel_bias"] = small["rel_bias"].reshape(rel_bias.shape)
    p2 = dict(conv_w=(conv_w, m_conv_w, v_conv_w), a_log=(a_log, m_a_log, v_a_log), dt_bias=(dt_bias, m_dt_bias, v_dt_bias),
              delta_norm_w=(delta_norm_w, m_delta_norm_w, v_delta_norm_w), attn_sinks=(attn_sinks, m_attn_sinks, v_attn_sinks),
              rel_bias=(rel_bias, m_rel_bias, v_rel_bias), ln1_g=(ln1_g, m_ln1_g, v_ln1_g), ln1_b=(ln1_b, m_ln1_b, v_ln1_b),
              ln2_g=(ln2_g, m_ln2_g, v_ln2_g), ln2_b=(ln2_b, m_ln2_b, v_ln2_b))
    two_d = lambda a: a.reshape(-1, a.shape[-1])
    ws = [two_d(p2[k][0]) for k in SMALL_ORDER]
    gs = [two_d(small[k]) for k in SMALL_ORDER]
    ms = [two_d(p2[k][1]) for k in SMALL_ORDER]
    vs = [two_d(p2[k][2]) for k in SMALL_ORDER]
    ds, nms, nvs = _adamw_small(ws, gs, ms, vs)
    res = {}
    for i, k in enumerate(SMALL_ORDER):
        shp = p2[k][0].shape
        res[k] = [gs[i].reshape(shp), ds[i].reshape(shp), nms[i].reshape(shp), nvs[i].reshape(shp)]
    res.update(big)
    order = ("w_in", "conv_w", "a_log", "dt_bias", "delta_norm_w", "attn_sinks", "rel_bias", "w_o", "ln1_g", "ln1_b",
             "w_up", "w_down", "ln2_g", "ln2_b")
    return (loss, grad_x[None], *[res[k][0] for k in order], *[res[k][1] for k in order],
            *[res[k][2] for k in order], *[res[k][3] for k in order])
```

```python
import functools
import math

import numpy as np
import jax
import jax.numpy as jnp
from jax import lax
from jax.experimental import pallas as pl
from jax.experimental.pallas import tpu as pltpu

F32 = jnp.float32
BF16 = jnp.bfloat16
HIGHEST = lax.Precision.HIGHEST

N_DEV = 8
HEAD_A = 64
GQA = 4
BLK = 128
N_BUCKETS = 32
MAX_DISTANCE = 128
HEAD_D = 128
CONV_W = 4
CHUNK = 64
NEG_INF = -1e30
LN_EPS = 1e-5
RMS_EPS = 1e-6
DN_ALPHA = 2.0 ** 0.25
ADAM_LR, ADAM_B1, ADAM_B2, ADAM_EPS, ADAM_WD, ADAM_STEP = 0.001, 0.9, 0.999, 1e-08, 0.01, 10

LANE = 128
VMEM_LIMIT = 56 * 1024 * 1024

NN = ((1,), (0,))
NT = ((1,), (1,))
TN = ((0,), (0,))


def _dot(a, b, dims, prec=None):
    return lax.dot_general(a, b, (dims, ((), ())), precision=prec, preferred_element_type=F32)


def _tile(dim, pref):
    if dim <= pref:
        return dim
    t = (pref // LANE) * LANE
    while t > LANE and dim % t:
        t -= LANE
    assert dim % t == 0, (dim, pref)
    return t


def _params(sem):
    return pltpu.CompilerParams(dimension_semantics=sem, vmem_limit_bytes=VMEM_LIMIT)


def _matmul(a, b, dims, *, name, out_dtype=F32, tm=1024, tn=1024, tk=2048, a_fn=None, epi=None, epi_in=(),
            b_groups=None, out_groups=None, deps=()):
    (ca,), (cb,) = dims
    M, K = a.shape[1 - ca], a.shape[ca]
    if b_groups:
        G, R, C = b.shape
        bshape = (R, G * C)
    else:
        bshape = b.shape
    N = bshape[1 - cb]
    assert bshape[cb] == K, (a.shape, b.shape, dims)
    tm, tk = _tile(M, tm), _tile(K, tk)
    if b_groups:
        lim = C if cb == 0 else tn
        tn = _tile(N, min(tn, lim))
        if cb == 1:
            tk = _tile(K, min(tk, C))
    else:
        tn = _tile(N, tn)
    if out_groups:
        tn = _tile(N, min(tn, N // out_groups))
    nk = K // tk

    def body(*refs):
        a_ref, b_ref = refs[0], refs[1]
        e_refs = refs[2:2 + len(epi_in)]
        o_ref = refs[2 + len(epi_in) + len(deps)]
        acc_ref = refs[3 + len(epi_in) + len(deps)] if nk > 1 else None
        k = pl.program_id(2)
        av = a_ref[...]
        if a_fn is not None:
            av = a_fn(av)
        prod = _dot(av.astype(BF16), b_ref[...].astype(BF16), dims)

        def finish(r):
            if epi is not None:
                r = epi(r, *[e[...] for e in e_refs])
            o_ref[...] = r.astype(out_dtype)

        if nk == 1:
            finish(prod)
            return

        @pl.when(k == 0)
        def _():
            acc_ref[...] = prod

        @pl.when(k > 0)
        def _():
            acc_ref[...] += prod

        @pl.when(k == nk - 1)
        def _():
            finish(acc_ref[...])

    a_spec = (pl.BlockSpec((tm, tk), lambda i, j, k: (i, k)) if ca == 1
              else pl.BlockSpec((tk, tm), lambda i, j, k: (k, i)))
    if b_groups:
        if cb == 0:
            per = C // tn
            b_spec = pl.BlockSpec((None, tk, tn), lambda i, j, k: (j // per, k, j % per))
        else:
            per = C // tk
            b_spec = pl.BlockSpec((None, tn, tk), lambda i, j, k: (k // per, j, k % per))
    else:
        b_spec = (pl.BlockSpec((tk, tn), lambda i, j, k: (k, j)) if cb == 0
                  else pl.BlockSpec((tn, tk), lambda i, j, k: (j, k)))
    e_specs = [pl.BlockSpec((tm, tn), lambda i, j, k: (i, j)) for _ in epi_in]
    if out_groups:
        per_o = (N // out_groups) // tn
        o_spec = pl.BlockSpec((None, tm, tn), lambda i, j, k: (j // per_o, i, j % per_o))
        o_shape = jax.ShapeDtypeStruct((out_groups, M, N // out_groups), out_dtype)
    else:
        o_spec = pl.BlockSpec((tm, tn), lambda i, j, k: (i, j))
        o_shape = jax.ShapeDtypeStruct((M, N), out_dtype)
    return pl.pallas_call(
        body, grid=(M // tm, N // tn, nk), out_specs=o_spec,
        in_specs=[a_spec, b_spec] + e_specs + [pl.BlockSpec(memory_space=pl.ANY)] * len(deps),
        out_shape=o_shape, scratch_shapes=[pltpu.VMEM((tm, tn), F32)] if nk > 1 else [],
        compiler_params=_params(("parallel", "parallel", "arbitrary")), name=name)(a, b, *epi_in, *deps)


def _relu_sq(u):
    r = jnp.maximum(u, 0.0)
    return r * r


def _relu_sq_grad(acc, u):
    return acc * (2.0 * jnp.maximum(u, 0.0))


def _ln_stats(r):
    mu = jnp.mean(r, axis=-1, keepdims=True)
    xc = r - mu
    var = jnp.mean(xc * xc, axis=-1, keepdims=True)
    rstd = lax.rsqrt(var + LN_EPS)
    return xc * rstd, rstd


def _ln_bwd(dy, xhat, rstd, g):
    dxh = dy * g
    m1 = jnp.mean(dxh, axis=-1, keepdims=True)
    m2 = jnp.mean(dxh * xhat, axis=-1, keepdims=True)
    return rstd * (dxh - m1 - xhat * m2)


def _row_call(body, ins, row_ins, outs, acc_outs, name, tr=256):
    S = ins[0].shape[0]
    tr = min(tr, S)
    n_in, n_row, n_out = len(ins), len(row_ins), len(outs)

    def wrapped(*refs):
        i = pl.program_id(0)
        acc_refs = refs[n_in + n_row + n_out:]

        @pl.when(i == 0)
        def _():
            for r in acc_refs:
                r[...] = jnp.zeros_like(r)

        body(*refs)

    in_specs = [pl.BlockSpec((tr, a.shape[1]), lambda i: (i, 0)) for a in ins]
    in_specs += [pl.BlockSpec(a.shape, lambda i: (0, 0)) for a in row_ins]
    out_specs = [pl.BlockSpec((tr, s.shape[1]), lambda i: (i, 0)) for s in outs]
    out_specs += [pl.BlockSpec(s.shape, lambda i: (0, 0)) for s in acc_outs]
    return pl.pallas_call(wrapped, grid=(S // tr,), in_specs=in_specs, out_specs=out_specs,
                          out_shape=list(outs) + list(acc_outs),
                          compiler_params=_params(("arbitrary",)), name=name)(*ins, *row_ins)


def _ln1_fwd(x, mixed, g, b):
    def body(x_ref, m_ref, g_ref, b_ref, h_ref):
        xhat, _ = _ln_stats(DN_ALPHA * x_ref[...] + m_ref[...])
        h_ref[...] = xhat * g_ref[...] + b_ref[...]
    return _row_call(body, [x, mixed], [g, b], [jax.ShapeDtypeStruct(x.shape, F32)], [], "ln1_fwd")[0]


def _ln2_loss(h1, mlp, g, b, target):
    S, D = h1.shape
    sds = jax.ShapeDtypeStruct

    def body(h_ref, m_ref, t_ref, g_ref, b_ref, dr_ref, loss_ref, dg_ref, db_ref):
        xhat, rstd = _ln_stats(DN_ALPHA * h_ref[...] + m_ref[...])
        gv = g_ref[...]
        err = xhat * gv + b_ref[...] - t_ref[...]
        loss_ref[...] += jnp.sum(jnp.sum(err * err, axis=0, keepdims=True), axis=1, keepdims=True) * (0.5 / D)
        dy = err * (1.0 / D)
        dg_ref[...] += jnp.sum(dy * xhat, axis=0, keepdims=True)
        db_ref[...] += jnp.sum(dy, axis=0, keepdims=True)
        dr_ref[...] = _ln_bwd(dy, xhat, rstd, gv)

    return _row_call(body, [h1, mlp, target], [g, b], [sds((S, D), F32)],
                     [sds((1, LANE), F32), sds((1, D), F32), sds((1, D), F32)], "ln2_loss")


def _ln1_bwd(x, mixed, g, dr2, dh_mlp):
    S, D = x.shape
    sds = jax.ShapeDtypeStruct

    def body(x_ref, m_ref, dr2_ref, dh_ref, g_ref, dr_ref, dg_ref, db_ref):
        xhat, rstd = _ln_stats(DN_ALPHA * x_ref[...] + m_ref[...])
        dy = DN_ALPHA * dr2_ref[...] + dh_ref[...]
        dg_ref[...] += jnp.sum(dy * xhat, axis=0, keepdims=True)
        db_ref[...] += jnp.sum(dy, axis=0, keepdims=True)
        dr_ref[...] = _ln_bwd(dy, xhat, rstd, g_ref[...])

    return _row_call(body, [x, mixed, dr2, dh_mlp], [g], [sds((S, D), F32)],
                     [sds((1, D), F32), sds((1, D), F32)], "ln1_bwd")


def _grad_x(dr1, dx_proj, dep):
    def body(a_ref, b_ref, dep_ref, o_ref):
        o_ref[...] = DN_ALPHA * a_ref[...] + b_ref[...]
    return _row_call(body, [dr1, dx_proj], [dep], [jax.ShapeDtypeStruct(dr1.shape, F32)], [], "grad_x")[0]


def _bucket_table():
    qi = np.arange(BLK, dtype=np.int32)[:, None]
    kj = np.arange(2 * BLK, dtype=np.int32)[None, :]
    dist = qi + BLK - kj
    n = np.maximum(dist, 0)
    max_exact = N_BUCKETS // 2
    nf = np.maximum(n, 1).astype(np.float32)
    large = max_exact + (np.log(nf / np.float32(max_exact)) / np.float32(math.log(MAX_DISTANCE / max_exact))
                         * np.float32(N_BUCKETS - max_exact)).astype(np.int32)
    large = np.minimum(large, N_BUCKETS - 1)
    bucket = np.where(n < max_exact, n, large)
    return np.where((dist >= 0) & (dist < BLK), bucket, -1).astype(np.int32)


def _attn_bias(rel_bias_t):
    hq = rel_bias_t.shape[0]
    bucket = jnp.asarray(_bucket_table())

    def body(rb_ref, bk_ref, o_ref):
        h = pl.program_id(0)
        bk = bk_ref[...]
        acc = jnp.zeros((BLK, 2 * BLK), F32)
        for b in range(N_BUCKETS):
            acc = jnp.where(bk == b, rb_ref[h, b], acc)
        o_ref[...] = acc

    return pl.pallas_call(
        body, grid=(hq,),
        in_specs=[pl.BlockSpec(memory_space=pltpu.SMEM), pl.BlockSpec((BLK, 2 * BLK), lambda h: (0, 0))],
        out_specs=pl.BlockSpec((BLK, 2 * BLK), lambda h: (h, 0)),
        out_shape=jax.ShapeDtypeStruct((hq * BLK, 2 * BLK), F32),
        compiler_params=_params(("arbitrary",)), name="attn_bias")(rel_bias_t, bucket)


def _attn_probs(sc, sp, bias, sink, mask_c, mask_p):
    lc = jnp.where(mask_c, sc + bias[:, BLK:], NEG_INF)
    lp = jnp.where(mask_p, sp + bias[:, :BLK], NEG_INF)
    m = jnp.maximum(jnp.maximum(jnp.max(lc, axis=1, keepdims=True), jnp.max(lp, axis=1, keepdims=True)), sink)
    pc, pp, ps = jnp.exp(lc - m), jnp.exp(lp - m), jnp.exp(sink - m)
    inv = 1.0 / (jnp.sum(pc, axis=1, keepdims=True) + jnp.sum(pp, axis=1, keepdims=True) + ps)
    return pc, pp, ps, inv


def _attn_masks(n):
    qi = lax.broadcasted_iota(jnp.int32, (BLK, BLK), 0)
    kj = lax.broadcasted_iota(jnp.int32, (BLK, BLK), 1)
    return kj <= qi, (kj > qi) & (n > 0)


def _attn_fwd(proj, bias, sinks, hq, q_blk, k_blk, v_blk, out_width):
    S = proj.shape[0]
    hkv = hq // GQA
    wq, wk = hq * HEAD_A, hkv * HEAD_A

    def body(q_ref, k_ref, v_ref, bias_ref, sink_ref, o_ref):
        n = pl.program_id(0)
        cur = pl.multiple_of(n * BLK, BLK)
        prev = pl.multiple_of(jnp.maximum(n - 1, 0) * BLK, BLK)
        mask_c, mask_p = _attn_masks(n)
        for h4 in range(hkv):
            cs = slice(h4 * HEAD_A, (h4 + 1) * HEAD_A)
            kc, kp = k_ref[pl.ds(cur, BLK), cs].astype(BF16), k_ref[pl.ds(prev, BLK), cs].astype(BF16)
            vc, vp = v_ref[pl.ds(cur, BLK), cs].astype(BF16), v_ref[pl.ds(prev, BLK), cs].astype(BF16)
            hs_of = [slice(h * HEAD_A, (h + 1) * HEAD_A) for h in range(h4 * GQA, (h4 + 1) * GQA)]
            qs = [(q_ref[:, hs] * (HEAD_A ** -0.5)).astype(BF16) for hs in hs_of]
            scs = [_dot(q, kc, NT) for q in qs]
            sps = [_dot(q, kp, NT) for q in qs]
            pr = [_attn_probs(scs[g], sps[g], bias_ref[(h4 * GQA + g) * BLK:(h4 * GQA + g + 1) * BLK, :],
                              sink_ref[h4 * GQA + g], mask_c, mask_p) for g in range(GQA)]
            oc = [_dot(p[0].astype(BF16), vc, NN) for p in pr]
            op = [_dot(p[1].astype(BF16), vp, NN) for p in pr]
            for g, hs in enumerate(hs_of):
                o_ref[:, hs] = (oc[g] + op[g]) * pr[g][3]

    return pl.pallas_call(
        body, grid=(S // BLK,),
        in_specs=[pl.BlockSpec((BLK, wq), lambda n: (n, q_blk)), pl.BlockSpec((S, wk), lambda n: (0, k_blk)),
                  pl.BlockSpec((S, wk), lambda n: (0, v_blk)), pl.BlockSpec((hq * BLK, 2 * BLK), lambda n: (0, 0)),
                  pl.BlockSpec(memory_space=pltpu.SMEM)],
        out_specs=pl.BlockSpec((BLK, wq), lambda n: (n, 0)),
        out_shape=jax.ShapeDtypeStruct((S, out_width), F32),
        compiler_params=_params(("arbitrary",)), name="attn_fwd")(proj, proj, proj, bias, sinks)


def _attn_bwd(proj, bias, sinks, out, dmix, hq, q_blk, k_blk, v_blk):
    S = proj.shape[0]
    hkv = hq // GQA
    wq, wk = hq * HEAD_A, hkv * HEAD_A
    sds = jax.ShapeDtypeStruct

    def body(q_ref, k_ref, v_ref, bias_ref, sink_ref, o_ref, do_ref, dq_ref, dk_ref, dv_ref, dbias_ref, dsink_ref):
        n = pl.program_id(0)

        @pl.when(n == 0)
        def _():
            dk_ref[...] = jnp.zeros_like(dk_ref)
            dv_ref[...] = jnp.zeros_like(dv_ref)
            dbias_ref[...] = jnp.zeros_like(dbias_ref)
            dsink_ref[...] = jnp.zeros_like(dsink_ref)

        cur = pl.multiple_of(n * BLK, BLK)
        prev = pl.multiple_of(jnp.maximum(n - 1, 0) * BLK, BLK)
        mask_c, mask_p = _attn_masks(n)
        for h4 in range(hkv):
            cs = slice(h4 * HEAD_A, (h4 + 1) * HEAD_A)
            kc, kp = k_ref[pl.ds(cur, BLK), cs].astype(BF16), k_ref[pl.ds(prev, BLK), cs].astype(BF16)
            vc, vp = v_ref[pl.ds(cur, BLK), cs].astype(BF16), v_ref[pl.ds(prev, BLK), cs].astype(BF16)
            heads = list(range(h4 * GQA, (h4 + 1) * GQA))
            hs_of = [slice(h * HEAD_A, (h + 1) * HEAD_A) for h in heads]
            rows_of = [slice(h * BLK, (h + 1) * BLK) for h in heads]
            G = range(GQA)
            qs = [(q_ref[:, hs] * (HEAD_A ** -0.5)).astype(BF16) for hs in hs_of]
            dos = [do_ref[:, hs] for hs in hs_of]
            dobs = [d.astype(BF16) for d in dos]
            scs = [_dot(q, kc, NT) for q in qs]
            sps = [_dot(q, kp, NT) for q in qs]
            dpc = [_dot(d, vc, NT) for d in dobs]
            dpp = [_dot(d, vp, NT) for d in dobs]
            pcs, pps, dscs, dsps = [], [], [], []
            for g in G:
                pc, pp, ps, inv = _attn_probs(scs[g], sps[g], bias_ref[rows_of[g], :], sink_ref[heads[g]], mask_c, mask_p)
                pc, pp, ps = pc * inv, pp * inv, ps * inv
                delta = jnp.sum(dos[g] * o_ref[:, hs_of[g]], axis=1, keepdims=True)
                dsc, dsp = pc * (dpc[g] - delta), pp * (dpp[g] - delta)
                dsink_ref[heads[g]:heads[g] + 1, :] += jnp.broadcast_to(jnp.sum(-ps * delta, axis=0, keepdims=True), (1, LANE))
                dbias_ref[rows_of[g], BLK:] += dsc
                dbias_ref[rows_of[g], :BLK] += dsp
                pcs.append(pc.astype(BF16))
                pps.append(pp.astype(BF16))
                dscs.append(dsc.astype(BF16))
                dsps.append(dsp.astype(BF16))
            dq1 = [_dot(dscs[g], kc, NN) for g in G]
            dq2 = [_dot(dsps[g], kp, NN) for g in G]
            dkc = [_dot(dscs[g], qs[g], TN) for g in G]
            dkp = [_dot(dsps[g], qs[g], TN) for g in G]
            dvc = [_dot(pcs[g], dobs[g], TN) for g in G]
            dvp = [_dot(pps[g], dobs[g], TN) for g in G]
            for g in G:
                dq_ref[:, hs_of[g]] = (dq1[g] + dq2[g]) * (HEAD_A ** -0.5)
            dk_ref[pl.ds(cur, BLK), cs] += sum(dkc[1:], dkc[0])
            dk_ref[pl.ds(prev, BLK), cs] += sum(dkp[1:], dkp[0])
            dv_ref[pl.ds(cur, BLK), cs] += sum(dvc[1:], dvc[0])
            dv_ref[pl.ds(prev, BLK), cs] += sum(dvp[1:], dvp[0])

    return pl.pallas_call(
        body, grid=(S // BLK,),
        in_specs=[pl.BlockSpec((BLK, wq), lambda n: (n, q_blk)), pl.BlockSpec((S, wk), lambda n: (0, k_blk)),
                  pl.BlockSpec((S, wk), lambda n: (0, v_blk)), pl.BlockSpec((hq * BLK, 2 * BLK), lambda n: (0, 0)),
                  pl.BlockSpec(memory_space=pltpu.SMEM),
                  pl.BlockSpec((BLK, wq), lambda n: (n, 0)), pl.BlockSpec((BLK, wq), lambda n: (n, 0))],
        out_specs=[pl.BlockSpec((BLK, wq), lambda n: (n, 0)), pl.BlockSpec((S, wk), lambda n: (0, 0)),
                   pl.BlockSpec((S, wk), lambda n: (0, 0)), pl.BlockSpec((hq * BLK, 2 * BLK), lambda n: (0, 0)),
                   pl.BlockSpec((hq, LANE), lambda n: (0, 0))],
        out_shape=[sds((S, wq), F32), sds((S, wk), F32), sds((S, wk), F32), sds((hq * BLK, 2 * BLK), F32),
                   sds((hq, LANE), F32)],
        compiler_params=_params(("arbitrary",)), name="attn_bwd")(proj, proj, proj, bias, sinks, out, dmix)


def _rel_bias_grad(dbias, hq):
    bucket = jnp.asarray(_bucket_table())

    def body(d_ref, bk_ref, o_ref):
        d = d_ref[...]
        bk = bk_ref[...]
        rows = [jnp.sum(jnp.where(bk == b, d, 0.0), axis=0, keepdims=True) for b in range(N_BUCKETS)]
        tot = jnp.sum(jnp.concatenate(rows, axis=0), axis=1, keepdims=True)
        o_ref[...] = jnp.broadcast_to(tot, (N_BUCKETS, LANE))

    return pl.pallas_call(
        body, grid=(hq,),
        in_specs=[pl.BlockSpec((BLK, 2 * BLK), lambda h: (h, 0)), pl.BlockSpec((BLK, 2 * BLK), lambda h: (0, 0))],
        out_specs=pl.BlockSpec((None, N_BUCKETS, LANE), lambda h: (h, 0, 0)),
        out_shape=jax.ShapeDtypeStruct((hq, N_BUCKETS, LANE), F32),
        compiler_params=_params(("arbitrary",)), name="rel_bias_grad")(dbias, bucket)


def _sigmoid(x):
    return 1.0 / (1.0 + jnp.exp(-x))


def _shift_rows(x, s):
    n = x.shape[0]
    row = lax.broadcasted_iota(jnp.int32, x.shape, 0)
    if s > 0:
        return jnp.where(row >= s, pltpu.roll(x, s, 0), 0.0)
    return jnp.where(row < n + s, pltpu.roll(x, n + s, 0), 0.0)


def _conv_silu_norm(xv, w, j, nh):
    c = w[CONV_W - 1:CONV_W, :] * xv
    for s in range(1, CONV_W):
        c = c + w[CONV_W - 1 - s:CONV_W - s, :] * _shift_rows(xv, s)
    sg = _sigmoid(c)
    a = c * sg
    r = lax.rsqrt(jnp.sum(a * a, axis=1, keepdims=True) + RMS_EPS)
    scale = jnp.where(j < nh, HEAD_D ** -0.5, 1.0)
    is_norm = j < 2 * nh
    y = jnp.where(is_norm, a * (r * scale), a)
    return c, sg, a, r, scale, is_norm, y


def _gdn_prep_fwd(proj, conv_w, nh, blk0):
    S = proj.shape[0]

    def body(x_ref, w_ref, o_ref):
        j = pl.program_id(0)
        o_ref[...] = _conv_silu_norm(x_ref[...], w_ref[...], j, nh)[-1]

    return pl.pallas_call(
        body, grid=(3 * nh,),
        in_specs=[pl.BlockSpec((S, HEAD_D), lambda j: (0, blk0 + j)), pl.BlockSpec((CONV_W, HEAD_D), lambda j: (0, j))],
        out_specs=pl.BlockSpec((S, HEAD_D), lambda j: (0, 3 * (j % nh) + j // nh)),
        out_shape=jax.ShapeDtypeStruct((S, 3 * nh * HEAD_D), F32),
        compiler_params=_params(("parallel",)), name="gdn_prep_fwd")(proj, conv_w)


def _gdn_prep_bwd(proj, conv_w, dqkv, nh, blk0):
    S = proj.shape[0]
    sds = jax.ShapeDtypeStruct

    def body(x_ref, w_ref, dy_ref, dx_ref, dw_ref):
        j = pl.program_id(0)
        xv, w = x_ref[...], w_ref[...]
        c, sg, a, r, scale, is_norm, _ = _conv_silu_norm(xv, w, j, nh)
        dy = dy_ref[...]
        rs = r * scale
        da_n = rs * dy - a * (r * r * rs) * jnp.sum(dy * a, axis=1, keepdims=True)
        da = jnp.where(is_norm, da_n, dy)
        dc = da * (sg * (1.0 + c * (1.0 - sg)))
        dx = w[CONV_W - 1:CONV_W, :] * dc
        dws = [jnp.sum(dc * xv, axis=0, keepdims=True)]
        for s in range(1, CONV_W):
            dx = dx + w[CONV_W - 1 - s:CONV_W - s, :] * _shift_rows(dc, -s)
            dws.insert(0, jnp.sum(dc * _shift_rows(xv, s), axis=0, keepdims=True))
        dx_ref[...] = dx
        dw_ref[...] = jnp.concatenate(dws, axis=0)

    return pl.pallas_call(
        body, grid=(3 * nh,),
        in_specs=[pl.BlockSpec((S, HEAD_D), lambda j: (0, blk0 + j)), pl.BlockSpec((CONV_W, HEAD_D), lambda j: (0, j)),
                  pl.BlockSpec((S, HEAD_D), lambda j: (0, 3 * (j % nh) + j // nh))],
        out_specs=[pl.BlockSpec((S, HEAD_D), lambda j: (0, j)), pl.BlockSpec((CONV_W, HEAD_D), lambda j: (0, j))],
        out_shape=[sds((S, 3 * nh * HEAD_D), F32), sds((CONV_W, 3 * nh * HEAD_D), F32)],
        compiler_params=_params(("parallel",)), name="gdn_prep_bwd")(proj, conv_w, dqkv)


def _softplus(x):
    return jnp.maximum(x, 0.0) + jnp.log(1.0 + jnp.exp(-jnp.abs(x)))


def _gates_fwd(ab, al, dt, nh):
    S = ab.shape[0]

    def body(ab_ref, al_ref, dt_ref, o_ref):
        v = ab_ref[...]
        lane = lax.broadcasted_iota(jnp.int32, v.shape, 1)
        g = -jnp.exp(al_ref[...]) * _softplus(v + dt_ref[...])
        o_ref[...] = jnp.where(lane < nh, g, jnp.where(lane < 2 * nh, _sigmoid(v), 0.0))

    row = pl.BlockSpec((1, LANE), lambda i: (0, 0))
    full = pl.BlockSpec((S, LANE), lambda i: (0, 0))
    return pl.pallas_call(body, grid=(1,), in_specs=[full, row, row], out_specs=full,
                          out_shape=jax.ShapeDtypeStruct((S, LANE), F32),
                          compiler_params=_params(("arbitrary",)), name="gates_fwd")(ab, al, dt)


def _gates_bwd(ab, al, dt, dgb, nh):
    S = ab.shape[0]
    sds = jax.ShapeDtypeStruct

    def body(ab_ref, al_ref, dt_ref, d_ref, dab_ref, dal_ref, ddt_ref):
        v, d = ab_ref[...], d_ref[...]
        lane = lax.broadcasted_iota(jnp.int32, v.shape, 1)
        is_a = lane < nh
        z = v + dt_ref[...]
        dsp = jnp.where(is_a, d * (-jnp.exp(al_ref[...])), 0.0)
        dz = dsp * _sigmoid(z)
        beta = _sigmoid(v)
        dab_ref[...] = jnp.where(is_a, dz, jnp.where(lane < 2 * nh, d * beta * (1.0 - beta), 0.0))
        dal_ref[...] = jnp.sum(dsp * _softplus(z), axis=0, keepdims=True)
        ddt_ref[...] = jnp.sum(dz, axis=0, keepdims=True)

    row = pl.BlockSpec((1, LANE), lambda i: (0, 0))
    full = pl.BlockSpec((S, LANE), lambda i: (0, 0))
    return pl.pallas_call(body, grid=(1,), in_specs=[full, row, row, full], out_specs=[full, row, row],
                          out_shape=[sds((S, LANE), F32), sds((1, LANE), F32), sds((1, LANE), F32)],
                          compiler_params=_params(("arbitrary",)), name="gates_bwd")(ab, al, dt, dgb)


def _col_of(tile, h):
    lane = lax.broadcasted_iota(jnp.int32, tile.shape, 1)
    return jnp.sum(jnp.where(lane == h, tile, 0.0), axis=1, keepdims=True)


def _to_row(col, eye):
    return jnp.sum(jnp.where(eye, col, 0.0), axis=0, keepdims=True)


def _to_col(row, eye):
    return jnp.sum(jnp.where(eye, row, 0.0), axis=1, keepdims=True)


def _split(a):
    hi = a.astype(BF16)
    return hi, (a - hi.astype(F32)).astype(BF16)


def _gdot(a, b, dims):
    ah, al = _split(a)
    bh, bl = _split(b)
    return _dot(ah, bh, dims) + (_dot(ah, bl, dims) + _dot(al, bh, dims))


def _chunks_local(qs, ks, vs, gcols, bcols, Ts=None):
    C = CHUNK
    row = lax.broadcasted_iota(jnp.int32, (C, C), 0)
    col = lax.broadcasted_iota(jnp.int32, (C, C), 1)
    tril, strict, eye = col <= row, col < row, col == row
    outs = []
    for k, gcol, bcol in zip(ks, gcols, bcols):
        grow = _to_row(gcol, eye)
        G_row = jnp.sum(jnp.where(row <= col, gcol, 0.0), axis=0, keepdims=True)
        G_col = jnp.sum(jnp.where(tril, grow, 0.0), axis=1, keepdims=True)
        G_last = G_col[C - 1:C, :]
        outs.append(dict(strict=strict, eye=eye, row=row, col=col, decay=jnp.exp(jnp.where(tril, G_col - G_row, NEG_INF)),
                         eG=jnp.exp(G_col), eGr=jnp.exp(G_last - G_col), gl=jnp.exp(G_last), kb=k * bcol))
    Ms = [_gdot(o["kb"], k, NT) for o, k in zip(outs, ks)]
    Ns = [_gdot(q, k, NT) for q, k in zip(qs, ks)]
    for o, q, k, M, N in zip(outs, qs, ks, Ms, Ns):
        o.update(A=jnp.where(strict, M * o["decay"], 0.0), attn=N * o["decay"], rhs_k=o["kb"] * o["eG"],
                 q_dec=q * o["eG"], k_dec=k * o["eGr"])
    if Ts is None:
        Ts = [jnp.where(eye, 1.0, 0.0) - o["A"] for o in outs]
        Ps = [o["A"] for o in outs]
        for _ in range(int(math.log2(C)) - 1):
            Ps = [_gdot(P, P, NN) for P in Ps]
            Ts = [T + _gdot(T, P, NN) for T, P in zip(Ts, Ps)]
        us = [_gdot(T, v * bcol, NN) for T, v, bcol in zip(Ts, vs, bcols)]
        ws = [_gdot(T, o["rhs_k"], NN) for T, o in zip(Ts, outs)]
        for o, T, u, w in zip(outs, Ts, us, ws):
            o.update(T=T, u=u, w=w)
    return outs


GDN_ROWS = 256
GDN_LOCAL_ROWS = 512
WQK = 3 * CHUNK


def _gdn_local_fwd(qkv, gb, nh):
    S = qkv.shape[0]
    nc = S // CHUNK
    rb = min(GDN_LOCAL_ROWS, S)
    cpb = rb // CHUNK
    sds = jax.ShapeDtypeStruct

    def body(q_ref, k_ref, v_ref, gb_ref, u_ref, wqk_ref, attn_ref, t_ref):
        h = pl.program_id(0)
        rows_of = [slice(ci * CHUNK, (ci + 1) * CHUNK) for ci in range(cpb)]
        gbts = [gb_ref[rows, :] for rows in rows_of]
        Ls = _chunks_local([q_ref[rows, :] for rows in rows_of], [k_ref[rows, :] for rows in rows_of],
                           [v_ref[rows, :] for rows in rows_of], [_col_of(t, h) for t in gbts],
                           [_col_of(t, nh + h) for t in gbts])
        for ci, (rows, L) in enumerate(zip(rows_of, Ls)):
            u_ref[rows, :] = L["u"]
            base = ci * WQK
            wqk_ref[base:base + CHUNK, :] = L["w"]
            wqk_ref[base + CHUNK:base + 2 * CHUNK, :] = L["q_dec"]
            wqk_ref[base + 2 * CHUNK:base + WQK, :] = L["k_dec"]
            attn_ref[ci] = L["attn"]
            t_ref[ci] = L["T"]

    cc = pl.BlockSpec((None, cpb, CHUNK, CHUNK), lambda h, i: (h, i, 0, 0))
    return pl.pallas_call(
        body, grid=(nh, S // rb),
        in_specs=[pl.BlockSpec((rb, HEAD_D), lambda h, i: (i, 3 * h)), pl.BlockSpec((rb, HEAD_D), lambda h, i: (i, 3 * h + 1)),
                  pl.BlockSpec((rb, HEAD_D), lambda h, i: (i, 3 * h + 2)), pl.BlockSpec((rb, LANE), lambda h, i: (i, 0))],
        out_specs=[pl.BlockSpec((rb, HEAD_D), lambda h, i: (i, h)),
                   pl.BlockSpec((None, 3 * rb, HEAD_D), lambda h, i: (h, i, 0)), cc, cc],
        out_shape=[sds((S, nh * HEAD_D), F32), sds((nh, 3 * S, HEAD_D), F32), sds((nh, nc, CHUNK, CHUNK), F32),
                   sds((nh, nc, CHUNK, CHUNK), F32)],
        compiler_params=_params(("parallel", "parallel")), name="gdn_local_fwd")(qkv, qkv, qkv, gb)


def _gdn_scan_fwd(u, wqk, attn, gb, nh, dep):
    S = u.shape[0]
    nc = S // CHUNK
    rb = min(GDN_ROWS, S)
    cpb = rb // CHUNK
    sds = jax.ShapeDtypeStruct

    def body(u_ref, wqk_ref, attn_ref, gb_ref, dep_ref, o_ref, vn_ref, st_ref, s_ref):
        @pl.when(pl.program_id(0) == 0)
        def _():
            s_ref[...] = jnp.zeros_like(s_ref)

        for ci in range(cpb):
            rows = slice(ci * CHUNK, (ci + 1) * CHUNK)
            glv = jnp.exp(jnp.sum(gb_ref[rows, :], axis=0, keepdims=True))
            base = ci * WQK
            heads = range(nh)
            cols = [slice(h * HEAD_D, (h + 1) * HEAD_D) for h in heads]
            states = [s_ref[h] for h in heads]
            rs = [_gdot(wqk_ref[h, base:base + 2 * CHUNK, :], states[h], NN) for h in heads]
            vbs = [u_ref[rows, cols[h]] - rs[h][:CHUNK] for h in heads]
            os_ = [_gdot(attn_ref[h, ci], vbs[h], NN) for h in heads]
            ks_ = [_gdot(wqk_ref[h, base + 2 * CHUNK:base + WQK, :], vbs[h], TN) for h in heads]
            for h in heads:
                st_ref[h, ci] = states[h]
                o_ref[rows, cols[h]] = rs[h][CHUNK:] + os_[h]
                vn_ref[rows, cols[h]] = vbs[h]
                s_ref[h] = states[h] * glv[:, h:h + 1] + ks_[h]

    return pl.pallas_call(
        body, grid=(S // rb,),
        in_specs=[pl.BlockSpec((rb, nh * HEAD_D), lambda i: (i, 0)), pl.BlockSpec((nh, 3 * rb, HEAD_D), lambda i: (0, i, 0)),
                  pl.BlockSpec((nh, cpb, CHUNK, CHUNK), lambda i: (0, i, 0, 0)), pl.BlockSpec((rb, LANE), lambda i: (i, 0)),
                  pl.BlockSpec(memory_space=pl.ANY)],
        out_specs=[pl.BlockSpec((rb, nh * HEAD_D), lambda i: (i, 0)), pl.BlockSpec((rb, nh * HEAD_D), lambda i: (i, 0)),
                   pl.BlockSpec((nh, cpb, HEAD_D, HEAD_D), lambda i: (0, i, 0, 0))],
        out_shape=[sds((S, nh * HEAD_D), F32), sds((S, nh * HEAD_D), F32), sds((nh, nc, HEAD_D, HEAD_D), F32)],
        scratch_shapes=[pltpu.VMEM((nh, HEAD_D, HEAD_D), F32)],
        compiler_params=_params(("arbitrary",)), name="gdn_scan_fwd")(u, wqk, attn, gb, dep)


def _gdn_scan_bwd(wqk, attn, gb, states, vn, do, nh, dep):
    S = vn.shape[0]
    nc = S // CHUNK
    rb = min(GDN_ROWS, S)
    cpb = rb // CHUNK
    last = S // rb - 1
    sds = jax.ShapeDtypeStruct

    def body(wqk_ref, attn_ref, gb_ref, st_ref, vn_ref, do_ref, dep_ref, dvn_ref, dw_ref, dqd_ref, dkd_ref, da_ref, dgl_ref,
             ds_ref):
        @pl.when(pl.program_id(0) == 0)
        def _():
            ds_ref[...] = jnp.zeros_like(ds_ref)

        row = lax.broadcasted_iota(jnp.int32, (CHUNK, CHUNK), 0)
        col = lax.broadcasted_iota(jnp.int32, (CHUNK, CHUNK), 1)
        for ci in reversed(range(cpb)):
            rows = slice(ci * CHUNK, (ci + 1) * CHUNK)
            glv = jnp.exp(jnp.sum(gb_ref[rows, :], axis=0, keepdims=True))
            base = ci * WQK
            heads = range(nh)
            cols = [slice(h * HEAD_D, (h + 1) * HEAD_D) for h in heads]
            states = [st_ref[h, ci] for h in heads]
            dSs = [ds_ref[h] for h in heads]
            vbs = [vn_ref[rows, cols[h]] for h in heads]
            dobs = [do_ref[rows, cols[h]] for h in heads]
            dv1 = [_gdot(attn_ref[h, ci], dobs[h], TN) for h in heads]
            dv2 = [_gdot(wqk_ref[h, base + 2 * CHUNK:base + WQK, :], dSs[h], NN) for h in heads]
            das = [_gdot(dobs[h], vbs[h], NT) for h in heads]
            dkds = [_gdot(vbs[h], dSs[h], NT) for h in heads]
            dvbs = [dv1[h] + dv2[h] for h in heads]
            xs = [_gdot(jnp.concatenate([dobs[h], dvbs[h]], axis=0), states[h], NT) for h in heads]
            dss = [_gdot(wqk_ref[h, base:base + 2 * CHUNK, :], jnp.concatenate([-dvbs[h], dobs[h]], axis=0), TN)
                   for h in heads]
            for h in heads:
                dqd_ref[rows, cols[h]] = xs[h][:CHUNK]
                dw_ref[rows, cols[h]] = -xs[h][CHUNK:]
                dvn_ref[rows, cols[h]] = dvbs[h]
                da_ref[h, ci] = jnp.where(col <= row, das[h], 0.0)
                dkd_ref[rows, cols[h]] = dkds[h]
                gl = glv[:, h:h + 1]
                dgl = jnp.sum(jnp.sum(states[h] * dSs[h], axis=0, keepdims=True), axis=1, keepdims=True)
                dgl_ref[h, ci] = jnp.broadcast_to(dgl * gl, (1, LANE))
                ds_ref[h] = dSs[h] * gl + dss[h]

    rv = lambda i: last - i
    wide = pl.BlockSpec((rb, nh * HEAD_D), lambda i: (rv(i), 0))
    return pl.pallas_call(
        body, grid=(S // rb,),
        in_specs=[pl.BlockSpec((nh, 3 * rb, HEAD_D), lambda i: (0, rv(i), 0)),
                  pl.BlockSpec((nh, cpb, CHUNK, CHUNK), lambda i: (0, rv(i), 0, 0)),
                  pl.BlockSpec((rb, LANE), lambda i: (rv(i), 0)),
                  pl.BlockSpec((nh, cpb, HEAD_D, HEAD_D), lambda i: (0, rv(i), 0, 0)), wide, wide,
                  pl.BlockSpec(memory_space=pl.ANY)],
        out_specs=[wide, wide, wide, wide, pl.BlockSpec((nh, cpb, CHUNK, CHUNK), lambda i: (0, rv(i), 0, 0)),
                   pl.BlockSpec((nh, cpb, 1, LANE), lambda i: (0, rv(i), 0, 0))],
        out_shape=[sds((S, nh * HEAD_D), F32), sds((S, nh * HEAD_D), F32), sds((S, nh * HEAD_D), F32),
                   sds((S, nh * HEAD_D), F32), sds((nh, nc, CHUNK, CHUNK), F32), sds((nh, nc, 1, LANE), F32)],
        scratch_shapes=[pltpu.VMEM((nh, HEAD_D, HEAD_D), F32)],
        compiler_params=_params(("arbitrary",)), name="gdn_scan_bwd")(wqk, attn, gb, states, vn, do, dep)


def _gdn_local_bwd(qkv, gb, T, u, wqk, dvn, dw, dqd, dkd, dattn, dgl, nh):
    S = qkv.shape[0]
    rb = min(GDN_LOCAL_ROWS, S)
    cpb = rb // CHUNK
    sds = jax.ShapeDtypeStruct

    def body(q_ref, k_ref, v_ref, gb_ref, t_ref, u_ref, wqk_ref, dvn_ref, dw_ref, dqd_ref, dkd_ref, da_ref, dgl_ref,
             dqkv_ref, dg_ref, db_ref):
        h = pl.program_id(0)
        n = range(cpb)
        rows_of = [slice(ci * CHUNK, (ci + 1) * CHUNK) for ci in n]
        qs, ks, vs = ([r[rows, :] for rows in rows_of] for r in (q_ref, k_ref, v_ref))
        gbts = [gb_ref[rows, :] for rows in rows_of]
        bcols = [_col_of(t, nh + h) for t in gbts]
        Ts = [t_ref[ci] for ci in n]
        Ls = _chunks_local(qs, ks, vs, [_col_of(t, h) for t in gbts], bcols, Ts=Ts)
        drvs = [_gdot(Ts[ci], dvn_ref[rows_of[ci], :], TN) for ci in n]
        drks = [_gdot(Ts[ci], dw_ref[rows_of[ci], :], TN) for ci in n]
        dAs = [jnp.where(Ls[ci]["strict"], -(_gdot(drvs[ci], u_ref[rows_of[ci], :], NT)
                                             + _gdot(drks[ci], wqk_ref[ci * WQK:ci * WQK + CHUNK, :], NT)), 0.0) for ci in n]
        dMs = [dAs[ci] * Ls[ci]["decay"] for ci in n]
        dNs = [da_ref[ci] * Ls[ci]["decay"] for ci in n]
        dkbs = [_gdot(dMs[ci], ks[ci], NN) for ci in n]
        dq1 = [_gdot(dNs[ci], ks[ci], NN) for ci in n]
        dk1 = [_gdot(dMs[ci], Ls[ci]["kb"], TN) for ci in n]
        dk2 = [_gdot(dNs[ci], qs[ci], TN) for ci in n]
        for ci in n:
            rows, L, q, k, v, bcol = rows_of[ci], Ls[ci], qs[ci], ks[ci], vs[ci], bcols[ci]
            eye, eG, eGr = L["eye"], L["eG"], L["eGr"]
            drv, drk, dkb = drvs[ci], drks[ci], dkbs[ci]
            dq_dec, dk_dec, dattn_c = dqd_ref[rows, :], dkd_ref[rows, :], da_ref[ci]
            dqkv_ref[rows, :HEAD_D] = dq1[ci] + dq_dec * eG
            dqkv_ref[rows, HEAD_D:2 * HEAD_D] = drk * (bcol * eG) + dk1[ci] + dkb * bcol + dk2[ci] + dk_dec * eGr
            dqkv_ref[rows, 2 * HEAD_D:] = drv * bcol
            db_ref[rows, :] = (jnp.sum(drv * v, axis=1, keepdims=True) + jnp.sum(drk * k, axis=1, keepdims=True) * eG
                               + jnp.sum(dkb * k, axis=1, keepdims=True))
            E = dAs[ci] * L["A"] + dattn_c * L["attn"]
            kd = jnp.sum(dk_dec * L["k_dec"], axis=1, keepdims=True)
            dG = (jnp.sum(dq_dec * L["q_dec"], axis=1, keepdims=True) - kd
                  + jnp.sum(drk * L["rhs_k"], axis=1, keepdims=True)
                  + jnp.sum(E, axis=1, keepdims=True) - _to_col(jnp.sum(E, axis=0, keepdims=True), eye))
            d_last = jnp.sum(kd, axis=0, keepdims=True) + dgl_ref[ci][:, :1]
            dG = dG + jnp.where(L["row"][:, :1] == CHUNK - 1, d_last, 0.0)
            dg_ref[rows, :] = jnp.sum(jnp.where(L["col"] >= L["row"], _to_row(dG, eye), 0.0), axis=1, keepdims=True)

    hd = pl.BlockSpec((rb, HEAD_D), lambda h, i: (i, h))
    cc = pl.BlockSpec((None, cpb, CHUNK, CHUNK), lambda h, i: (h, i, 0, 0))
    col1 = pl.BlockSpec((None, rb, 1), lambda h, i: (h, i, 0))
    return pl.pallas_call(
        body, grid=(nh, S // rb),
        in_specs=[pl.BlockSpec((rb, HEAD_D), lambda h, i: (i, 3 * h)), pl.BlockSpec((rb, HEAD_D), lambda h, i: (i, 3 * h + 1)),
                  pl.BlockSpec((rb, HEAD_D), lambda h, i: (i, 3 * h + 2)), pl.BlockSpec((rb, LANE), lambda h, i: (i, 0)),
                  cc, hd, pl.BlockSpec((None, 3 * rb, HEAD_D), lambda h, i: (h, i, 0)), hd, hd, hd, hd, cc,
                  pl.BlockSpec((None, cpb, 1, LANE), lambda h, i: (h, i, 0, 0))],
        out_specs=[pl.BlockSpec((rb, 3 * HEAD_D), lambda h, i: (i, h)), col1, col1],
        out_shape=[sds((S, 3 * nh * HEAD_D), F32)] + [sds((nh, S, 1), F32)] * 2,
        compiler_params=_params(("parallel", "parallel")), name="gdn_local_bwd")(
            qkv, qkv, qkv, gb, T, u, wqk, dvn, dw, dqd, dkd, dattn, dgl)


def _gated_norm_fwd(o, proj, norm_w, nh, z_blk0, mix, m_blk0):
    S = o.shape[0]

    def body(o_ref, z_ref, w_ref, mix_ref, y_ref):
        ov, z = o_ref[...], z_ref[...]
        r = lax.rsqrt(jnp.mean(ov * ov, axis=1, keepdims=True) + RMS_EPS)
        y_ref[...] = ov * r * w_ref[...] * (z * _sigmoid(z))

    return pl.pallas_call(
        body, grid=(nh,),
        in_specs=[pl.BlockSpec((S, HEAD_D), lambda h: (0, h)), pl.BlockSpec((S, HEAD_D), lambda h: (0, z_blk0 + h)),
                  pl.BlockSpec((1, HEAD_D), lambda h: (0, 0)), pl.BlockSpec(memory_space=pl.ANY)],
        out_specs=pl.BlockSpec((S, HEAD_D), lambda h: (0, m_blk0 + h)),
        out_shape=jax.ShapeDtypeStruct(mix.shape, F32), input_output_aliases={3: 0},
        compiler_params=_params(("parallel",)), name="gated_norm_fwd")(o, proj, norm_w, mix)


def _gated_norm_bwd(o, proj, norm_w, dmix, nh, z_blk0, d_blk0):
    S = o.shape[0]
    sds = jax.ShapeDtypeStruct

    def body(o_ref, z_ref, w_ref, dy_ref, do_ref, dz_ref, dw_ref):
        ov, z, w, dy = o_ref[...], z_ref[...], w_ref[...], dy_ref[...]
        r = lax.rsqrt(jnp.mean(ov * ov, axis=1, keepdims=True) + RMS_EPS)
        oh = ov * r
        sg = _sigmoid(z)
        dz_ref[...] = dy * (oh * w) * (sg * (1.0 + z * (1.0 - sg)))
        don = dy * (z * sg)
        @pl.when(pl.program_id(0) == 0)
        def _():
            dw_ref[...] = jnp.zeros_like(dw_ref)

        dw_ref[...] += jnp.sum(don * oh, axis=0, keepdims=True)
        doh = don * w
        do_ref[...] = r * (doh - oh * jnp.mean(doh * oh, axis=1, keepdims=True))

    return pl.pallas_call(
        body, grid=(nh,),
        in_specs=[pl.BlockSpec((S, HEAD_D), lambda h: (0, h)), pl.BlockSpec((S, HEAD_D), lambda h: (0, z_blk0 + h)),
                  pl.BlockSpec((1, HEAD_D), lambda h: (0, 0)), pl.BlockSpec((S, HEAD_D), lambda h: (0, d_blk0 + h))],
        out_specs=[pl.BlockSpec((S, HEAD_D), lambda h: (0, h)), pl.BlockSpec((S, HEAD_D), lambda h: (0, h)),
                   pl.BlockSpec((1, HEAD_D), lambda h: (0, 0))],
        out_shape=[sds((S, nh * HEAD_D), F32), sds((S, nh * HEAD_D), F32), sds((1, HEAD_D), F32)],
        compiler_params=_params(("arbitrary",)), name="gated_norm_bwd")(o, proj, norm_w, dmix)


def _adamw_math(w, g, m, v):
    m = ADAM_B1 * m + (1.0 - ADAM_B1) * g
    v = ADAM_B2 * v + (1.0 - ADAM_B2) * (g * g)
    m_hat = m / (1.0 - ADAM_B1 ** ADAM_STEP)
    v_hat = v / (1.0 - ADAM_B2 ** ADAM_STEP)
    delta = -ADAM_LR * (m_hat / (jnp.sqrt(v_hat) + ADAM_EPS) + ADAM_WD * w)
    return delta, m, v


def _slab_tiles(R, C, rows=256, cols=256):
    if R % rows == 0:
        return (rows, C), R // rows, lambda i: (i, 0)
    tc = _tile(C, cols)
    return (R, tc), C // tc, lambda i: (0, i)


def _adamw_big(parts, terms, chip, w, m, v, name):
    R, C = w.shape
    blk, steps, at = _slab_tiles(R, C)
    sds = jax.ShapeDtypeStruct

    def body(q_ref, p_ref, t_ref, w_ref, m_ref, v_ref, g_ref, d_ref, nm_ref, nv_ref):
        g = ((p_ref[...].astype(F32) + t_ref[0].astype(F32)) + t_ref[1].astype(F32)) + t_ref[2].astype(F32)
        g_ref[...] = g
        d_ref[...], nm_ref[...], nv_ref[...] = _adamw_math(w_ref[...], g, m_ref[...], v_ref[...])

    spec = pl.BlockSpec(blk, lambda i, q_ref: at(i))
    grid_spec = pltpu.PrefetchScalarGridSpec(
        num_scalar_prefetch=1, grid=(steps,),
        in_specs=[pl.BlockSpec((None,) + blk, lambda i, q_ref: (q_ref[0],) + at(i)),
                  pl.BlockSpec((3,) + blk, lambda i, q_ref: (0,) + at(i)), spec, spec, spec],
        out_specs=[spec] * 4)
    return pl.pallas_call(body, grid_spec=grid_spec, out_shape=[sds((R, C), F32)] * 4,
                          compiler_params=_params(("parallel",)), name=name)(chip, parts, terms, w, m, v)


def _adamw_small(ws, gs, ms, vs):
    n = len(ws)

    def body(*refs):
        for i in range(n):
            w, g, m, v = (refs[k * n + i][...] for k in range(4))
            d, nm, nv = _adamw_math(w, g, m, v)
            refs[4 * n + i][...] = d
            refs[5 * n + i][...] = nm
            refs[6 * n + i][...] = nv

    shapes = [jax.ShapeDtypeStruct(w.shape, F32) for w in ws]
    vm = pl.BlockSpec(memory_space=pltpu.VMEM)
    outs = pl.pallas_call(body, in_specs=[vm] * (4 * n), out_specs=[vm] * (3 * n), out_shape=shapes * 3,
                          name="adamw_small")(*ws, *gs, *ms, *vs)
    return outs[:n], outs[n:2 * n], outs[2 * n:]


MESH = pl.DeviceIdType.MESH
ANY = pl.BlockSpec(memory_space=pl.ANY)


def _place():
    x, y, c = lax.axis_index("x"), lax.axis_index("y"), lax.axis_index("c")
    return x, y, c, [(1 - x, y), (x, 1 - y), (1 - x, 1 - y)]


def _all_gather(shards):
    n = len(shards)

    def body(*refs):
        ins, outs = refs[:n], refs[n:2 * n]
        send_sems, recv_sems, local_sems = refs[2 * n:]
        x, y, c, chips = _place()
        me, sibling = (x, y, c), (x, y, 1 - c)

        def slot(px, py, pc):
            return 4 * px + 2 * py + pc

        def copy(w, k, block, to, src=None):
            dst = outs[w].at[slot(*block)]
            return pltpu.make_async_remote_copy(src_ref=dst if src is None else src, dst_ref=dst,
                                                send_sem=send_sems.at[w * 7 + k], recv_sem=recv_sems.at[w * 7 + k],
                                                device_id=to, device_id_type=MESH)

        mine = [pltpu.make_async_copy(ins[w], outs[w].at[slot(*me)], local_sems.at[w]) for w in range(n)]
        for cp in mine:
            cp.start()
        first = []
        for w in range(n):
            first.append(copy(w, 0, me, sibling, src=ins[w]))
            first += [copy(w, 1 + j, me, (*chip, c), src=ins[w]) for j, chip in enumerate(chips)]
        for cp in first:
            cp.start()
        passed = []
        for w in range(n):
            for j, chip in enumerate(chips):
                copy(w, 1 + j, (*chip, c), me).wait_recv()
                cp = copy(w, 4 + j, (*chip, c), sibling)
                cp.start()
                passed.append(cp)
        for w in range(n):
            copy(w, 0, sibling, me).wait_recv()
            for j, chip in enumerate(chips):
                copy(w, 4 + j, (*chip, 1 - c), me).wait_recv()
        for cp in first + passed:
            cp.wait_send()
        for cp in mine:
            cp.wait()

    return pl.pallas_call(
        body, in_specs=[ANY] * n, out_specs=[ANY] * n,
        out_shape=[jax.ShapeDtypeStruct((N_DEV,) + s.shape, s.dtype) for s in shards],
        scratch_shapes=[pltpu.SemaphoreType.DMA((7 * n,)), pltpu.SemaphoreType.DMA((7 * n,)),
                        pltpu.SemaphoreType.DMA((n,))],
        name="all_gather_weights")(*shards)


def _chip_sum(grad, recv, core, name):
    _, R, C = grad.shape
    blk, steps, at = _slab_tiles(R, C)

    def body(c_ref, g_ref, r_ref, o_ref):
        o_ref[...] = (g_ref[...].astype(F32) + r_ref[...].astype(F32)).astype(o_ref.dtype)

    grid_spec = pltpu.PrefetchScalarGridSpec(
        num_scalar_prefetch=1, grid=(4, steps),
        in_specs=[pl.BlockSpec((None,) + blk, lambda q, i, c_ref: (2 * q + c_ref[0],) + at(i)),
                  pl.BlockSpec((None,) + blk, lambda q, i, c_ref: (q,) + at(i))],
        out_specs=pl.BlockSpec((None,) + blk, lambda q, i, c_ref: (q,) + at(i)))
    return pl.pallas_call(body, grid_spec=grid_spec, out_shape=jax.ShapeDtypeStruct((4, R, C), BF16),
                          compiler_params=_params(("parallel", "parallel")), name=name)(core, grad, recv)


HBM_SPEC = pl.BlockSpec(memory_space=pltpu.HBM)
SEM_SPEC = pl.BlockSpec(memory_space=pltpu.SEMAPHORE)
DATAFLOW = pltpu.SideEffectType.DATAFLOW_SIDE_EFFECTING


def _split_start(name, bufs, plan, n, after=None):
    nb = len(bufs)
    extra = [] if after is None else [after]

    def body(*refs):
        send_sems, recv_sems, token = refs[nb + len(extra)], refs[nb + len(extra) + 1], refs[-1]
        for k, (src, dst, to) in enumerate(plan(refs[:nb])):
            pltpu.make_async_remote_copy(src_ref=src, dst_ref=dst, send_sem=send_sems.at[k], recv_sem=recv_sems.at[k],
                                         device_id=to, device_id_type=MESH).start()
        token[...] = jnp.zeros_like(token)

    outs = pl.pallas_call(
        body, name=name,
        out_shape=(pltpu.SemaphoreType.DMA((n,)), pltpu.SemaphoreType.DMA((n,)),
                   *[pltpu.HBM(b.shape, b.dtype) for b in bufs], jax.ShapeDtypeStruct((8, LANE), F32)),
        in_specs=[HBM_SPEC] * nb + [ANY] * len(extra),
        out_specs=(SEM_SPEC, SEM_SPEC, *[HBM_SPEC] * nb, pl.BlockSpec(memory_space=pltpu.VMEM)),
        input_output_aliases={i: 2 + i for i in range(nb)},
        compiler_params=pltpu.CompilerParams(has_side_effects=DATAFLOW))(
            *[pltpu.with_memory_space_constraint(b, pltpu.HBM) for b in bufs], *extra)
    return outs[0], outs[1], list(outs[2:2 + nb]), outs[-1]


def _split_wait(name, send_sems, recv_sems, bufs, plan, n, after):
    nb = len(bufs)

    def body(*refs):
        send_s, recv_s = refs[nb], refs[nb + 1]
        for k, (src, dst, to) in enumerate(plan(refs[:nb])):
            cp = pltpu.make_async_remote_copy(src_ref=src, dst_ref=dst, send_sem=send_s.at[k], recv_sem=recv_s.at[k],
                                              device_id=to, device_id_type=MESH)
            cp.wait_send()
            cp.wait_recv()

    outs = pl.pallas_call(
        body, name=name, out_shape=tuple(pltpu.HBM(b.shape, b.dtype) for b in bufs),
        in_specs=[HBM_SPEC] * nb + [SEM_SPEC, SEM_SPEC, ANY], out_specs=tuple([HBM_SPEC] * nb),
        input_output_aliases={i: i for i in range(nb)},
        compiler_params=pltpu.CompilerParams(has_side_effects=DATAFLOW))(*bufs, send_sems, recv_sems, after)
    return list(outs)


def _slot(px, py, pc):
    return 4 * px + 2 * py + pc


class _Exchanges:
    def __init__(self, tag, shards, dev, core):
        self.tag, self.shards, self.dev, self.core = tag, shards, dev, core
        self.n = len(shards)

    def _gather_plan1(self, refs):
        n = self.n
        x, y, c, chips = _place()
        out = []
        for w in range(n):
            dst = refs[n + w].at[_slot(x, y, c)]
            out.append((refs[w], dst, (x, y, 1 - c)))
            out += [(refs[w], dst, (px, py, c)) for px, py in chips]
        return out

    def _gather_plan2(self, refs):
        x, y, c, chips = _place()
        return [(refs[w].at[_slot(px, py, c)],) * 2 + ((x, y, 1 - c),) for w in range(self.n) for px, py in chips]

    def weights_start(self, after):
        lands = [lax.dynamic_update_slice(lax.empty((N_DEV,) + s.shape, s.dtype), s[None], (self.dev, 0, 0))
                 for s in self.shards]
        self.w1 = _split_start(self.tag + "gather_start_1", list(self.shards) + lands, self._gather_plan1, 4 * self.n, after)
        return self.w1[3]

    def weights_mid(self, after):
        send, recv, bufs, _ = self.w1
        lands = _split_wait(self.tag + "gather_wait_1", send, recv, bufs, self._gather_plan1, 4 * self.n, after)[self.n:]
        self.w2 = _split_start(self.tag + "gather_start_2", lands, self._gather_plan2, 3 * self.n)
        return self.w2[3]

    def weights_finish(self, after):
        send, recv, bufs, _ = self.w2
        return _split_wait(self.tag + "gather_wait_2", send, recv, bufs, self._gather_plan2, 3 * self.n, after)

    def _reduce_plan1(self, refs):
        n = self.n
        x, y, c, _ = _place()
        return [(refs[w].at[2 * q + (1 - c)], refs[n + w].at[q], (x, y, 1 - c)) for w in range(n) for q in range(4)]

    def _reduce_plan2(self, refs):
        n = self.n
        x, y, c, chips = _place()
        return [(refs[w].at[2 * px + py], refs[n + w].at[j], (px, py, c))
                for w in range(n) for j, (px, py) in enumerate(chips)]

    def grads_start(self, grads):
        lands = [lax.empty((4,) + g.shape[1:], g.dtype) for g in grads]
        self.g1 = _split_start(self.tag + "reduce_start_1", list(grads) + lands, self._reduce_plan1, 4 * self.n)
        return self.g1[3]

    def grads_mid(self, after):
        n = self.n
        send, recv, bufs, _ = self.g1
        bufs = _split_wait(self.tag + "reduce_wait_1", send, recv, bufs, self._reduce_plan1, 4 * n, after)
        core = self.core.reshape(1).astype(jnp.int32)
        self.parts = [_chip_sum(bufs[w], bufs[n + w], core, self.tag + "reduce_chip_sum_%d" % w) for w in range(n)]
        lands = [lax.empty((3,) + p.shape[1:], p.dtype) for p in self.parts]
        self.g2 = _split_start(self.tag + "reduce_start_2", self.parts + lands, self._reduce_plan2, 3 * n)
        return self.g2[3]

    def grads_finish(self, after):
        n = self.n
        send, recv, bufs, _ = self.g2
        bufs = _split_wait(self.tag + "reduce_wait_2", send, recv, bufs, self._reduce_plan2, 3 * n, after)
        self.parts, self.terms = bufs[:n], bufs[n:]


def _all_reduce_small(buf):
    R = buf.shape[0]

    def body(x_ref, o_ref, g_ref, send_sems, recv_sems):
        x, y, c, chips = _place()
        me, sibling = (x, y, c), (x, y, 1 - c)

        def slot(px, py, pc):
            return 4 * px + 2 * py + pc

        def copy(k, block, to, src=None):
            dst = g_ref.at[slot(*block)]
            return pltpu.make_async_remote_copy(src_ref=dst if src is None else src, dst_ref=dst,
                                                send_sem=send_sems.at[k], recv_sem=recv_sems.at[k],
                                                device_id=to, device_id_type=MESH)

        first = [copy(0, me, sibling, src=x_ref)]
        first += [copy(1 + j, me, (*chip, c), src=x_ref) for j, chip in enumerate(chips)]
        for cp in first:
            cp.start()
        g_ref[slot(*me)] = x_ref[...]
        passed = [copy(4 + j, (*chip, c), sibling) for j, chip in enumerate(chips)]
        for j, chip in enumerate(chips):
            copy(1 + j, (*chip, c), me).wait_recv()
            passed[j].start()
        copy(0, sibling, me).wait_recv()
        for j, chip in enumerate(chips):
            copy(4 + j, (*chip, 1 - c), me).wait_recv()
        for cp in first + passed:
            cp.wait_send()
        acc = g_ref[0]
        for s in range(1, N_DEV):
            acc = acc + g_ref[s]
        o_ref[...] = acc

    vm = pl.BlockSpec(memory_space=pltpu.VMEM)
    return pl.pallas_call(
        body, in_specs=[vm], out_specs=vm, out_shape=jax.ShapeDtypeStruct((R, LANE), F32),
        scratch_shapes=[pltpu.VMEM((N_DEV, R, LANE), F32), pltpu.SemaphoreType.DMA((7,)), pltpu.SemaphoreType.DMA((7,))],
        name="all_reduce_small")(buf)


def _pad_cols(a, width):
    return jnp.pad(a, ((0, 0), (0, width - a.shape[1])))


def _local_step(x, target, w_in_t, conv_w, a_log, dt_bias, delta_norm_w, sinks, rel_bias, ln1_g, ln1_b, ln2_g, ln2_b,
                ex, ex_in):
    S, D = x.shape
    aw = D // 2
    hq, hkv, nh = aw // HEAD_A, aw // HEAD_A // GQA, aw // HEAD_D
    kvw = hkv * HEAD_A
    c_q, c_k, c_v, c_d = 0, aw, aw + kvw, aw + 2 * kvw
    c_ab = c_d + 3 * aw
    c_z = c_ab + 2 * nh
    n_in = c_z + aw
    assert w_in_t.shape == (n_in, D), (w_in_t.shape, n_in)
    w_pt = jnp.concatenate([w_in_t[:c_ab], w_in_t[c_z:], jnp.pad(w_in_t[c_ab:c_z], ((0, LANE - 2 * nh), (0, 0)))], axis=0)
    p_z, p_ab = c_ab, c_ab + aw
    n_p = p_ab + LANE

    proj = _matmul(x, w_pt, NT, name="proj", tn=1152, deps=(ex.weights_start(w_pt),))
    bias = _attn_bias(rel_bias.T)
    attn_out = _attn_fwd(proj, bias, sinks.reshape(-1), hq, 0, c_k // kvw, c_v // kvw, D)
    conv2 = conv_w.reshape(CONV_W, 3 * aw)
    qkv = _gdn_prep_fwd(proj, conv2, nh, c_d // HEAD_D)
    ab = proj[:, p_ab:]
    al, dt = _pad_cols(a_log, LANE), _pad_cols(dt_bias, LANE)
    gb = _gates_fwd(ab, al, dt, nh)
    u_d, wqk, attn_d, t_d = _gdn_local_fwd(qkv, gb, nh)
    o_d, vn, states = _gdn_scan_fwd(u_d, wqk, attn_d, gb, nh, ex.weights_mid(u_d))
    mix = _gated_norm_fwd(o_d, proj, delta_norm_w, nh, p_z // HEAD_D, attn_out, aw // HEAD_D)
    w_o_g, w_up_g, w_down_g = ex.weights_finish(mix)
    w_o, w_down = w_o_g.reshape(D, D), w_down_g.reshape(-1, D)
    mixed = _matmul(mix, w_o, NN, name="out_proj")
    h1 = _ln1_fwd(x, mixed, ln1_g, ln1_b)
    u = _matmul(h1, w_up_g, NN, name="mlp_up", b_groups=True)
    mlp = _matmul(u, w_down, NN, name="mlp_down", a_fn=_relu_sq)
    dr2, loss_row, dln2_g, dln2_b = _ln2_loss(h1, mlp, ln2_g, ln2_b, target)

    du = _matmul(dr2, w_down, NT, name="d_mlp_act", epi=_relu_sq_grad, epi_in=(u,))
    dw_down = _matmul(u, dr2, TN, name="dw_down", a_fn=_relu_sq, out_dtype=BF16)
    dw_up = _matmul(h1, du, TN, name="dw_up", out_dtype=BF16, out_groups=N_DEV)
    dh_mlp = _matmul(du, w_up_g, NT, name="d_h1", b_groups=True)
    dr1, dln1_g, dln1_b = _ln1_bwd(x, mixed, ln1_g, dr2, dh_mlp)
    dw_o = _matmul(mix, dr1, TN, name="dw_o", out_dtype=BF16)
    tok = ex.grads_start([dw_o.reshape(N_DEV, -1, D), dw_up, dw_down.reshape(N_DEV, -1, D)])
    dmix = _matmul(dr1, w_o, NT, name="d_mix", deps=(tok,))
    dq_a, dk_a, dv_a, dbias, dsink = _attn_bwd(proj, bias, sinks.reshape(-1), mix, dmix, hq, 0, c_k // kvw, c_v // kvw)
    drel = _rel_bias_grad(dbias, hq)
    do_d, dz, dnw = _gated_norm_bwd(o_d, proj, delta_norm_w, dmix, nh, p_z // HEAD_D, aw // HEAD_D)
    dvn_s, dw_s, dqd, dkd, dattn_d, dgl = _gdn_scan_bwd(wqk, attn_d, gb, states, vn, do_d, nh, ex.grads_mid(dq_a))
    dqkv_n, dg, dbeta = _gdn_local_bwd(qkv, gb, t_d, u_d, wqk, dvn_s, dw_s, dqd, dkd, dattn_d, dgl, nh)
    dgb = _pad_cols(jnp.concatenate([dg.reshape(nh, S).T, dbeta.reshape(nh, S).T], axis=1), LANE)
    dab, da_log, ddt_bias = _gates_bwd(ab, al, dt, dgb, nh)
    dqkv_d, dconv = _gdn_prep_bwd(proj, conv2, dqkv_n, nh, c_d // HEAD_D)
    dproj = jnp.concatenate([dq_a, dk_a, dv_a, dqkv_d, dz, dab], axis=1)
    dw_pt = _matmul(dproj, x, TN, name="dw_in", out_dtype=BF16, tm=1152)
    dw_in_t = jnp.concatenate([dw_pt[:p_z], dw_pt[p_ab:p_ab + 2 * nh], dw_pt[p_z:p_ab]], axis=0)
    tok = ex_in.grads_start([dw_in_t.reshape(N_DEV, -1, D)])
    dx_proj = _matmul(dproj, w_pt, NN, name="d_x", tk=1920, deps=(tok,))
    grad_x = _grad_x(dr1, dx_proj, ex_in.grads_mid(dx_proj))
    ex.grads_finish(grad_x)

    small = dict(conv_w=dconv, a_log=da_log[:, :nh], dt_bias=ddt_bias[:, :nh], delta_norm_w=dnw,
                 attn_sinks=dsink[:, 0].reshape(1, hq), rel_bias=drel[:, :, 0].T,
                 ln1_g=dln1_g, ln1_b=dln1_b, ln2_g=dln2_g, ln2_b=dln2_b)
    return loss_row, grad_x, small


SMALL_ORDER = ("conv_w", "a_log", "dt_bias", "delta_norm_w", "attn_sinks", "rel_bias", "ln1_g", "ln1_b", "ln2_g", "ln2_b")


def _pack_small(loss_row, small):
    parts = [loss_row.reshape(-1)]
    for k in SMALL_ORDER:
        flat = small[k].reshape(-1)
        parts.append(jnp.pad(flat, (0, (-flat.shape[0]) % LANE)))
    flat = jnp.concatenate(parts)
    flat = jnp.pad(flat, (0, (-flat.shape[0]) % (8 * LANE)))
    return flat.reshape(-1, LANE)


def _unpack_small(buf, small_shapes):
    flat = buf.reshape(-1)
    loss = flat[0]
    off = LANE
    out = {}
    for k in SMALL_ORDER:
        n = int(np.prod(small_shapes[k]))
        out[k] = flat[off:off + n].reshape(small_shapes[k])
        off += n + (-n) % LANE
    return loss, out


def kernel(x, w_in, conv_w, a_log, dt_bias, delta_norm_w, attn_sinks, rel_bias, w_o, ln1_g, ln1_b, w_up, w_down, ln2_g, ln2_b, loss_target, m_w_in, m_conv_w, m_a_log, m_dt_bias, m_delta_norm_w, m_attn_sinks, m_rel_bias, m_w_o, m_ln1_g, m_ln1_b, m_w_up, m_w_down, m_ln2_g, m_ln2_b, v_w_in, v_conv_w, v_a_log, v_dt_bias, v_delta_norm_w, v_attn_sinks, v_rel_bias, v_w_o, v_ln1_g, v_ln1_b, v_w_up, v_w_down, v_ln2_g, v_ln2_b):
    S, D = x.shape[1], x.shape[2]
    core = lax.axis_index("c")
    dev = 4 * lax.axis_index("x") + 2 * lax.axis_index("y") + core

    (w_in_g,) = _all_gather([w_in[0].T.astype(BF16)])
    w_in_t = w_in_g.reshape(-1, D)
    ex = _Exchanges("", [w_o[0].astype(BF16), w_up[0].astype(BF16), w_down[0].astype(BF16)], dev, core)
    ex_in = _Exchanges("in_", [w_in_g[0]], dev, core)

    cw_sh = conv_w.shape[3]
    conv_place = lax.dynamic_update_slice(jnp.zeros((CONV_W, N_DEV * cw_sh), F32), conv_w[0, :, 0, :], (0, dev * cw_sh))
    conv_full = _all_reduce_small(jnp.pad(conv_place.reshape(-1, LANE), ((0, (-conv_place.size // LANE) % 8), (0, 0))))
    conv_full = conv_full[:conv_place.size // LANE].reshape(CONV_W, N_DEV * cw_sh)

    loss_row, grad_x, small = _local_step(
        x[0], loss_target[0], w_in_t, conv_full, a_log, dt_bias, delta_norm_w, attn_sinks, rel_bias,
        ln1_g, ln1_b, ln2_g, ln2_b, ex, ex_in)

    chip_arr = (dev // 2).reshape(1).astype(jnp.int32)
    big = {}
    for i, (name, w, m, v) in enumerate((("w_o", w_o, m_w_o, v_w_o), ("w_up", w_up, m_w_up, v_w_up),
                                         ("w_down", w_down, m_w_down, v_w_down))):
        big[name] = [o[None] for o in _adamw_big(ex.parts[i], ex.terms[i], chip_arr, w[0], m[0], v[0], "adamw_" + name)]
    ex_in.grads_finish(big["w_down"][0])
    outs = _adamw_big(ex_in.parts[0], ex_in.terms[0], chip_arr, w_in[0].T, m_w_in[0].T, v_w_in[0].T, "adamw_w_in")
    big["w_in"] = [o.T[None] for o in outs]

    small_shapes = {k: v.shape for k, v in small.items()}
    loss, small = _unpack_small(_all_reduce_small(_pack_small(loss_row, small)), small_shapes)
    small["conv_w"] = lax.dynamic_slice(small["conv_w"], (0, dev * cw_sh), (CONV_W, cw_sh))
    small["rel_bias"] = small["rel_bias"].reshape(rel_bias.shape)
    p2 = dict(conv_w=(conv_w, m_conv_w, v_conv_w), a_log=(a_log, m_a_log, v_a_log), dt_bias=(dt_bias, m_dt_bias, v_dt_bias),
              delta_norm_w=(delta_norm_w, m_delta_norm_w, v_delta_norm_w), attn_sinks=(attn_sinks, m_attn_sinks, v_attn_sinks),
              rel_bias=(rel_bias, m_rel_bias, v_rel_bias), ln1_g=(ln1_g, m_ln1_g, v_ln1_g), ln1_b=(ln1_b, m_ln1_b, v_ln1_b),
              ln2_g=(ln2_g, m_ln2_g, v_ln2_g), ln2_b=(ln2_b, m_ln2_b, v_ln2_b))
    two_d = lambda a: a.reshape(-1, a.shape[-1])
    ws = [two_d(p2[k][0]) for k in SMALL_ORDER]
    gs = [two_d(small[k]) for k in SMALL_ORDER]
    ms = [two_d(p2[k][1]) for k in SMALL_ORDER]
    vs = [two_d(p2[k][2]) for k in SMALL_ORDER]
    ds, nms, nvs = _adamw_small(ws, gs, ms, vs)
    res = {}
    for i, k in enumerate(SMALL_ORDER):
        shp = p2[k][0].shape
        res[k] = [gs[i].reshape(shp), ds[i].reshape(shp), nms[i].reshape(shp), nvs[i].reshape(shp)]
    res.update(big)
    order = ("w_in", "conv_w", "a_log", "dt_bias", "delta_norm_w", "attn_sinks", "rel_bias", "w_o", "ln1_g", "ln1_b",
             "w_up", "w_down", "ln2_g", "ln2_b")
    return (loss, grad_x[None], *[res[k][0] for k in order], *[res[k][1] for k in order],
            *[res[k][2] for k in order], *[res[k][3] for k in order])
```

```python
import functools
import math

import numpy as np
import jax
import jax.numpy as jnp
from jax import lax
from jax.experimental import pallas as pl
from jax.experimental.pallas import tpu as pltpu

F32 = jnp.float32
BF16 = jnp.bfloat16
HIGHEST = lax.Precision.HIGHEST

N_DEV = 8
HEAD_A = 64
GQA = 4
BLK = 128
N_BUCKETS = 32
MAX_DISTANCE = 128
HEAD_D = 128
CONV_W = 4
CHUNK = 64
NEG_INF = -1e30
LN_EPS = 1e-5
RMS_EPS = 1e-6
DN_ALPHA = 2.0 ** 0.25
ADAM_LR, ADAM_B1, ADAM_B2, ADAM_EPS, ADAM_WD, ADAM_STEP = 0.001, 0.9, 0.999, 1e-08, 0.01, 10

LANE = 128
VMEM_LIMIT = 56 * 1024 * 1024

NN = ((1,), (0,))
NT = ((1,), (1,))
TN = ((0,), (0,))


def _dot(a, b, dims, prec=None):
    return lax.dot_general(a, b, (dims, ((), ())), precision=prec, preferred_element_type=F32)


def _tile(dim, pref):
    if dim <= pref:
        return dim
    t = (pref // LANE) * LANE
    while t > LANE and dim % t:
        t -= LANE
    assert dim % t == 0, (dim, pref)
    return t


def _params(sem):
    return pltpu.CompilerParams(dimension_semantics=sem, vmem_limit_bytes=VMEM_LIMIT)


def _matmul(a, b, dims, *, name, out_dtype=F32, tm=1024, tn=1024, tk=2048, a_fn=None, epi=None, epi_in=(),
            b_groups=None, out_groups=None, deps=()):
    (ca,), (cb,) = dims
    M, K = a.shape[1 - ca], a.shape[ca]
    if b_groups:
        G, R, C = b.shape
        bshape = (R, G * C)
    else:
        bshape = b.shape
    N = bshape[1 - cb]
    assert bshape[cb] == K, (a.shape, b.shape, dims)
    tm, tk = _tile(M, tm), _tile(K, tk)
    if b_groups:
        lim = C if cb == 0 else tn
        tn = _tile(N, min(tn, lim))
        if cb == 1:
            tk = _tile(K, min(tk, C))
    else:
        tn = _tile(N, tn)
    if out_groups:
        tn = _tile(N, min(tn, N // out_groups))
    nk = K // tk

    def body(*refs):
        a_ref, b_ref = refs[0], refs[1]
        e_refs = refs[2:2 + len(epi_in)]
        o_ref = refs[2 + len(epi_in) + len(deps)]
        acc_ref = refs[3 + len(epi_in) + len(deps)] if nk > 1 else None
        k = pl.program_id(2)
        av = a_ref[...]
        if a_fn is not None:
            av = a_fn(av)
        prod = _dot(av.astype(BF16), b_ref[...].astype(BF16), dims)

        def finish(r):
            if epi is not None:
                r = epi(r, *[e[...] for e in e_refs])
            o_ref[...] = r.astype(out_dtype)

        if nk == 1:
            finish(prod)
            return

        @pl.when(k == 0)
        def _():
            acc_ref[...] = prod

        @pl.when(k > 0)
        def _():
            acc_ref[...] += prod

        @pl.when(k == nk - 1)
        def _():
            finish(acc_ref[...])

    a_spec = (pl.BlockSpec((tm, tk), lambda i, j, k: (i, k)) if ca == 1
              else pl.BlockSpec((tk, tm), lambda i, j, k: (k, i)))
    if b_groups:
        if cb == 0:
            per = C // tn
            b_spec = pl.BlockSpec((None, tk, tn), lambda i, j, k: (j // per, k, j % per))
        else:
            per = C // tk
            b_spec = pl.BlockSpec((None, tn, tk), lambda i, j, k: (k // per, j, k % per))
    else:
        b_spec = (pl.BlockSpec((tk, tn), lambda i, j, k: (k, j)) if cb == 0
                  else pl.BlockSpec((tn, tk), lambda i, j, k: (j, k)))
    e_specs = [pl.BlockSpec((tm, tn), lambda i, j, k: (i, j)) for _ in epi_in]
    if out_groups:
        per_o = (N // out_groups) // tn
        o_spec = pl.BlockSpec((None, tm, tn), lambda i, j, k: (j // per_o, i, j % per_o))
        o_shape = jax.ShapeDtypeStruct((out_groups, M, N // out_groups), out_dtype)
    else:
        o_spec = pl.BlockSpec((tm, tn), lambda i, j, k: (i, j))
        o_shape = jax.ShapeDtypeStruct((M, N), out_dtype)
    return pl.pallas_call(
        body, grid=(M // tm, N // tn, nk), out_specs=o_spec,
        in_specs=[a_spec, b_spec] + e_specs + [pl.BlockSpec(memory_space=pl.ANY)] * len(deps),
        out_shape=o_shape, scratch_shapes=[pltpu.VMEM((tm, tn), F32)] if nk > 1 else [],
        compiler_params=_params(("parallel", "parallel", "arbitrary")), name=name)(a, b, *epi_in, *deps)


def _relu_sq(u):
    r = jnp.maximum(u, 0.0)
    return r * r


def _relu_sq_grad(acc, u):
    return acc * (2.0 * jnp.maximum(u, 0.0))


def _ln_stats(r):
    mu = jnp.mean(r, axis=-1, keepdims=True)
    xc = r - mu
    var = jnp.mean(xc * xc, axis=-1, keepdims=True)
    rstd = lax.rsqrt(var + LN_EPS)
    return xc * rstd, rstd


def _ln_bwd(dy, xhat, rstd, g):
    dxh = dy * g
    m1 = jnp.mean(dxh, axis=-1, keepdims=True)
    m2 = jnp.mean(dxh * xhat, axis=-1, keepdims=True)
    return rstd * (dxh - m1 - xhat * m2)


def _row_call(body, ins, row_ins, outs, acc_outs, name, tr=256):
    S = ins[0].shape[0]
    tr = min(tr, S)
    n_in, n_row, n_out = len(ins), len(row_ins), len(outs)

    def wrapped(*refs):
        i = pl.program_id(0)
        acc_refs = refs[n_in + n_row + n_out:]

        @pl.when(i == 0)
        def _():
            for r in acc_refs:
                r[...] = jnp.zeros_like(r)

        body(*refs)

    in_specs = [pl.BlockSpec((tr, a.shape[1]), lambda i: (i, 0)) for a in ins]
    in_specs += [pl.BlockSpec(a.shape, lambda i: (0, 0)) for a in row_ins]
    out_specs = [pl.BlockSpec((tr, s.shape[1]), lambda i: (i, 0)) for s in outs]
    out_specs += [pl.BlockSpec(s.shape, lambda i: (0, 0)) for s in acc_outs]
    return pl.pallas_call(wrapped, grid=(S // tr,), in_specs=in_specs, out_specs=out_specs,
                          out_shape=list(outs) + list(acc_outs),
                          compiler_params=_params(("arbitrary",)), name=name)(*ins, *row_ins)


def _ln1_fwd(x, mixed, g, b):
    def body(x_ref, m_ref, g_ref, b_ref, h_ref):
        xhat, _ = _ln_stats(DN_ALPHA * x_ref[...] + m_ref[...])
        h_ref[...] = xhat * g_ref[...] + b_ref[...]
    return _row_call(body, [x, mixed], [g, b], [jax.ShapeDtypeStruct(x.shape, F32)], [], "ln1_fwd")[0]


def _ln2_loss(h1, mlp, g, b, target):
    S, D = h1.shape
    sds = jax.ShapeDtypeStruct

    def body(h_ref, m_ref, t_ref, g_ref, b_ref, dr_ref, loss_ref, dg_ref, db_ref):
        xhat, rstd = _ln_stats(DN_ALPHA * h_ref[...] + m_ref[...])
        gv = g_ref[...]
        err = xhat * gv + b_ref[...] - t_ref[...]
        loss_ref[...] += jnp.sum(jnp.sum(err * err, axis=0, keepdims=True), axis=1, keepdims=True) * (0.5 / D)
        dy = err * (1.0 / D)
        dg_ref[...] += jnp.sum(dy * xhat, axis=0, keepdims=True)
        db_ref[...] += jnp.sum(dy, axis=0, keepdims=True)
        dr_ref[...] = _ln_bwd(dy, xhat, rstd, gv)

    return _row_call(body, [h1, mlp, target], [g, b], [sds((S, D), F32)],
                     [sds((1, LANE), F32), sds((1, D), F32), sds((1, D), F32)], "ln2_loss")


def _ln1_bwd(x, mixed, g, dr2, dh_mlp):
    S, D = x.shape
    sds = jax.ShapeDtypeStruct

    def body(x_ref, m_ref, dr2_ref, dh_ref, g_ref, dr_ref, dg_ref, db_ref):
        xhat, rstd = _ln_stats(DN_ALPHA * x_ref[...] + m_ref[...])
        dy = DN_ALPHA * dr2_ref[...] + dh_ref[...]
        dg_ref[...] += jnp.sum(dy * xhat, axis=0, keepdims=True)
        db_ref[...] += jnp.sum(dy, axis=0, keepdims=True)
        dr_ref[...] = _ln_bwd(dy, xhat, rstd, g_ref[...])

    return _row_call(body, [x, mixed, dr2, dh_mlp], [g], [sds((S, D), F32)],
                     [sds((1, D), F32), sds((1, D), F32)], "ln1_bwd")


def _grad_x(dr1, dx_proj, dep):
    def body(a_ref, b_ref, dep_ref, o_ref):
        o_ref[...] = DN_ALPHA * a_ref[...] + b_ref[...]
    return _row_call(body, [dr1, dx_proj], [dep], [jax.ShapeDtypeStruct(dr1.shape, F32)], [], "grad_x")[0]


def _bucket_table():
    qi = np.arange(BLK, dtype=np.int32)[:, None]
    kj = np.arange(2 * BLK, dtype=np.int32)[None, :]
    dist = qi + BLK - kj
    n = np.maximum(dist, 0)
    max_exact = N_BUCKETS // 2
    nf = np.maximum(n, 1).astype(np.float32)
    large = max_exact + (np.log(nf / np.float32(max_exact)) / np.float32(math.log(MAX_DISTANCE / max_exact))
                         * np.float32(N_BUCKETS - max_exact)).astype(np.int32)
    large = np.minimum(large, N_BUCKETS - 1)
    bucket = np.where(n < max_exact, n, large)
    return np.where((dist >= 0) & (dist < BLK), bucket, -1).astype(np.int32)


def _attn_bias(rel_bias_t):
    hq = rel_bias_t.shape[0]
    bucket = jnp.asarray(_bucket_table())

    def body(rb_ref, bk_ref, o_ref):
        h = pl.program_id(0)
        bk = bk_ref[...]
        acc = jnp.zeros((BLK, 2 * BLK), F32)
        for b in range(N_BUCKETS):
            acc = jnp.where(bk == b, rb_ref[h, b], acc)
        o_ref[...] = acc

    return pl.pallas_call(
        body, grid=(hq,),
        in_specs=[pl.BlockSpec(memory_space=pltpu.SMEM), pl.BlockSpec((BLK, 2 * BLK), lambda h: (0, 0))],
        out_specs=pl.BlockSpec((BLK, 2 * BLK), lambda h: (h, 0)),
        out_shape=jax.ShapeDtypeStruct((hq * BLK, 2 * BLK), F32),
        compiler_params=_params(("arbitrary",)), name="attn_bias")(rel_bias_t, bucket)


def _attn_probs(sc, sp, bias, sink, mask_c, mask_p):
    lc = jnp.where(mask_c, sc + bias[:, BLK:], NEG_INF)
    lp = jnp.where(mask_p, sp + bias[:, :BLK], NEG_INF)
    m = jnp.maximum(jnp.maximum(jnp.max(lc, axis=1, keepdims=True), jnp.max(lp, axis=1, keepdims=True)), sink)
    pc, pp, ps = jnp.exp(lc - m), jnp.exp(lp - m), jnp.exp(sink - m)
    inv = 1.0 / (jnp.sum(pc, axis=1, keepdims=True) + jnp.sum(pp, axis=1, keepdims=True) + ps)
    return pc, pp, ps, inv


def _attn_masks(n):
    qi = lax.broadcasted_iota(jnp.int32, (BLK, BLK), 0)
    kj = lax.broadcasted_iota(jnp.int32, (BLK, BLK), 1)
    return kj <= qi, (kj > qi) & (n > 0)


def _attn_fwd(proj, bias, sinks, hq, q_blk, k_blk, v_blk, out_width):
    S = proj.shape[0]
    hkv = hq // GQA
    wq, wk = hq * HEAD_A, hkv * HEAD_A

    def body(q_ref, k_ref, v_ref, bias_ref, sink_ref, o_ref):
        n = pl.program_id(0)
        cur = pl.multiple_of(n * BLK, BLK)
        prev = pl.multiple_of(jnp.maximum(n - 1, 0) * BLK, BLK)
        mask_c, mask_p = _attn_masks(n)
        for h4 in range(hkv):
            cs = slice(h4 * HEAD_A, (h4 + 1) * HEAD_A)
            kc, kp = k_ref[pl.ds(cur, BLK), cs].astype(BF16), k_ref[pl.ds(prev, BLK), cs].astype(BF16)
            vc, vp = v_ref[pl.ds(cur, BLK), cs].astype(BF16), v_ref[pl.ds(prev, BLK), cs].astype(BF16)
            hs_of = [slice(h * HEAD_A, (h + 1) * HEAD_A) for h in range(h4 * GQA, (h4 + 1) * GQA)]
            qs = [(q_ref[:, hs] * (HEAD_A ** -0.5)).astype(BF16) for hs in hs_of]
            scs = [_dot(q, kc, NT) for q in qs]
            sps = [_dot(q, kp, NT) for q in qs]
            pr = [_attn_probs(scs[g], sps[g], bias_ref[(h4 * GQA + g) * BLK:(h4 * GQA + g + 1) * BLK, :],
                              sink_ref[h4 * GQA + g], mask_c, mask_p) for g in range(GQA)]
            oc = [_dot(p[0].astype(BF16), vc, NN) for p in pr]
            op = [_dot(p[1].astype(BF16), vp, NN) for p in pr]
            for g, hs in enumerate(hs_of):
                o_ref[:, hs] = (oc[g] + op[g]) * pr[g][3]

    return pl.pallas_call(
        body, grid=(S // BLK,),
        in_specs=[pl.BlockSpec((BLK, wq), lambda n: (n, q_blk)), pl.BlockSpec((S, wk), lambda n: (0, k_blk)),
                  pl.BlockSpec((S, wk), lambda n: (0, v_blk)), pl.BlockSpec((hq * BLK, 2 * BLK), lambda n: (0, 0)),
                  pl.BlockSpec(memory_space=pltpu.SMEM)],
        out_specs=pl.BlockSpec((BLK, wq), lambda n: (n, 0)),
        out_shape=jax.ShapeDtypeStruct((S, out_width), F32),
        compiler_params=_params(("arbitrary",)), name="attn_fwd")(proj, proj, proj, bias, sinks)


def _attn_bwd(proj, bias, sinks, out, dmix, hq, q_blk, k_blk, v_blk):
    S = proj.shape[0]
    hkv = hq // GQA
    wq, wk = hq * HEAD_A, hkv * HEAD_A
    sds = jax.ShapeDtypeStruct

    def body(q_ref, k_ref, v_ref, bias_ref, sink_ref, o_ref, do_ref, dq_ref, dk_ref, dv_ref, dbias_ref, dsink_ref):
        n = pl.program_id(0)

        @pl.when(n == 0)
        def _():
            dk_ref[...] = jnp.zeros_like(dk_ref)
            dv_ref[...] = jnp.zeros_like(dv_ref)
            dbias_ref[...] = jnp.zeros_like(dbias_ref)
            dsink_ref[...] = jnp.zeros_like(dsink_ref)

        cur = pl.multiple_of(n * BLK, BLK)
        prev = pl.multiple_of(jnp.maximum(n - 1, 0) * BLK, BLK)
        mask_c, mask_p = _attn_masks(n)
        for h4 in range(hkv):
            cs = slice(h4 * HEAD_A, (h4 + 1) * HEAD_A)
            kc, kp = k_ref[pl.ds(cur, BLK), cs].astype(BF16), k_ref[pl.ds(prev, BLK), cs].astype(BF16)
            vc, vp = v_ref[pl.ds(cur, BLK), cs].astype(BF16), v_ref[pl.ds(prev, BLK), cs].astype(BF16)
            heads = list(range(h4 * GQA, (h4 + 1) * GQA))
            hs_of = [slice(h * HEAD_A, (h + 1) * HEAD_A) for h in heads]
            rows_of = [slice(h * BLK, (h + 1) * BLK) for h in heads]
            G = range(GQA)
            qs = [(q_ref[:, hs] * (HEAD_A ** -0.5)).astype(BF16) for hs in hs_of]
            dos = [do_ref[:, hs] for hs in hs_of]
            dobs = [d.astype(BF16) for d in dos]
            scs = [_dot(q, kc, NT) for q in qs]
            sps = [_dot(q, kp, NT) for q in qs]
            dpc = [_dot(d, vc, NT) for d in dobs]
            dpp = [_dot(d, vp, NT) for d in dobs]
            pcs, pps, dscs, dsps = [], [], [], []
            for g in G:
                pc, pp, ps, inv = _attn_probs(scs[g], sps[g], bias_ref[rows_of[g], :], sink_ref[heads[g]], mask_c, mask_p)
                pc, pp, ps = pc * inv, pp * inv, ps * inv
                delta = jnp.sum(dos[g] * o_ref[:, hs_of[g]], axis=1, keepdims=True)
                dsc, dsp = pc * (dpc[g] - delta), pp * (dpp[g] - delta)
                dsink_ref[heads[g]:heads[g] + 1, :] += jnp.broadcast_to(jnp.sum(-ps * delta, axis=0, keepdims=True), (1, LANE))
                dbias_ref[rows_of[g], BLK:] += dsc
                dbias_ref[rows_of[g], :BLK] += dsp
                pcs.append(pc.astype(BF16))
                pps.append(pp.astype(BF16))
                dscs.append(dsc.astype(BF16))
                dsps.append(dsp.astype(BF16))
            dq1 = [_dot(dscs[g], kc, NN) for g in G]
            dq2 = [_dot(dsps[g], kp, NN) for g in G]
            dkc = [_dot(dscs[g], qs[g], TN) for g in G]
            dkp = [_dot(dsps[g], qs[g], TN) for g in G]
            dvc = [_dot(pcs[g], dobs[g], TN) for g in G]
            dvp = [_dot(pps[g], dobs[g], TN) for g in G]
            for g in G:
                dq_ref[:, hs_of[g]] = (dq1[g] + dq2[g]) * (HEAD_A ** -0.5)
            dk_ref[pl.ds(cur, BLK), cs] += sum(dkc[1:], dkc[0])
            dk_ref[pl.ds(prev, BLK), cs] += sum(dkp[1:], dkp[0])
            dv_ref[pl.ds(cur, BLK), cs] += sum(dvc[1:], dvc[0])
            dv_ref[pl.ds(prev, BLK), cs] += sum(dvp[1:], dvp[0])

    return pl.pallas_call(
        body, grid=(S // BLK,),
        in_specs=[pl.BlockSpec((BLK, wq), lambda n: (n, q_blk)), pl.BlockSpec((S, wk), lambda n: (0, k_blk)),
                  pl.BlockSpec((S, wk), lambda n: (0, v_blk)), pl.BlockSpec((hq * BLK, 2 * BLK), lambda n: (0, 0)),
                  pl.BlockSpec(memory_space=pltpu.SMEM),
                  pl.BlockSpec((BLK, wq), lambda n: (n, 0)), pl.BlockSpec((BLK, wq), lambda n: (n, 0))],
        out_specs=[pl.BlockSpec((BLK, wq), lambda n: (n, 0)), pl.BlockSpec((S, wk), lambda n: (0, 0)),
                   pl.BlockSpec((S, wk), lambda n: (0, 0)), pl.BlockSpec((hq * BLK, 2 * BLK), lambda n: (0, 0)),
                   pl.BlockSpec((hq, LANE), lambda n: (0, 0))],
        out_shape=[sds((S, wq), F32), sds((S, wk), F32), sds((S, wk), F32), sds((hq * BLK, 2 * BLK), F32),
                   sds((hq, LANE), F32)],
        compiler_params=_params(("arbitrary",)), name="attn_bwd")(proj, proj, proj, bias, sinks, out, dmix)


def _rel_bias_grad(dbias, hq):
    bucket = jnp.asarray(_bucket_table())

    def body(d_ref, bk_ref, o_ref):
        d = d_ref[...]
        bk = bk_ref[...]
        rows = [jnp.sum(jnp.where(bk == b, d, 0.0), axis=0, keepdims=True) for b in range(N_BUCKETS)]
        tot = jnp.sum(jnp.concatenate(rows, axis=0), axis=1, keepdims=True)
        o_ref[...] = jnp.broadcast_to(tot, (N_BUCKETS, LANE))

    return pl.pallas_call(
        body, grid=(hq,),
        in_specs=[pl.BlockSpec((BLK, 2 * BLK), lambda h: (h, 0)), pl.BlockSpec((BLK, 2 * BLK), lambda h: (0, 0))],
        out_specs=pl.BlockSpec((None, N_BUCKETS, LANE), lambda h: (h, 0, 0)),
        out_shape=jax.ShapeDtypeStruct((hq, N_BUCKETS, LANE), F32),
        compiler_params=_params(("arbitrary",)), name="rel_bias_grad")(dbias, bucket)


def _sigmoid(x):
    return 1.0 / (1.0 + jnp.exp(-x))


def _shift_rows(x, s):
    n = x.shape[0]
    row = lax.broadcasted_iota(jnp.int32, x.shape, 0)
    if s > 0:
        return jnp.where(row >= s, pltpu.roll(x, s, 0), 0.0)
    return jnp.where(row < n + s, pltpu.roll(x, n + s, 0), 0.0)


def _conv_silu_norm(xv, w, j, nh):
    c = w[CONV_W - 1:CONV_W, :] * xv
    for s in range(1, CONV_W):
        c = c + w[CONV_W - 1 - s:CONV_W - s, :] * _shift_rows(xv, s)
    sg = _sigmoid(c)
    a = c * sg
    r = lax.rsqrt(jnp.sum(a * a, axis=1, keepdims=True) + RMS_EPS)
    scale = jnp.where(j < nh, HEAD_D ** -0.5, 1.0)
    is_norm = j < 2 * nh
    y = jnp.where(is_norm, a * (r * scale), a)
    return c, sg, a, r, scale, is_norm, y


def _gdn_prep_fwd(proj, conv_w, nh, blk0):
    S = proj.shape[0]

    def body(x_ref, w_ref, o_ref):
        j = pl.program_id(0)
        o_ref[...] = _conv_silu_norm(x_ref[...], w_ref[...], j, nh)[-1]

    return pl.pallas_call(
        body, grid=(3 * nh,),
        in_specs=[pl.BlockSpec((S, HEAD_D), lambda j: (0, blk0 + j)), pl.BlockSpec((CONV_W, HEAD_D), lambda j: (0, j))],
        out_specs=pl.BlockSpec((S, HEAD_D), lambda j: (0, 3 * (j % nh) + j // nh)),
        out_shape=jax.ShapeDtypeStruct((S, 3 * nh * HEAD_D), F32),
        compiler_params=_params(("parallel",)), name="gdn_prep_fwd")(proj, conv_w)


def _gdn_prep_bwd(proj, conv_w, dqkv, nh, blk0):
    S = proj.shape[0]
    sds = jax.ShapeDtypeStruct

    def body(x_ref, w_ref, dy_ref, dx_ref, dw_ref):
        j = pl.program_id(0)
        xv, w = x_ref[...], w_ref[...]
        c, sg, a, r, scale, is_norm, _ = _conv_silu_norm(xv, w, j, nh)
        dy = dy_ref[...]
        rs = r * scale
        da_n = rs * dy - a * (r * r * rs) * jnp.sum(dy * a, axis=1, keepdims=True)
        da = jnp.where(is_norm, da_n, dy)
        dc = da * (sg * (1.0 + c * (1.0 - sg)))
        dx = w[CONV_W - 1:CONV_W, :] * dc
        dws = [jnp.sum(dc * xv, axis=0, keepdims=True)]
        for s in range(1, CONV_W):
            dx = dx + w[CONV_W - 1 - s:CONV_W - s, :] * _shift_rows(dc, -s)
            dws.insert(0, jnp.sum(dc * _shift_rows(xv, s), axis=0, keepdims=True))
        dx_ref[...] = dx
        dw_ref[...] = jnp.concatenate(dws, axis=0)

    return pl.pallas_call(
        body, grid=(3 * nh,),
        in_specs=[pl.BlockSpec((S, HEAD_D), lambda j: (0, blk0 + j)), pl.BlockSpec((CONV_W, HEAD_D), lambda j: (0, j)),
                  pl.BlockSpec((S, HEAD_D), lambda j: (0, 3 * (j % nh) + j // nh))],
        out_specs=[pl.BlockSpec((S, HEAD_D), lambda j: (0, j)), pl.BlockSpec((CONV_W, HEAD_D), lambda j: (0, j))],
        out_shape=[sds((S, 3 * nh * HEAD_D), F32), sds((CONV_W, 3 * nh * HEAD_D), F32)],
        compiler_params=_params(("parallel",)), name="gdn_prep_bwd")(proj, conv_w, dqkv)


def _softplus(x):
    return jnp.maximum(x, 0.0) + jnp.log(1.0 + jnp.exp(-jnp.abs(x)))


def _gates_fwd(ab, al, dt, nh):
    S = ab.shape[0]

    def body(ab_ref, al_ref, dt_ref, o_ref):
        v = ab_ref[...]
        lane = lax.broadcasted_iota(jnp.int32, v.shape, 1)
        g = -jnp.exp(al_ref[...]) * _softplus(v + dt_ref[...])
        o_ref[...] = jnp.where(lane < nh, g, jnp.where(lane < 2 * nh, _sigmoid(v), 0.0))

    row = pl.BlockSpec((1, LANE), lambda i: (0, 0))
    full = pl.BlockSpec((S, LANE), lambda i: (0, 0))
    return pl.pallas_call(body, grid=(1,), in_specs=[full, row, row], out_specs=full,
                          out_shape=jax.ShapeDtypeStruct((S, LANE), F32),
                          compiler_params=_params(("arbitrary",)), name="gates_fwd")(ab, al, dt)


def _gates_bwd(ab, al, dt, dgb, nh):
    S = ab.shape[0]
    sds = jax.ShapeDtypeStruct

    def body(ab_ref, al_ref, dt_ref, d_ref, dab_ref, dal_ref, ddt_ref):
        v, d = ab_ref[...], d_ref[...]
        lane = lax.broadcasted_iota(jnp.int32, v.shape, 1)
        is_a = lane < nh
        z = v + dt_ref[...]
        dsp = jnp.where(is_a, d * (-jnp.exp(al_ref[...])), 0.0)
        dz = dsp * _sigmoid(z)
        beta = _sigmoid(v)
        dab_ref[...] = jnp.where(is_a, dz, jnp.where(lane < 2 * nh, d * beta * (1.0 - beta), 0.0))
        dal_ref[...] = jnp.sum(dsp * _softplus(z), axis=0, keepdims=True)
        ddt_ref[...] = jnp.sum(dz, axis=0, keepdims=True)

    row = pl.BlockSpec((1, LANE), lambda i: (0, 0))
    full = pl.BlockSpec((S, LANE), lambda i: (0, 0))
    return pl.pallas_call(body, grid=(1,), in_specs=[full, row, row, full], out_specs=[full, row, row],
                          out_shape=[sds((S, LANE), F32), sds((1, LANE), F32), sds((1, LANE), F32)],
                          compiler_params=_params(("arbitrary",)), name="gates_bwd")(ab, al, dt, dgb)


def _col_of(tile, h):
    lane = lax.broadcasted_iota(jnp.int32, tile.shape, 1)
    return jnp.sum(jnp.where(lane == h, tile, 0.0), axis=1, keepdims=True)


def _to_row(col, eye):
    return jnp.sum(jnp.where(eye, col, 0.0), axis=0, keepdims=True)


def _to_col(row, eye):
    return jnp.sum(jnp.where(eye, row, 0.0), axis=1, keepdims=True)


def _split(a):
    hi = a.astype(BF16)
    return hi, (a - hi.astype(F32)).astype(BF16)


def _gdot(a, b, dims):
    ah, al = _split(a)
    bh, bl = _split(b)
    return _dot(ah, bh, dims) + (_dot(ah, bl, dims) + _dot(al, bh, dims))


def _chunks_local(qs, ks, vs, gcols, bcols, Ts=None):
    C = CHUNK
    row = lax.broadcasted_iota(jnp.int32, (C, C), 0)
    col = lax.broadcasted_iota(jnp.int32, (C, C), 1)
    tril, strict, eye = col <= row, col < row, col == row
    outs = []
    for k, gcol, bcol in zip(ks, gcols, bcols):
        grow = _to_row(gcol, eye)
        G_row = jnp.sum(jnp.where(row <= col, gcol, 0.0), axis=0, keepdims=True)
        G_col = jnp.sum(jnp.where(tril, grow, 0.0), axis=1, keepdims=True)
        G_last = G_col[C - 1:C, :]
        outs.append(dict(strict=strict, eye=eye, row=row, col=col, decay=jnp.exp(jnp.where(tril, G_col - G_row, NEG_INF)),
                         eG=jnp.exp(G_col), eGr=jnp.exp(G_last - G_col), gl=jnp.exp(G_last), kb=k * bcol))
    Ms = [_gdot(o["kb"], k, NT) for o, k in zip(outs, ks)]
    Ns = [_gdot(q, k, NT) for q, k in zip(qs, ks)]
    for o, q, k, M, N in zip(outs, qs, ks, Ms, Ns):
        o.update(A=jnp.where(strict, M * o["decay"], 0.0), attn=N * o["decay"], rhs_k=o["kb"] * o["eG"],
                 q_dec=q * o["eG"], k_dec=k * o["eGr"])
    if Ts is None:
        Ts = [jnp.where(eye, 1.0, 0.0) - o["A"] for o in outs]
        Ps = [o["A"] for o in outs]
        for _ in range(int(math.log2(C)) - 1):
            Ps = [_gdot(P, P, NN) for P in Ps]
            Ts = [T + _gdot(T, P, NN) for T, P in zip(Ts, Ps)]
        us = [_gdot(T, v * bcol, NN) for T, v, bcol in zip(Ts, vs, bcols)]
        ws = [_gdot(T, o["rhs_k"], NN) for T, o in zip(Ts, outs)]
        for o, T, u, w in zip(outs, Ts, us, ws):
            o.update(T=T, u=u, w=w)
    return outs


GDN_ROWS = 256
GDN_LOCAL_ROWS = 512
WQK = 3 * CHUNK


def _gdn_local_fwd(qkv, gb, nh):
    S = qkv.shape[0]
    nc = S // CHUNK
    rb = min(GDN_LOCAL_ROWS, S)
    cpb = rb // CHUNK
    sds = jax.ShapeDtypeStruct

    def body(q_ref, k_ref, v_ref, gb_ref, u_ref, wqk_ref, attn_ref, t_ref):
        h = pl.program_id(0)
        rows_of = [slice(ci * CHUNK, (ci + 1) * CHUNK) for ci in range(cpb)]
        gbts = [gb_ref[rows, :] for rows in rows_of]
        Ls = _chunks_local([q_ref[rows, :] for rows in rows_of], [k_ref[rows, :] for rows in rows_of],
                           [v_ref[rows, :] for rows in rows_of], [_col_of(t, h) for t in gbts],
                           [_col_of(t, nh + h) for t in gbts])
        for ci, (rows, L) in enumerate(zip(rows_of, Ls)):
            u_ref[rows, :] = L["u"]
            base = ci * WQK
            wqk_ref[base:base + CHUNK, :] = L["w"]
            wqk_ref[base + CHUNK:base + 2 * CHUNK, :] = L["q_dec"]
            wqk_ref[base + 2 * CHUNK:base + WQK, :] = L["k_dec"]
            attn_ref[ci] = L["attn"]
            t_ref[ci] = L["T"]

    cc = pl.BlockSpec((None, cpb, CHUNK, CHUNK), lambda h, i: (h, i, 0, 0))
    return pl.pallas_call(
        body, grid=(nh, S // rb),
        in_specs=[pl.BlockSpec((rb, HEAD_D), lambda h, i: (i, 3 * h)), pl.BlockSpec((rb, HEAD_D), lambda h, i: (i, 3 * h + 1)),
                  pl.BlockSpec((rb, HEAD_D), lambda h, i: (i, 3 * h + 2)), pl.BlockSpec((rb, LANE), lambda h, i: (i, 0))],
        out_specs=[pl.BlockSpec((rb, HEAD_D), lambda h, i: (i, h)),
                   pl.BlockSpec((None, 3 * rb, HEAD_D), lambda h, i: (h, i, 0)), cc, cc],
        out_shape=[sds((S, nh * HEAD_D), F32), sds((nh, 3 * S, HEAD_D), F32), sds((nh, nc, CHUNK, CHUNK), F32),
                   sds((nh, nc, CHUNK, CHUNK), F32)],
        compiler_params=_params(("parallel", "parallel")), name="gdn_local_fwd")(qkv, qkv, qkv, gb)


def _gdn_scan_fwd(u, wqk, attn, gb, nh, dep):
    S = u.shape[0]
    nc = S // CHUNK
    rb = min(GDN_ROWS, S)
    cpb = rb // CHUNK
    sds = jax.ShapeDtypeStruct

    def body(u_ref, wqk_ref, attn_ref, gb_ref, dep_ref, o_ref, vn_ref, st_ref, s_ref):
        @pl.when(pl.program_id(0) == 0)
        def _():
            s_ref[...] = jnp.zeros_like(s_ref)

        for ci in range(cpb):
            rows = slice(ci * CHUNK, (ci + 1) * CHUNK)
            glv = jnp.exp(jnp.sum(gb_ref[rows, :], axis=0, keepdims=True))
            base = ci * WQK
            heads = range(nh)
            cols = [slice(h * HEAD_D, (h + 1) * HEAD_D) for h in heads]
            states = [s_ref[h] for h in heads]
            rs = [_gdot(wqk_ref[h, base:base + 2 * CHUNK, :], states[h], NN) for h in heads]
            vbs = [u_ref[rows, cols[h]] - rs[h][:CHUNK] for h in heads]
            os_ = [_gdot(attn_ref[h, ci], vbs[h], NN) for h in heads]
            ks_ = [_gdot(wqk_ref[h, base + 2 * CHUNK:base + WQK, :], vbs[h], TN) for h in heads]
            for h in heads:
                st_ref[h, ci] = states[h]
                o_ref[rows, cols[h]] = rs[h][CHUNK:] + os_[h]
                vn_ref[rows, cols[h]] = vbs[h]
                s_ref[h] = states[h] * glv[:, h:h + 1] + ks_[h]

    return pl.pallas_call(
        body, grid=(S // rb,),
        in_specs=[pl.BlockSpec((rb, nh * HEAD_D), lambda i: (i, 0)), pl.BlockSpec((nh, 3 * rb, HEAD_D), lambda i: (0, i, 0)),
                  pl.BlockSpec((nh, cpb, CHUNK, CHUNK), lambda i: (0, i, 0, 0)), pl.BlockSpec((rb, LANE), lambda i: (i, 0)),
                  pl.BlockSpec(memory_space=pl.ANY)],
        out_specs=[pl.BlockSpec((rb, nh * HEAD_D), lambda i: (i, 0)), pl.BlockSpec((rb, nh * HEAD_D), lambda i: (i, 0)),
                   pl.BlockSpec((nh, cpb, HEAD_D, HEAD_D), lambda i: (0, i, 0, 0))],
        out_shape=[sds((S, nh * HEAD_D), F32), sds((S, nh * HEAD_D), F32), sds((nh, nc, HEAD_D, HEAD_D), F32)],
        scratch_shapes=[pltpu.VMEM((nh, HEAD_D, HEAD_D), F32)],
        compiler_params=_params(("arbitrary",)), name="gdn_scan_fwd")(u, wqk, attn, gb, dep)


def _gdn_scan_bwd(wqk, attn, gb, states, vn, do, nh, dep):
    S = vn.shape[0]
    nc = S // CHUNK
    rb = min(GDN_ROWS, S)
    cpb = rb // CHUNK
    last = S // rb - 1
    sds = jax.ShapeDtypeStruct

    def body(wqk_ref, attn_ref, gb_ref, st_ref, vn_ref, do_ref, dep_ref, dvn_ref, dw_ref, dqd_ref, dkd_ref, da_ref, dgl_ref,
             ds_ref):
        @pl.when(pl.program_id(0) == 0)
        def _():
            ds_ref[...] = jnp.zeros_like(ds_ref)

        row = lax.broadcasted_iota(jnp.int32, (CHUNK, CHUNK), 0)
        col = lax.broadcasted_iota(jnp.int32, (CHUNK, CHUNK), 1)
        for ci in reversed(range(cpb)):
            rows = slice(ci * CHUNK, (ci + 1) * CHUNK)
            glv = jnp.exp(jnp.sum(gb_ref[rows, :], axis=0, keepdims=True))
            base = ci * WQK
            heads = range(nh)
            cols = [slice(h * HEAD_D, (h + 1) * HEAD_D) for h in heads]
            states = [st_ref[h, ci] for h in heads]
            dSs = [ds_ref[h] for h in heads]
            vbs = [vn_ref[rows, cols[h]] for h in heads]
            dobs = [do_ref[rows, cols[h]] for h in heads]
            dv1 = [_gdot(attn_ref[h, ci], dobs[h], TN) for h in heads]
            dv2 = [_gdot(wqk_ref[h, base + 2 * CHUNK:base + WQK, :], dSs[h], NN) for h in heads]
            das = [_gdot(dobs[h], vbs[h], NT) for h in heads]
            dkds = [_gdot(vbs[h], dSs[h], NT) for h in heads]
            dvbs = [dv1[h] + dv2[h] for h in heads]
            xs = [_gdot(jnp.concatenate([dobs[h], dvbs[h]], axis=0), states[h], NT) for h in heads]
            dss = [_gdot(wqk_ref[h, base:base + 2 * CHUNK, :], jnp.concatenate([-dvbs[h], dobs[h]], axis=0), TN)
                   for h in heads]
            for h in heads:
                dqd_ref[rows, cols[h]] = xs[h][:CHUNK]
                dw_ref[rows, cols[h]] = -xs[h][CHUNK:]
                dvn_ref[rows, cols[h]] = dvbs[h]
                da_ref[h, ci] = jnp.where(col <= row, das[h], 0.0)
                dkd_ref[rows, cols[h]] = dkds[h]
                gl = glv[:, h:h + 1]
                dgl = jnp.sum(jnp.sum(states[h] * dSs[h], axis=0, keepdims=True), axis=1, keepdims=True)
                dgl_ref[h, ci] = jnp.broadcast_to(dgl * gl, (1, LANE))
                ds_ref[h] = dSs[h] * gl + dss[h]

    rv = lambda i: last - i
    wide = pl.BlockSpec((rb, nh * HEAD_D), lambda i: (rv(i), 0))
    return pl.pallas_call(
        body, grid=(S // rb,),
        in_specs=[pl.BlockSpec((nh, 3 * rb, HEAD_D), lambda i: (0, rv(i), 0)),
                  pl.BlockSpec((nh, cpb, CHUNK, CHUNK), lambda i: (0, rv(i), 0, 0)),
                  pl.BlockSpec((rb, LANE), lambda i: (rv(i), 0)),
                  pl.BlockSpec((nh, cpb, HEAD_D, HEAD_D), lambda i: (0, rv(i), 0, 0)), wide, wide,
                  pl.BlockSpec(memory_space=pl.ANY)],
        out_specs=[wide, wide, wide, wide, pl.BlockSpec((nh, cpb, CHUNK, CHUNK), lambda i: (0, rv(i), 0, 0)),
                   pl.BlockSpec((nh, cpb, 1, LANE), lambda i: (0, rv(i), 0, 0))],
        out_shape=[sds((S, nh * HEAD_D), F32), sds((S, nh * HEAD_D), F32), sds((S, nh * HEAD_D), F32),
                   sds((S, nh * HEAD_D), F32), sds((nh, nc, CHUNK, CHUNK), F32), sds((nh, nc, 1, LANE), F32)],
        scratch_shapes=[pltpu.VMEM((nh, HEAD_D, HEAD_D), F32)],
        compiler_params=_params(("arbitrary",)), name="gdn_scan_bwd")(wqk, attn, gb, states, vn, do, dep)


def _gdn_local_bwd(qkv, gb, T, u, wqk, dvn, dw, dqd, dkd, dattn, dgl, nh):
    S = qkv.shape[0]
    rb = min(GDN_LOCAL_ROWS, S)
    cpb = rb // CHUNK
    sds = jax.ShapeDtypeStruct

    def body(q_ref, k_ref, v_ref, gb_ref, t_ref, u_ref, wqk_ref, dvn_ref, dw_ref, dqd_ref, dkd_ref, da_ref, dgl_ref,
             dqkv_ref, dg_ref, db_ref):
        h = pl.program_id(0)
        n = range(cpb)
        rows_of = [slice(ci * CHUNK, (ci + 1) * CHUNK) for ci in n]
        qs, ks, vs = ([r[rows, :] for rows in rows_of] for r in (q_ref, k_ref, v_ref))
        gbts = [gb_ref[rows, :] for rows in rows_of]
        bcols = [_col_of(t, nh + h) for t in gbts]
        Ts = [t_ref[ci] for ci in n]
        Ls = _chunks_local(qs, ks, vs, [_col_of(t, h) for t in gbts], bcols, Ts=Ts)
        drvs = [_gdot(Ts[ci], dvn_ref[rows_of[ci], :], TN) for ci in n]
        drks = [_gdot(Ts[ci], dw_ref[rows_of[ci], :], TN) for ci in n]
        dAs = [jnp.where(Ls[ci]["strict"], -(_gdot(drvs[ci], u_ref[rows_of[ci], :], NT)
                                             + _gdot(drks[ci], wqk_ref[ci * WQK:ci * WQK + CHUNK, :], NT)), 0.0) for ci in n]
        dMs = [dAs[ci] * Ls[ci]["decay"] for ci in n]
        dNs = [da_ref[ci] * Ls[ci]["decay"] for ci in n]
        dkbs = [_gdot(dMs[ci], ks[ci], NN) for ci in n]
        dq1 = [_gdot(dNs[ci], ks[ci], NN) for ci in n]
        dk1 = [_gdot(dMs[ci], Ls[ci]["kb"], TN) for ci in n]
        dk2 = [_gdot(dNs[ci], qs[ci], TN) for ci in n]
        for ci in n:
            rows, L, q, k, v, bcol = rows_of[ci], Ls[ci], qs[ci], ks[ci], vs[ci], bcols[ci]
            eye, eG, eGr = L["eye"], L["eG"], L["eGr"]
            drv, drk, dkb = drvs[ci], drks[ci], dkbs[ci]
            dq_dec, dk_dec, dattn_c = dqd_ref[rows, :], dkd_ref[rows, :], da_ref[ci]
            dqkv_ref[rows, :HEAD_D] = dq1[ci] + dq_dec * eG
            dqkv_ref[rows, HEAD_D:2 * HEAD_D] = drk * (bcol * eG) + dk1[ci] + dkb * bcol + dk2[ci] + dk_dec * eGr
            dqkv_ref[rows, 2 * HEAD_D:] = drv * bcol
            db_ref[rows, :] = (jnp.sum(drv * v, axis=1, keepdims=True) + jnp.sum(drk * k, axis=1, keepdims=True) * eG
                               + jnp.sum(dkb * k, axis=1, keepdims=True))
            E = dAs[ci] * L["A"] + dattn_c * L["attn"]
            kd = jnp.sum(dk_dec * L["k_dec"], axis=1, keepdims=True)
            dG = (jnp.sum(dq_dec * L["q_dec"], axis=1, keepdims=True) - kd
                  + jnp.sum(drk * L["rhs_k"], axis=1, keepdims=True)
                  + jnp.sum(E, axis=1, keepdims=True) - _to_col(jnp.sum(E, axis=0, keepdims=True), eye))
            d_last = jnp.sum(kd, axis=0, keepdims=True) + dgl_ref[ci][:, :1]
            dG = dG + jnp.where(L["row"][:, :1] == CHUNK - 1, d_last, 0.0)
            dg_ref[rows, :] = jnp.sum(jnp.where(L["col"] >= L["row"], _to_row(dG, eye), 0.0), axis=1, keepdims=True)

    hd = pl.BlockSpec((rb, HEAD_D), lambda h, i: (i, h))
    cc = pl.BlockSpec((None, cpb, CHUNK, CHUNK), lambda h, i: (h, i, 0, 0))
    col1 = pl.BlockSpec((None, rb, 1), lambda h, i: (h, i, 0))
    return pl.pallas_call(
        body, grid=(nh, S // rb),
        in_specs=[pl.BlockSpec((rb, HEAD_D), lambda h, i: (i, 3 * h)), pl.BlockSpec((rb, HEAD_D), lambda h, i: (i, 3 * h + 1)),
                  pl.BlockSpec((rb, HEAD_D), lambda h, i: (i, 3 * h + 2)), pl.BlockSpec((rb, LANE), lambda h, i: (i, 0)),
                  cc, hd, pl.BlockSpec((None, 3 * rb, HEAD_D), lambda h, i: (h, i, 0)), hd, hd, hd, hd, cc,
                  pl.BlockSpec((None, cpb, 1, LANE), lambda h, i: (h, i, 0, 0))],
        out_specs=[pl.BlockSpec((rb, 3 * HEAD_D), lambda h, i: (i, h)), col1, col1],
        out_shape=[sds((S, 3 * nh * HEAD_D), F32)] + [sds((nh, S, 1), F32)] * 2,
        compiler_params=_params(("parallel", "parallel")), name="gdn_local_bwd")(
            qkv, qkv, qkv, gb, T, u, wqk, dvn, dw, dqd, dkd, dattn, dgl)


def _gated_norm_fwd(o, proj, norm_w, nh, z_blk0, mix, m_blk0):
    S = o.shape[0]

    def body(o_ref, z_ref, w_ref, mix_ref, y_ref):
        ov, z = o_ref[...], z_ref[...]
        r = lax.rsqrt(jnp.mean(ov * ov, axis=1, keepdims=True) + RMS_EPS)
        y_ref[...] = ov * r * w_ref[...] * (z * _sigmoid(z))

    return pl.pallas_call(
        body, grid=(nh,),
        in_specs=[pl.BlockSpec((S, HEAD_D), lambda h: (0, h)), pl.BlockSpec((S, HEAD_D), lambda h: (0, z_blk0 + h)),
                  pl.BlockSpec((1, HEAD_D), lambda h: (0, 0)), pl.BlockSpec(memory_space=pl.ANY)],
        out_specs=pl.BlockSpec((S, HEAD_D), lambda h: (0, m_blk0 + h)),
        out_shape=jax.ShapeDtypeStruct(mix.shape, F32), input_output_aliases={3: 0},
        compiler_params=_params(("parallel",)), name="gated_norm_fwd")(o, proj, norm_w, mix)


def _gated_norm_bwd(o, proj, norm_w, dmix, nh, z_blk0, d_blk0):
    S = o.shape[0]
    sds = jax.ShapeDtypeStruct

    def body(o_ref, z_ref, w_ref, dy_ref, do_ref, dz_ref, dw_ref):
        ov, z, w, dy = o_ref[...], z_ref[...], w_ref[...], dy_ref[...]
        r = lax.rsqrt(jnp.mean(ov * ov, axis=1, keepdims=True) + RMS_EPS)
        oh = ov * r
        sg = _sigmoid(z)
        dz_ref[...] = dy * (oh * w) * (sg * (1.0 + z * (1.0 - sg)))
        don = dy * (z * sg)
        @pl.when(pl.program_id(0) == 0)
        def _():
            dw_ref[...] = jnp.zeros_like(dw_ref)

        dw_ref[...] += jnp.sum(don * oh, axis=0, keepdims=True)
        doh = don * w
        do_ref[...] = r * (doh - oh * jnp.mean(doh * oh, axis=1, keepdims=True))

    return pl.pallas_call(
        body, grid=(nh,),
        in_specs=[pl.BlockSpec((S, HEAD_D), lambda h: (0, h)), pl.BlockSpec((S, HEAD_D), lambda h: (0, z_blk0 + h)),
                  pl.BlockSpec((1, HEAD_D), lambda h: (0, 0)), pl.BlockSpec((S, HEAD_D), lambda h: (0, d_blk0 + h))],
        out_specs=[pl.BlockSpec((S, HEAD_D), lambda h: (0, h)), pl.BlockSpec((S, HEAD_D), lambda h: (0, h)),
                   pl.BlockSpec((1, HEAD_D), lambda h: (0, 0))],
        out_shape=[sds((S, nh * HEAD_D), F32), sds((S, nh * HEAD_D), F32), sds((1, HEAD_D), F32)],
        compiler_params=_params(("arbitrary",)), name="gated_norm_bwd")(o, proj, norm_w, dmix)


def _adamw_math(w, g, m, v):
    m = ADAM_B1 * m + (1.0 - ADAM_B1) * g
    v = ADAM_B2 * v + (1.0 - ADAM_B2) * (g * g)
    m_hat = m / (1.0 - ADAM_B1 ** ADAM_STEP)
    v_hat = v / (1.0 - ADAM_B2 ** ADAM_STEP)
    delta = -ADAM_LR * (m_hat / (jnp.sqrt(v_hat) + ADAM_EPS) + ADAM_WD * w)
    return delta, m, v


def _slab_tiles(R, C, rows=256, cols=256):
    if R % rows == 0:
        return (rows, C), R // rows, lambda i: (i, 0)
    tc = _tile(C, cols)
    return (R, tc), C // tc, lambda i: (0, i)


def _adamw_big(parts, terms, chip, w, m, v, name):
    R, C = w.shape
    blk, steps, at = _slab_tiles(R, C)
    sds = jax.ShapeDtypeStruct

    def body(q_ref, p_ref, t_ref, w_ref, m_ref, v_ref, g_ref, d_ref, nm_ref, nv_ref):
        g = ((p_ref[...].astype(F32) + t_ref[0].astype(F32)) + t_ref[1].astype(F32)) + t_ref[2].astype(F32)
        g_ref[...] = g
        d_ref[...], nm_ref[...], nv_ref[...] = _adamw_math(w_ref[...], g, m_ref[...], v_ref[...])

    spec = pl.BlockSpec(blk, lambda i, q_ref: at(i))
    grid_spec = pltpu.PrefetchScalarGridSpec(
        num_scalar_prefetch=1, grid=(steps,),
        in_specs=[pl.BlockSpec((None,) + blk, lambda i, q_ref: (q_ref[0],) + at(i)),
                  pl.BlockSpec((3,) + blk, lambda i, q_ref: (0,) + at(i)), spec, spec, spec],
        out_specs=[spec] * 4)
    return pl.pallas_call(body, grid_spec=grid_spec, out_shape=[sds((R, C), F32)] * 4,
                          compiler_params=_params(("parallel",)), name=name)(chip, parts, terms, w, m, v)


def _adamw_small(ws, gs, ms, vs):
    n = len(ws)

    def body(*refs):
        for i in range(n):
            w, g, m, v = (refs[k * n + i][...] for k in range(4))
            d, nm, nv = _adamw_math(w, g, m, v)
            refs[4 * n + i][...] = d
            refs[5 * n + i][...] = nm
            refs[6 * n + i][...] = nv

    shapes = [jax.ShapeDtypeStruct(w.shape, F32) for w in ws]
    vm = pl.BlockSpec(memory_space=pltpu.VMEM)
    outs = pl.pallas_call(body, in_specs=[vm] * (4 * n), out_specs=[vm] * (3 * n), out_shape=shapes * 3,
                          name="adamw_small")(*ws, *gs, *ms, *vs)
    return outs[:n], outs[n:2 * n], outs[2 * n:]


MESH = pl.DeviceIdType.MESH
ANY = pl.BlockSpec(memory_space=pl.ANY)


def _place():
    x, y, c = lax.axis_index("x"), lax.axis_index("y"), lax.axis_index("c")
    return x, y, c, [(1 - x, y), (x, 1 - y), (1 - x, 1 - y)]


def _chip_sum(grad, recv, core, name):
    _, R, C = grad.shape
    blk, steps, at = _slab_tiles(R, C)

    def body(c_ref, g_ref, r_ref, o_ref):
        o_ref[...] = (g_ref[...].astype(F32) + r_ref[...].astype(F32)).astype(o_ref.dtype)

    grid_spec = pltpu.PrefetchScalarGridSpec(
        num_scalar_prefetch=1, grid=(4, steps),
        in_specs=[pl.BlockSpec((None,) + blk, lambda q, i, c_ref: (2 * q + c_ref[0],) + at(i)),
                  pl.BlockSpec((None,) + blk, lambda q, i, c_ref: (q,) + at(i))],
        out_specs=pl.BlockSpec((None,) + blk, lambda q, i, c_ref: (q,) + at(i)))
    return pl.pallas_call(body, grid_spec=grid_spec, out_shape=jax.ShapeDtypeStruct((4, R, C), BF16),
                          compiler_params=_params(("parallel", "parallel")), name=name)(core, grad, recv)


HBM_SPEC = pl.BlockSpec(memory_space=pltpu.HBM)
SEM_SPEC = pl.BlockSpec(memory_space=pltpu.SEMAPHORE)
DATAFLOW = pltpu.SideEffectType.DATAFLOW_SIDE_EFFECTING


def _split_start(name, bufs, plan, counts, after=None):
    nb, ng = len(bufs), len(counts)
    extra = [] if after is None else [after]
    place = [(g, k) for g, cnt in enumerate(counts) for k in range(cnt)]

    def body(*refs):
        sems, token = refs[nb + len(extra):nb + len(extra) + 2 * ng], refs[-1]
        for (g, k), (src, dst, to) in zip(place, plan(refs[:nb])):
            pltpu.make_async_remote_copy(src_ref=src, dst_ref=dst, send_sem=sems[2 * g].at[k], recv_sem=sems[2 * g + 1].at[k],
                                         device_id=to, device_id_type=MESH).start()
        token[...] = jnp.zeros_like(token)

    outs = pl.pallas_call(
        body, name=name,
        out_shape=(*[pltpu.SemaphoreType.DMA((cnt,)) for cnt in counts for _ in range(2)],
                   *[pltpu.HBM(b.shape, b.dtype) for b in bufs], jax.ShapeDtypeStruct((8, LANE), F32)),
        in_specs=[HBM_SPEC] * nb + [ANY] * len(extra),
        out_specs=(*[SEM_SPEC] * (2 * ng), *[HBM_SPEC] * nb, pl.BlockSpec(memory_space=pltpu.VMEM)),
        input_output_aliases={i: 2 * ng + i for i in range(nb)},
        compiler_params=pltpu.CompilerParams(has_side_effects=DATAFLOW))(
            *[pltpu.with_memory_space_constraint(b, pltpu.HBM) for b in bufs], *extra)
    return [(outs[2 * g], outs[2 * g + 1]) for g in range(ng)], list(outs[2 * ng:2 * ng + nb]), outs[-1]


def _split_wait(name, sems, bufs, plan, after):
    nb = len(bufs)
    send_sems, recv_sems = sems

    def body(*refs):
        send_s, recv_s = refs[nb], refs[nb + 1]
        for k, (src, dst, to) in enumerate(plan(refs[:nb])):
            cp = pltpu.make_async_remote_copy(src_ref=src, dst_ref=dst, send_sem=send_s.at[k], recv_sem=recv_s.at[k],
                                              device_id=to, device_id_type=MESH)
            cp.wait_send()
            cp.wait_recv()

    outs = pl.pallas_call(
        body, name=name, out_shape=tuple(pltpu.HBM(b.shape, b.dtype) for b in bufs),
        in_specs=[HBM_SPEC] * nb + [SEM_SPEC, SEM_SPEC, ANY], out_specs=tuple([HBM_SPEC] * nb),
        input_output_aliases={i: i for i in range(nb)},
        compiler_params=pltpu.CompilerParams(has_side_effects=DATAFLOW))(*bufs, send_sems, recv_sems, after)
    return list(outs)


def _slot(px, py, pc):
    return 4 * px + 2 * py + pc


class _Gather:
    def __init__(self, shards, groups, dev):
        self.shards, self.groups, self.dev = shards, groups, dev
        self.second = {}

    @staticmethod
    def _plan1(pairs, refs):
        x, y, c, chips = _place()
        out = []
        for s, l in pairs:
            dst = refs[l].at[_slot(x, y, c)]
            out.append((refs[s], dst, (x, y, 1 - c)))
            out += [(refs[s], dst, (px, py, c)) for px, py in chips]
        return out

    @staticmethod
    def _plan2(refs):
        x, y, c, chips = _place()
        return [(r.at[_slot(px, py, c)],) * 2 + ((x, y, 1 - c),) for r in refs for px, py in chips]

    def start(self):
        n = len(self.shards)
        lands = [lax.dynamic_update_slice(lax.empty((N_DEV,) + s.shape, s.dtype), s[None], (self.dev, 0, 0))
                 for s in self.shards]
        pairs = [(w, n + w) for g in self.groups for w in g]
        sems, bufs, token = _split_start("gather_start_1", list(self.shards) + lands, functools.partial(self._plan1, pairs),
                                         tuple(4 * len(g) for g in self.groups))
        self.first = [(sems[i], [bufs[w] for w in g], [bufs[n + w] for w in g]) for i, g in enumerate(self.groups)]
        return token

    def mid(self, gi, after):
        sems, srcs, lands = self.first[gi]
        m = len(srcs)
        plan = functools.partial(self._plan1, [(w, m + w) for w in range(m)])
        lands = _split_wait("gather_%d_wait_1" % gi, sems, srcs + lands, plan, after)[m:]
        sems, lands, token = _split_start("gather_%d_start_2" % gi, lands, self._plan2, (3 * m,))
        self.second[gi] = (sems[0], lands)
        return token

    def finish(self, gi, after):
        sems, lands = self.second[gi]
        return _split_wait("gather_%d_wait_2" % gi, sems, lands, self._plan2, after)


class _Exchanges:
    def __init__(self, tag, n, core, gather=None):
        self.tag, self.n, self.core, self.gather = tag, n, core, gather

    def weights_mid(self, after):
        return self.gather.mid(1, after)

    def weights_finish(self, after):
        return self.gather.finish(1, after)

    def _reduce_plan1(self, refs):
        n = self.n
        x, y, c, _ = _place()
        return [(refs[w].at[2 * q + (1 - c)], refs[n + w].at[q], (x, y, 1 - c)) for w in range(n) for q in range(4)]

    def _reduce_plan2(self, refs):
        n = self.n
        x, y, c, chips = _place()
        return [(refs[w].at[2 * px + py], refs[n + w].at[j], (px, py, c))
                for w in range(n) for j, (px, py) in enumerate(chips)]

    def grads_start(self, grads):
        lands = [lax.empty((4,) + g.shape[1:], g.dtype) for g in grads]
        self.g1 = _split_start(self.tag + "reduce_start_1", list(grads) + lands, self._reduce_plan1, (4 * self.n,))
        return self.g1[2]

    def grads_mid(self, after):
        n = self.n
        sems, bufs, _ = self.g1
        bufs = _split_wait(self.tag + "reduce_wait_1", sems[0], bufs, self._reduce_plan1, after)
        core = self.core.reshape(1).astype(jnp.int32)
        self.parts = [_chip_sum(bufs[w], bufs[n + w], core, self.tag + "reduce_chip_sum_%d" % w) for w in range(n)]
        lands = [lax.empty((3,) + p.shape[1:], p.dtype) for p in self.parts]
        self.g2 = _split_start(self.tag + "reduce_start_2", self.parts + lands, self._reduce_plan2, (3 * n,))
        return self.g2[2]

    def grads_finish(self, after):
        n = self.n
        sems, bufs, _ = self.g2
        bufs = _split_wait(self.tag + "reduce_wait_2", sems[0], bufs, self._reduce_plan2, after)
        self.parts, self.terms = bufs[:n], bufs[n:]


def _all_reduce_small(buf):
    R = buf.shape[0]

    def body(x_ref, o_ref, g_ref, send_sems, recv_sems):
        x, y, c, chips = _place()
        me, sibling = (x, y, c), (x, y, 1 - c)

        def slot(px, py, pc):
            return 4 * px + 2 * py + pc

        def copy(k, block, to, src=None):
            dst = g_ref.at[slot(*block)]
            return pltpu.make_async_remote_copy(src_ref=dst if src is None else src, dst_ref=dst,
                                                send_sem=send_sems.at[k], recv_sem=recv_sems.at[k],
                                                device_id=to, device_id_type=MESH)

        first = [copy(0, me, sibling, src=x_ref)]
        first += [copy(1 + j, me, (*chip, c), src=x_ref) for j, chip in enumerate(chips)]
        for cp in first:
            cp.start()
        g_ref[slot(*me)] = x_ref[...]
        passed = [copy(4 + j, (*chip, c), sibling) for j, chip in enumerate(chips)]
        for j, chip in enumerate(chips):
            copy(1 + j, (*chip, c), me).wait_recv()
            passed[j].start()
        copy(0, sibling, me).wait_recv()
        for j, chip in enumerate(chips):
            copy(4 + j, (*chip, 1 - c), me).wait_recv()
        for cp in first + passed:
            cp.wait_send()
        acc = g_ref[0]
        for s in range(1, N_DEV):
            acc = acc + g_ref[s]
        o_ref[...] = acc

    vm = pl.BlockSpec(memory_space=pltpu.VMEM)
    return pl.pallas_call(
        body, in_specs=[vm], out_specs=vm, out_shape=jax.ShapeDtypeStruct((R, LANE), F32),
        scratch_shapes=[pltpu.VMEM((N_DEV, R, LANE), F32), pltpu.SemaphoreType.DMA((7,)), pltpu.SemaphoreType.DMA((7,))],
        name="all_reduce_small")(buf)


def _pad_cols(a, width):
    return jnp.pad(a, ((0, 0), (0, width - a.shape[1])))


def _local_step(x, target, w_in_t, conv_w, a_log, dt_bias, delta_norm_w, sinks, rel_bias, ln1_g, ln1_b, ln2_g, ln2_b,
                ex, ex_in):
    S, D = x.shape
    aw = D // 2
    hq, hkv, nh = aw // HEAD_A, aw // HEAD_A // GQA, aw // HEAD_D
    kvw = hkv * HEAD_A
    c_q, c_k, c_v, c_d = 0, aw, aw + kvw, aw + 2 * kvw
    c_ab = c_d + 3 * aw
    c_z = c_ab + 2 * nh
    n_in = c_z + aw
    assert w_in_t.shape == (n_in, D), (w_in_t.shape, n_in)
    w_pt = jnp.concatenate([w_in_t[:c_ab], w_in_t[c_z:], jnp.pad(w_in_t[c_ab:c_z], ((0, LANE - 2 * nh), (0, 0)))], axis=0)
    p_z, p_ab = c_ab, c_ab + aw
    n_p = p_ab + LANE

    proj = _matmul(x, w_pt, NT, name="proj", tn=1152)
    bias = _attn_bias(rel_bias.T)
    attn_out = _attn_fwd(proj, bias, sinks.reshape(-1), hq, 0, c_k // kvw, c_v // kvw, D)
    conv2 = conv_w.reshape(CONV_W, 3 * aw)
    qkv = _gdn_prep_fwd(proj, conv2, nh, c_d // HEAD_D)
    ab = proj[:, p_ab:]
    al, dt = _pad_cols(a_log, LANE), _pad_cols(dt_bias, LANE)
    gb = _gates_fwd(ab, al, dt, nh)
    u_d, wqk, attn_d, t_d = _gdn_local_fwd(qkv, gb, nh)
    o_d, vn, states = _gdn_scan_fwd(u_d, wqk, attn_d, gb, nh, ex.weights_mid(u_d))
    mix = _gated_norm_fwd(o_d, proj, delta_norm_w, nh, p_z // HEAD_D, attn_out, aw // HEAD_D)
    w_o_g, w_up_g, w_down_g = ex.weights_finish(mix)
    w_o, w_down = w_o_g.reshape(D, D), w_down_g.reshape(-1, D)
    mixed = _matmul(mix, w_o, NN, name="out_proj")
    h1 = _ln1_fwd(x, mixed, ln1_g, ln1_b)
    u = _matmul(h1, w_up_g, NN, name="mlp_up", b_groups=True)
    mlp = _matmul(u, w_down, NN, name="mlp_down", a_fn=_relu_sq)
    dr2, loss_row, dln2_g, dln2_b = _ln2_loss(h1, mlp, ln2_g, ln2_b, target)

    du = _matmul(dr2, w_down, NT, name="d_mlp_act", epi=_relu_sq_grad, epi_in=(u,))
    dw_down = _matmul(u, dr2, TN, name="dw_down", a_fn=_relu_sq, out_dtype=BF16)
    dw_up = _matmul(h1, du, TN, name="dw_up", out_dtype=BF16, out_groups=N_DEV)
    dh_mlp = _matmul(du, w_up_g, NT, name="d_h1", b_groups=True)
    dr1, dln1_g, dln1_b = _ln1_bwd(x, mixed, ln1_g, dr2, dh_mlp)
    dw_o = _matmul(mix, dr1, TN, name="dw_o", out_dtype=BF16)
    tok = ex.grads_start([dw_o.reshape(N_DEV, -1, D), dw_up, dw_down.reshape(N_DEV, -1, D)])
    dmix = _matmul(dr1, w_o, NT, name="d_mix", deps=(tok,))
    dq_a, dk_a, dv_a, dbias, dsink = _attn_bwd(proj, bias, sinks.reshape(-1), mix, dmix, hq, 0, c_k // kvw, c_v // kvw)
    drel = _rel_bias_grad(dbias, hq)
    do_d, dz, dnw = _gated_norm_bwd(o_d, proj, delta_norm_w, dmix, nh, p_z // HEAD_D, aw // HEAD_D)
    dvn_s, dw_s, dqd, dkd, dattn_d, dgl = _gdn_scan_bwd(wqk, attn_d, gb, states, vn, do_d, nh, ex.grads_mid(dq_a))
    dqkv_n, dg, dbeta = _gdn_local_bwd(qkv, gb, t_d, u_d, wqk, dvn_s, dw_s, dqd, dkd, dattn_d, dgl, nh)
    dgb = _pad_cols(jnp.concatenate([dg.reshape(nh, S).T, dbeta.reshape(nh, S).T], axis=1), LANE)
    dab, da_log, ddt_bias = _gates_bwd(ab, al, dt, dgb, nh)
    dqkv_d, dconv = _gdn_prep_bwd(proj, conv2, dqkv_n, nh, c_d // HEAD_D)
    dproj = jnp.concatenate([dq_a, dk_a, dv_a, dqkv_d, dz, dab], axis=1)
    dw_pt = _matmul(dproj, x, TN, name="dw_in", out_dtype=BF16, tm=1152)
    dw_in_t = jnp.concatenate([dw_pt[:p_z], dw_pt[p_ab:p_ab + 2 * nh], dw_pt[p_z:p_ab]], axis=0)
    tok = ex_in.grads_start([dw_in_t.reshape(N_DEV, -1, D)])
    dx_proj = _matmul(dproj, w_pt, NN, name="d_x", tk=1920, deps=(tok,))
    grad_x = _grad_x(dr1, dx_proj, ex_in.grads_mid(dx_proj))
    ex.grads_finish(grad_x)

    small = dict(conv_w=dconv, a_log=da_log[:, :nh], dt_bias=ddt_bias[:, :nh], delta_norm_w=dnw,
                 attn_sinks=dsink[:, 0].reshape(1, hq), rel_bias=drel[:, :, 0].T,
                 ln1_g=dln1_g, ln1_b=dln1_b, ln2_g=dln2_g, ln2_b=dln2_b)
    return loss_row, grad_x, small


SMALL_ORDER = ("conv_w", "a_log", "dt_bias", "delta_norm_w", "attn_sinks", "rel_bias", "ln1_g", "ln1_b", "ln2_g", "ln2_b")


def _pack_small(loss_row, small):
    parts = [loss_row.reshape(-1)]
    for k in SMALL_ORDER:
        flat = small[k].reshape(-1)
        parts.append(jnp.pad(flat, (0, (-flat.shape[0]) % LANE)))
    flat = jnp.concatenate(parts)
    flat = jnp.pad(flat, (0, (-flat.shape[0]) % (8 * LANE)))
    return flat.reshape(-1, LANE)


def _unpack_small(buf, small_shapes):
    flat = buf.reshape(-1)
    loss = flat[0]
    off = LANE
    out = {}
    for k in SMALL_ORDER:
        n = int(np.prod(small_shapes[k]))
        out[k] = flat[off:off + n].reshape(small_shapes[k])
        off += n + (-n) % LANE
    return loss, out


def kernel(x, w_in, conv_w, a_log, dt_bias, delta_norm_w, attn_sinks, rel_bias, w_o, ln1_g, ln1_b, w_up, w_down, ln2_g, ln2_b, loss_target, m_w_in, m_conv_w, m_a_log, m_dt_bias, m_delta_norm_w, m_attn_sinks, m_rel_bias, m_w_o, m_ln1_g, m_ln1_b, m_w_up, m_w_down, m_ln2_g, m_ln2_b, v_w_in, v_conv_w, v_a_log, v_dt_bias, v_delta_norm_w, v_attn_sinks, v_rel_bias, v_w_o, v_ln1_g, v_ln1_b, v_w_up, v_w_down, v_ln2_g, v_ln2_b):
    S, D = x.shape[1], x.shape[2]
    core = lax.axis_index("c")
    dev = 4 * lax.axis_index("x") + 2 * lax.axis_index("y") + core

    gather = _Gather([w_in[0].T.astype(BF16), w_o[0].astype(BF16), w_up[0].astype(BF16), w_down[0].astype(BF16)],
                     [[0], [1, 2, 3]], dev)
    (w_in_g,) = gather.finish(0, gather.mid(0, gather.start()))
    w_in_t = w_in_g.reshape(-1, D)
    ex = _Exchanges("", 3, core, gather)
    ex_in = _Exchanges("in_", 1, core)

    cw_sh = conv_w.shape[3]
    conv_place = lax.dynamic_update_slice(jnp.zeros((CONV_W, N_DEV * cw_sh), F32), conv_w[0, :, 0, :], (0, dev * cw_sh))
    conv_full = _all_reduce_small(jnp.pad(conv_place.reshape(-1, LANE), ((0, (-conv_place.size // LANE) % 8), (0, 0))))
    conv_full = conv_full[:conv_place.size // LANE].reshape(CONV_W, N_DEV * cw_sh)

    loss_row, grad_x, small = _local_step(
        x[0], loss_target[0], w_in_t, conv_full, a_log, dt_bias, delta_norm_w, attn_sinks, rel_bias,
        ln1_g, ln1_b, ln2_g, ln2_b, ex, ex_in)

    chip_arr = (dev // 2).reshape(1).astype(jnp.int32)
    big = {}
    for i, (name, w, m, v) in enumerate((("w_o", w_o, m_w_o, v_w_o), ("w_up", w_up, m_w_up, v_w_up),
                                         ("w_down", w_down, m_w_down, v_w_down))):
        big[name] = [o[None] for o in _adamw_big(ex.parts[i], ex.terms[i], chip_arr, w[0], m[0], v[0], "adamw_" + name)]
    ex_in.grads_finish(big["w_down"][0])
    outs = _adamw_big(ex_in.parts[0], ex_in.terms[0], chip_arr, w_in[0].T, m_w_in[0].T, v_w_in[0].T, "adamw_w_in")
    big["w_in"] = [o.T[None] for o in outs]

    small_shapes = {k: v.shape for k, v in small.items()}
    loss, small = _unpack_small(_all_reduce_small(_pack_small(loss_row, small)), small_shapes)
    small["conv_w"] = lax.dynamic_slice(small["conv_w"], (0, dev * cw_sh), (CONV_W, cw_sh))
    small["rel_bias"] = small["rel_bias"].reshape(rel_bias.shape)
    p2 = dict(conv_w=(conv_w, m_conv_w, v_conv_w), a_log=(a_log, m_a_log, v_a_log), dt_bias=(dt_bias, m_dt_bias, v_dt_bias),
              delta_norm_w=(delta_norm_w, m_delta_norm_w, v_delta_norm_w), attn_sinks=(attn_sinks, m_attn_sinks, v_attn_sinks),
              rel_bias=(rel_bias, m_rel_bias, v_rel_bias), ln1_g=(ln1_g, m_ln1_g, v_ln1_g), ln1_b=(ln1_b, m_ln1_b, v_ln1_b),
              ln2_g=(ln2_g, m_ln2_g, v_ln2_g), ln2_b=(ln2_b, m_ln2_b, v_ln2_b))
    two_d = lambda a: a.reshape(-1, a.shape[-1])
    ws = [two_d(p2[k][0]) for k in SMALL_ORDER]
    gs = [two_d(small[k]) for k in SMALL_ORDER]
    ms = [two_d(p2[k][1]) for k in SMALL_ORDER]
    vs = [two_d(p2[k][2]) for k in SMALL_ORDER]
    ds, nms, nvs = _adamw_small(ws, gs, ms, vs)
    res = {}
    for i, k in enumerate(SMALL_ORDER):
        shp = p2[k][0].shape
        res[k] = [gs[i].reshape(shp), ds[i].reshape(shp), nms[i].reshape(shp), nvs[i].reshape(shp)]
    res.update(big)
    order = ("w_in", "conv_w", "a_log", "dt_bias", "delta_norm_w", "attn_sinks", "rel_bias", "w_o", "ln1_g", "ln1_b",
             "w_up", "w_down", "ln2_g", "ln2_b")
    return (loss, grad_x[None], *[res[k][0] for k in order], *[res[k][1] for k in order],
            *[res[k][2] for k in order], *[res[k][3] for k in order])
```

```python
import functools
import math

import numpy as np
import jax
import jax.numpy as jnp
from jax import lax
from jax.experimental import pallas as pl
from jax.experimental.pallas import tpu as pltpu

F32 = jnp.float32
BF16 = jnp.bfloat16
HIGHEST = lax.Precision.HIGHEST

N_DEV = 8
HEAD_A = 64
GQA = 4
BLK = 128
N_BUCKETS = 32
MAX_DISTANCE = 128
HEAD_D = 128
CONV_W = 4
CHUNK = 64
NEG_INF = -1e30
LN_EPS = 1e-5
RMS_EPS = 1e-6
DN_ALPHA = 2.0 ** 0.25
ADAM_LR, ADAM_B1, ADAM_B2, ADAM_EPS, ADAM_WD, ADAM_STEP = 0.001, 0.9, 0.999, 1e-08, 0.01, 10

LANE = 128
VMEM_LIMIT = 56 * 1024 * 1024

NN = ((1,), (0,))
NT = ((1,), (1,))
TN = ((0,), (0,))


def _dot(a, b, dims, prec=None):
    return lax.dot_general(a, b, (dims, ((), ())), precision=prec, preferred_element_type=F32)


def _tile(dim, pref):
    if dim <= pref:
        return dim
    t = (pref // LANE) * LANE
    while t > LANE and dim % t:
        t -= LANE
    assert dim % t == 0, (dim, pref)
    return t


def _params(sem):
    return pltpu.CompilerParams(dimension_semantics=sem, vmem_limit_bytes=VMEM_LIMIT)


def _matmul(a, b, dims, *, name, out_dtype=F32, tm=1024, tn=1024, tk=2048, a_fn=None, epi=None, epi_in=(),
            b_groups=None, out_groups=None, deps=()):
    (ca,), (cb,) = dims
    M, K = a.shape[1 - ca], a.shape[ca]
    if b_groups:
        G, R, C = b.shape
        bshape = (R, G * C)
    else:
        bshape = b.shape
    N = bshape[1 - cb]
    assert bshape[cb] == K, (a.shape, b.shape, dims)
    tm, tk = _tile(M, tm), _tile(K, tk)
    if b_groups:
        lim = C if cb == 0 else tn
        tn = _tile(N, min(tn, lim))
        if cb == 1:
            tk = _tile(K, min(tk, C))
    else:
        tn = _tile(N, tn)
    if out_groups:
        tn = _tile(N, min(tn, N // out_groups))
    nk = K // tk

    def body(*refs):
        a_ref, b_ref = refs[0], refs[1]
        e_refs = refs[2:2 + len(epi_in)]
        o_ref = refs[2 + len(epi_in) + len(deps)]
        acc_ref = refs[3 + len(epi_in) + len(deps)] if nk > 1 else None
        k = pl.program_id(2)
        av = a_ref[...]
        if a_fn is not None:
            av = a_fn(av)
        prod = _dot(av.astype(BF16), b_ref[...].astype(BF16), dims)

        def finish(r):
            if epi is not None:
                r = epi(r, *[e[...] for e in e_refs])
            o_ref[...] = r.astype(out_dtype)

        if nk == 1:
            finish(prod)
            return

        @pl.when(k == 0)
        def _():
            acc_ref[...] = prod

        @pl.when(k > 0)
        def _():
            acc_ref[...] += prod

        @pl.when(k == nk - 1)
        def _():
            finish(acc_ref[...])

    a_spec = (pl.BlockSpec((tm, tk), lambda i, j, k: (i, k)) if ca == 1
              else pl.BlockSpec((tk, tm), lambda i, j, k: (k, i)))
    if b_groups:
        if cb == 0:
            per = C // tn
            b_spec = pl.BlockSpec((None, tk, tn), lambda i, j, k: (j // per, k, j % per))
        else:
            per = C // tk
            b_spec = pl.BlockSpec((None, tn, tk), lambda i, j, k: (k // per, j, k % per))
    else:
        b_spec = (pl.BlockSpec((tk, tn), lambda i, j, k: (k, j)) if cb == 0
                  else pl.BlockSpec((tn, tk), lambda i, j, k: (j, k)))
    e_specs = [pl.BlockSpec((tm, tn), lambda i, j, k: (i, j)) for _ in epi_in]
    if out_groups:
        per_o = (N // out_groups) // tn
        o_spec = pl.BlockSpec((None, tm, tn), lambda i, j, k: (j // per_o, i, j % per_o))
        o_shape = jax.ShapeDtypeStruct((out_groups, M, N // out_groups), out_dtype)
    else:
        o_spec = pl.BlockSpec((tm, tn), lambda i, j, k: (i, j))
        o_shape = jax.ShapeDtypeStruct((M, N), out_dtype)
    return pl.pallas_call(
        body, grid=(M // tm, N // tn, nk), out_specs=o_spec,
        in_specs=[a_spec, b_spec] + e_specs + [pl.BlockSpec(memory_space=pl.ANY)] * len(deps),
        out_shape=o_shape, scratch_shapes=[pltpu.VMEM((tm, tn), F32)] if nk > 1 else [],
        compiler_params=_params(("parallel", "parallel", "arbitrary")), name=name)(a, b, *epi_in, *deps)


def _relu_sq(u):
    r = jnp.maximum(u, 0.0)
    return r * r


def _relu_sq_grad(acc, u):
    return acc * (2.0 * jnp.maximum(u, 0.0))


def _ln_stats(r):
    mu = jnp.mean(r, axis=-1, keepdims=True)
    xc = r - mu
    var = jnp.mean(xc * xc, axis=-1, keepdims=True)
    rstd = lax.rsqrt(var + LN_EPS)
    return xc * rstd, rstd


def _ln_bwd(dy, xhat, rstd, g):
    dxh = dy * g
    m1 = jnp.mean(dxh, axis=-1, keepdims=True)
    m2 = jnp.mean(dxh * xhat, axis=-1, keepdims=True)
    return rstd * (dxh - m1 - xhat * m2)


def _row_call(body, ins, row_ins, outs, acc_outs, name, tr=256):
    S = ins[0].shape[0]
    tr = min(tr, S)
    n_in, n_row, n_out = len(ins), len(row_ins), len(outs)

    def wrapped(*refs):
        i = pl.program_id(0)
        acc_refs = refs[n_in + n_row + n_out:]

        @pl.when(i == 0)
        def _():
            for r in acc_refs:
                r[...] = jnp.zeros_like(r)

        body(*refs)

    in_specs = [pl.BlockSpec((tr, a.shape[1]), lambda i: (i, 0)) for a in ins]
    in_specs += [pl.BlockSpec(a.shape, lambda i: (0, 0)) for a in row_ins]
    out_specs = [pl.BlockSpec((tr, s.shape[1]), lambda i: (i, 0)) for s in outs]
    out_specs += [pl.BlockSpec(s.shape, lambda i: (0, 0)) for s in acc_outs]
    return pl.pallas_call(wrapped, grid=(S // tr,), in_specs=in_specs, out_specs=out_specs,
                          out_shape=list(outs) + list(acc_outs),
                          compiler_params=_params(("arbitrary",)), name=name)(*ins, *row_ins)


def _ln1_fwd(x, mixed, g, b):
    def body(x_ref, m_ref, g_ref, b_ref, h_ref):
        xhat, _ = _ln_stats(DN_ALPHA * x_ref[...] + m_ref[...])
        h_ref[...] = xhat * g_ref[...] + b_ref[...]
    return _row_call(body, [x, mixed], [g, b], [jax.ShapeDtypeStruct(x.shape, F32)], [], "ln1_fwd")[0]


def _ln2_loss(h1, mlp, g, b, target):
    S, D = h1.shape
    sds = jax.ShapeDtypeStruct

    def body(h_ref, m_ref, t_ref, g_ref, b_ref, dr_ref, loss_ref, dg_ref, db_ref):
        xhat, rstd = _ln_stats(DN_ALPHA * h_ref[...] + m_ref[...])
        gv = g_ref[...]
        err = xhat * gv + b_ref[...] - t_ref[...]
        loss_ref[...] += jnp.sum(jnp.sum(err * err, axis=0, keepdims=True), axis=1, keepdims=True) * (0.5 / D)
        dy = err * (1.0 / D)
        dg_ref[...] += jnp.sum(dy * xhat, axis=0, keepdims=True)
        db_ref[...] += jnp.sum(dy, axis=0, keepdims=True)
        dr_ref[...] = _ln_bwd(dy, xhat, rstd, gv)

    return _row_call(body, [h1, mlp, target], [g, b], [sds((S, D), F32)],
                     [sds((1, LANE), F32), sds((1, D), F32), sds((1, D), F32)], "ln2_loss")


def _ln1_bwd(x, mixed, g, dr2, dh_mlp):
    S, D = x.shape
    sds = jax.ShapeDtypeStruct

    def body(x_ref, m_ref, dr2_ref, dh_ref, g_ref, dr_ref, dg_ref, db_ref):
        xhat, rstd = _ln_stats(DN_ALPHA * x_ref[...] + m_ref[...])
        dy = DN_ALPHA * dr2_ref[...] + dh_ref[...]
        dg_ref[...] += jnp.sum(dy * xhat, axis=0, keepdims=True)
        db_ref[...] += jnp.sum(dy, axis=0, keepdims=True)
        dr_ref[...] = _ln_bwd(dy, xhat, rstd, g_ref[...])

    return _row_call(body, [x, mixed, dr2, dh_mlp], [g], [sds((S, D), F32)],
                     [sds((1, D), F32), sds((1, D), F32)], "ln1_bwd")


def _grad_x(dr1, dx_proj, dep):
    def body(a_ref, b_ref, dep_ref, o_ref):
        o_ref[...] = DN_ALPHA * a_ref[...] + b_ref[...]
    return _row_call(body, [dr1, dx_proj], [dep], [jax.ShapeDtypeStruct(dr1.shape, F32)], [], "grad_x")[0]


def _bucket_table():
    qi = np.arange(BLK, dtype=np.int32)[:, None]
    kj = np.arange(2 * BLK, dtype=np.int32)[None, :]
    dist = qi + BLK - kj
    n = np.maximum(dist, 0)
    max_exact = N_BUCKETS // 2
    nf = np.maximum(n, 1).astype(np.float32)
    large = max_exact + (np.log(nf / np.float32(max_exact)) / np.float32(math.log(MAX_DISTANCE / max_exact))
                         * np.float32(N_BUCKETS - max_exact)).astype(np.int32)
    large = np.minimum(large, N_BUCKETS - 1)
    bucket = np.where(n < max_exact, n, large)
    return np.where((dist >= 0) & (dist < BLK), bucket, -1).astype(np.int32)


def _attn_bias(rel_bias_t):
    hq = rel_bias_t.shape[0]
    bucket = jnp.asarray(_bucket_table())

    def body(rb_ref, bk_ref, o_ref):
        h = pl.program_id(0)
        bk = bk_ref[...]
        acc = jnp.zeros((BLK, 2 * BLK), F32)
        for b in range(N_BUCKETS):
            acc = jnp.where(bk == b, rb_ref[h, b], acc)
        o_ref[...] = acc

    return pl.pallas_call(
        body, grid=(hq,),
        in_specs=[pl.BlockSpec(memory_space=pltpu.SMEM), pl.BlockSpec((BLK, 2 * BLK), lambda h: (0, 0))],
        out_specs=pl.BlockSpec((BLK, 2 * BLK), lambda h: (h, 0)),
        out_shape=jax.ShapeDtypeStruct((hq * BLK, 2 * BLK), F32),
        compiler_params=_params(("arbitrary",)), name="attn_bias")(rel_bias_t, bucket)


def _attn_probs(sc, sp, bias, sink, mask_c, mask_p):
    lc = jnp.where(mask_c, sc + bias[:, BLK:], NEG_INF)
    lp = jnp.where(mask_p, sp + bias[:, :BLK], NEG_INF)
    m = jnp.maximum(jnp.maximum(jnp.max(lc, axis=1, keepdims=True), jnp.max(lp, axis=1, keepdims=True)), sink)
    pc, pp, ps = jnp.exp(lc - m), jnp.exp(lp - m), jnp.exp(sink - m)
    inv = 1.0 / (jnp.sum(pc, axis=1, keepdims=True) + jnp.sum(pp, axis=1, keepdims=True) + ps)
    return pc, pp, ps, inv


def _attn_masks(n):
    qi = lax.broadcasted_iota(jnp.int32, (BLK, BLK), 0)
    kj = lax.broadcasted_iota(jnp.int32, (BLK, BLK), 1)
    return kj <= qi, (kj > qi) & (n > 0)


def _attn_fwd(proj, bias, sinks, hq, q_blk, k_blk, v_blk, out_width):
    S = proj.shape[0]
    hkv = hq // GQA
    wq, wk = hq * HEAD_A, hkv * HEAD_A

    def body(q_ref, k_ref, v_ref, bias_ref, sink_ref, o_ref):
        n = pl.program_id(0)
        cur = pl.multiple_of(n * BLK, BLK)
        prev = pl.multiple_of(jnp.maximum(n - 1, 0) * BLK, BLK)
        mask_c, mask_p = _attn_masks(n)
        for h4 in range(hkv):
            cs = slice(h4 * HEAD_A, (h4 + 1) * HEAD_A)
            kc, kp = k_ref[pl.ds(cur, BLK), cs].astype(BF16), k_ref[pl.ds(prev, BLK), cs].astype(BF16)
            vc, vp = v_ref[pl.ds(cur, BLK), cs].astype(BF16), v_ref[pl.ds(prev, BLK), cs].astype(BF16)
            hs_of = [slice(h * HEAD_A, (h + 1) * HEAD_A) for h in range(h4 * GQA, (h4 + 1) * GQA)]
            qs = [(q_ref[:, hs] * (HEAD_A ** -0.5)).astype(BF16) for hs in hs_of]
            scs = [_dot(q, kc, NT) for q in qs]
            sps = [_dot(q, kp, NT) for q in qs]
            pr = [_attn_probs(scs[g], sps[g], bias_ref[(h4 * GQA + g) * BLK:(h4 * GQA + g + 1) * BLK, :],
                              sink_ref[h4 * GQA + g], mask_c, mask_p) for g in range(GQA)]
            oc = [_dot(p[0].astype(BF16), vc, NN) for p in pr]
            op = [_dot(p[1].astype(BF16), vp, NN) for p in pr]
            for g, hs in enumerate(hs_of):
                o_ref[:, hs] = (oc[g] + op[g]) * pr[g][3]

    return pl.pallas_call(
        body, grid=(S // BLK,),
        in_specs=[pl.BlockSpec((BLK, wq), lambda n: (n, q_blk)), pl.BlockSpec((S, wk), lambda n: (0, k_blk)),
                  pl.BlockSpec((S, wk), lambda n: (0, v_blk)), pl.BlockSpec((hq * BLK, 2 * BLK), lambda n: (0, 0)),
                  pl.BlockSpec(memory_space=pltpu.SMEM)],
        out_specs=pl.BlockSpec((BLK, wq), lambda n: (n, 0)),
        out_shape=jax.ShapeDtypeStruct((S, out_width), F32),
        compiler_params=_params(("arbitrary",)), name="attn_fwd")(proj, proj, proj, bias, sinks)


def _attn_bwd(proj, bias, sinks, out, dmix, hq, q_blk, k_blk, v_blk):
    S = proj.shape[0]
    hkv = hq // GQA
    wq, wk = hq * HEAD_A, hkv * HEAD_A
    sds = jax.ShapeDtypeStruct

    def body(q_ref, k_ref, v_ref, bias_ref, sink_ref, o_ref, do_ref, dq_ref, dk_ref, dv_ref, dbias_ref, dsink_ref):
        n = pl.program_id(0)

        @pl.when(n == 0)
        def _():
            dk_ref[...] = jnp.zeros_like(dk_ref)
            dv_ref[...] = jnp.zeros_like(dv_ref)
            dbias_ref[...] = jnp.zeros_like(dbias_ref)
            dsink_ref[...] = jnp.zeros_like(dsink_ref)

        cur = pl.multiple_of(n * BLK, BLK)
        prev = pl.multiple_of(jnp.maximum(n - 1, 0) * BLK, BLK)
        mask_c, mask_p = _attn_masks(n)
        for h4 in range(hkv):
            cs = slice(h4 * HEAD_A, (h4 + 1) * HEAD_A)
            kc, kp = k_ref[pl.ds(cur, BLK), cs].astype(BF16), k_ref[pl.ds(prev, BLK), cs].astype(BF16)
            vc, vp = v_ref[pl.ds(cur, BLK), cs].astype(BF16), v_ref[pl.ds(prev, BLK), cs].astype(BF16)
            heads = list(range(h4 * GQA, (h4 + 1) * GQA))
            hs_of = [slice(h * HEAD_A, (h + 1) * HEAD_A) for h in heads]
            rows_of = [slice(h * BLK, (h + 1) * BLK) for h in heads]
            G = range(GQA)
            qs = [(q_ref[:, hs] * (HEAD_A ** -0.5)).astype(BF16) for hs in hs_of]
            dos = [do_ref[:, hs] for hs in hs_of]
            dobs = [d.astype(BF16) for d in dos]
            scs = [_dot(q, kc, NT) for q in qs]
            sps = [_dot(q, kp, NT) for q in qs]
            dpc = [_dot(d, vc, NT) for d in dobs]
            dpp = [_dot(d, vp, NT) for d in dobs]
            pcs, pps, dscs, dsps = [], [], [], []
            for g in G:
                pc, pp, ps, inv = _attn_probs(scs[g], sps[g], bias_ref[rows_of[g], :], sink_ref[heads[g]], mask_c, mask_p)
                pc, pp, ps = pc * inv, pp * inv, ps * inv
                delta = jnp.sum(dos[g] * o_ref[:, hs_of[g]], axis=1, keepdims=True)
                dsc, dsp = pc * (dpc[g] - delta), pp * (dpp[g] - delta)
                dsink_ref[heads[g]:heads[g] + 1, :] += jnp.broadcast_to(jnp.sum(-ps * delta, axis=0, keepdims=True), (1, LANE))
                dbias_ref[rows_of[g], BLK:] += dsc
                dbias_ref[rows_of[g], :BLK] += dsp
                pcs.append(pc.astype(BF16))
                pps.append(pp.astype(BF16))
                dscs.append(dsc.astype(BF16))
                dsps.append(dsp.astype(BF16))
            dq1 = [_dot(dscs[g], kc, NN) for g in G]
            dq2 = [_dot(dsps[g], kp, NN) for g in G]
            dkc = [_dot(dscs[g], qs[g], TN) for g in G]
            dkp = [_dot(dsps[g], qs[g], TN) for g in G]
            dvc = [_dot(pcs[g], dobs[g], TN) for g in G]
            dvp = [_dot(pps[g], dobs[g], TN) for g in G]
            for g in G:
                dq_ref[:, hs_of[g]] = (dq1[g] + dq2[g]) * (HEAD_A ** -0.5)
            dk_ref[pl.ds(cur, BLK), cs] += sum(dkc[1:], dkc[0])
            dk_ref[pl.ds(prev, BLK), cs] += sum(dkp[1:], dkp[0])
            dv_ref[pl.ds(cur, BLK), cs] += sum(dvc[1:], dvc[0])
            dv_ref[pl.ds(prev, BLK), cs] += sum(dvp[1:], dvp[0])

    return pl.pallas_call(
        body, grid=(S // BLK,),
        in_specs=[pl.BlockSpec((BLK, wq), lambda n: (n, q_blk)), pl.BlockSpec((S, wk), lambda n: (0, k_blk)),
                  pl.BlockSpec((S, wk), lambda n: (0, v_blk)), pl.BlockSpec((hq * BLK, 2 * BLK), lambda n: (0, 0)),
                  pl.BlockSpec(memory_space=pltpu.SMEM),
                  pl.BlockSpec((BLK, wq), lambda n: (n, 0)), pl.BlockSpec((BLK, wq), lambda n: (n, 0))],
        out_specs=[pl.BlockSpec((BLK, wq), lambda n: (n, 0)), pl.BlockSpec((S, wk), lambda n: (0, 0)),
                   pl.BlockSpec((S, wk), lambda n: (0, 0)), pl.BlockSpec((hq * BLK, 2 * BLK), lambda n: (0, 0)),
                   pl.BlockSpec((hq, LANE), lambda n: (0, 0))],
        out_shape=[sds((S, wq), F32), sds((S, wk), F32), sds((S, wk), F32), sds((hq * BLK, 2 * BLK), F32),
                   sds((hq, LANE), F32)],
        compiler_params=_params(("arbitrary",)), name="attn_bwd")(proj, proj, proj, bias, sinks, out, dmix)


def _rel_bias_grad(dbias, hq):
    bucket = jnp.asarray(_bucket_table())

    def body(d_ref, bk_ref, o_ref):
        d = d_ref[...]
        bk = bk_ref[...]
        rows = [jnp.sum(jnp.where(bk == b, d, 0.0), axis=0, keepdims=True) for b in range(N_BUCKETS)]
        tot = jnp.sum(jnp.concatenate(rows, axis=0), axis=1, keepdims=True)
        o_ref[...] = jnp.broadcast_to(tot, (N_BUCKETS, LANE))

    return pl.pallas_call(
        body, grid=(hq,),
        in_specs=[pl.BlockSpec((BLK, 2 * BLK), lambda h: (h, 0)), pl.BlockSpec((BLK, 2 * BLK), lambda h: (0, 0))],
        out_specs=pl.BlockSpec((None, N_BUCKETS, LANE), lambda h: (h, 0, 0)),
        out_shape=jax.ShapeDtypeStruct((hq, N_BUCKETS, LANE), F32),
        compiler_params=_params(("arbitrary",)), name="rel_bias_grad")(dbias, bucket)


def _sigmoid(x):
    return 1.0 / (1.0 + jnp.exp(-x))


def _shift_rows(x, s):
    n = x.shape[0]
    row = lax.broadcasted_iota(jnp.int32, x.shape, 0)
    if s > 0:
        return jnp.where(row >= s, pltpu.roll(x, s, 0), 0.0)
    return jnp.where(row < n + s, pltpu.roll(x, n + s, 0), 0.0)


def _conv_silu_norm(xv, w, j, nh):
    c = w[CONV_W - 1:CONV_W, :] * xv
    for s in range(1, CONV_W):
        c = c + w[CONV_W - 1 - s:CONV_W - s, :] * _shift_rows(xv, s)
    sg = _sigmoid(c)
    a = c * sg
    r = lax.rsqrt(jnp.sum(a * a, axis=1, keepdims=True) + RMS_EPS)
    scale = jnp.where(j < nh, HEAD_D ** -0.5, 1.0)
    is_norm = j < 2 * nh
    y = jnp.where(is_norm, a * (r * scale), a)
    return c, sg, a, r, scale, is_norm, y


def _gdn_prep_fwd(proj, conv_w, nh, blk0):
    S = proj.shape[0]

    def body(x_ref, w_ref, o_ref):
        j = pl.program_id(0)
        o_ref[...] = _conv_silu_norm(x_ref[...], w_ref[...], j, nh)[-1]

    return pl.pallas_call(
        body, grid=(3 * nh,),
        in_specs=[pl.BlockSpec((S, HEAD_D), lambda j: (0, blk0 + j)), pl.BlockSpec((CONV_W, HEAD_D), lambda j: (0, j))],
        out_specs=pl.BlockSpec((S, HEAD_D), lambda j: (0, 3 * (j % nh) + j // nh)),
        out_shape=jax.ShapeDtypeStruct((S, 3 * nh * HEAD_D), F32),
        compiler_params=_params(("parallel",)), name="gdn_prep_fwd")(proj, conv_w)


def _gdn_prep_bwd(proj, conv_w, dqkv, nh, blk0):
    S = proj.shape[0]
    sds = jax.ShapeDtypeStruct

    def body(x_ref, w_ref, dy_ref, dx_ref, dw_ref):
        j = pl.program_id(0)
        xv, w = x_ref[...], w_ref[...]
        c, sg, a, r, scale, is_norm, _ = _conv_silu_norm(xv, w, j, nh)
        dy = dy_ref[...]
        rs = r * scale
        da_n = rs * dy - a * (r * r * rs) * jnp.sum(dy * a, axis=1, keepdims=True)
        da = jnp.where(is_norm, da_n, dy)
        dc = da * (sg * (1.0 + c * (1.0 - sg)))
        dx = w[CONV_W - 1:CONV_W, :] * dc
        dws = [jnp.sum(dc * xv, axis=0, keepdims=True)]
        for s in range(1, CONV_W):
            dx = dx + w[CONV_W - 1 - s:CONV_W - s, :] * _shift_rows(dc, -s)
            dws.insert(0, jnp.sum(dc * _shift_rows(xv, s), axis=0, keepdims=True))
        dx_ref[...] = dx
        dw_ref[...] = jnp.concatenate(dws, axis=0)

    return pl.pallas_call(
        body, grid=(3 * nh,),
        in_specs=[pl.BlockSpec((S, HEAD_D), lambda j: (0, blk0 + j)), pl.BlockSpec((CONV_W, HEAD_D), lambda j: (0, j)),
                  pl.BlockSpec((S, HEAD_D), lambda j: (0, 3 * (j % nh) + j // nh))],
        out_specs=[pl.BlockSpec((S, HEAD_D), lambda j: (0, j)), pl.BlockSpec((CONV_W, HEAD_D), lambda j: (0, j))],
        out_shape=[sds((S, 3 * nh * HEAD_D), F32), sds((CONV_W, 3 * nh * HEAD_D), F32)],
        compiler_params=_params(("parallel",)), name="gdn_prep_bwd")(proj, conv_w, dqkv)


def _softplus(x):
    return jnp.maximum(x, 0.0) + jnp.log(1.0 + jnp.exp(-jnp.abs(x)))


def _gates_fwd(ab, al, dt, nh):
    S = ab.shape[0]

    def body(ab_ref, al_ref, dt_ref, o_ref):
        v = ab_ref[...]
        lane = lax.broadcasted_iota(jnp.int32, v.shape, 1)
        g = -jnp.exp(al_ref[...]) * _softplus(v + dt_ref[...])
        o_ref[...] = jnp.where(lane < nh, g, jnp.where(lane < 2 * nh, _sigmoid(v), 0.0))

    row = pl.BlockSpec((1, LANE), lambda i: (0, 0))
    full = pl.BlockSpec((S, LANE), lambda i: (0, 0))
    return pl.pallas_call(body, grid=(1,), in_specs=[full, row, row], out_specs=full,
                          out_shape=jax.ShapeDtypeStruct((S, LANE), F32),
                          compiler_params=_params(("arbitrary",)), name="gates_fwd")(ab, al, dt)


def _gates_bwd(ab, al, dt, dgb, nh):
    S = ab.shape[0]
    sds = jax.ShapeDtypeStruct

    def body(ab_ref, al_ref, dt_ref, d_ref, dab_ref, dal_ref, ddt_ref):
        v, d = ab_ref[...], d_ref[...]
        lane = lax.broadcasted_iota(jnp.int32, v.shape, 1)
        is_a = lane < nh
        z = v + dt_ref[...]
        dsp = jnp.where(is_a, d * (-jnp.exp(al_ref[...])), 0.0)
        dz = dsp * _sigmoid(z)
        beta = _sigmoid(v)
        dab_ref[...] = jnp.where(is_a, dz, jnp.where(lane < 2 * nh, d * beta * (1.0 - beta), 0.0))
        dal_ref[...] = jnp.sum(dsp * _softplus(z), axis=0, keepdims=True)
        ddt_ref[...] = jnp.sum(dz, axis=0, keepdims=True)

    row = pl.BlockSpec((1, LANE), lambda i: (0, 0))
    full = pl.BlockSpec((S, LANE), lambda i: (0, 0))
    return pl.pallas_call(body, grid=(1,), in_specs=[full, row, row, full], out_specs=[full, row, row],
                          out_shape=[sds((S, LANE), F32), sds((1, LANE), F32), sds((1, LANE), F32)],
                          compiler_params=_params(("arbitrary",)), name="gates_bwd")(ab, al, dt, dgb)


def _col_of(tile, h):
    lane = lax.broadcasted_iota(jnp.int32, tile.shape, 1)
    return jnp.sum(jnp.where(lane == h, tile, 0.0), axis=1, keepdims=True)


def _to_row(col, eye):
    return jnp.sum(jnp.where(eye, col, 0.0), axis=0, keepdims=True)


def _to_col(row, eye):
    return jnp.sum(jnp.where(eye, row, 0.0), axis=1, keepdims=True)


def _split(a):
    hi = a.astype(BF16)
    return hi, (a - hi.astype(F32)).astype(BF16)


def _gdot(a, b, dims):
    ah, al = _split(a)
    bh, bl = _split(b)
    return _dot(ah, bh, dims) + (_dot(ah, bl, dims) + _dot(al, bh, dims))


def _chunks_local(qs, ks, vs, gcols, bcols, Ts=None):
    C = CHUNK
    row = lax.broadcasted_iota(jnp.int32, (C, C), 0)
    col = lax.broadcasted_iota(jnp.int32, (C, C), 1)
    tril, strict, eye = col <= row, col < row, col == row
    outs = []
    for k, gcol, bcol in zip(ks, gcols, bcols):
        grow = _to_row(gcol, eye)
        G_row = jnp.sum(jnp.where(row <= col, gcol, 0.0), axis=0, keepdims=True)
        G_col = jnp.sum(jnp.where(tril, grow, 0.0), axis=1, keepdims=True)
        G_last = G_col[C - 1:C, :]
        outs.append(dict(strict=strict, eye=eye, row=row, col=col, decay=jnp.exp(jnp.where(tril, G_col - G_row, NEG_INF)),
                         eG=jnp.exp(G_col), eGr=jnp.exp(G_last - G_col), gl=jnp.exp(G_last), kb=k * bcol))
    Ms = [_gdot(o["kb"], k, NT) for o, k in zip(outs, ks)]
    Ns = [_gdot(q, k, NT) for q, k in zip(qs, ks)]
    for o, q, k, M, N in zip(outs, qs, ks, Ms, Ns):
        o.update(A=jnp.where(strict, M * o["decay"], 0.0), attn=N * o["decay"], rhs_k=o["kb"] * o["eG"],
                 q_dec=q * o["eG"], k_dec=k * o["eGr"])
    if Ts is None:
        Ts = [jnp.where(eye, 1.0, 0.0) - o["A"] for o in outs]
        Ps = [o["A"] for o in outs]
        for _ in range(int(math.log2(C)) - 1):
            Ps = [_gdot(P, P, NN) for P in Ps]
            Ts = [T + _gdot(T, P, NN) for T, P in zip(Ts, Ps)]
        us = [_gdot(T, v * bcol, NN) for T, v, bcol in zip(Ts, vs, bcols)]
        ws = [_gdot(T, o["rhs_k"], NN) for T, o in zip(Ts, outs)]
        for o, T, u, w in zip(outs, Ts, us, ws):
            o.update(T=T, u=u, w=w)
    return outs


GDN_ROWS = 256
GDN_LOCAL_ROWS = 512
WQK = 3 * CHUNK


def _gdn_local_fwd(qkv, gb, nh):
    S = qkv.shape[0]
    nc = S // CHUNK
    rb = min(GDN_LOCAL_ROWS, S)
    cpb = rb // CHUNK
    sds = jax.ShapeDtypeStruct

    def body(q_ref, k_ref, v_ref, gb_ref, u_ref, wqk_ref, attn_ref, t_ref):
        h = pl.program_id(0)
        rows_of = [slice(ci * CHUNK, (ci + 1) * CHUNK) for ci in range(cpb)]
        gbts = [gb_ref[rows, :] for rows in rows_of]
        Ls = _chunks_local([q_ref[rows, :] for rows in rows_of], [k_ref[rows, :] for rows in rows_of],
                           [v_ref[rows, :] for rows in rows_of], [_col_of(t, h) for t in gbts],
                           [_col_of(t, nh + h) for t in gbts])
        for ci, (rows, L) in enumerate(zip(rows_of, Ls)):
            u_ref[rows, :] = L["u"]
            base = ci * WQK
            wqk_ref[base:base + CHUNK, :] = L["w"]
            wqk_ref[base + CHUNK:base + 2 * CHUNK, :] = L["q_dec"]
            wqk_ref[base + 2 * CHUNK:base + WQK, :] = L["k_dec"]
            attn_ref[ci] = L["attn"]
            t_ref[ci] = L["T"]

    cc = pl.BlockSpec((None, cpb, CHUNK, CHUNK), lambda h, i: (h, i, 0, 0))
    return pl.pallas_call(
        body, grid=(nh, S // rb),
        in_specs=[pl.BlockSpec((rb, HEAD_D), lambda h, i: (i, 3 * h)), pl.BlockSpec((rb, HEAD_D), lambda h, i: (i, 3 * h + 1)),
                  pl.BlockSpec((rb, HEAD_D), lambda h, i: (i, 3 * h + 2)), pl.BlockSpec((rb, LANE), lambda h, i: (i, 0))],
        out_specs=[pl.BlockSpec((rb, HEAD_D), lambda h, i: (i, h)),
                   pl.BlockSpec((None, 3 * rb, HEAD_D), lambda h, i: (h, i, 0)), cc, cc],
        out_shape=[sds((S, nh * HEAD_D), F32), sds((nh, 3 * S, HEAD_D), F32), sds((nh, nc, CHUNK, CHUNK), F32),
                   sds((nh, nc, CHUNK, CHUNK), F32)],
        compiler_params=_params(("parallel", "parallel")), name="gdn_local_fwd")(qkv, qkv, qkv, gb)


def _gdn_scan_fwd(u, wqk, attn, gb, nh, dep):
    S = u.shape[0]
    nc = S // CHUNK
    rb = min(GDN_ROWS, S)
    cpb = rb // CHUNK
    sds = jax.ShapeDtypeStruct

    def body(u_ref, wqk_ref, attn_ref, gb_ref, dep_ref, o_ref, vn_ref, st_ref, s_ref):
        @pl.when(pl.program_id(0) == 0)
        def _():
            s_ref[...] = jnp.zeros_like(s_ref)

        for ci in range(cpb):
            rows = slice(ci * CHUNK, (ci + 1) * CHUNK)
            glv = jnp.exp(jnp.sum(gb_ref[rows, :], axis=0, keepdims=True))
            base = ci * WQK
            heads = range(nh)
            cols = [slice(h * HEAD_D, (h + 1) * HEAD_D) for h in heads]
            states = [s_ref[h] for h in heads]
            rs = [_gdot(wqk_ref[h, base:base + 2 * CHUNK, :], states[h], NN) for h in heads]
            vbs = [u_ref[rows, cols[h]] - rs[h][:CHUNK] for h in heads]
            os_ = [_gdot(attn_ref[h, ci], vbs[h], NN) for h in heads]
            ks_ = [_gdot(wqk_ref[h, base + 2 * CHUNK:base + WQK, :], vbs[h], TN) for h in heads]
            for h in heads:
                st_ref[h, ci] = states[h]
                o_ref[rows, cols[h]] = rs[h][CHUNK:] + os_[h]
                vn_ref[rows, cols[h]] = vbs[h]
                s_ref[h] = states[h] * glv[:, h:h + 1] + ks_[h]

    return pl.pallas_call(
        body, grid=(S // rb,),
        in_specs=[pl.BlockSpec((rb, nh * HEAD_D), lambda i: (i, 0)), pl.BlockSpec((nh, 3 * rb, HEAD_D), lambda i: (0, i, 0)),
                  pl.BlockSpec((nh, cpb, CHUNK, CHUNK), lambda i: (0, i, 0, 0)), pl.BlockSpec((rb, LANE), lambda i: (i, 0)),
                  pl.BlockSpec(memory_space=pl.ANY)],
        out_specs=[pl.BlockSpec((rb, nh * HEAD_D), lambda i: (i, 0)), pl.BlockSpec((rb, nh * HEAD_D), lambda i: (i, 0)),
                   pl.BlockSpec((nh, cpb, HEAD_D, HEAD_D), lambda i: (0, i, 0, 0))],
        out_shape=[sds((S, nh * HEAD_D), F32), sds((S, nh * HEAD_D), F32), sds((nh, nc, HEAD_D, HEAD_D), F32)],
        scratch_shapes=[pltpu.VMEM((nh, HEAD_D, HEAD_D), F32)],
        compiler_params=_params(("arbitrary",)), name="gdn_scan_fwd")(u, wqk, attn, gb, dep)


def _gdn_scan_bwd(wqk, attn, gb, states, vn, do, nh, dep):
    S = vn.shape[0]
    nc = S // CHUNK
    rb = min(GDN_ROWS, S)
    cpb = rb // CHUNK
    last = S // rb - 1
    sds = jax.ShapeDtypeStruct

    def body(wqk_ref, attn_ref, gb_ref, st_ref, vn_ref, do_ref, dep_ref, dvn_ref, dw_ref, dqd_ref, dkd_ref, da_ref, dgl_ref,
             ds_ref):
        @pl.when(pl.program_id(0) == 0)
        def _():
            ds_ref[...] = jnp.zeros_like(ds_ref)

        row = lax.broadcasted_iota(jnp.int32, (CHUNK, CHUNK), 0)
        col = lax.broadcasted_iota(jnp.int32, (CHUNK, CHUNK), 1)
        for ci in reversed(range(cpb)):
            rows = slice(ci * CHUNK, (ci + 1) * CHUNK)
            glv = jnp.exp(jnp.sum(gb_ref[rows, :], axis=0, keepdims=True))
            base = ci * WQK
            heads = range(nh)
            cols = [slice(h * HEAD_D, (h + 1) * HEAD_D) for h in heads]
            states = [st_ref[h, ci] for h in heads]
            dSs = [ds_ref[h] for h in heads]
            vbs = [vn_ref[rows, cols[h]] for h in heads]
            dobs = [do_ref[rows, cols[h]] for h in heads]
            dv1 = [_gdot(attn_ref[h, ci], dobs[h], TN) for h in heads]
            dv2 = [_gdot(wqk_ref[h, base + 2 * CHUNK:base + WQK, :], dSs[h], NN) for h in heads]
            das = [_gdot(dobs[h], vbs[h], NT) for h in heads]
            dkds = [_gdot(vbs[h], dSs[h], NT) for h in heads]
            dvbs = [dv1[h] + dv2[h] for h in heads]
            xs = [_gdot(jnp.concatenate([dobs[h], dvbs[h]], axis=0), states[h], NT) for h in heads]
            dss = [_gdot(wqk_ref[h, base:base + 2 * CHUNK, :], jnp.concatenate([-dvbs[h], dobs[h]], axis=0), TN)
                   for h in heads]
            for h in heads:
                dqd_ref[rows, cols[h]] = xs[h][:CHUNK]
                dw_ref[rows, cols[h]] = -xs[h][CHUNK:]
                dvn_ref[rows, cols[h]] = dvbs[h]
                da_ref[h, ci] = jnp.where(col <= row, das[h], 0.0)
                dkd_ref[rows, cols[h]] = dkds[h]
                gl = glv[:, h:h + 1]
                dgl = jnp.sum(jnp.sum(states[h] * dSs[h], axis=0, keepdims=True), axis=1, keepdims=True)
                dgl_ref[h, ci] = jnp.broadcast_to(dgl * gl, (1, LANE))
                ds_ref[h] = dSs[h] * gl + dss[h]

    rv = lambda i: last - i
    wide = pl.BlockSpec((rb, nh * HEAD_D), lambda i: (rv(i), 0))
    return pl.pallas_call(
        body, grid=(S // rb,),
        in_specs=[pl.BlockSpec((nh, 3 * rb, HEAD_D), lambda i: (0, rv(i), 0)),
                  pl.BlockSpec((nh, cpb, CHUNK, CHUNK), lambda i: (0, rv(i), 0, 0)),
                  pl.BlockSpec((rb, LANE), lambda i: (rv(i), 0)),
                  pl.BlockSpec((nh, cpb, HEAD_D, HEAD_D), lambda i: (0, rv(i), 0, 0)), wide, wide,
                  pl.BlockSpec(memory_space=pl.ANY)],
        out_specs=[wide, wide, wide, wide, pl.BlockSpec((nh, cpb, CHUNK, CHUNK), lambda i: (0, rv(i), 0, 0)),
                   pl.BlockSpec((nh, cpb, 1, LANE), lambda i: (0, rv(i), 0, 0))],
        out_shape=[sds((S, nh * HEAD_D), F32), sds((S, nh * HEAD_D), F32), sds((S, nh * HEAD_D), F32),
                   sds((S, nh * HEAD_D), F32), sds((nh, nc, CHUNK, CHUNK), F32), sds((nh, nc, 1, LANE), F32)],
        scratch_shapes=[pltpu.VMEM((nh, HEAD_D, HEAD_D), F32)],
        compiler_params=_params(("arbitrary",)), name="gdn_scan_bwd")(wqk, attn, gb, states, vn, do, dep)


def _gdn_local_bwd(qkv, gb, T, u, wqk, dvn, dw, dqd, dkd, dattn, dgl, nh):
    S = qkv.shape[0]
    rb = min(GDN_LOCAL_ROWS, S)
    cpb = rb // CHUNK
    sds = jax.ShapeDtypeStruct

    def body(q_ref, k_ref, v_ref, gb_ref, t_ref, u_ref, wqk_ref, dvn_ref, dw_ref, dqd_ref, dkd_ref, da_ref, dgl_ref,
             dqkv_ref, dg_ref, db_ref):
        h = pl.program_id(0)
        n = range(cpb)
        rows_of = [slice(ci * CHUNK, (ci + 1) * CHUNK) for ci in n]
        qs, ks, vs = ([r[rows, :] for rows in rows_of] for r in (q_ref, k_ref, v_ref))
        gbts = [gb_ref[rows, :] for rows in rows_of]
        bcols = [_col_of(t, nh + h) for t in gbts]
        Ts = [t_ref[ci] for ci in n]
        Ls = _chunks_local(qs, ks, vs, [_col_of(t, h) for t in gbts], bcols, Ts=Ts)
        drvs = [_gdot(Ts[ci], dvn_ref[rows_of[ci], :], TN) for ci in n]
        drks = [_gdot(Ts[ci], dw_ref[rows_of[ci], :], TN) for ci in n]
        dAs = [jnp.where(Ls[ci]["strict"], -(_gdot(drvs[ci], u_ref[rows_of[ci], :], NT)
                                             + _gdot(drks[ci], wqk_ref[ci * WQK:ci * WQK + CHUNK, :], NT)), 0.0) for ci in n]
        dMs = [dAs[ci] * Ls[ci]["decay"] for ci in n]
        dNs = [da_ref[ci] * Ls[ci]["decay"] for ci in n]
        dkbs = [_gdot(dMs[ci], ks[ci], NN) for ci in n]
        dq1 = [_gdot(dNs[ci], ks[ci], NN) for ci in n]
        dk1 = [_gdot(dMs[ci], Ls[ci]["kb"], TN) for ci in n]
        dk2 = [_gdot(dNs[ci], qs[ci], TN) for ci in n]
        for ci in n:
            rows, L, q, k, v, bcol = rows_of[ci], Ls[ci], qs[ci], ks[ci], vs[ci], bcols[ci]
            eye, eG, eGr = L["eye"], L["eG"], L["eGr"]
            drv, drk, dkb = drvs[ci], drks[ci], dkbs[ci]
            dq_dec, dk_dec, dattn_c = dqd_ref[rows, :], dkd_ref[rows, :], da_ref[ci]
            dqkv_ref[rows, :HEAD_D] = dq1[ci] + dq_dec * eG
            dqkv_ref[rows, HEAD_D:2 * HEAD_D] = drk * (bcol * eG) + dk1[ci] + dkb * bcol + dk2[ci] + dk_dec * eGr
            dqkv_ref[rows, 2 * HEAD_D:] = drv * bcol
            db_ref[rows, :] = (jnp.sum(drv * v, axis=1, keepdims=True) + jnp.sum(drk * k, axis=1, keepdims=True) * eG
                               + jnp.sum(dkb * k, axis=1, keepdims=True))
            E = dAs[ci] * L["A"] + dattn_c * L["attn"]
            kd = jnp.sum(dk_dec * L["k_dec"], axis=1, keepdims=True)
            dG = (jnp.sum(dq_dec * L["q_dec"], axis=1, keepdims=True) - kd
                  + jnp.sum(drk * L["rhs_k"], axis=1, keepdims=True)
                  + jnp.sum(E, axis=1, keepdims=True) - _to_col(jnp.sum(E, axis=0, keepdims=True), eye))
            d_last = jnp.sum(kd, axis=0, keepdims=True) + dgl_ref[ci][:, :1]
            dG = dG + jnp.where(L["row"][:, :1] == CHUNK - 1, d_last, 0.0)
            dg_ref[rows, :] = jnp.sum(jnp.where(L["col"] >= L["row"], _to_row(dG, eye), 0.0), axis=1, keepdims=True)

    hd = pl.BlockSpec((rb, HEAD_D), lambda h, i: (i, h))
    cc = pl.BlockSpec((None, cpb, CHUNK, CHUNK), lambda h, i: (h, i, 0, 0))
    col1 = pl.BlockSpec((None, rb, 1), lambda h, i: (h, i, 0))
    return pl.pallas_call(
        body, grid=(nh, S // rb),
        in_specs=[pl.BlockSpec((rb, HEAD_D), lambda h, i: (i, 3 * h)), pl.BlockSpec((rb, HEAD_D), lambda h, i: (i, 3 * h + 1)),
                  pl.BlockSpec((rb, HEAD_D), lambda h, i: (i, 3 * h + 2)), pl.BlockSpec((rb, LANE), lambda h, i: (i, 0)),
                  cc, hd, pl.BlockSpec((None, 3 * rb, HEAD_D), lambda h, i: (h, i, 0)), hd, hd, hd, hd, cc,
                  pl.BlockSpec((None, cpb, 1, LANE), lambda h, i: (h, i, 0, 0))],
        out_specs=[pl.BlockSpec((rb, 3 * HEAD_D), lambda h, i: (i, h)), col1, col1],
        out_shape=[sds((S, 3 * nh * HEAD_D), F32)] + [sds((nh, S, 1), F32)] * 2,
        compiler_params=_params(("parallel", "parallel")), name="gdn_local_bwd")(
            qkv, qkv, qkv, gb, T, u, wqk, dvn, dw, dqd, dkd, dattn, dgl)


def _gated_norm_fwd(o, proj, norm_w, nh, z_blk0, mix, m_blk0):
    S = o.shape[0]

    def body(o_ref, z_ref, w_ref, mix_ref, y_ref):
        ov, z = o_ref[...], z_ref[...]
        r = lax.rsqrt(jnp.mean(ov * ov, axis=1, keepdims=True) + RMS_EPS)
        y_ref[...] = ov * r * w_ref[...] * (z * _sigmoid(z))

    return pl.pallas_call(
        body, grid=(nh,),
        in_specs=[pl.BlockSpec((S, HEAD_D), lambda h: (0, h)), pl.BlockSpec((S, HEAD_D), lambda h: (0, z_blk0 + h)),
                  pl.BlockSpec((1, HEAD_D), lambda h: (0, 0)), pl.BlockSpec(memory_space=pl.ANY)],
        out_specs=pl.BlockSpec((S, HEAD_D), lambda h: (0, m_blk0 + h)),
        out_shape=jax.ShapeDtypeStruct(mix.shape, F32), input_output_aliases={3: 0},
        compiler_params=_params(("parallel",)), name="gated_norm_fwd")(o, proj, norm_w, mix)


def _gated_norm_bwd(o, proj, norm_w, dmix, nh, z_blk0, d_blk0):
    S = o.shape[0]
    sds = jax.ShapeDtypeStruct

    def body(o_ref, z_ref, w_ref, dy_ref, do_ref, dz_ref, dw_ref):
        ov, z, w, dy = o_ref[...], z_ref[...], w_ref[...], dy_ref[...]
        r = lax.rsqrt(jnp.mean(ov * ov, axis=1, keepdims=True) + RMS_EPS)
        oh = ov * r
        sg = _sigmoid(z)
        dz_ref[...] = dy * (oh * w) * (sg * (1.0 + z * (1.0 - sg)))
        don = dy * (z * sg)
        @pl.when(pl.program_id(0) == 0)
        def _():
            dw_ref[...] = jnp.zeros_like(dw_ref)

        dw_ref[...] += jnp.sum(don * oh, axis=0, keepdims=True)
        doh = don * w
        do_ref[...] = r * (doh - oh * jnp.mean(doh * oh, axis=1, keepdims=True))

    return pl.pallas_call(
        body, grid=(nh,),
        in_specs=[pl.BlockSpec((S, HEAD_D), lambda h: (0, h)), pl.BlockSpec((S, HEAD_D), lambda h: (0, z_blk0 + h)),
                  pl.BlockSpec((1, HEAD_D), lambda h: (0, 0)), pl.BlockSpec((S, HEAD_D), lambda h: (0, d_blk0 + h))],
        out_specs=[pl.BlockSpec((S, HEAD_D), lambda h: (0, h)), pl.BlockSpec((S, HEAD_D), lambda h: (0, h)),
                   pl.BlockSpec((1, HEAD_D), lambda h: (0, 0))],
        out_shape=[sds((S, nh * HEAD_D), F32), sds((S, nh * HEAD_D), F32), sds((1, HEAD_D), F32)],
        compiler_params=_params(("arbitrary",)), name="gated_norm_bwd")(o, proj, norm_w, dmix)


def _adamw_math(w, g, m, v):
    m = ADAM_B1 * m + (1.0 - ADAM_B1) * g
    v = ADAM_B2 * v + (1.0 - ADAM_B2) * (g * g)
    m_hat = m / (1.0 - ADAM_B1 ** ADAM_STEP)
    v_hat = v / (1.0 - ADAM_B2 ** ADAM_STEP)
    delta = -ADAM_LR * (m_hat / (jnp.sqrt(v_hat) + ADAM_EPS) + ADAM_WD * w)
    return delta, m, v


def _slab_tiles(R, C, rows=256, cols=256):
    if R % rows == 0:
        return (rows, C), R // rows, lambda i: (i, 0)
    tc = _tile(C, cols)
    return (R, tc), C // tc, lambda i: (0, i)


def _adamw_big(parts, terms, chip, w, m, v, name):
    R, C = w.shape
    blk, steps, at = _slab_tiles(R, C)
    sds = jax.ShapeDtypeStruct

    def body(q_ref, p_ref, t_ref, w_ref, m_ref, v_ref, g_ref, d_ref, nm_ref, nv_ref):
        g = ((p_ref[...].astype(F32) + t_ref[0].astype(F32)) + t_ref[1].astype(F32)) + t_ref[2].astype(F32)
        g_ref[...] = g
        d_ref[...], nm_ref[...], nv_ref[...] = _adamw_math(w_ref[...], g, m_ref[...], v_ref[...])

    spec = pl.BlockSpec(blk, lambda i, q_ref: at(i))
    grid_spec = pltpu.PrefetchScalarGridSpec(
        num_scalar_prefetch=1, grid=(steps,),
        in_specs=[pl.BlockSpec((None,) + blk, lambda i, q_ref: (q_ref[0],) + at(i)),
                  pl.BlockSpec((3,) + blk, lambda i, q_ref: (0,) + at(i)), spec, spec, spec],
        out_specs=[spec] * 4)
    return pl.pallas_call(body, grid_spec=grid_spec, out_shape=[sds((R, C), F32)] * 4,
                          compiler_params=_params(("parallel",)), name=name)(chip, parts, terms, w, m, v)


def _adamw_small(ws, gs, ms, vs):
    n = len(ws)

    def body(*refs):
        for i in range(n):
            w, g, m, v = (refs[k * n + i][...] for k in range(4))
            d, nm, nv = _adamw_math(w, g, m, v)
            refs[4 * n + i][...] = d
            refs[5 * n + i][...] = nm
            refs[6 * n + i][...] = nv

    shapes = [jax.ShapeDtypeStruct(w.shape, F32) for w in ws]
    vm = pl.BlockSpec(memory_space=pltpu.VMEM)
    outs = pl.pallas_call(body, in_specs=[vm] * (4 * n), out_specs=[vm] * (3 * n), out_shape=shapes * 3,
                          name="adamw_small")(*ws, *gs, *ms, *vs)
    return outs[:n], outs[n:2 * n], outs[2 * n:]


MESH = pl.DeviceIdType.MESH
ANY = pl.BlockSpec(memory_space=pl.ANY)


def _place():
    x, y, c = lax.axis_index("x"), lax.axis_index("y"), lax.axis_index("c")
    return x, y, c, [(1 - x, y), (x, 1 - y), (1 - x, 1 - y)]


def _chip_sum(grad, recv, core, name):
    _, R, C = grad.shape
    blk, steps, at = _slab_tiles(R, C)

    def body(c_ref, g_ref, r_ref, o_ref):
        o_ref[...] = (g_ref[...].astype(F32) + r_ref[...].astype(F32)).astype(o_ref.dtype)

    grid_spec = pltpu.PrefetchScalarGridSpec(
        num_scalar_prefetch=1, grid=(4, steps),
        in_specs=[pl.BlockSpec((None,) + blk, lambda q, i, c_ref: (2 * q + c_ref[0],) + at(i)),
                  pl.BlockSpec((None,) + blk, lambda q, i, c_ref: (q,) + at(i))],
        out_specs=pl.BlockSpec((None,) + blk, lambda q, i, c_ref: (q,) + at(i)))
    return pl.pallas_call(body, grid_spec=grid_spec, out_shape=jax.ShapeDtypeStruct((4, R, C), BF16),
                          compiler_params=_params(("parallel", "parallel")), name=name)(core, grad, recv)


HBM_SPEC = pl.BlockSpec(memory_space=pltpu.HBM)
SEM_SPEC = pl.BlockSpec(memory_space=pltpu.SEMAPHORE)
DATAFLOW = pltpu.SideEffectType.DATAFLOW_SIDE_EFFECTING


def _split_start(name, bufs, plan, counts, after=None):
    nb, ng = len(bufs), len(counts)
    extra = [] if after is None else [after]
    place = [(g, k) for g, cnt in enumerate(counts) for k in range(cnt)]

    def body(*refs):
        sems, token = refs[nb + len(extra):nb + len(extra) + 2 * ng], refs[-1]
        for (g, k), (src, dst, to) in zip(place, plan(refs[:nb])):
            pltpu.make_async_remote_copy(src_ref=src, dst_ref=dst, send_sem=sems[2 * g].at[k], recv_sem=sems[2 * g + 1].at[k],
                                         device_id=to, device_id_type=MESH).start()
        token[...] = jnp.zeros_like(token)

    outs = pl.pallas_call(
        body, name=name,
        out_shape=(*[pltpu.SemaphoreType.DMA((cnt,)) for cnt in counts for _ in range(2)],
                   *[pltpu.HBM(b.shape, b.dtype) for b in bufs], jax.ShapeDtypeStruct((8, LANE), F32)),
        in_specs=[HBM_SPEC] * nb + [ANY] * len(extra),
        out_specs=(*[SEM_SPEC] * (2 * ng), *[HBM_SPEC] * nb, pl.BlockSpec(memory_space=pltpu.VMEM)),
        input_output_aliases={i: 2 * ng + i for i in range(nb)},
        compiler_params=pltpu.CompilerParams(has_side_effects=DATAFLOW))(
            *[pltpu.with_memory_space_constraint(b, pltpu.HBM) for b in bufs], *extra)
    return [(outs[2 * g], outs[2 * g + 1]) for g in range(ng)], list(outs[2 * ng:2 * ng + nb]), outs[-1]


def _split_wait(name, sems, bufs, plan, after):
    nb = len(bufs)
    send_sems, recv_sems = sems

    def body(*refs):
        send_s, recv_s = refs[nb], refs[nb + 1]
        for k, (src, dst, to) in enumerate(plan(refs[:nb])):
            cp = pltpu.make_async_remote_copy(src_ref=src, dst_ref=dst, send_sem=send_s.at[k], recv_sem=recv_s.at[k],
                                              device_id=to, device_id_type=MESH)
            cp.wait_send()
            cp.wait_recv()

    outs = pl.pallas_call(
        body, name=name, out_shape=tuple(pltpu.HBM(b.shape, b.dtype) for b in bufs),
        in_specs=[HBM_SPEC] * nb + [SEM_SPEC, SEM_SPEC, ANY], out_specs=tuple([HBM_SPEC] * nb),
        input_output_aliases={i: i for i in range(nb)},
        compiler_params=pltpu.CompilerParams(has_side_effects=DATAFLOW))(*bufs, send_sems, recv_sems, after)
    return list(outs)


def _slot(px, py, pc):
    return 4 * px + 2 * py + pc


class _Gather:
    def __init__(self, shards, groups, dev):
        self.shards, self.groups, self.dev = shards, groups, dev
        self.second = {}

    @staticmethod
    def _plan1(pairs, refs):
        x, y, c, chips = _place()
        out = []
        for s, l in pairs:
            dst = refs[l].at[_slot(x, y, c)]
            out.append((refs[s], dst, (x, y, 1 - c)))
            out += [(refs[s], dst, (px, py, c)) for px, py in chips]
        return out

    @staticmethod
    def _plan2(refs):
        x, y, c, chips = _place()
        return [(r.at[_slot(px, py, c)],) * 2 + ((x, y, 1 - c),) for r in refs for px, py in chips]

    def start(self):
        n = len(self.shards)
        lands = [lax.dynamic_update_slice(lax.empty((N_DEV,) + s.shape, s.dtype), s[None], (self.dev, 0, 0))
                 for s in self.shards]
        pairs = [(w, n + w) for g in self.groups for w in g]
        sems, bufs, token = _split_start("gather_start_1", list(self.shards) + lands, functools.partial(self._plan1, pairs),
                                         tuple(4 * len(g) for g in self.groups))
        self.first = [(sems[i], [bufs[w] for w in g], [bufs[n + w] for w in g]) for i, g in enumerate(self.groups)]
        return token

    def mid(self, gi, after):
        sems, srcs, lands = self.first[gi]
        m = len(srcs)
        plan = functools.partial(self._plan1, [(w, m + w) for w in range(m)])
        lands = _split_wait("gather_%d_wait_1" % gi, sems, srcs + lands, plan, after)[m:]
        sems, lands, token = _split_start("gather_%d_start_2" % gi, lands, self._plan2, (3 * m,))
        self.second[gi] = (sems[0], lands)
        return token

    def finish(self, gi, after):
        sems, lands = self.second[gi]
        return _split_wait("gather_%d_wait_2" % gi, sems, lands, self._plan2, after)


class _Exchanges:
    def __init__(self, tag, n, core, gather=None):
        self.tag, self.n, self.core, self.gather = tag, n, core, gather

    def weights_mid(self, after):
        return self.gather.mid(1, after)

    def weights_finish(self, after):
        return self.gather.finish(1, after)

    def _reduce_plan1(self, refs):
        n = self.n
        x, y, c, _ = _place()
        return [(refs[w].at[2 * q + (1 - c)], refs[n + w].at[q], (x, y, 1 - c)) for w in range(n) for q in range(4)]

    def _reduce_plan2(self, refs):
        n = self.n
        x, y, c, chips = _place()
        return [(refs[w].at[2 * px + py], refs[n + w].at[j], (px, py, c))
                for w in range(n) for j, (px, py) in enumerate(chips)]

    def grads_start(self, grads):
        lands = [lax.empty((4,) + g.shape[1:], g.dtype) for g in grads]
        self.g1 = _split_start(self.tag + "reduce_start_1", list(grads) + lands, self._reduce_plan1, (4 * self.n,))
        return self.g1[2]

    def grads_mid(self, after):
        n = self.n
        sems, bufs, _ = self.g1
        bufs = _split_wait(self.tag + "reduce_wait_1", sems[0], bufs, self._reduce_plan1, after)
        core = self.core.reshape(1).astype(jnp.int32)
        self.parts = [_chip_sum(bufs[w], bufs[n + w], core, self.tag + "reduce_chip_sum_%d" % w) for w in range(n)]
        lands = [lax.empty((3,) + p.shape[1:], p.dtype) for p in self.parts]
        self.g2 = _split_start(self.tag + "reduce_start_2", self.parts + lands, self._reduce_plan2, (3 * n,))
        return self.g2[2]

    def grads_finish(self, after):
        n = self.n
        sems, bufs, _ = self.g2
        bufs = _split_wait(self.tag + "reduce_wait_2", sems[0], bufs, self._reduce_plan2, after)
        self.parts, self.terms = bufs[:n], bufs[n:]


def _all_reduce_small(buf):
    R = buf.shape[0]

    def body(x_ref, o_ref, g_ref, send_sems, recv_sems):
        x, y, c, chips = _place()
        me, sibling = (x, y, c), (x, y, 1 - c)

        def slot(px, py, pc):
            return 4 * px + 2 * py + pc

        def copy(k, block, to, src=None):
            dst = g_ref.at[slot(*block)]
            return pltpu.make_async_remote_copy(src_ref=dst if src is None else src, dst_ref=dst,
                                                send_sem=send_sems.at[k], recv_sem=recv_sems.at[k],
                                                device_id=to, device_id_type=MESH)

        first = [copy(0, me, sibling, src=x_ref)]
        first += [copy(1 + j, me, (*chip, c), src=x_ref) for j, chip in enumerate(chips)]
        for cp in first:
            cp.start()
        g_ref[slot(*me)] = x_ref[...]
        passed = [copy(4 + j, (*chip, c), sibling) for j, chip in enumerate(chips)]
        for j, chip in enumerate(chips):
            copy(1 + j, (*chip, c), me).wait_recv()
            passed[j].start()
        copy(0, sibling, me).wait_recv()
        for j, chip in enumerate(chips):
            copy(4 + j, (*chip, 1 - c), me).wait_recv()
        for cp in first + passed:
            cp.wait_send()
        acc = g_ref[0]
        for s in range(1, N_DEV):
            acc = acc + g_ref[s]
        o_ref[...] = acc

    vm = pl.BlockSpec(memory_space=pltpu.VMEM)
    return pl.pallas_call(
        body, in_specs=[vm], out_specs=vm, out_shape=jax.ShapeDtypeStruct((R, LANE), F32),
        scratch_shapes=[pltpu.VMEM((N_DEV, R, LANE), F32), pltpu.SemaphoreType.DMA((7,)), pltpu.SemaphoreType.DMA((7,))],
        name="all_reduce_small")(buf)


def _pad_cols(a, width):
    return jnp.pad(a, ((0, 0), (0, width - a.shape[1])))


def _local_step(x, target, w_in_t, conv_w, a_log, dt_bias, delta_norm_w, sinks, rel_bias, ln1_g, ln1_b, ln2_g, ln2_b,
                ex, ex_in):
    S, D = x.shape
    aw = D // 2
    hq, hkv, nh = aw // HEAD_A, aw // HEAD_A // GQA, aw // HEAD_D
    kvw = hkv * HEAD_A
    c_q, c_k, c_v, c_d = 0, aw, aw + kvw, aw + 2 * kvw
    c_ab = c_d + 3 * aw
    c_z = c_ab + 2 * nh
    n_in = c_z + aw
    assert w_in_t.shape == (n_in, D), (w_in_t.shape, n_in)
    w_pt = jnp.concatenate([w_in_t[:c_ab], w_in_t[c_z:], jnp.pad(w_in_t[c_ab:c_z], ((0, LANE - 2 * nh), (0, 0)))], axis=0)
    p_z, p_ab = c_ab, c_ab + aw
    n_p = p_ab + LANE

    proj = _matmul(x, w_pt, NT, name="proj", tn=1152)
    bias = _attn_bias(rel_bias.T)
    attn_out = _attn_fwd(proj, bias, sinks.reshape(-1), hq, 0, c_k // kvw, c_v // kvw, D)
    conv2 = conv_w.reshape(CONV_W, 3 * aw)
    qkv = _gdn_prep_fwd(proj, conv2, nh, c_d // HEAD_D)
    ab = proj[:, p_ab:]
    al, dt = _pad_cols(a_log, LANE), _pad_cols(dt_bias, LANE)
    gb = _gates_fwd(ab, al, dt, nh)
    u_d, wqk, attn_d, t_d = _gdn_local_fwd(qkv, gb, nh)
    o_d, vn, states = _gdn_scan_fwd(u_d, wqk, attn_d, gb, nh, ex.weights_mid(u_d))
    mix = _gated_norm_fwd(o_d, proj, delta_norm_w, nh, p_z // HEAD_D, attn_out, aw // HEAD_D)
    w_o_g, w_up_g, w_down_g = ex.weights_finish(mix)
    w_o, w_down = w_o_g.reshape(D, D), w_down_g.reshape(-1, D)
    mixed = _matmul(mix, w_o, NN, name="out_proj")
    h1 = _ln1_fwd(x, mixed, ln1_g, ln1_b)
    u = _matmul(h1, w_up_g, NN, name="mlp_up", b_groups=True)
    mlp = _matmul(u, w_down, NN, name="mlp_down", a_fn=_relu_sq)
    dr2, loss_row, dln2_g, dln2_b = _ln2_loss(h1, mlp, ln2_g, ln2_b, target)

    du = _matmul(dr2, w_down, NT, name="d_mlp_act", epi=_relu_sq_grad, epi_in=(u,))
    dw_down = _matmul(u, dr2, TN, name="dw_down", a_fn=_relu_sq, out_dtype=BF16)
    dw_up = _matmul(h1, du, TN, name="dw_up", out_dtype=BF16, out_groups=N_DEV)
    dh_mlp = _matmul(du, w_up_g, NT, name="d_h1", b_groups=True)
    dr1, dln1_g, dln1_b = _ln1_bwd(x, mixed, ln1_g, dr2, dh_mlp)
    dw_o = _matmul(mix, dr1, TN, name="dw_o", out_dtype=BF16)
    tok = ex.grads_start([dw_o.reshape(N_DEV, -1, D), dw_up, dw_down.reshape(N_DEV, -1, D)])
    dmix = _matmul(dr1, w_o, NT, name="d_mix", deps=(tok,))
    dq_a, dk_a, dv_a, dbias, dsink = _attn_bwd(proj, bias, sinks.reshape(-1), mix, dmix, hq, 0, c_k // kvw, c_v // kvw)
    drel = _rel_bias_grad(dbias, hq)
    do_d, dz, dnw = _gated_norm_bwd(o_d, proj, delta_norm_w, dmix, nh, p_z // HEAD_D, aw // HEAD_D)
    dvn_s, dw_s, dqd, dkd, dattn_d, dgl = _gdn_scan_bwd(wqk, attn_d, gb, states, vn, do_d, nh, ex.grads_mid(dq_a))
    dqkv_n, dg, dbeta = _gdn_local_bwd(qkv, gb, t_d, u_d, wqk, dvn_s, dw_s, dqd, dkd, dattn_d, dgl, nh)
    dgb = _pad_cols(jnp.concatenate([dg.reshape(nh, S).T, dbeta.reshape(nh, S).T], axis=1), LANE)
    dab, da_log, ddt_bias = _gates_bwd(ab, al, dt, dgb, nh)
    dqkv_d, dconv = _gdn_prep_bwd(proj, conv2, dqkv_n, nh, c_d // HEAD_D)
    dproj = jnp.concatenate([dq_a, dk_a, dv_a, dqkv_d, dz, dab], axis=1)
    dw_pt = _matmul(dproj, x, TN, name="dw_in", out_dtype=BF16, tm=1152)
    dw_in_t = jnp.concatenate([dw_pt[:p_z], dw_pt[p_ab:p_ab + 2 * nh], dw_pt[p_z:p_ab]], axis=0)
    tok = ex_in.grads_start([dw_in_t.reshape(N_DEV, -1, D)])
    dx_proj = _matmul(dproj, w_pt, NN, name="d_x", tk=1920, deps=(tok,))
    grad_x = _grad_x(dr1, dx_proj, ex_in.grads_mid(dx_proj))
    ex.grads_finish(grad_x)

    small = dict(conv_w=dconv, a_log=da_log[:, :nh], dt_bias=ddt_bias[:, :nh], delta_norm_w=dnw,
                 attn_sinks=dsink[:, 0].reshape(1, hq), rel_bias=drel[:, :, 0].T,
                 ln1_g=dln1_g, ln1_b=dln1_b, ln2_g=dln2_g, ln2_b=dln2_b)
    return loss_row, grad_x, small


SMALL_ORDER = ("conv_w", "a_log", "dt_bias", "delta_norm_w", "attn_sinks", "rel_bias", "ln1_g", "ln1_b", "ln2_g", "ln2_b")


def _pack_small(loss_row, small):
    parts = [loss_row.reshape(-1)]
    for k in SMALL_ORDER:
        flat = small[k].reshape(-1)
        parts.append(jnp.pad(flat, (0, (-flat.shape[0]) % LANE)))
    flat = jnp.concatenate(parts)
    flat = jnp.pad(flat, (0, (-flat.shape[0]) % (8 * LANE)))
    return flat.reshape(-1, LANE)


def _unpack_small(buf, small_shapes):
    flat = buf.reshape(-1)
    loss = flat[0]
    off = LANE
    out = {}
    for k in SMALL_ORDER:
        n = int(np.prod(small_shapes[k]))
        out[k] = flat[off:off + n].reshape(small_shapes[k])
        off += n + (-n) % LANE
    return loss, out


def kernel(x, w_in, conv_w, a_log, dt_bias, delta_norm_w, attn_sinks, rel_bias, w_o, ln1_g, ln1_b, w_up, w_down, ln2_g, ln2_b, loss_target, m_w_in, m_conv_w, m_a_log, m_dt_bias, m_delta_norm_w, m_attn_sinks, m_rel_bias, m_w_o, m_ln1_g, m_ln1_b, m_w_up, m_w_down, m_ln2_g, m_ln2_b, v_w_in, v_conv_w, v_a_log, v_dt_bias, v_delta_norm_w, v_attn_sinks, v_rel_bias, v_w_o, v_ln1_g, v_ln1_b, v_w_up, v_w_down, v_ln2_g, v_ln2_b):
    S, D = x.shape[1], x.shape[2]
    core = lax.axis_index("c")
    dev = 4 * lax.axis_index("x") + 2 * lax.axis_index("y") + core

    gather = _Gather([conv_w[0, :, 0, :], w_in[0].T.astype(BF16), w_o[0].astype(BF16), w_up[0].astype(BF16),
                      w_down[0].astype(BF16)], [[0, 1], [2, 3, 4]], dev)
    conv_g, w_in_g = gather.finish(0, gather.mid(0, gather.start()))
    w_in_t = w_in_g.reshape(-1, D)
    ex = _Exchanges("", 3, core, gather)
    ex_in = _Exchanges("in_", 1, core)

    cw_sh = conv_w.shape[3]
    conv_full = jnp.transpose(conv_g, (1, 0, 2)).reshape(CONV_W, N_DEV * cw_sh)

    loss_row, grad_x, small = _local_step(
        x[0], loss_target[0], w_in_t, conv_full, a_log, dt_bias, delta_norm_w, attn_sinks, rel_bias,
        ln1_g, ln1_b, ln2_g, ln2_b, ex, ex_in)

    chip_arr = (dev // 2).reshape(1).astype(jnp.int32)
    big = {}
    for i, (name, w, m, v) in enumerate((("w_o", w_o, m_w_o, v_w_o), ("w_up", w_up, m_w_up, v_w_up),
                                         ("w_down", w_down, m_w_down, v_w_down))):
        big[name] = [o[None] for o in _adamw_big(ex.parts[i], ex.terms[i], chip_arr, w[0], m[0], v[0], "adamw_" + name)]
    ex_in.grads_finish(big["w_down"][0])
    outs = _adamw_big(ex_in.parts[0], ex_in.terms[0], chip_arr, w_in[0].T, m_w_in[0].T, v_w_in[0].T, "adamw_w_in")
    big["w_in"] = [o.T[None] for o in outs]

    small_shapes = {k: v.shape for k, v in small.items()}
    loss, small = _unpack_small(_all_reduce_small(_pack_small(loss_row, small)), small_shapes)
    small["conv_w"] = lax.dynamic_slice(small["conv_w"], (0, dev * cw_sh), (CONV_W, cw_sh))
    small["rel_bias"] = small["rel_bias"].reshape(rel_bias.shape)
    p2 = dict(conv_w=(conv_w, m_conv_w, v_conv_w), a_log=(a_log, m_a_log, v_a_log), dt_bias=(dt_bias, m_dt_bias, v_dt_bias),
              delta_norm_w=(delta_norm_w, m_delta_norm_w, v_delta_norm_w), attn_sinks=(attn_sinks, m_attn_sinks, v_attn_sinks),
              rel_bias=(rel_bias, m_rel_bias, v_rel_bias), ln1_g=(ln1_g, m_ln1_g, v_ln1_g), ln1_b=(ln1_b, m_ln1_b, v_ln1_b),
              ln2_g=(ln2_g, m_ln2_g, v_ln2_g), ln2_b=(ln2_b, m_ln2_b, v_ln2_b))
    two_d = lambda a: a.reshape(-1, a.shape[-1])
    ws = [two_d(p2[k][0]) for k in SMALL_ORDER]
    gs = [two_d(small[k]) for k in SMALL_ORDER]
    ms = [two_d(p2[k][1]) for k in SMALL_ORDER]
    vs = [two_d(p2[k][2]) for k in SMALL_ORDER]
    ds, nms, nvs = _adamw_small(ws, gs, ms, vs)
    res = {}
    for i, k in enumerate(SMALL_ORDER):
        shp = p2[k][0].shape
        res[k] = [gs[i].reshape(shp), ds[i].reshape(shp), nms[i].reshape(shp), nvs[i].reshape(shp)]
    res.update(big)
    order = ("w_in", "conv_w", "a_log", "dt_bias", "delta_norm_w", "attn_sinks", "rel_bias", "w_o", "ln1_g", "ln1_b",
             "w_up", "w_down", "ln2_g", "ln2_b")
    return (loss, grad_x[None], *[res[k][0] for k in order], *[res[k][1] for k in order],
            *[res[k][2] for k in order], *[res[k][3] for k in order])
```

```python
import functools
import math

import numpy as np
import jax
import jax.numpy as jnp
from jax import lax
from jax.experimental import pallas as pl
from jax.experimental.pallas import tpu as pltpu

F32 = jnp.float32
BF16 = jnp.bfloat16
HIGHEST = lax.Precision.HIGHEST

N_DEV = 8
HEAD_A = 64
GQA = 4
BLK = 128
N_BUCKETS = 32
MAX_DISTANCE = 128
HEAD_D = 128
CONV_W = 4
CHUNK = 64
NEG_INF = -1e30
LN_EPS = 1e-5
RMS_EPS = 1e-6
DN_ALPHA = 2.0 ** 0.25
ADAM_LR, ADAM_B1, ADAM_B2, ADAM_EPS, ADAM_WD, ADAM_STEP = 0.001, 0.9, 0.999, 1e-08, 0.01, 10

LANE = 128
VMEM_LIMIT = 56 * 1024 * 1024

NN = ((1,), (0,))
NT = ((1,), (1,))
TN = ((0,), (0,))


def _dot(a, b, dims, prec=None):
    return lax.dot_general(a, b, (dims, ((), ())), precision=prec, preferred_element_type=F32)


def _tile(dim, pref):
    if dim <= pref:
        return dim
    t = (pref // LANE) * LANE
    while t > LANE and dim % t:
        t -= LANE
    assert dim % t == 0, (dim, pref)
    return t


def _params(sem):
    return pltpu.CompilerParams(dimension_semantics=sem, vmem_limit_bytes=VMEM_LIMIT)


def _matmul(a, b, dims, *, name, out_dtype=F32, tm=1024, tn=1024, tk=2048, a_fn=None, epi=None, epi_in=(),
            b_groups=None, out_groups=None, deps=()):
    (ca,), (cb,) = dims
    M, K = a.shape[1 - ca], a.shape[ca]
    if b_groups:
        G, R, C = b.shape
        bshape = (R, G * C)
    else:
        bshape = b.shape
    N = bshape[1 - cb]
    assert bshape[cb] == K, (a.shape, b.shape, dims)
    tm, tk = _tile(M, tm), _tile(K, tk)
    if b_groups:
        lim = C if cb == 0 else tn
        tn = _tile(N, min(tn, lim))
        if cb == 1:
            tk = _tile(K, min(tk, C))
    else:
        tn = _tile(N, tn)
    if out_groups:
        tn = _tile(N, min(tn, N // out_groups))
    nk = K // tk

    def body(*refs):
        a_ref, b_ref = refs[0], refs[1]
        e_refs = refs[2:2 + len(epi_in)]
        o_ref = refs[2 + len(epi_in) + len(deps)]
        acc_ref = refs[3 + len(epi_in) + len(deps)] if nk > 1 else None
        k = pl.program_id(2)
        av = a_ref[...]
        if a_fn is not None:
            av = a_fn(av)
        prod = _dot(av.astype(BF16), b_ref[...].astype(BF16), dims)

        def finish(r):
            if epi is not None:
                r = epi(r, *[e[...] for e in e_refs])
            o_ref[...] = r.astype(out_dtype)

        if nk == 1:
            finish(prod)
            return

        @pl.when(k == 0)
        def _():
            acc_ref[...] = prod

        @pl.when(k > 0)
        def _():
            acc_ref[...] += prod

        @pl.when(k == nk - 1)
        def _():
            finish(acc_ref[...])

    a_spec = (pl.BlockSpec((tm, tk), lambda i, j, k: (i, k)) if ca == 1
              else pl.BlockSpec((tk, tm), lambda i, j, k: (k, i)))
    if b_groups:
        if cb == 0:
            per = C // tn
            b_spec = pl.BlockSpec((None, tk, tn), lambda i, j, k: (j // per, k, j % per))
        else:
            per = C // tk
            b_spec = pl.BlockSpec((None, tn, tk), lambda i, j, k: (k // per, j, k % per))
    else:
        b_spec = (pl.BlockSpec((tk, tn), lambda i, j, k: (k, j)) if cb == 0
                  else pl.BlockSpec((tn, tk), lambda i, j, k: (j, k)))
    e_specs = [pl.BlockSpec((tm, tn), lambda i, j, k: (i, j)) for _ in epi_in]
    if out_groups:
        per_o = (N // out_groups) // tn
        o_spec = pl.BlockSpec((None, tm, tn), lambda i, j, k: (j // per_o, i, j % per_o))
        o_shape = jax.ShapeDtypeStruct((out_groups, M, N // out_groups), out_dtype)
    else:
        o_spec = pl.BlockSpec((tm, tn), lambda i, j, k: (i, j))
        o_shape = jax.ShapeDtypeStruct((M, N), out_dtype)
    return pl.pallas_call(
        body, grid=(M // tm, N // tn, nk), out_specs=o_spec,
        in_specs=[a_spec, b_spec] + e_specs + [pl.BlockSpec(memory_space=pl.ANY)] * len(deps),
        out_shape=o_shape, scratch_shapes=[pltpu.VMEM((tm, tn), F32)] if nk > 1 else [],
        compiler_params=_params(("parallel", "parallel", "arbitrary")), name=name)(a, b, *epi_in, *deps)


def _relu_sq(u):
    r = jnp.maximum(u, 0.0)
    return r * r


def _relu_sq_grad(acc, u):
    return acc * (2.0 * jnp.maximum(u, 0.0))


def _ln_stats(r):
    mu = jnp.mean(r, axis=-1, keepdims=True)
    xc = r - mu
    var = jnp.mean(xc * xc, axis=-1, keepdims=True)
    rstd = lax.rsqrt(var + LN_EPS)
    return xc * rstd, rstd


def _ln_bwd(dy, xhat, rstd, g):
    dxh = dy * g
    m1 = jnp.mean(dxh, axis=-1, keepdims=True)
    m2 = jnp.mean(dxh * xhat, axis=-1, keepdims=True)
    return rstd * (dxh - m1 - xhat * m2)


def _row_call(body, ins, row_ins, outs, acc_outs, name, tr=256):
    S = ins[0].shape[0]
    tr = min(tr, S)
    n_in, n_row, n_out = len(ins), len(row_ins), len(outs)

    def wrapped(*refs):
        i = pl.program_id(0)
        acc_refs = refs[n_in + n_row + n_out:]

        @pl.when(i == 0)
        def _():
            for r in acc_refs:
                r[...] = jnp.zeros_like(r)

        body(*refs)

    in_specs = [pl.BlockSpec((tr, a.shape[1]), lambda i: (i, 0)) for a in ins]
    in_specs += [pl.BlockSpec(a.shape, lambda i: (0, 0)) for a in row_ins]
    out_specs = [pl.BlockSpec((tr, s.shape[1]), lambda i: (i, 0)) for s in outs]
    out_specs += [pl.BlockSpec(s.shape, lambda i: (0, 0)) for s in acc_outs]
    return pl.pallas_call(wrapped, grid=(S // tr,), in_specs=in_specs, out_specs=out_specs,
                          out_shape=list(outs) + list(acc_outs),
                          compiler_params=_params(("arbitrary",)), name=name)(*ins, *row_ins)


def _ln1_fwd(x, mixed, g, b):
    def body(x_ref, m_ref, g_ref, b_ref, h_ref):
        xhat, _ = _ln_stats(DN_ALPHA * x_ref[...] + m_ref[...])
        h_ref[...] = xhat * g_ref[...] + b_ref[...]
    return _row_call(body, [x, mixed], [g, b], [jax.ShapeDtypeStruct(x.shape, F32)], [], "ln1_fwd")[0]


def _ln2_loss(h1, mlp, g, b, target):
    S, D = h1.shape
    sds = jax.ShapeDtypeStruct

    def body(h_ref, m_ref, t_ref, g_ref, b_ref, dr_ref, loss_ref, dg_ref, db_ref):
        xhat, rstd = _ln_stats(DN_ALPHA * h_ref[...] + m_ref[...])
        gv = g_ref[...]
        err = xhat * gv + b_ref[...] - t_ref[...]
        loss_ref[...] += jnp.sum(jnp.sum(err * err, axis=0, keepdims=True), axis=1, keepdims=True) * (0.5 / D)
        dy = err * (1.0 / D)
        dg_ref[...] += jnp.sum(dy * xhat, axis=0, keepdims=True)
        db_ref[...] += jnp.sum(dy, axis=0, keepdims=True)
        dr_ref[...] = _ln_bwd(dy, xhat, rstd, gv)

    return _row_call(body, [h1, mlp, target], [g, b], [sds((S, D), F32)],
                     [sds((1, LANE), F32), sds((1, D), F32), sds((1, D), F32)], "ln2_loss")


def _ln1_bwd(x, mixed, g, dr2, dh_mlp):
    S, D = x.shape
    sds = jax.ShapeDtypeStruct

    def body(x_ref, m_ref, dr2_ref, dh_ref, g_ref, dr_ref, dg_ref, db_ref):
        xhat, rstd = _ln_stats(DN_ALPHA * x_ref[...] + m_ref[...])
        dy = DN_ALPHA * dr2_ref[...] + dh_ref[...]
        dg_ref[...] += jnp.sum(dy * xhat, axis=0, keepdims=True)
        db_ref[...] += jnp.sum(dy, axis=0, keepdims=True)
        dr_ref[...] = _ln_bwd(dy, xhat, rstd, g_ref[...])

    return _row_call(body, [x, mixed, dr2, dh_mlp], [g], [sds((S, D), F32)],
                     [sds((1, D), F32), sds((1, D), F32)], "ln1_bwd")


def _grad_x(dr1, dx_proj, dep):
    def body(a_ref, b_ref, dep_ref, o_ref):
        o_ref[...] = DN_ALPHA * a_ref[...] + b_ref[...]
    return _row_call(body, [dr1, dx_proj], [dep], [jax.ShapeDtypeStruct(dr1.shape, F32)], [], "grad_x")[0]


def _bucket_table():
    qi = np.arange(BLK, dtype=np.int32)[:, None]
    kj = np.arange(2 * BLK, dtype=np.int32)[None, :]
    dist = qi + BLK - kj
    n = np.maximum(dist, 0)
    max_exact = N_BUCKETS // 2
    nf = np.maximum(n, 1).astype(np.float32)
    large = max_exact + (np.log(nf / np.float32(max_exact)) / np.float32(math.log(MAX_DISTANCE / max_exact))
                         * np.float32(N_BUCKETS - max_exact)).astype(np.int32)
    large = np.minimum(large, N_BUCKETS - 1)
    bucket = np.where(n < max_exact, n, large)
    return np.where((dist >= 0) & (dist < BLK), bucket, -1).astype(np.int32)


def _attn_bias(rel_bias_t):
    hq = rel_bias_t.shape[0]
    bucket = jnp.asarray(_bucket_table())

    def body(rb_ref, bk_ref, o_ref):
        h = pl.program_id(0)
        bk = bk_ref[...]
        acc = jnp.zeros((BLK, 2 * BLK), F32)
        for b in range(N_BUCKETS):
            acc = jnp.where(bk == b, rb_ref[h, b], acc)
        o_ref[...] = acc

    return pl.pallas_call(
        body, grid=(hq,),
        in_specs=[pl.BlockSpec(memory_space=pltpu.SMEM), pl.BlockSpec((BLK, 2 * BLK), lambda h: (0, 0))],
        out_specs=pl.BlockSpec((BLK, 2 * BLK), lambda h: (h, 0)),
        out_shape=jax.ShapeDtypeStruct((hq * BLK, 2 * BLK), F32),
        compiler_params=_params(("arbitrary",)), name="attn_bias")(rel_bias_t, bucket)


def _attn_probs(sc, sp, bias, sink, mask_c, mask_p):
    lc = jnp.where(mask_c, sc + bias[:, BLK:], NEG_INF)
    lp = jnp.where(mask_p, sp + bias[:, :BLK], NEG_INF)
    m = jnp.maximum(jnp.maximum(jnp.max(lc, axis=1, keepdims=True), jnp.max(lp, axis=1, keepdims=True)), sink)
    pc, pp, ps = jnp.exp(lc - m), jnp.exp(lp - m), jnp.exp(sink - m)
    inv = 1.0 / (jnp.sum(pc, axis=1, keepdims=True) + jnp.sum(pp, axis=1, keepdims=True) + ps)
    return pc, pp, ps, inv


def _attn_masks(n):
    qi = lax.broadcasted_iota(jnp.int32, (BLK, BLK), 0)
    kj = lax.broadcasted_iota(jnp.int32, (BLK, BLK), 1)
    return kj <= qi, (kj > qi) & (n > 0)


def _attn_fwd(proj, bias, sinks, hq, q_blk, k_blk, v_blk, out_width):
    S = proj.shape[0]
    hkv = hq // GQA
    wq, wk = hq * HEAD_A, hkv * HEAD_A

    def body(q_ref, k_ref, v_ref, bias_ref, sink_ref, o_ref):
        n = pl.program_id(0)
        cur = pl.multiple_of(n * BLK, BLK)
        prev = pl.multiple_of(jnp.maximum(n - 1, 0) * BLK, BLK)
        mask_c, mask_p = _attn_masks(n)
        for h4 in range(hkv):
            cs = slice(h4 * HEAD_A, (h4 + 1) * HEAD_A)
            kc, kp = k_ref[pl.ds(cur, BLK), cs].astype(BF16), k_ref[pl.ds(prev, BLK), cs].astype(BF16)
            vc, vp = v_ref[pl.ds(cur, BLK), cs].astype(BF16), v_ref[pl.ds(prev, BLK), cs].astype(BF16)
            hs_of = [slice(h * HEAD_A, (h + 1) * HEAD_A) for h in range(h4 * GQA, (h4 + 1) * GQA)]
            qs = [(q_ref[:, hs] * (HEAD_A ** -0.5)).astype(BF16) for hs in hs_of]
            scs = [_dot(q, kc, NT) for q in qs]
            sps = [_dot(q, kp, NT) for q in qs]
            pr = [_attn_probs(scs[g], sps[g], bias_ref[(h4 * GQA + g) * BLK:(h4 * GQA + g + 1) * BLK, :],
                              sink_ref[h4 * GQA + g], mask_c, mask_p) for g in range(GQA)]
            oc = [_dot(p[0].astype(BF16), vc, NN) for p in pr]
            op = [_dot(p[1].astype(BF16), vp, NN) for p in pr]
            for g, hs in enumerate(hs_of):
                o_ref[:, hs] = (oc[g] + op[g]) * pr[g][3]

    return pl.pallas_call(
        body, grid=(S // BLK,),
        in_specs=[pl.BlockSpec((BLK, wq), lambda n: (n, q_blk)), pl.BlockSpec((S, wk), lambda n: (0, k_blk)),
                  pl.BlockSpec((S, wk), lambda n: (0, v_blk)), pl.BlockSpec((hq * BLK, 2 * BLK), lambda n: (0, 0)),
                  pl.BlockSpec(memory_space=pltpu.SMEM)],
        out_specs=pl.BlockSpec((BLK, wq), lambda n: (n, 0)),
        out_shape=jax.ShapeDtypeStruct((S, out_width), F32),
        compiler_params=_params(("arbitrary",)), name="attn_fwd")(proj, proj, proj, bias, sinks)


def _attn_bwd(proj, bias, sinks, out, dmix, hq, q_blk, k_blk, v_blk):
    S = proj.shape[0]
    hkv = hq // GQA
    wq, wk = hq * HEAD_A, hkv * HEAD_A
    sds = jax.ShapeDtypeStruct

    def body(q_ref, k_ref, v_ref, bias_ref, sink_ref, o_ref, do_ref, dq_ref, dk_ref, dv_ref, dbias_ref, dsink_ref):
        n = pl.program_id(0)

        @pl.when(n == 0)
        def _():
            dk_ref[...] = jnp.zeros_like(dk_ref)
            dv_ref[...] = jnp.zeros_like(dv_ref)
            dbias_ref[...] = jnp.zeros_like(dbias_ref)
            dsink_ref[...] = jnp.zeros_like(dsink_ref)

        cur = pl.multiple_of(n * BLK, BLK)
        prev = pl.multiple_of(jnp.maximum(n - 1, 0) * BLK, BLK)
        mask_c, mask_p = _attn_masks(n)
        for h4 in range(hkv):
            cs = slice(h4 * HEAD_A, (h4 + 1) * HEAD_A)
            kc, kp = k_ref[pl.ds(cur, BLK), cs].astype(BF16), k_ref[pl.ds(prev, BLK), cs].astype(BF16)
            vc, vp = v_ref[pl.ds(cur, BLK), cs].astype(BF16), v_ref[pl.ds(prev, BLK), cs].astype(BF16)
            heads = list(range(h4 * GQA, (h4 + 1) * GQA))
            hs_of = [slice(h * HEAD_A, (h + 1) * HEAD_A) for h in heads]
            rows_of = [slice(h * BLK, (h + 1) * BLK) for h in heads]
            G = range(GQA)
            qs = [(q_ref[:, hs] * (HEAD_A ** -0.5)).astype(BF16) for hs in hs_of]
            dos = [do_ref[:, hs] for hs in hs_of]
            dobs = [d.astype(BF16) for d in dos]
            scs = [_dot(q, kc, NT) for q in qs]
            sps = [_dot(q, kp, NT) for q in qs]
            dpc = [_dot(d, vc, NT) for d in dobs]
            dpp = [_dot(d, vp, NT) for d in dobs]
            pcs, pps, dscs, dsps = [], [], [], []
            for g in G:
                pc, pp, ps, inv = _attn_probs(scs[g], sps[g], bias_ref[rows_of[g], :], sink_ref[heads[g]], mask_c, mask_p)
                pc, pp, ps = pc * inv, pp * inv, ps * inv
                delta = jnp.sum(dos[g] * o_ref[:, hs_of[g]], axis=1, keepdims=True)
                dsc, dsp = pc * (dpc[g] - delta), pp * (dpp[g] - delta)
                dsink_ref[heads[g]:heads[g] + 1, :] += jnp.broadcast_to(jnp.sum(-ps * delta, axis=0, keepdims=True), (1, LANE))
                dbias_ref[rows_of[g], BLK:] += dsc
                dbias_ref[rows_of[g], :BLK] += dsp
                pcs.append(pc.astype(BF16))
                pps.append(pp.astype(BF16))
                dscs.append(dsc.astype(BF16))
                dsps.append(dsp.astype(BF16))
            dq1 = [_dot(dscs[g], kc, NN) for g in G]
            dq2 = [_dot(dsps[g], kp, NN) for g in G]
            dkc = [_dot(dscs[g], qs[g], TN) for g in G]
            dkp = [_dot(dsps[g], qs[g], TN) for g in G]
            dvc = [_dot(pcs[g], dobs[g], TN) for g in G]
            dvp = [_dot(pps[g], dobs[g], TN) for g in G]
            for g in G:
                dq_ref[:, hs_of[g]] = (dq1[g] + dq2[g]) * (HEAD_A ** -0.5)
            dk_ref[pl.ds(cur, BLK), cs] += sum(dkc[1:], dkc[0])
            dk_ref[pl.ds(prev, BLK), cs] += sum(dkp[1:], dkp[0])
            dv_ref[pl.ds(cur, BLK), cs] += sum(dvc[1:], dvc[0])
            dv_ref[pl.ds(prev, BLK), cs] += sum(dvp[1:], dvp[0])

    return pl.pallas_call(
        body, grid=(S // BLK,),
        in_specs=[pl.BlockSpec((BLK, wq), lambda n: (n, q_blk)), pl.BlockSpec((S, wk), lambda n: (0, k_blk)),
                  pl.BlockSpec((S, wk), lambda n: (0, v_blk)), pl.BlockSpec((hq * BLK, 2 * BLK), lambda n: (0, 0)),
                  pl.BlockSpec(memory_space=pltpu.SMEM),
                  pl.BlockSpec((BLK, wq), lambda n: (n, 0)), pl.BlockSpec((BLK, wq), lambda n: (n, 0))],
        out_specs=[pl.BlockSpec((BLK, wq), lambda n: (n, 0)), pl.BlockSpec((S, wk), lambda n: (0, 0)),
                   pl.BlockSpec((S, wk), lambda n: (0, 0)), pl.BlockSpec((hq * BLK, 2 * BLK), lambda n: (0, 0)),
                   pl.BlockSpec((hq, LANE), lambda n: (0, 0))],
        out_shape=[sds((S, wq), F32), sds((S, wk), F32), sds((S, wk), F32), sds((hq * BLK, 2 * BLK), F32),
                   sds((hq, LANE), F32)],
        compiler_params=_params(("arbitrary",)), name="attn_bwd")(proj, proj, proj, bias, sinks, out, dmix)


def _rel_bias_grad(dbias, hq):
    bucket = jnp.asarray(_bucket_table())

    def body(d_ref, bk_ref, o_ref):
        d = d_ref[...]
        bk = bk_ref[...]
        rows = [jnp.sum(jnp.where(bk == b, d, 0.0), axis=0, keepdims=True) for b in range(N_BUCKETS)]
        tot = jnp.sum(jnp.concatenate(rows, axis=0), axis=1, keepdims=True)
        o_ref[...] = jnp.broadcast_to(tot, (N_BUCKETS, LANE))

    return pl.pallas_call(
        body, grid=(hq,),
        in_specs=[pl.BlockSpec((BLK, 2 * BLK), lambda h: (h, 0)), pl.BlockSpec((BLK, 2 * BLK), lambda h: (0, 0))],
        out_specs=pl.BlockSpec((None, N_BUCKETS, LANE), lambda h: (h, 0, 0)),
        out_shape=jax.ShapeDtypeStruct((hq, N_BUCKETS, LANE), F32),
        compiler_params=_params(("arbitrary",)), name="rel_bias_grad")(dbias, bucket)


def _sigmoid(x):
    return 1.0 / (1.0 + jnp.exp(-x))


def _shift_rows(x, s):
    n = x.shape[0]
    row = lax.broadcasted_iota(jnp.int32, x.shape, 0)
    if s > 0:
        return jnp.where(row >= s, pltpu.roll(x, s, 0), 0.0)
    return jnp.where(row < n + s, pltpu.roll(x, n + s, 0), 0.0)


def _conv_silu_norm(xv, w, j, nh):
    c = w[CONV_W - 1:CONV_W, :] * xv
    for s in range(1, CONV_W):
        c = c + w[CONV_W - 1 - s:CONV_W - s, :] * _shift_rows(xv, s)
    sg = _sigmoid(c)
    a = c * sg
    r = lax.rsqrt(jnp.sum(a * a, axis=1, keepdims=True) + RMS_EPS)
    scale = jnp.where(j < nh, HEAD_D ** -0.5, 1.0)
    is_norm = j < 2 * nh
    y = jnp.where(is_norm, a * (r * scale), a)
    return c, sg, a, r, scale, is_norm, y


def _gdn_prep_fwd(proj, conv_w, nh, blk0):
    S = proj.shape[0]

    def body(x_ref, w_ref, o_ref):
        j = pl.program_id(0)
        o_ref[...] = _conv_silu_norm(x_ref[...], w_ref[...], j, nh)[-1]

    return pl.pallas_call(
        body, grid=(3 * nh,),
        in_specs=[pl.BlockSpec((S, HEAD_D), lambda j: (0, blk0 + j)), pl.BlockSpec((CONV_W, HEAD_D), lambda j: (0, j))],
        out_specs=pl.BlockSpec((S, HEAD_D), lambda j: (0, 3 * (j % nh) + j // nh)),
        out_shape=jax.ShapeDtypeStruct((S, 3 * nh * HEAD_D), F32),
        compiler_params=_params(("parallel",)), name="gdn_prep_fwd")(proj, conv_w)


def _gdn_prep_bwd(proj, conv_w, dqkv, nh, blk0):
    S = proj.shape[0]
    sds = jax.ShapeDtypeStruct

    def body(x_ref, w_ref, dy_ref, dx_ref, dw_ref):
        j = pl.program_id(0)
        xv, w = x_ref[...], w_ref[...]
        c, sg, a, r, scale, is_norm, _ = _conv_silu_norm(xv, w, j, nh)
        dy = dy_ref[...]
        rs = r * scale
        da_n = rs * dy - a * (r * r * rs) * jnp.sum(dy * a, axis=1, keepdims=True)
        da = jnp.where(is_norm, da_n, dy)
        dc = da * (sg * (1.0 + c * (1.0 - sg)))
        dx = w[CONV_W - 1:CONV_W, :] * dc
        dws = [jnp.sum(dc * xv, axis=0, keepdims=True)]
        for s in range(1, CONV_W):
            dx = dx + w[CONV_W - 1 - s:CONV_W - s, :] * _shift_rows(dc, -s)
            dws.insert(0, jnp.sum(dc * _shift_rows(xv, s), axis=0, keepdims=True))
        dx_ref[...] = dx
        dw_ref[...] = jnp.concatenate(dws, axis=0)

    return pl.pallas_call(
        body, grid=(3 * nh,),
        in_specs=[pl.BlockSpec((S, HEAD_D), lambda j: (0, blk0 + j)), pl.BlockSpec((CONV_W, HEAD_D), lambda j: (0, j)),
                  pl.BlockSpec((S, HEAD_D), lambda j: (0, 3 * (j % nh) + j // nh))],
        out_specs=[pl.BlockSpec((S, HEAD_D), lambda j: (0, j)), pl.BlockSpec((CONV_W, HEAD_D), lambda j: (0, j))],
        out_shape=[sds((S, 3 * nh * HEAD_D), F32), sds((CONV_W, 3 * nh * HEAD_D), F32)],
        compiler_params=_params(("parallel",)), name="gdn_prep_bwd")(proj, conv_w, dqkv)


def _softplus(x):
    return jnp.maximum(x, 0.0) + jnp.log(1.0 + jnp.exp(-jnp.abs(x)))


def _gates_fwd(ab, al, dt, nh):
    S = ab.shape[0]

    def body(ab_ref, al_ref, dt_ref, o_ref):
        v = ab_ref[...]
        lane = lax.broadcasted_iota(jnp.int32, v.shape, 1)
        g = -jnp.exp(al_ref[...]) * _softplus(v + dt_ref[...])
        o_ref[...] = jnp.where(lane < nh, g, jnp.where(lane < 2 * nh, _sigmoid(v), 0.0))

    row = pl.BlockSpec((1, LANE), lambda i: (0, 0))
    full = pl.BlockSpec((S, LANE), lambda i: (0, 0))
    return pl.pallas_call(body, grid=(1,), in_specs=[full, row, row], out_specs=full,
                          out_shape=jax.ShapeDtypeStruct((S, LANE), F32),
                          compiler_params=_params(("arbitrary",)), name="gates_fwd")(ab, al, dt)


def _gates_bwd(ab, al, dt, dgb, nh):
    S = ab.shape[0]
    sds = jax.ShapeDtypeStruct

    def body(ab_ref, al_ref, dt_ref, d_ref, dab_ref, dal_ref, ddt_ref):
        v, d = ab_ref[...], d_ref[...]
        lane = lax.broadcasted_iota(jnp.int32, v.shape, 1)
        is_a = lane < nh
        z = v + dt_ref[...]
        dsp = jnp.where(is_a, d * (-jnp.exp(al_ref[...])), 0.0)
        dz = dsp * _sigmoid(z)
        beta = _sigmoid(v)
        dab_ref[...] = jnp.where(is_a, dz, jnp.where(lane < 2 * nh, d * beta * (1.0 - beta), 0.0))
        dal_ref[...] = jnp.sum(dsp * _softplus(z), axis=0, keepdims=True)
        ddt_ref[...] = jnp.sum(dz, axis=0, keepdims=True)

    row = pl.BlockSpec((1, LANE), lambda i: (0, 0))
    full = pl.BlockSpec((S, LANE), lambda i: (0, 0))
    return pl.pallas_call(body, grid=(1,), in_specs=[full, row, row, full], out_specs=[full, row, row],
                          out_shape=[sds((S, LANE), F32), sds((1, LANE), F32), sds((1, LANE), F32)],
                          compiler_params=_params(("arbitrary",)), name="gates_bwd")(ab, al, dt, dgb)


def _col_of(tile, h):
    lane = lax.broadcasted_iota(jnp.int32, tile.shape, 1)
    return jnp.sum(jnp.where(lane == h, tile, 0.0), axis=1, keepdims=True)


def _to_row(col, eye):
    return jnp.sum(jnp.where(eye, col, 0.0), axis=0, keepdims=True)


def _to_col(row, eye):
    return jnp.sum(jnp.where(eye, row, 0.0), axis=1, keepdims=True)


def _split(a):
    hi = a.astype(BF16)
    return hi, (a - hi.astype(F32)).astype(BF16)


def _gdot(a, b, dims):
    ah, al = _split(a)
    bh, bl = _split(b)
    return _dot(ah, bh, dims) + (_dot(ah, bl, dims) + _dot(al, bh, dims))


def _chunks_local(qs, ks, vs, gcols, bcols, Ts=None):
    C = CHUNK
    row = lax.broadcasted_iota(jnp.int32, (C, C), 0)
    col = lax.broadcasted_iota(jnp.int32, (C, C), 1)
    tril, strict, eye = col <= row, col < row, col == row
    outs = []
    for k, gcol, bcol in zip(ks, gcols, bcols):
        grow = _to_row(gcol, eye)
        G_row = jnp.sum(jnp.where(row <= col, gcol, 0.0), axis=0, keepdims=True)
        G_col = jnp.sum(jnp.where(tril, grow, 0.0), axis=1, keepdims=True)
        G_last = G_col[C - 1:C, :]
        outs.append(dict(strict=strict, eye=eye, row=row, col=col, decay=jnp.exp(jnp.where(tril, G_col - G_row, NEG_INF)),
                         eG=jnp.exp(G_col), eGr=jnp.exp(G_last - G_col), gl=jnp.exp(G_last), kb=k * bcol))
    Ms = [_gdot(o["kb"], k, NT) for o, k in zip(outs, ks)]
    Ns = [_gdot(q, k, NT) for q, k in zip(qs, ks)]
    for o, q, k, M, N in zip(outs, qs, ks, Ms, Ns):
        o.update(A=jnp.where(strict, M * o["decay"], 0.0), attn=N * o["decay"], rhs_k=o["kb"] * o["eG"],
                 q_dec=q * o["eG"], k_dec=k * o["eGr"])
    if Ts is None:
        Ts = [jnp.where(eye, 1.0, 0.0) - o["A"] for o in outs]
        Ps = [o["A"] for o in outs]
        for _ in range(int(math.log2(C)) - 1):
            Ps = [_gdot(P, P, NN) for P in Ps]
            Ts = [T + _gdot(T, P, NN) for T, P in zip(Ts, Ps)]
        us = [_gdot(T, v * bcol, NN) for T, v, bcol in zip(Ts, vs, bcols)]
        ws = [_gdot(T, o["rhs_k"], NN) for T, o in zip(Ts, outs)]
        for o, T, u, w in zip(outs, Ts, us, ws):
            o.update(T=T, u=u, w=w)
    return outs


GDN_ROWS = 256
GDN_LOCAL_ROWS = 512
WQK = 3 * CHUNK


def _gdn_local_fwd(qkv, gb, nh):
    S = qkv.shape[0]
    nc = S // CHUNK
    rb = min(GDN_LOCAL_ROWS, S)
    cpb = rb // CHUNK
    sds = jax.ShapeDtypeStruct

    def body(q_ref, k_ref, v_ref, gb_ref, u_ref, wqk_ref, attn_ref, t_ref):
        h = pl.program_id(0)
        rows_of = [slice(ci * CHUNK, (ci + 1) * CHUNK) for ci in range(cpb)]
        gbts = [gb_ref[rows, :] for rows in rows_of]
        Ls = _chunks_local([q_ref[rows, :] for rows in rows_of], [k_ref[rows, :] for rows in rows_of],
                           [v_ref[rows, :] for rows in rows_of], [_col_of(t, h) for t in gbts],
                           [_col_of(t, nh + h) for t in gbts])
        for ci, (rows, L) in enumerate(zip(rows_of, Ls)):
            u_ref[rows, :] = L["u"]
            base = ci * WQK
            wqk_ref[base:base + CHUNK, :] = L["w"]
            wqk_ref[base + CHUNK:base + 2 * CHUNK, :] = L["q_dec"]
            wqk_ref[base + 2 * CHUNK:base + WQK, :] = L["k_dec"]
            attn_ref[ci] = L["attn"]
            t_ref[ci] = L["T"]

    cc = pl.BlockSpec((None, cpb, CHUNK, CHUNK), lambda h, i: (h, i, 0, 0))
    return pl.pallas_call(
        body, grid=(nh, S // rb),
        in_specs=[pl.BlockSpec((rb, HEAD_D), lambda h, i: (i, 3 * h)), pl.BlockSpec((rb, HEAD_D), lambda h, i: (i, 3 * h + 1)),
                  pl.BlockSpec((rb, HEAD_D), lambda h, i: (i, 3 * h + 2)), pl.BlockSpec((rb, LANE), lambda h, i: (i, 0))],
        out_specs=[pl.BlockSpec((rb, HEAD_D), lambda h, i: (i, h)),
                   pl.BlockSpec((None, 3 * rb, HEAD_D), lambda h, i: (h, i, 0)), cc, cc],
        out_shape=[sds((S, nh * HEAD_D), F32), sds((nh, 3 * S, HEAD_D), F32), sds((nh, nc, CHUNK, CHUNK), F32),
                   sds((nh, nc, CHUNK, CHUNK), F32)],
        compiler_params=_params(("parallel", "parallel")), name="gdn_local_fwd")(qkv, qkv, qkv, gb)


def _gdn_scan_fwd(u, wqk, attn, gb, nh, dep):
    S = u.shape[0]
    nc = S // CHUNK
    rb = min(GDN_ROWS, S)
    cpb = rb // CHUNK
    sds = jax.ShapeDtypeStruct

    def body(u_ref, wqk_ref, attn_ref, gb_ref, dep_ref, o_ref, vn_ref, st_ref, s_ref):
        @pl.when(pl.program_id(0) == 0)
        def _():
            s_ref[...] = jnp.zeros_like(s_ref)

        for ci in range(cpb):
            rows = slice(ci * CHUNK, (ci + 1) * CHUNK)
            glv = jnp.exp(jnp.sum(gb_ref[rows, :], axis=0, keepdims=True))
            base = ci * WQK
            heads = range(nh)
            cols = [slice(h * HEAD_D, (h + 1) * HEAD_D) for h in heads]
            states = [s_ref[h] for h in heads]
            rs = [_gdot(wqk_ref[h, base:base + 2 * CHUNK, :], states[h], NN) for h in heads]
            vbs = [u_ref[rows, cols[h]] - rs[h][:CHUNK] for h in heads]
            os_ = [_gdot(attn_ref[h, ci], vbs[h], NN) for h in heads]
            ks_ = [_gdot(wqk_ref[h, base + 2 * CHUNK:base + WQK, :], vbs[h], TN) for h in heads]
            for h in heads:
                st_ref[h, ci] = states[h]
                o_ref[rows, cols[h]] = rs[h][CHUNK:] + os_[h]
                vn_ref[rows, cols[h]] = vbs[h]
                s_ref[h] = states[h] * glv[:, h:h + 1] + ks_[h]

    return pl.pallas_call(
        body, grid=(S // rb,),
        in_specs=[pl.BlockSpec((rb, nh * HEAD_D), lambda i: (i, 0)), pl.BlockSpec((nh, 3 * rb, HEAD_D), lambda i: (0, i, 0)),
                  pl.BlockSpec((nh, cpb, CHUNK, CHUNK), lambda i: (0, i, 0, 0)), pl.BlockSpec((rb, LANE), lambda i: (i, 0)),
                  pl.BlockSpec(memory_space=pl.ANY)],
        out_specs=[pl.BlockSpec((rb, nh * HEAD_D), lambda i: (i, 0)), pl.BlockSpec((rb, nh * HEAD_D), lambda i: (i, 0)),
                   pl.BlockSpec((nh, cpb, HEAD_D, HEAD_D), lambda i: (0, i, 0, 0))],
        out_shape=[sds((S, nh * HEAD_D), F32), sds((S, nh * HEAD_D), F32), sds((nh, nc, HEAD_D, HEAD_D), F32)],
        scratch_shapes=[pltpu.VMEM((nh, HEAD_D, HEAD_D), F32)],
        compiler_params=_params(("arbitrary",)), name="gdn_scan_fwd")(u, wqk, attn, gb, dep)


def _gdn_scan_bwd(wqk, attn, gb, states, vn, do, nh, dep):
    S = vn.shape[0]
    nc = S // CHUNK
    rb = min(GDN_ROWS, S)
    cpb = rb // CHUNK
    last = S // rb - 1
    sds = jax.ShapeDtypeStruct

    def body(wqk_ref, attn_ref, gb_ref, st_ref, vn_ref, do_ref, dep_ref, dvn_ref, dw_ref, dqd_ref, dkd_ref, da_ref, dgl_ref,
             ds_ref):
        @pl.when(pl.program_id(0) == 0)
        def _():
            ds_ref[...] = jnp.zeros_like(ds_ref)

        row = lax.broadcasted_iota(jnp.int32, (CHUNK, CHUNK), 0)
        col = lax.broadcasted_iota(jnp.int32, (CHUNK, CHUNK), 1)
        for ci in reversed(range(cpb)):
            rows = slice(ci * CHUNK, (ci + 1) * CHUNK)
            glv = jnp.exp(jnp.sum(gb_ref[rows, :], axis=0, keepdims=True))
            base = ci * WQK
            heads = range(nh)
            cols = [slice(h * HEAD_D, (h + 1) * HEAD_D) for h in heads]
            states = [st_ref[h, ci] for h in heads]
            dSs = [ds_ref[h] for h in heads]
            vbs = [vn_ref[rows, cols[h]] for h in heads]
            dobs = [do_ref[rows, cols[h]] for h in heads]
            dv1 = [_gdot(attn_ref[h, ci], dobs[h], TN) for h in heads]
            dv2 = [_gdot(wqk_ref[h, base + 2 * CHUNK:base + WQK, :], dSs[h], NN) for h in heads]
            das = [_gdot(dobs[h], vbs[h], NT) for h in heads]
            dkds = [_gdot(vbs[h], dSs[h], NT) for h in heads]
            dvbs = [dv1[h] + dv2[h] for h in heads]
            xs = [_gdot(jnp.concatenate([dobs[h], dvbs[h]], axis=0), states[h], NT) for h in heads]
            dss = [_gdot(wqk_ref[h, base:base + 2 * CHUNK, :], jnp.concatenate([-dvbs[h], dobs[h]], axis=0), TN)
                   for h in heads]
            for h in heads:
                dqd_ref[rows, cols[h]] = xs[h][:CHUNK]
                dw_ref[rows, cols[h]] = -xs[h][CHUNK:]
                dvn_ref[rows, cols[h]] = dvbs[h]
                da_ref[h, ci] = jnp.where(col <= row, das[h], 0.0)
                dkd_ref[rows, cols[h]] = dkds[h]
                gl = glv[:, h:h + 1]
                dgl = jnp.sum(jnp.sum(states[h] * dSs[h], axis=0, keepdims=True), axis=1, keepdims=True)
                dgl_ref[h, ci] = jnp.broadcast_to(dgl * gl, (1, LANE))
                ds_ref[h] = dSs[h] * gl + dss[h]

    rv = lambda i: last - i
    wide = pl.BlockSpec((rb, nh * HEAD_D), lambda i: (rv(i), 0))
    return pl.pallas_call(
        body, grid=(S // rb,),
        in_specs=[pl.BlockSpec((nh, 3 * rb, HEAD_D), lambda i: (0, rv(i), 0)),
                  pl.BlockSpec((nh, cpb, CHUNK, CHUNK), lambda i: (0, rv(i), 0, 0)),
                  pl.BlockSpec((rb, LANE), lambda i: (rv(i), 0)),
                  pl.BlockSpec((nh, cpb, HEAD_D, HEAD_D), lambda i: (0, rv(i), 0, 0)), wide, wide,
                  pl.BlockSpec(memory_space=pl.ANY)],
        out_specs=[wide, wide, wide, wide, pl.BlockSpec((nh, cpb, CHUNK, CHUNK), lambda i: (0, rv(i), 0, 0)),
                   pl.BlockSpec((nh, cpb, 1, LANE), lambda i: (0, rv(i), 0, 0))],
        out_shape=[sds((S, nh * HEAD_D), F32), sds((S, nh * HEAD_D), F32), sds((S, nh * HEAD_D), F32),
                   sds((S, nh * HEAD_D), F32), sds((nh, nc, CHUNK, CHUNK), F32), sds((nh, nc, 1, LANE), F32)],
        scratch_shapes=[pltpu.VMEM((nh, HEAD_D, HEAD_D), F32)],
        compiler_params=_params(("arbitrary",)), name="gdn_scan_bwd")(wqk, attn, gb, states, vn, do, dep)


def _gdn_local_bwd(qkv, gb, T, u, wqk, dvn, dw, dqd, dkd, dattn, dgl, nh):
    S = qkv.shape[0]
    rb = min(GDN_LOCAL_ROWS, S)
    cpb = rb // CHUNK
    sds = jax.ShapeDtypeStruct

    def body(q_ref, k_ref, v_ref, gb_ref, t_ref, u_ref, wqk_ref, dvn_ref, dw_ref, dqd_ref, dkd_ref, da_ref, dgl_ref,
             dqkv_ref, dg_ref, db_ref):
        h = pl.program_id(0)
        n = range(cpb)
        rows_of = [slice(ci * CHUNK, (ci + 1) * CHUNK) for ci in n]
        qs, ks, vs = ([r[rows, :] for rows in rows_of] for r in (q_ref, k_ref, v_ref))
        gbts = [gb_ref[rows, :] for rows in rows_of]
        bcols = [_col_of(t, nh + h) for t in gbts]
        Ts = [t_ref[ci] for ci in n]
        Ls = _chunks_local(qs, ks, vs, [_col_of(t, h) for t in gbts], bcols, Ts=Ts)
        drvs = [_gdot(Ts[ci], dvn_ref[rows_of[ci], :], TN) for ci in n]
        drks = [_gdot(Ts[ci], dw_ref[rows_of[ci], :], TN) for ci in n]
        dAs = [jnp.where(Ls[ci]["strict"], -(_gdot(drvs[ci], u_ref[rows_of[ci], :], NT)
                                             + _gdot(drks[ci], wqk_ref[ci * WQK:ci * WQK + CHUNK, :], NT)), 0.0) for ci in n]
        dMs = [dAs[ci] * Ls[ci]["decay"] for ci in n]
        dNs = [da_ref[ci] * Ls[ci]["decay"] for ci in n]
        dkbs = [_gdot(dMs[ci], ks[ci], NN) for ci in n]
        dq1 = [_gdot(dNs[ci], ks[ci], NN) for ci in n]
        dk1 = [_gdot(dMs[ci], Ls[ci]["kb"], TN) for ci in n]
        dk2 = [_gdot(dNs[ci], qs[ci], TN) for ci in n]
        for ci in n:
            rows, L, q, k, v, bcol = rows_of[ci], Ls[ci], qs[ci], ks[ci], vs[ci], bcols[ci]
            eye, eG, eGr = L["eye"], L["eG"], L["eGr"]
            drv, drk, dkb = drvs[ci], drks[ci], dkbs[ci]
            dq_dec, dk_dec, dattn_c = dqd_ref[rows, :], dkd_ref[rows, :], da_ref[ci]
            dqkv_ref[rows, :HEAD_D] = dq1[ci] + dq_dec * eG
            dqkv_ref[rows, HEAD_D:2 * HEAD_D] = drk * (bcol * eG) + dk1[ci] + dkb * bcol + dk2[ci] + dk_dec * eGr
            dqkv_ref[rows, 2 * HEAD_D:] = drv * bcol
            db_ref[rows, :] = (jnp.sum(drv * v, axis=1, keepdims=True) + jnp.sum(drk * k, axis=1, keepdims=True) * eG
                               + jnp.sum(dkb * k, axis=1, keepdims=True))
            E = dAs[ci] * L["A"] + dattn_c * L["attn"]
            kd = jnp.sum(dk_dec * L["k_dec"], axis=1, keepdims=True)
            dG = (jnp.sum(dq_dec * L["q_dec"], axis=1, keepdims=True) - kd
                  + jnp.sum(drk * L["rhs_k"], axis=1, keepdims=True)
                  + jnp.sum(E, axis=1, keepdims=True) - _to_col(jnp.sum(E, axis=0, keepdims=True), eye))
            d_last = jnp.sum(kd, axis=0, keepdims=True) + dgl_ref[ci][:, :1]
            dG = dG + jnp.where(L["row"][:, :1] == CHUNK - 1, d_last, 0.0)
            dg_ref[rows, :] = jnp.sum(jnp.where(L["col"] >= L["row"], _to_row(dG, eye), 0.0), axis=1, keepdims=True)

    hd = pl.BlockSpec((rb, HEAD_D), lambda h, i: (i, h))
    cc = pl.BlockSpec((None, cpb, CHUNK, CHUNK), lambda h, i: (h, i, 0, 0))
    col1 = pl.BlockSpec((None, rb, 1), lambda h, i: (h, i, 0))
    return pl.pallas_call(
        body, grid=(nh, S // rb),
        in_specs=[pl.BlockSpec((rb, HEAD_D), lambda h, i: (i, 3 * h)), pl.BlockSpec((rb, HEAD_D), lambda h, i: (i, 3 * h + 1)),
                  pl.BlockSpec((rb, HEAD_D), lambda h, i: (i, 3 * h + 2)), pl.BlockSpec((rb, LANE), lambda h, i: (i, 0)),
                  cc, hd, pl.BlockSpec((None, 3 * rb, HEAD_D), lambda h, i: (h, i, 0)), hd, hd, hd, hd, cc,
                  pl.BlockSpec((None, cpb, 1, LANE), lambda h, i: (h, i, 0, 0))],
        out_specs=[pl.BlockSpec((rb, 3 * HEAD_D), lambda h, i: (i, h)), col1, col1],
        out_shape=[sds((S, 3 * nh * HEAD_D), F32)] + [sds((nh, S, 1), F32)] * 2,
        compiler_params=_params(("parallel", "parallel")), name="gdn_local_bwd")(
            qkv, qkv, qkv, gb, T, u, wqk, dvn, dw, dqd, dkd, dattn, dgl)


def _gated_norm_fwd(o, proj, norm_w, nh, z_blk0, mix, m_blk0):
    S = o.shape[0]

    def body(o_ref, z_ref, w_ref, mix_ref, y_ref):
        ov, z = o_ref[...], z_ref[...]
        r = lax.rsqrt(jnp.mean(ov * ov, axis=1, keepdims=True) + RMS_EPS)
        y_ref[...] = ov * r * w_ref[...] * (z * _sigmoid(z))

    return pl.pallas_call(
        body, grid=(nh,),
        in_specs=[pl.BlockSpec((S, HEAD_D), lambda h: (0, h)), pl.BlockSpec((S, HEAD_D), lambda h: (0, z_blk0 + h)),
                  pl.BlockSpec((1, HEAD_D), lambda h: (0, 0)), pl.BlockSpec(memory_space=pl.ANY)],
        out_specs=pl.BlockSpec((S, HEAD_D), lambda h: (0, m_blk0 + h)),
        out_shape=jax.ShapeDtypeStruct(mix.shape, F32), input_output_aliases={3: 0},
        compiler_params=_params(("parallel",)), name="gated_norm_fwd")(o, proj, norm_w, mix)


def _gated_norm_bwd(o, proj, norm_w, dmix, nh, z_blk0, d_blk0):
    S = o.shape[0]
    sds = jax.ShapeDtypeStruct

    def body(o_ref, z_ref, w_ref, dy_ref, do_ref, dz_ref, dw_ref):
        ov, z, w, dy = o_ref[...], z_ref[...], w_ref[...], dy_ref[...]
        r = lax.rsqrt(jnp.mean(ov * ov, axis=1, keepdims=True) + RMS_EPS)
        oh = ov * r
        sg = _sigmoid(z)
        dz_ref[...] = dy * (oh * w) * (sg * (1.0 + z * (1.0 - sg)))
        don = dy * (z * sg)
        @pl.when(pl.program_id(0) == 0)
        def _():
            dw_ref[...] = jnp.zeros_like(dw_ref)

        dw_ref[...] += jnp.sum(don * oh, axis=0, keepdims=True)
        doh = don * w
        do_ref[...] = r * (doh - oh * jnp.mean(doh * oh, axis=1, keepdims=True))

    return pl.pallas_call(
        body, grid=(nh,),
        in_specs=[pl.BlockSpec((S, HEAD_D), lambda h: (0, h)), pl.BlockSpec((S, HEAD_D), lambda h: (0, z_blk0 + h)),
                  pl.BlockSpec((1, HEAD_D), lambda h: (0, 0)), pl.BlockSpec((S, HEAD_D), lambda h: (0, d_blk0 + h))],
        out_specs=[pl.BlockSpec((S, HEAD_D), lambda h: (0, h)), pl.BlockSpec((S, HEAD_D), lambda h: (0, h)),
                   pl.BlockSpec((1, HEAD_D), lambda h: (0, 0))],
        out_shape=[sds((S, nh * HEAD_D), F32), sds((S, nh * HEAD_D), F32), sds((1, HEAD_D), F32)],
        compiler_params=_params(("arbitrary",)), name="gated_norm_bwd")(o, proj, norm_w, dmix)


def _adamw_math(w, g, m, v):
    m = ADAM_B1 * m + (1.0 - ADAM_B1) * g
    v = ADAM_B2 * v + (1.0 - ADAM_B2) * (g * g)
    m_hat = m / (1.0 - ADAM_B1 ** ADAM_STEP)
    v_hat = v / (1.0 - ADAM_B2 ** ADAM_STEP)
    delta = -ADAM_LR * (m_hat / (jnp.sqrt(v_hat) + ADAM_EPS) + ADAM_WD * w)
    return delta, m, v


def _slab_tiles(R, C, rows=256, cols=256):
    if R % rows == 0:
        return (rows, C), R // rows, lambda i: (i, 0)
    tc = _tile(C, cols)
    return (R, tc), C // tc, lambda i: (0, i)


def _adamw_big(parts, terms, chip, w, m, v, name):
    R, C = w.shape
    blk, steps, at = _slab_tiles(R, C)
    sds = jax.ShapeDtypeStruct

    def body(q_ref, p_ref, t_ref, w_ref, m_ref, v_ref, g_ref, d_ref, nm_ref, nv_ref):
        g = ((p_ref[...].astype(F32) + t_ref[0].astype(F32)) + t_ref[1].astype(F32)) + t_ref[2].astype(F32)
        g_ref[...] = g
        d_ref[...], nm_ref[...], nv_ref[...] = _adamw_math(w_ref[...], g, m_ref[...], v_ref[...])

    spec = pl.BlockSpec(blk, lambda i, q_ref: at(i))
    grid_spec = pltpu.PrefetchScalarGridSpec(
        num_scalar_prefetch=1, grid=(steps,),
        in_specs=[pl.BlockSpec((None,) + blk, lambda i, q_ref: (q_ref[0],) + at(i)),
                  pl.BlockSpec((3,) + blk, lambda i, q_ref: (0,) + at(i)), spec, spec, spec],
        out_specs=[spec] * 4)
    return pl.pallas_call(body, grid_spec=grid_spec, out_shape=[sds((R, C), F32)] * 4,
                          compiler_params=_params(("parallel",)), name=name)(chip, parts, terms, w, m, v)


def _adamw_small(ws, gs, ms, vs):
    n = len(ws)

    def body(*refs):
        for i in range(n):
            w, g, m, v = (refs[k * n + i][...] for k in range(4))
            d, nm, nv = _adamw_math(w, g, m, v)
            refs[4 * n + i][...] = d
            refs[5 * n + i][...] = nm
            refs[6 * n + i][...] = nv

    shapes = [jax.ShapeDtypeStruct(w.shape, F32) for w in ws]
    vm = pl.BlockSpec(memory_space=pltpu.VMEM)
    outs = pl.pallas_call(body, in_specs=[vm] * (4 * n), out_specs=[vm] * (3 * n), out_shape=shapes * 3,
                          name="adamw_small")(*ws, *gs, *ms, *vs)
    return outs[:n], outs[n:2 * n], outs[2 * n:]


MESH = pl.DeviceIdType.MESH
ANY = pl.BlockSpec(memory_space=pl.ANY)


def _place():
    x, y, c = lax.axis_index("x"), lax.axis_index("y"), lax.axis_index("c")
    return x, y, c, [(1 - x, y), (x, 1 - y), (1 - x, 1 - y)]


def _chip_sum(grad, recv, core, name):
    _, R, C = grad.shape
    blk, steps, at = _slab_tiles(R, C)

    def body(c_ref, g_ref, r_ref, o_ref):
        o_ref[...] = (g_ref[...].astype(F32) + r_ref[...].astype(F32)).astype(o_ref.dtype)

    grid_spec = pltpu.PrefetchScalarGridSpec(
        num_scalar_prefetch=1, grid=(4, steps),
        in_specs=[pl.BlockSpec((None,) + blk, lambda q, i, c_ref: (2 * q + c_ref[0],) + at(i)),
                  pl.BlockSpec((None,) + blk, lambda q, i, c_ref: (q,) + at(i))],
        out_specs=pl.BlockSpec((None,) + blk, lambda q, i, c_ref: (q,) + at(i)))
    return pl.pallas_call(body, grid_spec=grid_spec, out_shape=jax.ShapeDtypeStruct((4, R, C), BF16),
                          compiler_params=_params(("parallel", "parallel")), name=name)(core, grad, recv)


HBM_SPEC = pl.BlockSpec(memory_space=pltpu.HBM)
SEM_SPEC = pl.BlockSpec(memory_space=pltpu.SEMAPHORE)
DATAFLOW = pltpu.SideEffectType.DATAFLOW_SIDE_EFFECTING


def _split_start(name, bufs, plan, counts, after=None):
    nb, ng = len(bufs), len(counts)
    extra = [] if after is None else [after]
    place = [(g, k) for g, cnt in enumerate(counts) for k in range(cnt)]

    def body(*refs):
        sems, token = refs[nb + len(extra):nb + len(extra) + 2 * ng], refs[-1]
        for (g, k), (src, dst, to) in zip(place, plan(refs[:nb])):
            pltpu.make_async_remote_copy(src_ref=src, dst_ref=dst, send_sem=sems[2 * g].at[k], recv_sem=sems[2 * g + 1].at[k],
                                         device_id=to, device_id_type=MESH).start()
        token[...] = jnp.zeros_like(token)

    outs = pl.pallas_call(
        body, name=name,
        out_shape=(*[pltpu.SemaphoreType.DMA((cnt,)) for cnt in counts for _ in range(2)],
                   *[pltpu.HBM(b.shape, b.dtype) for b in bufs], jax.ShapeDtypeStruct((8, LANE), F32)),
        in_specs=[HBM_SPEC] * nb + [ANY] * len(extra),
        out_specs=(*[SEM_SPEC] * (2 * ng), *[HBM_SPEC] * nb, pl.BlockSpec(memory_space=pltpu.VMEM)),
        input_output_aliases={i: 2 * ng + i for i in range(nb)},
        compiler_params=pltpu.CompilerParams(has_side_effects=DATAFLOW))(
            *[pltpu.with_memory_space_constraint(b, pltpu.HBM) for b in bufs], *extra)
    return [(outs[2 * g], outs[2 * g + 1]) for g in range(ng)], list(outs[2 * ng:2 * ng + nb]), outs[-1]


def _split_wait(name, sems, bufs, plan, after):
    nb = len(bufs)
    send_sems, recv_sems = sems

    def body(*refs):
        send_s, recv_s = refs[nb], refs[nb + 1]
        for k, (src, dst, to) in enumerate(plan(refs[:nb])):
            cp = pltpu.make_async_remote_copy(src_ref=src, dst_ref=dst, send_sem=send_s.at[k], recv_sem=recv_s.at[k],
                                              device_id=to, device_id_type=MESH)
            cp.wait_send()
            cp.wait_recv()

    outs = pl.pallas_call(
        body, name=name, out_shape=tuple(pltpu.HBM(b.shape, b.dtype) for b in bufs),
        in_specs=[HBM_SPEC] * nb + [SEM_SPEC, SEM_SPEC, ANY], out_specs=tuple([HBM_SPEC] * nb),
        input_output_aliases={i: i for i in range(nb)},
        compiler_params=pltpu.CompilerParams(has_side_effects=DATAFLOW))(*bufs, send_sems, recv_sems, after)
    return list(outs)


def _slot(px, py, pc):
    return 4 * px + 2 * py + pc


class _Gather:
    def __init__(self, shards, groups, dev):
        self.shards, self.groups, self.dev = shards, groups, dev
        self.second = {}

    @staticmethod
    def _plan1(pairs, refs):
        x, y, c, chips = _place()
        out = []
        for s, l in pairs:
            dst = refs[l].at[_slot(x, y, c)]
            out.append((refs[s], dst, (x, y, 1 - c)))
            out += [(refs[s], dst, (px, py, c)) for px, py in chips]
        return out

    @staticmethod
    def _plan2(refs):
        x, y, c, chips = _place()
        return [(r.at[_slot(px, py, c)],) * 2 + ((x, y, 1 - c),) for r in refs for px, py in chips]

    def start(self):
        n = len(self.shards)
        lands = [lax.dynamic_update_slice(lax.empty((N_DEV,) + s.shape, s.dtype), s[None], (self.dev, 0, 0))
                 for s in self.shards]
        pairs = [(w, n + w) for g in self.groups for w in g]
        sems, bufs, token = _split_start("gather_start_1", list(self.shards) + lands, functools.partial(self._plan1, pairs),
                                         tuple(4 * len(g) for g in self.groups))
        self.first = [(sems[i], [bufs[w] for w in g], [bufs[n + w] for w in g]) for i, g in enumerate(self.groups)]
        return token

    def mid(self, gi, after):
        sems, srcs, lands = self.first[gi]
        m = len(srcs)
        plan = functools.partial(self._plan1, [(w, m + w) for w in range(m)])
        lands = _split_wait("gather_%d_wait_1" % gi, sems, srcs + lands, plan, after)[m:]
        sems, lands, token = _split_start("gather_%d_start_2" % gi, lands, self._plan2, (3 * m,))
        self.second[gi] = (sems[0], lands)
        return token

    def finish(self, gi, after):
        sems, lands = self.second[gi]
        return _split_wait("gather_%d_wait_2" % gi, sems, lands, self._plan2, after)


class _Exchanges:
    def __init__(self, tag, n, core, gather=None):
        self.tag, self.n, self.core, self.gather = tag, n, core, gather

    def weights_mid(self, group, after):
        return self.gather.mid(group, after)

    def weights_finish(self, group, after):
        return self.gather.finish(group, after)

    def _reduce_plan1(self, refs):
        n = self.n
        x, y, c, _ = _place()
        return [(refs[w].at[2 * q + (1 - c)], refs[n + w].at[q], (x, y, 1 - c)) for w in range(n) for q in range(4)]

    def _reduce_plan2(self, refs):
        n = self.n
        x, y, c, chips = _place()
        return [(refs[w].at[2 * px + py], refs[n + w].at[j], (px, py, c))
                for w in range(n) for j, (px, py) in enumerate(chips)]

    def grads_start(self, grads):
        lands = [lax.empty((4,) + g.shape[1:], g.dtype) for g in grads]
        self.g1 = _split_start(self.tag + "reduce_start_1", list(grads) + lands, self._reduce_plan1, (4 * self.n,))
        return self.g1[2]

    def grads_mid(self, after):
        n = self.n
        sems, bufs, _ = self.g1
        bufs = _split_wait(self.tag + "reduce_wait_1", sems[0], bufs, self._reduce_plan1, after)
        core = self.core.reshape(1).astype(jnp.int32)
        self.parts = [_chip_sum(bufs[w], bufs[n + w], core, self.tag + "reduce_chip_sum_%d" % w) for w in range(n)]
        lands = [lax.empty((3,) + p.shape[1:], p.dtype) for p in self.parts]
        self.g2 = _split_start(self.tag + "reduce_start_2", self.parts + lands, self._reduce_plan2, (3 * n,))
        return self.g2[2]

    def grads_finish(self, after):
        n = self.n
        sems, bufs, _ = self.g2
        bufs = _split_wait(self.tag + "reduce_wait_2", sems[0], bufs, self._reduce_plan2, after)
        self.parts, self.terms = bufs[:n], bufs[n:]


def _all_reduce_small(buf):
    R = buf.shape[0]

    def body(x_ref, o_ref, g_ref, send_sems, recv_sems):
        x, y, c, chips = _place()
        me, sibling = (x, y, c), (x, y, 1 - c)

        def slot(px, py, pc):
            return 4 * px + 2 * py + pc

        def copy(k, block, to, src=None):
            dst = g_ref.at[slot(*block)]
            return pltpu.make_async_remote_copy(src_ref=dst if src is None else src, dst_ref=dst,
                                                send_sem=send_sems.at[k], recv_sem=recv_sems.at[k],
                                                device_id=to, device_id_type=MESH)

        first = [copy(0, me, sibling, src=x_ref)]
        first += [copy(1 + j, me, (*chip, c), src=x_ref) for j, chip in enumerate(chips)]
        for cp in first:
            cp.start()
        g_ref[slot(*me)] = x_ref[...]
        passed = [copy(4 + j, (*chip, c), sibling) for j, chip in enumerate(chips)]
        for j, chip in enumerate(chips):
            copy(1 + j, (*chip, c), me).wait_recv()
            passed[j].start()
        copy(0, sibling, me).wait_recv()
        for j, chip in enumerate(chips):
            copy(4 + j, (*chip, 1 - c), me).wait_recv()
        for cp in first + passed:
            cp.wait_send()
        acc = g_ref[0]
        for s in range(1, N_DEV):
            acc = acc + g_ref[s]
        o_ref[...] = acc

    vm = pl.BlockSpec(memory_space=pltpu.VMEM)
    return pl.pallas_call(
        body, in_specs=[vm], out_specs=vm, out_shape=jax.ShapeDtypeStruct((R, LANE), F32),
        scratch_shapes=[pltpu.VMEM((N_DEV, R, LANE), F32), pltpu.SemaphoreType.DMA((7,)), pltpu.SemaphoreType.DMA((7,))],
        name="all_reduce_small")(buf)


def _pad_cols(a, width):
    return jnp.pad(a, ((0, 0), (0, width - a.shape[1])))


def _local_step(x, target, w_in_t, conv_w, a_log, dt_bias, delta_norm_w, sinks, rel_bias, ln1_g, ln1_b, ln2_g, ln2_b,
                ex, ex_in):
    S, D = x.shape
    aw = D // 2
    hq, hkv, nh = aw // HEAD_A, aw // HEAD_A // GQA, aw // HEAD_D
    kvw = hkv * HEAD_A
    c_q, c_k, c_v, c_d = 0, aw, aw + kvw, aw + 2 * kvw
    c_ab = c_d + 3 * aw
    c_z = c_ab + 2 * nh
    n_in = c_z + aw
    assert w_in_t.shape == (n_in, D), (w_in_t.shape, n_in)
    w_pt = jnp.concatenate([w_in_t[:c_ab], w_in_t[c_z:], jnp.pad(w_in_t[c_ab:c_z], ((0, LANE - 2 * nh), (0, 0)))], axis=0)
    p_z, p_ab = c_ab, c_ab + aw
    n_p = p_ab + LANE

    xb = x.astype(BF16)
    proj = _matmul(xb, w_pt, NT, name="proj", tn=1152)
    bias = _attn_bias(rel_bias.T)
    attn_out = _attn_fwd(proj, bias, sinks.reshape(-1), hq, 0, c_k // kvw, c_v // kvw, D)
    conv2 = conv_w.reshape(CONV_W, 3 * aw)
    qkv = _gdn_prep_fwd(proj, conv2, nh, c_d // HEAD_D)
    ab = proj[:, p_ab:]
    al, dt = _pad_cols(a_log, LANE), _pad_cols(dt_bias, LANE)
    gb = _gates_fwd(ab, al, dt, nh)
    u_d, wqk, attn_d, t_d = _gdn_local_fwd(qkv, gb, nh)
    o_d, vn, states = _gdn_scan_fwd(u_d, wqk, attn_d, gb, nh, ex.weights_mid(1, u_d))
    mix = _gated_norm_fwd(o_d, proj, delta_norm_w, nh, p_z // HEAD_D, attn_out, aw // HEAD_D)
    w_o_g, w_up_g = ex.weights_finish(1, mix)
    w_o = w_o_g.reshape(D, D)
    mixed = _matmul(mix, w_o, NN, name="out_proj")
    h1 = _ln1_fwd(x, mixed, ln1_g, ln1_b)
    u = _matmul(h1, w_up_g, NN, name="mlp_up", b_groups=True, out_dtype=BF16, deps=(ex.weights_mid(2, h1),))
    (w_down_g,) = ex.weights_finish(2, u)
    w_down = w_down_g.reshape(-1, D)
    mlp = _matmul(u, w_down, NN, name="mlp_down", a_fn=_relu_sq)
    dr2, loss_row, dln2_g, dln2_b = _ln2_loss(h1, mlp, ln2_g, ln2_b, target)

    du = _matmul(dr2, w_down, NT, name="d_mlp_act", epi=_relu_sq_grad, epi_in=(u,), out_dtype=BF16)
    dw_down = _matmul(u, dr2, TN, name="dw_down", a_fn=_relu_sq, out_dtype=BF16)
    dw_up = _matmul(h1, du, TN, name="dw_up", out_dtype=BF16, out_groups=N_DEV)
    dh_mlp = _matmul(du, w_up_g, NT, name="d_h1", b_groups=True)
    dr1, dln1_g, dln1_b = _ln1_bwd(x, mixed, ln1_g, dr2, dh_mlp)
    dw_o = _matmul(mix, dr1, TN, name="dw_o", out_dtype=BF16)
    tok = ex.grads_start([dw_o.reshape(N_DEV, -1, D), dw_up, dw_down.reshape(N_DEV, -1, D)])
    dmix = _matmul(dr1, w_o, NT, name="d_mix", deps=(tok,))
    dq_a, dk_a, dv_a, dbias, dsink = _attn_bwd(proj, bias, sinks.reshape(-1), mix, dmix, hq, 0, c_k // kvw, c_v // kvw)
    drel = _rel_bias_grad(dbias, hq)
    do_d, dz, dnw = _gated_norm_bwd(o_d, proj, delta_norm_w, dmix, nh, p_z // HEAD_D, aw // HEAD_D)
    dvn_s, dw_s, dqd, dkd, dattn_d, dgl = _gdn_scan_bwd(wqk, attn_d, gb, states, vn, do_d, nh, ex.grads_mid(dq_a))
    dqkv_n, dg, dbeta = _gdn_local_bwd(qkv, gb, t_d, u_d, wqk, dvn_s, dw_s, dqd, dkd, dattn_d, dgl, nh)
    dgb = _pad_cols(jnp.concatenate([dg.reshape(nh, S).T, dbeta.reshape(nh, S).T], axis=1), LANE)
    dab, da_log, ddt_bias = _gates_bwd(ab, al, dt, dgb, nh)
    dqkv_d, dconv = _gdn_prep_bwd(proj, conv2, dqkv_n, nh, c_d // HEAD_D)
    dproj = jnp.concatenate([dq_a, dk_a, dv_a, dqkv_d, dz, dab], axis=1)
    dw_pt = _matmul(dproj, xb, TN, name="dw_in", out_dtype=BF16, tm=1152)
    dw_in_t = jnp.concatenate([dw_pt[:p_z], dw_pt[p_ab:p_ab + 2 * nh], dw_pt[p_z:p_ab]], axis=0)
    tok = ex_in.grads_mid(ex_in.grads_start([dw_in_t.reshape(N_DEV, -1, D)]))
    dx_proj = _matmul(dproj, w_pt, NN, name="d_x", tk=1920, deps=(tok,))
    grad_x = _grad_x(dr1, dx_proj, tok)
    ex.grads_finish(grad_x)

    small = dict(conv_w=dconv, a_log=da_log[:, :nh], dt_bias=ddt_bias[:, :nh], delta_norm_w=dnw,
                 attn_sinks=dsink[:, 0].reshape(1, hq), rel_bias=drel[:, :, 0].T,
                 ln1_g=dln1_g, ln1_b=dln1_b, ln2_g=dln2_g, ln2_b=dln2_b)
    return loss_row, grad_x, small


SMALL_ORDER = ("conv_w", "a_log", "dt_bias", "delta_norm_w", "attn_sinks", "rel_bias", "ln1_g", "ln1_b", "ln2_g", "ln2_b")


def _pack_small(loss_row, small):
    parts = [loss_row.reshape(-1)]
    for k in SMALL_ORDER:
        flat = small[k].reshape(-1)
        parts.append(jnp.pad(flat, (0, (-flat.shape[0]) % LANE)))
    flat = jnp.concatenate(parts)
    flat = jnp.pad(flat, (0, (-flat.shape[0]) % (8 * LANE)))
    return flat.reshape(-1, LANE)


def _unpack_small(buf, small_shapes):
    flat = buf.reshape(-1)
    loss = flat[0]
    off = LANE
    out = {}
    for k in SMALL_ORDER:
        n = int(np.prod(small_shapes[k]))
        out[k] = flat[off:off + n].reshape(small_shapes[k])
        off += n + (-n) % LANE
    return loss, out


def kernel(x, w_in, conv_w, a_log, dt_bias, delta_norm_w, attn_sinks, rel_bias, w_o, ln1_g, ln1_b, w_up, w_down, ln2_g, ln2_b, loss_target, m_w_in, m_conv_w, m_a_log, m_dt_bias, m_delta_norm_w, m_attn_sinks, m_rel_bias, m_w_o, m_ln1_g, m_ln1_b, m_w_up, m_w_down, m_ln2_g, m_ln2_b, v_w_in, v_conv_w, v_a_log, v_dt_bias, v_delta_norm_w, v_attn_sinks, v_rel_bias, v_w_o, v_ln1_g, v_ln1_b, v_w_up, v_w_down, v_ln2_g, v_ln2_b):
    S, D = x.shape[1], x.shape[2]
    core = lax.axis_index("c")
    dev = 4 * lax.axis_index("x") + 2 * lax.axis_index("y") + core

    gather = _Gather([conv_w[0, :, 0, :], w_in[0].T.astype(BF16), w_o[0].astype(BF16), w_up[0].astype(BF16),
                      w_down[0].astype(BF16)], [[0, 1], [2, 3], [4]], dev)
    conv_g, w_in_g = gather.finish(0, gather.mid(0, gather.start()))
    w_in_t = w_in_g.reshape(-1, D)
    ex = _Exchanges("", 3, core, gather)
    ex_in = _Exchanges("in_", 1, core)

    cw_sh = conv_w.shape[3]
    conv_full = jnp.transpose(conv_g, (1, 0, 2)).reshape(CONV_W, N_DEV * cw_sh)

    loss_row, grad_x, small = _local_step(
        x[0], loss_target[0], w_in_t, conv_full, a_log, dt_bias, delta_norm_w, attn_sinks, rel_bias,
        ln1_g, ln1_b, ln2_g, ln2_b, ex, ex_in)

    chip_arr = (dev // 2).reshape(1).astype(jnp.int32)
    big = {}
    for i, (name, w, m, v) in enumerate((("w_o", w_o, m_w_o, v_w_o), ("w_up", w_up, m_w_up, v_w_up),
                                         ("w_down", w_down, m_w_down, v_w_down))):
        big[name] = [o[None] for o in _adamw_big(ex.parts[i], ex.terms[i], chip_arr, w[0], m[0], v[0], "adamw_" + name)]
    ex_in.grads_finish(big["w_down"][0])
    outs = _adamw_big(ex_in.parts[0], ex_in.terms[0], chip_arr, w_in[0].T, m_w_in[0].T, v_w_in[0].T, "adamw_w_in")
    big["w_in"] = [o.T[None] for o in outs]

    small_shapes = {k: v.shape for k, v in small.items()}
    loss, small = _unpack_small(_all_reduce_small(_pack_small(loss_row, small)), small_shapes)
    small["conv_w"] = lax.dynamic_slice(small["conv_w"], (0, dev * cw_sh), (CONV_W, cw_sh))
    small["rel_bias"] = small["rel_bias"].reshape(rel_bias.shape)
    p2 = dict(conv_w=(conv_w, m_conv_w, v_conv_w), a_log=(a_log, m_a_log, v_a_log), dt_bias=(dt_bias, m_dt_bias, v_dt_bias),
              delta_norm_w=(delta_norm_w, m_delta_norm_w, v_delta_norm_w), attn_sinks=(attn_sinks, m_attn_sinks, v_attn_sinks),
              rel_bias=(rel_bias, m_rel_bias, v_rel_bias), ln1_g=(ln1_g, m_ln1_g, v_ln1_g), ln1_b=(ln1_b, m_ln1_b, v_ln1_b),
              ln2_g=(ln2_g, m_ln2_g, v_ln2_g), ln2_b=(ln2_b, m_ln2_b, v_ln2_b))
    two_d = lambda a: a.reshape(-1, a.shape[-1])
    ws = [two_d(p2[k][0]) for k in SMALL_ORDER]
    gs = [two_d(small[k]) for k in SMALL_ORDER]
    ms = [two_d(p2[k][1]) for k in SMALL_ORDER]
    vs = [two_d(p2[k][2]) for k in SMALL_ORDER]
    ds, nms, nvs = _adamw_small(ws, gs, ms, vs)
    res = {}
    for i, k in enumerate(SMALL_ORDER):
        shp = p2[k][0].shape
        res[k] = [gs[i].reshape(shp), ds[i].reshape(shp), nms[i].reshape(shp), nvs[i].reshape(shp)]
    res.update(big)
    order = ("w_in", "conv_w", "a_log", "dt_bias", "delta_norm_w", "attn_sinks", "rel_bias", "w_o", "ln1_g", "ln1_b",
             "w_up", "w_down", "ln2_g", "ln2_b")
    return (loss, grad_x[None], *[res[k][0] for k in order], *[res[k][1] for k in order],
            *[res[k][2] for k in order], *[res[k][3] for k in order])
```

```python
import functools
import math

import numpy as np
import jax
import jax.numpy as jnp
from jax import lax
from jax.experimental import pallas as pl
from jax.experimental.pallas import tpu as pltpu

F32 = jnp.float32
BF16 = jnp.bfloat16
HIGHEST = lax.Precision.HIGHEST

N_DEV = 8
HEAD_A = 64
GQA = 4
BLK = 128
N_BUCKETS = 32
MAX_DISTANCE = 128
HEAD_D = 128
CONV_W = 4
CHUNK = 64
NEG_INF = -1e30
LN_EPS = 1e-5
RMS_EPS = 1e-6
DN_ALPHA = 2.0 ** 0.25
ADAM_LR, ADAM_B1, ADAM_B2, ADAM_EPS, ADAM_WD, ADAM_STEP = 0.001, 0.9, 0.999, 1e-08, 0.01, 10

LANE = 128
VMEM_LIMIT = 56 * 1024 * 1024

NN = ((1,), (0,))
NT = ((1,), (1,))
TN = ((0,), (0,))


def _dot(a, b, dims, prec=None):
    return lax.dot_general(a, b, (dims, ((), ())), precision=prec, preferred_element_type=F32)


def _tile(dim, pref):
    if dim <= pref:
        return dim
    t = (pref // LANE) * LANE
    while t > LANE and dim % t:
        t -= LANE
    assert dim % t == 0, (dim, pref)
    return t


def _params(sem):
    return pltpu.CompilerParams(dimension_semantics=sem, vmem_limit_bytes=VMEM_LIMIT)


def _matmul(a, b, dims, *, name, out_dtype=F32, tm=1024, tn=1024, tk=2048, a_fn=None, epi=None, epi_in=(),
            b_groups=None, out_groups=None, deps=()):
    (ca,), (cb,) = dims
    M, K = a.shape[1 - ca], a.shape[ca]
    if b_groups:
        G, R, C = b.shape
        bshape = (R, G * C)
    else:
        bshape = b.shape
    N = bshape[1 - cb]
    assert bshape[cb] == K, (a.shape, b.shape, dims)
    tm, tk = _tile(M, tm), _tile(K, tk)
    if b_groups:
        lim = C if cb == 0 else tn
        tn = _tile(N, min(tn, lim))
        if cb == 1 and tk < C:
            tk = _tile(K, min(tk, C))
        elif cb == 1:
            tk = C * max(1, tk // C)
    else:
        tn = _tile(N, tn)
    if out_groups:
        tn = _tile(N, min(tn, N // out_groups))
    nk = K // tk
    b_span = tk // C if (b_groups and cb == 1 and tk > C) else 1

    def body(*refs):
        a_ref, b_ref = refs[0], refs[1]
        e_refs = refs[2:2 + len(epi_in)]
        o_ref = refs[2 + len(epi_in) + len(deps)]
        acc_ref = refs[3 + len(epi_in) + len(deps)] if nk > 1 else None
        k = pl.program_id(2)
        av = a_ref[...]
        if a_fn is not None:
            av = a_fn(av)
        if b_span > 1:
            prod = sum(_dot(av[:, g * C:(g + 1) * C].astype(BF16), b_ref[g].astype(BF16), dims) for g in range(b_span))
        else:
            prod = _dot(av.astype(BF16), b_ref[...].astype(BF16), dims)

        def finish(r):
            if epi is not None:
                r = epi(r, *[e[...] for e in e_refs])
            o_ref[...] = r.astype(out_dtype)

        if nk == 1:
            finish(prod)
            return

        @pl.when(k == 0)
        def _():
            acc_ref[...] = prod

        @pl.when(k > 0)
        def _():
            acc_ref[...] += prod

        @pl.when(k == nk - 1)
        def _():
            finish(acc_ref[...])

    a_spec = (pl.BlockSpec((tm, tk), lambda i, j, k: (i, k)) if ca == 1
              else pl.BlockSpec((tk, tm), lambda i, j, k: (k, i)))
    if b_groups:
        if cb == 0:
            per = C // tn
            b_spec = pl.BlockSpec((None, tk, tn), lambda i, j, k: (j // per, k, j % per))
        elif b_span > 1:
            b_spec = pl.BlockSpec((b_span, tn, C), lambda i, j, k: (k, j, 0))
        else:
            per = C // tk
            b_spec = pl.BlockSpec((None, tn, tk), lambda i, j, k: (k // per, j, k % per))
    else:
        b_spec = (pl.BlockSpec((tk, tn), lambda i, j, k: (k, j)) if cb == 0
                  else pl.BlockSpec((tn, tk), lambda i, j, k: (j, k)))
    e_specs = [pl.BlockSpec((tm, tn), lambda i, j, k: (i, j)) for _ in epi_in]
    if out_groups:
        per_o = (N // out_groups) // tn
        o_spec = pl.BlockSpec((None, tm, tn), lambda i, j, k: (j // per_o, i, j % per_o))
        o_shape = jax.ShapeDtypeStruct((out_groups, M, N // out_groups), out_dtype)
    else:
        o_spec = pl.BlockSpec((tm, tn), lambda i, j, k: (i, j))
        o_shape = jax.ShapeDtypeStruct((M, N), out_dtype)
    return pl.pallas_call(
        body, grid=(M // tm, N // tn, nk), out_specs=o_spec,
        in_specs=[a_spec, b_spec] + e_specs + [pl.BlockSpec(memory_space=pl.ANY)] * len(deps),
        out_shape=o_shape, scratch_shapes=[pltpu.VMEM((tm, tn), F32)] if nk > 1 else [],
        compiler_params=_params(("parallel", "parallel", "arbitrary")), name=name)(a, b, *epi_in, *deps)


def _relu_sq(u):
    r = jnp.maximum(u, 0.0)
    return r * r


def _relu_sq_grad(acc, u):
    return acc * (2.0 * jnp.maximum(u, 0.0))


def _ln_stats(r):
    mu = jnp.mean(r, axis=-1, keepdims=True)
    xc = r - mu
    var = jnp.mean(xc * xc, axis=-1, keepdims=True)
    rstd = lax.rsqrt(var + LN_EPS)
    return xc * rstd, rstd


def _ln_bwd(dy, xhat, rstd, g):
    dxh = dy * g
    m1 = jnp.mean(dxh, axis=-1, keepdims=True)
    m2 = jnp.mean(dxh * xhat, axis=-1, keepdims=True)
    return rstd * (dxh - m1 - xhat * m2)


def _row_call(body, ins, row_ins, outs, acc_outs, name, tr=256):
    S = ins[0].shape[0]
    tr = min(tr, S)
    n_in, n_row, n_out = len(ins), len(row_ins), len(outs)

    def wrapped(*refs):
        i = pl.program_id(0)
        acc_refs = refs[n_in + n_row + n_out:]

        @pl.when(i == 0)
        def _():
            for r in acc_refs:
                r[...] = jnp.zeros_like(r)

        body(*refs)

    in_specs = [pl.BlockSpec((tr, a.shape[1]), lambda i: (i, 0)) for a in ins]
    in_specs += [pl.BlockSpec(a.shape, lambda i: (0, 0)) for a in row_ins]
    out_specs = [pl.BlockSpec((tr, s.shape[1]), lambda i: (i, 0)) for s in outs]
    out_specs += [pl.BlockSpec(s.shape, lambda i: (0, 0)) for s in acc_outs]
    return pl.pallas_call(wrapped, grid=(S // tr,), in_specs=in_specs, out_specs=out_specs,
                          out_shape=list(outs) + list(acc_outs),
                          compiler_params=_params(("arbitrary",)), name=name)(*ins, *row_ins)


def _ln1_fwd(x, mixed, g, b):
    def body(x_ref, m_ref, g_ref, b_ref, h_ref):
        xhat, _ = _ln_stats(DN_ALPHA * x_ref[...] + m_ref[...])
        h_ref[...] = xhat * g_ref[...] + b_ref[...]
    return _row_call(body, [x, mixed], [g, b], [jax.ShapeDtypeStruct(x.shape, F32)], [], "ln1_fwd")[0]


def _ln2_loss(h1, mlp, g, b, target):
    S, D = h1.shape
    sds = jax.ShapeDtypeStruct

    def body(h_ref, m_ref, t_ref, g_ref, b_ref, dr_ref, loss_ref, dg_ref, db_ref):
        xhat, rstd = _ln_stats(DN_ALPHA * h_ref[...] + m_ref[...])
        gv = g_ref[...]
        err = xhat * gv + b_ref[...] - t_ref[...]
        loss_ref[...] += jnp.sum(jnp.sum(err * err, axis=0, keepdims=True), axis=1, keepdims=True) * (0.5 / D)
        dy = err * (1.0 / D)
        dg_ref[...] += jnp.sum(dy * xhat, axis=0, keepdims=True)
        db_ref[...] += jnp.sum(dy, axis=0, keepdims=True)
        dr_ref[...] = _ln_bwd(dy, xhat, rstd, gv)

    return _row_call(body, [h1, mlp, target], [g, b], [sds((S, D), F32)],
                     [sds((1, LANE), F32), sds((1, D), F32), sds((1, D), F32)], "ln2_loss")


def _ln1_bwd(x, mixed, g, dr2, dh_mlp):
    S, D = x.shape
    sds = jax.ShapeDtypeStruct

    def body(x_ref, m_ref, dr2_ref, dh_ref, g_ref, dr_ref, dg_ref, db_ref):
        xhat, rstd = _ln_stats(DN_ALPHA * x_ref[...] + m_ref[...])
        dy = DN_ALPHA * dr2_ref[...] + dh_ref[...]
        dg_ref[...] += jnp.sum(dy * xhat, axis=0, keepdims=True)
        db_ref[...] += jnp.sum(dy, axis=0, keepdims=True)
        dr_ref[...] = _ln_bwd(dy, xhat, rstd, g_ref[...])

    return _row_call(body, [x, mixed, dr2, dh_mlp], [g], [sds((S, D), F32)],
                     [sds((1, D), F32), sds((1, D), F32)], "ln1_bwd")


def _residual_grad(acc, dr):
    return DN_ALPHA * dr + acc


def _bucket_table():
    qi = np.arange(BLK, dtype=np.int32)[:, None]
    kj = np.arange(2 * BLK, dtype=np.int32)[None, :]
    dist = qi + BLK - kj
    n = np.maximum(dist, 0)
    max_exact = N_BUCKETS // 2
    nf = np.maximum(n, 1).astype(np.float32)
    large = max_exact + (np.log(nf / np.float32(max_exact)) / np.float32(math.log(MAX_DISTANCE / max_exact))
                         * np.float32(N_BUCKETS - max_exact)).astype(np.int32)
    large = np.minimum(large, N_BUCKETS - 1)
    bucket = np.where(n < max_exact, n, large)
    return np.where((dist >= 0) & (dist < BLK), bucket, -1).astype(np.int32)


def _attn_bias(rel_bias_t):
    hq = rel_bias_t.shape[0]
    bucket = jnp.asarray(_bucket_table())

    def body(rb_ref, bk_ref, o_ref):
        h = pl.program_id(0)
        bk = bk_ref[...]
        acc = jnp.zeros((BLK, 2 * BLK), F32)
        for b in range(N_BUCKETS):
            acc = jnp.where(bk == b, rb_ref[h, b], acc)
        o_ref[...] = acc

    return pl.pallas_call(
        body, grid=(hq,),
        in_specs=[pl.BlockSpec(memory_space=pltpu.SMEM), pl.BlockSpec((BLK, 2 * BLK), lambda h: (0, 0))],
        out_specs=pl.BlockSpec((BLK, 2 * BLK), lambda h: (h, 0)),
        out_shape=jax.ShapeDtypeStruct((hq * BLK, 2 * BLK), F32),
        compiler_params=_params(("arbitrary",)), name="attn_bias")(rel_bias_t, bucket)


def _attn_probs(sc, sp, bias, sink, mask_c, mask_p):
    lc = jnp.where(mask_c, sc + bias[:, BLK:], NEG_INF)
    lp = jnp.where(mask_p, sp + bias[:, :BLK], NEG_INF)
    m = jnp.maximum(jnp.maximum(jnp.max(lc, axis=1, keepdims=True), jnp.max(lp, axis=1, keepdims=True)), sink)
    pc, pp, ps = jnp.exp(lc - m), jnp.exp(lp - m), jnp.exp(sink - m)
    inv = 1.0 / (jnp.sum(pc, axis=1, keepdims=True) + jnp.sum(pp, axis=1, keepdims=True) + ps)
    return pc, pp, ps, inv


def _attn_masks(n):
    qi = lax.broadcasted_iota(jnp.int32, (BLK, BLK), 0)
    kj = lax.broadcasted_iota(jnp.int32, (BLK, BLK), 1)
    return kj <= qi, (kj > qi) & (n > 0)


def _attn_fwd(proj, bias, sinks, hq, q_blk, k_blk, v_blk, out_width):
    S = proj.shape[0]
    hkv = hq // GQA
    wq, wk = hq * HEAD_A, hkv * HEAD_A

    def body(q_ref, k_ref, v_ref, bias_ref, sink_ref, o_ref):
        n = pl.program_id(0)
        cur = pl.multiple_of(n * BLK, BLK)
        prev = pl.multiple_of(jnp.maximum(n - 1, 0) * BLK, BLK)
        mask_c, mask_p = _attn_masks(n)
        for h4 in range(hkv):
            cs = slice(h4 * HEAD_A, (h4 + 1) * HEAD_A)
            kc, kp = k_ref[pl.ds(cur, BLK), cs].astype(BF16), k_ref[pl.ds(prev, BLK), cs].astype(BF16)
            vc, vp = v_ref[pl.ds(cur, BLK), cs].astype(BF16), v_ref[pl.ds(prev, BLK), cs].astype(BF16)
            hs_of = [slice(h * HEAD_A, (h + 1) * HEAD_A) for h in range(h4 * GQA, (h4 + 1) * GQA)]
            qs = [(q_ref[:, hs] * (HEAD_A ** -0.5)).astype(BF16) for hs in hs_of]
            scs = [_dot(q, kc, NT) for q in qs]
            sps = [_dot(q, kp, NT) for q in qs]
            pr = [_attn_probs(scs[g], sps[g], bias_ref[(h4 * GQA + g) * BLK:(h4 * GQA + g + 1) * BLK, :],
                              sink_ref[h4 * GQA + g], mask_c, mask_p) for g in range(GQA)]
            oc = [_dot(p[0].astype(BF16), vc, NN) for p in pr]
            op = [_dot(p[1].astype(BF16), vp, NN) for p in pr]
            for g, hs in enumerate(hs_of):
                o_ref[:, hs] = (oc[g] + op[g]) * pr[g][3]

    return pl.pallas_call(
        body, grid=(S // BLK,),
        in_specs=[pl.BlockSpec((BLK, wq), lambda n: (n, q_blk)), pl.BlockSpec((S, wk), lambda n: (0, k_blk)),
                  pl.BlockSpec((S, wk), lambda n: (0, v_blk)), pl.BlockSpec((hq * BLK, 2 * BLK), lambda n: (0, 0)),
                  pl.BlockSpec(memory_space=pltpu.SMEM)],
        out_specs=pl.BlockSpec((BLK, wq), lambda n: (n, 0)),
        out_shape=jax.ShapeDtypeStruct((S, out_width), F32),
        compiler_params=_params(("arbitrary",)), name="attn_fwd")(proj, proj, proj, bias, sinks)


def _attn_bwd(proj, bias, sinks, out, dmix, hq, q_blk, k_blk, v_blk):
    S = proj.shape[0]
    hkv = hq // GQA
    wq, wk = hq * HEAD_A, hkv * HEAD_A
    sds = jax.ShapeDtypeStruct

    def body(q_ref, k_ref, v_ref, bias_ref, sink_ref, o_ref, do_ref, dq_ref, dk_ref, dv_ref, dbias_ref, dsink_ref):
        n = pl.program_id(0)

        @pl.when(n == 0)
        def _():
            dk_ref[...] = jnp.zeros_like(dk_ref)
            dv_ref[...] = jnp.zeros_like(dv_ref)
            dbias_ref[...] = jnp.zeros_like(dbias_ref)
            dsink_ref[...] = jnp.zeros_like(dsink_ref)

        cur = pl.multiple_of(n * BLK, BLK)
        prev = pl.multiple_of(jnp.maximum(n - 1, 0) * BLK, BLK)
        mask_c, mask_p = _attn_masks(n)
        for h4 in range(hkv):
            cs = slice(h4 * HEAD_A, (h4 + 1) * HEAD_A)
            kc, kp = k_ref[pl.ds(cur, BLK), cs].astype(BF16), k_ref[pl.ds(prev, BLK), cs].astype(BF16)
            vc, vp = v_ref[pl.ds(cur, BLK), cs].astype(BF16), v_ref[pl.ds(prev, BLK), cs].astype(BF16)
            heads = list(range(h4 * GQA, (h4 + 1) * GQA))
            hs_of = [slice(h * HEAD_A, (h + 1) * HEAD_A) for h in heads]
            rows_of = [slice(h * BLK, (h + 1) * BLK) for h in heads]
            G = range(GQA)
            qs = [(q_ref[:, hs] * (HEAD_A ** -0.5)).astype(BF16) for hs in hs_of]
            dos = [do_ref[:, hs] for hs in hs_of]
            dobs = [d.astype(BF16) for d in dos]
            scs = [_dot(q, kc, NT) for q in qs]
            sps = [_dot(q, kp, NT) for q in qs]
            dpc = [_dot(d, vc, NT) for d in dobs]
            dpp = [_dot(d, vp, NT) for d in dobs]
            pcs, pps, dscs, dsps = [], [], [], []
            for g in G:
                pc, pp, ps, inv = _attn_probs(scs[g], sps[g], bias_ref[rows_of[g], :], sink_ref[heads[g]], mask_c, mask_p)
                pc, pp, ps = pc * inv, pp * inv, ps * inv
                delta = jnp.sum(dos[g] * o_ref[:, hs_of[g]], axis=1, keepdims=True)
                dsc, dsp = pc * (dpc[g] - delta), pp * (dpp[g] - delta)
                dsink_ref[heads[g]:heads[g] + 1, :] += jnp.broadcast_to(jnp.sum(-ps * delta, axis=0, keepdims=True), (1, LANE))
                dbias_ref[rows_of[g], BLK:] += dsc
                dbias_ref[rows_of[g], :BLK] += dsp
                pcs.append(pc.astype(BF16))
                pps.append(pp.astype(BF16))
                dscs.append(dsc.astype(BF16))
                dsps.append(dsp.astype(BF16))
            dq1 = [_dot(dscs[g], kc, NN) for g in G]
            dq2 = [_dot(dsps[g], kp, NN) for g in G]
            dkc = [_dot(dscs[g], qs[g], TN) for g in G]
            dkp = [_dot(dsps[g], qs[g], TN) for g in G]
            dvc = [_dot(pcs[g], dobs[g], TN) for g in G]
            dvp = [_dot(pps[g], dobs[g], TN) for g in G]
            for g in G:
                dq_ref[:, hs_of[g]] = (dq1[g] + dq2[g]) * (HEAD_A ** -0.5)
            dk_ref[pl.ds(cur, BLK), cs] += sum(dkc[1:], dkc[0])
            dk_ref[pl.ds(prev, BLK), cs] += sum(dkp[1:], dkp[0])
            dv_ref[pl.ds(cur, BLK), cs] += sum(dvc[1:], dvc[0])
            dv_ref[pl.ds(prev, BLK), cs] += sum(dvp[1:], dvp[0])

    return pl.pallas_call(
        body, grid=(S // BLK,),
        in_specs=[pl.BlockSpec((BLK, wq), lambda n: (n, q_blk)), pl.BlockSpec((S, wk), lambda n: (0, k_blk)),
                  pl.BlockSpec((S, wk), lambda n: (0, v_blk)), pl.BlockSpec((hq * BLK, 2 * BLK), lambda n: (0, 0)),
                  pl.BlockSpec(memory_space=pltpu.SMEM),
                  pl.BlockSpec((BLK, wq), lambda n: (n, 0)), pl.BlockSpec((BLK, wq), lambda n: (n, 0))],
        out_specs=[pl.BlockSpec((BLK, wq), lambda n: (n, 0)), pl.BlockSpec((S, wk), lambda n: (0, 0)),
                   pl.BlockSpec((S, wk), lambda n: (0, 0)), pl.BlockSpec((hq * BLK, 2 * BLK), lambda n: (0, 0)),
                   pl.BlockSpec((hq, LANE), lambda n: (0, 0))],
        out_shape=[sds((S, wq), F32), sds((S, wk), F32), sds((S, wk), F32), sds((hq * BLK, 2 * BLK), F32),
                   sds((hq, LANE), F32)],
        compiler_params=_params(("arbitrary",)), name="attn_bwd")(proj, proj, proj, bias, sinks, out, dmix)


def _rel_bias_grad(dbias, hq):
    bucket = jnp.asarray(_bucket_table())

    def body(d_ref, bk_ref, o_ref):
        d = d_ref[...]
        bk = bk_ref[...]
        rows = [jnp.sum(jnp.where(bk == b, d, 0.0), axis=0, keepdims=True) for b in range(N_BUCKETS)]
        tot = jnp.sum(jnp.concatenate(rows, axis=0), axis=1, keepdims=True)
        o_ref[...] = jnp.broadcast_to(tot, (N_BUCKETS, LANE))

    return pl.pallas_call(
        body, grid=(hq,),
        in_specs=[pl.BlockSpec((BLK, 2 * BLK), lambda h: (h, 0)), pl.BlockSpec((BLK, 2 * BLK), lambda h: (0, 0))],
        out_specs=pl.BlockSpec((None, N_BUCKETS, LANE), lambda h: (h, 0, 0)),
        out_shape=jax.ShapeDtypeStruct((hq, N_BUCKETS, LANE), F32),
        compiler_params=_params(("arbitrary",)), name="rel_bias_grad")(dbias, bucket)


def _sigmoid(x):
    return 1.0 / (1.0 + jnp.exp(-x))


def _shift_rows(x, s):
    n = x.shape[0]
    row = lax.broadcasted_iota(jnp.int32, x.shape, 0)
    if s > 0:
        return jnp.where(row >= s, pltpu.roll(x, s, 0), 0.0)
    return jnp.where(row < n + s, pltpu.roll(x, n + s, 0), 0.0)


def _conv_silu_norm(xv, w, j, nh):
    c = w[CONV_W - 1:CONV_W, :] * xv
    for s in range(1, CONV_W):
        c = c + w[CONV_W - 1 - s:CONV_W - s, :] * _shift_rows(xv, s)
    sg = _sigmoid(c)
    a = c * sg
    r = lax.rsqrt(jnp.sum(a * a, axis=1, keepdims=True) + RMS_EPS)
    scale = jnp.where(j < nh, HEAD_D ** -0.5, 1.0)
    is_norm = j < 2 * nh
    y = jnp.where(is_norm, a * (r * scale), a)
    return c, sg, a, r, scale, is_norm, y


def _gdn_prep_fwd(proj, conv_w, nh, blk0):
    S = proj.shape[0]

    def body(x_ref, w_ref, o_ref):
        j = pl.program_id(0)
        o_ref[...] = _conv_silu_norm(x_ref[...], w_ref[...], j, nh)[-1]

    return pl.pallas_call(
        body, grid=(3 * nh,),
        in_specs=[pl.BlockSpec((S, HEAD_D), lambda j: (0, blk0 + j)), pl.BlockSpec((CONV_W, HEAD_D), lambda j: (0, j))],
        out_specs=pl.BlockSpec((S, HEAD_D), lambda j: (0, 3 * (j % nh) + j // nh)),
        out_shape=jax.ShapeDtypeStruct((S, 3 * nh * HEAD_D), F32),
        compiler_params=_params(("parallel",)), name="gdn_prep_fwd")(proj, conv_w)


def _gdn_prep_bwd(proj, conv_w, dqkv, nh, blk0):
    S = proj.shape[0]
    sds = jax.ShapeDtypeStruct

    def body(x_ref, w_ref, dy_ref, dx_ref, dw_ref):
        j = pl.program_id(0)
        xv, w = x_ref[...], w_ref[...]
        c, sg, a, r, scale, is_norm, _ = _conv_silu_norm(xv, w, j, nh)
        dy = dy_ref[...]
        rs = r * scale
        da_n = rs * dy - a * (r * r * rs) * jnp.sum(dy * a, axis=1, keepdims=True)
        da = jnp.where(is_norm, da_n, dy)
        dc = da * (sg * (1.0 + c * (1.0 - sg)))
        dx = w[CONV_W - 1:CONV_W, :] * dc
        dws = [jnp.sum(dc * xv, axis=0, keepdims=True)]
        for s in range(1, CONV_W):
            dx = dx + w[CONV_W - 1 - s:CONV_W - s, :] * _shift_rows(dc, -s)
            dws.insert(0, jnp.sum(dc * _shift_rows(xv, s), axis=0, keepdims=True))
        dx_ref[...] = dx
        dw_ref[...] = jnp.concatenate(dws, axis=0)

    return pl.pallas_call(
        body, grid=(3 * nh,),
        in_specs=[pl.BlockSpec((S, HEAD_D), lambda j: (0, blk0 + j)), pl.BlockSpec((CONV_W, HEAD_D), lambda j: (0, j)),
                  pl.BlockSpec((S, HEAD_D), lambda j: (0, 3 * (j % nh) + j // nh))],
        out_specs=[pl.BlockSpec((S, HEAD_D), lambda j: (0, j)), pl.BlockSpec((CONV_W, HEAD_D), lambda j: (0, j))],
        out_shape=[sds((S, 3 * nh * HEAD_D), F32), sds((CONV_W, 3 * nh * HEAD_D), F32)],
        compiler_params=_params(("parallel",)), name="gdn_prep_bwd")(proj, conv_w, dqkv)


def _softplus(x):
    return jnp.maximum(x, 0.0) + jnp.log(1.0 + jnp.exp(-jnp.abs(x)))


def _gates_fwd(ab, al, dt, nh):
    S = ab.shape[0]

    def body(ab_ref, al_ref, dt_ref, o_ref):
        v = ab_ref[...]
        lane = lax.broadcasted_iota(jnp.int32, v.shape, 1)
        g = -jnp.exp(al_ref[...]) * _softplus(v + dt_ref[...])
        o_ref[...] = jnp.where(lane < nh, g, jnp.where(lane < 2 * nh, _sigmoid(v), 0.0))

    row = pl.BlockSpec((1, LANE), lambda i: (0, 0))
    full = pl.BlockSpec((S, LANE), lambda i: (0, 0))
    return pl.pallas_call(body, grid=(1,), in_specs=[full, row, row], out_specs=full,
                          out_shape=jax.ShapeDtypeStruct((S, LANE), F32),
                          compiler_params=_params(("arbitrary",)), name="gates_fwd")(ab, al, dt)


def _gates_bwd(ab, al, dt, dgb, nh):
    S = ab.shape[0]
    sds = jax.ShapeDtypeStruct

    def body(ab_ref, al_ref, dt_ref, d_ref, dab_ref, dal_ref, ddt_ref):
        v, d = ab_ref[...], d_ref[...]
        lane = lax.broadcasted_iota(jnp.int32, v.shape, 1)
        is_a = lane < nh
        z = v + dt_ref[...]
        dsp = jnp.where(is_a, d * (-jnp.exp(al_ref[...])), 0.0)
        dz = dsp * _sigmoid(z)
        beta = _sigmoid(v)
        dab_ref[...] = jnp.where(is_a, dz, jnp.where(lane < 2 * nh, d * beta * (1.0 - beta), 0.0))
        dal_ref[...] = jnp.sum(dsp * _softplus(z), axis=0, keepdims=True)
        ddt_ref[...] = jnp.sum(dz, axis=0, keepdims=True)

    row = pl.BlockSpec((1, LANE), lambda i: (0, 0))
    full = pl.BlockSpec((S, LANE), lambda i: (0, 0))
    return pl.pallas_call(body, grid=(1,), in_specs=[full, row, row, full], out_specs=[full, row, row],
                          out_shape=[sds((S, LANE), F32), sds((1, LANE), F32), sds((1, LANE), F32)],
                          compiler_params=_params(("arbitrary",)), name="gates_bwd")(ab, al, dt, dgb)


def _col_of(tile, h):
    lane = lax.broadcasted_iota(jnp.int32, tile.shape, 1)
    return jnp.sum(jnp.where(lane == h, tile, 0.0), axis=1, keepdims=True)


def _to_row(col, eye):
    return jnp.sum(jnp.where(eye, col, 0.0), axis=0, keepdims=True)


def _to_col(row, eye):
    return jnp.sum(jnp.where(eye, row, 0.0), axis=1, keepdims=True)


def _split(a):
    hi = a.astype(BF16)
    return hi, (a - hi.astype(F32)).astype(BF16)


def _gdot(a, b, dims):
    ah, al = _split(a)
    bh, bl = _split(b)
    return _dot(ah, bh, dims) + (_dot(ah, bl, dims) + _dot(al, bh, dims))


def _bdot(a, b, dims):
    return _dot(a.astype(BF16), b.astype(BF16), dims)


def _chunks_local(qs, ks, vs, gcols, bcols, Ts=None):
    C = CHUNK
    row = lax.broadcasted_iota(jnp.int32, (C, C), 0)
    col = lax.broadcasted_iota(jnp.int32, (C, C), 1)
    tril, strict, eye = col <= row, col < row, col == row
    outs = []
    for k, gcol, bcol in zip(ks, gcols, bcols):
        grow = _to_row(gcol, eye)
        G_row = jnp.sum(jnp.where(row <= col, gcol, 0.0), axis=0, keepdims=True)
        G_col = jnp.sum(jnp.where(tril, grow, 0.0), axis=1, keepdims=True)
        G_last = G_col[C - 1:C, :]
        outs.append(dict(strict=strict, eye=eye, row=row, col=col, decay=jnp.exp(jnp.where(tril, G_col - G_row, NEG_INF)),
                         eG=jnp.exp(G_col), eGr=jnp.exp(G_last - G_col), gl=jnp.exp(G_last), kb=k * bcol))
    Ms = [_gdot(o["kb"], k, NT) for o, k in zip(outs, ks)]
    Ns = [_gdot(q, k, NT) for q, k in zip(qs, ks)]
    for o, q, k, M, N in zip(outs, qs, ks, Ms, Ns):
        o.update(A=jnp.where(strict, M * o["decay"], 0.0), attn=N * o["decay"], rhs_k=o["kb"] * o["eG"],
                 q_dec=q * o["eG"], k_dec=k * o["eGr"])
    if Ts is None:
        Ts = [jnp.where(eye, 1.0, 0.0) - o["A"] for o in outs]
        Ps = [o["A"] for o in outs]
        for _ in range(int(math.log2(C)) - 1):
            Ps = [_bdot(P, P, NN) for P in Ps]
            Ts = [T + _bdot(T, P, NN) for T, P in zip(Ts, Ps)]
        us = [_bdot(T, v * bcol, NN) for T, v, bcol in zip(Ts, vs, bcols)]
        ws = [_bdot(T, o["rhs_k"], NN) for T, o in zip(Ts, outs)]
        for o, T, u, w in zip(outs, Ts, us, ws):
            o.update(T=T, u=u, w=w)
    return outs


GDN_ROWS = 256
GDN_LOCAL_ROWS = 512
WQK = 3 * CHUNK


def _gdn_local_fwd(qkv, gb, nh):
    S = qkv.shape[0]
    nc = S // CHUNK
    rb = min(GDN_LOCAL_ROWS, S)
    cpb = rb // CHUNK
    sds = jax.ShapeDtypeStruct

    def body(q_ref, k_ref, v_ref, gb_ref, u_ref, wqk_ref, attn_ref, t_ref):
        h = pl.program_id(0)
        rows_of = [slice(ci * CHUNK, (ci + 1) * CHUNK) for ci in range(cpb)]
        gbts = [gb_ref[rows, :] for rows in rows_of]
        Ls = _chunks_local([q_ref[rows, :] for rows in rows_of], [k_ref[rows, :] for rows in rows_of],
                           [v_ref[rows, :] for rows in rows_of], [_col_of(t, h) for t in gbts],
                           [_col_of(t, nh + h) for t in gbts])
        for ci, (rows, L) in enumerate(zip(rows_of, Ls)):
            u_ref[rows, :] = L["u"]
            base = ci * WQK
            wqk_ref[base:base + CHUNK, :] = L["w"]
            wqk_ref[base + CHUNK:base + 2 * CHUNK, :] = L["q_dec"]
            wqk_ref[base + 2 * CHUNK:base + WQK, :] = L["k_dec"]
            attn_ref[ci] = L["attn"]
            t_ref[ci] = L["T"]

    cc = pl.BlockSpec((None, cpb, CHUNK, CHUNK), lambda h, i: (h, i, 0, 0))
    return pl.pallas_call(
        body, grid=(nh, S // rb),
        in_specs=[pl.BlockSpec((rb, HEAD_D), lambda h, i: (i, 3 * h)), pl.BlockSpec((rb, HEAD_D), lambda h, i: (i, 3 * h + 1)),
                  pl.BlockSpec((rb, HEAD_D), lambda h, i: (i, 3 * h + 2)), pl.BlockSpec((rb, LANE), lambda h, i: (i, 0))],
        out_specs=[pl.BlockSpec((rb, HEAD_D), lambda h, i: (i, h)),
                   pl.BlockSpec((None, 3 * rb, HEAD_D), lambda h, i: (h, i, 0)), cc, cc],
        out_shape=[sds((S, nh * HEAD_D), F32), sds((nh, 3 * S, HEAD_D), F32), sds((nh, nc, CHUNK, CHUNK), F32),
                   sds((nh, nc, CHUNK, CHUNK), F32)],
        compiler_params=_params(("parallel", "parallel")), name="gdn_local_fwd")(qkv, qkv, qkv, gb)


def _gdn_scan_fwd(u, wqk, attn, gb, nh, dep):
    S = u.shape[0]
    nc = S // CHUNK
    rb = min(GDN_ROWS, S)
    cpb = rb // CHUNK
    sds = jax.ShapeDtypeStruct

    def body(u_ref, wqk_ref, attn_ref, gb_ref, dep_ref, o_ref, vn_ref, st_ref, s_ref):
        @pl.when(pl.program_id(0) == 0)
        def _():
            s_ref[...] = jnp.zeros_like(s_ref)

        for ci in range(cpb):
            rows = slice(ci * CHUNK, (ci + 1) * CHUNK)
            glv = jnp.exp(jnp.sum(gb_ref[rows, :], axis=0, keepdims=True))
            base = ci * WQK
            heads = range(nh)
            cols = [slice(h * HEAD_D, (h + 1) * HEAD_D) for h in heads]
            states = [s_ref[h] for h in heads]
            rs = [_gdot(wqk_ref[h, base:base + 2 * CHUNK, :], states[h], NN) for h in heads]
            vbs = [u_ref[rows, cols[h]] - rs[h][:CHUNK] for h in heads]
            os_ = [_gdot(attn_ref[h, ci], vbs[h], NN) for h in heads]
            ks_ = [_gdot(wqk_ref[h, base + 2 * CHUNK:base + WQK, :], vbs[h], TN) for h in heads]
            for h in heads:
                st_ref[h, ci] = states[h]
                o_ref[rows, cols[h]] = rs[h][CHUNK:] + os_[h]
                vn_ref[rows, cols[h]] = vbs[h]
                s_ref[h] = states[h] * glv[:, h:h + 1] + ks_[h]

    return pl.pallas_call(
        body, grid=(S // rb,),
        in_specs=[pl.BlockSpec((rb, nh * HEAD_D), lambda i: (i, 0)), pl.BlockSpec((nh, 3 * rb, HEAD_D), lambda i: (0, i, 0)),
                  pl.BlockSpec((nh, cpb, CHUNK, CHUNK), lambda i: (0, i, 0, 0)), pl.BlockSpec((rb, LANE), lambda i: (i, 0)),
                  pl.BlockSpec(memory_space=pl.ANY)],
        out_specs=[pl.BlockSpec((rb, nh * HEAD_D), lambda i: (i, 0)), pl.BlockSpec((rb, nh * HEAD_D), lambda i: (i, 0)),
                   pl.BlockSpec((nh, cpb, HEAD_D, HEAD_D), lambda i: (0, i, 0, 0))],
        out_shape=[sds((S, nh * HEAD_D), F32), sds((S, nh * HEAD_D), F32), sds((nh, nc, HEAD_D, HEAD_D), F32)],
        scratch_shapes=[pltpu.VMEM((nh, HEAD_D, HEAD_D), F32)],
        compiler_params=_params(("arbitrary",)), name="gdn_scan_fwd")(u, wqk, attn, gb, dep)


def _gdn_scan_bwd(wqk, attn, gb, states, vn, do, nh, dep):
    S = vn.shape[0]
    nc = S // CHUNK
    rb = min(GDN_ROWS, S)
    cpb = rb // CHUNK
    last = S // rb - 1
    sds = jax.ShapeDtypeStruct

    def body(wqk_ref, attn_ref, gb_ref, st_ref, vn_ref, do_ref, dep_ref, dvn_ref, dw_ref, dqd_ref, dkd_ref, da_ref, dgl_ref,
             ds_ref):
        @pl.when(pl.program_id(0) == 0)
        def _():
            ds_ref[...] = jnp.zeros_like(ds_ref)

        row = lax.broadcasted_iota(jnp.int32, (CHUNK, CHUNK), 0)
        col = lax.broadcasted_iota(jnp.int32, (CHUNK, CHUNK), 1)
        for ci in reversed(range(cpb)):
            rows = slice(ci * CHUNK, (ci + 1) * CHUNK)
            glv = jnp.exp(jnp.sum(gb_ref[rows, :], axis=0, keepdims=True))
            base = ci * WQK
            heads = range(nh)
            cols = [slice(h * HEAD_D, (h + 1) * HEAD_D) for h in heads]
            states = [st_ref[h, ci] for h in heads]
            dSs = [ds_ref[h] for h in heads]
            vbs = [vn_ref[rows, cols[h]] for h in heads]
            dobs = [do_ref[rows, cols[h]] for h in heads]
            dv1 = [_gdot(attn_ref[h, ci], dobs[h], TN) for h in heads]
            dv2 = [_gdot(wqk_ref[h, base + 2 * CHUNK:base + WQK, :], dSs[h], NN) for h in heads]
            das = [_gdot(dobs[h], vbs[h], NT) for h in heads]
            dkds = [_gdot(vbs[h], dSs[h], NT) for h in heads]
            dvbs = [dv1[h] + dv2[h] for h in heads]
            xs = [_gdot(jnp.concatenate([dobs[h], dvbs[h]], axis=0), states[h], NT) for h in heads]
            dss = [_gdot(wqk_ref[h, base:base + 2 * CHUNK, :], jnp.concatenate([-dvbs[h], dobs[h]], axis=0), TN)
                   for h in heads]
            for h in heads:
                dqd_ref[rows, cols[h]] = xs[h][:CHUNK]
                dw_ref[rows, cols[h]] = -xs[h][CHUNK:]
                dvn_ref[rows, cols[h]] = dvbs[h]
                da_ref[h, ci] = jnp.where(col <= row, das[h], 0.0)
                dkd_ref[rows, cols[h]] = dkds[h]
                gl = glv[:, h:h + 1]
                dgl = jnp.sum(jnp.sum(states[h] * dSs[h], axis=0, keepdims=True), axis=1, keepdims=True)
                dgl_ref[h, ci] = jnp.broadcast_to(dgl * gl, (1, LANE))
                ds_ref[h] = dSs[h] * gl + dss[h]

    rv = lambda i: last - i
    wide = pl.BlockSpec((rb, nh * HEAD_D), lambda i: (rv(i), 0))
    return pl.pallas_call(
        body, grid=(S // rb,),
        in_specs=[pl.BlockSpec((nh, 3 * rb, HEAD_D), lambda i: (0, rv(i), 0)),
                  pl.BlockSpec((nh, cpb, CHUNK, CHUNK), lambda i: (0, rv(i), 0, 0)),
                  pl.BlockSpec((rb, LANE), lambda i: (rv(i), 0)),
                  pl.BlockSpec((nh, cpb, HEAD_D, HEAD_D), lambda i: (0, rv(i), 0, 0)), wide, wide,
                  pl.BlockSpec(memory_space=pl.ANY)],
        out_specs=[wide, wide, wide, wide, pl.BlockSpec((nh, cpb, CHUNK, CHUNK), lambda i: (0, rv(i), 0, 0)),
                   pl.BlockSpec((nh, cpb, 1, LANE), lambda i: (0, rv(i), 0, 0))],
        out_shape=[sds((S, nh * HEAD_D), F32), sds((S, nh * HEAD_D), F32), sds((S, nh * HEAD_D), F32),
                   sds((S, nh * HEAD_D), F32), sds((nh, nc, CHUNK, CHUNK), F32), sds((nh, nc, 1, LANE), F32)],
        scratch_shapes=[pltpu.VMEM((nh, HEAD_D, HEAD_D), F32)],
        compiler_params=_params(("arbitrary",)), name="gdn_scan_bwd")(wqk, attn, gb, states, vn, do, dep)


def _gdn_local_bwd(qkv, gb, T, u, wqk, dvn, dw, dqd, dkd, dattn, dgl, nh):
    S = qkv.shape[0]
    rb = min(GDN_LOCAL_ROWS, S)
    cpb = rb // CHUNK
    sds = jax.ShapeDtypeStruct

    def body(q_ref, k_ref, v_ref, gb_ref, t_ref, u_ref, wqk_ref, dvn_ref, dw_ref, dqd_ref, dkd_ref, da_ref, dgl_ref,
             dqkv_ref, dg_ref, db_ref):
        h = pl.program_id(0)
        n = range(cpb)
        rows_of = [slice(ci * CHUNK, (ci + 1) * CHUNK) for ci in n]
        qs, ks, vs = ([r[rows, :] for rows in rows_of] for r in (q_ref, k_ref, v_ref))
        gbts = [gb_ref[rows, :] for rows in rows_of]
        bcols = [_col_of(t, nh + h) for t in gbts]
        Ts = [t_ref[ci] for ci in n]
        Ls = _chunks_local(qs, ks, vs, [_col_of(t, h) for t in gbts], bcols, Ts=Ts)
        drvs = [_gdot(Ts[ci], dvn_ref[rows_of[ci], :], TN) for ci in n]
        drks = [_gdot(Ts[ci], dw_ref[rows_of[ci], :], TN) for ci in n]
        dAs = [jnp.where(Ls[ci]["strict"], -(_gdot(drvs[ci], u_ref[rows_of[ci], :], NT)
                                             + _gdot(drks[ci], wqk_ref[ci * WQK:ci * WQK + CHUNK, :], NT)), 0.0) for ci in n]
        dMs = [dAs[ci] * Ls[ci]["decay"] for ci in n]
        dNs = [da_ref[ci] * Ls[ci]["decay"] for ci in n]
        dkbs = [_gdot(dMs[ci], ks[ci], NN) for ci in n]
        dq1 = [_gdot(dNs[ci], ks[ci], NN) for ci in n]
        dk1 = [_gdot(dMs[ci], Ls[ci]["kb"], TN) for ci in n]
        dk2 = [_gdot(dNs[ci], qs[ci], TN) for ci in n]
        for ci in n:
            rows, L, q, k, v, bcol = rows_of[ci], Ls[ci], qs[ci], ks[ci], vs[ci], bcols[ci]
            eye, eG, eGr = L["eye"], L["eG"], L["eGr"]
            drv, drk, dkb = drvs[ci], drks[ci], dkbs[ci]
            dq_dec, dk_dec, dattn_c = dqd_ref[rows, :], dkd_ref[rows, :], da_ref[ci]
            dqkv_ref[rows, :HEAD_D] = dq1[ci] + dq_dec * eG
            dqkv_ref[rows, HEAD_D:2 * HEAD_D] = drk * (bcol * eG) + dk1[ci] + dkb * bcol + dk2[ci] + dk_dec * eGr
            dqkv_ref[rows, 2 * HEAD_D:] = drv * bcol
            db_ref[rows, :] = (jnp.sum(drv * v, axis=1, keepdims=True) + jnp.sum(drk * k, axis=1, keepdims=True) * eG
                               + jnp.sum(dkb * k, axis=1, keepdims=True))
            E = dAs[ci] * L["A"] + dattn_c * L["attn"]
            kd = jnp.sum(dk_dec * L["k_dec"], axis=1, keepdims=True)
            dG = (jnp.sum(dq_dec * L["q_dec"], axis=1, keepdims=True) - kd
                  + jnp.sum(drk * L["rhs_k"], axis=1, keepdims=True)
                  + jnp.sum(E, axis=1, keepdims=True) - _to_col(jnp.sum(E, axis=0, keepdims=True), eye))
            d_last = jnp.sum(kd, axis=0, keepdims=True) + dgl_ref[ci][:, :1]
            dG = dG + jnp.where(L["row"][:, :1] == CHUNK - 1, d_last, 0.0)
            dg_ref[rows, :] = jnp.sum(jnp.where(L["col"] >= L["row"], _to_row(dG, eye), 0.0), axis=1, keepdims=True)

    hd = pl.BlockSpec((rb, HEAD_D), lambda h, i: (i, h))
    cc = pl.BlockSpec((None, cpb, CHUNK, CHUNK), lambda h, i: (h, i, 0, 0))
    col1 = pl.BlockSpec((None, rb, 1), lambda h, i: (h, i, 0))
    return pl.pallas_call(
        body, grid=(nh, S // rb),
        in_specs=[pl.BlockSpec((rb, HEAD_D), lambda h, i: (i, 3 * h)), pl.BlockSpec((rb, HEAD_D), lambda h, i: (i, 3 * h + 1)),
                  pl.BlockSpec((rb, HEAD_D), lambda h, i: (i, 3 * h + 2)), pl.BlockSpec((rb, LANE), lambda h, i: (i, 0)),
                  cc, hd, pl.BlockSpec((None, 3 * rb, HEAD_D), lambda h, i: (h, i, 0)), hd, hd, hd, hd, cc,
                  pl.BlockSpec((None, cpb, 1, LANE), lambda h, i: (h, i, 0, 0))],
        out_specs=[pl.BlockSpec((rb, 3 * HEAD_D), lambda h, i: (i, h)), col1, col1],
        out_shape=[sds((S, 3 * nh * HEAD_D), F32)] + [sds((nh, S, 1), F32)] * 2,
        compiler_params=_params(("parallel", "parallel")), name="gdn_local_bwd")(
            qkv, qkv, qkv, gb, T, u, wqk, dvn, dw, dqd, dkd, dattn, dgl)


def _gated_norm_fwd(o, proj, norm_w, nh, z_blk0, mix, m_blk0):
    S = o.shape[0]

    def body(o_ref, z_ref, w_ref, mix_ref, y_ref):
        ov, z = o_ref[...], z_ref[...]
        r = lax.rsqrt(jnp.mean(ov * ov, axis=1, keepdims=True) + RMS_EPS)
        y_ref[...] = ov * r * w_ref[...] * (z * _sigmoid(z))

    return pl.pallas_call(
        body, grid=(nh,),
        in_specs=[pl.BlockSpec((S, HEAD_D), lambda h: (0, h)), pl.BlockSpec((S, HEAD_D), lambda h: (0, z_blk0 + h)),
                  pl.BlockSpec((1, HEAD_D), lambda h: (0, 0)), pl.BlockSpec(memory_space=pl.ANY)],
        out_specs=pl.BlockSpec((S, HEAD_D), lambda h: (0, m_blk0 + h)),
        out_shape=jax.ShapeDtypeStruct(mix.shape, F32), input_output_aliases={3: 0},
        compiler_params=_params(("parallel",)), name="gated_norm_fwd")(o, proj, norm_w, mix)


def _gated_norm_bwd(o, proj, norm_w, dmix, nh, z_blk0, d_blk0):
    S = o.shape[0]
    sds = jax.ShapeDtypeStruct

    def body(o_ref, z_ref, w_ref, dy_ref, do_ref, dz_ref, dw_ref):
        ov, z, w, dy = o_ref[...], z_ref[...], w_ref[...], dy_ref[...]
        r = lax.rsqrt(jnp.mean(ov * ov, axis=1, keepdims=True) + RMS_EPS)
        oh = ov * r
        sg = _sigmoid(z)
        dz_ref[...] = dy * (oh * w) * (sg * (1.0 + z * (1.0 - sg)))
        don = dy * (z * sg)
        @pl.when(pl.program_id(0) == 0)
        def _():
            dw_ref[...] = jnp.zeros_like(dw_ref)

        dw_ref[...] += jnp.sum(don * oh, axis=0, keepdims=True)
        doh = don * w
        do_ref[...] = r * (doh - oh * jnp.mean(doh * oh, axis=1, keepdims=True))

    return pl.pallas_call(
        body, grid=(nh,),
        in_specs=[pl.BlockSpec((S, HEAD_D), lambda h: (0, h)), pl.BlockSpec((S, HEAD_D), lambda h: (0, z_blk0 + h)),
                  pl.BlockSpec((1, HEAD_D), lambda h: (0, 0)), pl.BlockSpec((S, HEAD_D), lambda h: (0, d_blk0 + h))],
        out_specs=[pl.BlockSpec((S, HEAD_D), lambda h: (0, h)), pl.BlockSpec((S, HEAD_D), lambda h: (0, h)),
                   pl.BlockSpec((1, HEAD_D), lambda h: (0, 0))],
        out_shape=[sds((S, nh * HEAD_D), F32), sds((S, nh * HEAD_D), F32), sds((1, HEAD_D), F32)],
        compiler_params=_params(("arbitrary",)), name="gated_norm_bwd")(o, proj, norm_w, dmix)


def _adamw_math(w, g, m, v):
    m = ADAM_B1 * m + (1.0 - ADAM_B1) * g
    v = ADAM_B2 * v + (1.0 - ADAM_B2) * (g * g)
    m_hat = m / (1.0 - ADAM_B1 ** ADAM_STEP)
    v_hat = v / (1.0 - ADAM_B2 ** ADAM_STEP)
    delta = -ADAM_LR * (m_hat / (jnp.sqrt(v_hat) + ADAM_EPS) + ADAM_WD * w)
    return delta, m, v


def _slab_tiles(R, C, rows=256, cols=256):
    if R % rows == 0:
        return (rows, C), R // rows, lambda i: (i, 0)
    tc = _tile(C, cols)
    return (R, tc), C // tc, lambda i: (0, i)


def _adamw_big(parts, terms, chip, w, m, v, name):
    R, C = w.shape
    blk, steps, at = _slab_tiles(R, C)
    sds = jax.ShapeDtypeStruct

    def body(q_ref, p_ref, t_ref, w_ref, m_ref, v_ref, g_ref, d_ref, nm_ref, nv_ref):
        g = ((p_ref[...].astype(F32) + t_ref[0].astype(F32)) + t_ref[1].astype(F32)) + t_ref[2].astype(F32)
        g_ref[...] = g
        d_ref[...], nm_ref[...], nv_ref[...] = _adamw_math(w_ref[...], g, m_ref[...], v_ref[...])

    spec = pl.BlockSpec(blk, lambda i, q_ref: at(i))
    grid_spec = pltpu.PrefetchScalarGridSpec(
        num_scalar_prefetch=1, grid=(steps,),
        in_specs=[pl.BlockSpec((None,) + blk, lambda i, q_ref: (q_ref[0],) + at(i)),
                  pl.BlockSpec((3,) + blk, lambda i, q_ref: (0,) + at(i)), spec, spec, spec],
        out_specs=[spec] * 4)
    return pl.pallas_call(body, grid_spec=grid_spec, out_shape=[sds((R, C), F32)] * 4,
                          compiler_params=_params(("parallel",)), name=name)(chip, parts, terms, w, m, v)


def _adamw_small(ws, gs, ms, vs):
    n = len(ws)

    def body(*refs):
        for i in range(n):
            w, g, m, v = (refs[k * n + i][...] for k in range(4))
            d, nm, nv = _adamw_math(w, g, m, v)
            refs[4 * n + i][...] = d
            refs[5 * n + i][...] = nm
            refs[6 * n + i][...] = nv

    shapes = [jax.ShapeDtypeStruct(w.shape, F32) for w in ws]
    vm = pl.BlockSpec(memory_space=pltpu.VMEM)
    outs = pl.pallas_call(body, in_specs=[vm] * (4 * n), out_specs=[vm] * (3 * n), out_shape=shapes * 3,
                          name="adamw_small")(*ws, *gs, *ms, *vs)
    return outs[:n], outs[n:2 * n], outs[2 * n:]


MESH = pl.DeviceIdType.MESH
ANY = pl.BlockSpec(memory_space=pl.ANY)


def _place():
    x, y, c = lax.axis_index("x"), lax.axis_index("y"), lax.axis_index("c")
    return x, y, c, [(1 - x, y), (x, 1 - y), (1 - x, 1 - y)]


def _chip_sum(grad, recv, core, name):
    _, R, C = grad.shape
    blk, steps, at = _slab_tiles(R, C, rows=512 if R % 512 == 0 else 256, cols=512)

    def body(c_ref, g_ref, r_ref, o_ref):
        o_ref[...] = (g_ref[...].astype(F32) + r_ref[...].astype(F32)).astype(o_ref.dtype)

    grid_spec = pltpu.PrefetchScalarGridSpec(
        num_scalar_prefetch=1, grid=(4, steps),
        in_specs=[pl.BlockSpec((None,) + blk, lambda q, i, c_ref: (2 * q + c_ref[0],) + at(i)),
                  pl.BlockSpec((None,) + blk, lambda q, i, c_ref: (q,) + at(i))],
        out_specs=pl.BlockSpec((None,) + blk, lambda q, i, c_ref: (q,) + at(i)))
    return pl.pallas_call(body, grid_spec=grid_spec, out_shape=jax.ShapeDtypeStruct((4, R, C), BF16),
                          compiler_params=_params(("parallel", "parallel")), name=name)(core, grad, recv)


HBM_SPEC = pl.BlockSpec(memory_space=pltpu.HBM)
SEM_SPEC = pl.BlockSpec(memory_space=pltpu.SEMAPHORE)
DATAFLOW = pltpu.SideEffectType.DATAFLOW_SIDE_EFFECTING


def _split_start(name, bufs, plan, counts, after=None):
    nb, ng = len(bufs), len(counts)
    extra = [] if after is None else [after]
    place = [(g, k) for g, cnt in enumerate(counts) for k in range(cnt)]

    def body(*refs):
        sems, token = refs[nb + len(extra):nb + len(extra) + 2 * ng], refs[-1]
        for (g, k), (src, dst, to) in zip(place, plan(refs[:nb])):
            pltpu.make_async_remote_copy(src_ref=src, dst_ref=dst, send_sem=sems[2 * g].at[k], recv_sem=sems[2 * g + 1].at[k],
                                         device_id=to, device_id_type=MESH).start()
        token[...] = jnp.zeros_like(token)

    outs = pl.pallas_call(
        body, name=name,
        out_shape=(*[pltpu.SemaphoreType.DMA((cnt,)) for cnt in counts for _ in range(2)],
                   *[pltpu.HBM(b.shape, b.dtype) for b in bufs], jax.ShapeDtypeStruct((8, LANE), F32)),
        in_specs=[HBM_SPEC] * nb + [ANY] * len(extra),
        out_specs=(*[SEM_SPEC] * (2 * ng), *[HBM_SPEC] * nb, pl.BlockSpec(memory_space=pltpu.VMEM)),
        input_output_aliases={i: 2 * ng + i for i in range(nb)},
        compiler_params=pltpu.CompilerParams(has_side_effects=DATAFLOW))(
            *[pltpu.with_memory_space_constraint(b, pltpu.HBM) for b in bufs], *extra)
    return [(outs[2 * g], outs[2 * g + 1]) for g in range(ng)], list(outs[2 * ng:2 * ng + nb]), outs[-1]


def _split_wait(name, sems, bufs, plan, after):
    nb = len(bufs)
    send_sems, recv_sems = sems

    def body(*refs):
        send_s, recv_s = refs[nb], refs[nb + 1]
        for k, (src, dst, to) in enumerate(plan(refs[:nb])):
            cp = pltpu.make_async_remote_copy(src_ref=src, dst_ref=dst, send_sem=send_s.at[k], recv_sem=recv_s.at[k],
                                              device_id=to, device_id_type=MESH)
            cp.wait_send()
            cp.wait_recv()

    outs = pl.pallas_call(
        body, name=name, out_shape=tuple(pltpu.HBM(b.shape, b.dtype) for b in bufs),
        in_specs=[HBM_SPEC] * nb + [SEM_SPEC, SEM_SPEC, ANY], out_specs=tuple([HBM_SPEC] * nb),
        input_output_aliases={i: i for i in range(nb)},
        compiler_params=pltpu.CompilerParams(has_side_effects=DATAFLOW))(*bufs, send_sems, recv_sems, after)
    return list(outs)


def _slot(px, py, pc):
    return 4 * px + 2 * py + pc


class _Gather:
    def __init__(self, shards, groups, dev):
        self.shards, self.groups, self.dev = shards, groups, dev
        self.second = {}

    @staticmethod
    def _plan1(pairs, refs):
        x, y, c, chips = _place()
        out = []
        for s, l in pairs:
            dst = refs[l].at[_slot(x, y, c)]
            out.append((refs[s], dst, (x, y, 1 - c)))
            out += [(refs[s], dst, (px, py, c)) for px, py in chips]
        return out

    @staticmethod
    def _plan2(refs):
        x, y, c, chips = _place()
        return [(r.at[_slot(px, py, c)],) * 2 + ((x, y, 1 - c),) for r in refs for px, py in chips]

    def start(self):
        n = len(self.shards)
        lands = [lax.dynamic_update_slice(lax.empty((N_DEV,) + s.shape, s.dtype), s[None], (self.dev, 0, 0))
                 for s in self.shards]
        pairs = [(w, n + w) for g in self.groups for w in g]
        sems, bufs, token = _split_start("gather_start_1", list(self.shards) + lands, functools.partial(self._plan1, pairs),
                                         tuple(4 * len(g) for g in self.groups))
        self.first = [(sems[i], [bufs[w] for w in g], [bufs[n + w] for w in g]) for i, g in enumerate(self.groups)]
        return token

    def mid(self, gi, after):
        sems, srcs, lands = self.first[gi]
        m = len(srcs)
        plan = functools.partial(self._plan1, [(w, m + w) for w in range(m)])
        lands = _split_wait("gather_%d_wait_1" % gi, sems, srcs + lands, plan, after)[m:]
        sems, lands, token = _split_start("gather_%d_start_2" % gi, lands, self._plan2, (3 * m,))
        self.second[gi] = (sems[0], lands)
        return token

    def finish(self, gi, after):
        sems, lands = self.second[gi]
        return _split_wait("gather_%d_wait_2" % gi, sems, lands, self._plan2, after)


class _Exchanges:
    def __init__(self, tag, n, core, gather=None):
        self.tag, self.n, self.core, self.gather = tag, n, core, gather

    def weights_mid(self, group, after):
        return self.gather.mid(group, after)

    def weights_finish(self, group, after):
        return self.gather.finish(group, after)

    def _reduce_plan1(self, refs):
        n = self.n
        x, y, c, _ = _place()
        return [(refs[w].at[2 * q + (1 - c)], refs[n + w].at[q], (x, y, 1 - c)) for w in range(n) for q in range(4)]

    def _reduce_plan2(self, refs):
        n = self.n
        x, y, c, chips = _place()
        return [(refs[w].at[2 * px + py], refs[n + w].at[j], (px, py, c))
                for w in range(n) for j, (px, py) in enumerate(chips)]

    def grads_start(self, grads):
        lands = [lax.empty((4,) + g.shape[1:], g.dtype) for g in grads]
        self.g1 = _split_start(self.tag + "reduce_start_1", list(grads) + lands, self._reduce_plan1, (4 * self.n,))
        return self.g1[2]

    def grads_mid(self, after):
        n = self.n
        sems, bufs, _ = self.g1
        bufs = _split_wait(self.tag + "reduce_wait_1", sems[0], bufs, self._reduce_plan1, after)
        core = self.core.reshape(1).astype(jnp.int32)
        self.parts = [_chip_sum(bufs[w], bufs[n + w], core, self.tag + "reduce_chip_sum_%d" % w) for w in range(n)]
        lands = [lax.empty((3,) + p.shape[1:], p.dtype) for p in self.parts]
        self.g2 = _split_start(self.tag + "reduce_start_2", self.parts + lands, self._reduce_plan2, (3 * n,))
        return self.g2[2]

    def grads_finish(self, after):
        n = self.n
        sems, bufs, _ = self.g2
        bufs = _split_wait(self.tag + "reduce_wait_2", sems[0], bufs, self._reduce_plan2, after)
        self.parts, self.terms = bufs[:n], bufs[n:]


def _all_reduce_small(buf):
    R = buf.shape[0]

    def body(x_ref, o_ref, g_ref, send_sems, recv_sems):
        x, y, c, chips = _place()
        me, sibling = (x, y, c), (x, y, 1 - c)

        def slot(px, py, pc):
            return 4 * px + 2 * py + pc

        def copy(k, block, to, src=None):
            dst = g_ref.at[slot(*block)]
            return pltpu.make_async_remote_copy(src_ref=dst if src is None else src, dst_ref=dst,
                                                send_sem=send_sems.at[k], recv_sem=recv_sems.at[k],
                                                device_id=to, device_id_type=MESH)

        first = [copy(0, me, sibling, src=x_ref)]
        first += [copy(1 + j, me, (*chip, c), src=x_ref) for j, chip in enumerate(chips)]
        for cp in first:
            cp.start()
        g_ref[slot(*me)] = x_ref[...]
        passed = [copy(4 + j, (*chip, c), sibling) for j, chip in enumerate(chips)]
        for j, chip in enumerate(chips):
            copy(1 + j, (*chip, c), me).wait_recv()
            passed[j].start()
        copy(0, sibling, me).wait_recv()
        for j, chip in enumerate(chips):
            copy(4 + j, (*chip, 1 - c), me).wait_recv()
        for cp in first + passed:
            cp.wait_send()
        acc = g_ref[0]
        for s in range(1, N_DEV):
            acc = acc + g_ref[s]
        o_ref[...] = acc

    vm = pl.BlockSpec(memory_space=pltpu.VMEM)
    return pl.pallas_call(
        body, in_specs=[vm], out_specs=vm, out_shape=jax.ShapeDtypeStruct((R, LANE), F32),
        scratch_shapes=[pltpu.VMEM((N_DEV, R, LANE), F32), pltpu.SemaphoreType.DMA((7,)), pltpu.SemaphoreType.DMA((7,))],
        name="all_reduce_small")(buf)


def _pad_cols(a, width):
    return jnp.pad(a, ((0, 0), (0, width - a.shape[1])))


def _local_step(x, target, w_in_t, conv_w, a_log, dt_bias, delta_norm_w, sinks, rel_bias, ln1_g, ln1_b, ln2_g, ln2_b,
                ex, ex_in):
    S, D = x.shape
    aw = D // 2
    hq, hkv, nh = aw // HEAD_A, aw // HEAD_A // GQA, aw // HEAD_D
    kvw = hkv * HEAD_A
    c_q, c_k, c_v, c_d = 0, aw, aw + kvw, aw + 2 * kvw
    c_ab = c_d + 3 * aw
    c_z = c_ab + 2 * nh
    n_in = c_z + aw
    assert w_in_t.shape == (n_in, D), (w_in_t.shape, n_in)
    w_pt = jnp.concatenate([w_in_t[:c_ab], w_in_t[c_z:], jnp.pad(w_in_t[c_ab:c_z], ((0, LANE - 2 * nh), (0, 0)))], axis=0)
    p_z, p_ab = c_ab, c_ab + aw
    n_p = p_ab + LANE

    xb = x.astype(BF16)
    proj = _matmul(xb, w_pt, NT, name="proj", tn=1152)
    bias = _attn_bias(rel_bias.T)
    attn_out = _attn_fwd(proj, bias, sinks.reshape(-1), hq, 0, c_k // kvw, c_v // kvw, D)
    conv2 = conv_w.reshape(CONV_W, 3 * aw)
    qkv = _gdn_prep_fwd(proj, conv2, nh, c_d // HEAD_D)
    ab = proj[:, p_ab:]
    al, dt = _pad_cols(a_log, LANE), _pad_cols(dt_bias, LANE)
    gb = _gates_fwd(ab, al, dt, nh)
    u_d, wqk, attn_d, t_d = _gdn_local_fwd(qkv, gb, nh)
    o_d, vn, states = _gdn_scan_fwd(u_d, wqk, attn_d, gb, nh, ex.weights_mid(1, u_d))
    mix = _gated_norm_fwd(o_d, proj, delta_norm_w, nh, p_z // HEAD_D, attn_out, aw // HEAD_D)
    w_o_g, w_up_g = ex.weights_finish(1, mix)
    w_o = w_o_g.reshape(D, D)
    mixed = _matmul(mix, w_o, NN, name="out_proj")
    h1 = _ln1_fwd(x, mixed, ln1_g, ln1_b)
    u = _matmul(h1, w_up_g, NN, name="mlp_up", b_groups=True, out_dtype=BF16, deps=(ex.weights_mid(2, h1),))
    (w_down_g,) = ex.weights_finish(2, u)
    w_down = w_down_g.reshape(-1, D)
    mlp = _matmul(u, w_down, NN, name="mlp_down", a_fn=_relu_sq, tm=512, tn=512, tk=8192)
    dr2, loss_row, dln2_g, dln2_b = _ln2_loss(h1, mlp, ln2_g, ln2_b, target)

    du = _matmul(dr2, w_down, NT, name="d_mlp_act", epi=_relu_sq_grad, epi_in=(u,), out_dtype=BF16)
    dw_down = _matmul(u, dr2, TN, name="dw_down", a_fn=_relu_sq, out_dtype=BF16)
    dw_up = _matmul(h1, du, TN, name="dw_up", out_dtype=BF16, out_groups=N_DEV)
    dh_mlp = _matmul(du, w_up_g, NT, name="d_h1", b_groups=True, tm=512, tn=512, tk=8192)
    dr1, dln1_g, dln1_b = _ln1_bwd(x, mixed, ln1_g, dr2, dh_mlp)
    dw_o = _matmul(mix, dr1, TN, name="dw_o", out_dtype=BF16)
    tok = ex.grads_start([dw_o.reshape(N_DEV, -1, D), dw_up, dw_down.reshape(N_DEV, -1, D)])
    dmix = _matmul(dr1, w_o, NT, name="d_mix", deps=(tok,))
    dq_a, dk_a, dv_a, dbias, dsink = _attn_bwd(proj, bias, sinks.reshape(-1), mix, dmix, hq, 0, c_k // kvw, c_v // kvw)
    drel = _rel_bias_grad(dbias, hq)
    do_d, dz, dnw = _gated_norm_bwd(o_d, proj, delta_norm_w, dmix, nh, p_z // HEAD_D, aw // HEAD_D)
    dvn_s, dw_s, dqd, dkd, dattn_d, dgl = _gdn_scan_bwd(wqk, attn_d, gb, states, vn, do_d, nh, ex.grads_mid(dq_a))
    dqkv_n, dg, dbeta = _gdn_local_bwd(qkv, gb, t_d, u_d, wqk, dvn_s, dw_s, dqd, dkd, dattn_d, dgl, nh)
    dgb = _pad_cols(jnp.concatenate([dg.reshape(nh, S).T, dbeta.reshape(nh, S).T], axis=1), LANE)
    dab, da_log, ddt_bias = _gates_bwd(ab, al, dt, dgb, nh)
    dqkv_d, dconv = _gdn_prep_bwd(proj, conv2, dqkv_n, nh, c_d // HEAD_D)
    dproj = jnp.concatenate([dq_a, dk_a, dv_a, dqkv_d, dz, dab], axis=1)
    dw_pt = _matmul(dproj, xb, TN, name="dw_in", out_dtype=BF16, tm=1152)
    dw_in_t = jnp.concatenate([dw_pt[:p_z], dw_pt[p_ab:p_ab + 2 * nh], dw_pt[p_z:p_ab]], axis=0)
    tok = ex_in.grads_mid(ex_in.grads_start([dw_in_t.reshape(N_DEV, -1, D)]))
    grad_x = _matmul(dproj, w_pt, NN, name="d_x", tk=1920, deps=(tok,), epi=_residual_grad, epi_in=(dr1,))
    ex.grads_finish(grad_x)

    small = dict(conv_w=dconv, a_log=da_log[:, :nh], dt_bias=ddt_bias[:, :nh], delta_norm_w=dnw,
                 attn_sinks=dsink[:, 0].reshape(1, hq), rel_bias=drel[:, :, 0].T,
                 ln1_g=dln1_g, ln1_b=dln1_b, ln2_g=dln2_g, ln2_b=dln2_b)
    return loss_row, grad_x, small


SMALL_ORDER = ("conv_w", "a_log", "dt_bias", "delta_norm_w", "attn_sinks", "rel_bias", "ln1_g", "ln1_b", "ln2_g", "ln2_b")


def _pack_small(loss_row, small):
    parts = [loss_row.reshape(-1)]
    for k in SMALL_ORDER:
        flat = small[k].reshape(-1)
        parts.append(jnp.pad(flat, (0, (-flat.shape[0]) % LANE)))
    flat = jnp.concatenate(parts)
    flat = jnp.pad(flat, (0, (-flat.shape[0]) % (8 * LANE)))
    return flat.reshape(-1, LANE)


def _unpack_small(buf, small_shapes):
    flat = buf.reshape(-1)
    loss = flat[0]
    off = LANE
    out = {}
    for k in SMALL_ORDER:
        n = int(np.prod(small_shapes[k]))
        out[k] = flat[off:off + n].reshape(small_shapes[k])
        off += n + (-n) % LANE
    return loss, out


def kernel(x, w_in, conv_w, a_log, dt_bias, delta_norm_w, attn_sinks, rel_bias, w_o, ln1_g, ln1_b, w_up, w_down, ln2_g, ln2_b, loss_target, m_w_in, m_conv_w, m_a_log, m_dt_bias, m_delta_norm_w, m_attn_sinks, m_rel_bias, m_w_o, m_ln1_g, m_ln1_b, m_w_up, m_w_down, m_ln2_g, m_ln2_b, v_w_in, v_conv_w, v_a_log, v_dt_bias, v_delta_norm_w, v_attn_sinks, v_rel_bias, v_w_o, v_ln1_g, v_ln1_b, v_w_up, v_w_down, v_ln2_g, v_ln2_b):
    S, D = x.shape[1], x.shape[2]
    core = lax.axis_index("c")
    dev = 4 * lax.axis_index("x") + 2 * lax.axis_index("y") + core

    gather = _Gather([conv_w[0, :, 0, :], w_in[0].T.astype(BF16), w_o[0].astype(BF16), w_up[0].astype(BF16),
                      w_down[0].astype(BF16)], [[0, 1], [2, 3], [4]], dev)
    conv_g, w_in_g = gather.finish(0, gather.mid(0, gather.start()))
    w_in_t = w_in_g.reshape(-1, D)
    ex = _Exchanges("", 3, core, gather)
    ex_in = _Exchanges("in_", 1, core)

    cw_sh = conv_w.shape[3]
    conv_full = jnp.transpose(conv_g, (1, 0, 2)).reshape(CONV_W, N_DEV * cw_sh)

    loss_row, grad_x, small = _local_step(
        x[0], loss_target[0], w_in_t, conv_full, a_log, dt_bias, delta_norm_w, attn_sinks, rel_bias,
        ln1_g, ln1_b, ln2_g, ln2_b, ex, ex_in)

    chip_arr = (dev // 2).reshape(1).astype(jnp.int32)
    big = {}
    for i, (name, w, m, v) in enumerate((("w_o", w_o, m_w_o, v_w_o), ("w_up", w_up, m_w_up, v_w_up),
                                         ("w_down", w_down, m_w_down, v_w_down))):
        big[name] = [o[None] for o in _adamw_big(ex.parts[i], ex.terms[i], chip_arr, w[0], m[0], v[0], "adamw_" + name)]
    ex_in.grads_finish(big["w_down"][0])
    outs = _adamw_big(ex_in.parts[0], ex_in.terms[0], chip_arr, w_in[0].T, m_w_in[0].T, v_w_in[0].T, "adamw_w_in")
    big["w_in"] = [o.T[None] for o in outs]

    small_shapes = {k: v.shape for k, v in small.items()}
    loss, small = _unpack_small(_all_reduce_small(_pack_small(loss_row, small)), small_shapes)
    small["conv_w"] = lax.dynamic_slice(small["conv_w"], (0, dev * cw_sh), (CONV_W, cw_sh))
    small["rel_bias"] = small["rel_bias"].reshape(rel_bias.shape)
    p2 = dict(conv_w=(conv_w, m_conv_w, v_conv_w), a_log=(a_log, m_a_log, v_a_log), dt_bias=(dt_bias, m_dt_bias, v_dt_bias),
              delta_norm_w=(delta_norm_w, m_delta_norm_w, v_delta_norm_w), attn_sinks=(attn_sinks, m_attn_sinks, v_attn_sinks),
              rel_bias=(rel_bias, m_rel_bias, v_rel_bias), ln1_g=(ln1_g, m_ln1_g, v_ln1_g), ln1_b=(ln1_b, m_ln1_b, v_ln1_b),
              ln2_g=(ln2_g, m_ln2_g, v_ln2_g), ln2_b=(ln2_b, m_ln2_b, v_ln2_b))
    two_d = lambda a: a.reshape(-1, a.shape[-1])
    ws = [two_d(p2[k][0]) for k in SMALL_ORDER]
    gs = [two_d(small[k]) for k in SMALL_ORDER]
    ms = [two_d(p2[k][1]) for k in SMALL_ORDER]
    vs = [two_d(p2[k][2]) for k in SMALL_ORDER]
    ds, nms, nvs = _adamw_small(ws, gs, ms, vs)
    res = {}
    for i, k in enumerate(SMALL_ORDER):
        shp = p2[k][0].shape
        res[k] = [gs[i].reshape(shp), ds[i].reshape(shp), nms[i].reshape(shp), nvs[i].reshape(shp)]
    res.update(big)
    order = ("w_in", "conv_w", "a_log", "dt_bias", "delta_norm_w", "attn_sinks", "rel_bias", "w_o", "ln1_g", "ln1_b",
             "w_up", "w_down", "ln2_g", "ln2_b")
    return (loss, grad_x[None], *[res[k][0] for k in order], *[res[k][1] for k in order],
            *[res[k][2] for k in order], *[res[k][3] for k in order])
```

```python
import functools
import math

import numpy as np
import jax
import jax.numpy as jnp
from jax import lax
from jax.experimental import pallas as pl
from jax.experimental.pallas import tpu as pltpu

F32 = jnp.float32
BF16 = jnp.bfloat16
HIGHEST = lax.Precision.HIGHEST

N_DEV = 8
HEAD_A = 64
GQA = 4
BLK = 128
N_BUCKETS = 32
MAX_DISTANCE = 128
HEAD_D = 128
CONV_W = 4
CHUNK = 64
NEG_INF = -1e30
LN_EPS = 1e-5
RMS_EPS = 1e-6
DN_ALPHA = 2.0 ** 0.25
ADAM_LR, ADAM_B1, ADAM_B2, ADAM_EPS, ADAM_WD, ADAM_STEP = 0.001, 0.9, 0.999, 1e-08, 0.01, 10

LANE = 128
VMEM_LIMIT = 56 * 1024 * 1024

NN = ((1,), (0,))
NT = ((1,), (1,))
TN = ((0,), (0,))


def _dot(a, b, dims, prec=None):
    return lax.dot_general(a, b, (dims, ((), ())), precision=prec, preferred_element_type=F32)


def _tile(dim, pref):
    if dim <= pref:
        return dim
    t = (pref // LANE) * LANE
    while t > LANE and dim % t:
        t -= LANE
    assert dim % t == 0, (dim, pref)
    return t


def _params(sem):
    return pltpu.CompilerParams(dimension_semantics=sem, vmem_limit_bytes=VMEM_LIMIT)


def _matmul(a, b, dims, *, name, out_dtype=F32, tm=1024, tn=1024, tk=2048, a_fn=None, epi=None, epi_in=(),
            b_groups=None, out_groups=None, deps=()):
    (ca,), (cb,) = dims
    M, K = a.shape[1 - ca], a.shape[ca]
    if b_groups:
        G, R, C = b.shape
        bshape = (R, G * C)
    else:
        bshape = b.shape
    N = bshape[1 - cb]
    assert bshape[cb] == K, (a.shape, b.shape, dims)
    tm, tk = _tile(M, tm), _tile(K, tk)
    if b_groups:
        lim = C if cb == 0 else tn
        tn = _tile(N, min(tn, lim))
        if cb == 1 and tk < C:
            tk = _tile(K, min(tk, C))
        elif cb == 1:
            tk = C * max(1, tk // C)
    else:
        tn = _tile(N, tn)
    if out_groups:
        tn = _tile(N, min(tn, N // out_groups))
    nk = K // tk
    b_span = tk // C if (b_groups and cb == 1 and tk > C) else 1

    def body(*refs):
        a_ref, b_ref = refs[0], refs[1]
        e_refs = refs[2:2 + len(epi_in)]
        o_ref = refs[2 + len(epi_in) + len(deps)]
        acc_ref = refs[3 + len(epi_in) + len(deps)] if nk > 1 else None
        k = pl.program_id(2)
        av = a_ref[...]
        if a_fn is not None:
            av = a_fn(av)
        if b_span > 1:
            prod = sum(_dot(av[:, g * C:(g + 1) * C].astype(BF16), b_ref[g].astype(BF16), dims) for g in range(b_span))
        else:
            prod = _dot(av.astype(BF16), b_ref[...].astype(BF16), dims)

        def finish(r):
            if epi is not None:
                r = epi(r, *[e[...] for e in e_refs])
            o_ref[...] = r.astype(out_dtype)

        if nk == 1:
            finish(prod)
            return

        @pl.when(k == 0)
        def _():
            acc_ref[...] = prod

        @pl.when(k > 0)
        def _():
            acc_ref[...] += prod

        @pl.when(k == nk - 1)
        def _():
            finish(acc_ref[...])

    a_spec = (pl.BlockSpec((tm, tk), lambda i, j, k: (i, k)) if ca == 1
              else pl.BlockSpec((tk, tm), lambda i, j, k: (k, i)))
    if b_groups:
        if cb == 0:
            per = C // tn
            b_spec = pl.BlockSpec((None, tk, tn), lambda i, j, k: (j // per, k, j % per))
        elif b_span > 1:
            b_spec = pl.BlockSpec((b_span, tn, C), lambda i, j, k: (k, j, 0))
        else:
            per = C // tk
            b_spec = pl.BlockSpec((None, tn, tk), lambda i, j, k: (k // per, j, k % per))
    else:
        b_spec = (pl.BlockSpec((tk, tn), lambda i, j, k: (k, j)) if cb == 0
                  else pl.BlockSpec((tn, tk), lambda i, j, k: (j, k)))
    e_specs = [pl.BlockSpec((tm, tn), lambda i, j, k: (i, j)) for _ in epi_in]
    if out_groups:
        per_o = (N // out_groups) // tn
        o_spec = pl.BlockSpec((None, tm, tn), lambda i, j, k: (j // per_o, i, j % per_o))
        o_shape = jax.ShapeDtypeStruct((out_groups, M, N // out_groups), out_dtype)
    else:
        o_spec = pl.BlockSpec((tm, tn), lambda i, j, k: (i, j))
        o_shape = jax.ShapeDtypeStruct((M, N), out_dtype)
    return pl.pallas_call(
        body, grid=(M // tm, N // tn, nk), out_specs=o_spec,
        in_specs=[a_spec, b_spec] + e_specs + [pl.BlockSpec(memory_space=pl.ANY)] * len(deps),
        out_shape=o_shape, scratch_shapes=[pltpu.VMEM((tm, tn), F32)] if nk > 1 else [],
        compiler_params=_params(("parallel", "parallel", "arbitrary")), name=name)(a, b, *epi_in, *deps)


def _relu_sq(u):
    r = jnp.maximum(u, 0.0)
    return r * r


def _relu_sq_grad(acc, u):
    return acc * (2.0 * jnp.maximum(u, 0.0))


def _ln_stats(r):
    mu = jnp.mean(r, axis=-1, keepdims=True)
    xc = r - mu
    var = jnp.mean(xc * xc, axis=-1, keepdims=True)
    rstd = lax.rsqrt(var + LN_EPS)
    return xc * rstd, rstd


def _ln_bwd(dy, xhat, rstd, g):
    dxh = dy * g
    m1 = jnp.mean(dxh, axis=-1, keepdims=True)
    m2 = jnp.mean(dxh * xhat, axis=-1, keepdims=True)
    return rstd * (dxh - m1 - xhat * m2)


def _row_call(body, ins, row_ins, outs, acc_outs, name, tr=256):
    S = ins[0].shape[0]
    tr = min(tr, S)
    n_in, n_row, n_out = len(ins), len(row_ins), len(outs)

    def wrapped(*refs):
        i = pl.program_id(0)
        acc_refs = refs[n_in + n_row + n_out:]

        @pl.when(i == 0)
        def _():
            for r in acc_refs:
                r[...] = jnp.zeros_like(r)

        body(*refs)

    in_specs = [pl.BlockSpec((tr, a.shape[1]), lambda i: (i, 0)) for a in ins]
    in_specs += [pl.BlockSpec(a.shape, lambda i: (0, 0)) for a in row_ins]
    out_specs = [pl.BlockSpec((tr, s.shape[1]), lambda i: (i, 0)) for s in outs]
    out_specs += [pl.BlockSpec(s.shape, lambda i: (0, 0)) for s in acc_outs]
    return pl.pallas_call(wrapped, grid=(S // tr,), in_specs=in_specs, out_specs=out_specs,
                          out_shape=list(outs) + list(acc_outs),
                          compiler_params=_params(("arbitrary",)), name=name)(*ins, *row_ins)


def _ln1_fwd(x, mixed, g, b):
    def body(x_ref, m_ref, g_ref, b_ref, h_ref):
        xhat, _ = _ln_stats(DN_ALPHA * x_ref[...] + m_ref[...])
        h_ref[...] = xhat * g_ref[...] + b_ref[...]
    return _row_call(body, [x, mixed], [g, b], [jax.ShapeDtypeStruct(x.shape, F32)], [], "ln1_fwd")[0]


def _ln2_loss(h1, mlp, g, b, target):
    S, D = h1.shape
    sds = jax.ShapeDtypeStruct

    def body(h_ref, m_ref, t_ref, g_ref, b_ref, dr_ref, loss_ref, dg_ref, db_ref):
        xhat, rstd = _ln_stats(DN_ALPHA * h_ref[...] + m_ref[...])
        gv = g_ref[...]
        err = xhat * gv + b_ref[...] - t_ref[...]
        loss_ref[...] += jnp.sum(jnp.sum(err * err, axis=0, keepdims=True), axis=1, keepdims=True) * (0.5 / D)
        dy = err * (1.0 / D)
        dg_ref[...] += jnp.sum(dy * xhat, axis=0, keepdims=True)
        db_ref[...] += jnp.sum(dy, axis=0, keepdims=True)
        dr_ref[...] = _ln_bwd(dy, xhat, rstd, gv)

    return _row_call(body, [h1, mlp, target], [g, b], [sds((S, D), F32)],
                     [sds((1, LANE), F32), sds((1, D), F32), sds((1, D), F32)], "ln2_loss")


def _ln1_bwd(x, mixed, g, dr2, dh_mlp):
    S, D = x.shape
    sds = jax.ShapeDtypeStruct

    def body(x_ref, m_ref, dr2_ref, dh_ref, g_ref, dr_ref, dg_ref, db_ref):
        xhat, rstd = _ln_stats(DN_ALPHA * x_ref[...] + m_ref[...])
        dy = DN_ALPHA * dr2_ref[...] + dh_ref[...]
        dg_ref[...] += jnp.sum(dy * xhat, axis=0, keepdims=True)
        db_ref[...] += jnp.sum(dy, axis=0, keepdims=True)
        dr_ref[...] = _ln_bwd(dy, xhat, rstd, g_ref[...])

    return _row_call(body, [x, mixed, dr2, dh_mlp], [g], [sds((S, D), F32)],
                     [sds((1, D), F32), sds((1, D), F32)], "ln1_bwd")


def _residual_grad(acc, dr):
    return DN_ALPHA * dr + acc


def _bucket_table():
    qi = np.arange(BLK, dtype=np.int32)[:, None]
    kj = np.arange(2 * BLK, dtype=np.int32)[None, :]
    dist = qi + BLK - kj
    n = np.maximum(dist, 0)
    max_exact = N_BUCKETS // 2
    nf = np.maximum(n, 1).astype(np.float32)
    large = max_exact + (np.log(nf / np.float32(max_exact)) / np.float32(math.log(MAX_DISTANCE / max_exact))
                         * np.float32(N_BUCKETS - max_exact)).astype(np.int32)
    large = np.minimum(large, N_BUCKETS - 1)
    bucket = np.where(n < max_exact, n, large)
    return np.where((dist >= 0) & (dist < BLK), bucket, -1).astype(np.int32)


def _attn_bias(rel_bias_t):
    hq = rel_bias_t.shape[0]
    bucket = jnp.asarray(_bucket_table())

    def body(rb_ref, bk_ref, o_ref):
        h = pl.program_id(0)
        bk = bk_ref[...]
        acc = jnp.zeros((BLK, 2 * BLK), F32)
        for b in range(N_BUCKETS):
            acc = jnp.where(bk == b, rb_ref[h, b], acc)
        o_ref[...] = acc

    return pl.pallas_call(
        body, grid=(hq,),
        in_specs=[pl.BlockSpec(memory_space=pltpu.SMEM), pl.BlockSpec((BLK, 2 * BLK), lambda h: (0, 0))],
        out_specs=pl.BlockSpec((BLK, 2 * BLK), lambda h: (h, 0)),
        out_shape=jax.ShapeDtypeStruct((hq * BLK, 2 * BLK), F32),
        compiler_params=_params(("arbitrary",)), name="attn_bias")(rel_bias_t, bucket)


def _attn_probs(sc, sp, bias, sink, mask_c, mask_p):
    lc = jnp.where(mask_c, sc + bias[:, BLK:], NEG_INF)
    lp = jnp.where(mask_p, sp + bias[:, :BLK], NEG_INF)
    m = jnp.maximum(jnp.maximum(jnp.max(lc, axis=1, keepdims=True), jnp.max(lp, axis=1, keepdims=True)), sink)
    pc, pp, ps = jnp.exp(lc - m), jnp.exp(lp - m), jnp.exp(sink - m)
    inv = 1.0 / (jnp.sum(pc, axis=1, keepdims=True) + jnp.sum(pp, axis=1, keepdims=True) + ps)
    return pc, pp, ps, inv


def _attn_masks(n):
    qi = lax.broadcasted_iota(jnp.int32, (BLK, BLK), 0)
    kj = lax.broadcasted_iota(jnp.int32, (BLK, BLK), 1)
    return kj <= qi, (kj > qi) & (n > 0)


def _attn_fwd(proj, bias, sinks, hq, q_blk, k_blk, v_blk, out_width):
    S = proj.shape[0]
    hkv = hq // GQA
    wq, wk = hq * HEAD_A, hkv * HEAD_A

    def body(q_ref, k_ref, v_ref, bias_ref, sink_ref, o_ref):
        n = pl.program_id(0)
        cur = pl.multiple_of(n * BLK, BLK)
        prev = pl.multiple_of(jnp.maximum(n - 1, 0) * BLK, BLK)
        mask_c, mask_p = _attn_masks(n)
        for h4 in range(hkv):
            cs = slice(h4 * HEAD_A, (h4 + 1) * HEAD_A)
            kc, kp = k_ref[pl.ds(cur, BLK), cs].astype(BF16), k_ref[pl.ds(prev, BLK), cs].astype(BF16)
            vc, vp = v_ref[pl.ds(cur, BLK), cs].astype(BF16), v_ref[pl.ds(prev, BLK), cs].astype(BF16)
            hs_of = [slice(h * HEAD_A, (h + 1) * HEAD_A) for h in range(h4 * GQA, (h4 + 1) * GQA)]
            qs = [(q_ref[:, hs] * (HEAD_A ** -0.5)).astype(BF16) for hs in hs_of]
            scs = [_dot(q, kc, NT) for q in qs]
            sps = [_dot(q, kp, NT) for q in qs]
            pr = [_attn_probs(scs[g], sps[g], bias_ref[(h4 * GQA + g) * BLK:(h4 * GQA + g + 1) * BLK, :],
                              sink_ref[h4 * GQA + g], mask_c, mask_p) for g in range(GQA)]
            oc = [_dot(p[0].astype(BF16), vc, NN) for p in pr]
            op = [_dot(p[1].astype(BF16), vp, NN) for p in pr]
            for g, hs in enumerate(hs_of):
                o_ref[:, hs] = (oc[g] + op[g]) * pr[g][3]

    return pl.pallas_call(
        body, grid=(S // BLK,),
        in_specs=[pl.BlockSpec((BLK, wq), lambda n: (n, q_blk)), pl.BlockSpec((S, wk), lambda n: (0, k_blk)),
                  pl.BlockSpec((S, wk), lambda n: (0, v_blk)), pl.BlockSpec((hq * BLK, 2 * BLK), lambda n: (0, 0)),
                  pl.BlockSpec(memory_space=pltpu.SMEM)],
        out_specs=pl.BlockSpec((BLK, wq), lambda n: (n, 0)),
        out_shape=jax.ShapeDtypeStruct((S, out_width), F32),
        compiler_params=_params(("arbitrary",)), name="attn_fwd")(proj, proj, proj, bias, sinks)


def _attn_bwd(proj, bias, sinks, out, dmix, hq, q_blk, k_blk, v_blk):
    S = proj.shape[0]
    hkv = hq // GQA
    wq, wk = hq * HEAD_A, hkv * HEAD_A
    sds = jax.ShapeDtypeStruct

    def body(q_ref, k_ref, v_ref, bias_ref, sink_ref, o_ref, do_ref, dq_ref, dk_ref, dv_ref, dbias_ref, dsink_ref):
        n = pl.program_id(0)

        @pl.when(n == 0)
        def _():
            dk_ref[...] = jnp.zeros_like(dk_ref)
            dv_ref[...] = jnp.zeros_like(dv_ref)
            dbias_ref[...] = jnp.zeros_like(dbias_ref)
            dsink_ref[...] = jnp.zeros_like(dsink_ref)

        cur = pl.multiple_of(n * BLK, BLK)
        prev = pl.multiple_of(jnp.maximum(n - 1, 0) * BLK, BLK)
        mask_c, mask_p = _attn_masks(n)
        for h4 in range(hkv):
            cs = slice(h4 * HEAD_A, (h4 + 1) * HEAD_A)
            kc, kp = k_ref[pl.ds(cur, BLK), cs].astype(BF16), k_ref[pl.ds(prev, BLK), cs].astype(BF16)
            vc, vp = v_ref[pl.ds(cur, BLK), cs].astype(BF16), v_ref[pl.ds(prev, BLK), cs].astype(BF16)
            heads = list(range(h4 * GQA, (h4 + 1) * GQA))
            hs_of = [slice(h * HEAD_A, (h + 1) * HEAD_A) for h in heads]
            rows_of = [slice(h * BLK, (h + 1) * BLK) for h in heads]
            G = range(GQA)
            qs = [(q_ref[:, hs] * (HEAD_A ** -0.5)).astype(BF16) for hs in hs_of]
            dos = [do_ref[:, hs] for hs in hs_of]
            dobs = [d.astype(BF16) for d in dos]
            scs = [_dot(q, kc, NT) for q in qs]
            sps = [_dot(q, kp, NT) for q in qs]
            dpc = [_dot(d, vc, NT) for d in dobs]
            dpp = [_dot(d, vp, NT) for d in dobs]
            pcs, pps, dscs, dsps = [], [], [], []
            for g in G:
                pc, pp, ps, inv = _attn_probs(scs[g], sps[g], bias_ref[rows_of[g], :], sink_ref[heads[g]], mask_c, mask_p)
                pc, pp, ps = pc * inv, pp * inv, ps * inv
                delta = jnp.sum(dos[g] * o_ref[:, hs_of[g]], axis=1, keepdims=True)
                dsc, dsp = pc * (dpc[g] - delta), pp * (dpp[g] - delta)
                dsink_ref[heads[g]:heads[g] + 1, :] += jnp.broadcast_to(jnp.sum(-ps * delta, axis=0, keepdims=True), (1, LANE))
                dbias_ref[rows_of[g], BLK:] += dsc
                dbias_ref[rows_of[g], :BLK] += dsp
                pcs.append(pc.astype(BF16))
                pps.append(pp.astype(BF16))
                dscs.append(dsc.astype(BF16))
                dsps.append(dsp.astype(BF16))
            dq1 = [_dot(dscs[g], kc, NN) for g in G]
            dq2 = [_dot(dsps[g], kp, NN) for g in G]
            dkc = [_dot(dscs[g], qs[g], TN) for g in G]
            dkp = [_dot(dsps[g], qs[g], TN) for g in G]
            dvc = [_dot(pcs[g], dobs[g], TN) for g in G]
            dvp = [_dot(pps[g], dobs[g], TN) for g in G]
            for g in G:
                dq_ref[:, hs_of[g]] = (dq1[g] + dq2[g]) * (HEAD_A ** -0.5)
            dk_ref[pl.ds(cur, BLK), cs] += sum(dkc[1:], dkc[0])
            dk_ref[pl.ds(prev, BLK), cs] += sum(dkp[1:], dkp[0])
            dv_ref[pl.ds(cur, BLK), cs] += sum(dvc[1:], dvc[0])
            dv_ref[pl.ds(prev, BLK), cs] += sum(dvp[1:], dvp[0])

    return pl.pallas_call(
        body, grid=(S // BLK,),
        in_specs=[pl.BlockSpec((BLK, wq), lambda n: (n, q_blk)), pl.BlockSpec((S, wk), lambda n: (0, k_blk)),
                  pl.BlockSpec((S, wk), lambda n: (0, v_blk)), pl.BlockSpec((hq * BLK, 2 * BLK), lambda n: (0, 0)),
                  pl.BlockSpec(memory_space=pltpu.SMEM),
                  pl.BlockSpec((BLK, wq), lambda n: (n, 0)), pl.BlockSpec((BLK, wq), lambda n: (n, 0))],
        out_specs=[pl.BlockSpec((BLK, wq), lambda n: (n, 0)), pl.BlockSpec((S, wk), lambda n: (0, 0)),
                   pl.BlockSpec((S, wk), lambda n: (0, 0)), pl.BlockSpec((hq * BLK, 2 * BLK), lambda n: (0, 0)),
                   pl.BlockSpec((hq, LANE), lambda n: (0, 0))],
        out_shape=[sds((S, wq), F32), sds((S, wk), F32), sds((S, wk), F32), sds((hq * BLK, 2 * BLK), F32),
                   sds((hq, LANE), F32)],
        compiler_params=_params(("arbitrary",)), name="attn_bwd")(proj, proj, proj, bias, sinks, out, dmix)


def _rel_bias_grad(dbias, hq):
    bucket = jnp.asarray(_bucket_table())

    def body(d_ref, bk_ref, o_ref):
        d = d_ref[...]
        bk = bk_ref[...]
        rows = [jnp.sum(jnp.where(bk == b, d, 0.0), axis=0, keepdims=True) for b in range(N_BUCKETS)]
        tot = jnp.sum(jnp.concatenate(rows, axis=0), axis=1, keepdims=True)
        o_ref[...] = jnp.broadcast_to(tot, (N_BUCKETS, LANE))

    return pl.pallas_call(
        body, grid=(hq,),
        in_specs=[pl.BlockSpec((BLK, 2 * BLK), lambda h: (h, 0)), pl.BlockSpec((BLK, 2 * BLK), lambda h: (0, 0))],
        out_specs=pl.BlockSpec((None, N_BUCKETS, LANE), lambda h: (h, 0, 0)),
        out_shape=jax.ShapeDtypeStruct((hq, N_BUCKETS, LANE), F32),
        compiler_params=_params(("arbitrary",)), name="rel_bias_grad")(dbias, bucket)


def _sigmoid(x):
    return 1.0 / (1.0 + jnp.exp(-x))


def _shift_rows(x, s):
    n = x.shape[0]
    row = lax.broadcasted_iota(jnp.int32, x.shape, 0)
    if s > 0:
        return jnp.where(row >= s, pltpu.roll(x, s, 0), 0.0)
    return jnp.where(row < n + s, pltpu.roll(x, n + s, 0), 0.0)


def _conv_silu_norm(xv, w, j, nh):
    c = w[CONV_W - 1:CONV_W, :] * xv
    for s in range(1, CONV_W):
        c = c + w[CONV_W - 1 - s:CONV_W - s, :] * _shift_rows(xv, s)
    sg = _sigmoid(c)
    a = c * sg
    r = lax.rsqrt(jnp.sum(a * a, axis=1, keepdims=True) + RMS_EPS)
    scale = jnp.where(j < nh, HEAD_D ** -0.5, 1.0)
    is_norm = j < 2 * nh
    y = jnp.where(is_norm, a * (r * scale), a)
    return c, sg, a, r, scale, is_norm, y


def _gdn_prep_fwd(proj, conv_w, nh, blk0):
    S = proj.shape[0]

    def body(x_ref, w_ref, o_ref):
        j = pl.program_id(0)
        o_ref[...] = _conv_silu_norm(x_ref[...], w_ref[...], j, nh)[-1]

    return pl.pallas_call(
        body, grid=(3 * nh,),
        in_specs=[pl.BlockSpec((S, HEAD_D), lambda j: (0, blk0 + j)), pl.BlockSpec((CONV_W, HEAD_D), lambda j: (0, j))],
        out_specs=pl.BlockSpec((S, HEAD_D), lambda j: (0, 3 * (j % nh) + j // nh)),
        out_shape=jax.ShapeDtypeStruct((S, 3 * nh * HEAD_D), F32),
        compiler_params=_params(("parallel",)), name="gdn_prep_fwd")(proj, conv_w)


def _gdn_prep_bwd(proj, conv_w, dqkv, nh, blk0):
    S = proj.shape[0]
    sds = jax.ShapeDtypeStruct

    def body(x_ref, w_ref, dy_ref, dx_ref, dw_ref):
        j = pl.program_id(0)
        xv, w = x_ref[...], w_ref[...]
        c, sg, a, r, scale, is_norm, _ = _conv_silu_norm(xv, w, j, nh)
        dy = dy_ref[...]
        rs = r * scale
        da_n = rs * dy - a * (r * r * rs) * jnp.sum(dy * a, axis=1, keepdims=True)
        da = jnp.where(is_norm, da_n, dy)
        dc = da * (sg * (1.0 + c * (1.0 - sg)))
        dx = w[CONV_W - 1:CONV_W, :] * dc
        dws = [jnp.sum(dc * xv, axis=0, keepdims=True)]
        for s in range(1, CONV_W):
            dx = dx + w[CONV_W - 1 - s:CONV_W - s, :] * _shift_rows(dc, -s)
            dws.insert(0, jnp.sum(dc * _shift_rows(xv, s), axis=0, keepdims=True))
        dx_ref[...] = dx
        dw_ref[...] = jnp.concatenate(dws, axis=0)

    return pl.pallas_call(
        body, grid=(3 * nh,),
        in_specs=[pl.BlockSpec((S, HEAD_D), lambda j: (0, blk0 + j)), pl.BlockSpec((CONV_W, HEAD_D), lambda j: (0, j)),
                  pl.BlockSpec((S, HEAD_D), lambda j: (0, 3 * (j % nh) + j // nh))],
        out_specs=[pl.BlockSpec((S, HEAD_D), lambda j: (0, j)), pl.BlockSpec((CONV_W, HEAD_D), lambda j: (0, j))],
        out_shape=[sds((S, 3 * nh * HEAD_D), F32), sds((CONV_W, 3 * nh * HEAD_D), F32)],
        compiler_params=_params(("parallel",)), name="gdn_prep_bwd")(proj, conv_w, dqkv)


def _softplus(x):
    return jnp.maximum(x, 0.0) + jnp.log(1.0 + jnp.exp(-jnp.abs(x)))


def _gates_fwd(ab, al, dt, nh):
    S = ab.shape[0]

    def body(ab_ref, al_ref, dt_ref, o_ref):
        v = ab_ref[...]
        lane = lax.broadcasted_iota(jnp.int32, v.shape, 1)
        g = -jnp.exp(al_ref[...]) * _softplus(v + dt_ref[...])
        o_ref[...] = jnp.where(lane < nh, g, jnp.where(lane < 2 * nh, _sigmoid(v), 0.0))

    row = pl.BlockSpec((1, LANE), lambda i: (0, 0))
    full = pl.BlockSpec((S, LANE), lambda i: (0, 0))
    return pl.pallas_call(body, grid=(1,), in_specs=[full, row, row], out_specs=full,
                          out_shape=jax.ShapeDtypeStruct((S, LANE), F32),
                          compiler_params=_params(("arbitrary",)), name="gates_fwd")(ab, al, dt)


def _gates_bwd(ab, al, dt, dgb, nh):
    S = ab.shape[0]
    sds = jax.ShapeDtypeStruct

    def body(ab_ref, al_ref, dt_ref, d_ref, dab_ref, dal_ref, ddt_ref):
        v, d = ab_ref[...], d_ref[...]
        lane = lax.broadcasted_iota(jnp.int32, v.shape, 1)
        is_a = lane < nh
        z = v + dt_ref[...]
        dsp = jnp.where(is_a, d * (-jnp.exp(al_ref[...])), 0.0)
        dz = dsp * _sigmoid(z)
        beta = _sigmoid(v)
        dab_ref[...] = jnp.where(is_a, dz, jnp.where(lane < 2 * nh, d * beta * (1.0 - beta), 0.0))
        dal_ref[...] = jnp.sum(dsp * _softplus(z), axis=0, keepdims=True)
        ddt_ref[...] = jnp.sum(dz, axis=0, keepdims=True)

    row = pl.BlockSpec((1, LANE), lambda i: (0, 0))
    full = pl.BlockSpec((S, LANE), lambda i: (0, 0))
    return pl.pallas_call(body, grid=(1,), in_specs=[full, row, row, full], out_specs=[full, row, row],
                          out_shape=[sds((S, LANE), F32), sds((1, LANE), F32), sds((1, LANE), F32)],
                          compiler_params=_params(("arbitrary",)), name="gates_bwd")(ab, al, dt, dgb)


def _col_of(tile, h):
    lane = lax.broadcasted_iota(jnp.int32, tile.shape, 1)
    return jnp.sum(jnp.where(lane == h, tile, 0.0), axis=1, keepdims=True)


def _to_row(col, eye):
    return jnp.sum(jnp.where(eye, col, 0.0), axis=0, keepdims=True)


def _to_col(row, eye):
    return jnp.sum(jnp.where(eye, row, 0.0), axis=1, keepdims=True)


def _split(a):
    hi = a.astype(BF16)
    return hi, (a - hi.astype(F32)).astype(BF16)


def _gdot(a, b, dims):
    ah, al = _split(a)
    bh, bl = _split(b)
    return _dot(ah, bh, dims) + (_dot(ah, bl, dims) + _dot(al, bh, dims))


def _bdot(a, b, dims):
    return _dot(a.astype(BF16), b.astype(BF16), dims)


def _chunks_local(qs, ks, vs, gcols, bcols, Ts=None):
    C = CHUNK
    row = lax.broadcasted_iota(jnp.int32, (C, C), 0)
    col = lax.broadcasted_iota(jnp.int32, (C, C), 1)
    tril, strict, eye = col <= row, col < row, col == row
    outs = []
    for k, gcol, bcol in zip(ks, gcols, bcols):
        grow = _to_row(gcol, eye)
        G_row = jnp.sum(jnp.where(row <= col, gcol, 0.0), axis=0, keepdims=True)
        G_col = jnp.sum(jnp.where(tril, grow, 0.0), axis=1, keepdims=True)
        G_last = G_col[C - 1:C, :]
        outs.append(dict(strict=strict, eye=eye, row=row, col=col, decay=jnp.exp(jnp.where(tril, G_col - G_row, NEG_INF)),
                         eG=jnp.exp(G_col), eGr=jnp.exp(G_last - G_col), gl=jnp.exp(G_last), kb=k * bcol))
    Ms = [_gdot(o["kb"], k, NT) for o, k in zip(outs, ks)]
    Ns = [_gdot(q, k, NT) for q, k in zip(qs, ks)]
    for o, q, k, M, N in zip(outs, qs, ks, Ms, Ns):
        o.update(A=jnp.where(strict, M * o["decay"], 0.0), attn=N * o["decay"], rhs_k=o["kb"] * o["eG"],
                 q_dec=q * o["eG"], k_dec=k * o["eGr"])
    if Ts is None:
        Ts = [jnp.where(eye, 1.0, 0.0) - o["A"] for o in outs]
        Ps = [o["A"] for o in outs]
        for _ in range(int(math.log2(C)) - 1):
            Ps = [_bdot(P, P, NN) for P in Ps]
            Ts = [T + _bdot(T, P, NN) for T, P in zip(Ts, Ps)]
        us = [_bdot(T, v * bcol, NN) for T, v, bcol in zip(Ts, vs, bcols)]
        ws = [_bdot(T, o["rhs_k"], NN) for T, o in zip(Ts, outs)]
        for o, T, u, w in zip(outs, Ts, us, ws):
            o.update(T=T, u=u, w=w)
    return outs


GDN_ROWS = 256
GDN_LOCAL_ROWS = 512
WQK = 3 * CHUNK


def _gdn_local_fwd(qkv, gb, nh):
    S = qkv.shape[0]
    nc = S // CHUNK
    rb = min(GDN_LOCAL_ROWS, S)
    cpb = rb // CHUNK
    sds = jax.ShapeDtypeStruct

    def body(q_ref, k_ref, v_ref, gb_ref, u_ref, wqk_ref, attn_ref, t_ref):
        h = pl.program_id(0)
        rows_of = [slice(ci * CHUNK, (ci + 1) * CHUNK) for ci in range(cpb)]
        gbts = [gb_ref[rows, :] for rows in rows_of]
        Ls = _chunks_local([q_ref[rows, :] for rows in rows_of], [k_ref[rows, :] for rows in rows_of],
                           [v_ref[rows, :] for rows in rows_of], [_col_of(t, h) for t in gbts],
                           [_col_of(t, nh + h) for t in gbts])
        for ci, (rows, L) in enumerate(zip(rows_of, Ls)):
            u_ref[rows, :] = L["u"]
            base = ci * WQK
            wqk_ref[base:base + CHUNK, :] = L["w"]
            wqk_ref[base + CHUNK:base + 2 * CHUNK, :] = L["q_dec"]
            wqk_ref[base + 2 * CHUNK:base + WQK, :] = L["k_dec"]
            attn_ref[ci] = L["attn"]
            t_ref[ci] = L["T"]

    cc = pl.BlockSpec((None, cpb, CHUNK, CHUNK), lambda h, i: (h, i, 0, 0))
    return pl.pallas_call(
        body, grid=(nh, S // rb),
        in_specs=[pl.BlockSpec((rb, HEAD_D), lambda h, i: (i, 3 * h)), pl.BlockSpec((rb, HEAD_D), lambda h, i: (i, 3 * h + 1)),
                  pl.BlockSpec((rb, HEAD_D), lambda h, i: (i, 3 * h + 2)), pl.BlockSpec((rb, LANE), lambda h, i: (i, 0))],
        out_specs=[pl.BlockSpec((rb, HEAD_D), lambda h, i: (i, h)),
                   pl.BlockSpec((None, 3 * rb, HEAD_D), lambda h, i: (h, i, 0)), cc, cc],
        out_shape=[sds((S, nh * HEAD_D), F32), sds((nh, 3 * S, HEAD_D), F32), sds((nh, nc, CHUNK, CHUNK), F32),
                   sds((nh, nc, CHUNK, CHUNK), F32)],
        compiler_params=_params(("parallel", "parallel")), name="gdn_local_fwd")(qkv, qkv, qkv, gb)


def _gdn_scan_fwd(u, wqk, attn, gb, nh, dep):
    S = u.shape[0]
    nc = S // CHUNK
    rb = min(GDN_ROWS, S)
    cpb = rb // CHUNK
    sds = jax.ShapeDtypeStruct

    def body(u_ref, wqk_ref, attn_ref, gb_ref, dep_ref, o_ref, vn_ref, st_ref, s_ref):
        @pl.when(pl.program_id(0) == 0)
        def _():
            s_ref[...] = jnp.zeros_like(s_ref)

        for ci in range(cpb):
            rows = slice(ci * CHUNK, (ci + 1) * CHUNK)
            glv = jnp.exp(jnp.sum(gb_ref[rows, :], axis=0, keepdims=True))
            base = ci * WQK
            heads = range(nh)
            cols = [slice(h * HEAD_D, (h + 1) * HEAD_D) for h in heads]
            states = [s_ref[h] for h in heads]
            rs = [_gdot(wqk_ref[h, base:base + 2 * CHUNK, :], states[h], NN) for h in heads]
            vbs = [u_ref[rows, cols[h]] - rs[h][:CHUNK] for h in heads]
            os_ = [_gdot(attn_ref[h, ci], vbs[h], NN) for h in heads]
            ks_ = [_gdot(wqk_ref[h, base + 2 * CHUNK:base + WQK, :], vbs[h], TN) for h in heads]
            for h in heads:
                st_ref[h, ci] = states[h]
                o_ref[rows, cols[h]] = rs[h][CHUNK:] + os_[h]
                vn_ref[rows, cols[h]] = vbs[h]
                s_ref[h] = states[h] * glv[:, h:h + 1] + ks_[h]

    return pl.pallas_call(
        body, grid=(S // rb,),
        in_specs=[pl.BlockSpec((rb, nh * HEAD_D), lambda i: (i, 0)), pl.BlockSpec((nh, 3 * rb, HEAD_D), lambda i: (0, i, 0)),
                  pl.BlockSpec((nh, cpb, CHUNK, CHUNK), lambda i: (0, i, 0, 0)), pl.BlockSpec((rb, LANE), lambda i: (i, 0)),
                  pl.BlockSpec(memory_space=pl.ANY)],
        out_specs=[pl.BlockSpec((rb, nh * HEAD_D), lambda i: (i, 0)), pl.BlockSpec((rb, nh * HEAD_D), lambda i: (i, 0)),
                   pl.BlockSpec((nh, cpb, HEAD_D, HEAD_D), lambda i: (0, i, 0, 0))],
        out_shape=[sds((S, nh * HEAD_D), F32), sds((S, nh * HEAD_D), F32), sds((nh, nc, HEAD_D, HEAD_D), F32)],
        scratch_shapes=[pltpu.VMEM((nh, HEAD_D, HEAD_D), F32)],
        compiler_params=_params(("arbitrary",)), name="gdn_scan_fwd")(u, wqk, attn, gb, dep)


def _gdn_scan_bwd(wqk, attn, gb, states, vn, do, nh, dep):
    S = vn.shape[0]
    nc = S // CHUNK
    rb = min(GDN_ROWS, S)
    cpb = rb // CHUNK
    last = S // rb - 1
    sds = jax.ShapeDtypeStruct

    def body(wqk_ref, attn_ref, gb_ref, st_ref, vn_ref, do_ref, dep_ref, dvn_ref, dw_ref, dqd_ref, dkd_ref, da_ref, dgl_ref,
             ds_ref):
        @pl.when(pl.program_id(0) == 0)
        def _():
            ds_ref[...] = jnp.zeros_like(ds_ref)

        row = lax.broadcasted_iota(jnp.int32, (CHUNK, CHUNK), 0)
        col = lax.broadcasted_iota(jnp.int32, (CHUNK, CHUNK), 1)
        for ci in reversed(range(cpb)):
            rows = slice(ci * CHUNK, (ci + 1) * CHUNK)
            glv = jnp.exp(jnp.sum(gb_ref[rows, :], axis=0, keepdims=True))
            base = ci * WQK
            heads = range(nh)
            cols = [slice(h * HEAD_D, (h + 1) * HEAD_D) for h in heads]
            states = [st_ref[h, ci] for h in heads]
            dSs = [ds_ref[h] for h in heads]
            vbs = [vn_ref[rows, cols[h]] for h in heads]
            dobs = [do_ref[rows, cols[h]] for h in heads]
            dv1 = [_bdot(attn_ref[h, ci], dobs[h], TN) for h in heads]
            dv2 = [_bdot(wqk_ref[h, base + 2 * CHUNK:base + WQK, :], dSs[h], NN) for h in heads]
            das = [_bdot(dobs[h], vbs[h], NT) for h in heads]
            dkds = [_bdot(vbs[h], dSs[h], NT) for h in heads]
            dvbs = [dv1[h] + dv2[h] for h in heads]
            xs = [_bdot(jnp.concatenate([dobs[h], dvbs[h]], axis=0), states[h], NT) for h in heads]
            dss = [_bdot(wqk_ref[h, base:base + 2 * CHUNK, :], jnp.concatenate([-dvbs[h], dobs[h]], axis=0), TN)
                   for h in heads]
            for h in heads:
                dqd_ref[rows, cols[h]] = xs[h][:CHUNK]
                dw_ref[rows, cols[h]] = -xs[h][CHUNK:]
                dvn_ref[rows, cols[h]] = dvbs[h]
                da_ref[h, ci] = jnp.where(col <= row, das[h], 0.0)
                dkd_ref[rows, cols[h]] = dkds[h]
                gl = glv[:, h:h + 1]
                dgl = jnp.sum(jnp.sum(states[h] * dSs[h], axis=0, keepdims=True), axis=1, keepdims=True)
                dgl_ref[h, ci] = jnp.broadcast_to(dgl * gl, (1, LANE))
                ds_ref[h] = dSs[h] * gl + dss[h]

    rv = lambda i: last - i
    wide = pl.BlockSpec((rb, nh * HEAD_D), lambda i: (rv(i), 0))
    return pl.pallas_call(
        body, grid=(S // rb,),
        in_specs=[pl.BlockSpec((nh, 3 * rb, HEAD_D), lambda i: (0, rv(i), 0)),
                  pl.BlockSpec((nh, cpb, CHUNK, CHUNK), lambda i: (0, rv(i), 0, 0)),
                  pl.BlockSpec((rb, LANE), lambda i: (rv(i), 0)),
                  pl.BlockSpec((nh, cpb, HEAD_D, HEAD_D), lambda i: (0, rv(i), 0, 0)), wide, wide,
                  pl.BlockSpec(memory_space=pl.ANY)],
        out_specs=[wide, wide, wide, wide, pl.BlockSpec((nh, cpb, CHUNK, CHUNK), lambda i: (0, rv(i), 0, 0)),
                   pl.BlockSpec((nh, cpb, 1, LANE), lambda i: (0, rv(i), 0, 0))],
        out_shape=[sds((S, nh * HEAD_D), F32), sds((S, nh * HEAD_D), F32), sds((S, nh * HEAD_D), F32),
                   sds((S, nh * HEAD_D), F32), sds((nh, nc, CHUNK, CHUNK), F32), sds((nh, nc, 1, LANE), F32)],
        scratch_shapes=[pltpu.VMEM((nh, HEAD_D, HEAD_D), F32)],
        compiler_params=_params(("arbitrary",)), name="gdn_scan_bwd")(wqk, attn, gb, states, vn, do, dep)


def _gdn_local_bwd(qkv, gb, T, u, wqk, dvn, dw, dqd, dkd, dattn, dgl, nh):
    S = qkv.shape[0]
    rb = min(GDN_LOCAL_ROWS, S)
    cpb = rb // CHUNK
    sds = jax.ShapeDtypeStruct

    def body(q_ref, k_ref, v_ref, gb_ref, t_ref, u_ref, wqk_ref, dvn_ref, dw_ref, dqd_ref, dkd_ref, da_ref, dgl_ref,
             dqkv_ref, dg_ref, db_ref):
        h = pl.program_id(0)
        n = range(cpb)
        rows_of = [slice(ci * CHUNK, (ci + 1) * CHUNK) for ci in n]
        qs, ks, vs = ([r[rows, :] for rows in rows_of] for r in (q_ref, k_ref, v_ref))
        gbts = [gb_ref[rows, :] for rows in rows_of]
        bcols = [_col_of(t, nh + h) for t in gbts]
        Ts = [t_ref[ci] for ci in n]
        Ls = _chunks_local(qs, ks, vs, [_col_of(t, h) for t in gbts], bcols, Ts=Ts)
        drvs = [_bdot(Ts[ci], dvn_ref[rows_of[ci], :], TN) for ci in n]
        drks = [_bdot(Ts[ci], dw_ref[rows_of[ci], :], TN) for ci in n]
        dAs = [jnp.where(Ls[ci]["strict"], -(_bdot(drvs[ci], u_ref[rows_of[ci], :], NT)
                                             + _bdot(drks[ci], wqk_ref[ci * WQK:ci * WQK + CHUNK, :], NT)), 0.0) for ci in n]
        dMs = [dAs[ci] * Ls[ci]["decay"] for ci in n]
        dNs = [da_ref[ci] * Ls[ci]["decay"] for ci in n]
        dkbs = [_bdot(dMs[ci], ks[ci], NN) for ci in n]
        dq1 = [_bdot(dNs[ci], ks[ci], NN) for ci in n]
        dk1 = [_bdot(dMs[ci], Ls[ci]["kb"], TN) for ci in n]
        dk2 = [_bdot(dNs[ci], qs[ci], TN) for ci in n]
        for ci in n:
            rows, L, q, k, v, bcol = rows_of[ci], Ls[ci], qs[ci], ks[ci], vs[ci], bcols[ci]
            eye, eG, eGr = L["eye"], L["eG"], L["eGr"]
            drv, drk, dkb = drvs[ci], drks[ci], dkbs[ci]
            dq_dec, dk_dec, dattn_c = dqd_ref[rows, :], dkd_ref[rows, :], da_ref[ci]
            dqkv_ref[rows, :HEAD_D] = dq1[ci] + dq_dec * eG
            dqkv_ref[rows, HEAD_D:2 * HEAD_D] = drk * (bcol * eG) + dk1[ci] + dkb * bcol + dk2[ci] + dk_dec * eGr
            dqkv_ref[rows, 2 * HEAD_D:] = drv * bcol
            db_ref[rows, :] = (jnp.sum(drv * v, axis=1, keepdims=True) + jnp.sum(drk * k, axis=1, keepdims=True) * eG
                               + jnp.sum(dkb * k, axis=1, keepdims=True))
            E = dAs[ci] * L["A"] + dattn_c * L["attn"]
            kd = jnp.sum(dk_dec * L["k_dec"], axis=1, keepdims=True)
            dG = (jnp.sum(dq_dec * L["q_dec"], axis=1, keepdims=True) - kd
                  + jnp.sum(drk * L["rhs_k"], axis=1, keepdims=True)
                  + jnp.sum(E, axis=1, keepdims=True) - _to_col(jnp.sum(E, axis=0, keepdims=True), eye))
            d_last = jnp.sum(kd, axis=0, keepdims=True) + dgl_ref[ci][:, :1]
            dG = dG + jnp.where(L["row"][:, :1] == CHUNK - 1, d_last, 0.0)
            dg_ref[rows, :] = jnp.sum(jnp.where(L["col"] >= L["row"], _to_row(dG, eye), 0.0), axis=1, keepdims=True)

    hd = pl.BlockSpec((rb, HEAD_D), lambda h, i: (i, h))
    cc = pl.BlockSpec((None, cpb, CHUNK, CHUNK), lambda h, i: (h, i, 0, 0))
    col1 = pl.BlockSpec((None, rb, 1), lambda h, i: (h, i, 0))
    return pl.pallas_call(
        body, grid=(nh, S // rb),
        in_specs=[pl.BlockSpec((rb, HEAD_D), lambda h, i: (i, 3 * h)), pl.BlockSpec((rb, HEAD_D), lambda h, i: (i, 3 * h + 1)),
                  pl.BlockSpec((rb, HEAD_D), lambda h, i: (i, 3 * h + 2)), pl.BlockSpec((rb, LANE), lambda h, i: (i, 0)),
                  cc, hd, pl.BlockSpec((None, 3 * rb, HEAD_D), lambda h, i: (h, i, 0)), hd, hd, hd, hd, cc,
                  pl.BlockSpec((None, cpb, 1, LANE), lambda h, i: (h, i, 0, 0))],
        out_specs=[pl.BlockSpec((rb, 3 * HEAD_D), lambda h, i: (i, h)), col1, col1],
        out_shape=[sds((S, 3 * nh * HEAD_D), F32)] + [sds((nh, S, 1), F32)] * 2,
        compiler_params=_params(("parallel", "parallel")), name="gdn_local_bwd")(
            qkv, qkv, qkv, gb, T, u, wqk, dvn, dw, dqd, dkd, dattn, dgl)


def _gated_norm_fwd(o, proj, norm_w, nh, z_blk0, mix, m_blk0):
    S = o.shape[0]

    def body(o_ref, z_ref, w_ref, mix_ref, y_ref):
        ov, z = o_ref[...], z_ref[...]
        r = lax.rsqrt(jnp.mean(ov * ov, axis=1, keepdims=True) + RMS_EPS)
        y_ref[...] = ov * r * w_ref[...] * (z * _sigmoid(z))

    return pl.pallas_call(
        body, grid=(nh,),
        in_specs=[pl.BlockSpec((S, HEAD_D), lambda h: (0, h)), pl.BlockSpec((S, HEAD_D), lambda h: (0, z_blk0 + h)),
                  pl.BlockSpec((1, HEAD_D), lambda h: (0, 0)), pl.BlockSpec(memory_space=pl.ANY)],
        out_specs=pl.BlockSpec((S, HEAD_D), lambda h: (0, m_blk0 + h)),
        out_shape=jax.ShapeDtypeStruct(mix.shape, F32), input_output_aliases={3: 0},
        compiler_params=_params(("parallel",)), name="gated_norm_fwd")(o, proj, norm_w, mix)


def _gated_norm_bwd(o, proj, norm_w, dmix, nh, z_blk0, d_blk0):
    S = o.shape[0]
    sds = jax.ShapeDtypeStruct

    def body(o_ref, z_ref, w_ref, dy_ref, do_ref, dz_ref, dw_ref):
        ov, z, w, dy = o_ref[...], z_ref[...], w_ref[...], dy_ref[...]
        r = lax.rsqrt(jnp.mean(ov * ov, axis=1, keepdims=True) + RMS_EPS)
        oh = ov * r
        sg = _sigmoid(z)
        dz_ref[...] = dy * (oh * w) * (sg * (1.0 + z * (1.0 - sg)))
        don = dy * (z * sg)
        @pl.when(pl.program_id(0) == 0)
        def _():
            dw_ref[...] = jnp.zeros_like(dw_ref)

        dw_ref[...] += jnp.sum(don * oh, axis=0, keepdims=True)
        doh = don * w
        do_ref[...] = r * (doh - oh * jnp.mean(doh * oh, axis=1, keepdims=True))

    return pl.pallas_call(
        body, grid=(nh,),
        in_specs=[pl.BlockSpec((S, HEAD_D), lambda h: (0, h)), pl.BlockSpec((S, HEAD_D), lambda h: (0, z_blk0 + h)),
                  pl.BlockSpec((1, HEAD_D), lambda h: (0, 0)), pl.BlockSpec((S, HEAD_D), lambda h: (0, d_blk0 + h))],
        out_specs=[pl.BlockSpec((S, HEAD_D), lambda h: (0, h)), pl.BlockSpec((S, HEAD_D), lambda h: (0, h)),
                   pl.BlockSpec((1, HEAD_D), lambda h: (0, 0))],
        out_shape=[sds((S, nh * HEAD_D), F32), sds((S, nh * HEAD_D), F32), sds((1, HEAD_D), F32)],
        compiler_params=_params(("arbitrary",)), name="gated_norm_bwd")(o, proj, norm_w, dmix)


def _adamw_math(w, g, m, v):
    m = ADAM_B1 * m + (1.0 - ADAM_B1) * g
    v = ADAM_B2 * v + (1.0 - ADAM_B2) * (g * g)
    m_hat = m / (1.0 - ADAM_B1 ** ADAM_STEP)
    v_hat = v / (1.0 - ADAM_B2 ** ADAM_STEP)
    delta = -ADAM_LR * (m_hat / (jnp.sqrt(v_hat) + ADAM_EPS) + ADAM_WD * w)
    return delta, m, v


def _slab_tiles(R, C, rows=256, cols=256):
    if R % rows == 0:
        return (rows, C), R // rows, lambda i: (i, 0)
    tc = _tile(C, cols)
    return (R, tc), C // tc, lambda i: (0, i)


def _adamw_big(parts, terms, chip, w, m, v, name):
    R, C = w.shape
    blk, steps, at = _slab_tiles(R, C)
    sds = jax.ShapeDtypeStruct

    def body(q_ref, p_ref, t_ref, w_ref, m_ref, v_ref, g_ref, d_ref, nm_ref, nv_ref):
        g = ((p_ref[...].astype(F32) + t_ref[0].astype(F32)) + t_ref[1].astype(F32)) + t_ref[2].astype(F32)
        g_ref[...] = g
        d_ref[...], nm_ref[...], nv_ref[...] = _adamw_math(w_ref[...], g, m_ref[...], v_ref[...])

    spec = pl.BlockSpec(blk, lambda i, q_ref: at(i))
    grid_spec = pltpu.PrefetchScalarGridSpec(
        num_scalar_prefetch=1, grid=(steps,),
        in_specs=[pl.BlockSpec((None,) + blk, lambda i, q_ref: (q_ref[0],) + at(i)),
                  pl.BlockSpec((3,) + blk, lambda i, q_ref: (0,) + at(i)), spec, spec, spec],
        out_specs=[spec] * 4)
    return pl.pallas_call(body, grid_spec=grid_spec, out_shape=[sds((R, C), F32)] * 4,
                          compiler_params=_params(("parallel",)), name=name)(chip, parts, terms, w, m, v)


def _adamw_small(ws, gs, ms, vs):
    n = len(ws)

    def body(*refs):
        for i in range(n):
            w, g, m, v = (refs[k * n + i][...] for k in range(4))
            d, nm, nv = _adamw_math(w, g, m, v)
            refs[4 * n + i][...] = d
            refs[5 * n + i][...] = nm
            refs[6 * n + i][...] = nv

    shapes = [jax.ShapeDtypeStruct(w.shape, F32) for w in ws]
    vm = pl.BlockSpec(memory_space=pltpu.VMEM)
    outs = pl.pallas_call(body, in_specs=[vm] * (4 * n), out_specs=[vm] * (3 * n), out_shape=shapes * 3,
                          name="adamw_small")(*ws, *gs, *ms, *vs)
    return outs[:n], outs[n:2 * n], outs[2 * n:]


MESH = pl.DeviceIdType.MESH
ANY = pl.BlockSpec(memory_space=pl.ANY)


def _place():
    x, y, c = lax.axis_index("x"), lax.axis_index("y"), lax.axis_index("c")
    return x, y, c, [(1 - x, y), (x, 1 - y), (1 - x, 1 - y)]


def _chip_sum(grad, recv, core, name):
    _, R, C = grad.shape
    blk, steps, at = _slab_tiles(R, C, rows=512 if R % 512 == 0 else 256, cols=512)

    def body(c_ref, g_ref, r_ref, o_ref):
        o_ref[...] = (g_ref[...].astype(F32) + r_ref[...].astype(F32)).astype(o_ref.dtype)

    grid_spec = pltpu.PrefetchScalarGridSpec(
        num_scalar_prefetch=1, grid=(4, steps),
        in_specs=[pl.BlockSpec((None,) + blk, lambda q, i, c_ref: (2 * q + c_ref[0],) + at(i)),
                  pl.BlockSpec((None,) + blk, lambda q, i, c_ref: (q,) + at(i))],
        out_specs=pl.BlockSpec((None,) + blk, lambda q, i, c_ref: (q,) + at(i)))
    return pl.pallas_call(body, grid_spec=grid_spec, out_shape=jax.ShapeDtypeStruct((4, R, C), BF16),
                          compiler_params=_params(("parallel", "parallel")), name=name)(core, grad, recv)


HBM_SPEC = pl.BlockSpec(memory_space=pltpu.HBM)
SEM_SPEC = pl.BlockSpec(memory_space=pltpu.SEMAPHORE)
DATAFLOW = pltpu.SideEffectType.DATAFLOW_SIDE_EFFECTING


def _split_start(name, bufs, plan, counts, after=None):
    nb, ng = len(bufs), len(counts)
    extra = [] if after is None else [after]
    place = [(g, k) for g, cnt in enumerate(counts) for k in range(cnt)]

    def body(*refs):
        sems, token = refs[nb + len(extra):nb + len(extra) + 2 * ng], refs[-1]
        for (g, k), (src, dst, to) in zip(place, plan(refs[:nb])):
            pltpu.make_async_remote_copy(src_ref=src, dst_ref=dst, send_sem=sems[2 * g].at[k], recv_sem=sems[2 * g + 1].at[k],
                                         device_id=to, device_id_type=MESH).start()
        token[...] = jnp.zeros_like(token)

    outs = pl.pallas_call(
        body, name=name,
        out_shape=(*[pltpu.SemaphoreType.DMA((cnt,)) for cnt in counts for _ in range(2)],
                   *[pltpu.HBM(b.shape, b.dtype) for b in bufs], jax.ShapeDtypeStruct((8, LANE), F32)),
        in_specs=[HBM_SPEC] * nb + [ANY] * len(extra),
        out_specs=(*[SEM_SPEC] * (2 * ng), *[HBM_SPEC] * nb, pl.BlockSpec(memory_space=pltpu.VMEM)),
        input_output_aliases={i: 2 * ng + i for i in range(nb)},
        compiler_params=pltpu.CompilerParams(has_side_effects=DATAFLOW))(
            *[pltpu.with_memory_space_constraint(b, pltpu.HBM) for b in bufs], *extra)
    return [(outs[2 * g], outs[2 * g + 1]) for g in range(ng)], list(outs[2 * ng:2 * ng + nb]), outs[-1]


def _split_wait(name, sems, bufs, plan, after):
    nb = len(bufs)
    send_sems, recv_sems = sems

    def body(*refs):
        send_s, recv_s = refs[nb], refs[nb + 1]
        for k, (src, dst, to) in enumerate(plan(refs[:nb])):
            cp = pltpu.make_async_remote_copy(src_ref=src, dst_ref=dst, send_sem=send_s.at[k], recv_sem=recv_s.at[k],
                                              device_id=to, device_id_type=MESH)
            cp.wait_send()
            cp.wait_recv()

    outs = pl.pallas_call(
        body, name=name, out_shape=tuple(pltpu.HBM(b.shape, b.dtype) for b in bufs),
        in_specs=[HBM_SPEC] * nb + [SEM_SPEC, SEM_SPEC, ANY], out_specs=tuple([HBM_SPEC] * nb),
        input_output_aliases={i: i for i in range(nb)},
        compiler_params=pltpu.CompilerParams(has_side_effects=DATAFLOW))(*bufs, send_sems, recv_sems, after)
    return list(outs)


def _slot(px, py, pc):
    return 4 * px + 2 * py + pc


class _Gather:
    def __init__(self, shards, groups, dev):
        self.shards, self.groups, self.dev = shards, groups, dev
        self.second = {}

    @staticmethod
    def _plan1(pairs, refs):
        x, y, c, chips = _place()
        out = []
        for s, l in pairs:
            dst = refs[l].at[_slot(x, y, c)]
            out.append((refs[s], dst, (x, y, 1 - c)))
            out += [(refs[s], dst, (px, py, c)) for px, py in chips]
        return out

    @staticmethod
    def _plan2(refs):
        x, y, c, chips = _place()
        return [(r.at[_slot(px, py, c)],) * 2 + ((x, y, 1 - c),) for r in refs for px, py in chips]

    def start(self):
        n = len(self.shards)
        lands = [lax.dynamic_update_slice(lax.empty((N_DEV,) + s.shape, s.dtype), s[None], (self.dev, 0, 0))
                 for s in self.shards]
        pairs = [(w, n + w) for g in self.groups for w in g]
        sems, bufs, token = _split_start("gather_start_1", list(self.shards) + lands, functools.partial(self._plan1, pairs),
                                         tuple(4 * len(g) for g in self.groups))
        self.first = [(sems[i], [bufs[w] for w in g], [bufs[n + w] for w in g]) for i, g in enumerate(self.groups)]
        return token

    def mid(self, gi, after):
        sems, srcs, lands = self.first[gi]
        m = len(srcs)
        plan = functools.partial(self._plan1, [(w, m + w) for w in range(m)])
        lands = _split_wait("gather_%d_wait_1" % gi, sems, srcs + lands, plan, after)[m:]
        sems, lands, token = _split_start("gather_%d_start_2" % gi, lands, self._plan2, (3 * m,))
        self.second[gi] = (sems[0], lands)
        return token

    def finish(self, gi, after):
        sems, lands = self.second[gi]
        return _split_wait("gather_%d_wait_2" % gi, sems, lands, self._plan2, after)


class _Exchanges:
    def __init__(self, tag, n, core, gather=None):
        self.tag, self.n, self.core, self.gather = tag, n, core, gather

    def weights_mid(self, group, after):
        return self.gather.mid(group, after)

    def weights_finish(self, group, after):
        return self.gather.finish(group, after)

    def _reduce_plan1(self, refs):
        n = self.n
        x, y, c, _ = _place()
        return [(refs[w].at[2 * q + (1 - c)], refs[n + w].at[q], (x, y, 1 - c)) for w in range(n) for q in range(4)]

    def _reduce_plan2(self, refs):
        n = self.n
        x, y, c, chips = _place()
        return [(refs[w].at[2 * px + py], refs[n + w].at[j], (px, py, c))
                for w in range(n) for j, (px, py) in enumerate(chips)]

    def grads_start(self, grads):
        lands = [lax.empty((4,) + g.shape[1:], g.dtype) for g in grads]
        self.g1 = _split_start(self.tag + "reduce_start_1", list(grads) + lands, self._reduce_plan1, (4 * self.n,))
        return self.g1[2]

    def grads_mid(self, after):
        n = self.n
        sems, bufs, _ = self.g1
        bufs = _split_wait(self.tag + "reduce_wait_1", sems[0], bufs, self._reduce_plan1, after)
        core = self.core.reshape(1).astype(jnp.int32)
        self.parts = [_chip_sum(bufs[w], bufs[n + w], core, self.tag + "reduce_chip_sum_%d" % w) for w in range(n)]
        lands = [lax.empty((3,) + p.shape[1:], p.dtype) for p in self.parts]
        self.g2 = _split_start(self.tag + "reduce_start_2", self.parts + lands, self._reduce_plan2, (3 * n,))
        return self.g2[2]

    def grads_finish(self, after):
        n = self.n
        sems, bufs, _ = self.g2
        bufs = _split_wait(self.tag + "reduce_wait_2", sems[0], bufs, self._reduce_plan2, after)
        self.parts, self.terms = bufs[:n], bufs[n:]


def _all_reduce_small(buf):
    R = buf.shape[0]

    def body(x_ref, o_ref, g_ref, send_sems, recv_sems):
        x, y, c, chips = _place()
        me, sibling = (x, y, c), (x, y, 1 - c)

        def slot(px, py, pc):
            return 4 * px + 2 * py + pc

        def copy(k, block, to, src=None):
            dst = g_ref.at[slot(*block)]
            return pltpu.make_async_remote_copy(src_ref=dst if src is None else src, dst_ref=dst,
                                                send_sem=send_sems.at[k], recv_sem=recv_sems.at[k],
                                                device_id=to, device_id_type=MESH)

        first = [copy(0, me, sibling, src=x_ref)]
        first += [copy(1 + j, me, (*chip, c), src=x_ref) for j, chip in enumerate(chips)]
        for cp in first:
            cp.start()
        g_ref[slot(*me)] = x_ref[...]
        passed = [copy(4 + j, (*chip, c), sibling) for j, chip in enumerate(chips)]
        for j, chip in enumerate(chips):
            copy(1 + j, (*chip, c), me).wait_recv()
            passed[j].start()
        copy(0, sibling, me).wait_recv()
        for j, chip in enumerate(chips):
            copy(4 + j, (*chip, 1 - c), me).wait_recv()
        for cp in first + passed:
            cp.wait_send()
        acc = g_ref[0]
        for s in range(1, N_DEV):
            acc = acc + g_ref[s]
        o_ref[...] = acc

    vm = pl.BlockSpec(memory_space=pltpu.VMEM)
    return pl.pallas_call(
        body, in_specs=[vm], out_specs=vm, out_shape=jax.ShapeDtypeStruct((R, LANE), F32),
        scratch_shapes=[pltpu.VMEM((N_DEV, R, LANE), F32), pltpu.SemaphoreType.DMA((7,)), pltpu.SemaphoreType.DMA((7,))],
        name="all_reduce_small")(buf)


def _pad_cols(a, width):
    return jnp.pad(a, ((0, 0), (0, width - a.shape[1])))


def _local_step(x, target, w_in_t, conv_w, a_log, dt_bias, delta_norm_w, sinks, rel_bias, ln1_g, ln1_b, ln2_g, ln2_b,
                ex, ex_in):
    S, D = x.shape
    aw = D // 2
    hq, hkv, nh = aw // HEAD_A, aw // HEAD_A // GQA, aw // HEAD_D
    kvw = hkv * HEAD_A
    c_q, c_k, c_v, c_d = 0, aw, aw + kvw, aw + 2 * kvw
    c_ab = c_d + 3 * aw
    c_z = c_ab + 2 * nh
    n_in = c_z + aw
    assert w_in_t.shape == (n_in, D), (w_in_t.shape, n_in)
    w_pt = jnp.concatenate([w_in_t[:c_ab], w_in_t[c_z:], jnp.pad(w_in_t[c_ab:c_z], ((0, LANE - 2 * nh), (0, 0)))], axis=0)
    p_z, p_ab = c_ab, c_ab + aw
    n_p = p_ab + LANE

    xb = x.astype(BF16)
    proj = _matmul(xb, w_pt, NT, name="proj", tn=1152)
    bias = _attn_bias(rel_bias.T)
    attn_out = _attn_fwd(proj, bias, sinks.reshape(-1), hq, 0, c_k // kvw, c_v // kvw, D)
    conv2 = conv_w.reshape(CONV_W, 3 * aw)
    qkv = _gdn_prep_fwd(proj, conv2, nh, c_d // HEAD_D)
    ab = proj[:, p_ab:]
    al, dt = _pad_cols(a_log, LANE), _pad_cols(dt_bias, LANE)
    gb = _gates_fwd(ab, al, dt, nh)
    u_d, wqk, attn_d, t_d = _gdn_local_fwd(qkv, gb, nh)
    o_d, vn, states = _gdn_scan_fwd(u_d, wqk, attn_d, gb, nh, ex.weights_mid(1, u_d))
    mix = _gated_norm_fwd(o_d, proj, delta_norm_w, nh, p_z // HEAD_D, attn_out, aw // HEAD_D)
    w_o_g, w_up_g = ex.weights_finish(1, mix)
    w_o = w_o_g.reshape(D, D)
    mixed = _matmul(mix, w_o, NN, name="out_proj")
    h1 = _ln1_fwd(x, mixed, ln1_g, ln1_b)
    u = _matmul(h1, w_up_g, NN, name="mlp_up", b_groups=True, out_dtype=BF16, deps=(ex.weights_mid(2, h1),))
    (w_down_g,) = ex.weights_finish(2, u)
    w_down = w_down_g.reshape(-1, D)
    mlp = _matmul(u, w_down, NN, name="mlp_down", a_fn=_relu_sq, tk=4096)
    dr2, loss_row, dln2_g, dln2_b = _ln2_loss(h1, mlp, ln2_g, ln2_b, target)

    du = _matmul(dr2, w_down, NT, name="d_mlp_act", epi=_relu_sq_grad, epi_in=(u,), out_dtype=BF16)
    dw_down = _matmul(u, dr2, TN, name="dw_down", a_fn=_relu_sq, out_dtype=BF16)
    dw_up = _matmul(h1, du, TN, name="dw_up", out_dtype=BF16, out_groups=N_DEV)
    dh_mlp = _matmul(du, w_up_g, NT, name="d_h1", b_groups=True, tk=4096)
    dr1, dln1_g, dln1_b = _ln1_bwd(x, mixed, ln1_g, dr2, dh_mlp)
    dw_o = _matmul(mix, dr1, TN, name="dw_o", out_dtype=BF16)
    tok = ex.grads_start([dw_o.reshape(N_DEV, -1, D), dw_up, dw_down.reshape(N_DEV, -1, D)])
    dmix = _matmul(dr1, w_o, NT, name="d_mix", deps=(tok,))
    dq_a, dk_a, dv_a, dbias, dsink = _attn_bwd(proj, bias, sinks.reshape(-1), mix, dmix, hq, 0, c_k // kvw, c_v // kvw)
    drel = _rel_bias_grad(dbias, hq)
    do_d, dz, dnw = _gated_norm_bwd(o_d, proj, delta_norm_w, dmix, nh, p_z // HEAD_D, aw // HEAD_D)
    dvn_s, dw_s, dqd, dkd, dattn_d, dgl = _gdn_scan_bwd(wqk, attn_d, gb, states, vn, do_d, nh, ex.grads_mid(dq_a))
    dqkv_n, dg, dbeta = _gdn_local_bwd(qkv, gb, t_d, u_d, wqk, dvn_s, dw_s, dqd, dkd, dattn_d, dgl, nh)
    dgb = _pad_cols(jnp.concatenate([dg.reshape(nh, S).T, dbeta.reshape(nh, S).T], axis=1), LANE)
    dab, da_log, ddt_bias = _gates_bwd(ab, al, dt, dgb, nh)
    dqkv_d, dconv = _gdn_prep_bwd(proj, conv2, dqkv_n, nh, c_d // HEAD_D)
    dproj = jnp.concatenate([dq_a, dk_a, dv_a, dqkv_d, dz, dab], axis=1)
    dw_pt = _matmul(dproj, xb, TN, name="dw_in", out_dtype=BF16, tm=1152)
    dw_in_t = jnp.concatenate([dw_pt[:p_z], dw_pt[p_ab:p_ab + 2 * nh], dw_pt[p_z:p_ab]], axis=0)
    tok = ex_in.grads_mid(ex_in.grads_start([dw_in_t.reshape(N_DEV, -1, D)]))
    grad_x = _matmul(dproj, w_pt, NN, name="d_x", tk=1920, deps=(tok,), epi=_residual_grad, epi_in=(dr1,))
    ex.grads_finish(grad_x)

    small = dict(conv_w=dconv, a_log=da_log[:, :nh], dt_bias=ddt_bias[:, :nh], delta_norm_w=dnw,
                 attn_sinks=dsink[:, 0].reshape(1, hq), rel_bias=drel[:, :, 0].T,
                 ln1_g=dln1_g, ln1_b=dln1_b, ln2_g=dln2_g, ln2_b=dln2_b)
    return loss_row, grad_x, small


SMALL_ORDER = ("conv_w", "a_log", "dt_bias", "delta_norm_w", "attn_sinks", "rel_bias", "ln1_g", "ln1_b", "ln2_g", "ln2_b")


def _pack_small(loss_row, small):
    parts = [loss_row.reshape(-1)]
    for k in SMALL_ORDER:
        flat = small[k].reshape(-1)
        parts.append(jnp.pad(flat, (0, (-flat.shape[0]) % LANE)))
    flat = jnp.concatenate(parts)
    flat = jnp.pad(flat, (0, (-flat.shape[0]) % (8 * LANE)))
    return flat.reshape(-1, LANE)


def _unpack_small(buf, small_shapes):
    flat = buf.reshape(-1)
    loss = flat[0]
    off = LANE
    out = {}
    for k in SMALL_ORDER:
        n = int(np.prod(small_shapes[k]))
        out[k] = flat[off:off + n].reshape(small_shapes[k])
        off += n + (-n) % LANE
    return loss, out


def kernel(x, w_in, conv_w, a_log, dt_bias, delta_norm_w, attn_sinks, rel_bias, w_o, ln1_g, ln1_b, w_up, w_down, ln2_g, ln2_b, loss_target, m_w_in, m_conv_w, m_a_log, m_dt_bias, m_delta_norm_w, m_attn_sinks, m_rel_bias, m_w_o, m_ln1_g, m_ln1_b, m_w_up, m_w_down, m_ln2_g, m_ln2_b, v_w_in, v_conv_w, v_a_log, v_dt_bias, v_delta_norm_w, v_attn_sinks, v_rel_bias, v_w_o, v_ln1_g, v_ln1_b, v_w_up, v_w_down, v_ln2_g, v_ln2_b):
    S, D = x.shape[1], x.shape[2]
    core = lax.axis_index("c")
    dev = 4 * lax.axis_index("x") + 2 * lax.axis_index("y") + core

    gather = _Gather([conv_w[0, :, 0, :], w_in[0].T.astype(BF16), w_o[0].astype(BF16), w_up[0].astype(BF16),
                      w_down[0].astype(BF16)], [[0, 1], [2, 3], [4]], dev)
    conv_g, w_in_g = gather.finish(0, gather.mid(0, gather.start()))
    w_in_t = w_in_g.reshape(-1, D)
    ex = _Exchanges("", 3, core, gather)
    ex_in = _Exchanges("in_", 1, core)

    cw_sh = conv_w.shape[3]
    conv_full = jnp.transpose(conv_g, (1, 0, 2)).reshape(CONV_W, N_DEV * cw_sh)

    loss_row, grad_x, small = _local_step(
        x[0], loss_target[0], w_in_t, conv_full, a_log, dt_bias, delta_norm_w, attn_sinks, rel_bias,
        ln1_g, ln1_b, ln2_g, ln2_b, ex, ex_in)

    chip_arr = (dev // 2).reshape(1).astype(jnp.int32)
    big = {}
    for i, (name, w, m, v) in enumerate((("w_o", w_o, m_w_o, v_w_o), ("w_up", w_up, m_w_up, v_w_up),
                                         ("w_down", w_down, m_w_down, v_w_down))):
        big[name] = [o[None] for o in _adamw_big(ex.parts[i], ex.terms[i], chip_arr, w[0], m[0], v[0], "adamw_" + name)]
    ex_in.grads_finish(big["w_down"][0])
    outs = _adamw_big(ex_in.parts[0], ex_in.terms[0], chip_arr, w_in[0].T, m_w_in[0].T, v_w_in[0].T, "adamw_w_in")
    big["w_in"] = [o.T[None] for o in outs]

    small_shapes = {k: v.shape for k, v in small.items()}
    loss, small = _unpack_small(_all_reduce_small(_pack_small(loss_row, small)), small_shapes)
    small["conv_w"] = lax.dynamic_slice(small["conv_w"], (0, dev * cw_sh), (CONV_W, cw_sh))
    small["rel_bias"] = small["rel_bias"].reshape(rel_bias.shape)
    p2 = dict(conv_w=(conv_w, m_conv_w, v_conv_w), a_log=(a_log, m_a_log, v_a_log), dt_bias=(dt_bias, m_dt_bias, v_dt_bias),
              delta_norm_w=(delta_norm_w, m_delta_norm_w, v_delta_norm_w), attn_sinks=(attn_sinks, m_attn_sinks, v_attn_sinks),
              rel_bias=(rel_bias, m_rel_bias, v_rel_bias), ln1_g=(ln1_g, m_ln1_g, v_ln1_g), ln1_b=(ln1_b, m_ln1_b, v_ln1_b),
              ln2_g=(ln2_g, m_ln2_g, v_ln2_g), ln2_b=(ln2_b, m_ln2_b, v_ln2_b))
    two_d = lambda a: a.reshape(-1, a.shape[-1])
    ws = [two_d(p2[k][0]) for k in SMALL_ORDER]
    gs = [two_d(small[k]) for k in SMALL_ORDER]
    ms = [two_d(p2[k][1]) for k in SMALL_ORDER]
    vs = [two_d(p2[k][2]) for k in SMALL_ORDER]
    ds, nms, nvs = _adamw_small(ws, gs, ms, vs)
    res = {}
    for i, k in enumerate(SMALL_ORDER):
        shp = p2[k][0].shape
        res[k] = [gs[i].reshape(shp), ds[i].reshape(shp), nms[i].reshape(shp), nvs[i].reshape(shp)]
    res.update(big)
    order = ("w_in", "conv_w", "a_log", "dt_bias", "delta_norm_w", "attn_sinks", "rel_bias", "w_o", "ln1_g", "ln1_b",
             "w_up", "w_down", "ln2_g", "ln2_b")
    return (loss, grad_x[None], *[res[k][0] for k in order], *[res[k][1] for k in order],
            *[res[k][2] for k in order], *[res[k][3] for k in order])
```

```python
import functools
import math

import numpy as np
import jax
import jax.numpy as jnp
from jax import lax
from jax.experimental import pallas as pl
from jax.experimental.pallas import tpu as pltpu

F32 = jnp.float32
BF16 = jnp.bfloat16
HIGHEST = lax.Precision.HIGHEST

N_DEV = 8
HEAD_A = 64
GQA = 4
BLK = 128
N_BUCKETS = 32
MAX_DISTANCE = 128
HEAD_D = 128
CONV_W = 4
CHUNK = 64
NEG_INF = -1e30
LN_EPS = 1e-5
RMS_EPS = 1e-6
DN_ALPHA = 2.0 ** 0.25
ADAM_LR, ADAM_B1, ADAM_B2, ADAM_EPS, ADAM_WD, ADAM_STEP = 0.001, 0.9, 0.999, 1e-08, 0.01, 10

LANE = 128
VMEM_LIMIT = 56 * 1024 * 1024

NN = ((1,), (0,))
NT = ((1,), (1,))
TN = ((0,), (0,))


def _dot(a, b, dims, prec=None):
    return lax.dot_general(a, b, (dims, ((), ())), precision=prec, preferred_element_type=F32)


def _tile(dim, pref):
    if dim <= pref:
        return dim
    t = (pref // LANE) * LANE
    while t > LANE and dim % t:
        t -= LANE
    assert dim % t == 0, (dim, pref)
    return t


def _params(sem):
    return pltpu.CompilerParams(dimension_semantics=sem, vmem_limit_bytes=VMEM_LIMIT)


def _matmul(a, b, dims, *, name, out_dtype=F32, tm=1024, tn=1024, tk=2048, a_fn=None, epi=None, epi_in=(),
            b_groups=None, out_groups=None, deps=()):
    (ca,), (cb,) = dims
    M, K = a.shape[1 - ca], a.shape[ca]
    if b_groups:
        G, R, C = b.shape
        bshape = (R, G * C)
    else:
        bshape = b.shape
    N = bshape[1 - cb]
    assert bshape[cb] == K, (a.shape, b.shape, dims)
    tm, tk = _tile(M, tm), _tile(K, tk)
    if b_groups:
        lim = C if cb == 0 else tn
        tn = _tile(N, min(tn, lim))
        if cb == 1 and tk < C:
            tk = _tile(K, min(tk, C))
        elif cb == 1:
            tk = C * max(1, tk // C)
    else:
        tn = _tile(N, tn)
    if out_groups:
        tn = _tile(N, min(tn, N // out_groups))
    nk = K // tk
    b_span = tk // C if (b_groups and cb == 1 and tk > C) else 1

    def body(*refs):
        a_ref, b_ref = refs[0], refs[1]
        e_refs = refs[2:2 + len(epi_in)]
        o_ref = refs[2 + len(epi_in) + len(deps)]
        acc_ref = refs[3 + len(epi_in) + len(deps)] if nk > 1 else None
        k = pl.program_id(2)
        av = a_ref[...]
        if a_fn is not None:
            av = a_fn(av)
        if b_span > 1:
            prod = sum(_dot(av[:, g * C:(g + 1) * C].astype(BF16), b_ref[g].astype(BF16), dims) for g in range(b_span))
        else:
            prod = _dot(av.astype(BF16), b_ref[...].astype(BF16), dims)

        def finish(r):
            if epi is not None:
                r = epi(r, *[e[...] for e in e_refs])
            o_ref[...] = r.astype(out_dtype)

        if nk == 1:
            finish(prod)
            return

        @pl.when(k == 0)
        def _():
            acc_ref[...] = prod

        @pl.when(k > 0)
        def _():
            acc_ref[...] += prod

        @pl.when(k == nk - 1)
        def _():
            finish(acc_ref[...])

    a_spec = (pl.BlockSpec((tm, tk), lambda i, j, k: (i, k)) if ca == 1
              else pl.BlockSpec((tk, tm), lambda i, j, k: (k, i)))
    if b_groups:
        if cb == 0:
            per = C // tn
            b_spec = pl.BlockSpec((None, tk, tn), lambda i, j, k: (j // per, k, j % per))
        elif b_span > 1:
            b_spec = pl.BlockSpec((b_span, tn, C), lambda i, j, k: (k, j, 0))
        else:
            per = C // tk
            b_spec = pl.BlockSpec((None, tn, tk), lambda i, j, k: (k // per, j, k % per))
    else:
        b_spec = (pl.BlockSpec((tk, tn), lambda i, j, k: (k, j)) if cb == 0
                  else pl.BlockSpec((tn, tk), lambda i, j, k: (j, k)))
    e_specs = [pl.BlockSpec((tm, tn), lambda i, j, k: (i, j)) for _ in epi_in]
    if out_groups:
        per_o = (N // out_groups) // tn
        o_spec = pl.BlockSpec((None, tm, tn), lambda i, j, k: (j // per_o, i, j % per_o))
        o_shape = jax.ShapeDtypeStruct((out_groups, M, N // out_groups), out_dtype)
    else:
        o_spec = pl.BlockSpec((tm, tn), lambda i, j, k: (i, j))
        o_shape = jax.ShapeDtypeStruct((M, N), out_dtype)
    return pl.pallas_call(
        body, grid=(M // tm, N // tn, nk), out_specs=o_spec,
        in_specs=[a_spec, b_spec] + e_specs + [pl.BlockSpec(memory_space=pl.ANY)] * len(deps),
        out_shape=o_shape, scratch_shapes=[pltpu.VMEM((tm, tn), F32)] if nk > 1 else [],
        compiler_params=_params(("parallel", "parallel", "arbitrary")), name=name)(a, b, *epi_in, *deps)


def _relu_sq(u):
    r = jnp.maximum(u, 0.0)
    return r * r


def _relu_sq_grad(acc, u):
    return acc * (2.0 * jnp.maximum(u, 0.0))


def _ln_stats(r):
    mu = jnp.mean(r, axis=-1, keepdims=True)
    xc = r - mu
    var = jnp.mean(xc * xc, axis=-1, keepdims=True)
    rstd = lax.rsqrt(var + LN_EPS)
    return xc * rstd, rstd


def _ln_bwd(dy, xhat, rstd, g):
    dxh = dy * g
    m1 = jnp.mean(dxh, axis=-1, keepdims=True)
    m2 = jnp.mean(dxh * xhat, axis=-1, keepdims=True)
    return rstd * (dxh - m1 - xhat * m2)


def _row_call(body, ins, row_ins, outs, acc_outs, name, tr=256):
    S = ins[0].shape[0]
    tr = min(tr, S)
    n_in, n_row, n_out = len(ins), len(row_ins), len(outs)

    def wrapped(*refs):
        i = pl.program_id(0)
        acc_refs = refs[n_in + n_row + n_out:]

        @pl.when(i == 0)
        def _():
            for r in acc_refs:
                r[...] = jnp.zeros_like(r)

        body(*refs)

    in_specs = [pl.BlockSpec((tr, a.shape[1]), lambda i: (i, 0)) for a in ins]
    in_specs += [pl.BlockSpec(a.shape, lambda i: (0, 0)) for a in row_ins]
    out_specs = [pl.BlockSpec((tr, s.shape[1]), lambda i: (i, 0)) for s in outs]
    out_specs += [pl.BlockSpec(s.shape, lambda i: (0, 0)) for s in acc_outs]
    return pl.pallas_call(wrapped, grid=(S // tr,), in_specs=in_specs, out_specs=out_specs,
                          out_shape=list(outs) + list(acc_outs),
                          compiler_params=_params(("arbitrary",)), name=name)(*ins, *row_ins)


def _ln1_fwd(x, mixed, g, b):
    def body(x_ref, m_ref, g_ref, b_ref, h_ref):
        xhat, _ = _ln_stats(DN_ALPHA * x_ref[...] + m_ref[...])
        h_ref[...] = xhat * g_ref[...] + b_ref[...]
    return _row_call(body, [x, mixed], [g, b], [jax.ShapeDtypeStruct(x.shape, F32)], [], "ln1_fwd")[0]


def _ln2_loss(h1, mlp, g, b, target):
    S, D = h1.shape
    sds = jax.ShapeDtypeStruct

    def body(h_ref, m_ref, t_ref, g_ref, b_ref, dr_ref, loss_ref, dg_ref, db_ref):
        xhat, rstd = _ln_stats(DN_ALPHA * h_ref[...] + m_ref[...])
        gv = g_ref[...]
        err = xhat * gv + b_ref[...] - t_ref[...]
        loss_ref[...] += jnp.sum(jnp.sum(err * err, axis=0, keepdims=True), axis=1, keepdims=True) * (0.5 / D)
        dy = err * (1.0 / D)
        dg_ref[...] += jnp.sum(dy * xhat, axis=0, keepdims=True)
        db_ref[...] += jnp.sum(dy, axis=0, keepdims=True)
        dr_ref[...] = _ln_bwd(dy, xhat, rstd, gv)

    return _row_call(body, [h1, mlp, target], [g, b], [sds((S, D), F32)],
                     [sds((1, LANE), F32), sds((1, D), F32), sds((1, D), F32)], "ln2_loss")


def _ln1_bwd(x, mixed, g, dr2, dh_mlp):
    S, D = x.shape
    sds = jax.ShapeDtypeStruct

    def body(x_ref, m_ref, dr2_ref, dh_ref, g_ref, dr_ref, dg_ref, db_ref):
        xhat, rstd = _ln_stats(DN_ALPHA * x_ref[...] + m_ref[...])
        dy = DN_ALPHA * dr2_ref[...] + dh_ref[...]
        dg_ref[...] += jnp.sum(dy * xhat, axis=0, keepdims=True)
        db_ref[...] += jnp.sum(dy, axis=0, keepdims=True)
        dr_ref[...] = _ln_bwd(dy, xhat, rstd, g_ref[...])

    return _row_call(body, [x, mixed, dr2, dh_mlp], [g], [sds((S, D), F32)],
                     [sds((1, D), F32), sds((1, D), F32)], "ln1_bwd")


def _residual_grad(acc, dr):
    return DN_ALPHA * dr + acc


def _bucket_table():
    qi = np.arange(BLK, dtype=np.int32)[:, None]
    kj = np.arange(2 * BLK, dtype=np.int32)[None, :]
    dist = qi + BLK - kj
    n = np.maximum(dist, 0)
    max_exact = N_BUCKETS // 2
    nf = np.maximum(n, 1).astype(np.float32)
    large = max_exact + (np.log(nf / np.float32(max_exact)) / np.float32(math.log(MAX_DISTANCE / max_exact))
                         * np.float32(N_BUCKETS - max_exact)).astype(np.int32)
    large = np.minimum(large, N_BUCKETS - 1)
    bucket = np.where(n < max_exact, n, large)
    return np.where((dist >= 0) & (dist < BLK), bucket, -1).astype(np.int32)


def _attn_bias(rel_bias_t):
    hq = rel_bias_t.shape[0]
    bucket = jnp.asarray(_bucket_table())

    def body(rb_ref, bk_ref, o_ref):
        h = pl.program_id(0)
        bk = bk_ref[...]
        acc = jnp.zeros((BLK, 2 * BLK), F32)
        for b in range(N_BUCKETS):
            acc = jnp.where(bk == b, rb_ref[h, b], acc)
        o_ref[...] = acc

    return pl.pallas_call(
        body, grid=(hq,),
        in_specs=[pl.BlockSpec(memory_space=pltpu.SMEM), pl.BlockSpec((BLK, 2 * BLK), lambda h: (0, 0))],
        out_specs=pl.BlockSpec((BLK, 2 * BLK), lambda h: (h, 0)),
        out_shape=jax.ShapeDtypeStruct((hq * BLK, 2 * BLK), F32),
        compiler_params=_params(("arbitrary",)), name="attn_bias")(rel_bias_t, bucket)


def _attn_probs(sc, sp, bias, sink, mask_c, mask_p):
    lc = jnp.where(mask_c, sc + bias[:, BLK:], NEG_INF)
    lp = jnp.where(mask_p, sp + bias[:, :BLK], NEG_INF)
    m = jnp.maximum(jnp.maximum(jnp.max(lc, axis=1, keepdims=True), jnp.max(lp, axis=1, keepdims=True)), sink)
    pc, pp, ps = jnp.exp(lc - m), jnp.exp(lp - m), jnp.exp(sink - m)
    inv = 1.0 / (jnp.sum(pc, axis=1, keepdims=True) + jnp.sum(pp, axis=1, keepdims=True) + ps)
    return pc, pp, ps, inv


def _attn_masks(n):
    qi = lax.broadcasted_iota(jnp.int32, (BLK, BLK), 0)
    kj = lax.broadcasted_iota(jnp.int32, (BLK, BLK), 1)
    return kj <= qi, (kj > qi) & (n > 0)


def _attn_fwd(proj, bias, sinks, hq, q_blk, k_blk, v_blk, out_width):
    S = proj.shape[0]
    hkv = hq // GQA
    wq, wk = hq * HEAD_A, hkv * HEAD_A

    def body(q_ref, k_ref, v_ref, bias_ref, sink_ref, o_ref):
        n = pl.program_id(0)
        cur = pl.multiple_of(n * BLK, BLK)
        prev = pl.multiple_of(jnp.maximum(n - 1, 0) * BLK, BLK)
        mask_c, mask_p = _attn_masks(n)
        for h4 in range(hkv):
            cs = slice(h4 * HEAD_A, (h4 + 1) * HEAD_A)
            kc, kp = k_ref[pl.ds(cur, BLK), cs].astype(BF16), k_ref[pl.ds(prev, BLK), cs].astype(BF16)
            vc, vp = v_ref[pl.ds(cur, BLK), cs].astype(BF16), v_ref[pl.ds(prev, BLK), cs].astype(BF16)
            hs_of = [slice(h * HEAD_A, (h + 1) * HEAD_A) for h in range(h4 * GQA, (h4 + 1) * GQA)]
            qs = [(q_ref[:, hs] * (HEAD_A ** -0.5)).astype(BF16) for hs in hs_of]
            scs = [_dot(q, kc, NT) for q in qs]
            sps = [_dot(q, kp, NT) for q in qs]
            pr = [_attn_probs(scs[g], sps[g], bias_ref[(h4 * GQA + g) * BLK:(h4 * GQA + g + 1) * BLK, :],
                              sink_ref[h4 * GQA + g], mask_c, mask_p) for g in range(GQA)]
            oc = [_dot(p[0].astype(BF16), vc, NN) for p in pr]
            op = [_dot(p[1].astype(BF16), vp, NN) for p in pr]
            for g, hs in enumerate(hs_of):
                o_ref[:, hs] = (oc[g] + op[g]) * pr[g][3]

    return pl.pallas_call(
        body, grid=(S // BLK,),
        in_specs=[pl.BlockSpec((BLK, wq), lambda n: (n, q_blk)), pl.BlockSpec((S, wk), lambda n: (0, k_blk)),
                  pl.BlockSpec((S, wk), lambda n: (0, v_blk)), pl.BlockSpec((hq * BLK, 2 * BLK), lambda n: (0, 0)),
                  pl.BlockSpec(memory_space=pltpu.SMEM)],
        out_specs=pl.BlockSpec((BLK, wq), lambda n: (n, 0)),
        out_shape=jax.ShapeDtypeStruct((S, out_width), F32),
        compiler_params=_params(("arbitrary",)), name="attn_fwd")(proj, proj, proj, bias, sinks)


def _attn_bwd(proj, bias, sinks, out, dmix, hq, q_blk, k_blk, v_blk):
    S = proj.shape[0]
    hkv = hq // GQA
    wq, wk = hq * HEAD_A, hkv * HEAD_A
    sds = jax.ShapeDtypeStruct

    def body(q_ref, k_ref, v_ref, bias_ref, sink_ref, o_ref, do_ref, dq_ref, dk_ref, dv_ref, dbias_ref, dsink_ref):
        n = pl.program_id(0)

        @pl.when(n == 0)
        def _():
            dk_ref[...] = jnp.zeros_like(dk_ref)
            dv_ref[...] = jnp.zeros_like(dv_ref)
            dbias_ref[...] = jnp.zeros_like(dbias_ref)
            dsink_ref[...] = jnp.zeros_like(dsink_ref)

        cur = pl.multiple_of(n * BLK, BLK)
        prev = pl.multiple_of(jnp.maximum(n - 1, 0) * BLK, BLK)
        mask_c, mask_p = _attn_masks(n)
        for h4 in range(hkv):
            cs = slice(h4 * HEAD_A, (h4 + 1) * HEAD_A)
            kc, kp = k_ref[pl.ds(cur, BLK), cs].astype(BF16), k_ref[pl.ds(prev, BLK), cs].astype(BF16)
            vc, vp = v_ref[pl.ds(cur, BLK), cs].astype(BF16), v_ref[pl.ds(prev, BLK), cs].astype(BF16)
            heads = list(range(h4 * GQA, (h4 + 1) * GQA))
            hs_of = [slice(h * HEAD_A, (h + 1) * HEAD_A) for h in heads]
            rows_of = [slice(h * BLK, (h + 1) * BLK) for h in heads]
            G = range(GQA)
            qs = [(q_ref[:, hs] * (HEAD_A ** -0.5)).astype(BF16) for hs in hs_of]
            dos = [do_ref[:, hs] for hs in hs_of]
            dobs = [d.astype(BF16) for d in dos]
            scs = [_dot(q, kc, NT) for q in qs]
            sps = [_dot(q, kp, NT) for q in qs]
            dpc = [_dot(d, vc, NT) for d in dobs]
            dpp = [_dot(d, vp, NT) for d in dobs]
            pcs, pps, dscs, dsps = [], [], [], []
            for g in G:
                pc, pp, ps, inv = _attn_probs(scs[g], sps[g], bias_ref[rows_of[g], :], sink_ref[heads[g]], mask_c, mask_p)
                pc, pp, ps = pc * inv, pp * inv, ps * inv
                delta = jnp.sum(dos[g] * o_ref[:, hs_of[g]], axis=1, keepdims=True)
                dsc, dsp = pc * (dpc[g] - delta), pp * (dpp[g] - delta)
                dsink_ref[heads[g]:heads[g] + 1, :] += jnp.broadcast_to(jnp.sum(-ps * delta, axis=0, keepdims=True), (1, LANE))
                dbias_ref[rows_of[g], BLK:] += dsc
                dbias_ref[rows_of[g], :BLK] += dsp
                pcs.append(pc.astype(BF16))
                pps.append(pp.astype(BF16))
                dscs.append(dsc.astype(BF16))
                dsps.append(dsp.astype(BF16))
            dq1 = [_dot(dscs[g], kc, NN) for g in G]
            dq2 = [_dot(dsps[g], kp, NN) for g in G]
            dkc = [_dot(dscs[g], qs[g], TN) for g in G]
            dkp = [_dot(dsps[g], qs[g], TN) for g in G]
            dvc = [_dot(pcs[g], dobs[g], TN) for g in G]
            dvp = [_dot(pps[g], dobs[g], TN) for g in G]
            for g in G:
                dq_ref[:, hs_of[g]] = (dq1[g] + dq2[g]) * (HEAD_A ** -0.5)
            dk_ref[pl.ds(cur, BLK), cs] += sum(dkc[1:], dkc[0])
            dk_ref[pl.ds(prev, BLK), cs] += sum(dkp[1:], dkp[0])
            dv_ref[pl.ds(cur, BLK), cs] += sum(dvc[1:], dvc[0])
            dv_ref[pl.ds(prev, BLK), cs] += sum(dvp[1:], dvp[0])

    return pl.pallas_call(
        body, grid=(S // BLK,),
        in_specs=[pl.BlockSpec((BLK, wq), lambda n: (n, q_blk)), pl.BlockSpec((S, wk), lambda n: (0, k_blk)),
                  pl.BlockSpec((S, wk), lambda n: (0, v_blk)), pl.BlockSpec((hq * BLK, 2 * BLK), lambda n: (0, 0)),
                  pl.BlockSpec(memory_space=pltpu.SMEM),
                  pl.BlockSpec((BLK, wq), lambda n: (n, 0)), pl.BlockSpec((BLK, wq), lambda n: (n, 0))],
        out_specs=[pl.BlockSpec((BLK, wq), lambda n: (n, 0)), pl.BlockSpec((S, wk), lambda n: (0, 0)),
                   pl.BlockSpec((S, wk), lambda n: (0, 0)), pl.BlockSpec((hq * BLK, 2 * BLK), lambda n: (0, 0)),
                   pl.BlockSpec((hq, LANE), lambda n: (0, 0))],
        out_shape=[sds((S, wq), F32), sds((S, wk), F32), sds((S, wk), F32), sds((hq * BLK, 2 * BLK), F32),
                   sds((hq, LANE), F32)],
        compiler_params=_params(("arbitrary",)), name="attn_bwd")(proj, proj, proj, bias, sinks, out, dmix)


def _rel_bias_grad(dbias, hq):
    bucket = jnp.asarray(_bucket_table())

    def body(d_ref, bk_ref, o_ref):
        d = d_ref[...]
        bk = bk_ref[...]
        rows = [jnp.sum(jnp.where(bk == b, d, 0.0), axis=0, keepdims=True) for b in range(N_BUCKETS)]
        tot = jnp.sum(jnp.concatenate(rows, axis=0), axis=1, keepdims=True)
        o_ref[...] = jnp.broadcast_to(tot, (N_BUCKETS, LANE))

    return pl.pallas_call(
        body, grid=(hq,),
        in_specs=[pl.BlockSpec((BLK, 2 * BLK), lambda h: (h, 0)), pl.BlockSpec((BLK, 2 * BLK), lambda h: (0, 0))],
        out_specs=pl.BlockSpec((None, N_BUCKETS, LANE), lambda h: (h, 0, 0)),
        out_shape=jax.ShapeDtypeStruct((hq, N_BUCKETS, LANE), F32),
        compiler_params=_params(("arbitrary",)), name="rel_bias_grad")(dbias, bucket)


def _sigmoid(x):
    return 1.0 / (1.0 + jnp.exp(-x))


def _shift_rows(x, s):
    n = x.shape[0]
    row = lax.broadcasted_iota(jnp.int32, x.shape, 0)
    if s > 0:
        return jnp.where(row >= s, pltpu.roll(x, s, 0), 0.0)
    return jnp.where(row < n + s, pltpu.roll(x, n + s, 0), 0.0)


def _conv_silu_norm(xv, w, j, nh):
    c = w[CONV_W - 1:CONV_W, :] * xv
    for s in range(1, CONV_W):
        c = c + w[CONV_W - 1 - s:CONV_W - s, :] * _shift_rows(xv, s)
    sg = _sigmoid(c)
    a = c * sg
    r = lax.rsqrt(jnp.sum(a * a, axis=1, keepdims=True) + RMS_EPS)
    scale = jnp.where(j < nh, HEAD_D ** -0.5, 1.0)
    is_norm = j < 2 * nh
    y = jnp.where(is_norm, a * (r * scale), a)
    return c, sg, a, r, scale, is_norm, y


def _gdn_prep_fwd(proj, conv_w, nh, blk0):
    S = proj.shape[0]

    def body(x_ref, w_ref, o_ref):
        j = pl.program_id(0)
        o_ref[...] = _conv_silu_norm(x_ref[...], w_ref[...], j, nh)[-1]

    return pl.pallas_call(
        body, grid=(3 * nh,),
        in_specs=[pl.BlockSpec((S, HEAD_D), lambda j: (0, blk0 + j)), pl.BlockSpec((CONV_W, HEAD_D), lambda j: (0, j))],
        out_specs=pl.BlockSpec((S, HEAD_D), lambda j: (0, 3 * (j % nh) + j // nh)),
        out_shape=jax.ShapeDtypeStruct((S, 3 * nh * HEAD_D), F32),
        compiler_params=_params(("parallel",)), name="gdn_prep_fwd")(proj, conv_w)


def _gdn_prep_bwd(proj, conv_w, dqkv, nh, blk0):
    S = proj.shape[0]
    sds = jax.ShapeDtypeStruct

    def body(x_ref, w_ref, dy_ref, dx_ref, dw_ref):
        j = pl.program_id(0)
        xv, w = x_ref[...], w_ref[...]
        c, sg, a, r, scale, is_norm, _ = _conv_silu_norm(xv, w, j, nh)
        dy = dy_ref[...]
        rs = r * scale
        da_n = rs * dy - a * (r * r * rs) * jnp.sum(dy * a, axis=1, keepdims=True)
        da = jnp.where(is_norm, da_n, dy)
        dc = da * (sg * (1.0 + c * (1.0 - sg)))
        dx = w[CONV_W - 1:CONV_W, :] * dc
        dws = [jnp.sum(dc * xv, axis=0, keepdims=True)]
        for s in range(1, CONV_W):
            dx = dx + w[CONV_W - 1 - s:CONV_W - s, :] * _shift_rows(dc, -s)
            dws.insert(0, jnp.sum(dc * _shift_rows(xv, s), axis=0, keepdims=True))
        dx_ref[...] = dx
        dw_ref[...] = jnp.concatenate(dws, axis=0)

    return pl.pallas_call(
        body, grid=(3 * nh,),
        in_specs=[pl.BlockSpec((S, HEAD_D), lambda j: (0, blk0 + j)), pl.BlockSpec((CONV_W, HEAD_D), lambda j: (0, j)),
                  pl.BlockSpec((S, HEAD_D), lambda j: (0, 3 * (j % nh) + j // nh))],
        out_specs=[pl.BlockSpec((S, HEAD_D), lambda j: (0, j)), pl.BlockSpec((CONV_W, HEAD_D), lambda j: (0, j))],
        out_shape=[sds((S, 3 * nh * HEAD_D), F32), sds((CONV_W, 3 * nh * HEAD_D), F32)],
        compiler_params=_params(("parallel",)), name="gdn_prep_bwd")(proj, conv_w, dqkv)


def _softplus(x):
    return jnp.maximum(x, 0.0) + jnp.log(1.0 + jnp.exp(-jnp.abs(x)))


def _gates_fwd(ab, al, dt, nh):
    S = ab.shape[0]

    def body(ab_ref, al_ref, dt_ref, o_ref):
        v = ab_ref[...]
        lane = lax.broadcasted_iota(jnp.int32, v.shape, 1)
        g = -jnp.exp(al_ref[...]) * _softplus(v + dt_ref[...])
        o_ref[...] = jnp.where(lane < nh, g, jnp.where(lane < 2 * nh, _sigmoid(v), 0.0))

    row = pl.BlockSpec((1, LANE), lambda i: (0, 0))
    full = pl.BlockSpec((S, LANE), lambda i: (0, 0))
    return pl.pallas_call(body, grid=(1,), in_specs=[full, row, row], out_specs=full,
                          out_shape=jax.ShapeDtypeStruct((S, LANE), F32),
                          compiler_params=_params(("arbitrary",)), name="gates_fwd")(ab, al, dt)


def _gates_bwd(ab, al, dt, dgb, nh):
    S = ab.shape[0]
    sds = jax.ShapeDtypeStruct

    def body(ab_ref, al_ref, dt_ref, d_ref, dab_ref, dal_ref, ddt_ref):
        v, d = ab_ref[...], d_ref[...]
        lane = lax.broadcasted_iota(jnp.int32, v.shape, 1)
        is_a = lane < nh
        z = v + dt_ref[...]
        dsp = jnp.where(is_a, d * (-jnp.exp(al_ref[...])), 0.0)
        dz = dsp * _sigmoid(z)
        beta = _sigmoid(v)
        dab_ref[...] = jnp.where(is_a, dz, jnp.where(lane < 2 * nh, d * beta * (1.0 - beta), 0.0))
        dal_ref[...] = jnp.sum(dsp * _softplus(z), axis=0, keepdims=True)
        ddt_ref[...] = jnp.sum(dz, axis=0, keepdims=True)

    row = pl.BlockSpec((1, LANE), lambda i: (0, 0))
    full = pl.BlockSpec((S, LANE), lambda i: (0, 0))
    return pl.pallas_call(body, grid=(1,), in_specs=[full, row, row, full], out_specs=[full, row, row],
                          out_shape=[sds((S, LANE), F32), sds((1, LANE), F32), sds((1, LANE), F32)],
                          compiler_params=_params(("arbitrary",)), name="gates_bwd")(ab, al, dt, dgb)


def _col_of(tile, h):
    lane = lax.broadcasted_iota(jnp.int32, tile.shape, 1)
    return jnp.sum(jnp.where(lane == h, tile, 0.0), axis=1, keepdims=True)


def _to_row(col, eye):
    return jnp.sum(jnp.where(eye, col, 0.0), axis=0, keepdims=True)


def _to_col(row, eye):
    return jnp.sum(jnp.where(eye, row, 0.0), axis=1, keepdims=True)


def _split(a):
    hi = a.astype(BF16)
    return hi, (a - hi.astype(F32)).astype(BF16)


def _gdot(a, b, dims):
    ah, al = _split(a)
    bh, bl = _split(b)
    return _dot(ah, bh, dims) + (_dot(ah, bl, dims) + _dot(al, bh, dims))


def _bdot(a, b, dims):
    return _dot(a.astype(BF16), b.astype(BF16), dims)


def _chunks_local(qs, ks, vs, gcols, bcols, Ts=None):
    C = CHUNK
    row = lax.broadcasted_iota(jnp.int32, (C, C), 0)
    col = lax.broadcasted_iota(jnp.int32, (C, C), 1)
    tril, strict, eye = col <= row, col < row, col == row
    outs = []
    for k, gcol, bcol in zip(ks, gcols, bcols):
        grow = _to_row(gcol, eye)
        G_row = jnp.sum(jnp.where(row <= col, gcol, 0.0), axis=0, keepdims=True)
        G_col = jnp.sum(jnp.where(tril, grow, 0.0), axis=1, keepdims=True)
        G_last = G_col[C - 1:C, :]
        outs.append(dict(strict=strict, eye=eye, row=row, col=col, decay=jnp.exp(jnp.where(tril, G_col - G_row, NEG_INF)),
                         eG=jnp.exp(G_col), eGr=jnp.exp(G_last - G_col), gl=jnp.exp(G_last), kb=k * bcol))
    Ms = [_gdot(o["kb"], k, NT) for o, k in zip(outs, ks)]
    Ns = [_gdot(q, k, NT) for q, k in zip(qs, ks)]
    for o, q, k, M, N in zip(outs, qs, ks, Ms, Ns):
        o.update(A=jnp.where(strict, M * o["decay"], 0.0), attn=N * o["decay"], rhs_k=o["kb"] * o["eG"],
                 q_dec=q * o["eG"], k_dec=k * o["eGr"])
    if Ts is None:
        Ts = [jnp.where(eye, 1.0, 0.0) - o["A"] for o in outs]
        Ps = [o["A"] for o in outs]
        for _ in range(int(math.log2(C)) - 1):
            Ps = [_bdot(P, P, NN) for P in Ps]
            Ts = [T + _bdot(T, P, NN) for T, P in zip(Ts, Ps)]
        us = [_bdot(T, v * bcol, NN) for T, v, bcol in zip(Ts, vs, bcols)]
        ws = [_bdot(T, o["rhs_k"], NN) for T, o in zip(Ts, outs)]
        for o, T, u, w in zip(outs, Ts, us, ws):
            o.update(T=T, u=u, w=w)
    return outs


GDN_ROWS = 256
GDN_LOCAL_ROWS = 512
WQK = 3 * CHUNK


def _gdn_local_fwd(qkv, gb, nh):
    S = qkv.shape[0]
    nc = S // CHUNK
    rb = min(GDN_LOCAL_ROWS, S)
    cpb = rb // CHUNK
    sds = jax.ShapeDtypeStruct

    def body(q_ref, k_ref, v_ref, gb_ref, u_ref, wqk_ref, attn_ref, t_ref):
        h = pl.program_id(0)
        rows_of = [slice(ci * CHUNK, (ci + 1) * CHUNK) for ci in range(cpb)]
        gbts = [gb_ref[rows, :] for rows in rows_of]
        Ls = _chunks_local([q_ref[rows, :] for rows in rows_of], [k_ref[rows, :] for rows in rows_of],
                           [v_ref[rows, :] for rows in rows_of], [_col_of(t, h) for t in gbts],
                           [_col_of(t, nh + h) for t in gbts])
        for ci, (rows, L) in enumerate(zip(rows_of, Ls)):
            u_ref[rows, :] = L["u"]
            base = ci * WQK
            wqk_ref[base:base + CHUNK, :] = L["w"]
            wqk_ref[base + CHUNK:base + 2 * CHUNK, :] = L["q_dec"]
            wqk_ref[base + 2 * CHUNK:base + WQK, :] = L["k_dec"]
            attn_ref[ci] = L["attn"]
            t_ref[ci] = L["T"]

    cc = pl.BlockSpec((None, cpb, CHUNK, CHUNK), lambda h, i: (h, i, 0, 0))
    return pl.pallas_call(
        body, grid=(nh, S // rb),
        in_specs=[pl.BlockSpec((rb, HEAD_D), lambda h, i: (i, 3 * h)), pl.BlockSpec((rb, HEAD_D), lambda h, i: (i, 3 * h + 1)),
                  pl.BlockSpec((rb, HEAD_D), lambda h, i: (i, 3 * h + 2)), pl.BlockSpec((rb, LANE), lambda h, i: (i, 0))],
        out_specs=[pl.BlockSpec((rb, HEAD_D), lambda h, i: (i, h)),
                   pl.BlockSpec((None, 3 * rb, HEAD_D), lambda h, i: (h, i, 0)), cc, cc],
        out_shape=[sds((S, nh * HEAD_D), F32), sds((nh, 3 * S, HEAD_D), F32), sds((nh, nc, CHUNK, CHUNK), F32),
                   sds((nh, nc, CHUNK, CHUNK), F32)],
        compiler_params=_params(("parallel", "parallel")), name="gdn_local_fwd")(qkv, qkv, qkv, gb)


def _gdn_scan_fwd(u, wqk, attn, gb, nh, dep):
    S = u.shape[0]
    nc = S // CHUNK
    rb = min(GDN_ROWS, S)
    cpb = rb // CHUNK
    sds = jax.ShapeDtypeStruct

    def body(u_ref, wqk_ref, attn_ref, gb_ref, dep_ref, o_ref, vn_ref, st_ref, s_ref):
        @pl.when(pl.program_id(0) == 0)
        def _():
            s_ref[...] = jnp.zeros_like(s_ref)

        for ci in range(cpb):
            rows = slice(ci * CHUNK, (ci + 1) * CHUNK)
            glv = jnp.exp(jnp.sum(gb_ref[rows, :], axis=0, keepdims=True))
            base = ci * WQK
            heads = range(nh)
            cols = [slice(h * HEAD_D, (h + 1) * HEAD_D) for h in heads]
            states = [s_ref[h] for h in heads]
            rs = [_gdot(wqk_ref[h, base:base + 2 * CHUNK, :], states[h], NN) for h in heads]
            vbs = [u_ref[rows, cols[h]] - rs[h][:CHUNK] for h in heads]
            os_ = [_gdot(attn_ref[h, ci], vbs[h], NN) for h in heads]
            ks_ = [_gdot(wqk_ref[h, base + 2 * CHUNK:base + WQK, :], vbs[h], TN) for h in heads]
            for h in heads:
                st_ref[h, ci] = states[h]
                o_ref[rows, cols[h]] = rs[h][CHUNK:] + os_[h]
                vn_ref[rows, cols[h]] = vbs[h]
                s_ref[h] = states[h] * glv[:, h:h + 1] + ks_[h]

    return pl.pallas_call(
        body, grid=(S // rb,),
        in_specs=[pl.BlockSpec((rb, nh * HEAD_D), lambda i: (i, 0)), pl.BlockSpec((nh, 3 * rb, HEAD_D), lambda i: (0, i, 0)),
                  pl.BlockSpec((nh, cpb, CHUNK, CHUNK), lambda i: (0, i, 0, 0)), pl.BlockSpec((rb, LANE), lambda i: (i, 0)),
                  pl.BlockSpec(memory_space=pl.ANY)],
        out_specs=[pl.BlockSpec((rb, nh * HEAD_D), lambda i: (i, 0)), pl.BlockSpec((rb, nh * HEAD_D), lambda i: (i, 0)),
                   pl.BlockSpec((nh, cpb, HEAD_D, HEAD_D), lambda i: (0, i, 0, 0))],
        out_shape=[sds((S, nh * HEAD_D), F32), sds((S, nh * HEAD_D), F32), sds((nh, nc, HEAD_D, HEAD_D), F32)],
        scratch_shapes=[pltpu.VMEM((nh, HEAD_D, HEAD_D), F32)],
        compiler_params=_params(("arbitrary",)), name="gdn_scan_fwd")(u, wqk, attn, gb, dep)


def _gdn_scan_bwd(wqk, attn, gb, states, vn, do, nh, dep):
    S = vn.shape[0]
    nc = S // CHUNK
    rb = min(GDN_ROWS, S)
    cpb = rb // CHUNK
    last = S // rb - 1
    sds = jax.ShapeDtypeStruct

    def body(wqk_ref, attn_ref, gb_ref, st_ref, vn_ref, do_ref, dep_ref, dvn_ref, dw_ref, dqd_ref, dkd_ref, da_ref, dgl_ref,
             ds_ref):
        @pl.when(pl.program_id(0) == 0)
        def _():
            ds_ref[...] = jnp.zeros_like(ds_ref)

        row = lax.broadcasted_iota(jnp.int32, (CHUNK, CHUNK), 0)
        col = lax.broadcasted_iota(jnp.int32, (CHUNK, CHUNK), 1)
        for ci in reversed(range(cpb)):
            rows = slice(ci * CHUNK, (ci + 1) * CHUNK)
            glv = jnp.exp(jnp.sum(gb_ref[rows, :], axis=0, keepdims=True))
            base = ci * WQK
            heads = range(nh)
            cols = [slice(h * HEAD_D, (h + 1) * HEAD_D) for h in heads]
            states = [st_ref[h, ci] for h in heads]
            dSs = [ds_ref[h] for h in heads]
            vbs = [vn_ref[rows, cols[h]] for h in heads]
            dobs = [do_ref[rows, cols[h]] for h in heads]
            dv1 = [_gdot(attn_ref[h, ci], dobs[h], TN) for h in heads]
            dv2 = [_gdot(wqk_ref[h, base + 2 * CHUNK:base + WQK, :], dSs[h], NN) for h in heads]
            das = [_gdot(dobs[h], vbs[h], NT) for h in heads]
            dkds = [_gdot(vbs[h], dSs[h], NT) for h in heads]
            dvbs = [dv1[h] + dv2[h] for h in heads]
            xs = [_gdot(jnp.concatenate([dobs[h], dvbs[h]], axis=0), states[h], NT) for h in heads]
            dss = [_gdot(wqk_ref[h, base:base + 2 * CHUNK, :], jnp.concatenate([-dvbs[h], dobs[h]], axis=0), TN)
                   for h in heads]
            for h in heads:
                dqd_ref[rows, cols[h]] = xs[h][:CHUNK]
                dw_ref[rows, cols[h]] = -xs[h][CHUNK:]
                dvn_ref[rows, cols[h]] = dvbs[h]
                da_ref[h, ci] = jnp.where(col <= row, das[h], 0.0)
                dkd_ref[rows, cols[h]] = dkds[h]
                gl = glv[:, h:h + 1]
                dgl = jnp.sum(jnp.sum(states[h] * dSs[h], axis=0, keepdims=True), axis=1, keepdims=True)
                dgl_ref[h, ci] = jnp.broadcast_to(dgl * gl, (1, LANE))
                ds_ref[h] = dSs[h] * gl + dss[h]

    rv = lambda i: last - i
    wide = pl.BlockSpec((rb, nh * HEAD_D), lambda i: (rv(i), 0))
    return pl.pallas_call(
        body, grid=(S // rb,),
        in_specs=[pl.BlockSpec((nh, 3 * rb, HEAD_D), lambda i: (0, rv(i), 0)),
                  pl.BlockSpec((nh, cpb, CHUNK, CHUNK), lambda i: (0, rv(i), 0, 0)),
                  pl.BlockSpec((rb, LANE), lambda i: (rv(i), 0)),
                  pl.BlockSpec((nh, cpb, HEAD_D, HEAD_D), lambda i: (0, rv(i), 0, 0)), wide, wide,
                  pl.BlockSpec(memory_space=pl.ANY)],
        out_specs=[wide, wide, wide, wide, pl.BlockSpec((nh, cpb, CHUNK, CHUNK), lambda i: (0, rv(i), 0, 0)),
                   pl.BlockSpec((nh, cpb, 1, LANE), lambda i: (0, rv(i), 0, 0))],
        out_shape=[sds((S, nh * HEAD_D), F32), sds((S, nh * HEAD_D), F32), sds((S, nh * HEAD_D), F32),
                   sds((S, nh * HEAD_D), F32), sds((nh, nc, CHUNK, CHUNK), F32), sds((nh, nc, 1, LANE), F32)],
        scratch_shapes=[pltpu.VMEM((nh, HEAD_D, HEAD_D), F32)],
        compiler_params=_params(("arbitrary",)), name="gdn_scan_bwd")(wqk, attn, gb, states, vn, do, dep)


def _gdn_local_bwd(qkv, gb, T, u, wqk, dvn, dw, dqd, dkd, dattn, dgl, nh):
    S = qkv.shape[0]
    rb = min(GDN_LOCAL_ROWS, S)
    cpb = rb // CHUNK
    sds = jax.ShapeDtypeStruct

    def body(q_ref, k_ref, v_ref, gb_ref, t_ref, u_ref, wqk_ref, dvn_ref, dw_ref, dqd_ref, dkd_ref, da_ref, dgl_ref,
             dqkv_ref, dg_ref, db_ref):
        h = pl.program_id(0)
        n = range(cpb)
        rows_of = [slice(ci * CHUNK, (ci + 1) * CHUNK) for ci in n]
        qs, ks, vs = ([r[rows, :] for rows in rows_of] for r in (q_ref, k_ref, v_ref))
        gbts = [gb_ref[rows, :] for rows in rows_of]
        bcols = [_col_of(t, nh + h) for t in gbts]
        Ts = [t_ref[ci] for ci in n]
        Ls = _chunks_local(qs, ks, vs, [_col_of(t, h) for t in gbts], bcols, Ts=Ts)
        drvs = [_gdot(Ts[ci], dvn_ref[rows_of[ci], :], TN) for ci in n]
        drks = [_gdot(Ts[ci], dw_ref[rows_of[ci], :], TN) for ci in n]
        dAs = [jnp.where(Ls[ci]["strict"], -(_gdot(drvs[ci], u_ref[rows_of[ci], :], NT)
                                             + _gdot(drks[ci], wqk_ref[ci * WQK:ci * WQK + CHUNK, :], NT)), 0.0) for ci in n]
        dMs = [dAs[ci] * Ls[ci]["decay"] for ci in n]
        dNs = [da_ref[ci] * Ls[ci]["decay"] for ci in n]
        dkbs = [_gdot(dMs[ci], ks[ci], NN) for ci in n]
        dq1 = [_gdot(dNs[ci], ks[ci], NN) for ci in n]
        dk1 = [_gdot(dMs[ci], Ls[ci]["kb"], TN) for ci in n]
        dk2 = [_gdot(dNs[ci], qs[ci], TN) for ci in n]
        for ci in n:
            rows, L, q, k, v, bcol = rows_of[ci], Ls[ci], qs[ci], ks[ci], vs[ci], bcols[ci]
            eye, eG, eGr = L["eye"], L["eG"], L["eGr"]
            drv, drk, dkb = drvs[ci], drks[ci], dkbs[ci]
            dq_dec, dk_dec, dattn_c = dqd_ref[rows, :], dkd_ref[rows, :], da_ref[ci]
            dqkv_ref[rows, :HEAD_D] = dq1[ci] + dq_dec * eG
            dqkv_ref[rows, HEAD_D:2 * HEAD_D] = drk * (bcol * eG) + dk1[ci] + dkb * bcol + dk2[ci] + dk_dec * eGr
            dqkv_ref[rows, 2 * HEAD_D:] = drv * bcol
            db_ref[rows, :] = (jnp.sum(drv * v, axis=1, keepdims=True) + jnp.sum(drk * k, axis=1, keepdims=True) * eG
                               + jnp.sum(dkb * k, axis=1, keepdims=True))
            E = dAs[ci] * L["A"] + dattn_c * L["attn"]
            kd = jnp.sum(dk_dec * L["k_dec"], axis=1, keepdims=True)
            dG = (jnp.sum(dq_dec * L["q_dec"], axis=1, keepdims=True) - kd
                  + jnp.sum(drk * L["rhs_k"], axis=1, keepdims=True)
                  + jnp.sum(E, axis=1, keepdims=True) - _to_col(jnp.sum(E, axis=0, keepdims=True), eye))
            d_last = jnp.sum(kd, axis=0, keepdims=True) + dgl_ref[ci][:, :1]
            dG = dG + jnp.where(L["row"][:, :1] == CHUNK - 1, d_last, 0.0)
            dg_ref[rows, :] = jnp.sum(jnp.where(L["col"] >= L["row"], _to_row(dG, eye), 0.0), axis=1, keepdims=True)

    hd = pl.BlockSpec((rb, HEAD_D), lambda h, i: (i, h))
    cc = pl.BlockSpec((None, cpb, CHUNK, CHUNK), lambda h, i: (h, i, 0, 0))
    col1 = pl.BlockSpec((None, rb, 1), lambda h, i: (h, i, 0))
    return pl.pallas_call(
        body, grid=(nh, S // rb),
        in_specs=[pl.BlockSpec((rb, HEAD_D), lambda h, i: (i, 3 * h)), pl.BlockSpec((rb, HEAD_D), lambda h, i: (i, 3 * h + 1)),
                  pl.BlockSpec((rb, HEAD_D), lambda h, i: (i, 3 * h + 2)), pl.BlockSpec((rb, LANE), lambda h, i: (i, 0)),
                  cc, hd, pl.BlockSpec((None, 3 * rb, HEAD_D), lambda h, i: (h, i, 0)), hd, hd, hd, hd, cc,
                  pl.BlockSpec((None, cpb, 1, LANE), lambda h, i: (h, i, 0, 0))],
        out_specs=[pl.BlockSpec((rb, 3 * HEAD_D), lambda h, i: (i, h)), col1, col1],
        out_shape=[sds((S, 3 * nh * HEAD_D), F32)] + [sds((nh, S, 1), F32)] * 2,
        compiler_params=_params(("parallel", "parallel")), name="gdn_local_bwd")(
            qkv, qkv, qkv, gb, T, u, wqk, dvn, dw, dqd, dkd, dattn, dgl)


def _gated_norm_fwd(o, proj, norm_w, nh, z_blk0, mix, m_blk0):
    S = o.shape[0]

    def body(o_ref, z_ref, w_ref, mix_ref, y_ref):
        ov, z = o_ref[...], z_ref[...]
        r = lax.rsqrt(jnp.mean(ov * ov, axis=1, keepdims=True) + RMS_EPS)
        y_ref[...] = ov * r * w_ref[...] * (z * _sigmoid(z))

    return pl.pallas_call(
        body, grid=(nh,),
        in_specs=[pl.BlockSpec((S, HEAD_D), lambda h: (0, h)), pl.BlockSpec((S, HEAD_D), lambda h: (0, z_blk0 + h)),
                  pl.BlockSpec((1, HEAD_D), lambda h: (0, 0)), pl.BlockSpec(memory_space=pl.ANY)],
        out_specs=pl.BlockSpec((S, HEAD_D), lambda h: (0, m_blk0 + h)),
        out_shape=jax.ShapeDtypeStruct(mix.shape, F32), input_output_aliases={3: 0},
        compiler_params=_params(("parallel",)), name="gated_norm_fwd")(o, proj, norm_w, mix)


def _gated_norm_bwd(o, proj, norm_w, dmix, nh, z_blk0, d_blk0):
    S = o.shape[0]
    sds = jax.ShapeDtypeStruct

    def body(o_ref, z_ref, w_ref, dy_ref, do_ref, dz_ref, dw_ref):
        ov, z, w, dy = o_ref[...], z_ref[...], w_ref[...], dy_ref[...]
        r = lax.rsqrt(jnp.mean(ov * ov, axis=1, keepdims=True) + RMS_EPS)
        oh = ov * r
        sg = _sigmoid(z)
        dz_ref[...] = dy * (oh * w) * (sg * (1.0 + z * (1.0 - sg)))
        don = dy * (z * sg)
        @pl.when(pl.program_id(0) == 0)
        def _():
            dw_ref[...] = jnp.zeros_like(dw_ref)

        dw_ref[...] += jnp.sum(don * oh, axis=0, keepdims=True)
        doh = don * w
        do_ref[...] = r * (doh - oh * jnp.mean(doh * oh, axis=1, keepdims=True))

    return pl.pallas_call(
        body, grid=(nh,),
        in_specs=[pl.BlockSpec((S, HEAD_D), lambda h: (0, h)), pl.BlockSpec((S, HEAD_D), lambda h: (0, z_blk0 + h)),
                  pl.BlockSpec((1, HEAD_D), lambda h: (0, 0)), pl.BlockSpec((S, HEAD_D), lambda h: (0, d_blk0 + h))],
        out_specs=[pl.BlockSpec((S, HEAD_D), lambda h: (0, h)), pl.BlockSpec((S, HEAD_D), lambda h: (0, h)),
                   pl.BlockSpec((1, HEAD_D), lambda h: (0, 0))],
        out_shape=[sds((S, nh * HEAD_D), F32), sds((S, nh * HEAD_D), F32), sds((1, HEAD_D), F32)],
        compiler_params=_params(("arbitrary",)), name="gated_norm_bwd")(o, proj, norm_w, dmix)


def _adamw_math(w, g, m, v):
    m = ADAM_B1 * m + (1.0 - ADAM_B1) * g
    v = ADAM_B2 * v + (1.0 - ADAM_B2) * (g * g)
    m_hat = m / (1.0 - ADAM_B1 ** ADAM_STEP)
    v_hat = v / (1.0 - ADAM_B2 ** ADAM_STEP)
    delta = -ADAM_LR * (m_hat / (jnp.sqrt(v_hat) + ADAM_EPS) + ADAM_WD * w)
    return delta, m, v


def _slab_tiles(R, C, rows=256, cols=256):
    if R % rows == 0:
        return (rows, C), R // rows, lambda i: (i, 0)
    tc = _tile(C, cols)
    return (R, tc), C // tc, lambda i: (0, i)


def _adamw_big(parts, terms, chip, w, m, v, name):
    R, C = w.shape
    blk, steps, at = _slab_tiles(R, C)
    sds = jax.ShapeDtypeStruct

    def body(q_ref, p_ref, t_ref, w_ref, m_ref, v_ref, g_ref, d_ref, nm_ref, nv_ref):
        g = ((p_ref[...].astype(F32) + t_ref[0].astype(F32)) + t_ref[1].astype(F32)) + t_ref[2].astype(F32)
        g_ref[...] = g
        d_ref[...], nm_ref[...], nv_ref[...] = _adamw_math(w_ref[...], g, m_ref[...], v_ref[...])

    spec = pl.BlockSpec(blk, lambda i, q_ref: at(i))
    grid_spec = pltpu.PrefetchScalarGridSpec(
        num_scalar_prefetch=1, grid=(steps,),
        in_specs=[pl.BlockSpec((None,) + blk, lambda i, q_ref: (q_ref[0],) + at(i)),
                  pl.BlockSpec((3,) + blk, lambda i, q_ref: (0,) + at(i)), spec, spec, spec],
        out_specs=[spec] * 4)
    return pl.pallas_call(body, grid_spec=grid_spec, out_shape=[sds((R, C), F32)] * 4,
                          compiler_params=_params(("parallel",)), name=name)(chip, parts, terms, w, m, v)


def _adamw_small(ws, gs, ms, vs):
    n = len(ws)

    def body(*refs):
        for i in range(n):
            w, g, m, v = (refs[k * n + i][...] for k in range(4))
            d, nm, nv = _adamw_math(w, g, m, v)
            refs[4 * n + i][...] = d
            refs[5 * n + i][...] = nm
            refs[6 * n + i][...] = nv

    shapes = [jax.ShapeDtypeStruct(w.shape, F32) for w in ws]
    vm = pl.BlockSpec(memory_space=pltpu.VMEM)
    outs = pl.pallas_call(body, in_specs=[vm] * (4 * n), out_specs=[vm] * (3 * n), out_shape=shapes * 3,
                          name="adamw_small")(*ws, *gs, *ms, *vs)
    return outs[:n], outs[n:2 * n], outs[2 * n:]


MESH = pl.DeviceIdType.MESH
ANY = pl.BlockSpec(memory_space=pl.ANY)


def _place():
    x, y, c = lax.axis_index("x"), lax.axis_index("y"), lax.axis_index("c")
    return x, y, c, [(1 - x, y), (x, 1 - y), (1 - x, 1 - y)]


def _chip_sum(grad, recv, core, name):
    _, R, C = grad.shape
    blk, steps, at = _slab_tiles(R, C, rows=512 if R % 512 == 0 else 256, cols=512)

    def body(c_ref, g_ref, r_ref, o_ref):
        o_ref[...] = (g_ref[...].astype(F32) + r_ref[...].astype(F32)).astype(o_ref.dtype)

    grid_spec = pltpu.PrefetchScalarGridSpec(
        num_scalar_prefetch=1, grid=(4, steps),
        in_specs=[pl.BlockSpec((None,) + blk, lambda q, i, c_ref: (2 * q + c_ref[0],) + at(i)),
                  pl.BlockSpec((None,) + blk, lambda q, i, c_ref: (q,) + at(i))],
        out_specs=pl.BlockSpec((None,) + blk, lambda q, i, c_ref: (q,) + at(i)))
    return pl.pallas_call(body, grid_spec=grid_spec, out_shape=jax.ShapeDtypeStruct((4, R, C), BF16),
                          compiler_params=_params(("parallel", "parallel")), name=name)(core, grad, recv)


HBM_SPEC = pl.BlockSpec(memory_space=pltpu.HBM)
SEM_SPEC = pl.BlockSpec(memory_space=pltpu.SEMAPHORE)
DATAFLOW = pltpu.SideEffectType.DATAFLOW_SIDE_EFFECTING


def _split_start(name, bufs, plan, counts, after=None):
    nb, ng = len(bufs), len(counts)
    extra = [] if after is None else [after]
    place = [(g, k) for g, cnt in enumerate(counts) for k in range(cnt)]

    def body(*refs):
        sems, token = refs[nb + len(extra):nb + len(extra) + 2 * ng], refs[-1]
        for (g, k), (src, dst, to) in zip(place, plan(refs[:nb])):
            pltpu.make_async_remote_copy(src_ref=src, dst_ref=dst, send_sem=sems[2 * g].at[k], recv_sem=sems[2 * g + 1].at[k],
                                         device_id=to, device_id_type=MESH).start()
        token[...] = jnp.zeros_like(token)

    outs = pl.pallas_call(
        body, name=name,
        out_shape=(*[pltpu.SemaphoreType.DMA((cnt,)) for cnt in counts for _ in range(2)],
                   *[pltpu.HBM(b.shape, b.dtype) for b in bufs], jax.ShapeDtypeStruct((8, LANE), F32)),
        in_specs=[HBM_SPEC] * nb + [ANY] * len(extra),
        out_specs=(*[SEM_SPEC] * (2 * ng), *[HBM_SPEC] * nb, pl.BlockSpec(memory_space=pltpu.VMEM)),
        input_output_aliases={i: 2 * ng + i for i in range(nb)},
        compiler_params=pltpu.CompilerParams(has_side_effects=DATAFLOW))(
            *[pltpu.with_memory_space_constraint(b, pltpu.HBM) for b in bufs], *extra)
    return [(outs[2 * g], outs[2 * g + 1]) for g in range(ng)], list(outs[2 * ng:2 * ng + nb]), outs[-1]


def _split_wait(name, sems, bufs, plan, after):
    nb = len(bufs)
    send_sems, recv_sems = sems

    def body(*refs):
        send_s, recv_s = refs[nb], refs[nb + 1]
        for k, (src, dst, to) in enumerate(plan(refs[:nb])):
            cp = pltpu.make_async_remote_copy(src_ref=src, dst_ref=dst, send_sem=send_s.at[k], recv_sem=recv_s.at[k],
                                              device_id=to, device_id_type=MESH)
            cp.wait_send()
            cp.wait_recv()

    outs = pl.pallas_call(
        body, name=name, out_shape=tuple(pltpu.HBM(b.shape, b.dtype) for b in bufs),
        in_specs=[HBM_SPEC] * nb + [SEM_SPEC, SEM_SPEC, ANY], out_specs=tuple([HBM_SPEC] * nb),
        input_output_aliases={i: i for i in range(nb)},
        compiler_params=pltpu.CompilerParams(has_side_effects=DATAFLOW))(*bufs, send_sems, recv_sems, after)
    return list(outs)


def _slot(px, py, pc):
    return 4 * px + 2 * py + pc


class _Gather:
    def __init__(self, shards, groups, dev):
        self.shards, self.groups, self.dev = shards, groups, dev
        self.second = {}

    @staticmethod
    def _plan1(pairs, refs):
        x, y, c, chips = _place()
        out = []
        for s, l in pairs:
            dst = refs[l].at[_slot(x, y, c)]
            out.append((refs[s], dst, (x, y, 1 - c)))
            out += [(refs[s], dst, (px, py, c)) for px, py in chips]
        return out

    @staticmethod
    def _plan2(refs):
        x, y, c, chips = _place()
        return [(r.at[_slot(px, py, c)],) * 2 + ((x, y, 1 - c),) for r in refs for px, py in chips]

    def start(self):
        n = len(self.shards)
        lands = [lax.dynamic_update_slice(lax.empty((N_DEV,) + s.shape, s.dtype), s[None], (self.dev, 0, 0))
                 for s in self.shards]
        pairs = [(w, n + w) for g in self.groups for w in g]
        sems, bufs, token = _split_start("gather_start_1", list(self.shards) + lands, functools.partial(self._plan1, pairs),
                                         tuple(4 * len(g) for g in self.groups))
        self.first = [(sems[i], [bufs[w] for w in g], [bufs[n + w] for w in g]) for i, g in enumerate(self.groups)]
        return token

    def mid(self, gi, after):
        sems, srcs, lands = self.first[gi]
        m = len(srcs)
        plan = functools.partial(self._plan1, [(w, m + w) for w in range(m)])
        lands = _split_wait("gather_%d_wait_1" % gi, sems, srcs + lands, plan, after)[m:]
        sems, lands, token = _split_start("gather_%d_start_2" % gi, lands, self._plan2, (3 * m,))
        self.second[gi] = (sems[0], lands)
        return token

    def finish(self, gi, after):
        sems, lands = self.second[gi]
        return _split_wait("gather_%d_wait_2" % gi, sems, lands, self._plan2, after)


class _Exchanges:
    def __init__(self, tag, n, core, gather=None):
        self.tag, self.n, self.core, self.gather = tag, n, core, gather

    def weights_mid(self, group, after):
        return self.gather.mid(group, after)

    def weights_finish(self, group, after):
        return self.gather.finish(group, after)

    def _reduce_plan1(self, refs):
        n = self.n
        x, y, c, _ = _place()
        return [(refs[w].at[2 * q + (1 - c)], refs[n + w].at[q], (x, y, 1 - c)) for w in range(n) for q in range(4)]

    def _reduce_plan2(self, refs):
        n = self.n
        x, y, c, chips = _place()
        return [(refs[w].at[2 * px + py], refs[n + w].at[j], (px, py, c))
                for w in range(n) for j, (px, py) in enumerate(chips)]

    def grads_start(self, grads):
        lands = [lax.empty((4,) + g.shape[1:], g.dtype) for g in grads]
        self.g1 = _split_start(self.tag + "reduce_start_1", list(grads) + lands, self._reduce_plan1, (4 * self.n,))
        return self.g1[2]

    def grads_mid(self, after):
        n = self.n
        sems, bufs, _ = self.g1
        bufs = _split_wait(self.tag + "reduce_wait_1", sems[0], bufs, self._reduce_plan1, after)
        core = self.core.reshape(1).astype(jnp.int32)
        self.parts = [_chip_sum(bufs[w], bufs[n + w], core, self.tag + "reduce_chip_sum_%d" % w) for w in range(n)]
        lands = [lax.empty((3,) + p.shape[1:], p.dtype) for p in self.parts]
        self.g2 = _split_start(self.tag + "reduce_start_2", self.parts + lands, self._reduce_plan2, (3 * n,))
        return self.g2[2]

    def grads_finish(self, after):
        n = self.n
        sems, bufs, _ = self.g2
        bufs = _split_wait(self.tag + "reduce_wait_2", sems[0], bufs, self._reduce_plan2, after)
        self.parts, self.terms = bufs[:n], bufs[n:]


def _all_reduce_small(buf):
    R = buf.shape[0]

    def body(x_ref, o_ref, g_ref, send_sems, recv_sems):
        x, y, c, chips = _place()
        me, sibling = (x, y, c), (x, y, 1 - c)

        def slot(px, py, pc):
            return 4 * px + 2 * py + pc

        def copy(k, block, to, src=None):
            dst = g_ref.at[slot(*block)]
            return pltpu.make_async_remote_copy(src_ref=dst if src is None else src, dst_ref=dst,
                                                send_sem=send_sems.at[k], recv_sem=recv_sems.at[k],
                                                device_id=to, device_id_type=MESH)

        first = [copy(0, me, sibling, src=x_ref)]
        first += [copy(1 + j, me, (*chip, c), src=x_ref) for j, chip in enumerate(chips)]
        for cp in first:
            cp.start()
        g_ref[slot(*me)] = x_ref[...]
        passed = [copy(4 + j, (*chip, c), sibling) for j, chip in enumerate(chips)]
        for j, chip in enumerate(chips):
            copy(1 + j, (*chip, c), me).wait_recv()
            passed[j].start()
        copy(0, sibling, me).wait_recv()
        for j, chip in enumerate(chips):
            copy(4 + j, (*chip, 1 - c), me).wait_recv()
        for cp in first + passed:
            cp.wait_send()
        acc = g_ref[0]
        for s in range(1, N_DEV):
            acc = acc + g_ref[s]
        o_ref[...] = acc

    vm = pl.BlockSpec(memory_space=pltpu.VMEM)
    return pl.pallas_call(
        body, in_specs=[vm], out_specs=vm, out_shape=jax.ShapeDtypeStruct((R, LANE), F32),
        scratch_shapes=[pltpu.VMEM((N_DEV, R, LANE), F32), pltpu.SemaphoreType.DMA((7,)), pltpu.SemaphoreType.DMA((7,))],
        name="all_reduce_small")(buf)


def _pad_cols(a, width):
    return jnp.pad(a, ((0, 0), (0, width - a.shape[1])))


def _local_step(x, target, w_in_g, conv_w, a_log, dt_bias, delta_norm_w, sinks, rel_bias, ln1_g, ln1_b, ln2_g, ln2_b,
                ex, ex_in):
    S, D = x.shape
    aw = D // 2
    hq, hkv, nh = aw // HEAD_A, aw // HEAD_A // GQA, aw // HEAD_D
    kvw = hkv * HEAD_A
    c_q, c_k, c_v, c_d = 0, aw, aw + kvw, aw + 2 * kvw
    c_ab = c_d + 3 * aw
    c_z = c_ab + 2 * nh
    n_in = c_z + aw
    n_slab = w_in_g.shape[1]
    assert w_in_g.shape == (N_DEV, n_in // N_DEV, D), (w_in_g.shape, n_in)

    def rows(lo, hi):
        out = []
        while lo < hi:
            g = lo // n_slab
            end = min(hi, (g + 1) * n_slab)
            out.append(w_in_g[g, lo - g * n_slab:end - g * n_slab])
            lo = end
        return out

    w_pt = jnp.concatenate(rows(0, c_ab) + rows(c_z, n_in) + rows(c_ab, c_z) + [jnp.zeros((LANE - 2 * nh, D), BF16)], axis=0)
    p_z, p_ab = c_ab, c_ab + aw
    n_p = p_ab + LANE

    xb = x.astype(BF16)
    proj = _matmul(xb, w_pt, NT, name="proj", tn=1152)
    bias = _attn_bias(rel_bias.T)
    attn_out = _attn_fwd(proj, bias, sinks.reshape(-1), hq, 0, c_k // kvw, c_v // kvw, D)
    conv2 = conv_w.reshape(CONV_W, 3 * aw)
    qkv = _gdn_prep_fwd(proj, conv2, nh, c_d // HEAD_D)
    ab = proj[:, p_ab:]
    al, dt = _pad_cols(a_log, LANE), _pad_cols(dt_bias, LANE)
    gb = _gates_fwd(ab, al, dt, nh)
    u_d, wqk, attn_d, t_d = _gdn_local_fwd(qkv, gb, nh)
    o_d, vn, states = _gdn_scan_fwd(u_d, wqk, attn_d, gb, nh, ex.weights_mid(1, u_d))
    mix = _gated_norm_fwd(o_d, proj, delta_norm_w, nh, p_z // HEAD_D, attn_out, aw // HEAD_D)
    w_o_g, w_up_g = ex.weights_finish(1, mix)
    w_o = w_o_g.reshape(D, D)
    mixed = _matmul(mix, w_o, NN, name="out_proj")
    h1 = _ln1_fwd(x, mixed, ln1_g, ln1_b)
    u = _matmul(h1, w_up_g, NN, name="mlp_up", b_groups=True, out_dtype=BF16, deps=(ex.weights_mid(2, h1),))
    (w_down_g,) = ex.weights_finish(2, u)
    w_down = w_down_g.reshape(-1, D)
    mlp = _matmul(u, w_down, NN, name="mlp_down", a_fn=_relu_sq, tk=4096)
    dr2, loss_row, dln2_g, dln2_b = _ln2_loss(h1, mlp, ln2_g, ln2_b, target)

    du = _matmul(dr2, w_down, NT, name="d_mlp_act", epi=_relu_sq_grad, epi_in=(u,), out_dtype=BF16)
    dw_down = _matmul(u, dr2, TN, name="dw_down", a_fn=_relu_sq, out_dtype=BF16)
    dw_up = _matmul(h1, du, TN, name="dw_up", out_dtype=BF16, out_groups=N_DEV)
    dh_mlp = _matmul(du, w_up_g, NT, name="d_h1", b_groups=True, tk=4096)
    dr1, dln1_g, dln1_b = _ln1_bwd(x, mixed, ln1_g, dr2, dh_mlp)
    dw_o = _matmul(mix, dr1, TN, name="dw_o", out_dtype=BF16)
    tok = ex.grads_start([dw_o.reshape(N_DEV, -1, D), dw_up, dw_down.reshape(N_DEV, -1, D)])
    dmix = _matmul(dr1, w_o, NT, name="d_mix", deps=(tok,))
    dq_a, dk_a, dv_a, dbias, dsink = _attn_bwd(proj, bias, sinks.reshape(-1), mix, dmix, hq, 0, c_k // kvw, c_v // kvw)
    drel = _rel_bias_grad(dbias, hq)
    do_d, dz, dnw = _gated_norm_bwd(o_d, proj, delta_norm_w, dmix, nh, p_z // HEAD_D, aw // HEAD_D)
    dvn_s, dw_s, dqd, dkd, dattn_d, dgl = _gdn_scan_bwd(wqk, attn_d, gb, states, vn, do_d, nh, ex.grads_mid(dq_a))
    dqkv_n, dg, dbeta = _gdn_local_bwd(qkv, gb, t_d, u_d, wqk, dvn_s, dw_s, dqd, dkd, dattn_d, dgl, nh)
    dgb = _pad_cols(jnp.concatenate([dg.reshape(nh, S).T, dbeta.reshape(nh, S).T], axis=1), LANE)
    dab, da_log, ddt_bias = _gates_bwd(ab, al, dt, dgb, nh)
    dqkv_d, dconv = _gdn_prep_bwd(proj, conv2, dqkv_n, nh, c_d // HEAD_D)
    dproj = jnp.concatenate([dq_a, dk_a, dv_a, dqkv_d, dz, dab], axis=1)
    dw_pt = _matmul(dproj, xb, TN, name="dw_in", out_dtype=BF16, tm=1152)

    def grad_rows(lo, hi):
        cuts = sorted({lo, hi, *[c for c in (c_ab, c_z) if lo < c < hi]})
        place = lambda r: r if r < c_ab else (p_ab + r - c_ab if r < c_z else p_z + r - c_z)
        return [dw_pt[place(a):place(a) + b - a] for a, b in zip(cuts[:-1], cuts[1:])]

    dw_in_g = jnp.stack([jnp.concatenate(grad_rows(g * n_slab, (g + 1) * n_slab), axis=0) for g in range(N_DEV)])
    tok = ex_in.grads_mid(ex_in.grads_start([dw_in_g]))
    grad_x = _matmul(dproj, w_pt, NN, name="d_x", tk=1920, deps=(tok,), epi=_residual_grad, epi_in=(dr1,))
    ex.grads_finish(grad_x)

    small = dict(conv_w=dconv, a_log=da_log[:, :nh], dt_bias=ddt_bias[:, :nh], delta_norm_w=dnw,
                 attn_sinks=dsink[:, 0].reshape(1, hq), rel_bias=drel[:, :, 0].T,
                 ln1_g=dln1_g, ln1_b=dln1_b, ln2_g=dln2_g, ln2_b=dln2_b)
    return loss_row, grad_x, small


SMALL_ORDER = ("conv_w", "a_log", "dt_bias", "delta_norm_w", "attn_sinks", "rel_bias", "ln1_g", "ln1_b", "ln2_g", "ln2_b")


def _pack_small(loss_row, small):
    parts = [loss_row.reshape(-1)]
    for k in SMALL_ORDER:
        flat = small[k].reshape(-1)
        parts.append(jnp.pad(flat, (0, (-flat.shape[0]) % LANE)))
    flat = jnp.concatenate(parts)
    flat = jnp.pad(flat, (0, (-flat.shape[0]) % (8 * LANE)))
    return flat.reshape(-1, LANE)


def _unpack_small(buf, small_shapes):
    flat = buf.reshape(-1)
    loss = flat[0]
    off = LANE
    out = {}
    for k in SMALL_ORDER:
        n = int(np.prod(small_shapes[k]))
        out[k] = flat[off:off + n].reshape(small_shapes[k])
        off += n + (-n) % LANE
    return loss, out


def kernel(x, w_in, conv_w, a_log, dt_bias, delta_norm_w, attn_sinks, rel_bias, w_o, ln1_g, ln1_b, w_up, w_down, ln2_g, ln2_b, loss_target, m_w_in, m_conv_w, m_a_log, m_dt_bias, m_delta_norm_w, m_attn_sinks, m_rel_bias, m_w_o, m_ln1_g, m_ln1_b, m_w_up, m_w_down, m_ln2_g, m_ln2_b, v_w_in, v_conv_w, v_a_log, v_dt_bias, v_delta_norm_w, v_attn_sinks, v_rel_bias, v_w_o, v_ln1_g, v_ln1_b, v_w_up, v_w_down, v_ln2_g, v_ln2_b):
    S, D = x.shape[1], x.shape[2]
    core = lax.axis_index("c")
    dev = 4 * lax.axis_index("x") + 2 * lax.axis_index("y") + core

    gather = _Gather([conv_w[0, :, 0, :], w_in[0].T.astype(BF16), w_o[0].astype(BF16), w_up[0].astype(BF16),
                      w_down[0].astype(BF16)], [[0, 1], [2, 3], [4]], dev)
    conv_g, w_in_g = gather.finish(0, gather.mid(0, gather.start()))
    ex = _Exchanges("", 3, core, gather)
    ex_in = _Exchanges("in_", 1, core)

    cw_sh = conv_w.shape[3]
    conv_full = jnp.transpose(conv_g, (1, 0, 2)).reshape(CONV_W, N_DEV * cw_sh)

    loss_row, grad_x, small = _local_step(
        x[0], loss_target[0], w_in_g, conv_full, a_log, dt_bias, delta_norm_w, attn_sinks, rel_bias,
        ln1_g, ln1_b, ln2_g, ln2_b, ex, ex_in)

    chip_arr = (dev // 2).reshape(1).astype(jnp.int32)
    big = {}
    for i, (name, w, m, v) in enumerate((("w_o", w_o, m_w_o, v_w_o), ("w_up", w_up, m_w_up, v_w_up),
                                         ("w_down", w_down, m_w_down, v_w_down))):
        big[name] = [o[None] for o in _adamw_big(ex.parts[i], ex.terms[i], chip_arr, w[0], m[0], v[0], "adamw_" + name)]
    ex_in.grads_finish(big["w_down"][0])
    outs = _adamw_big(ex_in.parts[0], ex_in.terms[0], chip_arr, w_in[0].T, m_w_in[0].T, v_w_in[0].T, "adamw_w_in")
    big["w_in"] = [o.T[None] for o in outs]

    small_shapes = {k: v.shape for k, v in small.items()}
    loss, small = _unpack_small(_all_reduce_small(_pack_small(loss_row, small)), small_shapes)
    small["conv_w"] = lax.dynamic_slice(small["conv_w"], (0, dev * cw_sh), (CONV_W, cw_sh))
    small["rel_bias"] = small["rel_bias"].reshape(rel_bias.shape)
    p2 = dict(conv_w=(conv_w, m_conv_w, v_conv_w), a_log=(a_log, m_a_log, v_a_log), dt_bias=(dt_bias, m_dt_bias, v_dt_bias),
              delta_norm_w=(delta_norm_w, m_delta_norm_w, v_delta_norm_w), attn_sinks=(attn_sinks, m_attn_sinks, v_attn_sinks),
              rel_bias=(rel_bias, m_rel_bias, v_rel_bias), ln1_g=(ln1_g, m_ln1_g, v_ln1_g), ln1_b=(ln1_b, m_ln1_b, v_ln1_b),
              ln2_g=(ln2_g, m_ln2_g, v_ln2_g), ln2_b=(ln2_b, m_ln2_b, v_ln2_b))
    two_d = lambda a: a.reshape(-1, a.shape[-1])
    ws = [two_d(p2[k][0]) for k in SMALL_ORDER]
    gs = [two_d(small[k]) for k in SMALL_ORDER]
    ms = [two_d(p2[k][1]) for k in SMALL_ORDER]
    vs = [two_d(p2[k][2]) for k in SMALL_ORDER]
    ds, nms, nvs = _adamw_small(ws, gs, ms, vs)
    res = {}
    for i, k in enumerate(SMALL_ORDER):
        shp = p2[k][0].shape
        res[k] = [gs[i].reshape(shp), ds[i].reshape(shp), nms[i].reshape(shp), nvs[i].reshape(shp)]
    res.update(big)
    order = ("w_in", "conv_w", "a_log", "dt_bias", "delta_norm_w", "attn_sinks", "rel_bias", "w_o", "ln1_g", "ln1_b",
             "w_up", "w_down", "ln2_g", "ln2_b")
    return (loss, grad_x[None], *[res[k][0] for k in order], *[res[k][1] for k in order],
            *[res[k][2] for k in order], *[res[k][3] for k in order])
```

```python
import functools
import math

import numpy as np
import jax
import jax.numpy as jnp
from jax import lax
from jax.experimental import pallas as pl
from jax.experimental.pallas import tpu as pltpu

F32 = jnp.float32
BF16 = jnp.bfloat16
HIGHEST = lax.Precision.HIGHEST

N_DEV = 8
HEAD_A = 64
GQA = 4
BLK = 128
N_BUCKETS = 32
MAX_DISTANCE = 128
HEAD_D = 128
CONV_W = 4
CHUNK = 64
NEG_INF = -1e30
LN_EPS = 1e-5
RMS_EPS = 1e-6
DN_ALPHA = 2.0 ** 0.25
ADAM_LR, ADAM_B1, ADAM_B2, ADAM_EPS, ADAM_WD, ADAM_STEP = 0.001, 0.9, 0.999, 1e-08, 0.01, 10

LANE = 128
VMEM_LIMIT = 56 * 1024 * 1024

NN = ((1,), (0,))
NT = ((1,), (1,))
TN = ((0,), (0,))


def _dot(a, b, dims, prec=None):
    return lax.dot_general(a, b, (dims, ((), ())), precision=prec, preferred_element_type=F32)


def _tile(dim, pref):
    if dim <= pref:
        return dim
    t = (pref // LANE) * LANE
    while t > LANE and dim % t:
        t -= LANE
    assert dim % t == 0, (dim, pref)
    return t


def _params(sem):
    return pltpu.CompilerParams(dimension_semantics=sem, vmem_limit_bytes=VMEM_LIMIT)


def _matmul(a, b, dims, *, name, out_dtype=F32, tm=1024, tn=1024, tk=2048, a_fn=None, epi=None, epi_in=(),
            b_groups=None, out_groups=None, deps=()):
    (ca,), (cb,) = dims
    M, K = a.shape[1 - ca], a.shape[ca]
    if b_groups:
        G, R, C = b.shape
        bshape = (R, G * C)
    else:
        bshape = b.shape
    N = bshape[1 - cb]
    assert bshape[cb] == K, (a.shape, b.shape, dims)
    tm, tk = _tile(M, tm), _tile(K, tk)
    if b_groups:
        lim = C if cb == 0 else tn
        tn = _tile(N, min(tn, lim))
        if cb == 1 and tk < C:
            tk = _tile(K, min(tk, C))
        elif cb == 1:
            tk = C * max(1, tk // C)
    else:
        tn = _tile(N, tn)
    if out_groups:
        tn = _tile(N, min(tn, N // out_groups))
    nk = K // tk
    b_span = tk // C if (b_groups and cb == 1 and tk > C) else 1

    def body(*refs):
        a_ref, b_ref = refs[0], refs[1]
        e_refs = refs[2:2 + len(epi_in)]
        o_ref = refs[2 + len(epi_in) + len(deps)]
        acc_ref = refs[3 + len(epi_in) + len(deps)] if nk > 1 else None
        k = pl.program_id(2)
        av = a_ref[...]
        if a_fn is not None:
            av = a_fn(av)
        if b_span > 1:
            prod = sum(_dot(av[:, g * C:(g + 1) * C].astype(BF16), b_ref[g].astype(BF16), dims) for g in range(b_span))
        else:
            prod = _dot(av.astype(BF16), b_ref[...].astype(BF16), dims)

        def finish(r):
            if epi is not None:
                r = epi(r, *[e[...] for e in e_refs])
            o_ref[...] = r.astype(out_dtype)

        if nk == 1:
            finish(prod)
            return

        @pl.when(k == 0)
        def _():
            acc_ref[...] = prod

        @pl.when(k > 0)
        def _():
            acc_ref[...] += prod

        @pl.when(k == nk - 1)
        def _():
            finish(acc_ref[...])

    a_spec = (pl.BlockSpec((tm, tk), lambda i, j, k: (i, k)) if ca == 1
              else pl.BlockSpec((tk, tm), lambda i, j, k: (k, i)))
    if b_groups:
        if cb == 0:
            per = C // tn
            b_spec = pl.BlockSpec((None, tk, tn), lambda i, j, k: (j // per, k, j % per))
        elif b_span > 1:
            b_spec = pl.BlockSpec((b_span, tn, C), lambda i, j, k: (k, j, 0))
        else:
            per = C // tk
            b_spec = pl.BlockSpec((None, tn, tk), lambda i, j, k: (k // per, j, k % per))
    else:
        b_spec = (pl.BlockSpec((tk, tn), lambda i, j, k: (k, j)) if cb == 0
                  else pl.BlockSpec((tn, tk), lambda i, j, k: (j, k)))
    e_specs = [pl.BlockSpec((tm, tn), lambda i, j, k: (i, j)) for _ in epi_in]
    if out_groups:
        per_o = (N // out_groups) // tn
        o_spec = pl.BlockSpec((None, tm, tn), lambda i, j, k: (j // per_o, i, j % per_o))
        o_shape = jax.ShapeDtypeStruct((out_groups, M, N // out_groups), out_dtype)
    else:
        o_spec = pl.BlockSpec((tm, tn), lambda i, j, k: (i, j))
        o_shape = jax.ShapeDtypeStruct((M, N), out_dtype)
    return pl.pallas_call(
        body, grid=(M // tm, N // tn, nk), out_specs=o_spec,
        in_specs=[a_spec, b_spec] + e_specs + [pl.BlockSpec(memory_space=pl.ANY)] * len(deps),
        out_shape=o_shape, scratch_shapes=[pltpu.VMEM((tm, tn), F32)] if nk > 1 else [],
        compiler_params=_params(("parallel", "parallel", "arbitrary")), name=name)(a, b, *epi_in, *deps)


def _relu_sq(u):
    r = jnp.maximum(u, 0.0)
    return r * r


def _relu_sq_grad(acc, u):
    return acc * (2.0 * jnp.maximum(u, 0.0))


def _ln_stats(r):
    mu = jnp.mean(r, axis=-1, keepdims=True)
    xc = r - mu
    var = jnp.mean(xc * xc, axis=-1, keepdims=True)
    rstd = lax.rsqrt(var + LN_EPS)
    return xc * rstd, rstd


def _ln_bwd(dy, xhat, rstd, g):
    dxh = dy * g
    m1 = jnp.mean(dxh, axis=-1, keepdims=True)
    m2 = jnp.mean(dxh * xhat, axis=-1, keepdims=True)
    return rstd * (dxh - m1 - xhat * m2)


def _row_call(body, ins, row_ins, outs, acc_outs, name, tr=256):
    S = ins[0].shape[0]
    tr = min(tr, S)
    n_in, n_row, n_out = len(ins), len(row_ins), len(outs)

    def wrapped(*refs):
        i = pl.program_id(0)
        acc_refs = refs[n_in + n_row + n_out:]

        @pl.when(i == 0)
        def _():
            for r in acc_refs:
                r[...] = jnp.zeros_like(r)

        body(*refs)

    in_specs = [pl.BlockSpec((tr, a.shape[1]), lambda i: (i, 0)) for a in ins]
    in_specs += [pl.BlockSpec(a.shape, lambda i: (0, 0)) for a in row_ins]
    out_specs = [pl.BlockSpec((tr, s.shape[1]), lambda i: (i, 0)) for s in outs]
    out_specs += [pl.BlockSpec(s.shape, lambda i: (0, 0)) for s in acc_outs]
    return pl.pallas_call(wrapped, grid=(S // tr,), in_specs=in_specs, out_specs=out_specs,
                          out_shape=list(outs) + list(acc_outs),
                          compiler_params=_params(("arbitrary",)), name=name)(*ins, *row_ins)


def _ln1_fwd(x, mixed, g, b):
    def body(x_ref, m_ref, g_ref, b_ref, h_ref):
        xhat, _ = _ln_stats(DN_ALPHA * x_ref[...] + m_ref[...])
        h_ref[...] = xhat * g_ref[...] + b_ref[...]
    return _row_call(body, [x, mixed], [g, b], [jax.ShapeDtypeStruct(x.shape, F32)], [], "ln1_fwd")[0]


def _ln2_loss(h1, mlp, g, b, target):
    S, D = h1.shape
    sds = jax.ShapeDtypeStruct

    def body(h_ref, m_ref, t_ref, g_ref, b_ref, dr_ref, loss_ref, dg_ref, db_ref):
        xhat, rstd = _ln_stats(DN_ALPHA * h_ref[...] + m_ref[...])
        gv = g_ref[...]
        err = xhat * gv + b_ref[...] - t_ref[...]
        loss_ref[...] += jnp.sum(jnp.sum(err * err, axis=0, keepdims=True), axis=1, keepdims=True) * (0.5 / D)
        dy = err * (1.0 / D)
        dg_ref[...] += jnp.sum(dy * xhat, axis=0, keepdims=True)
        db_ref[...] += jnp.sum(dy, axis=0, keepdims=True)
        dr_ref[...] = _ln_bwd(dy, xhat, rstd, gv)

    return _row_call(body, [h1, mlp, target], [g, b], [sds((S, D), F32)],
                     [sds((1, LANE), F32), sds((1, D), F32), sds((1, D), F32)], "ln2_loss")


def _ln1_bwd(x, mixed, g, dr2, dh_mlp):
    S, D = x.shape
    sds = jax.ShapeDtypeStruct

    def body(x_ref, m_ref, dr2_ref, dh_ref, g_ref, dr_ref, dg_ref, db_ref):
        xhat, rstd = _ln_stats(DN_ALPHA * x_ref[...] + m_ref[...])
        dy = DN_ALPHA * dr2_ref[...] + dh_ref[...]
        dg_ref[...] += jnp.sum(dy * xhat, axis=0, keepdims=True)
        db_ref[...] += jnp.sum(dy, axis=0, keepdims=True)
        dr_ref[...] = _ln_bwd(dy, xhat, rstd, g_ref[...])

    return _row_call(body, [x, mixed, dr2, dh_mlp], [g], [sds((S, D), F32)],
                     [sds((1, D), F32), sds((1, D), F32)], "ln1_bwd")


def _residual_grad(acc, dr):
    return DN_ALPHA * dr + acc


def _bucket_table():
    qi = np.arange(BLK, dtype=np.int32)[:, None]
    kj = np.arange(2 * BLK, dtype=np.int32)[None, :]
    dist = qi + BLK - kj
    n = np.maximum(dist, 0)
    max_exact = N_BUCKETS // 2
    nf = np.maximum(n, 1).astype(np.float32)
    large = max_exact + (np.log(nf / np.float32(max_exact)) / np.float32(math.log(MAX_DISTANCE / max_exact))
                         * np.float32(N_BUCKETS - max_exact)).astype(np.int32)
    large = np.minimum(large, N_BUCKETS - 1)
    bucket = np.where(n < max_exact, n, large)
    return np.where((dist >= 0) & (dist < BLK), bucket, -1).astype(np.int32)


def _attn_bias(rel_bias_t):
    hq = rel_bias_t.shape[0]
    bucket = jnp.asarray(_bucket_table())

    def body(rb_ref, bk_ref, o_ref):
        h = pl.program_id(0)
        bk = bk_ref[...]
        acc = jnp.zeros((BLK, 2 * BLK), F32)
        for b in range(N_BUCKETS):
            acc = jnp.where(bk == b, rb_ref[h, b], acc)
        o_ref[...] = acc

    return pl.pallas_call(
        body, grid=(hq,),
        in_specs=[pl.BlockSpec(memory_space=pltpu.SMEM), pl.BlockSpec((BLK, 2 * BLK), lambda h: (0, 0))],
        out_specs=pl.BlockSpec((BLK, 2 * BLK), lambda h: (h, 0)),
        out_shape=jax.ShapeDtypeStruct((hq * BLK, 2 * BLK), F32),
        compiler_params=_params(("arbitrary",)), name="attn_bias")(rel_bias_t, bucket)


def _attn_probs(sc, sp, bias, sink, mask_c, mask_p):
    lc = jnp.where(mask_c, sc + bias[:, BLK:], NEG_INF)
    lp = jnp.where(mask_p, sp + bias[:, :BLK], NEG_INF)
    m = jnp.maximum(jnp.maximum(jnp.max(lc, axis=1, keepdims=True), jnp.max(lp, axis=1, keepdims=True)), sink)
    pc, pp, ps = jnp.exp(lc - m), jnp.exp(lp - m), jnp.exp(sink - m)
    inv = 1.0 / (jnp.sum(pc, axis=1, keepdims=True) + jnp.sum(pp, axis=1, keepdims=True) + ps)
    return pc, pp, ps, inv


def _attn_masks(n):
    qi = lax.broadcasted_iota(jnp.int32, (BLK, BLK), 0)
    kj = lax.broadcasted_iota(jnp.int32, (BLK, BLK), 1)
    return kj <= qi, (kj > qi) & (n > 0)


def _attn_fwd(proj, bias, sinks, hq, q_blk, k_blk, v_blk, out_width):
    S = proj.shape[0]
    hkv = hq // GQA
    wq, wk = hq * HEAD_A, hkv * HEAD_A

    def body(q_ref, k_ref, v_ref, bias_ref, sink_ref, o_ref):
        n = pl.program_id(0)
        cur = pl.multiple_of(n * BLK, BLK)
        prev = pl.multiple_of(jnp.maximum(n - 1, 0) * BLK, BLK)
        mask_c, mask_p = _attn_masks(n)
        for h4 in range(hkv):
            cs = slice(h4 * HEAD_A, (h4 + 1) * HEAD_A)
            kc, kp = k_ref[pl.ds(cur, BLK), cs].astype(BF16), k_ref[pl.ds(prev, BLK), cs].astype(BF16)
            vc, vp = v_ref[pl.ds(cur, BLK), cs].astype(BF16), v_ref[pl.ds(prev, BLK), cs].astype(BF16)
            hs_of = [slice(h * HEAD_A, (h + 1) * HEAD_A) for h in range(h4 * GQA, (h4 + 1) * GQA)]
            qs = [(q_ref[:, hs] * (HEAD_A ** -0.5)).astype(BF16) for hs in hs_of]
            scs = [_dot(q, kc, NT) for q in qs]
            sps = [_dot(q, kp, NT) for q in qs]
            pr = [_attn_probs(scs[g], sps[g], bias_ref[(h4 * GQA + g) * BLK:(h4 * GQA + g + 1) * BLK, :],
                              sink_ref[h4 * GQA + g], mask_c, mask_p) for g in range(GQA)]
            oc = [_dot(p[0].astype(BF16), vc, NN) for p in pr]
            op = [_dot(p[1].astype(BF16), vp, NN) for p in pr]
            for g, hs in enumerate(hs_of):
                o_ref[:, hs] = (oc[g] + op[g]) * pr[g][3]

    return pl.pallas_call(
        body, grid=(S // BLK,),
        in_specs=[pl.BlockSpec((BLK, wq), lambda n: (n, q_blk)), pl.BlockSpec((S, wk), lambda n: (0, k_blk)),
                  pl.BlockSpec((S, wk), lambda n: (0, v_blk)), pl.BlockSpec((hq * BLK, 2 * BLK), lambda n: (0, 0)),
                  pl.BlockSpec(memory_space=pltpu.SMEM)],
        out_specs=pl.BlockSpec((BLK, wq), lambda n: (n, 0)),
        out_shape=jax.ShapeDtypeStruct((S, out_width), F32),
        compiler_params=_params(("arbitrary",)), name="attn_fwd")(proj, proj, proj, bias, sinks)


def _attn_bwd(proj, bias, sinks, out, dmix, hq, q_blk, k_blk, v_blk):
    S = proj.shape[0]
    hkv = hq // GQA
    wq, wk = hq * HEAD_A, hkv * HEAD_A
    sds = jax.ShapeDtypeStruct

    def body(q_ref, k_ref, v_ref, bias_ref, sink_ref, o_ref, do_ref, dq_ref, dk_ref, dv_ref, dbias_ref, dsink_ref):
        n = pl.program_id(0)

        @pl.when(n == 0)
        def _():
            dk_ref[...] = jnp.zeros_like(dk_ref)
            dv_ref[...] = jnp.zeros_like(dv_ref)
            dbias_ref[...] = jnp.zeros_like(dbias_ref)
            dsink_ref[...] = jnp.zeros_like(dsink_ref)

        cur = pl.multiple_of(n * BLK, BLK)
        prev = pl.multiple_of(jnp.maximum(n - 1, 0) * BLK, BLK)
        mask_c, mask_p = _attn_masks(n)
        for h4 in range(hkv):
            cs = slice(h4 * HEAD_A, (h4 + 1) * HEAD_A)
            kc, kp = k_ref[pl.ds(cur, BLK), cs].astype(BF16), k_ref[pl.ds(prev, BLK), cs].astype(BF16)
            vc, vp = v_ref[pl.ds(cur, BLK), cs].astype(BF16), v_ref[pl.ds(prev, BLK), cs].astype(BF16)
            heads = list(range(h4 * GQA, (h4 + 1) * GQA))
            hs_of = [slice(h * HEAD_A, (h + 1) * HEAD_A) for h in heads]
            rows_of = [slice(h * BLK, (h + 1) * BLK) for h in heads]
            G = range(GQA)
            qs = [(q_ref[:, hs] * (HEAD_A ** -0.5)).astype(BF16) for hs in hs_of]
            dos = [do_ref[:, hs] for hs in hs_of]
            dobs = [d.astype(BF16) for d in dos]
            scs = [_dot(q, kc, NT) for q in qs]
            sps = [_dot(q, kp, NT) for q in qs]
            dpc = [_dot(d, vc, NT) for d in dobs]
            dpp = [_dot(d, vp, NT) for d in dobs]
            pcs, pps, dscs, dsps = [], [], [], []
            for g in G:
                pc, pp, ps, inv = _attn_probs(scs[g], sps[g], bias_ref[rows_of[g], :], sink_ref[heads[g]], mask_c, mask_p)
                pc, pp, ps = pc * inv, pp * inv, ps * inv
                delta = jnp.sum(dos[g] * o_ref[:, hs_of[g]], axis=1, keepdims=True)
                dsc, dsp = pc * (dpc[g] - delta), pp * (dpp[g] - delta)
                dsink_ref[heads[g]:heads[g] + 1, :] += jnp.broadcast_to(jnp.sum(-ps * delta, axis=0, keepdims=True), (1, LANE))
                dbias_ref[rows_of[g], BLK:] += dsc
                dbias_ref[rows_of[g], :BLK] += dsp
                pcs.append(pc.astype(BF16))
                pps.append(pp.astype(BF16))
                dscs.append(dsc.astype(BF16))
                dsps.append(dsp.astype(BF16))
            dq1 = [_dot(dscs[g], kc, NN) for g in G]
            dq2 = [_dot(dsps[g], kp, NN) for g in G]
            dkc = [_dot(dscs[g], qs[g], TN) for g in G]
            dkp = [_dot(dsps[g], qs[g], TN) for g in G]
            dvc = [_dot(pcs[g], dobs[g], TN) for g in G]
            dvp = [_dot(pps[g], dobs[g], TN) for g in G]
            for g in G:
                dq_ref[:, hs_of[g]] = (dq1[g] + dq2[g]) * (HEAD_A ** -0.5)
            dk_ref[pl.ds(cur, BLK), cs] += sum(dkc[1:], dkc[0])
            dk_ref[pl.ds(prev, BLK), cs] += sum(dkp[1:], dkp[0])
            dv_ref[pl.ds(cur, BLK), cs] += sum(dvc[1:], dvc[0])
            dv_ref[pl.ds(prev, BLK), cs] += sum(dvp[1:], dvp[0])

    return pl.pallas_call(
        body, grid=(S // BLK,),
        in_specs=[pl.BlockSpec((BLK, wq), lambda n: (n, q_blk)), pl.BlockSpec((S, wk), lambda n: (0, k_blk)),
                  pl.BlockSpec((S, wk), lambda n: (0, v_blk)), pl.BlockSpec((hq * BLK, 2 * BLK), lambda n: (0, 0)),
                  pl.BlockSpec(memory_space=pltpu.SMEM),
                  pl.BlockSpec((BLK, wq), lambda n: (n, 0)), pl.BlockSpec((BLK, wq), lambda n: (n, 0))],
        out_specs=[pl.BlockSpec((BLK, wq), lambda n: (n, 0)), pl.BlockSpec((S, wk), lambda n: (0, 0)),
                   pl.BlockSpec((S, wk), lambda n: (0, 0)), pl.BlockSpec((hq * BLK, 2 * BLK), lambda n: (0, 0)),
                   pl.BlockSpec((hq, LANE), lambda n: (0, 0))],
        out_shape=[sds((S, wq), F32), sds((S, wk), F32), sds((S, wk), F32), sds((hq * BLK, 2 * BLK), F32),
                   sds((hq, LANE), F32)],
        compiler_params=_params(("arbitrary",)), name="attn_bwd")(proj, proj, proj, bias, sinks, out, dmix)


def _rel_bias_grad(dbias, hq):
    bucket = jnp.asarray(_bucket_table())

    def body(d_ref, bk_ref, o_ref):
        d = d_ref[...]
        bk = bk_ref[...]
        rows = [jnp.sum(jnp.where(bk == b, d, 0.0), axis=0, keepdims=True) for b in range(N_BUCKETS)]
        tot = jnp.sum(jnp.concatenate(rows, axis=0), axis=1, keepdims=True)
        o_ref[...] = jnp.broadcast_to(tot, (N_BUCKETS, LANE))

    return pl.pallas_call(
        body, grid=(hq,),
        in_specs=[pl.BlockSpec((BLK, 2 * BLK), lambda h: (h, 0)), pl.BlockSpec((BLK, 2 * BLK), lambda h: (0, 0))],
        out_specs=pl.BlockSpec((None, N_BUCKETS, LANE), lambda h: (h, 0, 0)),
        out_shape=jax.ShapeDtypeStruct((hq, N_BUCKETS, LANE), F32),
        compiler_params=_params(("arbitrary",)), name="rel_bias_grad")(dbias, bucket)


def _sigmoid(x):
    return 1.0 / (1.0 + jnp.exp(-x))


def _shift_rows(x, s):
    n = x.shape[0]
    row = lax.broadcasted_iota(jnp.int32, x.shape, 0)
    if s > 0:
        return jnp.where(row >= s, pltpu.roll(x, s, 0), 0.0)
    return jnp.where(row < n + s, pltpu.roll(x, n + s, 0), 0.0)


def _conv_silu_norm(xv, w, j, nh):
    c = w[CONV_W - 1:CONV_W, :] * xv
    for s in range(1, CONV_W):
        c = c + w[CONV_W - 1 - s:CONV_W - s, :] * _shift_rows(xv, s)
    sg = _sigmoid(c)
    a = c * sg
    r = lax.rsqrt(jnp.sum(a * a, axis=1, keepdims=True) + RMS_EPS)
    scale = jnp.where(j < nh, HEAD_D ** -0.5, 1.0)
    is_norm = j < 2 * nh
    y = jnp.where(is_norm, a * (r * scale), a)
    return c, sg, a, r, scale, is_norm, y


def _gdn_prep_fwd(proj, conv_w, nh, blk0):
    S = proj.shape[0]

    def body(x_ref, w_ref, o_ref):
        j = pl.program_id(0)
        o_ref[...] = _conv_silu_norm(x_ref[...], w_ref[...], j, nh)[-1]

    return pl.pallas_call(
        body, grid=(3 * nh,),
        in_specs=[pl.BlockSpec((S, HEAD_D), lambda j: (0, blk0 + j)), pl.BlockSpec((CONV_W, HEAD_D), lambda j: (0, j))],
        out_specs=pl.BlockSpec((S, HEAD_D), lambda j: (0, 3 * (j % nh) + j // nh)),
        out_shape=jax.ShapeDtypeStruct((S, 3 * nh * HEAD_D), F32),
        compiler_params=_params(("parallel",)), name="gdn_prep_fwd")(proj, conv_w)


def _gdn_prep_bwd(proj, conv_w, dqkv, nh, blk0):
    S = proj.shape[0]
    sds = jax.ShapeDtypeStruct

    def body(x_ref, w_ref, dy_ref, dx_ref, dw_ref):
        j = pl.program_id(0)
        xv, w = x_ref[...], w_ref[...]
        c, sg, a, r, scale, is_norm, _ = _conv_silu_norm(xv, w, j, nh)
        dy = dy_ref[...]
        rs = r * scale
        da_n = rs * dy - a * (r * r * rs) * jnp.sum(dy * a, axis=1, keepdims=True)
        da = jnp.where(is_norm, da_n, dy)
        dc = da * (sg * (1.0 + c * (1.0 - sg)))
        dx = w[CONV_W - 1:CONV_W, :] * dc
        dws = [jnp.sum(dc * xv, axis=0, keepdims=True)]
        for s in range(1, CONV_W):
            dx = dx + w[CONV_W - 1 - s:CONV_W - s, :] * _shift_rows(dc, -s)
            dws.insert(0, jnp.sum(dc * _shift_rows(xv, s), axis=0, keepdims=True))
        dx_ref[...] = dx
        dw_ref[...] = jnp.concatenate(dws, axis=0)

    return pl.pallas_call(
        body, grid=(3 * nh,),
        in_specs=[pl.BlockSpec((S, HEAD_D), lambda j: (0, blk0 + j)), pl.BlockSpec((CONV_W, HEAD_D), lambda j: (0, j)),
                  pl.BlockSpec((S, HEAD_D), lambda j: (0, 3 * (j % nh) + j // nh))],
        out_specs=[pl.BlockSpec((S, HEAD_D), lambda j: (0, j)), pl.BlockSpec((CONV_W, HEAD_D), lambda j: (0, j))],
        out_shape=[sds((S, 3 * nh * HEAD_D), F32), sds((CONV_W, 3 * nh * HEAD_D), F32)],
        compiler_params=_params(("parallel",)), name="gdn_prep_bwd")(proj, conv_w, dqkv)


def _softplus(x):
    return jnp.maximum(x, 0.0) + jnp.log(1.0 + jnp.exp(-jnp.abs(x)))


def _gates_fwd(ab, al, dt, nh):
    S = ab.shape[0]

    def body(ab_ref, al_ref, dt_ref, o_ref):
        v = ab_ref[...]
        lane = lax.broadcasted_iota(jnp.int32, v.shape, 1)
        g = -jnp.exp(al_ref[...]) * _softplus(v + dt_ref[...])
        o_ref[...] = jnp.where(lane < nh, g, jnp.where(lane < 2 * nh, _sigmoid(v), 0.0))

    row = pl.BlockSpec((1, LANE), lambda i: (0, 0))
    full = pl.BlockSpec((S, LANE), lambda i: (0, 0))
    return pl.pallas_call(body, grid=(1,), in_specs=[full, row, row], out_specs=full,
                          out_shape=jax.ShapeDtypeStruct((S, LANE), F32),
                          compiler_params=_params(("arbitrary",)), name="gates_fwd")(ab, al, dt)


def _gates_bwd(ab, al, dt, dgb, nh):
    S = ab.shape[0]
    sds = jax.ShapeDtypeStruct

    def body(ab_ref, al_ref, dt_ref, d_ref, dab_ref, dal_ref, ddt_ref):
        v, d = ab_ref[...], d_ref[...]
        lane = lax.broadcasted_iota(jnp.int32, v.shape, 1)
        is_a = lane < nh
        z = v + dt_ref[...]
        dsp = jnp.where(is_a, d * (-jnp.exp(al_ref[...])), 0.0)
        dz = dsp * _sigmoid(z)
        beta = _sigmoid(v)
        dab_ref[...] = jnp.where(is_a, dz, jnp.where(lane < 2 * nh, d * beta * (1.0 - beta), 0.0))
        dal_ref[...] = jnp.sum(dsp * _softplus(z), axis=0, keepdims=True)
        ddt_ref[...] = jnp.sum(dz, axis=0, keepdims=True)

    row = pl.BlockSpec((1, LANE), lambda i: (0, 0))
    full = pl.BlockSpec((S, LANE), lambda i: (0, 0))
    return pl.pallas_call(body, grid=(1,), in_specs=[full, row, row, full], out_specs=[full, row, row],
                          out_shape=[sds((S, LANE), F32), sds((1, LANE), F32), sds((1, LANE), F32)],
                          compiler_params=_params(("arbitrary",)), name="gates_bwd")(ab, al, dt, dgb)


def _col_of(tile, h):
    lane = lax.broadcasted_iota(jnp.int32, tile.shape, 1)
    return jnp.sum(jnp.where(lane == h, tile, 0.0), axis=1, keepdims=True)


def _to_row(col, eye):
    return jnp.sum(jnp.where(eye, col, 0.0), axis=0, keepdims=True)


def _to_col(row, eye):
    return jnp.sum(jnp.where(eye, row, 0.0), axis=1, keepdims=True)


def _split(a):
    hi = a.astype(BF16)
    return hi, (a - hi.astype(F32)).astype(BF16)


def _gdot(a, b, dims):
    ah, al = _split(a)
    bh, bl = _split(b)
    return _dot(ah, bh, dims) + (_dot(ah, bl, dims) + _dot(al, bh, dims))


def _bdot(a, b, dims):
    return _dot(a.astype(BF16), b.astype(BF16), dims)


def _chunks_local(qs, ks, vs, gcols, bcols, Ts=None):
    C = CHUNK
    row = lax.broadcasted_iota(jnp.int32, (C, C), 0)
    col = lax.broadcasted_iota(jnp.int32, (C, C), 1)
    tril, strict, eye = col <= row, col < row, col == row
    outs = []
    for k, gcol, bcol in zip(ks, gcols, bcols):
        grow = _to_row(gcol, eye)
        G_row = jnp.sum(jnp.where(row <= col, gcol, 0.0), axis=0, keepdims=True)
        G_col = jnp.sum(jnp.where(tril, grow, 0.0), axis=1, keepdims=True)
        G_last = G_col[C - 1:C, :]
        outs.append(dict(strict=strict, eye=eye, row=row, col=col, decay=jnp.exp(jnp.where(tril, G_col - G_row, NEG_INF)),
                         eG=jnp.exp(G_col), eGr=jnp.exp(G_last - G_col), gl=jnp.exp(G_last), kb=k * bcol))
    Ms = [_gdot(o["kb"], k, NT) for o, k in zip(outs, ks)]
    Ns = [_gdot(q, k, NT) for q, k in zip(qs, ks)]
    for o, q, k, M, N in zip(outs, qs, ks, Ms, Ns):
        o.update(A=jnp.where(strict, M * o["decay"], 0.0), attn=N * o["decay"], rhs_k=o["kb"] * o["eG"],
                 q_dec=q * o["eG"], k_dec=k * o["eGr"])
    if Ts is None:
        Ts = [jnp.where(eye, 1.0, 0.0) - o["A"] for o in outs]
        Ps = [o["A"] for o in outs]
        for _ in range(int(math.log2(C)) - 1):
            Ps = [_bdot(P, P, NN) for P in Ps]
            Ts = [T + _bdot(T, P, NN) for T, P in zip(Ts, Ps)]
        us = [_bdot(T, v * bcol, NN) for T, v, bcol in zip(Ts, vs, bcols)]
        ws = [_bdot(T, o["rhs_k"], NN) for T, o in zip(Ts, outs)]
        for o, T, u, w in zip(outs, Ts, us, ws):
            o.update(T=T, u=u, w=w)
    return outs


GDN_ROWS = 256
GDN_LOCAL_ROWS = 512
WQK = 3 * CHUNK


def _gdn_local_fwd(qkv, gb, nh):
    S = qkv.shape[0]
    nc = S // CHUNK
    rb = min(GDN_LOCAL_ROWS, S)
    cpb = rb // CHUNK
    sds = jax.ShapeDtypeStruct

    def body(q_ref, k_ref, v_ref, gb_ref, u_ref, wqk_ref, attn_ref, t_ref):
        h = pl.program_id(0)
        rows_of = [slice(ci * CHUNK, (ci + 1) * CHUNK) for ci in range(cpb)]
        gbts = [gb_ref[rows, :] for rows in rows_of]
        Ls = _chunks_local([q_ref[rows, :] for rows in rows_of], [k_ref[rows, :] for rows in rows_of],
                           [v_ref[rows, :] for rows in rows_of], [_col_of(t, h) for t in gbts],
                           [_col_of(t, nh + h) for t in gbts])
        for ci, (rows, L) in enumerate(zip(rows_of, Ls)):
            u_ref[rows, :] = L["u"]
            base = ci * WQK
            wqk_ref[base:base + CHUNK, :] = L["w"]
            wqk_ref[base + CHUNK:base + 2 * CHUNK, :] = L["q_dec"]
            wqk_ref[base + 2 * CHUNK:base + WQK, :] = L["k_dec"]
            attn_ref[ci] = L["attn"]
            t_ref[ci] = L["T"]

    cc = pl.BlockSpec((None, cpb, CHUNK, CHUNK), lambda h, i: (h, i, 0, 0))
    return pl.pallas_call(
        body, grid=(nh, S // rb),
        in_specs=[pl.BlockSpec((rb, HEAD_D), lambda h, i: (i, 3 * h)), pl.BlockSpec((rb, HEAD_D), lambda h, i: (i, 3 * h + 1)),
                  pl.BlockSpec((rb, HEAD_D), lambda h, i: (i, 3 * h + 2)), pl.BlockSpec((rb, LANE), lambda h, i: (i, 0))],
        out_specs=[pl.BlockSpec((rb, HEAD_D), lambda h, i: (i, h)),
                   pl.BlockSpec((None, 3 * rb, HEAD_D), lambda h, i: (h, i, 0)), cc, cc],
        out_shape=[sds((S, nh * HEAD_D), F32), sds((nh, 3 * S, HEAD_D), F32), sds((nh, nc, CHUNK, CHUNK), F32),
                   sds((nh, nc, CHUNK, CHUNK), F32)],
        compiler_params=_params(("parallel", "parallel")), name="gdn_local_fwd")(qkv, qkv, qkv, gb)


def _gdn_scan_fwd(u, wqk, attn, gb, nh, dep):
    S = u.shape[0]
    nc = S // CHUNK
    rb = min(GDN_ROWS, S)
    cpb = rb // CHUNK
    sds = jax.ShapeDtypeStruct

    def body(u_ref, wqk_ref, attn_ref, gb_ref, dep_ref, o_ref, vn_ref, st_ref, s_ref):
        @pl.when(pl.program_id(0) == 0)
        def _():
            s_ref[...] = jnp.zeros_like(s_ref)

        for ci in range(cpb):
            rows = slice(ci * CHUNK, (ci + 1) * CHUNK)
            glv = jnp.exp(jnp.sum(gb_ref[rows, :], axis=0, keepdims=True))
            base = ci * WQK
            heads = range(nh)
            cols = [slice(h * HEAD_D, (h + 1) * HEAD_D) for h in heads]
            states = [s_ref[h] for h in heads]
            rs = [_gdot(wqk_ref[h, base:base + 2 * CHUNK, :], states[h], NN) for h in heads]
            vbs = [u_ref[rows, cols[h]] - rs[h][:CHUNK] for h in heads]
            os_ = [_gdot(attn_ref[h, ci], vbs[h], NN) for h in heads]
            ks_ = [_gdot(wqk_ref[h, base + 2 * CHUNK:base + WQK, :], vbs[h], TN) for h in heads]
            for h in heads:
                st_ref[h, ci] = states[h]
                o_ref[rows, cols[h]] = rs[h][CHUNK:] + os_[h]
                vn_ref[rows, cols[h]] = vbs[h]
                s_ref[h] = states[h] * glv[:, h:h + 1] + ks_[h]

    return pl.pallas_call(
        body, grid=(S // rb,),
        in_specs=[pl.BlockSpec((rb, nh * HEAD_D), lambda i: (i, 0)), pl.BlockSpec((nh, 3 * rb, HEAD_D), lambda i: (0, i, 0)),
                  pl.BlockSpec((nh, cpb, CHUNK, CHUNK), lambda i: (0, i, 0, 0)), pl.BlockSpec((rb, LANE), lambda i: (i, 0)),
                  pl.BlockSpec(memory_space=pl.ANY)],
        out_specs=[pl.BlockSpec((rb, nh * HEAD_D), lambda i: (i, 0)), pl.BlockSpec((rb, nh * HEAD_D), lambda i: (i, 0)),
                   pl.BlockSpec((nh, cpb, HEAD_D, HEAD_D), lambda i: (0, i, 0, 0))],
        out_shape=[sds((S, nh * HEAD_D), F32), sds((S, nh * HEAD_D), F32), sds((nh, nc, HEAD_D, HEAD_D), F32)],
        scratch_shapes=[pltpu.VMEM((nh, HEAD_D, HEAD_D), F32)],
        compiler_params=_params(("arbitrary",)), name="gdn_scan_fwd")(u, wqk, attn, gb, dep)


def _gdn_scan_bwd(wqk, attn, gb, states, vn, do, nh, dep):
    S = vn.shape[0]
    nc = S // CHUNK
    rb = min(GDN_ROWS, S)
    cpb = rb // CHUNK
    last = S // rb - 1
    sds = jax.ShapeDtypeStruct

    def body(wqk_ref, attn_ref, gb_ref, st_ref, vn_ref, do_ref, dep_ref, dvn_ref, dw_ref, dqd_ref, dkd_ref, da_ref, dgl_ref,
             ds_ref):
        @pl.when(pl.program_id(0) == 0)
        def _():
            ds_ref[...] = jnp.zeros_like(ds_ref)

        row = lax.broadcasted_iota(jnp.int32, (CHUNK, CHUNK), 0)
        col = lax.broadcasted_iota(jnp.int32, (CHUNK, CHUNK), 1)
        for ci in reversed(range(cpb)):
            rows = slice(ci * CHUNK, (ci + 1) * CHUNK)
            glv = jnp.exp(jnp.sum(gb_ref[rows, :], axis=0, keepdims=True))
            base = ci * WQK
            heads = range(nh)
            cols = [slice(h * HEAD_D, (h + 1) * HEAD_D) for h in heads]
            states = [st_ref[h, ci] for h in heads]
            dSs = [ds_ref[h] for h in heads]
            vbs = [vn_ref[rows, cols[h]] for h in heads]
            dobs = [do_ref[rows, cols[h]] for h in heads]
            dv1 = [_gdot(attn_ref[h, ci], dobs[h], TN) for h in heads]
            dv2 = [_gdot(wqk_ref[h, base + 2 * CHUNK:base + WQK, :], dSs[h], NN) for h in heads]
            das = [_gdot(dobs[h], vbs[h], NT) for h in heads]
            dkds = [_gdot(vbs[h], dSs[h], NT) for h in heads]
            dvbs = [dv1[h] + dv2[h] for h in heads]
            xs = [_gdot(jnp.concatenate([dobs[h], dvbs[h]], axis=0), states[h], NT) for h in heads]
            dss = [_gdot(wqk_ref[h, base:base + 2 * CHUNK, :], jnp.concatenate([-dvbs[h], dobs[h]], axis=0), TN)
                   for h in heads]
            for h in heads:
                dqd_ref[rows, cols[h]] = xs[h][:CHUNK]
                dw_ref[rows, cols[h]] = -xs[h][CHUNK:]
                dvn_ref[rows, cols[h]] = dvbs[h]
                da_ref[h, ci] = jnp.where(col <= row, das[h], 0.0)
                dkd_ref[rows, cols[h]] = dkds[h]
                gl = glv[:, h:h + 1]
                dgl = jnp.sum(jnp.sum(states[h] * dSs[h], axis=0, keepdims=True), axis=1, keepdims=True)
                dgl_ref[h, ci] = jnp.broadcast_to(dgl * gl, (1, LANE))
                ds_ref[h] = dSs[h] * gl + dss[h]

    rv = lambda i: last - i
    wide = pl.BlockSpec((rb, nh * HEAD_D), lambda i: (rv(i), 0))
    return pl.pallas_call(
        body, grid=(S // rb,),
        in_specs=[pl.BlockSpec((nh, 3 * rb, HEAD_D), lambda i: (0, rv(i), 0)),
                  pl.BlockSpec((nh, cpb, CHUNK, CHUNK), lambda i: (0, rv(i), 0, 0)),
                  pl.BlockSpec((rb, LANE), lambda i: (rv(i), 0)),
                  pl.BlockSpec((nh, cpb, HEAD_D, HEAD_D), lambda i: (0, rv(i), 0, 0)), wide, wide,
                  pl.BlockSpec(memory_space=pl.ANY)],
        out_specs=[wide, wide, wide, wide, pl.BlockSpec((nh, cpb, CHUNK, CHUNK), lambda i: (0, rv(i), 0, 0)),
                   pl.BlockSpec((nh, cpb, 1, LANE), lambda i: (0, rv(i), 0, 0))],
        out_shape=[sds((S, nh * HEAD_D), F32), sds((S, nh * HEAD_D), F32), sds((S, nh * HEAD_D), F32),
                   sds((S, nh * HEAD_D), F32), sds((nh, nc, CHUNK, CHUNK), F32), sds((nh, nc, 1, LANE), F32)],
        scratch_shapes=[pltpu.VMEM((nh, HEAD_D, HEAD_D), F32)],
        compiler_params=_params(("arbitrary",)), name="gdn_scan_bwd")(wqk, attn, gb, states, vn, do, dep)


def _gdn_local_bwd(qkv, gb, T, u, wqk, dvn, dw, dqd, dkd, dattn, dgl, nh):
    S = qkv.shape[0]
    rb = min(GDN_LOCAL_ROWS, S)
    cpb = rb // CHUNK
    sds = jax.ShapeDtypeStruct

    def body(q_ref, k_ref, v_ref, gb_ref, t_ref, u_ref, wqk_ref, dvn_ref, dw_ref, dqd_ref, dkd_ref, da_ref, dgl_ref,
             dqkv_ref, dg_ref, db_ref):
        h = pl.program_id(0)
        n = range(cpb)
        rows_of = [slice(ci * CHUNK, (ci + 1) * CHUNK) for ci in n]
        qs, ks, vs = ([r[rows, :] for rows in rows_of] for r in (q_ref, k_ref, v_ref))
        gbts = [gb_ref[rows, :] for rows in rows_of]
        bcols = [_col_of(t, nh + h) for t in gbts]
        Ts = [t_ref[ci] for ci in n]
        Ls = _chunks_local(qs, ks, vs, [_col_of(t, h) for t in gbts], bcols, Ts=Ts)
        drvs = [_gdot(Ts[ci], dvn_ref[rows_of[ci], :], TN) for ci in n]
        drks = [_gdot(Ts[ci], dw_ref[rows_of[ci], :], TN) for ci in n]
        dAs = [jnp.where(Ls[ci]["strict"], -(_gdot(drvs[ci], u_ref[rows_of[ci], :], NT)
                                             + _gdot(drks[ci], wqk_ref[ci * WQK:ci * WQK + CHUNK, :], NT)), 0.0) for ci in n]
        dMs = [dAs[ci] * Ls[ci]["decay"] for ci in n]
        dNs = [da_ref[ci] * Ls[ci]["decay"] for ci in n]
        dkbs = [_gdot(dMs[ci], ks[ci], NN) for ci in n]
        dq1 = [_gdot(dNs[ci], ks[ci], NN) for ci in n]
        dk1 = [_gdot(dMs[ci], Ls[ci]["kb"], TN) for ci in n]
        dk2 = [_gdot(dNs[ci], qs[ci], TN) for ci in n]
        for ci in n:
            rows, L, q, k, v, bcol = rows_of[ci], Ls[ci], qs[ci], ks[ci], vs[ci], bcols[ci]
            eye, eG, eGr = L["eye"], L["eG"], L["eGr"]
            drv, drk, dkb = drvs[ci], drks[ci], dkbs[ci]
            dq_dec, dk_dec, dattn_c = dqd_ref[rows, :], dkd_ref[rows, :], da_ref[ci]
            dqkv_ref[rows, :HEAD_D] = dq1[ci] + dq_dec * eG
            dqkv_ref[rows, HEAD_D:2 * HEAD_D] = drk * (bcol * eG) + dk1[ci] + dkb * bcol + dk2[ci] + dk_dec * eGr
            dqkv_ref[rows, 2 * HEAD_D:] = drv * bcol
            db_ref[rows, :] = (jnp.sum(drv * v, axis=1, keepdims=True) + jnp.sum(drk * k, axis=1, keepdims=True) * eG
                               + jnp.sum(dkb * k, axis=1, keepdims=True))
            E = dAs[ci] * L["A"] + dattn_c * L["attn"]
            kd = jnp.sum(dk_dec * L["k_dec"], axis=1, keepdims=True)
            dG = (jnp.sum(dq_dec * L["q_dec"], axis=1, keepdims=True) - kd
                  + jnp.sum(drk * L["rhs_k"], axis=1, keepdims=True)
                  + jnp.sum(E, axis=1, keepdims=True) - _to_col(jnp.sum(E, axis=0, keepdims=True), eye))
            d_last = jnp.sum(kd, axis=0, keepdims=True) + dgl_ref[ci][:, :1]
            dG = dG + jnp.where(L["row"][:, :1] == CHUNK - 1, d_last, 0.0)
            dg_ref[rows, :] = jnp.sum(jnp.where(L["col"] >= L["row"], _to_row(dG, eye), 0.0), axis=1, keepdims=True)

    hd = pl.BlockSpec((rb, HEAD_D), lambda h, i: (i, h))
    cc = pl.BlockSpec((None, cpb, CHUNK, CHUNK), lambda h, i: (h, i, 0, 0))
    col1 = pl.BlockSpec((None, rb, 1), lambda h, i: (h, i, 0))
    return pl.pallas_call(
        body, grid=(nh, S // rb),
        in_specs=[pl.BlockSpec((rb, HEAD_D), lambda h, i: (i, 3 * h)), pl.BlockSpec((rb, HEAD_D), lambda h, i: (i, 3 * h + 1)),
                  pl.BlockSpec((rb, HEAD_D), lambda h, i: (i, 3 * h + 2)), pl.BlockSpec((rb, LANE), lambda h, i: (i, 0)),
                  cc, hd, pl.BlockSpec((None, 3 * rb, HEAD_D), lambda h, i: (h, i, 0)), hd, hd, hd, hd, cc,
                  pl.BlockSpec((None, cpb, 1, LANE), lambda h, i: (h, i, 0, 0))],
        out_specs=[pl.BlockSpec((rb, 3 * HEAD_D), lambda h, i: (i, h)), col1, col1],
        out_shape=[sds((S, 3 * nh * HEAD_D), F32)] + [sds((nh, S, 1), F32)] * 2,
        compiler_params=_params(("parallel", "parallel")), name="gdn_local_bwd")(
            qkv, qkv, qkv, gb, T, u, wqk, dvn, dw, dqd, dkd, dattn, dgl)


def _gated_norm_fwd(o, proj, norm_w, nh, z_blk0, mix, m_blk0):
    S = o.shape[0]

    def body(o_ref, z_ref, w_ref, mix_ref, y_ref):
        ov, z = o_ref[...], z_ref[...]
        r = lax.rsqrt(jnp.mean(ov * ov, axis=1, keepdims=True) + RMS_EPS)
        y_ref[...] = ov * r * w_ref[...] * (z * _sigmoid(z))

    return pl.pallas_call(
        body, grid=(nh,),
        in_specs=[pl.BlockSpec((S, HEAD_D), lambda h: (0, h)), pl.BlockSpec((S, HEAD_D), lambda h: (0, z_blk0 + h)),
                  pl.BlockSpec((1, HEAD_D), lambda h: (0, 0)), pl.BlockSpec(memory_space=pl.ANY)],
        out_specs=pl.BlockSpec((S, HEAD_D), lambda h: (0, m_blk0 + h)),
        out_shape=jax.ShapeDtypeStruct(mix.shape, F32), input_output_aliases={3: 0},
        compiler_params=_params(("parallel",)), name="gated_norm_fwd")(o, proj, norm_w, mix)


def _gated_norm_bwd(o, proj, norm_w, dmix, nh, z_blk0, d_blk0):
    S = o.shape[0]
    sds = jax.ShapeDtypeStruct

    def body(o_ref, z_ref, w_ref, dy_ref, do_ref, dz_ref, dw_ref):
        ov, z, w, dy = o_ref[...], z_ref[...], w_ref[...], dy_ref[...]
        r = lax.rsqrt(jnp.mean(ov * ov, axis=1, keepdims=True) + RMS_EPS)
        oh = ov * r
        sg = _sigmoid(z)
        dz_ref[...] = dy * (oh * w) * (sg * (1.0 + z * (1.0 - sg)))
        don = dy * (z * sg)
        @pl.when(pl.program_id(0) == 0)
        def _():
            dw_ref[...] = jnp.zeros_like(dw_ref)

        dw_ref[...] += jnp.sum(don * oh, axis=0, keepdims=True)
        doh = don * w
        do_ref[...] = r * (doh - oh * jnp.mean(doh * oh, axis=1, keepdims=True))

    return pl.pallas_call(
        body, grid=(nh,),
        in_specs=[pl.BlockSpec((S, HEAD_D), lambda h: (0, h)), pl.BlockSpec((S, HEAD_D), lambda h: (0, z_blk0 + h)),
                  pl.BlockSpec((1, HEAD_D), lambda h: (0, 0)), pl.BlockSpec((S, HEAD_D), lambda h: (0, d_blk0 + h))],
        out_specs=[pl.BlockSpec((S, HEAD_D), lambda h: (0, h)), pl.BlockSpec((S, HEAD_D), lambda h: (0, h)),
                   pl.BlockSpec((1, HEAD_D), lambda h: (0, 0))],
        out_shape=[sds((S, nh * HEAD_D), F32), sds((S, nh * HEAD_D), F32), sds((1, HEAD_D), F32)],
        compiler_params=_params(("arbitrary",)), name="gated_norm_bwd")(o, proj, norm_w, dmix)


def _adamw_math(w, g, m, v):
    m = ADAM_B1 * m + (1.0 - ADAM_B1) * g
    v = ADAM_B2 * v + (1.0 - ADAM_B2) * (g * g)
    m_hat = m / (1.0 - ADAM_B1 ** ADAM_STEP)
    v_hat = v / (1.0 - ADAM_B2 ** ADAM_STEP)
    delta = -ADAM_LR * (m_hat / (jnp.sqrt(v_hat) + ADAM_EPS) + ADAM_WD * w)
    return delta, m, v


def _slab_tiles(R, C, rows=256, cols=256):
    if R % rows == 0:
        return (rows, C), R // rows, lambda i: (i, 0)
    tc = _tile(C, cols)
    return (R, tc), C // tc, lambda i: (0, i)


def _adamw_big(parts, terms, chip, w, m, v, name):
    R, C = w.shape
    blk, steps, at = _slab_tiles(R, C)
    sds = jax.ShapeDtypeStruct

    def body(q_ref, p_ref, t_ref, w_ref, m_ref, v_ref, g_ref, d_ref, nm_ref, nv_ref):
        g = ((p_ref[...].astype(F32) + t_ref[0].astype(F32)) + t_ref[1].astype(F32)) + t_ref[2].astype(F32)
        g_ref[...] = g
        d_ref[...], nm_ref[...], nv_ref[...] = _adamw_math(w_ref[...], g, m_ref[...], v_ref[...])

    spec = pl.BlockSpec(blk, lambda i, q_ref: at(i))
    grid_spec = pltpu.PrefetchScalarGridSpec(
        num_scalar_prefetch=1, grid=(steps,),
        in_specs=[pl.BlockSpec((None,) + blk, lambda i, q_ref: (q_ref[0],) + at(i)),
                  pl.BlockSpec((3,) + blk, lambda i, q_ref: (0,) + at(i)), spec, spec, spec],
        out_specs=[spec] * 4)
    return pl.pallas_call(body, grid_spec=grid_spec, out_shape=[sds((R, C), F32)] * 4,
                          compiler_params=_params(("parallel",)), name=name)(chip, parts, terms, w, m, v)


def _adamw_small(ws, gs, ms, vs):
    n = len(ws)

    def body(*refs):
        for i in range(n):
            w, g, m, v = (refs[k * n + i][...] for k in range(4))
            d, nm, nv = _adamw_math(w, g, m, v)
            refs[4 * n + i][...] = d
            refs[5 * n + i][...] = nm
            refs[6 * n + i][...] = nv

    shapes = [jax.ShapeDtypeStruct(w.shape, F32) for w in ws]
    vm = pl.BlockSpec(memory_space=pltpu.VMEM)
    outs = pl.pallas_call(body, in_specs=[vm] * (4 * n), out_specs=[vm] * (3 * n), out_shape=shapes * 3,
                          name="adamw_small")(*ws, *gs, *ms, *vs)
    return outs[:n], outs[n:2 * n], outs[2 * n:]


MESH = pl.DeviceIdType.MESH
ANY = pl.BlockSpec(memory_space=pl.ANY)


def _place():
    x, y, c = lax.axis_index("x"), lax.axis_index("y"), lax.axis_index("c")
    return x, y, c, [(1 - x, y), (x, 1 - y), (1 - x, 1 - y)]


def _chip_sum(grad, recv, core, name):
    _, R, C = grad.shape
    blk, steps, at = _slab_tiles(R, C, rows=512 if R % 512 == 0 else 256, cols=512)

    def body(c_ref, g_ref, r_ref, o_ref):
        o_ref[...] = (g_ref[...].astype(F32) + r_ref[...].astype(F32)).astype(o_ref.dtype)

    grid_spec = pltpu.PrefetchScalarGridSpec(
        num_scalar_prefetch=1, grid=(4, steps),
        in_specs=[pl.BlockSpec((None,) + blk, lambda q, i, c_ref: (2 * q + c_ref[0],) + at(i)),
                  pl.BlockSpec((None,) + blk, lambda q, i, c_ref: (q,) + at(i))],
        out_specs=pl.BlockSpec((None,) + blk, lambda q, i, c_ref: (q,) + at(i)))
    return pl.pallas_call(body, grid_spec=grid_spec, out_shape=jax.ShapeDtypeStruct((4, R, C), BF16),
                          compiler_params=_params(("parallel", "parallel")), name=name)(core, grad, recv)


HBM_SPEC = pl.BlockSpec(memory_space=pltpu.HBM)
SEM_SPEC = pl.BlockSpec(memory_space=pltpu.SEMAPHORE)
DATAFLOW = pltpu.SideEffectType.DATAFLOW_SIDE_EFFECTING


def _split_start(name, bufs, plan, counts, after=None):
    nb, ng = len(bufs), len(counts)
    extra = [] if after is None else [after]
    place = [(g, k) for g, cnt in enumerate(counts) for k in range(cnt)]

    def body(*refs):
        sems, token = refs[nb + len(extra):nb + len(extra) + 2 * ng], refs[-1]
        for (g, k), (src, dst, to) in zip(place, plan(refs[:nb])):
            pltpu.make_async_remote_copy(src_ref=src, dst_ref=dst, send_sem=sems[2 * g].at[k], recv_sem=sems[2 * g + 1].at[k],
                                         device_id=to, device_id_type=MESH).start()
        token[...] = jnp.zeros_like(token)

    outs = pl.pallas_call(
        body, name=name,
        out_shape=(*[pltpu.SemaphoreType.DMA((cnt,)) for cnt in counts for _ in range(2)],
                   *[pltpu.HBM(b.shape, b.dtype) for b in bufs], jax.ShapeDtypeStruct((8, LANE), F32)),
        in_specs=[HBM_SPEC] * nb + [ANY] * len(extra),
        out_specs=(*[SEM_SPEC] * (2 * ng), *[HBM_SPEC] * nb, pl.BlockSpec(memory_space=pltpu.VMEM)),
        input_output_aliases={i: 2 * ng + i for i in range(nb)},
        compiler_params=pltpu.CompilerParams(has_side_effects=DATAFLOW))(
            *[pltpu.with_memory_space_constraint(b, pltpu.HBM) for b in bufs], *extra)
    return [(outs[2 * g], outs[2 * g + 1]) for g in range(ng)], list(outs[2 * ng:2 * ng + nb]), outs[-1]


def _split_wait(name, sems, bufs, plan, after):
    nb = len(bufs)
    send_sems, recv_sems = sems

    def body(*refs):
        send_s, recv_s = refs[nb], refs[nb + 1]
        for k, (src, dst, to) in enumerate(plan(refs[:nb])):
            cp = pltpu.make_async_remote_copy(src_ref=src, dst_ref=dst, send_sem=send_s.at[k], recv_sem=recv_s.at[k],
                                              device_id=to, device_id_type=MESH)
            cp.wait_send()
            cp.wait_recv()

    after = tuple(after) if isinstance(after, (tuple, list)) else (after,)
    outs = pl.pallas_call(
        body, name=name, out_shape=tuple(pltpu.HBM(b.shape, b.dtype) for b in bufs),
        in_specs=[HBM_SPEC] * nb + [SEM_SPEC, SEM_SPEC] + [ANY] * len(after), out_specs=tuple([HBM_SPEC] * nb),
        input_output_aliases={i: i for i in range(nb)},
        compiler_params=pltpu.CompilerParams(has_side_effects=DATAFLOW))(*bufs, send_sems, recv_sems, *after)
    return list(outs)


def _slot(px, py, pc):
    return 4 * px + 2 * py + pc


class _Gather:
    def __init__(self, shards, groups, dev):
        self.shards, self.groups, self.dev = shards, groups, dev
        self.second = {}

    @staticmethod
    def _plan1(pairs, refs):
        x, y, c, chips = _place()
        out = []
        for s, l in pairs:
            dst = refs[l].at[_slot(x, y, c)]
            out.append((refs[s], dst, (x, y, 1 - c)))
            out += [(refs[s], dst, (px, py, c)) for px, py in chips]
        return out

    @staticmethod
    def _plan2(refs):
        x, y, c, chips = _place()
        return [(r.at[_slot(px, py, c)],) * 2 + ((x, y, 1 - c),) for r in refs for px, py in chips]

    def start(self):
        n = len(self.shards)
        lands = [lax.dynamic_update_slice(lax.empty((N_DEV,) + s.shape, s.dtype), s[None], (self.dev, 0, 0))
                 for s in self.shards]
        pairs = [(w, n + w) for g in self.groups for w in g]
        sems, bufs, token = _split_start("gather_start_1", list(self.shards) + lands, functools.partial(self._plan1, pairs),
                                         tuple(4 * len(g) for g in self.groups))
        self.first = [(sems[i], [bufs[w] for w in g], [bufs[n + w] for w in g]) for i, g in enumerate(self.groups)]
        return token

    def mid(self, gi, after):
        sems, srcs, lands = self.first[gi]
        m = len(srcs)
        plan = functools.partial(self._plan1, [(w, m + w) for w in range(m)])
        lands = _split_wait("gather_%d_wait_1" % gi, sems, srcs + lands, plan, after)[m:]
        sems, lands, token = _split_start("gather_%d_start_2" % gi, lands, self._plan2, (3 * m,))
        self.second[gi] = (sems[0], lands)
        return token

    def finish(self, gi, after):
        sems, lands = self.second[gi]
        return _split_wait("gather_%d_wait_2" % gi, sems, lands, self._plan2, after)


class _Exchanges:
    def __init__(self, tag, n, core, gather=None):
        self.tag, self.n, self.core, self.gather = tag, n, core, gather

    def weights_mid(self, group, after):
        return self.gather.mid(group, after)

    def weights_finish(self, group, after):
        return self.gather.finish(group, after)

    def _reduce_plan1(self, refs):
        n = self.n
        x, y, c, _ = _place()
        return [(refs[w].at[2 * q + (1 - c)], refs[n + w].at[q], (x, y, 1 - c)) for w in range(n) for q in range(4)]

    def _reduce_plan2(self, refs):
        n = self.n
        x, y, c, chips = _place()
        return [(refs[w].at[2 * px + py], refs[n + w].at[j], (px, py, c))
                for w in range(n) for j, (px, py) in enumerate(chips)]

    def grads_start(self, grads):
        lands = [lax.empty((4,) + g.shape[1:], g.dtype) for g in grads]
        self.g1 = _split_start(self.tag + "reduce_start_1", list(grads) + lands, self._reduce_plan1, (4 * self.n,))
        return self.g1[2]

    def grads_mid(self, after):
        n = self.n
        sems, bufs, _ = self.g1
        bufs = _split_wait(self.tag + "reduce_wait_1", sems[0], bufs, self._reduce_plan1, after)
        core = self.core.reshape(1).astype(jnp.int32)
        self.parts = [_chip_sum(bufs[w], bufs[n + w], core, self.tag + "reduce_chip_sum_%d" % w) for w in range(n)]
        lands = [lax.empty((3,) + p.shape[1:], p.dtype) for p in self.parts]
        self.g2 = _split_start(self.tag + "reduce_start_2", self.parts + lands, self._reduce_plan2, (3 * n,))
        return self.g2[2]

    def grads_finish(self, after):
        n = self.n
        sems, bufs, _ = self.g2
        bufs = _split_wait(self.tag + "reduce_wait_2", sems[0], bufs, self._reduce_plan2, after)
        self.parts, self.terms = bufs[:n], bufs[n:]


def _all_reduce_small(buf):
    R = buf.shape[0]

    def body(x_ref, o_ref, g_ref, send_sems, recv_sems):
        x, y, c, chips = _place()
        me, sibling = (x, y, c), (x, y, 1 - c)

        def slot(px, py, pc):
            return 4 * px + 2 * py + pc

        def copy(k, block, to, src=None):
            dst = g_ref.at[slot(*block)]
            return pltpu.make_async_remote_copy(src_ref=dst if src is None else src, dst_ref=dst,
                                                send_sem=send_sems.at[k], recv_sem=recv_sems.at[k],
                                                device_id=to, device_id_type=MESH)

        first = [copy(0, me, sibling, src=x_ref)]
        first += [copy(1 + j, me, (*chip, c), src=x_ref) for j, chip in enumerate(chips)]
        for cp in first:
            cp.start()
        g_ref[slot(*me)] = x_ref[...]
        passed = [copy(4 + j, (*chip, c), sibling) for j, chip in enumerate(chips)]
        for j, chip in enumerate(chips):
            copy(1 + j, (*chip, c), me).wait_recv()
            passed[j].start()
        copy(0, sibling, me).wait_recv()
        for j, chip in enumerate(chips):
            copy(4 + j, (*chip, 1 - c), me).wait_recv()
        for cp in first + passed:
            cp.wait_send()
        acc = g_ref[0]
        for s in range(1, N_DEV):
            acc = acc + g_ref[s]
        o_ref[...] = acc

    vm = pl.BlockSpec(memory_space=pltpu.VMEM)
    return pl.pallas_call(
        body, in_specs=[vm], out_specs=vm, out_shape=jax.ShapeDtypeStruct((R, LANE), F32),
        scratch_shapes=[pltpu.VMEM((N_DEV, R, LANE), F32), pltpu.SemaphoreType.DMA((7,)), pltpu.SemaphoreType.DMA((7,))],
        name="all_reduce_small")(buf)


def _pad_cols(a, width):
    return jnp.pad(a, ((0, 0), (0, width - a.shape[1])))


def _local_step(x, target, w_in_g, conv_w, a_log, dt_bias, delta_norm_w, sinks, bias, ln1_g, ln1_b, ln2_g, ln2_b,
                ex, ex_in):
    S, D = x.shape
    aw = D // 2
    hq, hkv, nh = aw // HEAD_A, aw // HEAD_A // GQA, aw // HEAD_D
    kvw = hkv * HEAD_A
    c_q, c_k, c_v, c_d = 0, aw, aw + kvw, aw + 2 * kvw
    c_ab = c_d + 3 * aw
    c_z = c_ab + 2 * nh
    n_in = c_z + aw
    n_slab = w_in_g.shape[1]
    assert w_in_g.shape == (N_DEV, n_in // N_DEV, D), (w_in_g.shape, n_in)

    w_in_t = w_in_g.reshape(n_in, D)
    w_pt = jnp.concatenate([w_in_t[:c_ab], w_in_t[c_z:], jnp.pad(w_in_t[c_ab:c_z], ((0, LANE - 2 * nh), (0, 0)))], axis=0)
    p_z, p_ab = c_ab, c_ab + aw
    n_p = p_ab + LANE

    xb = x.astype(BF16)
    proj = _matmul(xb, w_pt, NT, name="proj", tn=1152)
    attn_out = _attn_fwd(proj, bias, sinks.reshape(-1), hq, 0, c_k // kvw, c_v // kvw, D)
    conv2 = conv_w.reshape(CONV_W, 3 * aw)
    qkv = _gdn_prep_fwd(proj, conv2, nh, c_d // HEAD_D)
    ab = proj[:, p_ab:]
    al, dt = _pad_cols(a_log, LANE), _pad_cols(dt_bias, LANE)
    gb = _gates_fwd(ab, al, dt, nh)
    u_d, wqk, attn_d, t_d = _gdn_local_fwd(qkv, gb, nh)
    o_d, vn, states = _gdn_scan_fwd(u_d, wqk, attn_d, gb, nh, ex.weights_mid(1, u_d))
    mix = _gated_norm_fwd(o_d, proj, delta_norm_w, nh, p_z // HEAD_D, attn_out, aw // HEAD_D)
    w_o_g, w_up_g = ex.weights_finish(1, mix)
    w_o = w_o_g.reshape(D, D)
    mixed = _matmul(mix, w_o, NN, name="out_proj")
    h1 = _ln1_fwd(x, mixed, ln1_g, ln1_b)
    u = _matmul(h1, w_up_g, NN, name="mlp_up", b_groups=True, out_dtype=BF16, deps=(ex.weights_mid(2, h1),))
    (w_down_g,) = ex.weights_finish(2, u)
    w_down = w_down_g.reshape(-1, D)
    mlp = _matmul(u, w_down, NN, name="mlp_down", a_fn=_relu_sq, tk=4096)
    dr2, loss_row, dln2_g, dln2_b = _ln2_loss(h1, mlp, ln2_g, ln2_b, target)

    du = _matmul(dr2, w_down, NT, name="d_mlp_act", epi=_relu_sq_grad, epi_in=(u,), out_dtype=BF16)
    dw_down = _matmul(u, dr2, TN, name="dw_down", a_fn=_relu_sq, out_dtype=BF16)
    dw_up = _matmul(h1, du, TN, name="dw_up", out_dtype=BF16, out_groups=N_DEV)
    dh_mlp = _matmul(du, w_up_g, NT, name="d_h1", b_groups=True, tk=4096)
    dr1, dln1_g, dln1_b = _ln1_bwd(x, mixed, ln1_g, dr2, dh_mlp)
    dw_o = _matmul(mix, dr1, TN, name="dw_o", out_dtype=BF16)
    tok = ex.grads_start([dw_o.reshape(N_DEV, -1, D), dw_up, dw_down.reshape(N_DEV, -1, D)])
    dmix = _matmul(dr1, w_o, NT, name="d_mix", deps=(tok,))
    dq_a, dk_a, dv_a, dbias, dsink = _attn_bwd(proj, bias, sinks.reshape(-1), mix, dmix, hq, 0, c_k // kvw, c_v // kvw)
    drel = _rel_bias_grad(dbias, hq)
    do_d, dz, dnw = _gated_norm_bwd(o_d, proj, delta_norm_w, dmix, nh, p_z // HEAD_D, aw // HEAD_D)
    dvn_s, dw_s, dqd, dkd, dattn_d, dgl = _gdn_scan_bwd(wqk, attn_d, gb, states, vn, do_d, nh, ex.grads_mid(dq_a))
    dqkv_n, dg, dbeta = _gdn_local_bwd(qkv, gb, t_d, u_d, wqk, dvn_s, dw_s, dqd, dkd, dattn_d, dgl, nh)
    dgb = _pad_cols(jnp.concatenate([dg.reshape(nh, S).T, dbeta.reshape(nh, S).T], axis=1), LANE)
    dab, da_log, ddt_bias = _gates_bwd(ab, al, dt, dgb, nh)
    dqkv_d, dconv = _gdn_prep_bwd(proj, conv2, dqkv_n, nh, c_d // HEAD_D)
    dproj = jnp.concatenate([dq_a, dk_a, dv_a, dqkv_d, dz, dab], axis=1)
    dw_pt = _matmul(dproj, xb, TN, name="dw_in", out_dtype=BF16, tm=1152)

    def grad_rows(lo, hi):
        cuts = sorted({lo, hi, *[c for c in (c_ab, c_z) if lo < c < hi]})
        place = lambda r: r if r < c_ab else (p_ab + r - c_ab if r < c_z else p_z + r - c_z)
        return [dw_pt[place(a):place(a) + b - a] for a, b in zip(cuts[:-1], cuts[1:])]

    dw_in_g = jnp.stack([jnp.concatenate(grad_rows(g * n_slab, (g + 1) * n_slab), axis=0) for g in range(N_DEV)])
    tok = ex_in.grads_mid(ex_in.grads_start([dw_in_g]))
    grad_x = _matmul(dproj, w_pt, NN, name="d_x", tk=1920, deps=(tok,), epi=_residual_grad, epi_in=(dr1,))
    ex.grads_finish(grad_x)

    small = dict(conv_w=dconv, a_log=da_log[:, :nh], dt_bias=ddt_bias[:, :nh], delta_norm_w=dnw,
                 attn_sinks=dsink[:, 0].reshape(1, hq), rel_bias=drel[:, :, 0].T,
                 ln1_g=dln1_g, ln1_b=dln1_b, ln2_g=dln2_g, ln2_b=dln2_b)
    return loss_row, grad_x, small


SMALL_ORDER = ("conv_w", "a_log", "dt_bias", "delta_norm_w", "attn_sinks", "rel_bias", "ln1_g", "ln1_b", "ln2_g", "ln2_b")


def _pack_small(loss_row, small):
    parts = [loss_row.reshape(-1)]
    for k in SMALL_ORDER:
        flat = small[k].reshape(-1)
        parts.append(jnp.pad(flat, (0, (-flat.shape[0]) % LANE)))
    flat = jnp.concatenate(parts)
    flat = jnp.pad(flat, (0, (-flat.shape[0]) % (8 * LANE)))
    return flat.reshape(-1, LANE)


def _unpack_small(buf, small_shapes):
    flat = buf.reshape(-1)
    loss = flat[0]
    off = LANE
    out = {}
    for k in SMALL_ORDER:
        n = int(np.prod(small_shapes[k]))
        out[k] = flat[off:off + n].reshape(small_shapes[k])
        off += n + (-n) % LANE
    return loss, out


def kernel(x, w_in, conv_w, a_log, dt_bias, delta_norm_w, attn_sinks, rel_bias, w_o, ln1_g, ln1_b, w_up, w_down, ln2_g, ln2_b, loss_target, m_w_in, m_conv_w, m_a_log, m_dt_bias, m_delta_norm_w, m_attn_sinks, m_rel_bias, m_w_o, m_ln1_g, m_ln1_b, m_w_up, m_w_down, m_ln2_g, m_ln2_b, v_w_in, v_conv_w, v_a_log, v_dt_bias, v_delta_norm_w, v_attn_sinks, v_rel_bias, v_w_o, v_ln1_g, v_ln1_b, v_w_up, v_w_down, v_ln2_g, v_ln2_b):
    S, D = x.shape[1], x.shape[2]
    core = lax.axis_index("c")
    dev = 4 * lax.axis_index("x") + 2 * lax.axis_index("y") + core

    gather = _Gather([conv_w[0, :, 0, :], w_in[0].T.astype(BF16), w_o[0].astype(BF16), w_up[0].astype(BF16),
                      w_down[0].astype(BF16)], [[0, 1], [2, 3], [4]], dev)
    token = gather.start()
    bias = _attn_bias(rel_bias.T)
    w_t, m_t, v_t = w_in[0].T, m_w_in[0].T, v_w_in[0].T
    conv_g, w_in_g = gather.finish(0, gather.mid(0, (token, bias, w_t, m_t, v_t)))
    ex = _Exchanges("", 3, core, gather)
    ex_in = _Exchanges("in_", 1, core)

    cw_sh = conv_w.shape[3]
    conv_full = jnp.transpose(conv_g, (1, 0, 2)).reshape(CONV_W, N_DEV * cw_sh)

    loss_row, grad_x, small = _local_step(
        x[0], loss_target[0], w_in_g, conv_full, a_log, dt_bias, delta_norm_w, attn_sinks, bias,
        ln1_g, ln1_b, ln2_g, ln2_b, ex, ex_in)

    chip_arr = (dev // 2).reshape(1).astype(jnp.int32)
    big = {}
    for i, (name, w, m, v) in enumerate((("w_o", w_o, m_w_o, v_w_o), ("w_up", w_up, m_w_up, v_w_up),
                                         ("w_down", w_down, m_w_down, v_w_down))):
        big[name] = [o[None] for o in _adamw_big(ex.parts[i], ex.terms[i], chip_arr, w[0], m[0], v[0], "adamw_" + name)]
    ex_in.grads_finish(big["w_down"][0])
    outs = _adamw_big(ex_in.parts[0], ex_in.terms[0], chip_arr, w_t, m_t, v_t, "adamw_w_in")
    big["w_in"] = [o.T[None] for o in outs]

    small_shapes = {k: v.shape for k, v in small.items()}
    loss, small = _unpack_small(_all_reduce_small(_pack_small(loss_row, small)), small_shapes)
    small["conv_w"] = lax.dynamic_slice(small["conv_w"], (0, dev * cw_sh), (CONV_W, cw_sh))
    small["rel_bias"] = small["rel_bias"].reshape(rel_bias.shape)
    p2 = dict(conv_w=(conv_w, m_conv_w, v_conv_w), a_log=(a_log, m_a_log, v_a_log), dt_bias=(dt_bias, m_dt_bias, v_dt_bias),
              delta_norm_w=(delta_norm_w, m_delta_norm_w, v_delta_norm_w), attn_sinks=(attn_sinks, m_attn_sinks, v_attn_sinks),
              rel_bias=(rel_bias, m_rel_bias, v_rel_bias), ln1_g=(ln1_g, m_ln1_g, v_ln1_g), ln1_b=(ln1_b, m_ln1_b, v_ln1_b),
              ln2_g=(ln2_g, m_ln2_g, v_ln2_g), ln2_b=(ln2_b, m_ln2_b, v_ln2_b))
    two_d = lambda a: a.reshape(-1, a.shape[-1])
    ws = [two_d(p2[k][0]) for k in SMALL_ORDER]
    gs = [two_d(small[k]) for k in SMALL_ORDER]
    ms = [two_d(p2[k][1]) for k in SMALL_ORDER]
    vs = [two_d(p2[k][2]) for k in SMALL_ORDER]
    ds, nms, nvs = _adamw_small(ws, gs, ms, vs)
    res = {}
    for i, k in enumerate(SMALL_ORDER):
        shp = p2[k][0].shape
        res[k] = [gs[i].reshape(shp), ds[i].reshape(shp), nms[i].reshape(shp), nvs[i].reshape(shp)]
    res.update(big)
    order = ("w_in", "conv_w", "a_log", "dt_bias", "delta_norm_w", "attn_sinks", "rel_bias", "w_o", "ln1_g", "ln1_b",
             "w_up", "w_down", "ln2_g", "ln2_b")
    return (loss, grad_x[None], *[res[k][0] for k in order], *[res[k][1] for k in order],
            *[res[k][2] for k in order], *[res[k][3] for k in order])
```

```python
import functools
import math

import numpy as np
import jax
import jax.numpy as jnp
from jax import lax
from jax.experimental import pallas as pl
from jax.experimental.pallas import tpu as pltpu

F32 = jnp.float32
BF16 = jnp.bfloat16
HIGHEST = lax.Precision.HIGHEST

N_DEV = 8
HEAD_A = 64
GQA = 4
BLK = 128
N_BUCKETS = 32
MAX_DISTANCE = 128
HEAD_D = 128
CONV_W = 4
CHUNK = 64
NEG_INF = -1e30
LN_EPS = 1e-5
RMS_EPS = 1e-6
DN_ALPHA = 2.0 ** 0.25
ADAM_LR, ADAM_B1, ADAM_B2, ADAM_EPS, ADAM_WD, ADAM_STEP = 0.001, 0.9, 0.999, 1e-08, 0.01, 10

LANE = 128
VMEM_LIMIT = 56 * 1024 * 1024

NN = ((1,), (0,))
NT = ((1,), (1,))
TN = ((0,), (0,))


def _dot(a, b, dims, prec=None):
    return lax.dot_general(a, b, (dims, ((), ())), precision=prec, preferred_element_type=F32)


def _tile(dim, pref):
    if dim <= pref:
        return dim
    t = (pref // LANE) * LANE
    while t > LANE and dim % t:
        t -= LANE
    assert dim % t == 0, (dim, pref)
    return t


def _params(sem):
    return pltpu.CompilerParams(dimension_semantics=sem, vmem_limit_bytes=VMEM_LIMIT)


def _matmul(a, b, dims, *, name, out_dtype=F32, tm=1024, tn=1024, tk=2048, a_fn=None, epi=None, epi_in=(),
            b_groups=None, out_groups=None, deps=()):
    (ca,), (cb,) = dims
    M, K = a.shape[1 - ca], a.shape[ca]
    if b_groups:
        G, R, C = b.shape
        bshape = (R, G * C)
    else:
        bshape = b.shape
    N = bshape[1 - cb]
    assert bshape[cb] == K, (a.shape, b.shape, dims)
    tm, tk = _tile(M, tm), _tile(K, tk)
    if b_groups:
        lim = C if cb == 0 else tn
        tn = _tile(N, min(tn, lim))
        if cb == 1 and tk < C:
            tk = _tile(K, min(tk, C))
        elif cb == 1:
            tk = C * max(1, tk // C)
    else:
        tn = _tile(N, tn)
    if out_groups:
        tn = _tile(N, min(tn, N // out_groups))
    nk = K // tk
    b_span = tk // C if (b_groups and cb == 1 and tk > C) else 1

    def body(*refs):
        a_ref, b_ref = refs[0], refs[1]
        e_refs = refs[2:2 + len(epi_in)]
        o_ref = refs[2 + len(epi_in) + len(deps)]
        acc_ref = refs[3 + len(epi_in) + len(deps)] if nk > 1 else None
        k = pl.program_id(2)
        av = a_ref[...]
        if a_fn is not None:
            av = a_fn(av)
        if b_span > 1:
            prod = sum(_dot(av[:, g * C:(g + 1) * C].astype(BF16), b_ref[g].astype(BF16), dims) for g in range(b_span))
        else:
            prod = _dot(av.astype(BF16), b_ref[...].astype(BF16), dims)

        def finish(r):
            if epi is not None:
                r = epi(r, *[e[...] for e in e_refs])
            o_ref[...] = r.astype(out_dtype)

        if nk == 1:
            finish(prod)
            return

        @pl.when(k == 0)
        def _():
            acc_ref[...] = prod

        @pl.when(k > 0)
        def _():
            acc_ref[...] += prod

        @pl.when(k == nk - 1)
        def _():
            finish(acc_ref[...])

    a_spec = (pl.BlockSpec((tm, tk), lambda i, j, k: (i, k)) if ca == 1
              else pl.BlockSpec((tk, tm), lambda i, j, k: (k, i)))
    if b_groups:
        if cb == 0:
            per = C // tn
            b_spec = pl.BlockSpec((None, tk, tn), lambda i, j, k: (j // per, k, j % per))
        elif b_span > 1:
            b_spec = pl.BlockSpec((b_span, tn, C), lambda i, j, k: (k, j, 0))
        else:
            per = C // tk
            b_spec = pl.BlockSpec((None, tn, tk), lambda i, j, k: (k // per, j, k % per))
    else:
        b_spec = (pl.BlockSpec((tk, tn), lambda i, j, k: (k, j)) if cb == 0
                  else pl.BlockSpec((tn, tk), lambda i, j, k: (j, k)))
    e_specs = [pl.BlockSpec((tm, tn), lambda i, j, k: (i, j)) for _ in epi_in]
    if out_groups:
        per_o = (N // out_groups) // tn
        o_spec = pl.BlockSpec((None, tm, tn), lambda i, j, k: (j // per_o, i, j % per_o))
        o_shape = jax.ShapeDtypeStruct((out_groups, M, N // out_groups), out_dtype)
    else:
        o_spec = pl.BlockSpec((tm, tn), lambda i, j, k: (i, j))
        o_shape = jax.ShapeDtypeStruct((M, N), out_dtype)
    return pl.pallas_call(
        body, grid=(M // tm, N // tn, nk), out_specs=o_spec,
        in_specs=[a_spec, b_spec] + e_specs + [pl.BlockSpec(memory_space=pl.ANY)] * len(deps),
        out_shape=o_shape, scratch_shapes=[pltpu.VMEM((tm, tn), F32)] if nk > 1 else [],
        compiler_params=_params(("parallel", "parallel", "arbitrary")), name=name)(a, b, *epi_in, *deps)


def _relu_sq(u):
    r = jnp.maximum(u, 0.0)
    return r * r


def _relu_sq_grad(acc, u):
    return acc * (2.0 * jnp.maximum(u, 0.0))


def _ln_stats(r):
    mu = jnp.mean(r, axis=-1, keepdims=True)
    xc = r - mu
    var = jnp.mean(xc * xc, axis=-1, keepdims=True)
    rstd = lax.rsqrt(var + LN_EPS)
    return xc * rstd, rstd


def _ln_bwd(dy, xhat, rstd, g):
    dxh = dy * g
    m1 = jnp.mean(dxh, axis=-1, keepdims=True)
    m2 = jnp.mean(dxh * xhat, axis=-1, keepdims=True)
    return rstd * (dxh - m1 - xhat * m2)


def _row_call(body, ins, row_ins, outs, acc_outs, name, tr=256):
    S = ins[0].shape[0]
    tr = min(tr, S)
    n_in, n_row, n_out = len(ins), len(row_ins), len(outs)

    def wrapped(*refs):
        i = pl.program_id(0)
        acc_refs = refs[n_in + n_row + n_out:]

        @pl.when(i == 0)
        def _():
            for r in acc_refs:
                r[...] = jnp.zeros_like(r)

        body(*refs)

    in_specs = [pl.BlockSpec((tr, a.shape[1]), lambda i: (i, 0)) for a in ins]
    in_specs += [pl.BlockSpec(a.shape, lambda i: (0, 0)) for a in row_ins]
    out_specs = [pl.BlockSpec((tr, s.shape[1]), lambda i: (i, 0)) for s in outs]
    out_specs += [pl.BlockSpec(s.shape, lambda i: (0, 0)) for s in acc_outs]
    return pl.pallas_call(wrapped, grid=(S // tr,), in_specs=in_specs, out_specs=out_specs,
                          out_shape=list(outs) + list(acc_outs),
                          compiler_params=_params(("arbitrary",)), name=name)(*ins, *row_ins)


def _ln1_fwd(x, mixed, g, b):
    def body(x_ref, m_ref, g_ref, b_ref, h_ref):
        xhat, _ = _ln_stats(DN_ALPHA * x_ref[...] + m_ref[...])
        h_ref[...] = xhat * g_ref[...] + b_ref[...]
    return _row_call(body, [x, mixed], [g, b], [jax.ShapeDtypeStruct(x.shape, F32)], [], "ln1_fwd")[0]


def _ln2_loss(h1, mlp, g, b, target):
    S, D = h1.shape
    sds = jax.ShapeDtypeStruct

    def body(h_ref, m_ref, t_ref, g_ref, b_ref, dr_ref, loss_ref, dg_ref, db_ref):
        xhat, rstd = _ln_stats(DN_ALPHA * h_ref[...] + m_ref[...])
        gv = g_ref[...]
        err = xhat * gv + b_ref[...] - t_ref[...]
        loss_ref[...] += jnp.sum(jnp.sum(err * err, axis=0, keepdims=True), axis=1, keepdims=True) * (0.5 / D)
        dy = err * (1.0 / D)
        dg_ref[...] += jnp.sum(dy * xhat, axis=0, keepdims=True)
        db_ref[...] += jnp.sum(dy, axis=0, keepdims=True)
        dr_ref[...] = _ln_bwd(dy, xhat, rstd, gv)

    return _row_call(body, [h1, mlp, target], [g, b], [sds((S, D), F32)],
                     [sds((1, LANE), F32), sds((1, D), F32), sds((1, D), F32)], "ln2_loss")


def _ln1_bwd(x, mixed, g, dr2, dh_mlp):
    S, D = x.shape
    sds = jax.ShapeDtypeStruct

    def body(x_ref, m_ref, dr2_ref, dh_ref, g_ref, dr_ref, dg_ref, db_ref):
        xhat, rstd = _ln_stats(DN_ALPHA * x_ref[...] + m_ref[...])
        dy = DN_ALPHA * dr2_ref[...] + dh_ref[...]
        dg_ref[...] += jnp.sum(dy * xhat, axis=0, keepdims=True)
        db_ref[...] += jnp.sum(dy, axis=0, keepdims=True)
        dr_ref[...] = _ln_bwd(dy, xhat, rstd, g_ref[...])

    return _row_call(body, [x, mixed, dr2, dh_mlp], [g], [sds((S, D), F32)],
                     [sds((1, D), F32), sds((1, D), F32)], "ln1_bwd")


def _residual_grad(acc, dr):
    return DN_ALPHA * dr + acc


def _bucket_table():
    qi = np.arange(BLK, dtype=np.int32)[:, None]
    kj = np.arange(2 * BLK, dtype=np.int32)[None, :]
    dist = qi + BLK - kj
    n = np.maximum(dist, 0)
    max_exact = N_BUCKETS // 2
    nf = np.maximum(n, 1).astype(np.float32)
    large = max_exact + (np.log(nf / np.float32(max_exact)) / np.float32(math.log(MAX_DISTANCE / max_exact))
                         * np.float32(N_BUCKETS - max_exact)).astype(np.int32)
    large = np.minimum(large, N_BUCKETS - 1)
    bucket = np.where(n < max_exact, n, large)
    return np.where((dist >= 0) & (dist < BLK), bucket, -1).astype(np.int32)


def _attn_bias(rel_bias_t):
    hq = rel_bias_t.shape[0]
    bucket = jnp.asarray(_bucket_table())

    def body(rb_ref, bk_ref, o_ref):
        h = pl.program_id(0)
        bk = bk_ref[...]
        acc = jnp.zeros((BLK, 2 * BLK), F32)
        for b in range(N_BUCKETS):
            acc = jnp.where(bk == b, rb_ref[h, b], acc)
        o_ref[...] = acc

    return pl.pallas_call(
        body, grid=(hq,),
        in_specs=[pl.BlockSpec(memory_space=pltpu.SMEM), pl.BlockSpec((BLK, 2 * BLK), lambda h: (0, 0))],
        out_specs=pl.BlockSpec((BLK, 2 * BLK), lambda h: (h, 0)),
        out_shape=jax.ShapeDtypeStruct((hq * BLK, 2 * BLK), F32),
        compiler_params=_params(("arbitrary",)), name="attn_bias")(rel_bias_t, bucket)


def _attn_probs(sc, sp, bias, sink, mask_c, mask_p):
    lc = jnp.where(mask_c, sc + bias[:, BLK:], NEG_INF)
    lp = jnp.where(mask_p, sp + bias[:, :BLK], NEG_INF)
    m = jnp.maximum(jnp.maximum(jnp.max(lc, axis=1, keepdims=True), jnp.max(lp, axis=1, keepdims=True)), sink)
    pc, pp, ps = jnp.exp(lc - m), jnp.exp(lp - m), jnp.exp(sink - m)
    inv = 1.0 / (jnp.sum(pc, axis=1, keepdims=True) + jnp.sum(pp, axis=1, keepdims=True) + ps)
    return pc, pp, ps, inv


def _attn_masks(n):
    qi = lax.broadcasted_iota(jnp.int32, (BLK, BLK), 0)
    kj = lax.broadcasted_iota(jnp.int32, (BLK, BLK), 1)
    return kj <= qi, (kj > qi) & (n > 0)


def _attn_fwd(proj, bias, sinks, hq, q_blk, k_blk, v_blk, out_width):
    S = proj.shape[0]
    hkv = hq // GQA
    wq, wk = hq * HEAD_A, hkv * HEAD_A

    def body(q_ref, k_ref, v_ref, bias_ref, sink_ref, o_ref):
        n = pl.program_id(0)
        cur = pl.multiple_of(n * BLK, BLK)
        prev = pl.multiple_of(jnp.maximum(n - 1, 0) * BLK, BLK)
        mask_c, mask_p = _attn_masks(n)
        for h4 in range(hkv):
            cs = slice(h4 * HEAD_A, (h4 + 1) * HEAD_A)
            kc, kp = k_ref[pl.ds(cur, BLK), cs].astype(BF16), k_ref[pl.ds(prev, BLK), cs].astype(BF16)
            vc, vp = v_ref[pl.ds(cur, BLK), cs].astype(BF16), v_ref[pl.ds(prev, BLK), cs].astype(BF16)
            hs_of = [slice(h * HEAD_A, (h + 1) * HEAD_A) for h in range(h4 * GQA, (h4 + 1) * GQA)]
            qs = [(q_ref[:, hs] * (HEAD_A ** -0.5)).astype(BF16) for hs in hs_of]
            scs = [_dot(q, kc, NT) for q in qs]
            sps = [_dot(q, kp, NT) for q in qs]
            pr = [_attn_probs(scs[g], sps[g], bias_ref[(h4 * GQA + g) * BLK:(h4 * GQA + g + 1) * BLK, :],
                              sink_ref[h4 * GQA + g], mask_c, mask_p) for g in range(GQA)]
            oc = [_dot(p[0].astype(BF16), vc, NN) for p in pr]
            op = [_dot(p[1].astype(BF16), vp, NN) for p in pr]
            for g, hs in enumerate(hs_of):
                o_ref[:, hs] = (oc[g] + op[g]) * pr[g][3]

    return pl.pallas_call(
        body, grid=(S // BLK,),
        in_specs=[pl.BlockSpec((BLK, wq), lambda n: (n, q_blk)), pl.BlockSpec((S, wk), lambda n: (0, k_blk)),
                  pl.BlockSpec((S, wk), lambda n: (0, v_blk)), pl.BlockSpec((hq * BLK, 2 * BLK), lambda n: (0, 0)),
                  pl.BlockSpec(memory_space=pltpu.SMEM)],
        out_specs=pl.BlockSpec((BLK, wq), lambda n: (n, 0)),
        out_shape=jax.ShapeDtypeStruct((S, out_width), F32),
        compiler_params=_params(("arbitrary",)), name="attn_fwd")(proj, proj, proj, bias, sinks)


def _attn_bwd(proj, bias, sinks, out, dmix, hq, q_blk, k_blk, v_blk):
    S = proj.shape[0]
    hkv = hq // GQA
    wq, wk = hq * HEAD_A, hkv * HEAD_A
    sds = jax.ShapeDtypeStruct

    def body(q_ref, k_ref, v_ref, bias_ref, sink_ref, o_ref, do_ref, dq_ref, dk_ref, dv_ref, dbias_ref, dsink_ref):
        n = pl.program_id(0)

        @pl.when(n == 0)
        def _():
            dk_ref[...] = jnp.zeros_like(dk_ref)
            dv_ref[...] = jnp.zeros_like(dv_ref)
            dbias_ref[...] = jnp.zeros_like(dbias_ref)
            dsink_ref[...] = jnp.zeros_like(dsink_ref)

        cur = pl.multiple_of(n * BLK, BLK)
        prev = pl.multiple_of(jnp.maximum(n - 1, 0) * BLK, BLK)
        mask_c, mask_p = _attn_masks(n)
        for h4 in range(hkv):
            cs = slice(h4 * HEAD_A, (h4 + 1) * HEAD_A)
            kc, kp = k_ref[pl.ds(cur, BLK), cs].astype(BF16), k_ref[pl.ds(prev, BLK), cs].astype(BF16)
            vc, vp = v_ref[pl.ds(cur, BLK), cs].astype(BF16), v_ref[pl.ds(prev, BLK), cs].astype(BF16)
            heads = list(range(h4 * GQA, (h4 + 1) * GQA))
            hs_of = [slice(h * HEAD_A, (h + 1) * HEAD_A) for h in heads]
            rows_of = [slice(h * BLK, (h + 1) * BLK) for h in heads]
            G = range(GQA)
            qs = [(q_ref[:, hs] * (HEAD_A ** -0.5)).astype(BF16) for hs in hs_of]
            dos = [do_ref[:, hs] for hs in hs_of]
            dobs = [d.astype(BF16) for d in dos]
            scs = [_dot(q, kc, NT) for q in qs]
            sps = [_dot(q, kp, NT) for q in qs]
            dpc = [_dot(d, vc, NT) for d in dobs]
            dpp = [_dot(d, vp, NT) for d in dobs]
            pcs, pps, dscs, dsps = [], [], [], []
            for g in G:
                pc, pp, ps, inv = _attn_probs(scs[g], sps[g], bias_ref[rows_of[g], :], sink_ref[heads[g]], mask_c, mask_p)
                pc, pp, ps = pc * inv, pp * inv, ps * inv
                delta = jnp.sum(dos[g] * o_ref[:, hs_of[g]], axis=1, keepdims=True)
                dsc, dsp = pc * (dpc[g] - delta), pp * (dpp[g] - delta)
                dsink_ref[heads[g]:heads[g] + 1, :] += jnp.broadcast_to(jnp.sum(-ps * delta, axis=0, keepdims=True), (1, LANE))
                dbias_ref[rows_of[g], BLK:] += dsc
                dbias_ref[rows_of[g], :BLK] += dsp
                pcs.append(pc.astype(BF16))
                pps.append(pp.astype(BF16))
                dscs.append(dsc.astype(BF16))
                dsps.append(dsp.astype(BF16))
            dq1 = [_dot(dscs[g], kc, NN) for g in G]
            dq2 = [_dot(dsps[g], kp, NN) for g in G]
            dkc = [_dot(dscs[g], qs[g], TN) for g in G]
            dkp = [_dot(dsps[g], qs[g], TN) for g in G]
            dvc = [_dot(pcs[g], dobs[g], TN) for g in G]
            dvp = [_dot(pps[g], dobs[g], TN) for g in G]
            for g in G:
                dq_ref[:, hs_of[g]] = (dq1[g] + dq2[g]) * (HEAD_A ** -0.5)
            dk_ref[pl.ds(cur, BLK), cs] += sum(dkc[1:], dkc[0])
            dk_ref[pl.ds(prev, BLK), cs] += sum(dkp[1:], dkp[0])
            dv_ref[pl.ds(cur, BLK), cs] += sum(dvc[1:], dvc[0])
            dv_ref[pl.ds(prev, BLK), cs] += sum(dvp[1:], dvp[0])

    return pl.pallas_call(
        body, grid=(S // BLK,),
        in_specs=[pl.BlockSpec((BLK, wq), lambda n: (n, q_blk)), pl.BlockSpec((S, wk), lambda n: (0, k_blk)),
                  pl.BlockSpec((S, wk), lambda n: (0, v_blk)), pl.BlockSpec((hq * BLK, 2 * BLK), lambda n: (0, 0)),
                  pl.BlockSpec(memory_space=pltpu.SMEM),
                  pl.BlockSpec((BLK, wq), lambda n: (n, 0)), pl.BlockSpec((BLK, wq), lambda n: (n, 0))],
        out_specs=[pl.BlockSpec((BLK, wq), lambda n: (n, 0)), pl.BlockSpec((S, wk), lambda n: (0, 0)),
                   pl.BlockSpec((S, wk), lambda n: (0, 0)), pl.BlockSpec((hq * BLK, 2 * BLK), lambda n: (0, 0)),
                   pl.BlockSpec((hq, LANE), lambda n: (0, 0))],
        out_shape=[sds((S, wq), F32), sds((S, wk), F32), sds((S, wk), F32), sds((hq * BLK, 2 * BLK), F32),
                   sds((hq, LANE), F32)],
        compiler_params=_params(("arbitrary",)), name="attn_bwd")(proj, proj, proj, bias, sinks, out, dmix)


def _rel_bias_grad(dbias, hq):
    bucket = jnp.asarray(_bucket_table())

    def body(d_ref, bk_ref, o_ref):
        d = d_ref[...]
        bk = bk_ref[...]
        rows = [jnp.sum(jnp.where(bk == b, d, 0.0), axis=0, keepdims=True) for b in range(N_BUCKETS)]
        tot = jnp.sum(jnp.concatenate(rows, axis=0), axis=1, keepdims=True)
        o_ref[...] = jnp.broadcast_to(tot, (N_BUCKETS, LANE))

    return pl.pallas_call(
        body, grid=(hq,),
        in_specs=[pl.BlockSpec((BLK, 2 * BLK), lambda h: (h, 0)), pl.BlockSpec((BLK, 2 * BLK), lambda h: (0, 0))],
        out_specs=pl.BlockSpec((None, N_BUCKETS, LANE), lambda h: (h, 0, 0)),
        out_shape=jax.ShapeDtypeStruct((hq, N_BUCKETS, LANE), F32),
        compiler_params=_params(("arbitrary",)), name="rel_bias_grad")(dbias, bucket)


def _sigmoid(x):
    return 0.5 * jnp.tanh(0.5 * x) + 0.5


def _shift_rows(x, s, row):
    n = x.shape[0]
    if s > 0:
        return jnp.where(row >= s, pltpu.roll(x, s, 0), 0.0)
    return jnp.where(row < n + s, pltpu.roll(x, n + s, 0), 0.0)


def _conv_silu_norm(xv, w, j, nh):
    row = lax.broadcasted_iota(jnp.int32, xv.shape, 0)
    xs = [xv] + [_shift_rows(xv, s, row) for s in range(1, CONV_W)]
    c = w[CONV_W - 1:CONV_W, :] * xv
    for s in range(1, CONV_W):
        c = c + w[CONV_W - 1 - s:CONV_W - s, :] * xs[s]
    sg = _sigmoid(c)
    a = c * sg
    r = lax.rsqrt(jnp.sum(a * a, axis=1, keepdims=True) + RMS_EPS)
    scale = jnp.where(j < nh, HEAD_D ** -0.5, 1.0)
    is_norm = j < 2 * nh
    y = jnp.where(is_norm, a * (r * scale), a)
    return c, sg, a, r, scale, is_norm, xs, row, y


def _gdn_prep_fwd(proj, conv_w, nh, blk0):
    S = proj.shape[0]

    def body(x_ref, w_ref, o_ref):
        j = pl.program_id(0)
        o_ref[...] = _conv_silu_norm(x_ref[...], w_ref[...], j, nh)[-1]

    return pl.pallas_call(
        body, grid=(3 * nh,),
        in_specs=[pl.BlockSpec((S, HEAD_D), lambda j: (0, blk0 + j)), pl.BlockSpec((CONV_W, HEAD_D), lambda j: (0, j))],
        out_specs=pl.BlockSpec((S, HEAD_D), lambda j: (0, 3 * (j % nh) + j // nh)),
        out_shape=jax.ShapeDtypeStruct((S, 3 * nh * HEAD_D), F32),
        compiler_params=_params(("parallel",)), name="gdn_prep_fwd")(proj, conv_w)


def _gdn_prep_bwd(proj, conv_w, dqkv, nh, blk0):
    S = proj.shape[0]
    sds = jax.ShapeDtypeStruct

    def body(x_ref, w_ref, dy_ref, dx_ref, dw_ref):
        j = pl.program_id(0)
        xv, w = x_ref[...], w_ref[...]
        c, sg, a, r, scale, is_norm, xs, row, _ = _conv_silu_norm(xv, w, j, nh)
        dy = dy_ref[...]
        rs = r * scale
        da_n = rs * dy - a * (r * r * rs) * jnp.sum(dy * a, axis=1, keepdims=True)
        da = jnp.where(is_norm, da_n, dy)
        dc = da * (sg * (1.0 + c * (1.0 - sg)))
        dx = w[CONV_W - 1:CONV_W, :] * dc
        dws = [jnp.sum(dc * xv, axis=0, keepdims=True)]
        for s in range(1, CONV_W):
            dx = dx + w[CONV_W - 1 - s:CONV_W - s, :] * _shift_rows(dc, -s, row)
            dws.insert(0, jnp.sum(dc * xs[s], axis=0, keepdims=True))
        dx_ref[...] = dx
        dw_ref[...] = jnp.concatenate(dws, axis=0)

    return pl.pallas_call(
        body, grid=(3 * nh,),
        in_specs=[pl.BlockSpec((S, HEAD_D), lambda j: (0, blk0 + j)), pl.BlockSpec((CONV_W, HEAD_D), lambda j: (0, j)),
                  pl.BlockSpec((S, HEAD_D), lambda j: (0, 3 * (j % nh) + j // nh))],
        out_specs=[pl.BlockSpec((S, HEAD_D), lambda j: (0, j)), pl.BlockSpec((CONV_W, HEAD_D), lambda j: (0, j))],
        out_shape=[sds((S, 3 * nh * HEAD_D), F32), sds((CONV_W, 3 * nh * HEAD_D), F32)],
        compiler_params=_params(("parallel",)), name="gdn_prep_bwd")(proj, conv_w, dqkv)


def _softplus(x):
    return jnp.maximum(x, 0.0) + jnp.log(1.0 + jnp.exp(-jnp.abs(x)))


def _gates_fwd(ab, al, dt, nh):
    S = ab.shape[0]

    def body(ab_ref, al_ref, dt_ref, o_ref):
        v = ab_ref[...]
        lane = lax.broadcasted_iota(jnp.int32, v.shape, 1)
        g = -jnp.exp(al_ref[...]) * _softplus(v + dt_ref[...])
        o_ref[...] = jnp.where(lane < nh, g, jnp.where(lane < 2 * nh, _sigmoid(v), 0.0))

    row = pl.BlockSpec((1, LANE), lambda i: (0, 0))
    full = pl.BlockSpec((S, LANE), lambda i: (0, 0))
    return pl.pallas_call(body, grid=(1,), in_specs=[full, row, row], out_specs=full,
                          out_shape=jax.ShapeDtypeStruct((S, LANE), F32),
                          compiler_params=_params(("arbitrary",)), name="gates_fwd")(ab, al, dt)


def _gates_bwd(ab, al, dt, dgb, nh):
    S = ab.shape[0]
    sds = jax.ShapeDtypeStruct

    def body(ab_ref, al_ref, dt_ref, d_ref, dab_ref, dal_ref, ddt_ref):
        v, d = ab_ref[...], d_ref[...]
        lane = lax.broadcasted_iota(jnp.int32, v.shape, 1)
        is_a = lane < nh
        z = v + dt_ref[...]
        dsp = jnp.where(is_a, d * (-jnp.exp(al_ref[...])), 0.0)
        dz = dsp * _sigmoid(z)
        beta = _sigmoid(v)
        dab_ref[...] = jnp.where(is_a, dz, jnp.where(lane < 2 * nh, d * beta * (1.0 - beta), 0.0))
        dal_ref[...] = jnp.sum(dsp * _softplus(z), axis=0, keepdims=True)
        ddt_ref[...] = jnp.sum(dz, axis=0, keepdims=True)

    row = pl.BlockSpec((1, LANE), lambda i: (0, 0))
    full = pl.BlockSpec((S, LANE), lambda i: (0, 0))
    return pl.pallas_call(body, grid=(1,), in_specs=[full, row, row, full], out_specs=[full, row, row],
                          out_shape=[sds((S, LANE), F32), sds((1, LANE), F32), sds((1, LANE), F32)],
                          compiler_params=_params(("arbitrary",)), name="gates_bwd")(ab, al, dt, dgb)


def _col_of(tile, h):
    lane = lax.broadcasted_iota(jnp.int32, tile.shape, 1)
    return jnp.sum(jnp.where(lane == h, tile, 0.0), axis=1, keepdims=True)


def _to_row(col, eye):
    return jnp.sum(jnp.where(eye, col, 0.0), axis=0, keepdims=True)


def _to_col(row, eye):
    return jnp.sum(jnp.where(eye, row, 0.0), axis=1, keepdims=True)


def _split(a):
    hi = a.astype(BF16)
    return hi, (a - hi.astype(F32)).astype(BF16)


def _gdot(a, b, dims):
    ah, al = _split(a)
    bh, bl = _split(b)
    return _dot(ah, bh, dims) + (_dot(ah, bl, dims) + _dot(al, bh, dims))


def _bdot(a, b, dims):
    return _dot(a.astype(BF16), b.astype(BF16), dims)


def _chunks_local(qs, ks, vs, gcols, bcols, Ts=None):
    C = CHUNK
    row = lax.broadcasted_iota(jnp.int32, (C, C), 0)
    col = lax.broadcasted_iota(jnp.int32, (C, C), 1)
    tril, strict, eye = col <= row, col < row, col == row
    outs = []
    for k, gcol, bcol in zip(ks, gcols, bcols):
        grow = _to_row(gcol, eye)
        G_row = jnp.sum(jnp.where(row <= col, gcol, 0.0), axis=0, keepdims=True)
        G_col = jnp.sum(jnp.where(tril, grow, 0.0), axis=1, keepdims=True)
        G_last = G_col[C - 1:C, :]
        outs.append(dict(strict=strict, eye=eye, row=row, col=col, decay=jnp.exp(jnp.where(tril, G_col - G_row, NEG_INF)),
                         eG=jnp.exp(G_col), eGr=jnp.exp(G_last - G_col), gl=jnp.exp(G_last), kb=k * bcol))
    Ms = [_gdot(o["kb"], k, NT) for o, k in zip(outs, ks)]
    Ns = [_gdot(q, k, NT) for q, k in zip(qs, ks)]
    for o, q, k, M, N in zip(outs, qs, ks, Ms, Ns):
        o.update(A=jnp.where(strict, M * o["decay"], 0.0), attn=N * o["decay"], rhs_k=o["kb"] * o["eG"],
                 q_dec=q * o["eG"], k_dec=k * o["eGr"])
    if Ts is None:
        Ts = [jnp.where(eye, 1.0, 0.0) - o["A"] for o in outs]
        Ps = [o["A"] for o in outs]
        for _ in range(int(math.log2(C)) - 1):
            Ps = [_bdot(P, P, NN) for P in Ps]
            Ts = [T + _bdot(T, P, NN) for T, P in zip(Ts, Ps)]
        us = [_bdot(T, v * bcol, NN) for T, v, bcol in zip(Ts, vs, bcols)]
        ws = [_bdot(T, o["rhs_k"], NN) for T, o in zip(Ts, outs)]
        for o, T, u, w in zip(outs, Ts, us, ws):
            o.update(T=T, u=u, w=w)
    return outs


GDN_ROWS = 256
GDN_LOCAL_ROWS = 512
WQK = 3 * CHUNK


def _gdn_local_fwd(qkv, gb, nh):
    S = qkv.shape[0]
    nc = S // CHUNK
    rb = min(GDN_LOCAL_ROWS, S)
    cpb = rb // CHUNK
    sds = jax.ShapeDtypeStruct

    def body(q_ref, k_ref, v_ref, gb_ref, u_ref, wqk_ref, attn_ref, t_ref):
        h = pl.program_id(0)
        rows_of = [slice(ci * CHUNK, (ci + 1) * CHUNK) for ci in range(cpb)]
        gbts = [gb_ref[rows, :] for rows in rows_of]
        Ls = _chunks_local([q_ref[rows, :] for rows in rows_of], [k_ref[rows, :] for rows in rows_of],
                           [v_ref[rows, :] for rows in rows_of], [_col_of(t, h) for t in gbts],
                           [_col_of(t, nh + h) for t in gbts])
        for ci, (rows, L) in enumerate(zip(rows_of, Ls)):
            u_ref[rows, :] = L["u"]
            base = ci * WQK
            wqk_ref[base:base + CHUNK, :] = L["w"]
            wqk_ref[base + CHUNK:base + 2 * CHUNK, :] = L["q_dec"]
            wqk_ref[base + 2 * CHUNK:base + WQK, :] = L["k_dec"]
            attn_ref[ci] = L["attn"]
            t_ref[ci] = L["T"]

    cc = pl.BlockSpec((None, cpb, CHUNK, CHUNK), lambda h, i: (h, i, 0, 0))
    return pl.pallas_call(
        body, grid=(nh, S // rb),
        in_specs=[pl.BlockSpec((rb, HEAD_D), lambda h, i: (i, 3 * h)), pl.BlockSpec((rb, HEAD_D), lambda h, i: (i, 3 * h + 1)),
                  pl.BlockSpec((rb, HEAD_D), lambda h, i: (i, 3 * h + 2)), pl.BlockSpec((rb, LANE), lambda h, i: (i, 0))],
        out_specs=[pl.BlockSpec((rb, HEAD_D), lambda h, i: (i, h)),
                   pl.BlockSpec((None, 3 * rb, HEAD_D), lambda h, i: (h, i, 0)), cc, cc],
        out_shape=[sds((S, nh * HEAD_D), F32), sds((nh, 3 * S, HEAD_D), F32), sds((nh, nc, CHUNK, CHUNK), F32),
                   sds((nh, nc, CHUNK, CHUNK), F32)],
        compiler_params=_params(("parallel", "parallel")), name="gdn_local_fwd")(qkv, qkv, qkv, gb)


def _gdn_scan_fwd(u, wqk, attn, gb, nh, dep):
    S = u.shape[0]
    nc = S // CHUNK
    rb = min(GDN_ROWS, S)
    cpb = rb // CHUNK
    sds = jax.ShapeDtypeStruct

    def body(u_ref, wqk_ref, attn_ref, gb_ref, dep_ref, o_ref, vn_ref, st_ref, s_ref):
        @pl.when(pl.program_id(0) == 0)
        def _():
            s_ref[...] = jnp.zeros_like(s_ref)

        for ci in range(cpb):
            rows = slice(ci * CHUNK, (ci + 1) * CHUNK)
            glv = jnp.exp(jnp.sum(gb_ref[rows, :], axis=0, keepdims=True))
            base = ci * WQK
            heads = range(nh)
            cols = [slice(h * HEAD_D, (h + 1) * HEAD_D) for h in heads]
            states = [s_ref[h] for h in heads]
            rs = [_gdot(wqk_ref[h, base:base + 2 * CHUNK, :], states[h], NN) for h in heads]
            vbs = [u_ref[rows, cols[h]] - rs[h][:CHUNK] for h in heads]
            os_ = [_gdot(attn_ref[h, ci], vbs[h], NN) for h in heads]
            ks_ = [_gdot(wqk_ref[h, base + 2 * CHUNK:base + WQK, :], vbs[h], TN) for h in heads]
            for h in heads:
                st_ref[h, ci] = states[h]
                o_ref[rows, cols[h]] = rs[h][CHUNK:] + os_[h]
                vn_ref[rows, cols[h]] = vbs[h]
                s_ref[h] = states[h] * glv[:, h:h + 1] + ks_[h]

    return pl.pallas_call(
        body, grid=(S // rb,),
        in_specs=[pl.BlockSpec((rb, nh * HEAD_D), lambda i: (i, 0)), pl.BlockSpec((nh, 3 * rb, HEAD_D), lambda i: (0, i, 0)),
                  pl.BlockSpec((nh, cpb, CHUNK, CHUNK), lambda i: (0, i, 0, 0)), pl.BlockSpec((rb, LANE), lambda i: (i, 0)),
                  pl.BlockSpec(memory_space=pl.ANY)],
        out_specs=[pl.BlockSpec((rb, nh * HEAD_D), lambda i: (i, 0)), pl.BlockSpec((rb, nh * HEAD_D), lambda i: (i, 0)),
                   pl.BlockSpec((nh, cpb, HEAD_D, HEAD_D), lambda i: (0, i, 0, 0))],
        out_shape=[sds((S, nh * HEAD_D), F32), sds((S, nh * HEAD_D), F32), sds((nh, nc, HEAD_D, HEAD_D), F32)],
        scratch_shapes=[pltpu.VMEM((nh, HEAD_D, HEAD_D), F32)],
        compiler_params=_params(("arbitrary",)), name="gdn_scan_fwd")(u, wqk, attn, gb, dep)


def _gdn_scan_bwd(wqk, attn, gb, states, vn, do, nh, dep):
    S = vn.shape[0]
    nc = S // CHUNK
    rb = min(GDN_ROWS, S)
    cpb = rb // CHUNK
    last = S // rb - 1
    sds = jax.ShapeDtypeStruct

    def body(wqk_ref, attn_ref, gb_ref, st_ref, vn_ref, do_ref, dep_ref, dvn_ref, dw_ref, dqd_ref, dkd_ref, da_ref, dgl_ref,
             ds_ref):
        @pl.when(pl.program_id(0) == 0)
        def _():
            ds_ref[...] = jnp.zeros_like(ds_ref)

        row = lax.broadcasted_iota(jnp.int32, (CHUNK, CHUNK), 0)
        col = lax.broadcasted_iota(jnp.int32, (CHUNK, CHUNK), 1)
        for ci in reversed(range(cpb)):
            rows = slice(ci * CHUNK, (ci + 1) * CHUNK)
            glv = jnp.exp(jnp.sum(gb_ref[rows, :], axis=0, keepdims=True))
            base = ci * WQK
            heads = range(nh)
            cols = [slice(h * HEAD_D, (h + 1) * HEAD_D) for h in heads]
            states = [st_ref[h, ci] for h in heads]
            dSs = [ds_ref[h] for h in heads]
            vbs = [vn_ref[rows, cols[h]] for h in heads]
            dobs = [do_ref[rows, cols[h]] for h in heads]
            dv1 = [_gdot(attn_ref[h, ci], dobs[h], TN) for h in heads]
            dv2 = [_gdot(wqk_ref[h, base + 2 * CHUNK:base + WQK, :], dSs[h], NN) for h in heads]
            das = [_gdot(dobs[h], vbs[h], NT) for h in heads]
            dkds = [_gdot(vbs[h], dSs[h], NT) for h in heads]
            dvbs = [dv1[h] + dv2[h] for h in heads]
            xs = [_gdot(jnp.concatenate([dobs[h], dvbs[h]], axis=0), states[h], NT) for h in heads]
            dss = [_gdot(wqk_ref[h, base:base + 2 * CHUNK, :], jnp.concatenate([-dvbs[h], dobs[h]], axis=0), TN)
                   for h in heads]
            for h in heads:
                dqd_ref[rows, cols[h]] = xs[h][:CHUNK]
                dw_ref[rows, cols[h]] = -xs[h][CHUNK:]
                dvn_ref[rows, cols[h]] = dvbs[h]
                da_ref[h, ci] = jnp.where(col <= row, das[h], 0.0)
                dkd_ref[rows, cols[h]] = dkds[h]
                gl = glv[:, h:h + 1]
                dgl = jnp.sum(jnp.sum(states[h] * dSs[h], axis=0, keepdims=True), axis=1, keepdims=True)
                dgl_ref[h, ci] = jnp.broadcast_to(dgl * gl, (1, LANE))
                ds_ref[h] = dSs[h] * gl + dss[h]

    rv = lambda i: last - i
    wide = pl.BlockSpec((rb, nh * HEAD_D), lambda i: (rv(i), 0))
    return pl.pallas_call(
        body, grid=(S // rb,),
        in_specs=[pl.BlockSpec((nh, 3 * rb, HEAD_D), lambda i: (0, rv(i), 0)),
                  pl.BlockSpec((nh, cpb, CHUNK, CHUNK), lambda i: (0, rv(i), 0, 0)),
                  pl.BlockSpec((rb, LANE), lambda i: (rv(i), 0)),
                  pl.BlockSpec((nh, cpb, HEAD_D, HEAD_D), lambda i: (0, rv(i), 0, 0)), wide, wide,
                  pl.BlockSpec(memory_space=pl.ANY)],
        out_specs=[wide, wide, wide, wide, pl.BlockSpec((nh, cpb, CHUNK, CHUNK), lambda i: (0, rv(i), 0, 0)),
                   pl.BlockSpec((nh, cpb, 1, LANE), lambda i: (0, rv(i), 0, 0))],
        out_shape=[sds((S, nh * HEAD_D), F32), sds((S, nh * HEAD_D), F32), sds((S, nh * HEAD_D), F32),
                   sds((S, nh * HEAD_D), F32), sds((nh, nc, CHUNK, CHUNK), F32), sds((nh, nc, 1, LANE), F32)],
        scratch_shapes=[pltpu.VMEM((nh, HEAD_D, HEAD_D), F32)],
        compiler_params=_params(("arbitrary",)), name="gdn_scan_bwd")(wqk, attn, gb, states, vn, do, dep)


def _gdn_local_bwd(qkv, gb, T, u, wqk, dvn, dw, dqd, dkd, dattn, dgl, nh):
    S = qkv.shape[0]
    rb = min(GDN_LOCAL_ROWS, S)
    cpb = rb // CHUNK
    sds = jax.ShapeDtypeStruct

    def body(q_ref, k_ref, v_ref, gb_ref, t_ref, u_ref, wqk_ref, dvn_ref, dw_ref, dqd_ref, dkd_ref, da_ref, dgl_ref,
             dqkv_ref, dg_ref, db_ref):
        h = pl.program_id(0)
        n = range(cpb)
        rows_of = [slice(ci * CHUNK, (ci + 1) * CHUNK) for ci in n]
        qs, ks, vs = ([r[rows, :] for rows in rows_of] for r in (q_ref, k_ref, v_ref))
        gbts = [gb_ref[rows, :] for rows in rows_of]
        bcols = [_col_of(t, nh + h) for t in gbts]
        Ts = [t_ref[ci] for ci in n]
        Ls = _chunks_local(qs, ks, vs, [_col_of(t, h) for t in gbts], bcols, Ts=Ts)
        drvs = [_gdot(Ts[ci], dvn_ref[rows_of[ci], :], TN) for ci in n]
        drks = [_gdot(Ts[ci], dw_ref[rows_of[ci], :], TN) for ci in n]
        dAs = [jnp.where(Ls[ci]["strict"], -(_gdot(drvs[ci], u_ref[rows_of[ci], :], NT)
                                             + _gdot(drks[ci], wqk_ref[ci * WQK:ci * WQK + CHUNK, :], NT)), 0.0) for ci in n]
        dMs = [dAs[ci] * Ls[ci]["decay"] for ci in n]
        dNs = [da_ref[ci] * Ls[ci]["decay"] for ci in n]
        dkbs = [_gdot(dMs[ci], ks[ci], NN) for ci in n]
        dq1 = [_gdot(dNs[ci], ks[ci], NN) for ci in n]
        dk1 = [_gdot(dMs[ci], Ls[ci]["kb"], TN) for ci in n]
        dk2 = [_gdot(dNs[ci], qs[ci], TN) for ci in n]
        for ci in n:
            rows, L, q, k, v, bcol = rows_of[ci], Ls[ci], qs[ci], ks[ci], vs[ci], bcols[ci]
            eye, eG, eGr = L["eye"], L["eG"], L["eGr"]
            drv, drk, dkb = drvs[ci], drks[ci], dkbs[ci]
            dq_dec, dk_dec, dattn_c = dqd_ref[rows, :], dkd_ref[rows, :], da_ref[ci]
            dqkv_ref[rows, :HEAD_D] = dq1[ci] + dq_dec * eG
            dqkv_ref[rows, HEAD_D:2 * HEAD_D] = drk * (bcol * eG) + dk1[ci] + dkb * bcol + dk2[ci] + dk_dec * eGr
            dqkv_ref[rows, 2 * HEAD_D:] = drv * bcol
            db_ref[rows, :] = (jnp.sum(drv * v, axis=1, keepdims=True) + jnp.sum(drk * k, axis=1, keepdims=True) * eG
                               + jnp.sum(dkb * k, axis=1, keepdims=True))
            E = dAs[ci] * L["A"] + dattn_c * L["attn"]
            kd = jnp.sum(dk_dec * L["k_dec"], axis=1, keepdims=True)
            dG = (jnp.sum(dq_dec * L["q_dec"], axis=1, keepdims=True) - kd
                  + jnp.sum(drk * L["rhs_k"], axis=1, keepdims=True)
                  + jnp.sum(E, axis=1, keepdims=True) - _to_col(jnp.sum(E, axis=0, keepdims=True), eye))
            d_last = jnp.sum(kd, axis=0, keepdims=True) + dgl_ref[ci][:, :1]
            dG = dG + jnp.where(L["row"][:, :1] == CHUNK - 1, d_last, 0.0)
            dg_ref[rows, :] = jnp.sum(jnp.where(L["col"] >= L["row"], _to_row(dG, eye), 0.0), axis=1, keepdims=True)

    hd = pl.BlockSpec((rb, HEAD_D), lambda h, i: (i, h))
    cc = pl.BlockSpec((None, cpb, CHUNK, CHUNK), lambda h, i: (h, i, 0, 0))
    col1 = pl.BlockSpec((None, rb, 1), lambda h, i: (h, i, 0))
    return pl.pallas_call(
        body, grid=(nh, S // rb),
        in_specs=[pl.BlockSpec((rb, HEAD_D), lambda h, i: (i, 3 * h)), pl.BlockSpec((rb, HEAD_D), lambda h, i: (i, 3 * h + 1)),
                  pl.BlockSpec((rb, HEAD_D), lambda h, i: (i, 3 * h + 2)), pl.BlockSpec((rb, LANE), lambda h, i: (i, 0)),
                  cc, hd, pl.BlockSpec((None, 3 * rb, HEAD_D), lambda h, i: (h, i, 0)), hd, hd, hd, hd, cc,
                  pl.BlockSpec((None, cpb, 1, LANE), lambda h, i: (h, i, 0, 0))],
        out_specs=[pl.BlockSpec((rb, 3 * HEAD_D), lambda h, i: (i, h)), col1, col1],
        out_shape=[sds((S, 3 * nh * HEAD_D), F32)] + [sds((nh, S, 1), F32)] * 2,
        compiler_params=_params(("parallel", "parallel")), name="gdn_local_bwd")(
            qkv, qkv, qkv, gb, T, u, wqk, dvn, dw, dqd, dkd, dattn, dgl)


def _gated_norm_fwd(o, proj, norm_w, nh, z_blk0, mix, m_blk0):
    S = o.shape[0]

    def body(o_ref, z_ref, w_ref, mix_ref, y_ref):
        ov, z = o_ref[...], z_ref[...]
        r = lax.rsqrt(jnp.mean(ov * ov, axis=1, keepdims=True) + RMS_EPS)
        y_ref[...] = ov * r * w_ref[...] * (z * _sigmoid(z))

    return pl.pallas_call(
        body, grid=(nh,),
        in_specs=[pl.BlockSpec((S, HEAD_D), lambda h: (0, h)), pl.BlockSpec((S, HEAD_D), lambda h: (0, z_blk0 + h)),
                  pl.BlockSpec((1, HEAD_D), lambda h: (0, 0)), pl.BlockSpec(memory_space=pl.ANY)],
        out_specs=pl.BlockSpec((S, HEAD_D), lambda h: (0, m_blk0 + h)),
        out_shape=jax.ShapeDtypeStruct(mix.shape, F32), input_output_aliases={3: 0},
        compiler_params=_params(("parallel",)), name="gated_norm_fwd")(o, proj, norm_w, mix)


def _gated_norm_bwd(o, proj, norm_w, dmix, nh, z_blk0, d_blk0):
    S = o.shape[0]
    sds = jax.ShapeDtypeStruct

    def body(o_ref, z_ref, w_ref, dy_ref, do_ref, dz_ref, dw_ref):
        ov, z, w, dy = o_ref[...], z_ref[...], w_ref[...], dy_ref[...]
        r = lax.rsqrt(jnp.mean(ov * ov, axis=1, keepdims=True) + RMS_EPS)
        oh = ov * r
        sg = _sigmoid(z)
        dz_ref[...] = dy * (oh * w) * (sg * (1.0 + z * (1.0 - sg)))
        don = dy * (z * sg)
        @pl.when(pl.program_id(0) == 0)
        def _():
            dw_ref[...] = jnp.zeros_like(dw_ref)

        dw_ref[...] += jnp.sum(don * oh, axis=0, keepdims=True)
        doh = don * w
        do_ref[...] = r * (doh - oh * jnp.mean(doh * oh, axis=1, keepdims=True))

    return pl.pallas_call(
        body, grid=(nh,),
        in_specs=[pl.BlockSpec((S, HEAD_D), lambda h: (0, h)), pl.BlockSpec((S, HEAD_D), lambda h: (0, z_blk0 + h)),
                  pl.BlockSpec((1, HEAD_D), lambda h: (0, 0)), pl.BlockSpec((S, HEAD_D), lambda h: (0, d_blk0 + h))],
        out_specs=[pl.BlockSpec((S, HEAD_D), lambda h: (0, h)), pl.BlockSpec((S, HEAD_D), lambda h: (0, h)),
                   pl.BlockSpec((1, HEAD_D), lambda h: (0, 0))],
        out_shape=[sds((S, nh * HEAD_D), F32), sds((S, nh * HEAD_D), F32), sds((1, HEAD_D), F32)],
        compiler_params=_params(("arbitrary",)), name="gated_norm_bwd")(o, proj, norm_w, dmix)


def _adamw_math(w, g, m, v):
    m = ADAM_B1 * m + (1.0 - ADAM_B1) * g
    v = ADAM_B2 * v + (1.0 - ADAM_B2) * (g * g)
    m_hat = m / (1.0 - ADAM_B1 ** ADAM_STEP)
    v_hat = v / (1.0 - ADAM_B2 ** ADAM_STEP)
    delta = -ADAM_LR * (m_hat / (jnp.sqrt(v_hat) + ADAM_EPS) + ADAM_WD * w)
    return delta, m, v


def _slab_tiles(R, C, rows=256, cols=256):
    if R % rows == 0:
        return (rows, C), R // rows, lambda i: (i, 0)
    tc = _tile(C, cols)
    return (R, tc), C // tc, lambda i: (0, i)


def _adamw_big(parts, terms, chip, w, m, v, name):
    R, C = w.shape
    blk, steps, at = _slab_tiles(R, C)
    sds = jax.ShapeDtypeStruct

    def body(q_ref, p_ref, t_ref, w_ref, m_ref, v_ref, g_ref, d_ref, nm_ref, nv_ref):
        g = ((p_ref[...].astype(F32) + t_ref[0].astype(F32)) + t_ref[1].astype(F32)) + t_ref[2].astype(F32)
        g_ref[...] = g
        d_ref[...], nm_ref[...], nv_ref[...] = _adamw_math(w_ref[...], g, m_ref[...], v_ref[...])

    spec = pl.BlockSpec(blk, lambda i, q_ref: at(i))
    grid_spec = pltpu.PrefetchScalarGridSpec(
        num_scalar_prefetch=1, grid=(steps,),
        in_specs=[pl.BlockSpec((None,) + blk, lambda i, q_ref: (q_ref[0],) + at(i)),
                  pl.BlockSpec((3,) + blk, lambda i, q_ref: (0,) + at(i)), spec, spec, spec],
        out_specs=[spec] * 4)
    return pl.pallas_call(body, grid_spec=grid_spec, out_shape=[sds((R, C), F32)] * 4,
                          compiler_params=_params(("parallel",)), name=name)(chip, parts, terms, w, m, v)


def _adamw_small(ws, gs, ms, vs):
    n = len(ws)

    def body(*refs):
        for i in range(n):
            w, g, m, v = (refs[k * n + i][...] for k in range(4))
            d, nm, nv = _adamw_math(w, g, m, v)
            refs[4 * n + i][...] = d
            refs[5 * n + i][...] = nm
            refs[6 * n + i][...] = nv

    shapes = [jax.ShapeDtypeStruct(w.shape, F32) for w in ws]
    vm = pl.BlockSpec(memory_space=pltpu.VMEM)
    outs = pl.pallas_call(body, in_specs=[vm] * (4 * n), out_specs=[vm] * (3 * n), out_shape=shapes * 3,
                          name="adamw_small")(*ws, *gs, *ms, *vs)
    return outs[:n], outs[n:2 * n], outs[2 * n:]


MESH = pl.DeviceIdType.MESH
ANY = pl.BlockSpec(memory_space=pl.ANY)


def _place():
    x, y, c = lax.axis_index("x"), lax.axis_index("y"), lax.axis_index("c")
    return x, y, c, [(1 - x, y), (x, 1 - y), (1 - x, 1 - y)]


def _chip_sum(grad, recv, core, name):
    _, R, C = grad.shape
    blk, steps, at = _slab_tiles(R, C, rows=512 if R % 512 == 0 else 256, cols=512)

    def body(c_ref, g_ref, r_ref, o_ref):
        o_ref[...] = (g_ref[...].astype(F32) + r_ref[...].astype(F32)).astype(o_ref.dtype)

    grid_spec = pltpu.PrefetchScalarGridSpec(
        num_scalar_prefetch=1, grid=(4, steps),
        in_specs=[pl.BlockSpec((None,) + blk, lambda q, i, c_ref: (2 * q + c_ref[0],) + at(i)),
                  pl.BlockSpec((None,) + blk, lambda q, i, c_ref: (q,) + at(i))],
        out_specs=pl.BlockSpec((None,) + blk, lambda q, i, c_ref: (q,) + at(i)))
    return pl.pallas_call(body, grid_spec=grid_spec, out_shape=jax.ShapeDtypeStruct((4, R, C), BF16),
                          compiler_params=_params(("parallel", "parallel")), name=name)(core, grad, recv)


HBM_SPEC = pl.BlockSpec(memory_space=pltpu.HBM)
SEM_SPEC = pl.BlockSpec(memory_space=pltpu.SEMAPHORE)
DATAFLOW = pltpu.SideEffectType.DATAFLOW_SIDE_EFFECTING


def _split_start(name, bufs, plan, counts, after=None):
    nb, ng = len(bufs), len(counts)
    extra = [] if after is None else [after]
    place = [(g, k) for g, cnt in enumerate(counts) for k in range(cnt)]

    def body(*refs):
        sems, token = refs[nb + len(extra):nb + len(extra) + 2 * ng], refs[-1]
        for (g, k), (src, dst, to) in zip(place, plan(refs[:nb])):
            pltpu.make_async_remote_copy(src_ref=src, dst_ref=dst, send_sem=sems[2 * g].at[k], recv_sem=sems[2 * g + 1].at[k],
                                         device_id=to, device_id_type=MESH).start()
        token[...] = jnp.zeros_like(token)

    outs = pl.pallas_call(
        body, name=name,
        out_shape=(*[pltpu.SemaphoreType.DMA((cnt,)) for cnt in counts for _ in range(2)],
                   *[pltpu.HBM(b.shape, b.dtype) for b in bufs], jax.ShapeDtypeStruct((8, LANE), F32)),
        in_specs=[HBM_SPEC] * nb + [ANY] * len(extra),
        out_specs=(*[SEM_SPEC] * (2 * ng), *[HBM_SPEC] * nb, pl.BlockSpec(memory_space=pltpu.VMEM)),
        input_output_aliases={i: 2 * ng + i for i in range(nb)},
        compiler_params=pltpu.CompilerParams(has_side_effects=DATAFLOW))(
            *[pltpu.with_memory_space_constraint(b, pltpu.HBM) for b in bufs], *extra)
    return [(outs[2 * g], outs[2 * g + 1]) for g in range(ng)], list(outs[2 * ng:2 * ng + nb]), outs[-1]


def _split_wait(name, sems, bufs, plan, after):
    nb = len(bufs)
    send_sems, recv_sems = sems

    def body(*refs):
        send_s, recv_s = refs[nb], refs[nb + 1]
        for k, (src, dst, to) in enumerate(plan(refs[:nb])):
            cp = pltpu.make_async_remote_copy(src_ref=src, dst_ref=dst, send_sem=send_s.at[k], recv_sem=recv_s.at[k],
                                              device_id=to, device_id_type=MESH)
            cp.wait_send()
            cp.wait_recv()

    after = tuple(after) if isinstance(after, (tuple, list)) else (after,)
    outs = pl.pallas_call(
        body, name=name, out_shape=tuple(pltpu.HBM(b.shape, b.dtype) for b in bufs),
        in_specs=[HBM_SPEC] * nb + [SEM_SPEC, SEM_SPEC] + [ANY] * len(after), out_specs=tuple([HBM_SPEC] * nb),
        input_output_aliases={i: i for i in range(nb)},
        compiler_params=pltpu.CompilerParams(has_side_effects=DATAFLOW))(*bufs, send_sems, recv_sems, *after)
    return list(outs)


def _slot(px, py, pc):
    return 4 * px + 2 * py + pc


class _Gather:
    def __init__(self, shards, groups, dev):
        self.shards, self.groups, self.dev = shards, groups, dev
        self.second = {}

    @staticmethod
    def _plan1(pairs, refs):
        x, y, c, chips = _place()
        out = []
        for s, l in pairs:
            dst = refs[l].at[_slot(x, y, c)]
            out.append((refs[s], dst, (x, y, 1 - c)))
            out += [(refs[s], dst, (px, py, c)) for px, py in chips]
        return out

    @staticmethod
    def _plan2(refs):
        x, y, c, chips = _place()
        return [(r.at[_slot(px, py, c)],) * 2 + ((x, y, 1 - c),) for r in refs for px, py in chips]

    def start(self):
        n = len(self.shards)
        lands = [lax.dynamic_update_slice(lax.empty((N_DEV,) + s.shape, s.dtype), s[None], (self.dev, 0, 0))
                 for s in self.shards]
        pairs = [(w, n + w) for g in self.groups for w in g]
        sems, bufs, token = _split_start("gather_start_1", list(self.shards) + lands, functools.partial(self._plan1, pairs),
                                         tuple(4 * len(g) for g in self.groups))
        self.first = [(sems[i], [bufs[w] for w in g], [bufs[n + w] for w in g]) for i, g in enumerate(self.groups)]
        return token

    def mid(self, gi, after):
        sems, srcs, lands = self.first[gi]
        m = len(srcs)
        plan = functools.partial(self._plan1, [(w, m + w) for w in range(m)])
        lands = _split_wait("gather_%d_wait_1" % gi, sems, srcs + lands, plan, after)[m:]
        sems, lands, token = _split_start("gather_%d_start_2" % gi, lands, self._plan2, (3 * m,))
        self.second[gi] = (sems[0], lands)
        return token

    def finish(self, gi, after):
        sems, lands = self.second[gi]
        return _split_wait("gather_%d_wait_2" % gi, sems, lands, self._plan2, after)


class _Exchanges:
    def __init__(self, tag, n, core, gather=None):
        self.tag, self.n, self.core, self.gather = tag, n, core, gather

    def weights_mid(self, group, after):
        return self.gather.mid(group, after)

    def weights_finish(self, group, after):
        return self.gather.finish(group, after)

    def _reduce_plan1(self, refs):
        n = self.n
        x, y, c, _ = _place()
        return [(refs[w].at[2 * q + (1 - c)], refs[n + w].at[q], (x, y, 1 - c)) for w in range(n) for q in range(4)]

    def _reduce_plan2(self, refs):
        n = self.n
        x, y, c, chips = _place()
        return [(refs[w].at[2 * px + py], refs[n + w].at[j], (px, py, c))
                for w in range(n) for j, (px, py) in enumerate(chips)]

    def grads_start(self, grads):
        lands = [lax.empty((4,) + g.shape[1:], g.dtype) for g in grads]
        self.g1 = _split_start(self.tag + "reduce_start_1", list(grads) + lands, self._reduce_plan1, (4 * self.n,))
        return self.g1[2]

    def grads_mid(self, after):
        n = self.n
        sems, bufs, _ = self.g1
        bufs = _split_wait(self.tag + "reduce_wait_1", sems[0], bufs, self._reduce_plan1, after)
        core = self.core.reshape(1).astype(jnp.int32)
        self.parts = [_chip_sum(bufs[w], bufs[n + w], core, self.tag + "reduce_chip_sum_%d" % w) for w in range(n)]
        lands = [lax.empty((3,) + p.shape[1:], p.dtype) for p in self.parts]
        self.g2 = _split_start(self.tag + "reduce_start_2", self.parts + lands, self._reduce_plan2, (3 * n,))
        return self.g2[2]

    def grads_finish(self, after):
        n = self.n
        sems, bufs, _ = self.g2
        bufs = _split_wait(self.tag + "reduce_wait_2", sems[0], bufs, self._reduce_plan2, after)
        self.parts, self.terms = bufs[:n], bufs[n:]


def _all_reduce_small(buf):
    R = buf.shape[0]

    def body(x_ref, o_ref, g_ref, send_sems, recv_sems):
        x, y, c, chips = _place()
        me, sibling = (x, y, c), (x, y, 1 - c)

        def slot(px, py, pc):
            return 4 * px + 2 * py + pc

        def copy(k, block, to, src=None):
            dst = g_ref.at[slot(*block)]
            return pltpu.make_async_remote_copy(src_ref=dst if src is None else src, dst_ref=dst,
                                                send_sem=send_sems.at[k], recv_sem=recv_sems.at[k],
                                                device_id=to, device_id_type=MESH)

        first = [copy(0, me, sibling, src=x_ref)]
        first += [copy(1 + j, me, (*chip, c), src=x_ref) for j, chip in enumerate(chips)]
        for cp in first:
            cp.start()
        g_ref[slot(*me)] = x_ref[...]
        passed = [copy(4 + j, (*chip, c), sibling) for j, chip in enumerate(chips)]
        for j, chip in enumerate(chips):
            copy(1 + j, (*chip, c), me).wait_recv()
            passed[j].start()
        copy(0, sibling, me).wait_recv()
        for j, chip in enumerate(chips):
            copy(4 + j, (*chip, 1 - c), me).wait_recv()
        for cp in first + passed:
            cp.wait_send()
        acc = g_ref[0]
        for s in range(1, N_DEV):
            acc = acc + g_ref[s]
        o_ref[...] = acc

    vm = pl.BlockSpec(memory_space=pltpu.VMEM)
    return pl.pallas_call(
        body, in_specs=[vm], out_specs=vm, out_shape=jax.ShapeDtypeStruct((R, LANE), F32),
        scratch_shapes=[pltpu.VMEM((N_DEV, R, LANE), F32), pltpu.SemaphoreType.DMA((7,)), pltpu.SemaphoreType.DMA((7,))],
        name="all_reduce_small")(buf)


def _pad_cols(a, width):
    return jnp.pad(a, ((0, 0), (0, width - a.shape[1])))


def _local_step(x, target, w_in_g, conv_w, a_log, dt_bias, delta_norm_w, sinks, bias, ln1_g, ln1_b, ln2_g, ln2_b,
                ex, ex_in):
    S, D = x.shape
    aw = D // 2
    hq, hkv, nh = aw // HEAD_A, aw // HEAD_A // GQA, aw // HEAD_D
    kvw = hkv * HEAD_A
    c_q, c_k, c_v, c_d = 0, aw, aw + kvw, aw + 2 * kvw
    c_ab = c_d + 3 * aw
    c_z = c_ab + 2 * nh
    n_in = c_z + aw
    n_slab = w_in_g.shape[1]
    assert w_in_g.shape == (N_DEV, n_in // N_DEV, D), (w_in_g.shape, n_in)

    w_in_t = w_in_g.reshape(n_in, D)
    w_pt = jnp.concatenate([w_in_t[:c_ab], w_in_t[c_z:], jnp.pad(w_in_t[c_ab:c_z], ((0, LANE - 2 * nh), (0, 0)))], axis=0)
    p_z, p_ab = c_ab, c_ab + aw
    n_p = p_ab + LANE

    xb = x.astype(BF16)
    proj = _matmul(xb, w_pt, NT, name="proj", tn=1152)
    attn_out = _attn_fwd(proj, bias, sinks.reshape(-1), hq, 0, c_k // kvw, c_v // kvw, D)
    conv2 = conv_w.reshape(CONV_W, 3 * aw)
    qkv = _gdn_prep_fwd(proj, conv2, nh, c_d // HEAD_D)
    ab = proj[:, p_ab:]
    al, dt = _pad_cols(a_log, LANE), _pad_cols(dt_bias, LANE)
    gb = _gates_fwd(ab, al, dt, nh)
    u_d, wqk, attn_d, t_d = _gdn_local_fwd(qkv, gb, nh)
    o_d, vn, states = _gdn_scan_fwd(u_d, wqk, attn_d, gb, nh, ex.weights_mid(1, u_d))
    mix = _gated_norm_fwd(o_d, proj, delta_norm_w, nh, p_z // HEAD_D, attn_out, aw // HEAD_D)
    w_o_g, w_up_g = ex.weights_finish(1, mix)
    w_o = w_o_g.reshape(D, D)
    mixed = _matmul(mix, w_o, NN, name="out_proj")
    h1 = _ln1_fwd(x, mixed, ln1_g, ln1_b)
    u = _matmul(h1, w_up_g, NN, name="mlp_up", b_groups=True, out_dtype=BF16, deps=(ex.weights_mid(2, h1),))
    (w_down_g,) = ex.weights_finish(2, u)
    w_down = w_down_g.reshape(-1, D)
    mlp = _matmul(u, w_down, NN, name="mlp_down", a_fn=_relu_sq, tk=4096)
    dr2, loss_row, dln2_g, dln2_b = _ln2_loss(h1, mlp, ln2_g, ln2_b, target)

    du = _matmul(dr2, w_down, NT, name="d_mlp_act", epi=_relu_sq_grad, epi_in=(u,), out_dtype=BF16)
    dw_down = _matmul(u, dr2, TN, name="dw_down", a_fn=_relu_sq, out_dtype=BF16)
    dw_up = _matmul(h1, du, TN, name="dw_up", out_dtype=BF16, out_groups=N_DEV)
    dh_mlp = _matmul(du, w_up_g, NT, name="d_h1", b_groups=True, tk=4096)
    dr1, dln1_g, dln1_b = _ln1_bwd(x, mixed, ln1_g, dr2, dh_mlp)
    dw_o = _matmul(mix, dr1, TN, name="dw_o", out_dtype=BF16)
    tok = ex.grads_start([dw_o.reshape(N_DEV, -1, D), dw_up, dw_down.reshape(N_DEV, -1, D)])
    dmix = _matmul(dr1, w_o, NT, name="d_mix", deps=(tok,))
    dq_a, dk_a, dv_a, dbias, dsink = _attn_bwd(proj, bias, sinks.reshape(-1), mix, dmix, hq, 0, c_k // kvw, c_v // kvw)
    drel = _rel_bias_grad(dbias, hq)
    do_d, dz, dnw = _gated_norm_bwd(o_d, proj, delta_norm_w, dmix, nh, p_z // HEAD_D, aw // HEAD_D)
    dvn_s, dw_s, dqd, dkd, dattn_d, dgl = _gdn_scan_bwd(wqk, attn_d, gb, states, vn, do_d, nh, ex.grads_mid(dq_a))
    dqkv_n, dg, dbeta = _gdn_local_bwd(qkv, gb, t_d, u_d, wqk, dvn_s, dw_s, dqd, dkd, dattn_d, dgl, nh)
    dgb = _pad_cols(jnp.concatenate([dg.reshape(nh, S).T, dbeta.reshape(nh, S).T], axis=1), LANE)
    dab, da_log, ddt_bias = _gates_bwd(ab, al, dt, dgb, nh)
    dqkv_d, dconv = _gdn_prep_bwd(proj, conv2, dqkv_n, nh, c_d // HEAD_D)
    dproj = jnp.concatenate([dq_a, dk_a, dv_a, dqkv_d, dz, dab], axis=1)
    dw_pt = _matmul(dproj, xb, TN, name="dw_in", out_dtype=BF16, tm=1152)

    def grad_rows(lo, hi):
        cuts = sorted({lo, hi, *[c for c in (c_ab, c_z) if lo < c < hi]})
        place = lambda r: r if r < c_ab else (p_ab + r - c_ab if r < c_z else p_z + r - c_z)
        return [dw_pt[place(a):place(a) + b - a] for a, b in zip(cuts[:-1], cuts[1:])]

    dw_in_g = jnp.stack([jnp.concatenate(grad_rows(g * n_slab, (g + 1) * n_slab), axis=0) for g in range(N_DEV)])
    tok = ex_in.grads_mid(ex_in.grads_start([dw_in_g]))
    grad_x = _matmul(dproj, w_pt, NN, name="d_x", tk=1920, deps=(tok,), epi=_residual_grad, epi_in=(dr1,))
    ex.grads_finish(grad_x)

    small = dict(conv_w=dconv, a_log=da_log[:, :nh], dt_bias=ddt_bias[:, :nh], delta_norm_w=dnw,
                 attn_sinks=dsink[:, 0].reshape(1, hq), rel_bias=drel[:, :, 0].T,
                 ln1_g=dln1_g, ln1_b=dln1_b, ln2_g=dln2_g, ln2_b=dln2_b)
    return loss_row, grad_x, small


SMALL_ORDER = ("conv_w", "a_log", "dt_bias", "delta_norm_w", "attn_sinks", "rel_bias", "ln1_g", "ln1_b", "ln2_g", "ln2_b")


def _pack_small(loss_row, small):
    parts = [loss_row.reshape(-1)]
    for k in SMALL_ORDER:
        flat = small[k].reshape(-1)
        parts.append(jnp.pad(flat, (0, (-flat.shape[0]) % LANE)))
    flat = jnp.concatenate(parts)
    flat = jnp.pad(flat, (0, (-flat.shape[0]) % (8 * LANE)))
    return flat.reshape(-1, LANE)


def _unpack_small(buf, small_shapes):
    flat = buf.reshape(-1)
    loss = flat[0]
    off = LANE
    out = {}
    for k in SMALL_ORDER:
        n = int(np.prod(small_shapes[k]))
        out[k] = flat[off:off + n].reshape(small_shapes[k])
        off += n + (-n) % LANE
    return loss, out


def kernel(x, w_in, conv_w, a_log, dt_bias, delta_norm_w, attn_sinks, rel_bias, w_o, ln1_g, ln1_b, w_up, w_down, ln2_g, ln2_b, loss_target, m_w_in, m_conv_w, m_a_log, m_dt_bias, m_delta_norm_w, m_attn_sinks, m_rel_bias, m_w_o, m_ln1_g, m_ln1_b, m_w_up, m_w_down, m_ln2_g, m_ln2_b, v_w_in, v_conv_w, v_a_log, v_dt_bias, v_delta_norm_w, v_attn_sinks, v_rel_bias, v_w_o, v_ln1_g, v_ln1_b, v_w_up, v_w_down, v_ln2_g, v_ln2_b):
    S, D = x.shape[1], x.shape[2]
    core = lax.axis_index("c")
    dev = 4 * lax.axis_index("x") + 2 * lax.axis_index("y") + core

    gather = _Gather([conv_w[0, :, 0, :], w_in[0].T.astype(BF16), w_o[0].astype(BF16), w_up[0].astype(BF16),
                      w_down[0].astype(BF16)], [[0, 1], [2, 3], [4]], dev)
    token = gather.start()
    bias = _attn_bias(rel_bias.T)
    w_t, m_t, v_t = w_in[0].T, m_w_in[0].T, v_w_in[0].T
    conv_g, w_in_g = gather.finish(0, gather.mid(0, (token, bias, w_t, m_t, v_t)))
    ex = _Exchanges("", 3, core, gather)
    ex_in = _Exchanges("in_", 1, core)

    cw_sh = conv_w.shape[3]
    conv_full = jnp.transpose(conv_g, (1, 0, 2)).reshape(CONV_W, N_DEV * cw_sh)

    loss_row, grad_x, small = _local_step(
        x[0], loss_target[0], w_in_g, conv_full, a_log, dt_bias, delta_norm_w, attn_sinks, bias,
        ln1_g, ln1_b, ln2_g, ln2_b, ex, ex_in)

    chip_arr = (dev // 2).reshape(1).astype(jnp.int32)
    big = {}
    for i, (name, w, m, v) in enumerate((("w_o", w_o, m_w_o, v_w_o), ("w_up", w_up, m_w_up, v_w_up),
                                         ("w_down", w_down, m_w_down, v_w_down))):
        big[name] = [o[None] for o in _adamw_big(ex.parts[i], ex.terms[i], chip_arr, w[0], m[0], v[0], "adamw_" + name)]
    ex_in.grads_finish(big["w_down"][0])
    outs = _adamw_big(ex_in.parts[0], ex_in.terms[0], chip_arr, w_t, m_t, v_t, "adamw_w_in")
    big["w_in"] = [o.T[None] for o in outs]

    small_shapes = {k: v.shape for k, v in small.items()}
    loss, small = _unpack_small(_all_reduce_small(_pack_small(loss_row, small)), small_shapes)
    small["conv_w"] = lax.dynamic_slice(small["conv_w"], (0, dev * cw_sh), (CONV_W, cw_sh))
    small["rel_bias"] = small["rel_bias"].reshape(rel_bias.shape)
    p2 = dict(conv_w=(conv_w, m_conv_w, v_conv_w), a_log=(a_log, m_a_log, v_a_log), dt_bias=(dt_bias, m_dt_bias, v_dt_bias),
              delta_norm_w=(delta_norm_w, m_delta_norm_w, v_delta_norm_w), attn_sinks=(attn_sinks, m_attn_sinks, v_attn_sinks),
              rel_bias=(rel_bias, m_rel_bias, v_rel_bias), ln1_g=(ln1_g, m_ln1_g, v_ln1_g), ln1_b=(ln1_b, m_ln1_b, v_ln1_b),
              ln2_g=(ln2_g, m_ln2_g, v_ln2_g), ln2_b=(ln2_b, m_ln2_b, v_ln2_b))
    two_d = lambda a: a.reshape(-1, a.shape[-1])
    ws = [two_d(p2[k][0]) for k in SMALL_ORDER]
    gs = [two_d(small[k]) for k in SMALL_ORDER]
    ms = [two_d(p2[k][1]) for k in SMALL_ORDER]
    vs = [two_d(p2[k][2]) for k in SMALL_ORDER]
    ds, nms, nvs = _adamw_small(ws, gs, ms, vs)
    res = {}
    for i, k in enumerate(SMALL_ORDER):
        shp = p2[k][0].shape
        res[k] = [gs[i].reshape(shp), ds[i].reshape(shp), nms[i].reshape(shp), nvs[i].reshape(shp)]
    res.update(big)
    order = ("w_in", "conv_w", "a_log", "dt_bias", "delta_norm_w", "attn_sinks", "rel_bias", "w_o", "ln1_g", "ln1_b",
             "w_up", "w_down", "ln2_g", "ln2_b")
    return (loss, grad_x[None], *[res[k][0] for k in order], *[res[k][1] for k in order],
            *[res[k][2] for k in order], *[res[k][3] for k in order])
```

```python
import functools
import math

import numpy as np
import jax
import jax.numpy as jnp
from jax import lax
from jax.experimental import pallas as pl
from jax.experimental.pallas import tpu as pltpu

F32 = jnp.float32
BF16 = jnp.bfloat16
HIGHEST = lax.Precision.HIGHEST

N_DEV = 8
HEAD_A = 64
GQA = 4
BLK = 128
N_BUCKETS = 32
MAX_DISTANCE = 128
HEAD_D = 128
CONV_W = 4
CHUNK = 64
NEG_INF = -1e30
LN_EPS = 1e-5
RMS_EPS = 1e-6
DN_ALPHA = 2.0 ** 0.25
ADAM_LR, ADAM_B1, ADAM_B2, ADAM_EPS, ADAM_WD, ADAM_STEP = 0.001, 0.9, 0.999, 1e-08, 0.01, 10

LANE = 128
VMEM_LIMIT = 56 * 1024 * 1024

NN = ((1,), (0,))
NT = ((1,), (1,))
TN = ((0,), (0,))


def _dot(a, b, dims, prec=None):
    return lax.dot_general(a, b, (dims, ((), ())), precision=prec, preferred_element_type=F32)


def _tile(dim, pref):
    if dim <= pref:
        return dim
    t = (pref // LANE) * LANE
    while t > LANE and dim % t:
        t -= LANE
    assert dim % t == 0, (dim, pref)
    return t


def _params(sem):
    return pltpu.CompilerParams(dimension_semantics=sem, vmem_limit_bytes=VMEM_LIMIT)


def _matmul(a, b, dims, *, name, out_dtype=F32, tm=1024, tn=1024, tk=2048, a_fn=None, epi=None, epi_in=(),
            b_groups=None, out_groups=None, deps=()):
    (ca,), (cb,) = dims
    M, K = a.shape[1 - ca], a.shape[ca]
    if b_groups:
        G, R, C = b.shape
        bshape = (R, G * C)
    else:
        bshape = b.shape
    N = bshape[1 - cb]
    assert bshape[cb] == K, (a.shape, b.shape, dims)
    tm, tk = _tile(M, tm), _tile(K, tk)
    if b_groups:
        lim = C if cb == 0 else tn
        tn = _tile(N, min(tn, lim))
        if cb == 1 and tk < C:
            tk = _tile(K, min(tk, C))
        elif cb == 1:
            tk = C * max(1, tk // C)
    else:
        tn = _tile(N, tn)
    if out_groups:
        tn = _tile(N, min(tn, N // out_groups))
    nk = K // tk
    b_span = tk // C if (b_groups and cb == 1 and tk > C) else 1

    def body(*refs):
        a_ref, b_ref = refs[0], refs[1]
        e_refs = refs[2:2 + len(epi_in)]
        o_ref = refs[2 + len(epi_in) + len(deps)]
        acc_ref = refs[3 + len(epi_in) + len(deps)] if nk > 1 else None
        k = pl.program_id(2)
        av = a_ref[...]
        if a_fn is not None:
            av = a_fn(av)
        if b_span > 1:
            prod = sum(_dot(av[:, g * C:(g + 1) * C].astype(BF16), b_ref[g].astype(BF16), dims) for g in range(b_span))
        else:
            prod = _dot(av.astype(BF16), b_ref[...].astype(BF16), dims)

        def finish(r):
            if epi is not None:
                r = epi(r, *[e[...] for e in e_refs])
            o_ref[...] = r.astype(out_dtype)

        if nk == 1:
            finish(prod)
            return

        @pl.when(k == 0)
        def _():
            acc_ref[...] = prod

        @pl.when(k > 0)
        def _():
            acc_ref[...] += prod

        @pl.when(k == nk - 1)
        def _():
            finish(acc_ref[...])

    a_spec = (pl.BlockSpec((tm, tk), lambda i, j, k: (i, k)) if ca == 1
              else pl.BlockSpec((tk, tm), lambda i, j, k: (k, i)))
    if b_groups:
        if cb == 0:
            per = C // tn
            b_spec = pl.BlockSpec((None, tk, tn), lambda i, j, k: (j // per, k, j % per))
        elif b_span > 1:
            b_spec = pl.BlockSpec((b_span, tn, C), lambda i, j, k: (k, j, 0))
        else:
            per = C // tk
            b_spec = pl.BlockSpec((None, tn, tk), lambda i, j, k: (k // per, j, k % per))
    else:
        b_spec = (pl.BlockSpec((tk, tn), lambda i, j, k: (k, j)) if cb == 0
                  else pl.BlockSpec((tn, tk), lambda i, j, k: (j, k)))
    e_specs = [pl.BlockSpec((tm, tn), lambda i, j, k: (i, j)) for _ in epi_in]
    if out_groups:
        per_o = (N // out_groups) // tn
        o_spec = pl.BlockSpec((None, tm, tn), lambda i, j, k: (j // per_o, i, j % per_o))
        o_shape = jax.ShapeDtypeStruct((out_groups, M, N // out_groups), out_dtype)
    else:
        o_spec = pl.BlockSpec((tm, tn), lambda i, j, k: (i, j))
        o_shape = jax.ShapeDtypeStruct((M, N), out_dtype)
    return pl.pallas_call(
        body, grid=(M // tm, N // tn, nk), out_specs=o_spec,
        in_specs=[a_spec, b_spec] + e_specs + [pl.BlockSpec(memory_space=pl.ANY)] * len(deps),
        out_shape=o_shape, scratch_shapes=[pltpu.VMEM((tm, tn), F32)] if nk > 1 else [],
        compiler_params=_params(("parallel", "parallel", "arbitrary")), name=name)(a, b, *epi_in, *deps)


def _relu_sq(u):
    r = jnp.maximum(u, 0.0)
    return r * r


def _relu_sq_grad(acc, u):
    return acc * (2.0 * jnp.maximum(u, 0.0))


def _ln_stats(r):
    mu = jnp.mean(r, axis=-1, keepdims=True)
    xc = r - mu
    var = jnp.mean(xc * xc, axis=-1, keepdims=True)
    rstd = lax.rsqrt(var + LN_EPS)
    return xc * rstd, rstd


def _ln_bwd(dy, xhat, rstd, g):
    dxh = dy * g
    m1 = jnp.mean(dxh, axis=-1, keepdims=True)
    m2 = jnp.mean(dxh * xhat, axis=-1, keepdims=True)
    return rstd * (dxh - m1 - xhat * m2)


def _row_call(body, ins, row_ins, outs, acc_outs, name, tr=256):
    S = ins[0].shape[0]
    tr = min(tr, S)
    n_in, n_row, n_out = len(ins), len(row_ins), len(outs)

    def wrapped(*refs):
        i = pl.program_id(0)
        acc_refs = refs[n_in + n_row + n_out:]

        @pl.when(i == 0)
        def _():
            for r in acc_refs:
                r[...] = jnp.zeros_like(r)

        body(*refs)

    in_specs = [pl.BlockSpec((tr, a.shape[1]), lambda i: (i, 0)) for a in ins]
    in_specs += [pl.BlockSpec(a.shape, lambda i: (0, 0)) for a in row_ins]
    out_specs = [pl.BlockSpec((tr, s.shape[1]), lambda i: (i, 0)) for s in outs]
    out_specs += [pl.BlockSpec(s.shape, lambda i: (0, 0)) for s in acc_outs]
    return pl.pallas_call(wrapped, grid=(S // tr,), in_specs=in_specs, out_specs=out_specs,
                          out_shape=list(outs) + list(acc_outs),
                          compiler_params=_params(("arbitrary",)), name=name)(*ins, *row_ins)


def _ln1_fwd(x, mixed, g, b):
    def body(x_ref, m_ref, g_ref, b_ref, h_ref):
        xhat, _ = _ln_stats(DN_ALPHA * x_ref[...] + m_ref[...])
        h_ref[...] = xhat * g_ref[...] + b_ref[...]
    return _row_call(body, [x, mixed], [g, b], [jax.ShapeDtypeStruct(x.shape, F32)], [], "ln1_fwd")[0]


def _ln2_loss(h1, mlp, g, b, target):
    S, D = h1.shape
    sds = jax.ShapeDtypeStruct

    def body(h_ref, m_ref, t_ref, g_ref, b_ref, dr_ref, loss_ref, dg_ref, db_ref):
        xhat, rstd = _ln_stats(DN_ALPHA * h_ref[...] + m_ref[...])
        gv = g_ref[...]
        err = xhat * gv + b_ref[...] - t_ref[...]
        loss_ref[...] += jnp.sum(jnp.sum(err * err, axis=0, keepdims=True), axis=1, keepdims=True) * (0.5 / D)
        dy = err * (1.0 / D)
        dg_ref[...] += jnp.sum(dy * xhat, axis=0, keepdims=True)
        db_ref[...] += jnp.sum(dy, axis=0, keepdims=True)
        dr_ref[...] = _ln_bwd(dy, xhat, rstd, gv)

    return _row_call(body, [h1, mlp, target], [g, b], [sds((S, D), F32)],
                     [sds((1, LANE), F32), sds((1, D), F32), sds((1, D), F32)], "ln2_loss")


def _ln1_bwd(x, mixed, g, dr2, dh_mlp):
    S, D = x.shape
    sds = jax.ShapeDtypeStruct

    def body(x_ref, m_ref, dr2_ref, dh_ref, g_ref, dr_ref, dg_ref, db_ref):
        xhat, rstd = _ln_stats(DN_ALPHA * x_ref[...] + m_ref[...])
        dy = DN_ALPHA * dr2_ref[...] + dh_ref[...]
        dg_ref[...] += jnp.sum(dy * xhat, axis=0, keepdims=True)
        db_ref[...] += jnp.sum(dy, axis=0, keepdims=True)
        dr_ref[...] = _ln_bwd(dy, xhat, rstd, g_ref[...])

    return _row_call(body, [x, mixed, dr2, dh_mlp], [g], [sds((S, D), F32)],
                     [sds((1, D), F32), sds((1, D), F32)], "ln1_bwd")


def _residual_grad(acc, dr):
    return DN_ALPHA * dr + acc


def _bucket_table():
    qi = np.arange(BLK, dtype=np.int32)[:, None]
    kj = np.arange(2 * BLK, dtype=np.int32)[None, :]
    dist = qi + BLK - kj
    n = np.maximum(dist, 0)
    max_exact = N_BUCKETS // 2
    nf = np.maximum(n, 1).astype(np.float32)
    large = max_exact + (np.log(nf / np.float32(max_exact)) / np.float32(math.log(MAX_DISTANCE / max_exact))
                         * np.float32(N_BUCKETS - max_exact)).astype(np.int32)
    large = np.minimum(large, N_BUCKETS - 1)
    bucket = np.where(n < max_exact, n, large)
    return np.where((dist >= 0) & (dist < BLK), bucket, -1).astype(np.int32)


def _attn_bias(rel_bias_t):
    hq = rel_bias_t.shape[0]
    bucket = jnp.asarray(_bucket_table())

    def body(rb_ref, bk_ref, o_ref):
        h = pl.program_id(0)
        bk = bk_ref[...]
        acc = jnp.zeros((BLK, 2 * BLK), F32)
        for b in range(N_BUCKETS):
            acc = jnp.where(bk == b, rb_ref[h, b], acc)
        o_ref[...] = acc

    return pl.pallas_call(
        body, grid=(hq,),
        in_specs=[pl.BlockSpec(memory_space=pltpu.SMEM), pl.BlockSpec((BLK, 2 * BLK), lambda h: (0, 0))],
        out_specs=pl.BlockSpec((BLK, 2 * BLK), lambda h: (h, 0)),
        out_shape=jax.ShapeDtypeStruct((hq * BLK, 2 * BLK), F32),
        compiler_params=_params(("arbitrary",)), name="attn_bias")(rel_bias_t, bucket)


def _attn_probs(sc, sp, bias, sink, mask_c, mask_p):
    lc = jnp.where(mask_c, sc + bias[:, BLK:], NEG_INF)
    lp = jnp.where(mask_p, sp + bias[:, :BLK], NEG_INF)
    m = jnp.maximum(jnp.maximum(jnp.max(lc, axis=1, keepdims=True), jnp.max(lp, axis=1, keepdims=True)), sink)
    pc, pp, ps = jnp.exp(lc - m), jnp.exp(lp - m), jnp.exp(sink - m)
    inv = 1.0 / (jnp.sum(pc, axis=1, keepdims=True) + jnp.sum(pp, axis=1, keepdims=True) + ps)
    return pc, pp, ps, inv


def _attn_masks(n):
    qi = lax.broadcasted_iota(jnp.int32, (BLK, BLK), 0)
    kj = lax.broadcasted_iota(jnp.int32, (BLK, BLK), 1)
    return kj <= qi, (kj > qi) & (n > 0)


def _attn_fwd(proj, bias, sinks, hq, q_blk, k_blk, v_blk, out_width):
    S = proj.shape[0]
    hkv = hq // GQA
    wq, wk = hq * HEAD_A, hkv * HEAD_A

    def body(q_ref, k_ref, v_ref, bias_ref, sink_ref, o_ref):
        n = pl.program_id(0)
        cur = pl.multiple_of(n * BLK, BLK)
        prev = pl.multiple_of(jnp.maximum(n - 1, 0) * BLK, BLK)
        mask_c, mask_p = _attn_masks(n)
        for h4 in range(hkv):
            cs = slice(h4 * HEAD_A, (h4 + 1) * HEAD_A)
            kc, kp = k_ref[pl.ds(cur, BLK), cs].astype(BF16), k_ref[pl.ds(prev, BLK), cs].astype(BF16)
            vc, vp = v_ref[pl.ds(cur, BLK), cs].astype(BF16), v_ref[pl.ds(prev, BLK), cs].astype(BF16)
            hs_of = [slice(h * HEAD_A, (h + 1) * HEAD_A) for h in range(h4 * GQA, (h4 + 1) * GQA)]
            qs = [(q_ref[:, hs] * (HEAD_A ** -0.5)).astype(BF16) for hs in hs_of]
            scs = [_dot(q, kc, NT) for q in qs]
            sps = [_dot(q, kp, NT) for q in qs]
            pr = [_attn_probs(scs[g], sps[g], bias_ref[(h4 * GQA + g) * BLK:(h4 * GQA + g + 1) * BLK, :],
                              sink_ref[h4 * GQA + g], mask_c, mask_p) for g in range(GQA)]
            oc = [_dot(p[0].astype(BF16), vc, NN) for p in pr]
            op = [_dot(p[1].astype(BF16), vp, NN) for p in pr]
            for g, hs in enumerate(hs_of):
                o_ref[:, hs] = (oc[g] + op[g]) * pr[g][3]

    return pl.pallas_call(
        body, grid=(S // BLK,),
        in_specs=[pl.BlockSpec((BLK, wq), lambda n: (n, q_blk)), pl.BlockSpec((S, wk), lambda n: (0, k_blk)),
                  pl.BlockSpec((S, wk), lambda n: (0, v_blk)), pl.BlockSpec((hq * BLK, 2 * BLK), lambda n: (0, 0)),
                  pl.BlockSpec(memory_space=pltpu.SMEM)],
        out_specs=pl.BlockSpec((BLK, wq), lambda n: (n, 0)),
        out_shape=jax.ShapeDtypeStruct((S, out_width), F32),
        compiler_params=_params(("arbitrary",)), name="attn_fwd")(proj, proj, proj, bias, sinks)


def _attn_bwd(proj, bias, sinks, out, dmix, hq, q_blk, k_blk, v_blk):
    S = proj.shape[0]
    hkv = hq // GQA
    wq, wk = hq * HEAD_A, hkv * HEAD_A
    sds = jax.ShapeDtypeStruct

    def body(q_ref, k_ref, v_ref, bias_ref, sink_ref, o_ref, do_ref, dq_ref, dk_ref, dv_ref, dbias_ref, dsink_ref):
        n = pl.program_id(0)

        @pl.when(n == 0)
        def _():
            dk_ref[...] = jnp.zeros_like(dk_ref)
            dv_ref[...] = jnp.zeros_like(dv_ref)
            dbias_ref[...] = jnp.zeros_like(dbias_ref)
            dsink_ref[...] = jnp.zeros_like(dsink_ref)

        cur = pl.multiple_of(n * BLK, BLK)
        prev = pl.multiple_of(jnp.maximum(n - 1, 0) * BLK, BLK)
        mask_c, mask_p = _attn_masks(n)
        for h4 in range(hkv):
            cs = slice(h4 * HEAD_A, (h4 + 1) * HEAD_A)
            kc, kp = k_ref[pl.ds(cur, BLK), cs].astype(BF16), k_ref[pl.ds(prev, BLK), cs].astype(BF16)
            vc, vp = v_ref[pl.ds(cur, BLK), cs].astype(BF16), v_ref[pl.ds(prev, BLK), cs].astype(BF16)
            heads = list(range(h4 * GQA, (h4 + 1) * GQA))
            hs_of = [slice(h * HEAD_A, (h + 1) * HEAD_A) for h in heads]
            rows_of = [slice(h * BLK, (h + 1) * BLK) for h in heads]
            G = range(GQA)
            qs = [(q_ref[:, hs] * (HEAD_A ** -0.5)).astype(BF16) for hs in hs_of]
            dos = [do_ref[:, hs] for hs in hs_of]
            dobs = [d.astype(BF16) for d in dos]
            scs = [_dot(q, kc, NT) for q in qs]
            sps = [_dot(q, kp, NT) for q in qs]
            dpc = [_dot(d, vc, NT) for d in dobs]
            dpp = [_dot(d, vp, NT) for d in dobs]
            pcs, pps, dscs, dsps = [], [], [], []
            for g in G:
                pc, pp, ps, inv = _attn_probs(scs[g], sps[g], bias_ref[rows_of[g], :], sink_ref[heads[g]], mask_c, mask_p)
                pc, pp, ps = pc * inv, pp * inv, ps * inv
                delta = jnp.sum(dos[g] * o_ref[:, hs_of[g]], axis=1, keepdims=True)
                dsc, dsp = pc * (dpc[g] - delta), pp * (dpp[g] - delta)
                dsink_ref[heads[g]:heads[g] + 1, :] += jnp.broadcast_to(jnp.sum(-ps * delta, axis=0, keepdims=True), (1, LANE))
                dbias_ref[rows_of[g], BLK:] += dsc
                dbias_ref[rows_of[g], :BLK] += dsp
                pcs.append(pc.astype(BF16))
                pps.append(pp.astype(BF16))
                dscs.append(dsc.astype(BF16))
                dsps.append(dsp.astype(BF16))
            dq1 = [_dot(dscs[g], kc, NN) for g in G]
            dq2 = [_dot(dsps[g], kp, NN) for g in G]
            dkc = [_dot(dscs[g], qs[g], TN) for g in G]
            dkp = [_dot(dsps[g], qs[g], TN) for g in G]
            dvc = [_dot(pcs[g], dobs[g], TN) for g in G]
            dvp = [_dot(pps[g], dobs[g], TN) for g in G]
            for g in G:
                dq_ref[:, hs_of[g]] = (dq1[g] + dq2[g]) * (HEAD_A ** -0.5)
            dk_ref[pl.ds(cur, BLK), cs] += sum(dkc[1:], dkc[0])
            dk_ref[pl.ds(prev, BLK), cs] += sum(dkp[1:], dkp[0])
            dv_ref[pl.ds(cur, BLK), cs] += sum(dvc[1:], dvc[0])
            dv_ref[pl.ds(prev, BLK), cs] += sum(dvp[1:], dvp[0])

    return pl.pallas_call(
        body, grid=(S // BLK,),
        in_specs=[pl.BlockSpec((BLK, wq), lambda n: (n, q_blk)), pl.BlockSpec((S, wk), lambda n: (0, k_blk)),
                  pl.BlockSpec((S, wk), lambda n: (0, v_blk)), pl.BlockSpec((hq * BLK, 2 * BLK), lambda n: (0, 0)),
                  pl.BlockSpec(memory_space=pltpu.SMEM),
                  pl.BlockSpec((BLK, wq), lambda n: (n, 0)), pl.BlockSpec((BLK, wq), lambda n: (n, 0))],
        out_specs=[pl.BlockSpec((BLK, wq), lambda n: (n, 0)), pl.BlockSpec((S, wk), lambda n: (0, 0)),
                   pl.BlockSpec((S, wk), lambda n: (0, 0)), pl.BlockSpec((hq * BLK, 2 * BLK), lambda n: (0, 0)),
                   pl.BlockSpec((hq, LANE), lambda n: (0, 0))],
        out_shape=[sds((S, wq), F32), sds((S, wk), F32), sds((S, wk), F32), sds((hq * BLK, 2 * BLK), F32),
                   sds((hq, LANE), F32)],
        compiler_params=_params(("arbitrary",)), name="attn_bwd")(proj, proj, proj, bias, sinks, out, dmix)


def _rel_bias_grad(dbias, hq):
    bucket = jnp.asarray(_bucket_table())

    def body(d_ref, bk_ref, o_ref):
        d = d_ref[...]
        bk = bk_ref[...]
        rows = [jnp.sum(jnp.where(bk == b, d, 0.0), axis=0, keepdims=True) for b in range(N_BUCKETS)]
        tot = jnp.sum(jnp.concatenate(rows, axis=0), axis=1, keepdims=True)
        o_ref[...] = jnp.broadcast_to(tot, (N_BUCKETS, LANE))

    return pl.pallas_call(
        body, grid=(hq,),
        in_specs=[pl.BlockSpec((BLK, 2 * BLK), lambda h: (h, 0)), pl.BlockSpec((BLK, 2 * BLK), lambda h: (0, 0))],
        out_specs=pl.BlockSpec((None, N_BUCKETS, LANE), lambda h: (h, 0, 0)),
        out_shape=jax.ShapeDtypeStruct((hq, N_BUCKETS, LANE), F32),
        compiler_params=_params(("arbitrary",)), name="rel_bias_grad")(dbias, bucket)


def _sigmoid(x):
    return 0.5 * jnp.tanh(0.5 * x) + 0.5


def _shift_rows(x, s, row):
    n = x.shape[0]
    if s > 0:
        return jnp.where(row >= s, pltpu.roll(x, s, 0), 0.0)
    return jnp.where(row < n + s, pltpu.roll(x, n + s, 0), 0.0)


def _conv_silu_norm(xv, w, j, nh):
    row = lax.broadcasted_iota(jnp.int32, xv.shape, 0)
    xs = [xv] + [_shift_rows(xv, s, row) for s in range(1, CONV_W)]
    c = w[CONV_W - 1:CONV_W, :] * xv
    for s in range(1, CONV_W):
        c = c + w[CONV_W - 1 - s:CONV_W - s, :] * xs[s]
    sg = _sigmoid(c)
    a = c * sg
    r = lax.rsqrt(jnp.sum(a * a, axis=1, keepdims=True) + RMS_EPS)
    scale = jnp.where(j < nh, HEAD_D ** -0.5, 1.0)
    is_norm = j < 2 * nh
    y = jnp.where(is_norm, a * (r * scale), a)
    return c, sg, a, r, scale, is_norm, xs, row, y


def _gdn_prep_fwd(proj, conv_w, nh, blk0):
    S = proj.shape[0]

    def body(x_ref, w_ref, o_ref):
        j = pl.program_id(0)
        o_ref[...] = _conv_silu_norm(x_ref[...], w_ref[...], j, nh)[-1]

    return pl.pallas_call(
        body, grid=(3 * nh,),
        in_specs=[pl.BlockSpec((S, HEAD_D), lambda j: (0, blk0 + j)), pl.BlockSpec((CONV_W, HEAD_D), lambda j: (0, j))],
        out_specs=pl.BlockSpec((S, HEAD_D), lambda j: (0, 3 * (j % nh) + j // nh)),
        out_shape=jax.ShapeDtypeStruct((S, 3 * nh * HEAD_D), F32),
        compiler_params=_params(("parallel",)), name="gdn_prep_fwd")(proj, conv_w)


def _gdn_prep_bwd(proj, conv_w, dqkv, nh, blk0):
    S = proj.shape[0]
    sds = jax.ShapeDtypeStruct

    def body(x_ref, w_ref, dy_ref, dx_ref, dw_ref):
        j = pl.program_id(0)
        xv, w = x_ref[...], w_ref[...]
        c, sg, a, r, scale, is_norm, xs, row, _ = _conv_silu_norm(xv, w, j, nh)
        dy = dy_ref[...]
        rs = r * scale
        da_n = rs * dy - a * (r * r * rs) * jnp.sum(dy * a, axis=1, keepdims=True)
        da = jnp.where(is_norm, da_n, dy)
        dc = da * (sg * (1.0 + c * (1.0 - sg)))
        dx = w[CONV_W - 1:CONV_W, :] * dc
        dws = [jnp.sum(dc * xv, axis=0, keepdims=True)]
        for s in range(1, CONV_W):
            dx = dx + w[CONV_W - 1 - s:CONV_W - s, :] * _shift_rows(dc, -s, row)
            dws.insert(0, jnp.sum(dc * xs[s], axis=0, keepdims=True))
        dx_ref[...] = dx
        dw_ref[...] = jnp.concatenate(dws, axis=0)

    return pl.pallas_call(
        body, grid=(3 * nh,),
        in_specs=[pl.BlockSpec((S, HEAD_D), lambda j: (0, blk0 + j)), pl.BlockSpec((CONV_W, HEAD_D), lambda j: (0, j)),
                  pl.BlockSpec((S, HEAD_D), lambda j: (0, 3 * (j % nh) + j // nh))],
        out_specs=[pl.BlockSpec((S, HEAD_D), lambda j: (0, j)), pl.BlockSpec((CONV_W, HEAD_D), lambda j: (0, j))],
        out_shape=[sds((S, 3 * nh * HEAD_D), F32), sds((CONV_W, 3 * nh * HEAD_D), F32)],
        compiler_params=_params(("parallel",)), name="gdn_prep_bwd")(proj, conv_w, dqkv)


def _softplus(x):
    return jnp.maximum(x, 0.0) + jnp.log(1.0 + jnp.exp(-jnp.abs(x)))


def _gates_fwd(ab, al, dt, nh):
    S = ab.shape[0]

    def body(ab_ref, al_ref, dt_ref, o_ref):
        v = ab_ref[...]
        lane = lax.broadcasted_iota(jnp.int32, v.shape, 1)
        g = -jnp.exp(al_ref[...]) * _softplus(v + dt_ref[...])
        o_ref[...] = jnp.where(lane < nh, g, jnp.where(lane < 2 * nh, _sigmoid(v), 0.0))

    row = pl.BlockSpec((1, LANE), lambda i: (0, 0))
    full = pl.BlockSpec((S, LANE), lambda i: (0, 0))
    return pl.pallas_call(body, grid=(1,), in_specs=[full, row, row], out_specs=full,
                          out_shape=jax.ShapeDtypeStruct((S, LANE), F32),
                          compiler_params=_params(("arbitrary",)), name="gates_fwd")(ab, al, dt)


def _gates_bwd(ab, al, dt, dgb, nh):
    S = ab.shape[0]
    sds = jax.ShapeDtypeStruct

    def body(ab_ref, al_ref, dt_ref, d_ref, dab_ref, dal_ref, ddt_ref):
        v, d = ab_ref[...], d_ref[...]
        lane = lax.broadcasted_iota(jnp.int32, v.shape, 1)
        is_a = lane < nh
        z = v + dt_ref[...]
        dsp = jnp.where(is_a, d * (-jnp.exp(al_ref[...])), 0.0)
        dz = dsp * _sigmoid(z)
        beta = _sigmoid(v)
        dab_ref[...] = jnp.where(is_a, dz, jnp.where(lane < 2 * nh, d * beta * (1.0 - beta), 0.0))
        dal_ref[...] = jnp.sum(dsp * _softplus(z), axis=0, keepdims=True)
        ddt_ref[...] = jnp.sum(dz, axis=0, keepdims=True)

    row = pl.BlockSpec((1, LANE), lambda i: (0, 0))
    full = pl.BlockSpec((S, LANE), lambda i: (0, 0))
    return pl.pallas_call(body, grid=(1,), in_specs=[full, row, row, full], out_specs=[full, row, row],
                          out_shape=[sds((S, LANE), F32), sds((1, LANE), F32), sds((1, LANE), F32)],
                          compiler_params=_params(("arbitrary",)), name="gates_bwd")(ab, al, dt, dgb)


def _col_of(tile, h):
    lane = lax.broadcasted_iota(jnp.int32, tile.shape, 1)
    return jnp.sum(jnp.where(lane == h, tile, 0.0), axis=1, keepdims=True)


def _to_row(col, eye):
    return jnp.sum(jnp.where(eye, col, 0.0), axis=0, keepdims=True)


def _to_col(row, eye):
    return jnp.sum(jnp.where(eye, row, 0.0), axis=1, keepdims=True)


def _split(a):
    hi = a.astype(BF16)
    return hi, (a - hi.astype(F32)).astype(BF16)


def _gdot(a, b, dims):
    ah, al = _split(a)
    bh, bl = _split(b)
    return _dot(ah, bh, dims) + (_dot(ah, bl, dims) + _dot(al, bh, dims))


def _bdot(a, b, dims):
    return _dot(a.astype(BF16), b.astype(BF16), dims)


def _chunks_local(qs, ks, vs, gcols, bcols, Ts=None):
    C = CHUNK
    row = lax.broadcasted_iota(jnp.int32, (C, C), 0)
    col = lax.broadcasted_iota(jnp.int32, (C, C), 1)
    tril, strict, eye = col <= row, col < row, col == row
    outs = []
    for k, gcol, bcol in zip(ks, gcols, bcols):
        grow = _to_row(gcol, eye)
        G_row = jnp.sum(jnp.where(row <= col, gcol, 0.0), axis=0, keepdims=True)
        G_col = jnp.sum(jnp.where(tril, grow, 0.0), axis=1, keepdims=True)
        G_last = G_col[C - 1:C, :]
        outs.append(dict(strict=strict, eye=eye, row=row, col=col, decay=jnp.exp(jnp.where(tril, G_col - G_row, NEG_INF)),
                         eG=jnp.exp(G_col), eGr=jnp.exp(G_last - G_col), gl=jnp.exp(G_last), kb=k * bcol))
    Ms = [_gdot(o["kb"], k, NT) for o, k in zip(outs, ks)]
    Ns = [_gdot(q, k, NT) for q, k in zip(qs, ks)]
    for o, q, k, M, N in zip(outs, qs, ks, Ms, Ns):
        o.update(A=jnp.where(strict, M * o["decay"], 0.0), attn=N * o["decay"], rhs_k=o["kb"] * o["eG"],
                 q_dec=q * o["eG"], k_dec=k * o["eGr"])
    if Ts is None:
        Ts = [jnp.where(eye, 1.0, 0.0) - o["A"] for o in outs]
        Ps = [o["A"] for o in outs]
        for _ in range(int(math.log2(C)) - 1):
            Ps = [_bdot(P, P, NN) for P in Ps]
            Ts = [T + _bdot(T, P, NN) for T, P in zip(Ts, Ps)]
        us = [_bdot(T, v * bcol, NN) for T, v, bcol in zip(Ts, vs, bcols)]
        ws = [_bdot(T, o["rhs_k"], NN) for T, o in zip(Ts, outs)]
        for o, T, u, w in zip(outs, Ts, us, ws):
            o.update(T=T, u=u, w=w)
    return outs


GDN_ROWS = 256
GDN_LOCAL_ROWS = 512
WQK = 3 * CHUNK


def _gdn_local_fwd(qkv, gb, nh):
    S = qkv.shape[0]
    nc = S // CHUNK
    rb = min(GDN_LOCAL_ROWS, S)
    cpb = rb // CHUNK
    sds = jax.ShapeDtypeStruct

    def body(q_ref, k_ref, v_ref, gb_ref, u_ref, wqk_ref, attn_ref, t_ref):
        h = pl.program_id(0)
        rows_of = [slice(ci * CHUNK, (ci + 1) * CHUNK) for ci in range(cpb)]
        gbts = [gb_ref[rows, :] for rows in rows_of]
        Ls = _chunks_local([q_ref[rows, :] for rows in rows_of], [k_ref[rows, :] for rows in rows_of],
                           [v_ref[rows, :] for rows in rows_of], [_col_of(t, h) for t in gbts],
                           [_col_of(t, nh + h) for t in gbts])
        for ci, (rows, L) in enumerate(zip(rows_of, Ls)):
            u_ref[rows, :] = L["u"]
            base = ci * WQK
            wqk_ref[base:base + CHUNK, :] = L["w"]
            wqk_ref[base + CHUNK:base + 2 * CHUNK, :] = L["q_dec"]
            wqk_ref[base + 2 * CHUNK:base + WQK, :] = L["k_dec"]
            attn_ref[ci] = L["attn"]
            t_ref[ci] = L["T"]

    cc = pl.BlockSpec((None, cpb, CHUNK, CHUNK), lambda h, i: (h, i, 0, 0))
    return pl.pallas_call(
        body, grid=(nh, S // rb),
        in_specs=[pl.BlockSpec((rb, HEAD_D), lambda h, i: (i, 3 * h)), pl.BlockSpec((rb, HEAD_D), lambda h, i: (i, 3 * h + 1)),
                  pl.BlockSpec((rb, HEAD_D), lambda h, i: (i, 3 * h + 2)), pl.BlockSpec((rb, LANE), lambda h, i: (i, 0))],
        out_specs=[pl.BlockSpec((rb, HEAD_D), lambda h, i: (i, h)),
                   pl.BlockSpec((None, 3 * rb, HEAD_D), lambda h, i: (h, i, 0)), cc, cc],
        out_shape=[sds((S, nh * HEAD_D), F32), sds((nh, 3 * S, HEAD_D), F32), sds((nh, nc, CHUNK, CHUNK), F32),
                   sds((nh, nc, CHUNK, CHUNK), F32)],
        compiler_params=_params(("parallel", "parallel")), name="gdn_local_fwd")(qkv, qkv, qkv, gb)


def _gdn_scan_fwd(u, wqk, attn, gb, nh, dep):
    S = u.shape[0]
    nc = S // CHUNK
    rb = min(GDN_ROWS, S)
    cpb = rb // CHUNK
    sds = jax.ShapeDtypeStruct

    def body(u_ref, wqk_ref, attn_ref, gb_ref, dep_ref, o_ref, vn_ref, st_ref, s_ref):
        @pl.when(pl.program_id(0) == 0)
        def _():
            s_ref[...] = jnp.zeros_like(s_ref)

        for ci in range(cpb):
            rows = slice(ci * CHUNK, (ci + 1) * CHUNK)
            glv = jnp.exp(jnp.sum(gb_ref[rows, :], axis=0, keepdims=True))
            base = ci * WQK
            heads = range(nh)
            cols = [slice(h * HEAD_D, (h + 1) * HEAD_D) for h in heads]
            states = [s_ref[h] for h in heads]
            rs = [_gdot(wqk_ref[h, base:base + 2 * CHUNK, :], states[h], NN) for h in heads]
            vbs = [u_ref[rows, cols[h]] - rs[h][:CHUNK] for h in heads]
            os_ = [_gdot(attn_ref[h, ci], vbs[h], NN) for h in heads]
            ks_ = [_gdot(wqk_ref[h, base + 2 * CHUNK:base + WQK, :], vbs[h], TN) for h in heads]
            for h in heads:
                st_ref[h, ci] = states[h]
                o_ref[rows, cols[h]] = rs[h][CHUNK:] + os_[h]
                vn_ref[rows, cols[h]] = vbs[h]
                s_ref[h] = states[h] * glv[:, h:h + 1] + ks_[h]

    return pl.pallas_call(
        body, grid=(S // rb,),
        in_specs=[pl.BlockSpec((rb, nh * HEAD_D), lambda i: (i, 0)), pl.BlockSpec((nh, 3 * rb, HEAD_D), lambda i: (0, i, 0)),
                  pl.BlockSpec((nh, cpb, CHUNK, CHUNK), lambda i: (0, i, 0, 0)), pl.BlockSpec((rb, LANE), lambda i: (i, 0)),
                  pl.BlockSpec(memory_space=pl.ANY)],
        out_specs=[pl.BlockSpec((rb, nh * HEAD_D), lambda i: (i, 0)), pl.BlockSpec((rb, nh * HEAD_D), lambda i: (i, 0)),
                   pl.BlockSpec((nh, cpb, HEAD_D, HEAD_D), lambda i: (0, i, 0, 0))],
        out_shape=[sds((S, nh * HEAD_D), F32), sds((S, nh * HEAD_D), F32), sds((nh, nc, HEAD_D, HEAD_D), F32)],
        scratch_shapes=[pltpu.VMEM((nh, HEAD_D, HEAD_D), F32)],
        compiler_params=_params(("arbitrary",)), name="gdn_scan_fwd")(u, wqk, attn, gb, dep)


def _gdn_scan_bwd(wqk, attn, gb, states, vn, do, nh, dep):
    S = vn.shape[0]
    nc = S // CHUNK
    rb = min(GDN_ROWS, S)
    cpb = rb // CHUNK
    last = S // rb - 1
    sds = jax.ShapeDtypeStruct

    def body(wqk_ref, attn_ref, gb_ref, st_ref, vn_ref, do_ref, dep_ref, dvn_ref, dw_ref, dqd_ref, dkd_ref, da_ref, dgl_ref,
             ds_ref):
        @pl.when(pl.program_id(0) == 0)
        def _():
            ds_ref[...] = jnp.zeros_like(ds_ref)

        row = lax.broadcasted_iota(jnp.int32, (CHUNK, CHUNK), 0)
        col = lax.broadcasted_iota(jnp.int32, (CHUNK, CHUNK), 1)
        for ci in reversed(range(cpb)):
            rows = slice(ci * CHUNK, (ci + 1) * CHUNK)
            glv = jnp.exp(jnp.sum(gb_ref[rows, :], axis=0, keepdims=True))
            base = ci * WQK
            heads = range(nh)
            cols = [slice(h * HEAD_D, (h + 1) * HEAD_D) for h in heads]
            states = [st_ref[h, ci] for h in heads]
            dSs = [ds_ref[h] for h in heads]
            vbs = [vn_ref[rows, cols[h]] for h in heads]
            dobs = [do_ref[rows, cols[h]] for h in heads]
            dv1 = [_gdot(attn_ref[h, ci], dobs[h], TN) for h in heads]
            dv2 = [_gdot(wqk_ref[h, base + 2 * CHUNK:base + WQK, :], dSs[h], NN) for h in heads]
            das = [_gdot(dobs[h], vbs[h], NT) for h in heads]
            dkds = [_gdot(vbs[h], dSs[h], NT) for h in heads]
            dvbs = [dv1[h] + dv2[h] for h in heads]
            xs = [_gdot(jnp.concatenate([dobs[h], dvbs[h]], axis=0), states[h], NT) for h in heads]
            dss = [_gdot(wqk_ref[h, base:base + 2 * CHUNK, :], jnp.concatenate([-dvbs[h], dobs[h]], axis=0), TN)
                   for h in heads]
            for h in heads:
                dqd_ref[rows, cols[h]] = xs[h][:CHUNK]
                dw_ref[rows, cols[h]] = -xs[h][CHUNK:]
                dvn_ref[rows, cols[h]] = dvbs[h]
                da_ref[h, ci] = jnp.where(col <= row, das[h], 0.0)
                dkd_ref[rows, cols[h]] = dkds[h]
                gl = glv[:, h:h + 1]
                dgl = jnp.sum(jnp.sum(states[h] * dSs[h], axis=0, keepdims=True), axis=1, keepdims=True)
                dgl_ref[h, ci] = jnp.broadcast_to(dgl * gl, (1, LANE))
                ds_ref[h] = dSs[h] * gl + dss[h]

    rv = lambda i: last - i
    wide = pl.BlockSpec((rb, nh * HEAD_D), lambda i: (rv(i), 0))
    return pl.pallas_call(
        body, grid=(S // rb,),
        in_specs=[pl.BlockSpec((nh, 3 * rb, HEAD_D), lambda i: (0, rv(i), 0)),
                  pl.BlockSpec((nh, cpb, CHUNK, CHUNK), lambda i: (0, rv(i), 0, 0)),
                  pl.BlockSpec((rb, LANE), lambda i: (rv(i), 0)),
                  pl.BlockSpec((nh, cpb, HEAD_D, HEAD_D), lambda i: (0, rv(i), 0, 0)), wide, wide,
                  pl.BlockSpec(memory_space=pl.ANY)],
        out_specs=[wide, wide, wide, wide, pl.BlockSpec((nh, cpb, CHUNK, CHUNK), lambda i: (0, rv(i), 0, 0)),
                   pl.BlockSpec((nh, cpb, 1, LANE), lambda i: (0, rv(i), 0, 0))],
        out_shape=[sds((S, nh * HEAD_D), F32), sds((S, nh * HEAD_D), F32), sds((S, nh * HEAD_D), F32),
                   sds((S, nh * HEAD_D), F32), sds((nh, nc, CHUNK, CHUNK), F32), sds((nh, nc, 1, LANE), F32)],
        scratch_shapes=[pltpu.VMEM((nh, HEAD_D, HEAD_D), F32)],
        compiler_params=_params(("arbitrary",)), name="gdn_scan_bwd")(wqk, attn, gb, states, vn, do, dep)


def _gdn_local_bwd(qkv, gb, T, u, wqk, dvn, dw, dqd, dkd, dattn, dgl, nh):
    S = qkv.shape[0]
    rb = min(GDN_LOCAL_ROWS, S)
    cpb = rb // CHUNK
    sds = jax.ShapeDtypeStruct

    def body(q_ref, k_ref, v_ref, gb_ref, t_ref, u_ref, wqk_ref, dvn_ref, dw_ref, dqd_ref, dkd_ref, da_ref, dgl_ref,
             dqkv_ref, dg_ref, db_ref):
        h = pl.program_id(0)
        n = range(cpb)
        rows_of = [slice(ci * CHUNK, (ci + 1) * CHUNK) for ci in n]
        qs, ks, vs = ([r[rows, :] for rows in rows_of] for r in (q_ref, k_ref, v_ref))
        gbts = [gb_ref[rows, :] for rows in rows_of]
        bcols = [_col_of(t, nh + h) for t in gbts]
        Ts = [t_ref[ci] for ci in n]
        Ls = _chunks_local(qs, ks, vs, [_col_of(t, h) for t in gbts], bcols, Ts=Ts)
        drvs = [_gdot(Ts[ci], dvn_ref[rows_of[ci], :], TN) for ci in n]
        drks = [_gdot(Ts[ci], dw_ref[rows_of[ci], :], TN) for ci in n]
        dAs = [jnp.where(Ls[ci]["strict"], -(_gdot(drvs[ci], u_ref[rows_of[ci], :], NT)
                                             + _gdot(drks[ci], wqk_ref[ci * WQK:ci * WQK + CHUNK, :], NT)), 0.0) for ci in n]
        dMs = [dAs[ci] * Ls[ci]["decay"] for ci in n]
        dNs = [da_ref[ci] * Ls[ci]["decay"] for ci in n]
        dkbs = [_gdot(dMs[ci], ks[ci], NN) for ci in n]
        dq1 = [_gdot(dNs[ci], ks[ci], NN) for ci in n]
        dk1 = [_gdot(dMs[ci], Ls[ci]["kb"], TN) for ci in n]
        dk2 = [_gdot(dNs[ci], qs[ci], TN) for ci in n]
        for ci in n:
            rows, L, q, k, v, bcol = rows_of[ci], Ls[ci], qs[ci], ks[ci], vs[ci], bcols[ci]
            eye, eG, eGr = L["eye"], L["eG"], L["eGr"]
            drv, drk, dkb = drvs[ci], drks[ci], dkbs[ci]
            dq_dec, dk_dec, dattn_c = dqd_ref[rows, :], dkd_ref[rows, :], da_ref[ci]
            dqkv_ref[rows, :HEAD_D] = dq1[ci] + dq_dec * eG
            dqkv_ref[rows, HEAD_D:2 * HEAD_D] = drk * (bcol * eG) + dk1[ci] + dkb * bcol + dk2[ci] + dk_dec * eGr
            dqkv_ref[rows, 2 * HEAD_D:] = drv * bcol
            db_ref[rows, :] = (jnp.sum(drv * v, axis=1, keepdims=True) + jnp.sum(drk * k, axis=1, keepdims=True) * eG
                               + jnp.sum(dkb * k, axis=1, keepdims=True))
            E = dAs[ci] * L["A"] + dattn_c * L["attn"]
            kd = jnp.sum(dk_dec * L["k_dec"], axis=1, keepdims=True)
            dG = (jnp.sum(dq_dec * L["q_dec"], axis=1, keepdims=True) - kd
                  + jnp.sum(drk * L["rhs_k"], axis=1, keepdims=True)
                  + jnp.sum(E, axis=1, keepdims=True) - _to_col(jnp.sum(E, axis=0, keepdims=True), eye))
            d_last = jnp.sum(kd, axis=0, keepdims=True) + dgl_ref[ci][:, :1]
            dG = dG + jnp.where(L["row"][:, :1] == CHUNK - 1, d_last, 0.0)
            dg_ref[rows, :] = jnp.sum(jnp.where(L["col"] >= L["row"], _to_row(dG, eye), 0.0), axis=1, keepdims=True)

    hd = pl.BlockSpec((rb, HEAD_D), lambda h, i: (i, h))
    cc = pl.BlockSpec((None, cpb, CHUNK, CHUNK), lambda h, i: (h, i, 0, 0))
    col1 = pl.BlockSpec((None, rb, 1), lambda h, i: (h, i, 0))
    return pl.pallas_call(
        body, grid=(nh, S // rb),
        in_specs=[pl.BlockSpec((rb, HEAD_D), lambda h, i: (i, 3 * h)), pl.BlockSpec((rb, HEAD_D), lambda h, i: (i, 3 * h + 1)),
                  pl.BlockSpec((rb, HEAD_D), lambda h, i: (i, 3 * h + 2)), pl.BlockSpec((rb, LANE), lambda h, i: (i, 0)),
                  cc, hd, pl.BlockSpec((None, 3 * rb, HEAD_D), lambda h, i: (h, i, 0)), hd, hd, hd, hd, cc,
                  pl.BlockSpec((None, cpb, 1, LANE), lambda h, i: (h, i, 0, 0))],
        out_specs=[pl.BlockSpec((rb, 3 * HEAD_D), lambda h, i: (i, h)), col1, col1],
        out_shape=[sds((S, 3 * nh * HEAD_D), F32)] + [sds((nh, S, 1), F32)] * 2,
        compiler_params=_params(("parallel", "parallel")), name="gdn_local_bwd")(
            qkv, qkv, qkv, gb, T, u, wqk, dvn, dw, dqd, dkd, dattn, dgl)


def _gated_norm_fwd(o, proj, norm_w, nh, z_blk0, mix, m_blk0):
    S = o.shape[0]

    def body(o_ref, z_ref, w_ref, mix_ref, y_ref):
        ov, z = o_ref[...], z_ref[...]
        r = lax.rsqrt(jnp.mean(ov * ov, axis=1, keepdims=True) + RMS_EPS)
        y_ref[...] = ov * r * w_ref[...] * (z * _sigmoid(z))

    return pl.pallas_call(
        body, grid=(nh,),
        in_specs=[pl.BlockSpec((S, HEAD_D), lambda h: (0, h)), pl.BlockSpec((S, HEAD_D), lambda h: (0, z_blk0 + h)),
                  pl.BlockSpec((1, HEAD_D), lambda h: (0, 0)), pl.BlockSpec(memory_space=pl.ANY)],
        out_specs=pl.BlockSpec((S, HEAD_D), lambda h: (0, m_blk0 + h)),
        out_shape=jax.ShapeDtypeStruct(mix.shape, F32), input_output_aliases={3: 0},
        compiler_params=_params(("parallel",)), name="gated_norm_fwd")(o, proj, norm_w, mix)


def _gated_norm_bwd(o, proj, norm_w, dmix, nh, z_blk0, d_blk0):
    S = o.shape[0]
    sds = jax.ShapeDtypeStruct

    def body(o_ref, z_ref, w_ref, dy_ref, do_ref, dz_ref, dw_ref):
        ov, z, w, dy = o_ref[...], z_ref[...], w_ref[...], dy_ref[...]
        r = lax.rsqrt(jnp.mean(ov * ov, axis=1, keepdims=True) + RMS_EPS)
        oh = ov * r
        sg = _sigmoid(z)
        dz_ref[...] = dy * (oh * w) * (sg * (1.0 + z * (1.0 - sg)))
        don = dy * (z * sg)
        @pl.when(pl.program_id(0) == 0)
        def _():
            dw_ref[...] = jnp.zeros_like(dw_ref)

        dw_ref[...] += jnp.sum(don * oh, axis=0, keepdims=True)
        doh = don * w
        do_ref[...] = r * (doh - oh * jnp.mean(doh * oh, axis=1, keepdims=True))

    return pl.pallas_call(
        body, grid=(nh,),
        in_specs=[pl.BlockSpec((S, HEAD_D), lambda h: (0, h)), pl.BlockSpec((S, HEAD_D), lambda h: (0, z_blk0 + h)),
                  pl.BlockSpec((1, HEAD_D), lambda h: (0, 0)), pl.BlockSpec((S, HEAD_D), lambda h: (0, d_blk0 + h))],
        out_specs=[pl.BlockSpec((S, HEAD_D), lambda h: (0, h)), pl.BlockSpec((S, HEAD_D), lambda h: (0, h)),
                   pl.BlockSpec((1, HEAD_D), lambda h: (0, 0))],
        out_shape=[sds((S, nh * HEAD_D), F32), sds((S, nh * HEAD_D), F32), sds((1, HEAD_D), F32)],
        compiler_params=_params(("arbitrary",)), name="gated_norm_bwd")(o, proj, norm_w, dmix)


def _adamw_math(w, g, m, v):
    m = ADAM_B1 * m + (1.0 - ADAM_B1) * g
    v = ADAM_B2 * v + (1.0 - ADAM_B2) * (g * g)
    m_hat = m / (1.0 - ADAM_B1 ** ADAM_STEP)
    v_hat = v / (1.0 - ADAM_B2 ** ADAM_STEP)
    delta = -ADAM_LR * (m_hat / (jnp.sqrt(v_hat) + ADAM_EPS) + ADAM_WD * w)
    return delta, m, v


def _slab_tiles(R, C, rows=256, cols=256):
    if R % rows == 0:
        return (rows, C), R // rows, lambda i: (i, 0)
    tc = _tile(C, cols)
    return (R, tc), C // tc, lambda i: (0, i)


def _adamw_big(parts, terms, chip, w, m, v, name):
    R, C = w.shape
    blk, steps, at = _slab_tiles(R, C)
    sds = jax.ShapeDtypeStruct

    def body(q_ref, p_ref, t_ref, w_ref, m_ref, v_ref, g_ref, d_ref, nm_ref, nv_ref):
        g = ((p_ref[...].astype(F32) + t_ref[0].astype(F32)) + t_ref[1].astype(F32)) + t_ref[2].astype(F32)
        g_ref[...] = g
        d_ref[...], nm_ref[...], nv_ref[...] = _adamw_math(w_ref[...], g, m_ref[...], v_ref[...])

    spec = pl.BlockSpec(blk, lambda i, q_ref: at(i))
    grid_spec = pltpu.PrefetchScalarGridSpec(
        num_scalar_prefetch=1, grid=(steps,),
        in_specs=[pl.BlockSpec((None,) + blk, lambda i, q_ref: (q_ref[0],) + at(i)),
                  pl.BlockSpec((3,) + blk, lambda i, q_ref: (0,) + at(i)), spec, spec, spec],
        out_specs=[spec] * 4)
    return pl.pallas_call(body, grid_spec=grid_spec, out_shape=[sds((R, C), F32)] * 4,
                          compiler_params=_params(("parallel",)), name=name)(chip, parts, terms, w, m, v)


def _adamw_small(ws, gs, ms, vs):
    n = len(ws)

    def body(*refs):
        for i in range(n):
            w, g, m, v = (refs[k * n + i][...] for k in range(4))
            d, nm, nv = _adamw_math(w, g, m, v)
            refs[4 * n + i][...] = d
            refs[5 * n + i][...] = nm
            refs[6 * n + i][...] = nv

    shapes = [jax.ShapeDtypeStruct(w.shape, F32) for w in ws]
    vm = pl.BlockSpec(memory_space=pltpu.VMEM)
    outs = pl.pallas_call(body, in_specs=[vm] * (4 * n), out_specs=[vm] * (3 * n), out_shape=shapes * 3,
                          name="adamw_small")(*ws, *gs, *ms, *vs)
    return outs[:n], outs[n:2 * n], outs[2 * n:]


MESH = pl.DeviceIdType.MESH
ANY = pl.BlockSpec(memory_space=pl.ANY)


def _place():
    x, y, c = lax.axis_index("x"), lax.axis_index("y"), lax.axis_index("c")
    return x, y, c, [(1 - x, y), (x, 1 - y), (1 - x, 1 - y)]


def _chip_sum(grad, recv, core, name):
    _, R, C = grad.shape
    blk, steps, at = _slab_tiles(R, C, rows=512 if R % 512 == 0 else 256, cols=512)

    def body(c_ref, g_ref, r_ref, o_ref):
        o_ref[...] = (g_ref[...].astype(F32) + r_ref[...].astype(F32)).astype(o_ref.dtype)

    grid_spec = pltpu.PrefetchScalarGridSpec(
        num_scalar_prefetch=1, grid=(4, steps),
        in_specs=[pl.BlockSpec((None,) + blk, lambda q, i, c_ref: (2 * q + c_ref[0],) + at(i)),
                  pl.BlockSpec((None,) + blk, lambda q, i, c_ref: (q,) + at(i))],
        out_specs=pl.BlockSpec((None,) + blk, lambda q, i, c_ref: (q,) + at(i)))
    return pl.pallas_call(body, grid_spec=grid_spec, out_shape=jax.ShapeDtypeStruct((4, R, C), BF16),
                          compiler_params=_params(("parallel", "parallel")), name=name)(core, grad, recv)


HBM_SPEC = pl.BlockSpec(memory_space=pltpu.HBM)
SEM_SPEC = pl.BlockSpec(memory_space=pltpu.SEMAPHORE)
DATAFLOW = pltpu.SideEffectType.DATAFLOW_SIDE_EFFECTING


def _split_start(name, bufs, plan, counts, after=None):
    nb, ng = len(bufs), len(counts)
    extra = [] if after is None else [after]
    place = [(g, k) for g, cnt in enumerate(counts) for k in range(cnt)]

    def body(*refs):
        sems, token = refs[nb + len(extra):nb + len(extra) + 2 * ng], refs[-1]
        for (g, k), (src, dst, to) in zip(place, plan(refs[:nb])):
            pltpu.make_async_remote_copy(src_ref=src, dst_ref=dst, send_sem=sems[2 * g].at[k], recv_sem=sems[2 * g + 1].at[k],
                                         device_id=to, device_id_type=MESH).start()
        token[...] = jnp.zeros_like(token)

    outs = pl.pallas_call(
        body, name=name,
        out_shape=(*[pltpu.SemaphoreType.DMA((cnt,)) for cnt in counts for _ in range(2)],
                   *[pltpu.HBM(b.shape, b.dtype) for b in bufs], jax.ShapeDtypeStruct((8, LANE), F32)),
        in_specs=[HBM_SPEC] * nb + [ANY] * len(extra),
        out_specs=(*[SEM_SPEC] * (2 * ng), *[HBM_SPEC] * nb, pl.BlockSpec(memory_space=pltpu.VMEM)),
        input_output_aliases={i: 2 * ng + i for i in range(nb)},
        compiler_params=pltpu.CompilerParams(has_side_effects=DATAFLOW))(
            *[pltpu.with_memory_space_constraint(b, pltpu.HBM) for b in bufs], *extra)
    return [(outs[2 * g], outs[2 * g + 1]) for g in range(ng)], list(outs[2 * ng:2 * ng + nb]), outs[-1]


def _split_wait(name, sems, bufs, plan, after):
    nb = len(bufs)
    send_sems, recv_sems = sems

    def body(*refs):
        send_s, recv_s = refs[nb], refs[nb + 1]
        for k, (src, dst, to) in enumerate(plan(refs[:nb])):
            cp = pltpu.make_async_remote_copy(src_ref=src, dst_ref=dst, send_sem=send_s.at[k], recv_sem=recv_s.at[k],
                                              device_id=to, device_id_type=MESH)
            cp.wait_send()
            cp.wait_recv()

    after = tuple(after) if isinstance(after, (tuple, list)) else (after,)
    outs = pl.pallas_call(
        body, name=name, out_shape=tuple(pltpu.HBM(b.shape, b.dtype) for b in bufs),
        in_specs=[HBM_SPEC] * nb + [SEM_SPEC, SEM_SPEC] + [ANY] * len(after), out_specs=tuple([HBM_SPEC] * nb),
        input_output_aliases={i: i for i in range(nb)},
        compiler_params=pltpu.CompilerParams(has_side_effects=DATAFLOW))(*bufs, send_sems, recv_sems, *after)
    return list(outs)


def _slot(px, py, pc):
    return 4 * px + 2 * py + pc


class _Gather:
    def __init__(self, shards, groups, dev):
        self.shards, self.groups, self.dev = shards, groups, dev
        self.second = {}

    @staticmethod
    def _plan1(pairs, refs):
        x, y, c, chips = _place()
        out = []
        for s, l in pairs:
            dst = refs[l].at[_slot(x, y, c)]
            out.append((refs[s], dst, (x, y, 1 - c)))
            out += [(refs[s], dst, (px, py, c)) for px, py in chips]
        return out

    @staticmethod
    def _plan2(refs):
        x, y, c, chips = _place()
        return [(r.at[_slot(px, py, c)],) * 2 + ((x, y, 1 - c),) for r in refs for px, py in chips]

    def start(self):
        n = len(self.shards)
        lands = [lax.dynamic_update_slice(lax.empty((N_DEV,) + s.shape, s.dtype), s[None], (self.dev, 0, 0))
                 for s in self.shards]
        pairs = [(w, n + w) for g in self.groups for w in g]
        sems, bufs, token = _split_start("gather_start_1", list(self.shards) + lands, functools.partial(self._plan1, pairs),
                                         tuple(4 * len(g) for g in self.groups))
        self.first = [(sems[i], [bufs[w] for w in g], [bufs[n + w] for w in g]) for i, g in enumerate(self.groups)]
        return token

    def mid(self, gi, after):
        sems, srcs, lands = self.first[gi]
        m = len(srcs)
        plan = functools.partial(self._plan1, [(w, m + w) for w in range(m)])
        lands = _split_wait("gather_%d_wait_1" % gi, sems, srcs + lands, plan, after)[m:]
        sems, lands, token = _split_start("gather_%d_start_2" % gi, lands, self._plan2, (3 * m,))
        self.second[gi] = (sems[0], lands)
        return token

    def finish(self, gi, after):
        sems, lands = self.second[gi]
        return _split_wait("gather_%d_wait_2" % gi, sems, lands, self._plan2, after)


class _Exchanges:
    def __init__(self, tag, n, core, gather=None):
        self.tag, self.n, self.core, self.gather = tag, n, core, gather

    def weights_mid(self, group, after):
        return self.gather.mid(group, after)

    def weights_finish(self, group, after):
        return self.gather.finish(group, after)

    def _reduce_plan1(self, refs):
        n = self.n
        x, y, c, _ = _place()
        return [(refs[w].at[2 * q + (1 - c)], refs[n + w].at[q], (x, y, 1 - c)) for w in range(n) for q in range(4)]

    def _reduce_plan2(self, refs):
        n = self.n
        x, y, c, chips = _place()
        return [(refs[w].at[2 * px + py], refs[n + w].at[j], (px, py, c))
                for w in range(n) for j, (px, py) in enumerate(chips)]

    def grads_start(self, grads):
        lands = [lax.empty((4,) + g.shape[1:], g.dtype) for g in grads]
        self.g1 = _split_start(self.tag + "reduce_start_1", list(grads) + lands, self._reduce_plan1, (4 * self.n,))
        return self.g1[2]

    def grads_mid(self, after):
        n = self.n
        sems, bufs, _ = self.g1
        bufs = _split_wait(self.tag + "reduce_wait_1", sems[0], bufs, self._reduce_plan1, after)
        core = self.core.reshape(1).astype(jnp.int32)
        self.parts = [_chip_sum(bufs[w], bufs[n + w], core, self.tag + "reduce_chip_sum_%d" % w) for w in range(n)]
        lands = [lax.empty((3,) + p.shape[1:], p.dtype) for p in self.parts]
        self.g2 = _split_start(self.tag + "reduce_start_2", self.parts + lands, self._reduce_plan2, (3 * n,))
        return self.g2[2]

    def grads_finish(self, after):
        n = self.n
        sems, bufs, _ = self.g2
        bufs = _split_wait(self.tag + "reduce_wait_2", sems[0], bufs, self._reduce_plan2, after)
        self.parts, self.terms = bufs[:n], bufs[n:]


def _all_reduce_small(buf):
    R = buf.shape[0]

    def body(x_ref, o_ref, g_ref, send_sems, recv_sems):
        x, y, c, chips = _place()
        me, sibling = (x, y, c), (x, y, 1 - c)

        def slot(px, py, pc):
            return 4 * px + 2 * py + pc

        def copy(k, block, to, src=None):
            dst = g_ref.at[slot(*block)]
            return pltpu.make_async_remote_copy(src_ref=dst if src is None else src, dst_ref=dst,
                                                send_sem=send_sems.at[k], recv_sem=recv_sems.at[k],
                                                device_id=to, device_id_type=MESH)

        first = [copy(0, me, sibling, src=x_ref)]
        first += [copy(1 + j, me, (*chip, c), src=x_ref) for j, chip in enumerate(chips)]
        for cp in first:
            cp.start()
        g_ref[slot(*me)] = x_ref[...]
        passed = [copy(4 + j, (*chip, c), sibling) for j, chip in enumerate(chips)]
        for j, chip in enumerate(chips):
            copy(1 + j, (*chip, c), me).wait_recv()
            passed[j].start()
        copy(0, sibling, me).wait_recv()
        for j, chip in enumerate(chips):
            copy(4 + j, (*chip, 1 - c), me).wait_recv()
        for cp in first + passed:
            cp.wait_send()
        acc = g_ref[0]
        for s in range(1, N_DEV):
            acc = acc + g_ref[s]
        o_ref[...] = acc

    vm = pl.BlockSpec(memory_space=pltpu.VMEM)
    return pl.pallas_call(
        body, in_specs=[vm], out_specs=vm, out_shape=jax.ShapeDtypeStruct((R, LANE), F32),
        scratch_shapes=[pltpu.VMEM((N_DEV, R, LANE), F32), pltpu.SemaphoreType.DMA((7,)), pltpu.SemaphoreType.DMA((7,))],
        name="all_reduce_small")(buf)


def _pad_cols(a, width):
    return jnp.pad(a, ((0, 0), (0, width - a.shape[1])))


def _local_step(x, target, w_in_g, conv_w, a_log, dt_bias, delta_norm_w, sinks, bias, ln1_g, ln1_b, ln2_g, ln2_b,
                ex, ex_in):
    S, D = x.shape
    aw = D // 2
    hq, hkv, nh = aw // HEAD_A, aw // HEAD_A // GQA, aw // HEAD_D
    kvw = hkv * HEAD_A
    c_q, c_k, c_v, c_d = 0, aw, aw + kvw, aw + 2 * kvw
    c_ab = c_d + 3 * aw
    c_z = c_ab + 2 * nh
    n_in = c_z + aw
    n_slab = w_in_g.shape[1]
    assert w_in_g.shape == (N_DEV, n_in // N_DEV, D), (w_in_g.shape, n_in)

    w_in_t = w_in_g.reshape(n_in, D)
    w_pt = jnp.concatenate([w_in_t[:c_ab], w_in_t[c_z:], jnp.pad(w_in_t[c_ab:c_z], ((0, LANE - 2 * nh), (0, 0)))], axis=0)
    p_z, p_ab = c_ab, c_ab + aw
    n_p = p_ab + LANE

    xb = x.astype(BF16)
    proj = _matmul(xb, w_pt, NT, name="proj", tn=1152)
    attn_out = _attn_fwd(proj, bias, sinks.reshape(-1), hq, 0, c_k // kvw, c_v // kvw, D)
    conv2 = conv_w.reshape(CONV_W, 3 * aw)
    qkv = _gdn_prep_fwd(proj, conv2, nh, c_d // HEAD_D)
    ab = proj[:, p_ab:]
    al, dt = _pad_cols(a_log, LANE), _pad_cols(dt_bias, LANE)
    gb = _gates_fwd(ab, al, dt, nh)
    u_d, wqk, attn_d, t_d = _gdn_local_fwd(qkv, gb, nh)
    o_d, vn, states = _gdn_scan_fwd(u_d, wqk, attn_d, gb, nh, ex.weights_mid(1, u_d))
    mix = _gated_norm_fwd(o_d, proj, delta_norm_w, nh, p_z // HEAD_D, attn_out, aw // HEAD_D)
    w_o_g, w_up_g = ex.weights_finish(1, mix)
    w_o = w_o_g.reshape(D, D)
    mixed = _matmul(mix, w_o, NN, name="out_proj")
    h1 = _ln1_fwd(x, mixed, ln1_g, ln1_b)
    u = _matmul(h1, w_up_g, NN, name="mlp_up", b_groups=True, out_dtype=BF16, deps=(ex.weights_mid(2, h1),))
    (w_down_g,) = ex.weights_finish(2, u)
    w_down = w_down_g.reshape(-1, D)
    mlp = _matmul(u, w_down, NN, name="mlp_down", a_fn=_relu_sq, tk=4096)
    dr2, loss_row, dln2_g, dln2_b = _ln2_loss(h1, mlp, ln2_g, ln2_b, target)

    du = _matmul(dr2, w_down, NT, name="d_mlp_act", epi=_relu_sq_grad, epi_in=(u,), out_dtype=BF16)
    dw_down = _matmul(u, dr2, TN, name="dw_down", a_fn=_relu_sq, out_dtype=BF16)
    dw_up = _matmul(h1, du, TN, name="dw_up", out_dtype=BF16, out_groups=N_DEV)
    dh_mlp = _matmul(du, w_up_g, NT, name="d_h1", b_groups=True, tk=4096)
    dr1, dln1_g, dln1_b = _ln1_bwd(x, mixed, ln1_g, dr2, dh_mlp)
    dw_o = _matmul(mix, dr1, TN, name="dw_o", out_dtype=BF16)
    tok = ex.grads_start([dw_o.reshape(N_DEV, -1, D), dw_up, dw_down.reshape(N_DEV, -1, D)])
    dmix = _matmul(dr1, w_o, NT, name="d_mix", deps=(tok,))
    dq_a, dk_a, dv_a, dbias, dsink = _attn_bwd(proj, bias, sinks.reshape(-1), mix, dmix, hq, 0, c_k // kvw, c_v // kvw)
    drel = _rel_bias_grad(dbias, hq)
    do_d, dz, dnw = _gated_norm_bwd(o_d, proj, delta_norm_w, dmix, nh, p_z // HEAD_D, aw // HEAD_D)
    dvn_s, dw_s, dqd, dkd, dattn_d, dgl = _gdn_scan_bwd(wqk, attn_d, gb, states, vn, do_d, nh, ex.grads_mid(dq_a))
    dqkv_n, dg, dbeta = _gdn_local_bwd(qkv, gb, t_d, u_d, wqk, dvn_s, dw_s, dqd, dkd, dattn_d, dgl, nh)
    dgb = _pad_cols(jnp.concatenate([dg.reshape(nh, S).T, dbeta.reshape(nh, S).T], axis=1), LANE)
    dab, da_log, ddt_bias = _gates_bwd(ab, al, dt, dgb, nh)
    dqkv_d, dconv = _gdn_prep_bwd(proj, conv2, dqkv_n, nh, c_d // HEAD_D)
    dproj = jnp.concatenate([dq_a, dk_a, dv_a, dqkv_d, dz, dab], axis=1).astype(BF16)
    dw_pt = _matmul(dproj, xb, TN, name="dw_in", out_dtype=BF16, tm=1152)

    def grad_rows(lo, hi):
        cuts = sorted({lo, hi, *[c for c in (c_ab, c_z) if lo < c < hi]})
        place = lambda r: r if r < c_ab else (p_ab + r - c_ab if r < c_z else p_z + r - c_z)
        return [dw_pt[place(a):place(a) + b - a] for a, b in zip(cuts[:-1], cuts[1:])]

    dw_in_g = jnp.stack([jnp.concatenate(grad_rows(g * n_slab, (g + 1) * n_slab), axis=0) for g in range(N_DEV)])
    tok = ex_in.grads_mid(ex_in.grads_start([dw_in_g]))
    grad_x = _matmul(dproj, w_pt, NN, name="d_x", tm=512, tk=n_p, deps=(tok,), epi=_residual_grad, epi_in=(dr1,))
    ex.grads_finish(grad_x)

    small = dict(conv_w=dconv, a_log=da_log[:, :nh], dt_bias=ddt_bias[:, :nh], delta_norm_w=dnw,
                 attn_sinks=dsink[:, 0].reshape(1, hq), rel_bias=drel[:, :, 0].T,
                 ln1_g=dln1_g, ln1_b=dln1_b, ln2_g=dln2_g, ln2_b=dln2_b)
    return loss_row, grad_x, small


SMALL_ORDER = ("conv_w", "a_log", "dt_bias", "delta_norm_w", "attn_sinks", "rel_bias", "ln1_g", "ln1_b", "ln2_g", "ln2_b")


def _pack_small(loss_row, small):
    parts = [loss_row.reshape(-1)]
    for k in SMALL_ORDER:
        flat = small[k].reshape(-1)
        parts.append(jnp.pad(flat, (0, (-flat.shape[0]) % LANE)))
    flat = jnp.concatenate(parts)
    flat = jnp.pad(flat, (0, (-flat.shape[0]) % (8 * LANE)))
    return flat.reshape(-1, LANE)


def _unpack_small(buf, small_shapes):
    flat = buf.reshape(-1)
    loss = flat[0]
    off = LANE
    out = {}
    for k in SMALL_ORDER:
        n = int(np.prod(small_shapes[k]))
        out[k] = flat[off:off + n].reshape(small_shapes[k])
        off += n + (-n) % LANE
    return loss, out


def kernel(x, w_in, conv_w, a_log, dt_bias, delta_norm_w, attn_sinks, rel_bias, w_o, ln1_g, ln1_b, w_up, w_down, ln2_g, ln2_b, loss_target, m_w_in, m_conv_w, m_a_log, m_dt_bias, m_delta_norm_w, m_attn_sinks, m_rel_bias, m_w_o, m_ln1_g, m_ln1_b, m_w_up, m_w_down, m_ln2_g, m_ln2_b, v_w_in, v_conv_w, v_a_log, v_dt_bias, v_delta_norm_w, v_attn_sinks, v_rel_bias, v_w_o, v_ln1_g, v_ln1_b, v_w_up, v_w_down, v_ln2_g, v_ln2_b):
    S, D = x.shape[1], x.shape[2]
    core = lax.axis_index("c")
    dev = 4 * lax.axis_index("x") + 2 * lax.axis_index("y") + core

    gather = _Gather([conv_w[0, :, 0, :], w_in[0].T.astype(BF16), w_o[0].astype(BF16), w_up[0].astype(BF16),
                      w_down[0].astype(BF16)], [[0, 1], [2, 3], [4]], dev)
    token = gather.start()
    bias = _attn_bias(rel_bias.T)
    w_t, m_t, v_t = w_in[0].T, m_w_in[0].T, v_w_in[0].T
    conv_g, w_in_g = gather.finish(0, gather.mid(0, (token, bias, w_t, m_t, v_t)))
    ex = _Exchanges("", 3, core, gather)
    ex_in = _Exchanges("in_", 1, core)

    cw_sh = conv_w.shape[3]
    conv_full = jnp.transpose(conv_g, (1, 0, 2)).reshape(CONV_W, N_DEV * cw_sh)

    loss_row, grad_x, small = _local_step(
        x[0], loss_target[0], w_in_g, conv_full, a_log, dt_bias, delta_norm_w, attn_sinks, bias,
        ln1_g, ln1_b, ln2_g, ln2_b, ex, ex_in)

    chip_arr = (dev // 2).reshape(1).astype(jnp.int32)
    big = {}
    for i, (name, w, m, v) in enumerate((("w_o", w_o, m_w_o, v_w_o), ("w_up", w_up, m_w_up, v_w_up),
                                         ("w_down", w_down, m_w_down, v_w_down))):
        big[name] = [o[None] for o in _adamw_big(ex.parts[i], ex.terms[i], chip_arr, w[0], m[0], v[0], "adamw_" + name)]
    ex_in.grads_finish(big["w_down"][0])
    outs = _adamw_big(ex_in.parts[0], ex_in.terms[0], chip_arr, w_t, m_t, v_t, "adamw_w_in")
    big["w_in"] = [o.T[None] for o in outs]

    small_shapes = {k: v.shape for k, v in small.items()}
    loss, small = _unpack_small(_all_reduce_small(_pack_small(loss_row, small)), small_shapes)
    small["conv_w"] = lax.dynamic_slice(small["conv_w"], (0, dev * cw_sh), (CONV_W, cw_sh))
    small["rel_bias"] = small["rel_bias"].reshape(rel_bias.shape)
    p2 = dict(conv_w=(conv_w, m_conv_w, v_conv_w), a_log=(a_log, m_a_log, v_a_log), dt_bias=(dt_bias, m_dt_bias, v_dt_bias),
              delta_norm_w=(delta_norm_w, m_delta_norm_w, v_delta_norm_w), attn_sinks=(attn_sinks, m_attn_sinks, v_attn_sinks),
              rel_bias=(rel_bias, m_rel_bias, v_rel_bias), ln1_g=(ln1_g, m_ln1_g, v_ln1_g), ln1_b=(ln1_b, m_ln1_b, v_ln1_b),
              ln2_g=(ln2_g, m_ln2_g, v_ln2_g), ln2_b=(ln2_b, m_ln2_b, v_ln2_b))
    two_d = lambda a: a.reshape(-1, a.shape[-1])
    ws = [two_d(p2[k][0]) for k in SMALL_ORDER]
    gs = [two_d(small[k]) for k in SMALL_ORDER]
    ms = [two_d(p2[k][1]) for k in SMALL_ORDER]
    vs = [two_d(p2[k][2]) for k in SMALL_ORDER]
    ds, nms, nvs = _adamw_small(ws, gs, ms, vs)
    res = {}
    for i, k in enumerate(SMALL_ORDER):
        shp = p2[k][0].shape
        res[k] = [gs[i].reshape(shp), ds[i].reshape(shp), nms[i].reshape(shp), nvs[i].reshape(shp)]
    res.update(big)
    order = ("w_in", "conv_w", "a_log", "dt_bias", "delta_norm_w", "attn_sinks", "rel_bias", "w_o", "ln1_g", "ln1_b",
             "w_up", "w_down", "ln2_g", "ln2_b")
    return (loss, grad_x[None], *[res[k][0] for k in order], *[res[k][1] for k in order],
            *[res[k][2] for k in order], *[res[k][3] for k in order])
```

```python
import functools
import math

import numpy as np
import jax
import jax.numpy as jnp
from jax import lax
from jax.experimental import pallas as pl
from jax.experimental.pallas import tpu as pltpu

F32 = jnp.float32
BF16 = jnp.bfloat16
HIGHEST = lax.Precision.HIGHEST

N_DEV = 8
HEAD_A = 64
GQA = 4
BLK = 128
N_BUCKETS = 32
MAX_DISTANCE = 128
HEAD_D = 128
CONV_W = 4
CHUNK = 64
NEG_INF = -1e30
LN_EPS = 1e-5
RMS_EPS = 1e-6
DN_ALPHA = 2.0 ** 0.25
ADAM_LR, ADAM_B1, ADAM_B2, ADAM_EPS, ADAM_WD, ADAM_STEP = 0.001, 0.9, 0.999, 1e-08, 0.01, 10

LANE = 128
VMEM_LIMIT = 56 * 1024 * 1024

NN = ((1,), (0,))
NT = ((1,), (1,))
TN = ((0,), (0,))


def _dot(a, b, dims, prec=None):
    return lax.dot_general(a, b, (dims, ((), ())), precision=prec, preferred_element_type=F32)


def _tile(dim, pref):
    if dim <= pref:
        return dim
    t = (pref // LANE) * LANE
    while t > LANE and dim % t:
        t -= LANE
    assert dim % t == 0, (dim, pref)
    return t


def _params(sem):
    return pltpu.CompilerParams(dimension_semantics=sem, vmem_limit_bytes=VMEM_LIMIT)


def _matmul(a, b, dims, *, name, out_dtype=F32, tm=1024, tn=1024, tk=2048, a_fn=None, epi=None, epi_in=(),
            b_groups=None, out_groups=None, deps=()):
    (ca,), (cb,) = dims
    M, K = a.shape[1 - ca], a.shape[ca]
    if b_groups:
        G, R, C = b.shape
        bshape = (R, G * C)
    else:
        bshape = b.shape
    N = bshape[1 - cb]
    assert bshape[cb] == K, (a.shape, b.shape, dims)
    tm, tk = _tile(M, tm), _tile(K, tk)
    if b_groups:
        lim = C if cb == 0 else tn
        tn = _tile(N, min(tn, lim))
        if cb == 1 and tk < C:
            tk = _tile(K, min(tk, C))
        elif cb == 1:
            tk = C * max(1, tk // C)
    else:
        tn = _tile(N, tn)
    if out_groups:
        tn = _tile(N, min(tn, N // out_groups))
    nk = K // tk
    b_span = tk // C if (b_groups and cb == 1 and tk > C) else 1

    def body(*refs):
        a_ref, b_ref = refs[0], refs[1]
        e_refs = refs[2:2 + len(epi_in)]
        o_ref = refs[2 + len(epi_in) + len(deps)]
        acc_ref = refs[3 + len(epi_in) + len(deps)] if nk > 1 else None
        k = pl.program_id(2)
        av = a_ref[...]
        if a_fn is not None:
            av = a_fn(av)
        if b_span > 1:
            prod = sum(_dot(av[:, g * C:(g + 1) * C].astype(BF16), b_ref[g].astype(BF16), dims) for g in range(b_span))
        else:
            prod = _dot(av.astype(BF16), b_ref[...].astype(BF16), dims)

        def finish(r):
            if epi is not None:
                r = epi(r, *[e[...] for e in e_refs])
            o_ref[...] = r.astype(out_dtype)

        if nk == 1:
            finish(prod)
            return

        @pl.when(k == 0)
        def _():
            acc_ref[...] = prod

        @pl.when(k > 0)
        def _():
            acc_ref[...] += prod

        @pl.when(k == nk - 1)
        def _():
            finish(acc_ref[...])

    a_spec = (pl.BlockSpec((tm, tk), lambda i, j, k: (i, k)) if ca == 1
              else pl.BlockSpec((tk, tm), lambda i, j, k: (k, i)))
    if b_groups:
        if cb == 0:
            per = C // tn
            b_spec = pl.BlockSpec((None, tk, tn), lambda i, j, k: (j // per, k, j % per))
        elif b_span > 1:
            b_spec = pl.BlockSpec((b_span, tn, C), lambda i, j, k: (k, j, 0))
        else:
            per = C // tk
            b_spec = pl.BlockSpec((None, tn, tk), lambda i, j, k: (k // per, j, k % per))
    else:
        b_spec = (pl.BlockSpec((tk, tn), lambda i, j, k: (k, j)) if cb == 0
                  else pl.BlockSpec((tn, tk), lambda i, j, k: (j, k)))
    e_specs = [pl.BlockSpec((tm, tn), lambda i, j, k: (i, j)) for _ in epi_in]
    if out_groups:
        per_o = (N // out_groups) // tn
        o_spec = pl.BlockSpec((None, tm, tn), lambda i, j, k: (j // per_o, i, j % per_o))
        o_shape = jax.ShapeDtypeStruct((out_groups, M, N // out_groups), out_dtype)
    else:
        o_spec = pl.BlockSpec((tm, tn), lambda i, j, k: (i, j))
        o_shape = jax.ShapeDtypeStruct((M, N), out_dtype)
    return pl.pallas_call(
        body, grid=(M // tm, N // tn, nk), out_specs=o_spec,
        in_specs=[a_spec, b_spec] + e_specs + [pl.BlockSpec(memory_space=pl.ANY)] * len(deps),
        out_shape=o_shape, scratch_shapes=[pltpu.VMEM((tm, tn), F32)] if nk > 1 else [],
        compiler_params=_params(("parallel", "parallel", "arbitrary")), name=name)(a, b, *epi_in, *deps)


def _relu_sq(u):
    r = jnp.maximum(u, 0.0)
    return r * r


def _relu_sq_grad(acc, u):
    return acc * (2.0 * jnp.maximum(u, 0.0))


def _ln_stats(r):
    mu = jnp.mean(r, axis=-1, keepdims=True)
    xc = r - mu
    var = jnp.mean(xc * xc, axis=-1, keepdims=True)
    rstd = lax.rsqrt(var + LN_EPS)
    return xc * rstd, rstd


def _ln_bwd(dy, xhat, rstd, g):
    dxh = dy * g
    m1 = jnp.mean(dxh, axis=-1, keepdims=True)
    m2 = jnp.mean(dxh * xhat, axis=-1, keepdims=True)
    return rstd * (dxh - m1 - xhat * m2)


def _row_call(body, ins, row_ins, outs, acc_outs, name, tr=256):
    S = ins[0].shape[0]
    tr = min(tr, S)
    n_in, n_row, n_out = len(ins), len(row_ins), len(outs)

    def wrapped(*refs):
        i = pl.program_id(0)
        acc_refs = refs[n_in + n_row + n_out:]

        @pl.when(i == 0)
        def _():
            for r in acc_refs:
                r[...] = jnp.zeros_like(r)

        body(*refs)

    in_specs = [pl.BlockSpec((tr, a.shape[1]), lambda i: (i, 0)) for a in ins]
    in_specs += [pl.BlockSpec(a.shape, lambda i: (0, 0)) for a in row_ins]
    out_specs = [pl.BlockSpec((tr, s.shape[1]), lambda i: (i, 0)) for s in outs]
    out_specs += [pl.BlockSpec(s.shape, lambda i: (0, 0)) for s in acc_outs]
    return pl.pallas_call(wrapped, grid=(S // tr,), in_specs=in_specs, out_specs=out_specs,
                          out_shape=list(outs) + list(acc_outs),
                          compiler_params=_params(("arbitrary",)), name=name)(*ins, *row_ins)


def _ln1_fwd(x, mixed, g, b):
    def body(x_ref, m_ref, g_ref, b_ref, h_ref):
        xhat, _ = _ln_stats(DN_ALPHA * x_ref[...] + m_ref[...])
        h_ref[...] = xhat * g_ref[...] + b_ref[...]
    return _row_call(body, [x, mixed], [g, b], [jax.ShapeDtypeStruct(x.shape, F32)], [], "ln1_fwd")[0]


def _ln2_loss(h1, mlp, g, b, target):
    S, D = h1.shape
    sds = jax.ShapeDtypeStruct

    def body(h_ref, m_ref, t_ref, g_ref, b_ref, dr_ref, loss_ref, dg_ref, db_ref):
        xhat, rstd = _ln_stats(DN_ALPHA * h_ref[...] + m_ref[...])
        gv = g_ref[...]
        err = xhat * gv + b_ref[...] - t_ref[...]
        loss_ref[...] += jnp.sum(jnp.sum(err * err, axis=0, keepdims=True), axis=1, keepdims=True) * (0.5 / D)
        dy = err * (1.0 / D)
        dg_ref[...] += jnp.sum(dy * xhat, axis=0, keepdims=True)
        db_ref[...] += jnp.sum(dy, axis=0, keepdims=True)
        dr_ref[...] = _ln_bwd(dy, xhat, rstd, gv)

    return _row_call(body, [h1, mlp, target], [g, b], [sds((S, D), F32)],
                     [sds((1, LANE), F32), sds((1, D), F32), sds((1, D), F32)], "ln2_loss")


def _ln1_bwd(x, mixed, g, dr2, dh_mlp):
    S, D = x.shape
    sds = jax.ShapeDtypeStruct

    def body(x_ref, m_ref, dr2_ref, dh_ref, g_ref, dr_ref, dg_ref, db_ref):
        xhat, rstd = _ln_stats(DN_ALPHA * x_ref[...] + m_ref[...])
        dy = DN_ALPHA * dr2_ref[...] + dh_ref[...]
        dg_ref[...] += jnp.sum(dy * xhat, axis=0, keepdims=True)
        db_ref[...] += jnp.sum(dy, axis=0, keepdims=True)
        dr_ref[...] = _ln_bwd(dy, xhat, rstd, g_ref[...])

    return _row_call(body, [x, mixed, dr2, dh_mlp], [g], [sds((S, D), F32)],
                     [sds((1, D), F32), sds((1, D), F32)], "ln1_bwd")


def _residual_grad(acc, dr):
    return DN_ALPHA * dr + acc


def _bucket_table():
    qi = np.arange(BLK, dtype=np.int32)[:, None]
    kj = np.arange(2 * BLK, dtype=np.int32)[None, :]
    dist = qi + BLK - kj
    n = np.maximum(dist, 0)
    max_exact = N_BUCKETS // 2
    nf = np.maximum(n, 1).astype(np.float32)
    large = max_exact + (np.log(nf / np.float32(max_exact)) / np.float32(math.log(MAX_DISTANCE / max_exact))
                         * np.float32(N_BUCKETS - max_exact)).astype(np.int32)
    large = np.minimum(large, N_BUCKETS - 1)
    bucket = np.where(n < max_exact, n, large)
    return np.where((dist >= 0) & (dist < BLK), bucket, -1).astype(np.int32)


def _attn_bias(rel_bias_t):
    hq = rel_bias_t.shape[0]
    bucket = jnp.asarray(_bucket_table())

    def body(rb_ref, bk_ref, o_ref):
        h = pl.program_id(0)
        bk = bk_ref[...]
        acc = jnp.zeros((BLK, 2 * BLK), F32)
        for b in range(N_BUCKETS):
            acc = jnp.where(bk == b, rb_ref[h, b], acc)
        o_ref[...] = acc

    return pl.pallas_call(
        body, grid=(hq,),
        in_specs=[pl.BlockSpec(memory_space=pltpu.SMEM), pl.BlockSpec((BLK, 2 * BLK), lambda h: (0, 0))],
        out_specs=pl.BlockSpec((BLK, 2 * BLK), lambda h: (h, 0)),
        out_shape=jax.ShapeDtypeStruct((hq * BLK, 2 * BLK), F32),
        compiler_params=_params(("arbitrary",)), name="attn_bias")(rel_bias_t, bucket)


def _attn_probs(sc, sp, bias, sink, mask_c, mask_p):
    lc = jnp.where(mask_c, sc + bias[:, BLK:], NEG_INF)
    lp = jnp.where(mask_p, sp + bias[:, :BLK], NEG_INF)
    m = jnp.maximum(jnp.maximum(jnp.max(lc, axis=1, keepdims=True), jnp.max(lp, axis=1, keepdims=True)), sink)
    pc, pp, ps = jnp.exp(lc - m), jnp.exp(lp - m), jnp.exp(sink - m)
    inv = 1.0 / (jnp.sum(pc, axis=1, keepdims=True) + jnp.sum(pp, axis=1, keepdims=True) + ps)
    return pc, pp, ps, inv


def _attn_masks(n):
    qi = lax.broadcasted_iota(jnp.int32, (BLK, BLK), 0)
    kj = lax.broadcasted_iota(jnp.int32, (BLK, BLK), 1)
    return kj <= qi, (kj > qi) & (n > 0)


def _attn_fwd(proj, bias, sinks, hq, q_blk, k_blk, v_blk, out_width):
    S = proj.shape[0]
    hkv = hq // GQA
    wq, wk = hq * HEAD_A, hkv * HEAD_A

    def body(q_ref, k_ref, v_ref, bias_ref, sink_ref, o_ref):
        n = pl.program_id(0)
        cur = pl.multiple_of(n * BLK, BLK)
        prev = pl.multiple_of(jnp.maximum(n - 1, 0) * BLK, BLK)
        mask_c, mask_p = _attn_masks(n)
        for h4 in range(hkv):
            cs = slice(h4 * HEAD_A, (h4 + 1) * HEAD_A)
            kc, kp = k_ref[pl.ds(cur, BLK), cs].astype(BF16), k_ref[pl.ds(prev, BLK), cs].astype(BF16)
            vc, vp = v_ref[pl.ds(cur, BLK), cs].astype(BF16), v_ref[pl.ds(prev, BLK), cs].astype(BF16)
            hs_of = [slice(h * HEAD_A, (h + 1) * HEAD_A) for h in range(h4 * GQA, (h4 + 1) * GQA)]
            qs = [(q_ref[:, hs] * (HEAD_A ** -0.5)).astype(BF16) for hs in hs_of]
            scs = [_dot(q, kc, NT) for q in qs]
            sps = [_dot(q, kp, NT) for q in qs]
            pr = [_attn_probs(scs[g], sps[g], bias_ref[(h4 * GQA + g) * BLK:(h4 * GQA + g + 1) * BLK, :],
                              sink_ref[h4 * GQA + g], mask_c, mask_p) for g in range(GQA)]
            oc = [_dot(p[0].astype(BF16), vc, NN) for p in pr]
            op = [_dot(p[1].astype(BF16), vp, NN) for p in pr]
            for g, hs in enumerate(hs_of):
                o_ref[:, hs] = (oc[g] + op[g]) * pr[g][3]

    return pl.pallas_call(
        body, grid=(S // BLK,),
        in_specs=[pl.BlockSpec((BLK, wq), lambda n: (n, q_blk)), pl.BlockSpec((S, wk), lambda n: (0, k_blk)),
                  pl.BlockSpec((S, wk), lambda n: (0, v_blk)), pl.BlockSpec((hq * BLK, 2 * BLK), lambda n: (0, 0)),
                  pl.BlockSpec(memory_space=pltpu.SMEM)],
        out_specs=pl.BlockSpec((BLK, wq), lambda n: (n, 0)),
        out_shape=jax.ShapeDtypeStruct((S, out_width), F32),
        compiler_params=_params(("arbitrary",)), name="attn_fwd")(proj, proj, proj, bias, sinks)


def _attn_bwd(proj, bias, sinks, out, dmix, hq, q_blk, k_blk, v_blk):
    S = proj.shape[0]
    hkv = hq // GQA
    wq, wk = hq * HEAD_A, hkv * HEAD_A
    sds = jax.ShapeDtypeStruct

    def body(q_ref, k_ref, v_ref, bias_ref, sink_ref, o_ref, do_ref, dq_ref, dk_ref, dv_ref, dbias_ref, dsink_ref):
        n = pl.program_id(0)

        @pl.when(n == 0)
        def _():
            dk_ref[...] = jnp.zeros_like(dk_ref)
            dv_ref[...] = jnp.zeros_like(dv_ref)
            dbias_ref[...] = jnp.zeros_like(dbias_ref)
            dsink_ref[...] = jnp.zeros_like(dsink_ref)

        cur = pl.multiple_of(n * BLK, BLK)
        prev = pl.multiple_of(jnp.maximum(n - 1, 0) * BLK, BLK)
        mask_c, mask_p = _attn_masks(n)
        for h4 in range(hkv):
            cs = slice(h4 * HEAD_A, (h4 + 1) * HEAD_A)
            kc, kp = k_ref[pl.ds(cur, BLK), cs].astype(BF16), k_ref[pl.ds(prev, BLK), cs].astype(BF16)
            vc, vp = v_ref[pl.ds(cur, BLK), cs].astype(BF16), v_ref[pl.ds(prev, BLK), cs].astype(BF16)
            heads = list(range(h4 * GQA, (h4 + 1) * GQA))
            hs_of = [slice(h * HEAD_A, (h + 1) * HEAD_A) for h in heads]
            rows_of = [slice(h * BLK, (h + 1) * BLK) for h in heads]
            G = range(GQA)
            qs = [(q_ref[:, hs] * (HEAD_A ** -0.5)).astype(BF16) for hs in hs_of]
            dos = [do_ref[:, hs] for hs in hs_of]
            dobs = [d.astype(BF16) for d in dos]
            scs = [_dot(q, kc, NT) for q in qs]
            sps = [_dot(q, kp, NT) for q in qs]
            dpc = [_dot(d, vc, NT) for d in dobs]
            dpp = [_dot(d, vp, NT) for d in dobs]
            pcs, pps, dscs, dsps = [], [], [], []
            for g in G:
                pc, pp, ps, inv = _attn_probs(scs[g], sps[g], bias_ref[rows_of[g], :], sink_ref[heads[g]], mask_c, mask_p)
                pc, pp, ps = pc * inv, pp * inv, ps * inv
                delta = jnp.sum(dos[g] * o_ref[:, hs_of[g]], axis=1, keepdims=True)
                dsc, dsp = pc * (dpc[g] - delta), pp * (dpp[g] - delta)
                dsink_ref[heads[g]:heads[g] + 1, :] += jnp.broadcast_to(jnp.sum(-ps * delta, axis=0, keepdims=True), (1, LANE))
                dbias_ref[rows_of[g], BLK:] += dsc
                dbias_ref[rows_of[g], :BLK] += dsp
                pcs.append(pc.astype(BF16))
                pps.append(pp.astype(BF16))
                dscs.append(dsc.astype(BF16))
                dsps.append(dsp.astype(BF16))
            dq1 = [_dot(dscs[g], kc, NN) for g in G]
            dq2 = [_dot(dsps[g], kp, NN) for g in G]
            dkc = [_dot(dscs[g], qs[g], TN) for g in G]
            dkp = [_dot(dsps[g], qs[g], TN) for g in G]
            dvc = [_dot(pcs[g], dobs[g], TN) for g in G]
            dvp = [_dot(pps[g], dobs[g], TN) for g in G]
            for g in G:
                dq_ref[:, hs_of[g]] = (dq1[g] + dq2[g]) * (HEAD_A ** -0.5)
            dk_ref[pl.ds(cur, BLK), cs] += sum(dkc[1:], dkc[0])
            dk_ref[pl.ds(prev, BLK), cs] += sum(dkp[1:], dkp[0])
            dv_ref[pl.ds(cur, BLK), cs] += sum(dvc[1:], dvc[0])
            dv_ref[pl.ds(prev, BLK), cs] += sum(dvp[1:], dvp[0])

    return pl.pallas_call(
        body, grid=(S // BLK,),
        in_specs=[pl.BlockSpec((BLK, wq), lambda n: (n, q_blk)), pl.BlockSpec((S, wk), lambda n: (0, k_blk)),
                  pl.BlockSpec((S, wk), lambda n: (0, v_blk)), pl.BlockSpec((hq * BLK, 2 * BLK), lambda n: (0, 0)),
                  pl.BlockSpec(memory_space=pltpu.SMEM),
                  pl.BlockSpec((BLK, wq), lambda n: (n, 0)), pl.BlockSpec((BLK, wq), lambda n: (n, 0))],
        out_specs=[pl.BlockSpec((BLK, wq), lambda n: (n, 0)), pl.BlockSpec((S, wk), lambda n: (0, 0)),
                   pl.BlockSpec((S, wk), lambda n: (0, 0)), pl.BlockSpec((hq * BLK, 2 * BLK), lambda n: (0, 0)),
                   pl.BlockSpec((hq, LANE), lambda n: (0, 0))],
        out_shape=[sds((S, wq), F32), sds((S, wk), F32), sds((S, wk), F32), sds((hq * BLK, 2 * BLK), F32),
                   sds((hq, LANE), F32)],
        compiler_params=_params(("arbitrary",)), name="attn_bwd")(proj, proj, proj, bias, sinks, out, dmix)


def _rel_bias_grad(dbias, hq):
    bucket = jnp.asarray(_bucket_table())

    def body(d_ref, bk_ref, o_ref):
        d = d_ref[...]
        bk = bk_ref[...]
        rows = [jnp.sum(jnp.where(bk == b, d, 0.0), axis=0, keepdims=True) for b in range(N_BUCKETS)]
        tot = jnp.sum(jnp.concatenate(rows, axis=0), axis=1, keepdims=True)
        o_ref[...] = jnp.broadcast_to(tot, (N_BUCKETS, LANE))

    return pl.pallas_call(
        body, grid=(hq,),
        in_specs=[pl.BlockSpec((BLK, 2 * BLK), lambda h: (h, 0)), pl.BlockSpec((BLK, 2 * BLK), lambda h: (0, 0))],
        out_specs=pl.BlockSpec((None, N_BUCKETS, LANE), lambda h: (h, 0, 0)),
        out_shape=jax.ShapeDtypeStruct((hq, N_BUCKETS, LANE), F32),
        compiler_params=_params(("arbitrary",)), name="rel_bias_grad")(dbias, bucket)


def _sigmoid(x):
    return 0.5 * jnp.tanh(0.5 * x) + 0.5


def _shift_rows(x, s, row):
    n = x.shape[0]
    if s > 0:
        return jnp.where(row >= s, pltpu.roll(x, s, 0), 0.0)
    return jnp.where(row < n + s, pltpu.roll(x, n + s, 0), 0.0)


def _conv_silu_norm(xv, w, j, nh):
    row = lax.broadcasted_iota(jnp.int32, xv.shape, 0)
    xs = [xv] + [_shift_rows(xv, s, row) for s in range(1, CONV_W)]
    c = w[CONV_W - 1:CONV_W, :] * xv
    for s in range(1, CONV_W):
        c = c + w[CONV_W - 1 - s:CONV_W - s, :] * xs[s]
    sg = _sigmoid(c)
    a = c * sg
    r = lax.rsqrt(jnp.sum(a * a, axis=1, keepdims=True) + RMS_EPS)
    scale = jnp.where(j < nh, HEAD_D ** -0.5, 1.0)
    is_norm = j < 2 * nh
    y = jnp.where(is_norm, a * (r * scale), a)
    return c, sg, a, r, scale, is_norm, xs, row, y


def _gdn_prep_fwd(proj, conv_w, nh, blk0):
    S = proj.shape[0]

    def body(x_ref, w_ref, o_ref):
        j = pl.program_id(0)
        o_ref[...] = _conv_silu_norm(x_ref[...], w_ref[...], j, nh)[-1]

    return pl.pallas_call(
        body, grid=(3 * nh,),
        in_specs=[pl.BlockSpec((S, HEAD_D), lambda j: (0, blk0 + j)), pl.BlockSpec((CONV_W, HEAD_D), lambda j: (0, j))],
        out_specs=pl.BlockSpec((S, HEAD_D), lambda j: (0, 3 * (j % nh) + j // nh)),
        out_shape=jax.ShapeDtypeStruct((S, 3 * nh * HEAD_D), F32),
        compiler_params=_params(("parallel",)), name="gdn_prep_fwd")(proj, conv_w)


def _gdn_prep_bwd(proj, conv_w, dqkv, nh, blk0):
    S = proj.shape[0]
    sds = jax.ShapeDtypeStruct

    def body(x_ref, w_ref, dy_ref, dx_ref, dw_ref):
        j = pl.program_id(0)
        xv, w = x_ref[...], w_ref[...]
        c, sg, a, r, scale, is_norm, xs, row, _ = _conv_silu_norm(xv, w, j, nh)
        dy = dy_ref[...]
        rs = r * scale
        da_n = rs * dy - a * (r * r * rs) * jnp.sum(dy * a, axis=1, keepdims=True)
        da = jnp.where(is_norm, da_n, dy)
        dc = da * (sg * (1.0 + c * (1.0 - sg)))
        dx = w[CONV_W - 1:CONV_W, :] * dc
        dws = [jnp.sum(dc * xv, axis=0, keepdims=True)]
        for s in range(1, CONV_W):
            dx = dx + w[CONV_W - 1 - s:CONV_W - s, :] * _shift_rows(dc, -s, row)
            dws.insert(0, jnp.sum(dc * xs[s], axis=0, keepdims=True))
        dx_ref[...] = dx
        dw_ref[...] = jnp.concatenate(dws, axis=0)

    return pl.pallas_call(
        body, grid=(3 * nh,),
        in_specs=[pl.BlockSpec((S, HEAD_D), lambda j: (0, blk0 + j)), pl.BlockSpec((CONV_W, HEAD_D), lambda j: (0, j)),
                  pl.BlockSpec((S, HEAD_D), lambda j: (0, 3 * (j % nh) + j // nh))],
        out_specs=[pl.BlockSpec((S, HEAD_D), lambda j: (0, j)), pl.BlockSpec((CONV_W, HEAD_D), lambda j: (0, j))],
        out_shape=[sds((S, 3 * nh * HEAD_D), F32), sds((CONV_W, 3 * nh * HEAD_D), F32)],
        compiler_params=_params(("parallel",)), name="gdn_prep_bwd")(proj, conv_w, dqkv)


def _softplus(x):
    return jnp.maximum(x, 0.0) + jnp.log(1.0 + jnp.exp(-jnp.abs(x)))


def _gates_fwd(ab, al, dt, nh):
    S = ab.shape[0]

    def body(ab_ref, al_ref, dt_ref, o_ref):
        v = ab_ref[...]
        lane = lax.broadcasted_iota(jnp.int32, v.shape, 1)
        g = -jnp.exp(al_ref[...]) * _softplus(v + dt_ref[...])
        o_ref[...] = jnp.where(lane < nh, g, jnp.where(lane < 2 * nh, _sigmoid(v), 0.0))

    row = pl.BlockSpec((1, LANE), lambda i: (0, 0))
    full = pl.BlockSpec((S, LANE), lambda i: (0, 0))
    return pl.pallas_call(body, grid=(1,), in_specs=[full, row, row], out_specs=full,
                          out_shape=jax.ShapeDtypeStruct((S, LANE), F32),
                          compiler_params=_params(("arbitrary",)), name="gates_fwd")(ab, al, dt)


def _gates_bwd(ab, al, dt, dgb, nh):
    S = ab.shape[0]
    sds = jax.ShapeDtypeStruct

    def body(ab_ref, al_ref, dt_ref, d_ref, dab_ref, dal_ref, ddt_ref):
        v, d = ab_ref[...], d_ref[...]
        lane = lax.broadcasted_iota(jnp.int32, v.shape, 1)
        is_a = lane < nh
        z = v + dt_ref[...]
        dsp = jnp.where(is_a, d * (-jnp.exp(al_ref[...])), 0.0)
        dz = dsp * _sigmoid(z)
        beta = _sigmoid(v)
        dab_ref[...] = jnp.where(is_a, dz, jnp.where(lane < 2 * nh, d * beta * (1.0 - beta), 0.0))
        dal_ref[...] = jnp.sum(dsp * _softplus(z), axis=0, keepdims=True)
        ddt_ref[...] = jnp.sum(dz, axis=0, keepdims=True)

    row = pl.BlockSpec((1, LANE), lambda i: (0, 0))
    full = pl.BlockSpec((S, LANE), lambda i: (0, 0))
    return pl.pallas_call(body, grid=(1,), in_specs=[full, row, row, full], out_specs=[full, row, row],
                          out_shape=[sds((S, LANE), F32), sds((1, LANE), F32), sds((1, LANE), F32)],
                          compiler_params=_params(("arbitrary",)), name="gates_bwd")(ab, al, dt, dgb)


def _col_of(tile, h):
    lane = lax.broadcasted_iota(jnp.int32, tile.shape, 1)
    return jnp.sum(jnp.where(lane == h, tile, 0.0), axis=1, keepdims=True)


def _to_row(col, eye):
    return jnp.sum(jnp.where(eye, col, 0.0), axis=0, keepdims=True)


def _to_col(row, eye):
    return jnp.sum(jnp.where(eye, row, 0.0), axis=1, keepdims=True)


def _split(a):
    hi = a.astype(BF16)
    return hi, (a - hi.astype(F32)).astype(BF16)


def _gdot(a, b, dims):
    ah, al = _split(a)
    bh, bl = _split(b)
    return _dot(ah, bh, dims) + (_dot(ah, bl, dims) + _dot(al, bh, dims))


def _bdot(a, b, dims):
    return _dot(a.astype(BF16), b.astype(BF16), dims)


def _chunks_local(qs, ks, vs, gcols, bcols, Ts=None):
    C = CHUNK
    row = lax.broadcasted_iota(jnp.int32, (C, C), 0)
    col = lax.broadcasted_iota(jnp.int32, (C, C), 1)
    tril, strict, eye = col <= row, col < row, col == row
    outs = []
    for k, gcol, bcol in zip(ks, gcols, bcols):
        grow = _to_row(gcol, eye)
        G_row = jnp.sum(jnp.where(row <= col, gcol, 0.0), axis=0, keepdims=True)
        G_col = jnp.sum(jnp.where(tril, grow, 0.0), axis=1, keepdims=True)
        G_last = G_col[C - 1:C, :]
        outs.append(dict(strict=strict, eye=eye, row=row, col=col, decay=jnp.exp(jnp.where(tril, G_col - G_row, NEG_INF)),
                         eG=jnp.exp(G_col), eGr=jnp.exp(G_last - G_col), gl=jnp.exp(G_last), kb=k * bcol))
    Ms = [_gdot(o["kb"], k, NT) for o, k in zip(outs, ks)]
    Ns = [_gdot(q, k, NT) for q, k in zip(qs, ks)]
    for o, q, k, M, N in zip(outs, qs, ks, Ms, Ns):
        o.update(A=jnp.where(strict, M * o["decay"], 0.0), attn=N * o["decay"], rhs_k=o["kb"] * o["eG"],
                 q_dec=q * o["eG"], k_dec=k * o["eGr"])
    if Ts is None:
        Ts = [jnp.where(eye, 1.0, 0.0) - o["A"] for o in outs]
        Ps = [o["A"] for o in outs]
        for _ in range(int(math.log2(C)) - 1):
            Ps = [_bdot(P, P, NN) for P in Ps]
            Ts = [T + _bdot(T, P, NN) for T, P in zip(Ts, Ps)]
        us = [_bdot(T, v * bcol, NN) for T, v, bcol in zip(Ts, vs, bcols)]
        ws = [_bdot(T, o["rhs_k"], NN) for T, o in zip(Ts, outs)]
        for o, T, u, w in zip(outs, Ts, us, ws):
            o.update(T=T, u=u, w=w)
    return outs


GDN_ROWS = 256
GDN_LOCAL_ROWS = 1024
WQK = 3 * CHUNK


def _gdn_local_fwd(qkv, gb, nh):
    S = qkv.shape[0]
    nc = S // CHUNK
    rb = min(GDN_LOCAL_ROWS, S)
    cpb = rb // CHUNK
    sds = jax.ShapeDtypeStruct

    def body(q_ref, k_ref, v_ref, gb_ref, u_ref, wqk_ref, attn_ref, t_ref):
        h = pl.program_id(0)
        rows_of = [slice(ci * CHUNK, (ci + 1) * CHUNK) for ci in range(cpb)]
        gbts = [gb_ref[rows, :] for rows in rows_of]
        Ls = _chunks_local([q_ref[rows, :] for rows in rows_of], [k_ref[rows, :] for rows in rows_of],
                           [v_ref[rows, :] for rows in rows_of], [_col_of(t, h) for t in gbts],
                           [_col_of(t, nh + h) for t in gbts])
        for ci, (rows, L) in enumerate(zip(rows_of, Ls)):
            u_ref[rows, :] = L["u"]
            base = ci * WQK
            wqk_ref[base:base + CHUNK, :] = L["w"]
            wqk_ref[base + CHUNK:base + 2 * CHUNK, :] = L["q_dec"]
            wqk_ref[base + 2 * CHUNK:base + WQK, :] = L["k_dec"]
            attn_ref[ci] = L["attn"]
            t_ref[ci] = L["T"]

    cc = pl.BlockSpec((None, cpb, CHUNK, CHUNK), lambda h, i: (h, i, 0, 0))
    return pl.pallas_call(
        body, grid=(nh, S // rb),
        in_specs=[pl.BlockSpec((rb, HEAD_D), lambda h, i: (i, 3 * h)), pl.BlockSpec((rb, HEAD_D), lambda h, i: (i, 3 * h + 1)),
                  pl.BlockSpec((rb, HEAD_D), lambda h, i: (i, 3 * h + 2)), pl.BlockSpec((rb, LANE), lambda h, i: (i, 0))],
        out_specs=[pl.BlockSpec((rb, HEAD_D), lambda h, i: (i, h)),
                   pl.BlockSpec((None, 3 * rb, HEAD_D), lambda h, i: (h, i, 0)), cc, cc],
        out_shape=[sds((S, nh * HEAD_D), F32), sds((nh, 3 * S, HEAD_D), F32), sds((nh, nc, CHUNK, CHUNK), F32),
                   sds((nh, nc, CHUNK, CHUNK), F32)],
        compiler_params=_params(("parallel", "parallel")), name="gdn_local_fwd")(qkv, qkv, qkv, gb)


def _gdn_scan_fwd(u, wqk, attn, gb, nh, dep):
    S = u.shape[0]
    nc = S // CHUNK
    rb = min(GDN_ROWS, S)
    cpb = rb // CHUNK
    sds = jax.ShapeDtypeStruct

    def body(u_ref, wqk_ref, attn_ref, gb_ref, dep_ref, o_ref, vn_ref, st_ref, s_ref):
        @pl.when(pl.program_id(0) == 0)
        def _():
            s_ref[...] = jnp.zeros_like(s_ref)

        for ci in range(cpb):
            rows = slice(ci * CHUNK, (ci + 1) * CHUNK)
            glv = jnp.exp(jnp.sum(gb_ref[rows, :], axis=0, keepdims=True))
            base = ci * WQK
            heads = range(nh)
            cols = [slice(h * HEAD_D, (h + 1) * HEAD_D) for h in heads]
            states = [s_ref[h] for h in heads]
            rs = [_gdot(wqk_ref[h, base:base + 2 * CHUNK, :], states[h], NN) for h in heads]
            vbs = [u_ref[rows, cols[h]] - rs[h][:CHUNK] for h in heads]
            os_ = [_gdot(attn_ref[h, ci], vbs[h], NN) for h in heads]
            ks_ = [_gdot(wqk_ref[h, base + 2 * CHUNK:base + WQK, :], vbs[h], TN) for h in heads]
            for h in heads:
                st_ref[h, ci] = states[h]
                o_ref[rows, cols[h]] = rs[h][CHUNK:] + os_[h]
                vn_ref[rows, cols[h]] = vbs[h]
                s_ref[h] = states[h] * glv[:, h:h + 1] + ks_[h]

    return pl.pallas_call(
        body, grid=(S // rb,),
        in_specs=[pl.BlockSpec((rb, nh * HEAD_D), lambda i: (i, 0)), pl.BlockSpec((nh, 3 * rb, HEAD_D), lambda i: (0, i, 0)),
                  pl.BlockSpec((nh, cpb, CHUNK, CHUNK), lambda i: (0, i, 0, 0)), pl.BlockSpec((rb, LANE), lambda i: (i, 0)),
                  pl.BlockSpec(memory_space=pl.ANY)],
        out_specs=[pl.BlockSpec((rb, nh * HEAD_D), lambda i: (i, 0)), pl.BlockSpec((rb, nh * HEAD_D), lambda i: (i, 0)),
                   pl.BlockSpec((nh, cpb, HEAD_D, HEAD_D), lambda i: (0, i, 0, 0))],
        out_shape=[sds((S, nh * HEAD_D), F32), sds((S, nh * HEAD_D), F32), sds((nh, nc, HEAD_D, HEAD_D), F32)],
        scratch_shapes=[pltpu.VMEM((nh, HEAD_D, HEAD_D), F32)],
        compiler_params=_params(("arbitrary",)), name="gdn_scan_fwd")(u, wqk, attn, gb, dep)


def _gdn_scan_bwd(wqk, attn, gb, states, vn, do, nh, dep):
    S = vn.shape[0]
    nc = S // CHUNK
    rb = min(GDN_ROWS, S)
    cpb = rb // CHUNK
    last = S // rb - 1
    sds = jax.ShapeDtypeStruct

    def body(wqk_ref, attn_ref, gb_ref, st_ref, vn_ref, do_ref, dep_ref, dvn_ref, dw_ref, dqd_ref, dkd_ref, da_ref, dgl_ref,
             ds_ref):
        @pl.when(pl.program_id(0) == 0)
        def _():
            ds_ref[...] = jnp.zeros_like(ds_ref)

        row = lax.broadcasted_iota(jnp.int32, (CHUNK, CHUNK), 0)
        col = lax.broadcasted_iota(jnp.int32, (CHUNK, CHUNK), 1)
        for ci in reversed(range(cpb)):
            rows = slice(ci * CHUNK, (ci + 1) * CHUNK)
            glv = jnp.exp(jnp.sum(gb_ref[rows, :], axis=0, keepdims=True))
            base = ci * WQK
            heads = range(nh)
            cols = [slice(h * HEAD_D, (h + 1) * HEAD_D) for h in heads]
            states = [st_ref[h, ci] for h in heads]
            dSs = [ds_ref[h] for h in heads]
            vbs = [vn_ref[rows, cols[h]] for h in heads]
            dobs = [do_ref[rows, cols[h]] for h in heads]
            dv1 = [_gdot(attn_ref[h, ci], dobs[h], TN) for h in heads]
            dv2 = [_gdot(wqk_ref[h, base + 2 * CHUNK:base + WQK, :], dSs[h], NN) for h in heads]
            das = [_gdot(dobs[h], vbs[h], NT) for h in heads]
            dkds = [_gdot(vbs[h], dSs[h], NT) for h in heads]
            dvbs = [dv1[h] + dv2[h] for h in heads]
            xs = [_gdot(jnp.concatenate([dobs[h], dvbs[h]], axis=0), states[h], NT) for h in heads]
            dss = [_gdot(wqk_ref[h, base:base + 2 * CHUNK, :], jnp.concatenate([-dvbs[h], dobs[h]], axis=0), TN)
                   for h in heads]
            for h in heads:
                dqd_ref[rows, cols[h]] = xs[h][:CHUNK]
                dw_ref[rows, cols[h]] = -xs[h][CHUNK:]
                dvn_ref[rows, cols[h]] = dvbs[h]
                da_ref[h, ci] = jnp.where(col <= row, das[h], 0.0)
                dkd_ref[rows, cols[h]] = dkds[h]
                gl = glv[:, h:h + 1]
                dgl = jnp.sum(jnp.sum(states[h] * dSs[h], axis=0, keepdims=True), axis=1, keepdims=True)
                dgl_ref[h, ci] = jnp.broadcast_to(dgl * gl, (1, LANE))
                ds_ref[h] = dSs[h] * gl + dss[h]

    rv = lambda i: last - i
    wide = pl.BlockSpec((rb, nh * HEAD_D), lambda i: (rv(i), 0))
    return pl.pallas_call(
        body, grid=(S // rb,),
        in_specs=[pl.BlockSpec((nh, 3 * rb, HEAD_D), lambda i: (0, rv(i), 0)),
                  pl.BlockSpec((nh, cpb, CHUNK, CHUNK), lambda i: (0, rv(i), 0, 0)),
                  pl.BlockSpec((rb, LANE), lambda i: (rv(i), 0)),
                  pl.BlockSpec((nh, cpb, HEAD_D, HEAD_D), lambda i: (0, rv(i), 0, 0)), wide, wide,
                  pl.BlockSpec(memory_space=pl.ANY)],
        out_specs=[wide, wide, wide, wide, pl.BlockSpec((nh, cpb, CHUNK, CHUNK), lambda i: (0, rv(i), 0, 0)),
                   pl.BlockSpec((nh, cpb, 1, LANE), lambda i: (0, rv(i), 0, 0))],
        out_shape=[sds((S, nh * HEAD_D), F32), sds((S, nh * HEAD_D), F32), sds((S, nh * HEAD_D), F32),
                   sds((S, nh * HEAD_D), F32), sds((nh, nc, CHUNK, CHUNK), F32), sds((nh, nc, 1, LANE), F32)],
        scratch_shapes=[pltpu.VMEM((nh, HEAD_D, HEAD_D), F32)],
        compiler_params=_params(("arbitrary",)), name="gdn_scan_bwd")(wqk, attn, gb, states, vn, do, dep)


def _gdn_local_bwd(qkv, gb, T, u, wqk, dvn, dw, dqd, dkd, dattn, dgl, nh):
    S = qkv.shape[0]
    rb = min(GDN_LOCAL_ROWS, S)
    cpb = rb // CHUNK
    sds = jax.ShapeDtypeStruct

    def body(q_ref, k_ref, v_ref, gb_ref, t_ref, u_ref, wqk_ref, dvn_ref, dw_ref, dqd_ref, dkd_ref, da_ref, dgl_ref,
             dqkv_ref, dg_ref, db_ref):
        h = pl.program_id(0)
        n = range(cpb)
        rows_of = [slice(ci * CHUNK, (ci + 1) * CHUNK) for ci in n]
        qs, ks, vs = ([r[rows, :] for rows in rows_of] for r in (q_ref, k_ref, v_ref))
        gbts = [gb_ref[rows, :] for rows in rows_of]
        bcols = [_col_of(t, nh + h) for t in gbts]
        Ts = [t_ref[ci] for ci in n]
        Ls = _chunks_local(qs, ks, vs, [_col_of(t, h) for t in gbts], bcols, Ts=Ts)
        drvs = [_gdot(Ts[ci], dvn_ref[rows_of[ci], :], TN) for ci in n]
        drks = [_gdot(Ts[ci], dw_ref[rows_of[ci], :], TN) for ci in n]
        dAs = [jnp.where(Ls[ci]["strict"], -(_gdot(drvs[ci], u_ref[rows_of[ci], :], NT)
                                             + _gdot(drks[ci], wqk_ref[ci * WQK:ci * WQK + CHUNK, :], NT)), 0.0) for ci in n]
        dMs = [dAs[ci] * Ls[ci]["decay"] for ci in n]
        dNs = [da_ref[ci] * Ls[ci]["decay"] for ci in n]
        dkbs = [_gdot(dMs[ci], ks[ci], NN) for ci in n]
        dq1 = [_gdot(dNs[ci], ks[ci], NN) for ci in n]
        dk1 = [_gdot(dMs[ci], Ls[ci]["kb"], TN) for ci in n]
        dk2 = [_gdot(dNs[ci], qs[ci], TN) for ci in n]
        for ci in n:
            rows, L, q, k, v, bcol = rows_of[ci], Ls[ci], qs[ci], ks[ci], vs[ci], bcols[ci]
            eye, eG, eGr = L["eye"], L["eG"], L["eGr"]
            drv, drk, dkb = drvs[ci], drks[ci], dkbs[ci]
            dq_dec, dk_dec, dattn_c = dqd_ref[rows, :], dkd_ref[rows, :], da_ref[ci]
            dqkv_ref[rows, :HEAD_D] = dq1[ci] + dq_dec * eG
            dqkv_ref[rows, HEAD_D:2 * HEAD_D] = drk * (bcol * eG) + dk1[ci] + dkb * bcol + dk2[ci] + dk_dec * eGr
            dqkv_ref[rows, 2 * HEAD_D:] = drv * bcol
            db_ref[rows, :] = (jnp.sum(drv * v, axis=1, keepdims=True) + jnp.sum(drk * k, axis=1, keepdims=True) * eG
                               + jnp.sum(dkb * k, axis=1, keepdims=True))
            E = dAs[ci] * L["A"] + dattn_c * L["attn"]
            kd = jnp.sum(dk_dec * L["k_dec"], axis=1, keepdims=True)
            dG = (jnp.sum(dq_dec * L["q_dec"], axis=1, keepdims=True) - kd
                  + jnp.sum(drk * L["rhs_k"], axis=1, keepdims=True)
                  + jnp.sum(E, axis=1, keepdims=True) - _to_col(jnp.sum(E, axis=0, keepdims=True), eye))
            d_last = jnp.sum(kd, axis=0, keepdims=True) + dgl_ref[ci][:, :1]
            dG = dG + jnp.where(L["row"][:, :1] == CHUNK - 1, d_last, 0.0)
            dg_ref[rows, :] = jnp.sum(jnp.where(L["col"] >= L["row"], _to_row(dG, eye), 0.0), axis=1, keepdims=True)

    hd = pl.BlockSpec((rb, HEAD_D), lambda h, i: (i, h))
    cc = pl.BlockSpec((None, cpb, CHUNK, CHUNK), lambda h, i: (h, i, 0, 0))
    col1 = pl.BlockSpec((None, rb, 1), lambda h, i: (h, i, 0))
    return pl.pallas_call(
        body, grid=(nh, S // rb),
        in_specs=[pl.BlockSpec((rb, HEAD_D), lambda h, i: (i, 3 * h)), pl.BlockSpec((rb, HEAD_D), lambda h, i: (i, 3 * h + 1)),
                  pl.BlockSpec((rb, HEAD_D), lambda h, i: (i, 3 * h + 2)), pl.BlockSpec((rb, LANE), lambda h, i: (i, 0)),
                  cc, hd, pl.BlockSpec((None, 3 * rb, HEAD_D), lambda h, i: (h, i, 0)), hd, hd, hd, hd, cc,
                  pl.BlockSpec((None, cpb, 1, LANE), lambda h, i: (h, i, 0, 0))],
        out_specs=[pl.BlockSpec((rb, 3 * HEAD_D), lambda h, i: (i, h)), col1, col1],
        out_shape=[sds((S, 3 * nh * HEAD_D), F32)] + [sds((nh, S, 1), F32)] * 2,
        compiler_params=_params(("parallel", "parallel")), name="gdn_local_bwd")(
            qkv, qkv, qkv, gb, T, u, wqk, dvn, dw, dqd, dkd, dattn, dgl)


def _gated_norm_fwd(o, proj, norm_w, nh, z_blk0, mix, m_blk0):
    S = o.shape[0]

    def body(o_ref, z_ref, w_ref, mix_ref, y_ref):
        ov, z = o_ref[...], z_ref[...]
        r = lax.rsqrt(jnp.mean(ov * ov, axis=1, keepdims=True) + RMS_EPS)
        y_ref[...] = ov * r * w_ref[...] * (z * _sigmoid(z))

    return pl.pallas_call(
        body, grid=(nh,),
        in_specs=[pl.BlockSpec((S, HEAD_D), lambda h: (0, h)), pl.BlockSpec((S, HEAD_D), lambda h: (0, z_blk0 + h)),
                  pl.BlockSpec((1, HEAD_D), lambda h: (0, 0)), pl.BlockSpec(memory_space=pl.ANY)],
        out_specs=pl.BlockSpec((S, HEAD_D), lambda h: (0, m_blk0 + h)),
        out_shape=jax.ShapeDtypeStruct(mix.shape, F32), input_output_aliases={3: 0},
        compiler_params=_params(("parallel",)), name="gated_norm_fwd")(o, proj, norm_w, mix)


def _gated_norm_bwd(o, proj, norm_w, dmix, nh, z_blk0, d_blk0):
    S = o.shape[0]
    sds = jax.ShapeDtypeStruct

    def body(o_ref, z_ref, w_ref, dy_ref, do_ref, dz_ref, dw_ref):
        ov, z, w, dy = o_ref[...], z_ref[...], w_ref[...], dy_ref[...]
        r = lax.rsqrt(jnp.mean(ov * ov, axis=1, keepdims=True) + RMS_EPS)
        oh = ov * r
        sg = _sigmoid(z)
        dz_ref[...] = dy * (oh * w) * (sg * (1.0 + z * (1.0 - sg)))
        don = dy * (z * sg)
        @pl.when(pl.program_id(0) == 0)
        def _():
            dw_ref[...] = jnp.zeros_like(dw_ref)

        dw_ref[...] += jnp.sum(don * oh, axis=0, keepdims=True)
        doh = don * w
        do_ref[...] = r * (doh - oh * jnp.mean(doh * oh, axis=1, keepdims=True))

    return pl.pallas_call(
        body, grid=(nh,),
        in_specs=[pl.BlockSpec((S, HEAD_D), lambda h: (0, h)), pl.BlockSpec((S, HEAD_D), lambda h: (0, z_blk0 + h)),
                  pl.BlockSpec((1, HEAD_D), lambda h: (0, 0)), pl.BlockSpec((S, HEAD_D), lambda h: (0, d_blk0 + h))],
        out_specs=[pl.BlockSpec((S, HEAD_D), lambda h: (0, h)), pl.BlockSpec((S, HEAD_D), lambda h: (0, h)),
                   pl.BlockSpec((1, HEAD_D), lambda h: (0, 0))],
        out_shape=[sds((S, nh * HEAD_D), F32), sds((S, nh * HEAD_D), F32), sds((1, HEAD_D), F32)],
        compiler_params=_params(("arbitrary",)), name="gated_norm_bwd")(o, proj, norm_w, dmix)


def _adamw_math(w, g, m, v):
    m = ADAM_B1 * m + (1.0 - ADAM_B1) * g
    v = ADAM_B2 * v + (1.0 - ADAM_B2) * (g * g)
    m_hat = m / (1.0 - ADAM_B1 ** ADAM_STEP)
    v_hat = v / (1.0 - ADAM_B2 ** ADAM_STEP)
    delta = -ADAM_LR * (m_hat / (jnp.sqrt(v_hat) + ADAM_EPS) + ADAM_WD * w)
    return delta, m, v


def _slab_tiles(R, C, rows=256, cols=256):
    if R % rows == 0:
        return (rows, C), R // rows, lambda i: (i, 0)
    tc = _tile(C, cols)
    return (R, tc), C // tc, lambda i: (0, i)


def _adamw_big(parts, terms, chip, w, m, v, name):
    R, C = w.shape
    blk, steps, at = _slab_tiles(R, C)
    sds = jax.ShapeDtypeStruct

    def body(q_ref, p_ref, t_ref, w_ref, m_ref, v_ref, g_ref, d_ref, nm_ref, nv_ref):
        g = ((p_ref[...].astype(F32) + t_ref[0].astype(F32)) + t_ref[1].astype(F32)) + t_ref[2].astype(F32)
        g_ref[...] = g
        d_ref[...], nm_ref[...], nv_ref[...] = _adamw_math(w_ref[...], g, m_ref[...], v_ref[...])

    spec = pl.BlockSpec(blk, lambda i, q_ref: at(i))
    grid_spec = pltpu.PrefetchScalarGridSpec(
        num_scalar_prefetch=1, grid=(steps,),
        in_specs=[pl.BlockSpec((None,) + blk, lambda i, q_ref: (q_ref[0],) + at(i)),
                  pl.BlockSpec((3,) + blk, lambda i, q_ref: (0,) + at(i)), spec, spec, spec],
        out_specs=[spec] * 4)
    return pl.pallas_call(body, grid_spec=grid_spec, out_shape=[sds((R, C), F32)] * 4,
                          compiler_params=_params(("parallel",)), name=name)(chip, parts, terms, w, m, v)


def _adamw_small(ws, gs, ms, vs):
    n = len(ws)

    def body(*refs):
        for i in range(n):
            w, g, m, v = (refs[k * n + i][...] for k in range(4))
            d, nm, nv = _adamw_math(w, g, m, v)
            refs[4 * n + i][...] = d
            refs[5 * n + i][...] = nm
            refs[6 * n + i][...] = nv

    shapes = [jax.ShapeDtypeStruct(w.shape, F32) for w in ws]
    vm = pl.BlockSpec(memory_space=pltpu.VMEM)
    outs = pl.pallas_call(body, in_specs=[vm] * (4 * n), out_specs=[vm] * (3 * n), out_shape=shapes * 3,
                          name="adamw_small")(*ws, *gs, *ms, *vs)
    return outs[:n], outs[n:2 * n], outs[2 * n:]


MESH = pl.DeviceIdType.MESH
ANY = pl.BlockSpec(memory_space=pl.ANY)


def _place():
    x, y, c = lax.axis_index("x"), lax.axis_index("y"), lax.axis_index("c")
    return x, y, c, [(1 - x, y), (x, 1 - y), (1 - x, 1 - y)]


def _chip_sum(grad, recv, core, name):
    _, R, C = grad.shape
    blk, steps, at = _slab_tiles(R, C, rows=512 if R % 512 == 0 else 256, cols=512)

    def body(c_ref, g_ref, r_ref, o_ref):
        o_ref[...] = (g_ref[...].astype(F32) + r_ref[...].astype(F32)).astype(o_ref.dtype)

    grid_spec = pltpu.PrefetchScalarGridSpec(
        num_scalar_prefetch=1, grid=(4, steps),
        in_specs=[pl.BlockSpec((None,) + blk, lambda q, i, c_ref: (2 * q + c_ref[0],) + at(i)),
                  pl.BlockSpec((None,) + blk, lambda q, i, c_ref: (q,) + at(i))],
        out_specs=pl.BlockSpec((None,) + blk, lambda q, i, c_ref: (q,) + at(i)))
    return pl.pallas_call(body, grid_spec=grid_spec, out_shape=jax.ShapeDtypeStruct((4, R, C), BF16),
                          compiler_params=_params(("parallel", "parallel")), name=name)(core, grad, recv)


HBM_SPEC = pl.BlockSpec(memory_space=pltpu.HBM)
SEM_SPEC = pl.BlockSpec(memory_space=pltpu.SEMAPHORE)
DATAFLOW = pltpu.SideEffectType.DATAFLOW_SIDE_EFFECTING


def _split_start(name, bufs, plan, counts, after=None):
    nb, ng = len(bufs), len(counts)
    extra = [] if after is None else [after]
    place = [(g, k) for g, cnt in enumerate(counts) for k in range(cnt)]

    def body(*refs):
        sems, token = refs[nb + len(extra):nb + len(extra) + 2 * ng], refs[-1]
        for (g, k), (src, dst, to) in zip(place, plan(refs[:nb])):
            pltpu.make_async_remote_copy(src_ref=src, dst_ref=dst, send_sem=sems[2 * g].at[k], recv_sem=sems[2 * g + 1].at[k],
                                         device_id=to, device_id_type=MESH).start()
        token[...] = jnp.zeros_like(token)

    outs = pl.pallas_call(
        body, name=name,
        out_shape=(*[pltpu.SemaphoreType.DMA((cnt,)) for cnt in counts for _ in range(2)],
                   *[pltpu.HBM(b.shape, b.dtype) for b in bufs], jax.ShapeDtypeStruct((8, LANE), F32)),
        in_specs=[HBM_SPEC] * nb + [ANY] * len(extra),
        out_specs=(*[SEM_SPEC] * (2 * ng), *[HBM_SPEC] * nb, pl.BlockSpec(memory_space=pltpu.VMEM)),
        input_output_aliases={i: 2 * ng + i for i in range(nb)},
        compiler_params=pltpu.CompilerParams(has_side_effects=DATAFLOW))(
            *[pltpu.with_memory_space_constraint(b, pltpu.HBM) for b in bufs], *extra)
    return [(outs[2 * g], outs[2 * g + 1]) for g in range(ng)], list(outs[2 * ng:2 * ng + nb]), outs[-1]


def _split_wait(name, sems, bufs, plan, after):
    nb = len(bufs)
    send_sems, recv_sems = sems

    def body(*refs):
        send_s, recv_s = refs[nb], refs[nb + 1]
        for k, (src, dst, to) in enumerate(plan(refs[:nb])):
            cp = pltpu.make_async_remote_copy(src_ref=src, dst_ref=dst, send_sem=send_s.at[k], recv_sem=recv_s.at[k],
                                              device_id=to, device_id_type=MESH)
            cp.wait_send()
            cp.wait_recv()

    after = tuple(after) if isinstance(after, (tuple, list)) else (after,)
    outs = pl.pallas_call(
        body, name=name, out_shape=tuple(pltpu.HBM(b.shape, b.dtype) for b in bufs),
        in_specs=[HBM_SPEC] * nb + [SEM_SPEC, SEM_SPEC] + [ANY] * len(after), out_specs=tuple([HBM_SPEC] * nb),
        input_output_aliases={i: i for i in range(nb)},
        compiler_params=pltpu.CompilerParams(has_side_effects=DATAFLOW))(*bufs, send_sems, recv_sems, *after)
    return list(outs)


def _slot(px, py, pc):
    return 4 * px + 2 * py + pc


class _Gather:
    def __init__(self, shards, groups, dev):
        self.shards, self.groups, self.dev = shards, groups, dev
        self.second = {}

    @staticmethod
    def _plan1(pairs, refs):
        x, y, c, chips = _place()
        out = []
        for s, l in pairs:
            dst = refs[l].at[_slot(x, y, c)]
            out.append((refs[s], dst, (x, y, 1 - c)))
            out += [(refs[s], dst, (px, py, c)) for px, py in chips]
        return out

    @staticmethod
    def _plan2(refs):
        x, y, c, chips = _place()
        return [(r.at[_slot(px, py, c)],) * 2 + ((x, y, 1 - c),) for r in refs for px, py in chips]

    def start(self):
        n = len(self.shards)
        lands = [lax.dynamic_update_slice(lax.empty((N_DEV,) + s.shape, s.dtype), s[None], (self.dev, 0, 0))
                 for s in self.shards]
        pairs = [(w, n + w) for g in self.groups for w in g]
        sems, bufs, token = _split_start("gather_start_1", list(self.shards) + lands, functools.partial(self._plan1, pairs),
                                         tuple(4 * len(g) for g in self.groups))
        self.first = [(sems[i], [bufs[w] for w in g], [bufs[n + w] for w in g]) for i, g in enumerate(self.groups)]
        return token

    def mid(self, gi, after):
        sems, srcs, lands = self.first[gi]
        m = len(srcs)
        plan = functools.partial(self._plan1, [(w, m + w) for w in range(m)])
        lands = _split_wait("gather_%d_wait_1" % gi, sems, srcs + lands, plan, after)[m:]
        sems, lands, token = _split_start("gather_%d_start_2" % gi, lands, self._plan2, (3 * m,))
        self.second[gi] = (sems[0], lands)
        return token

    def finish(self, gi, after):
        sems, lands = self.second[gi]
        return _split_wait("gather_%d_wait_2" % gi, sems, lands, self._plan2, after)


class _Exchanges:
    def __init__(self, tag, n, core, gather=None):
        self.tag, self.n, self.core, self.gather = tag, n, core, gather

    def weights_mid(self, group, after):
        return self.gather.mid(group, after)

    def weights_finish(self, group, after):
        return self.gather.finish(group, after)

    def _reduce_plan1(self, refs):
        n = self.n
        x, y, c, _ = _place()
        return [(refs[w].at[2 * q + (1 - c)], refs[n + w].at[q], (x, y, 1 - c)) for w in range(n) for q in range(4)]

    def _reduce_plan2(self, refs):
        n = self.n
        x, y, c, chips = _place()
        return [(refs[w].at[2 * px + py], refs[n + w].at[j], (px, py, c))
                for w in range(n) for j, (px, py) in enumerate(chips)]

    def grads_start(self, grads):
        lands = [lax.empty((4,) + g.shape[1:], g.dtype) for g in grads]
        self.g1 = _split_start(self.tag + "reduce_start_1", list(grads) + lands, self._reduce_plan1, (4 * self.n,))
        return self.g1[2]

    def grads_mid(self, after):
        n = self.n
        sems, bufs, _ = self.g1
        bufs = _split_wait(self.tag + "reduce_wait_1", sems[0], bufs, self._reduce_plan1, after)
        core = self.core.reshape(1).astype(jnp.int32)
        self.parts = [_chip_sum(bufs[w], bufs[n + w], core, self.tag + "reduce_chip_sum_%d" % w) for w in range(n)]
        lands = [lax.empty((3,) + p.shape[1:], p.dtype) for p in self.parts]
        self.g2 = _split_start(self.tag + "reduce_start_2", self.parts + lands, self._reduce_plan2, (3 * n,))
        return self.g2[2]

    def grads_finish(self, after):
        n = self.n
        sems, bufs, _ = self.g2
        bufs = _split_wait(self.tag + "reduce_wait_2", sems[0], bufs, self._reduce_plan2, after)
        self.parts, self.terms = bufs[:n], bufs[n:]


def _all_reduce_small(buf):
    R = buf.shape[0]

    def body(x_ref, o_ref, g_ref, send_sems, recv_sems):
        x, y, c, chips = _place()
        me, sibling = (x, y, c), (x, y, 1 - c)

        def slot(px, py, pc):
            return 4 * px + 2 * py + pc

        def copy(k, block, to, src=None):
            dst = g_ref.at[slot(*block)]
            return pltpu.make_async_remote_copy(src_ref=dst if src is None else src, dst_ref=dst,
                                                send_sem=send_sems.at[k], recv_sem=recv_sems.at[k],
                                                device_id=to, device_id_type=MESH)

        first = [copy(0, me, sibling, src=x_ref)]
        first += [copy(1 + j, me, (*chip, c), src=x_ref) for j, chip in enumerate(chips)]
        for cp in first:
            cp.start()
        g_ref[slot(*me)] = x_ref[...]
        passed = [copy(4 + j, (*chip, c), sibling) for j, chip in enumerate(chips)]
        for j, chip in enumerate(chips):
            copy(1 + j, (*chip, c), me).wait_recv()
            passed[j].start()
        copy(0, sibling, me).wait_recv()
        for j, chip in enumerate(chips):
            copy(4 + j, (*chip, 1 - c), me).wait_recv()
        for cp in first + passed:
            cp.wait_send()
        acc = g_ref[0]
        for s in range(1, N_DEV):
            acc = acc + g_ref[s]
        o_ref[...] = acc

    vm = pl.BlockSpec(memory_space=pltpu.VMEM)
    return pl.pallas_call(
        body, in_specs=[vm], out_specs=vm, out_shape=jax.ShapeDtypeStruct((R, LANE), F32),
        scratch_shapes=[pltpu.VMEM((N_DEV, R, LANE), F32), pltpu.SemaphoreType.DMA((7,)), pltpu.SemaphoreType.DMA((7,))],
        name="all_reduce_small")(buf)


def _pad_cols(a, width):
    return jnp.pad(a, ((0, 0), (0, width - a.shape[1])))


def _local_step(x, target, w_in_g, conv_w, a_log, dt_bias, delta_norm_w, sinks, bias, ln1_g, ln1_b, ln2_g, ln2_b,
                ex, ex_in):
    S, D = x.shape
    aw = D // 2
    hq, hkv, nh = aw // HEAD_A, aw // HEAD_A // GQA, aw // HEAD_D
    kvw = hkv * HEAD_A
    c_q, c_k, c_v, c_d = 0, aw, aw + kvw, aw + 2 * kvw
    c_ab = c_d + 3 * aw
    c_z = c_ab + 2 * nh
    n_in = c_z + aw
    n_slab = w_in_g.shape[1]
    assert w_in_g.shape == (N_DEV, n_in // N_DEV, D), (w_in_g.shape, n_in)

    w_in_t = w_in_g.reshape(n_in, D)
    w_pt = jnp.concatenate([w_in_t[:c_ab], w_in_t[c_z:], jnp.pad(w_in_t[c_ab:c_z], ((0, LANE - 2 * nh), (0, 0)))], axis=0)
    p_z, p_ab = c_ab, c_ab + aw
    n_p = p_ab + LANE

    xb = x.astype(BF16)
    proj = _matmul(xb, w_pt, NT, name="proj", tn=1152)
    attn_out = _attn_fwd(proj, bias, sinks.reshape(-1), hq, 0, c_k // kvw, c_v // kvw, D)
    conv2 = conv_w.reshape(CONV_W, 3 * aw)
    qkv = _gdn_prep_fwd(proj, conv2, nh, c_d // HEAD_D)
    ab = proj[:, p_ab:]
    al, dt = _pad_cols(a_log, LANE), _pad_cols(dt_bias, LANE)
    gb = _gates_fwd(ab, al, dt, nh)
    u_d, wqk, attn_d, t_d = _gdn_local_fwd(qkv, gb, nh)
    o_d, vn, states = _gdn_scan_fwd(u_d, wqk, attn_d, gb, nh, ex.weights_mid(1, u_d))
    mix = _gated_norm_fwd(o_d, proj, delta_norm_w, nh, p_z // HEAD_D, attn_out, aw // HEAD_D)
    w_o_g, w_up_g = ex.weights_finish(1, mix)
    w_o = w_o_g.reshape(D, D)
    mixed = _matmul(mix, w_o, NN, name="out_proj")
    h1 = _ln1_fwd(x, mixed, ln1_g, ln1_b)
    u = _matmul(h1, w_up_g, NN, name="mlp_up", b_groups=True, out_dtype=BF16, deps=(ex.weights_mid(2, h1),))
    (w_down_g,) = ex.weights_finish(2, u)
    w_down = w_down_g.reshape(-1, D)
    mlp = _matmul(u, w_down, NN, name="mlp_down", a_fn=_relu_sq, tk=4096)
    dr2, loss_row, dln2_g, dln2_b = _ln2_loss(h1, mlp, ln2_g, ln2_b, target)

    du = _matmul(dr2, w_down, NT, name="d_mlp_act", epi=_relu_sq_grad, epi_in=(u,), out_dtype=BF16)
    dw_down = _matmul(u, dr2, TN, name="dw_down", a_fn=_relu_sq, out_dtype=BF16)
    dw_up = _matmul(h1, du, TN, name="dw_up", out_dtype=BF16, out_groups=N_DEV)
    dh_mlp = _matmul(du, w_up_g, NT, name="d_h1", b_groups=True, tk=4096)
    dr1, dln1_g, dln1_b = _ln1_bwd(x, mixed, ln1_g, dr2, dh_mlp)
    dw_o = _matmul(mix, dr1, TN, name="dw_o", out_dtype=BF16)
    tok = ex.grads_start([dw_o.reshape(N_DEV, -1, D), dw_up, dw_down.reshape(N_DEV, -1, D)])
    dmix = _matmul(dr1, w_o, NT, name="d_mix", deps=(tok,))
    dq_a, dk_a, dv_a, dbias, dsink = _attn_bwd(proj, bias, sinks.reshape(-1), mix, dmix, hq, 0, c_k // kvw, c_v // kvw)
    drel = _rel_bias_grad(dbias, hq)
    do_d, dz, dnw = _gated_norm_bwd(o_d, proj, delta_norm_w, dmix, nh, p_z // HEAD_D, aw // HEAD_D)
    dvn_s, dw_s, dqd, dkd, dattn_d, dgl = _gdn_scan_bwd(wqk, attn_d, gb, states, vn, do_d, nh, ex.grads_mid(dq_a))
    dqkv_n, dg, dbeta = _gdn_local_bwd(qkv, gb, t_d, u_d, wqk, dvn_s, dw_s, dqd, dkd, dattn_d, dgl, nh)
    dgb = _pad_cols(jnp.concatenate([dg.reshape(nh, S).T, dbeta.reshape(nh, S).T], axis=1), LANE)
    dab, da_log, ddt_bias = _gates_bwd(ab, al, dt, dgb, nh)
    dqkv_d, dconv = _gdn_prep_bwd(proj, conv2, dqkv_n, nh, c_d // HEAD_D)
    dproj = jnp.concatenate([dq_a, dk_a, dv_a, dqkv_d, dz, dab], axis=1).astype(BF16)
    dw_pt = _matmul(dproj, xb, TN, name="dw_in", out_dtype=BF16, tm=1152)

    def grad_rows(lo, hi):
        cuts = sorted({lo, hi, *[c for c in (c_ab, c_z) if lo < c < hi]})
        place = lambda r: r if r < c_ab else (p_ab + r - c_ab if r < c_z else p_z + r - c_z)
        return [dw_pt[place(a):place(a) + b - a] for a, b in zip(cuts[:-1], cuts[1:])]

    dw_in_g = jnp.stack([jnp.concatenate(grad_rows(g * n_slab, (g + 1) * n_slab), axis=0) for g in range(N_DEV)])
    tok = ex_in.grads_mid(ex_in.grads_start([dw_in_g]))
    grad_x = _matmul(dproj, w_pt, NN, name="d_x", tm=512, tk=n_p, deps=(tok,), epi=_residual_grad, epi_in=(dr1,))
    ex.grads_finish(grad_x)

    small = dict(conv_w=dconv, a_log=da_log[:, :nh], dt_bias=ddt_bias[:, :nh], delta_norm_w=dnw,
                 attn_sinks=dsink[:, 0].reshape(1, hq), rel_bias=drel[:, :, 0].T,
                 ln1_g=dln1_g, ln1_b=dln1_b, ln2_g=dln2_g, ln2_b=dln2_b)
    return loss_row, grad_x, small


SMALL_ORDER = ("conv_w", "a_log", "dt_bias", "delta_norm_w", "attn_sinks", "rel_bias", "ln1_g", "ln1_b", "ln2_g", "ln2_b")


def _pack_small(loss_row, small):
    parts = [loss_row.reshape(-1)]
    for k in SMALL_ORDER:
        flat = small[k].reshape(-1)
        parts.append(jnp.pad(flat, (0, (-flat.shape[0]) % LANE)))
    flat = jnp.concatenate(parts)
    flat = jnp.pad(flat, (0, (-flat.shape[0]) % (8 * LANE)))
    return flat.reshape(-1, LANE)


def _unpack_small(buf, small_shapes):
    flat = buf.reshape(-1)
    loss = flat[0]
    off = LANE
    out = {}
    for k in SMALL_ORDER:
        n = int(np.prod(small_shapes[k]))
        out[k] = flat[off:off + n].reshape(small_shapes[k])
        off += n + (-n) % LANE
    return loss, out


def kernel(x, w_in, conv_w, a_log, dt_bias, delta_norm_w, attn_sinks, rel_bias, w_o, ln1_g, ln1_b, w_up, w_down, ln2_g, ln2_b, loss_target, m_w_in, m_conv_w, m_a_log, m_dt_bias, m_delta_norm_w, m_attn_sinks, m_rel_bias, m_w_o, m_ln1_g, m_ln1_b, m_w_up, m_w_down, m_ln2_g, m_ln2_b, v_w_in, v_conv_w, v_a_log, v_dt_bias, v_delta_norm_w, v_attn_sinks, v_rel_bias, v_w_o, v_ln1_g, v_ln1_b, v_w_up, v_w_down, v_ln2_g, v_ln2_b):
    S, D = x.shape[1], x.shape[2]
    core = lax.axis_index("c")
    dev = 4 * lax.axis_index("x") + 2 * lax.axis_index("y") + core

    gather = _Gather([conv_w[0, :, 0, :], w_in[0].T.astype(BF16), w_o[0].astype(BF16), w_up[0].astype(BF16),
                      w_down[0].astype(BF16)], [[0, 1], [2, 3], [4]], dev)
    token = gather.start()
    bias = _attn_bias(rel_bias.T)
    w_t, m_t, v_t = w_in[0].T, m_w_in[0].T, v_w_in[0].T
    conv_g, w_in_g = gather.finish(0, gather.mid(0, (token, bias, w_t, m_t, v_t)))
    ex = _Exchanges("", 3, core, gather)
    ex_in = _Exchanges("in_", 1, core)

    cw_sh = conv_w.shape[3]
    conv_full = jnp.transpose(conv_g, (1, 0, 2)).reshape(CONV_W, N_DEV * cw_sh)

    loss_row, grad_x, small = _local_step(
        x[0], loss_target[0], w_in_g, conv_full, a_log, dt_bias, delta_norm_w, attn_sinks, bias,
        ln1_g, ln1_b, ln2_g, ln2_b, ex, ex_in)

    chip_arr = (dev // 2).reshape(1).astype(jnp.int32)
    big = {}
    for i, (name, w, m, v) in enumerate((("w_o", w_o, m_w_o, v_w_o), ("w_up", w_up, m_w_up, v_w_up),
                                         ("w_down", w_down, m_w_down, v_w_down))):
        big[name] = [o[None] for o in _adamw_big(ex.parts[i], ex.terms[i], chip_arr, w[0], m[0], v[0], "adamw_" + name)]
    ex_in.grads_finish(big["w_down"][0])
    outs = _adamw_big(ex_in.parts[0], ex_in.terms[0], chip_arr, w_t, m_t, v_t, "adamw_w_in")
    big["w_in"] = [o.T[None] for o in outs]

    small_shapes = {k: v.shape for k, v in small.items()}
    loss, small = _unpack_small(_all_reduce_small(_pack_small(loss_row, small)), small_shapes)
    small["conv_w"] = lax.dynamic_slice(small["conv_w"], (0, dev * cw_sh), (CONV_W, cw_sh))
    small["rel_bias"] = small["rel_bias"].reshape(rel_bias.shape)
    p2 = dict(conv_w=(conv_w, m_conv_w, v_conv_w), a_log=(a_log, m_a_log, v_a_log), dt_bias=(dt_bias, m_dt_bias, v_dt_bias),
              delta_norm_w=(delta_norm_w, m_delta_norm_w, v_delta_norm_w), attn_sinks=(attn_sinks, m_attn_sinks, v_attn_sinks),
              rel_bias=(rel_bias, m_rel_bias, v_rel_bias), ln1_g=(ln1_g, m_ln1_g, v_ln1_g), ln1_b=(ln1_b, m_ln1_b, v_ln1_b),
              ln2_g=(ln2_g, m_ln2_g, v_ln2_g), ln2_b=(ln2_b, m_ln2_b, v_ln2_b))
    two_d = lambda a: a.reshape(-1, a.shape[-1])
    ws = [two_d(p2[k][0]) for k in SMALL_ORDER]
    gs = [two_d(small[k]) for k in SMALL_ORDER]
    ms = [two_d(p2[k][1]) for k in SMALL_ORDER]
    vs = [two_d(p2[k][2]) for k in SMALL_ORDER]
    ds, nms, nvs = _adamw_small(ws, gs, ms, vs)
    res = {}
    for i, k in enumerate(SMALL_ORDER):
        shp = p2[k][0].shape
        res[k] = [gs[i].reshape(shp), ds[i].reshape(shp), nms[i].reshape(shp), nvs[i].reshape(shp)]
    res.update(big)
    order = ("w_in", "conv_w", "a_log", "dt_bias", "delta_norm_w", "attn_sinks", "rel_bias", "w_o", "ln1_g", "ln1_b",
             "w_up", "w_down", "ln2_g", "ln2_b")
    return (loss, grad_x[None], *[res[k][0] for k in order], *[res[k][1] for k in order],
            *[res[k][2] for k in order], *[res[k][3] for k in order])
```

```python
import functools
import math

import numpy as np
import jax
import jax.numpy as jnp
from jax import lax
from jax.experimental import pallas as pl
from jax.experimental.pallas import tpu as pltpu

F32 = jnp.float32
BF16 = jnp.bfloat16
HIGHEST = lax.Precision.HIGHEST

N_DEV = 8
HEAD_A = 64
GQA = 4
BLK = 128
N_BUCKETS = 32
MAX_DISTANCE = 128
HEAD_D = 128
CONV_W = 4
CHUNK = 64
NEG_INF = -1e30
LN_EPS = 1e-5
RMS_EPS = 1e-6
DN_ALPHA = 2.0 ** 0.25
ADAM_LR, ADAM_B1, ADAM_B2, ADAM_EPS, ADAM_WD, ADAM_STEP = 0.001, 0.9, 0.999, 1e-08, 0.01, 10

LANE = 128
VMEM_LIMIT = 56 * 1024 * 1024

NN = ((1,), (0,))
NT = ((1,), (1,))
TN = ((0,), (0,))


def _dot(a, b, dims, prec=None):
    return lax.dot_general(a, b, (dims, ((), ())), precision=prec, preferred_element_type=F32)


def _tile(dim, pref):
    if dim <= pref:
        return dim
    t = (pref // LANE) * LANE
    while t > LANE and dim % t:
        t -= LANE
    assert dim % t == 0, (dim, pref)
    return t


def _params(sem):
    return pltpu.CompilerParams(dimension_semantics=sem, vmem_limit_bytes=VMEM_LIMIT)


def _matmul(a, b, dims, *, name, out_dtype=F32, tm=1024, tn=1024, tk=2048, a_fn=None, epi=None, epi_in=(),
            b_groups=None, out_groups=None, deps=()):
    (ca,), (cb,) = dims
    M, K = a.shape[1 - ca], a.shape[ca]
    if b_groups:
        G, R, C = b.shape
        bshape = (R, G * C)
    else:
        bshape = b.shape
    N = bshape[1 - cb]
    assert bshape[cb] == K, (a.shape, b.shape, dims)
    tm, tk = _tile(M, tm), _tile(K, tk)
    if b_groups:
        lim = C if cb == 0 else tn
        tn = _tile(N, min(tn, lim))
        if cb == 1 and tk < C:
            tk = _tile(K, min(tk, C))
        elif cb == 1:
            tk = C * max(1, tk // C)
    else:
        tn = _tile(N, tn)
    if out_groups:
        tn = _tile(N, min(tn, N // out_groups))
    nk = K // tk
    b_span = tk // C if (b_groups and cb == 1 and tk > C) else 1

    def body(*refs):
        a_ref, b_ref = refs[0], refs[1]
        e_refs = refs[2:2 + len(epi_in)]
        o_ref = refs[2 + len(epi_in) + len(deps)]
        acc_ref = refs[3 + len(epi_in) + len(deps)] if nk > 1 else None
        k = pl.program_id(2)
        av = a_ref[...]
        if a_fn is not None:
            av = a_fn(av)
        if b_span > 1:
            prod = sum(_dot(av[:, g * C:(g + 1) * C].astype(BF16), b_ref[g].astype(BF16), dims) for g in range(b_span))
        else:
            prod = _dot(av.astype(BF16), b_ref[...].astype(BF16), dims)

        def finish(r):
            if epi is not None:
                r = epi(r, *[e[...] for e in e_refs])
            o_ref[...] = r.astype(out_dtype)

        if nk == 1:
            finish(prod)
            return

        @pl.when(k == 0)
        def _():
            acc_ref[...] = prod

        @pl.when(k > 0)
        def _():
            acc_ref[...] += prod

        @pl.when(k == nk - 1)
        def _():
            finish(acc_ref[...])

    a_spec = (pl.BlockSpec((tm, tk), lambda i, j, k: (i, k)) if ca == 1
              else pl.BlockSpec((tk, tm), lambda i, j, k: (k, i)))
    if b_groups:
        if cb == 0:
            per = C // tn
            b_spec = pl.BlockSpec((None, tk, tn), lambda i, j, k: (j // per, k, j % per))
        elif b_span > 1:
            b_spec = pl.BlockSpec((b_span, tn, C), lambda i, j, k: (k, j, 0))
        else:
            per = C // tk
            b_spec = pl.BlockSpec((None, tn, tk), lambda i, j, k: (k // per, j, k % per))
    else:
        b_spec = (pl.BlockSpec((tk, tn), lambda i, j, k: (k, j)) if cb == 0
                  else pl.BlockSpec((tn, tk), lambda i, j, k: (j, k)))
    e_specs = [pl.BlockSpec((tm, tn), lambda i, j, k: (i, j)) for _ in epi_in]
    if out_groups:
        per_o = (N // out_groups) // tn
        o_spec = pl.BlockSpec((None, tm, tn), lambda i, j, k: (j // per_o, i, j % per_o))
        o_shape = jax.ShapeDtypeStruct((out_groups, M, N // out_groups), out_dtype)
    else:
        o_spec = pl.BlockSpec((tm, tn), lambda i, j, k: (i, j))
        o_shape = jax.ShapeDtypeStruct((M, N), out_dtype)
    return pl.pallas_call(
        body, grid=(M // tm, N // tn, nk), out_specs=o_spec,
        in_specs=[a_spec, b_spec] + e_specs + [pl.BlockSpec(memory_space=pl.ANY)] * len(deps),
        out_shape=o_shape, scratch_shapes=[pltpu.VMEM((tm, tn), F32)] if nk > 1 else [],
        compiler_params=_params(("parallel", "parallel", "arbitrary")), name=name)(a, b, *epi_in, *deps)


def _relu_sq(u):
    r = jnp.maximum(u, 0.0)
    return r * r


def _relu_sq_grad(acc, u):
    return acc * (2.0 * jnp.maximum(u, 0.0))


def _ln_stats(r):
    mu = jnp.mean(r, axis=-1, keepdims=True)
    xc = r - mu
    var = jnp.mean(xc * xc, axis=-1, keepdims=True)
    rstd = lax.rsqrt(var + LN_EPS)
    return xc * rstd, rstd


def _ln_bwd(dy, xhat, rstd, g):
    dxh = dy * g
    m1 = jnp.mean(dxh, axis=-1, keepdims=True)
    m2 = jnp.mean(dxh * xhat, axis=-1, keepdims=True)
    return rstd * (dxh - m1 - xhat * m2)


def _row_call(body, ins, row_ins, outs, acc_outs, name, tr=256):
    S = ins[0].shape[0]
    tr = min(tr, S)
    n_in, n_row, n_out = len(ins), len(row_ins), len(outs)

    def wrapped(*refs):
        i = pl.program_id(0)
        acc_refs = refs[n_in + n_row + n_out:]

        @pl.when(i == 0)
        def _():
            for r in acc_refs:
                r[...] = jnp.zeros_like(r)

        body(*refs)

    in_specs = [pl.BlockSpec((tr, a.shape[1]), lambda i: (i, 0)) for a in ins]
    in_specs += [pl.BlockSpec(a.shape, lambda i: (0, 0)) for a in row_ins]
    out_specs = [pl.BlockSpec((tr, s.shape[1]), lambda i: (i, 0)) for s in outs]
    out_specs += [pl.BlockSpec(s.shape, lambda i: (0, 0)) for s in acc_outs]
    return pl.pallas_call(wrapped, grid=(S // tr,), in_specs=in_specs, out_specs=out_specs,
                          out_shape=list(outs) + list(acc_outs),
                          compiler_params=_params(("arbitrary",)), name=name)(*ins, *row_ins)


def _ln1_fwd(x, mixed, g, b):
    def body(x_ref, m_ref, g_ref, b_ref, h_ref):
        xhat, _ = _ln_stats(DN_ALPHA * x_ref[...] + m_ref[...])
        h_ref[...] = xhat * g_ref[...] + b_ref[...]
    return _row_call(body, [x, mixed], [g, b], [jax.ShapeDtypeStruct(x.shape, F32)], [], "ln1_fwd")[0]


def _ln2_loss(h1, mlp, g, b, target):
    S, D = h1.shape
    sds = jax.ShapeDtypeStruct

    def body(h_ref, m_ref, t_ref, g_ref, b_ref, dr_ref, loss_ref, dg_ref, db_ref):
        xhat, rstd = _ln_stats(DN_ALPHA * h_ref[...] + m_ref[...])
        gv = g_ref[...]
        err = xhat * gv + b_ref[...] - t_ref[...]
        loss_ref[...] += jnp.sum(jnp.sum(err * err, axis=0, keepdims=True), axis=1, keepdims=True) * (0.5 / D)
        dy = err * (1.0 / D)
        dg_ref[...] += jnp.sum(dy * xhat, axis=0, keepdims=True)
        db_ref[...] += jnp.sum(dy, axis=0, keepdims=True)
        dr_ref[...] = _ln_bwd(dy, xhat, rstd, gv)

    return _row_call(body, [h1, mlp, target], [g, b], [sds((S, D), F32)],
                     [sds((1, LANE), F32), sds((1, D), F32), sds((1, D), F32)], "ln2_loss")


def _ln1_bwd(x, mixed, g, dr2, dh_mlp):
    S, D = x.shape
    sds = jax.ShapeDtypeStruct

    def body(x_ref, m_ref, dr2_ref, dh_ref, g_ref, dr_ref, dg_ref, db_ref):
        xhat, rstd = _ln_stats(DN_ALPHA * x_ref[...] + m_ref[...])
        dy = DN_ALPHA * dr2_ref[...] + dh_ref[...]
        dg_ref[...] += jnp.sum(dy * xhat, axis=0, keepdims=True)
        db_ref[...] += jnp.sum(dy, axis=0, keepdims=True)
        dr_ref[...] = _ln_bwd(dy, xhat, rstd, g_ref[...])

    return _row_call(body, [x, mixed, dr2, dh_mlp], [g], [sds((S, D), F32)],
                     [sds((1, D), F32), sds((1, D), F32)], "ln1_bwd")


def _residual_grad(acc, dr):
    return DN_ALPHA * dr + acc


def _bucket_table():
    qi = np.arange(BLK, dtype=np.int32)[:, None]
    kj = np.arange(2 * BLK, dtype=np.int32)[None, :]
    dist = qi + BLK - kj
    n = np.maximum(dist, 0)
    max_exact = N_BUCKETS // 2
    nf = np.maximum(n, 1).astype(np.float32)
    large = max_exact + (np.log(nf / np.float32(max_exact)) / np.float32(math.log(MAX_DISTANCE / max_exact))
                         * np.float32(N_BUCKETS - max_exact)).astype(np.int32)
    large = np.minimum(large, N_BUCKETS - 1)
    bucket = np.where(n < max_exact, n, large)
    return np.where((dist >= 0) & (dist < BLK), bucket, -1).astype(np.int32)


def _attn_bias(rel_bias_t):
    hq = rel_bias_t.shape[0]
    bucket = jnp.asarray(_bucket_table())

    def body(rb_ref, bk_ref, o_ref):
        h = pl.program_id(0)
        bk = bk_ref[...]
        acc = jnp.zeros((BLK, 2 * BLK), F32)
        for b in range(N_BUCKETS):
            acc = jnp.where(bk == b, rb_ref[h, b], acc)
        o_ref[...] = acc

    return pl.pallas_call(
        body, grid=(hq,),
        in_specs=[pl.BlockSpec(memory_space=pltpu.SMEM), pl.BlockSpec((BLK, 2 * BLK), lambda h: (0, 0))],
        out_specs=pl.BlockSpec((BLK, 2 * BLK), lambda h: (h, 0)),
        out_shape=jax.ShapeDtypeStruct((hq * BLK, 2 * BLK), F32),
        compiler_params=_params(("arbitrary",)), name="attn_bias")(rel_bias_t, bucket)


def _attn_probs(sc, sp, bias, sink, mask_c, mask_p):
    lc = jnp.where(mask_c, sc + bias[:, BLK:], NEG_INF)
    lp = jnp.where(mask_p, sp + bias[:, :BLK], NEG_INF)
    m = jnp.maximum(jnp.maximum(jnp.max(lc, axis=1, keepdims=True), jnp.max(lp, axis=1, keepdims=True)), sink)
    pc, pp, ps = jnp.exp(lc - m), jnp.exp(lp - m), jnp.exp(sink - m)
    inv = 1.0 / (jnp.sum(pc, axis=1, keepdims=True) + jnp.sum(pp, axis=1, keepdims=True) + ps)
    return pc, pp, ps, inv


def _attn_masks(n):
    qi = lax.broadcasted_iota(jnp.int32, (BLK, BLK), 0)
    kj = lax.broadcasted_iota(jnp.int32, (BLK, BLK), 1)
    return kj <= qi, (kj > qi) & (n > 0)


def _attn_fwd(proj, bias, sinks, hq, q_blk, k_blk, v_blk, out_width):
    S = proj.shape[0]
    hkv = hq // GQA
    wq, wk = hq * HEAD_A, hkv * HEAD_A

    def body(q_ref, k_ref, v_ref, bias_ref, sink_ref, o_ref):
        n = pl.program_id(0)
        cur = pl.multiple_of(n * BLK, BLK)
        prev = pl.multiple_of(jnp.maximum(n - 1, 0) * BLK, BLK)
        mask_c, mask_p = _attn_masks(n)
        for h4 in range(hkv):
            cs = slice(h4 * HEAD_A, (h4 + 1) * HEAD_A)
            kc, kp = k_ref[pl.ds(cur, BLK), cs].astype(BF16), k_ref[pl.ds(prev, BLK), cs].astype(BF16)
            vc, vp = v_ref[pl.ds(cur, BLK), cs].astype(BF16), v_ref[pl.ds(prev, BLK), cs].astype(BF16)
            hs_of = [slice(h * HEAD_A, (h + 1) * HEAD_A) for h in range(h4 * GQA, (h4 + 1) * GQA)]
            qs = [(q_ref[:, hs] * (HEAD_A ** -0.5)).astype(BF16) for hs in hs_of]
            scs = [_dot(q, kc, NT) for q in qs]
            sps = [_dot(q, kp, NT) for q in qs]
            pr = [_attn_probs(scs[g], sps[g], bias_ref[(h4 * GQA + g) * BLK:(h4 * GQA + g + 1) * BLK, :],
                              sink_ref[h4 * GQA + g], mask_c, mask_p) for g in range(GQA)]
            oc = [_dot(p[0].astype(BF16), vc, NN) for p in pr]
            op = [_dot(p[1].astype(BF16), vp, NN) for p in pr]
            for g, hs in enumerate(hs_of):
                o_ref[:, hs] = (oc[g] + op[g]) * pr[g][3]

    return pl.pallas_call(
        body, grid=(S // BLK,),
        in_specs=[pl.BlockSpec((BLK, wq), lambda n: (n, q_blk)), pl.BlockSpec((S, wk), lambda n: (0, k_blk)),
                  pl.BlockSpec((S, wk), lambda n: (0, v_blk)), pl.BlockSpec((hq * BLK, 2 * BLK), lambda n: (0, 0)),
                  pl.BlockSpec(memory_space=pltpu.SMEM)],
        out_specs=pl.BlockSpec((BLK, wq), lambda n: (n, 0)),
        out_shape=jax.ShapeDtypeStruct((S, out_width), F32),
        compiler_params=_params(("arbitrary",)), name="attn_fwd")(proj, proj, proj, bias, sinks)


def _attn_bwd(proj, bias, sinks, out, dmix, hq, q_blk, k_blk, v_blk):
    S = proj.shape[0]
    hkv = hq // GQA
    wq, wk = hq * HEAD_A, hkv * HEAD_A
    sds = jax.ShapeDtypeStruct

    def body(q_ref, k_ref, v_ref, bias_ref, sink_ref, o_ref, do_ref, dq_ref, dk_ref, dv_ref, dbias_ref, dsink_ref):
        n = pl.program_id(0)

        @pl.when(n == 0)
        def _():
            dk_ref[...] = jnp.zeros_like(dk_ref)
            dv_ref[...] = jnp.zeros_like(dv_ref)
            dbias_ref[...] = jnp.zeros_like(dbias_ref)
            dsink_ref[...] = jnp.zeros_like(dsink_ref)

        cur = pl.multiple_of(n * BLK, BLK)
        prev = pl.multiple_of(jnp.maximum(n - 1, 0) * BLK, BLK)
        mask_c, mask_p = _attn_masks(n)
        for h4 in range(hkv):
            cs = slice(h4 * HEAD_A, (h4 + 1) * HEAD_A)
            kc, kp = k_ref[pl.ds(cur, BLK), cs].astype(BF16), k_ref[pl.ds(prev, BLK), cs].astype(BF16)
            vc, vp = v_ref[pl.ds(cur, BLK), cs].astype(BF16), v_ref[pl.ds(prev, BLK), cs].astype(BF16)
            heads = list(range(h4 * GQA, (h4 + 1) * GQA))
            hs_of = [slice(h * HEAD_A, (h + 1) * HEAD_A) for h in heads]
            rows_of = [slice(h * BLK, (h + 1) * BLK) for h in heads]
            G = range(GQA)
            qs = [(q_ref[:, hs] * (HEAD_A ** -0.5)).astype(BF16) for hs in hs_of]
            dos = [do_ref[:, hs] for hs in hs_of]
            dobs = [d.astype(BF16) for d in dos]
            scs = [_dot(q, kc, NT) for q in qs]
            sps = [_dot(q, kp, NT) for q in qs]
            dpc = [_dot(d, vc, NT) for d in dobs]
            dpp = [_dot(d, vp, NT) for d in dobs]
            pcs, pps, dscs, dsps = [], [], [], []
            for g in G:
                pc, pp, ps, inv = _attn_probs(scs[g], sps[g], bias_ref[rows_of[g], :], sink_ref[heads[g]], mask_c, mask_p)
                pc, pp, ps = pc * inv, pp * inv, ps * inv
                delta = jnp.sum(dos[g] * o_ref[:, hs_of[g]], axis=1, keepdims=True)
                dsc, dsp = pc * (dpc[g] - delta), pp * (dpp[g] - delta)
                dsink_ref[heads[g]:heads[g] + 1, :] += jnp.broadcast_to(jnp.sum(-ps * delta, axis=0, keepdims=True), (1, LANE))
                dbias_ref[rows_of[g], BLK:] += dsc
                dbias_ref[rows_of[g], :BLK] += dsp
                pcs.append(pc.astype(BF16))
                pps.append(pp.astype(BF16))
                dscs.append(dsc.astype(BF16))
                dsps.append(dsp.astype(BF16))
            dq1 = [_dot(dscs[g], kc, NN) for g in G]
            dq2 = [_dot(dsps[g], kp, NN) for g in G]
            dkc = [_dot(dscs[g], qs[g], TN) for g in G]
            dkp = [_dot(dsps[g], qs[g], TN) for g in G]
            dvc = [_dot(pcs[g], dobs[g], TN) for g in G]
            dvp = [_dot(pps[g], dobs[g], TN) for g in G]
            for g in G:
                dq_ref[:, hs_of[g]] = (dq1[g] + dq2[g]) * (HEAD_A ** -0.5)
            dk_ref[pl.ds(cur, BLK), cs] += sum(dkc[1:], dkc[0])
            dk_ref[pl.ds(prev, BLK), cs] += sum(dkp[1:], dkp[0])
            dv_ref[pl.ds(cur, BLK), cs] += sum(dvc[1:], dvc[0])
            dv_ref[pl.ds(prev, BLK), cs] += sum(dvp[1:], dvp[0])

    return pl.pallas_call(
        body, grid=(S // BLK,),
        in_specs=[pl.BlockSpec((BLK, wq), lambda n: (n, q_blk)), pl.BlockSpec((S, wk), lambda n: (0, k_blk)),
                  pl.BlockSpec((S, wk), lambda n: (0, v_blk)), pl.BlockSpec((hq * BLK, 2 * BLK), lambda n: (0, 0)),
                  pl.BlockSpec(memory_space=pltpu.SMEM),
                  pl.BlockSpec((BLK, wq), lambda n: (n, 0)), pl.BlockSpec((BLK, wq), lambda n: (n, 0))],
        out_specs=[pl.BlockSpec((BLK, wq), lambda n: (n, 0)), pl.BlockSpec((S, wk), lambda n: (0, 0)),
                   pl.BlockSpec((S, wk), lambda n: (0, 0)), pl.BlockSpec((hq * BLK, 2 * BLK), lambda n: (0, 0)),
                   pl.BlockSpec((hq, LANE), lambda n: (0, 0))],
        out_shape=[sds((S, wq), F32), sds((S, wk), F32), sds((S, wk), F32), sds((hq * BLK, 2 * BLK), F32),
                   sds((hq, LANE), F32)],
        compiler_params=_params(("arbitrary",)), name="attn_bwd")(proj, proj, proj, bias, sinks, out, dmix)


def _rel_bias_grad(dbias, hq):
    bucket = jnp.asarray(_bucket_table())

    def body(d_ref, bk_ref, o_ref):
        d = d_ref[...]
        bk = bk_ref[...]
        rows = [jnp.sum(jnp.where(bk == b, d, 0.0), axis=0, keepdims=True) for b in range(N_BUCKETS)]
        tot = jnp.sum(jnp.concatenate(rows, axis=0), axis=1, keepdims=True)
        o_ref[...] = jnp.broadcast_to(tot, (N_BUCKETS, LANE))

    return pl.pallas_call(
        body, grid=(hq,),
        in_specs=[pl.BlockSpec((BLK, 2 * BLK), lambda h: (h, 0)), pl.BlockSpec((BLK, 2 * BLK), lambda h: (0, 0))],
        out_specs=pl.BlockSpec((None, N_BUCKETS, LANE), lambda h: (h, 0, 0)),
        out_shape=jax.ShapeDtypeStruct((hq, N_BUCKETS, LANE), F32),
        compiler_params=_params(("arbitrary",)), name="rel_bias_grad")(dbias, bucket)


def _sigmoid(x):
    return 0.5 * jnp.tanh(0.5 * x) + 0.5


def _shift_rows(x, s, row):
    n = x.shape[0]
    if s > 0:
        return jnp.where(row >= s, pltpu.roll(x, s, 0), 0.0)
    return jnp.where(row < n + s, pltpu.roll(x, n + s, 0), 0.0)


def _conv_silu_norm(xv, w, j, nh):
    row = lax.broadcasted_iota(jnp.int32, xv.shape, 0)
    xs = [xv] + [_shift_rows(xv, s, row) for s in range(1, CONV_W)]
    c = w[CONV_W - 1:CONV_W, :] * xv
    for s in range(1, CONV_W):
        c = c + w[CONV_W - 1 - s:CONV_W - s, :] * xs[s]
    sg = _sigmoid(c)
    a = c * sg
    r = lax.rsqrt(jnp.sum(a * a, axis=1, keepdims=True) + RMS_EPS)
    scale = jnp.where(j < nh, HEAD_D ** -0.5, 1.0)
    is_norm = j < 2 * nh
    y = jnp.where(is_norm, a * (r * scale), a)
    return c, sg, a, r, scale, is_norm, xs, row, y


def _gdn_prep_fwd(proj, conv_w, nh, blk0):
    S = proj.shape[0]

    def body(x_ref, w_ref, o_ref):
        j = pl.program_id(0)
        o_ref[...] = _conv_silu_norm(x_ref[...], w_ref[...], j, nh)[-1]

    return pl.pallas_call(
        body, grid=(3 * nh,),
        in_specs=[pl.BlockSpec((S, HEAD_D), lambda j: (0, blk0 + j)), pl.BlockSpec((CONV_W, HEAD_D), lambda j: (0, j))],
        out_specs=pl.BlockSpec((S, HEAD_D), lambda j: (0, 3 * (j % nh) + j // nh)),
        out_shape=jax.ShapeDtypeStruct((S, 3 * nh * HEAD_D), F32),
        compiler_params=_params(("parallel",)), name="gdn_prep_fwd")(proj, conv_w)


def _gdn_prep_bwd(proj, conv_w, dqkv, nh, blk0):
    S = proj.shape[0]
    sds = jax.ShapeDtypeStruct

    def body(x_ref, w_ref, dy_ref, dx_ref, dw_ref):
        j = pl.program_id(0)
        xv, w = x_ref[...], w_ref[...]
        c, sg, a, r, scale, is_norm, xs, row, _ = _conv_silu_norm(xv, w, j, nh)
        dy = dy_ref[...]
        rs = r * scale
        da_n = rs * dy - a * (r * r * rs) * jnp.sum(dy * a, axis=1, keepdims=True)
        da = jnp.where(is_norm, da_n, dy)
        dc = da * (sg * (1.0 + c * (1.0 - sg)))
        dx = w[CONV_W - 1:CONV_W, :] * dc
        dws = [jnp.sum(dc * xv, axis=0, keepdims=True)]
        for s in range(1, CONV_W):
            dx = dx + w[CONV_W - 1 - s:CONV_W - s, :] * _shift_rows(dc, -s, row)
            dws.insert(0, jnp.sum(dc * xs[s], axis=0, keepdims=True))
        dx_ref[...] = dx
        dw_ref[...] = jnp.concatenate(dws, axis=0)

    return pl.pallas_call(
        body, grid=(3 * nh,),
        in_specs=[pl.BlockSpec((S, HEAD_D), lambda j: (0, blk0 + j)), pl.BlockSpec((CONV_W, HEAD_D), lambda j: (0, j)),
                  pl.BlockSpec((S, HEAD_D), lambda j: (0, 3 * (j % nh) + j // nh))],
        out_specs=[pl.BlockSpec((S, HEAD_D), lambda j: (0, j)), pl.BlockSpec((CONV_W, HEAD_D), lambda j: (0, j))],
        out_shape=[sds((S, 3 * nh * HEAD_D), F32), sds((CONV_W, 3 * nh * HEAD_D), F32)],
        compiler_params=_params(("parallel",)), name="gdn_prep_bwd")(proj, conv_w, dqkv)


def _softplus(x):
    return jnp.maximum(x, 0.0) + jnp.log(1.0 + jnp.exp(-jnp.abs(x)))


def _gates_fwd(ab, al, dt, nh):
    S = ab.shape[0]

    def body(ab_ref, al_ref, dt_ref, o_ref):
        v = ab_ref[...]
        lane = lax.broadcasted_iota(jnp.int32, v.shape, 1)
        g = -jnp.exp(al_ref[...]) * _softplus(v + dt_ref[...])
        o_ref[...] = jnp.where(lane < nh, g, jnp.where(lane < 2 * nh, _sigmoid(v), 0.0))

    row = pl.BlockSpec((1, LANE), lambda i: (0, 0))
    full = pl.BlockSpec((S, LANE), lambda i: (0, 0))
    return pl.pallas_call(body, grid=(1,), in_specs=[full, row, row], out_specs=full,
                          out_shape=jax.ShapeDtypeStruct((S, LANE), F32),
                          compiler_params=_params(("arbitrary",)), name="gates_fwd")(ab, al, dt)


def _gates_bwd(ab, al, dt, dgb, nh):
    S = ab.shape[0]
    sds = jax.ShapeDtypeStruct

    def body(ab_ref, al_ref, dt_ref, d_ref, dab_ref, dal_ref, ddt_ref):
        v, d = ab_ref[...], d_ref[...]
        lane = lax.broadcasted_iota(jnp.int32, v.shape, 1)
        is_a = lane < nh
        z = v + dt_ref[...]
        dsp = jnp.where(is_a, d * (-jnp.exp(al_ref[...])), 0.0)
        dz = dsp * _sigmoid(z)
        beta = _sigmoid(v)
        dab_ref[...] = jnp.where(is_a, dz, jnp.where(lane < 2 * nh, d * beta * (1.0 - beta), 0.0))
        dal_ref[...] = jnp.sum(dsp * _softplus(z), axis=0, keepdims=True)
        ddt_ref[...] = jnp.sum(dz, axis=0, keepdims=True)

    row = pl.BlockSpec((1, LANE), lambda i: (0, 0))
    full = pl.BlockSpec((S, LANE), lambda i: (0, 0))
    return pl.pallas_call(body, grid=(1,), in_specs=[full, row, row, full], out_specs=[full, row, row],
                          out_shape=[sds((S, LANE), F32), sds((1, LANE), F32), sds((1, LANE), F32)],
                          compiler_params=_params(("arbitrary",)), name="gates_bwd")(ab, al, dt, dgb)


def _col_of(tile, h):
    lane = lax.broadcasted_iota(jnp.int32, tile.shape, 1)
    return jnp.sum(jnp.where(lane == h, tile, 0.0), axis=1, keepdims=True)


def _to_row(col, eye):
    return jnp.sum(jnp.where(eye, col, 0.0), axis=0, keepdims=True)


def _to_col(row, eye):
    return jnp.sum(jnp.where(eye, row, 0.0), axis=1, keepdims=True)


def _split(a):
    hi = a.astype(BF16)
    return hi, (a - hi.astype(F32)).astype(BF16)


def _gdot(a, b, dims):
    ah, al = _split(a)
    bh, bl = _split(b)
    return _dot(ah, bh, dims) + (_dot(ah, bl, dims) + _dot(al, bh, dims))


def _bdot(a, b, dims):
    return _dot(a.astype(BF16), b.astype(BF16), dims)


def _chunks_local(qs, ks, vs, gcols, bcols, Ts=None):
    C = CHUNK
    row = lax.broadcasted_iota(jnp.int32, (C, C), 0)
    col = lax.broadcasted_iota(jnp.int32, (C, C), 1)
    tril, strict, eye = col <= row, col < row, col == row
    outs = []
    for k, gcol, bcol in zip(ks, gcols, bcols):
        grow = _to_row(gcol, eye)
        G_row = jnp.sum(jnp.where(row <= col, gcol, 0.0), axis=0, keepdims=True)
        G_col = jnp.sum(jnp.where(tril, grow, 0.0), axis=1, keepdims=True)
        G_last = G_col[C - 1:C, :]
        outs.append(dict(strict=strict, eye=eye, row=row, col=col, decay=jnp.exp(jnp.where(tril, G_col - G_row, NEG_INF)),
                         eG=jnp.exp(G_col), eGr=jnp.exp(G_last - G_col), gl=jnp.exp(G_last), kb=k * bcol))
    Ms = [_gdot(o["kb"], k, NT) for o, k in zip(outs, ks)]
    Ns = [_gdot(q, k, NT) for q, k in zip(qs, ks)]
    for o, q, k, M, N in zip(outs, qs, ks, Ms, Ns):
        o.update(A=jnp.where(strict, M * o["decay"], 0.0), attn=N * o["decay"], rhs_k=o["kb"] * o["eG"],
                 q_dec=q * o["eG"], k_dec=k * o["eGr"])
    if Ts is None:
        Ts = [jnp.where(eye, 1.0, 0.0) - o["A"] for o in outs]
        Ps = [o["A"] for o in outs]
        for _ in range(int(math.log2(C)) - 1):
            Ps = [_bdot(P, P, NN) for P in Ps]
            Ts = [T + _bdot(T, P, NN) for T, P in zip(Ts, Ps)]
        us = [_bdot(T, v * bcol, NN) for T, v, bcol in zip(Ts, vs, bcols)]
        ws = [_bdot(T, o["rhs_k"], NN) for T, o in zip(Ts, outs)]
        for o, T, u, w in zip(outs, Ts, us, ws):
            o.update(T=T, u=u, w=w)
    return outs


GDN_ROWS = 256
GDN_LOCAL_ROWS = 1024
WQK = 3 * CHUNK


def _gdn_local_fwd(qkv, gb, nh):
    S = qkv.shape[0]
    nc = S // CHUNK
    rb = min(GDN_LOCAL_ROWS, S)
    cpb = rb // CHUNK
    sds = jax.ShapeDtypeStruct

    def body(q_ref, k_ref, v_ref, gb_ref, u_ref, wqk_ref, attn_ref, t_ref):
        h = pl.program_id(0)
        rows_of = [slice(ci * CHUNK, (ci + 1) * CHUNK) for ci in range(cpb)]
        gbts = [gb_ref[rows, :] for rows in rows_of]
        Ls = _chunks_local([q_ref[rows, :] for rows in rows_of], [k_ref[rows, :] for rows in rows_of],
                           [v_ref[rows, :] for rows in rows_of], [_col_of(t, h) for t in gbts],
                           [_col_of(t, nh + h) for t in gbts])
        for ci, (rows, L) in enumerate(zip(rows_of, Ls)):
            u_ref[rows, :] = L["u"]
            base = ci * WQK
            wqk_ref[base:base + CHUNK, :] = L["w"]
            wqk_ref[base + CHUNK:base + 2 * CHUNK, :] = L["q_dec"]
            wqk_ref[base + 2 * CHUNK:base + WQK, :] = L["k_dec"]
            attn_ref[ci] = L["attn"]
            t_ref[ci] = L["T"]

    cc = pl.BlockSpec((None, cpb, CHUNK, CHUNK), lambda h, i: (h, i, 0, 0))
    return pl.pallas_call(
        body, grid=(nh, S // rb),
        in_specs=[pl.BlockSpec((rb, HEAD_D), lambda h, i: (i, 3 * h)), pl.BlockSpec((rb, HEAD_D), lambda h, i: (i, 3 * h + 1)),
                  pl.BlockSpec((rb, HEAD_D), lambda h, i: (i, 3 * h + 2)), pl.BlockSpec((rb, LANE), lambda h, i: (i, 0))],
        out_specs=[pl.BlockSpec((rb, HEAD_D), lambda h, i: (i, h)),
                   pl.BlockSpec((None, 3 * rb, HEAD_D), lambda h, i: (h, i, 0)), cc, cc],
        out_shape=[sds((S, nh * HEAD_D), F32), sds((nh, 3 * S, HEAD_D), F32), sds((nh, nc, CHUNK, CHUNK), F32),
                   sds((nh, nc, CHUNK, CHUNK), F32)],
        compiler_params=_params(("parallel", "parallel")), name="gdn_local_fwd")(qkv, qkv, qkv, gb)


def _gdn_scan_fwd(u, wqk, attn, gb, nh, dep):
    S = u.shape[0]
    nc = S // CHUNK
    rb = min(GDN_ROWS, S)
    cpb = rb // CHUNK
    sds = jax.ShapeDtypeStruct

    def body(u_ref, wqk_ref, attn_ref, gb_ref, dep_ref, o_ref, vn_ref, st_ref, s_ref):
        @pl.when(pl.program_id(0) == 0)
        def _():
            s_ref[...] = jnp.zeros_like(s_ref)

        for ci in range(cpb):
            rows = slice(ci * CHUNK, (ci + 1) * CHUNK)
            glv = jnp.exp(jnp.sum(gb_ref[rows, :], axis=0, keepdims=True))
            base = ci * WQK
            heads = range(nh)
            cols = [slice(h * HEAD_D, (h + 1) * HEAD_D) for h in heads]
            states = [s_ref[h] for h in heads]
            rs = [_gdot(wqk_ref[h, base:base + 2 * CHUNK, :], states[h], NN) for h in heads]
            vbs = [u_ref[rows, cols[h]] - rs[h][:CHUNK] for h in heads]
            os_ = [_gdot(attn_ref[h, ci], vbs[h], NN) for h in heads]
            ks_ = [_gdot(wqk_ref[h, base + 2 * CHUNK:base + WQK, :], vbs[h], TN) for h in heads]
            for h in heads:
                st_ref[h, ci] = states[h]
                o_ref[rows, cols[h]] = rs[h][CHUNK:] + os_[h]
                vn_ref[rows, cols[h]] = vbs[h]
                s_ref[h] = states[h] * glv[:, h:h + 1] + ks_[h]

    return pl.pallas_call(
        body, grid=(S // rb,),
        in_specs=[pl.BlockSpec((rb, nh * HEAD_D), lambda i: (i, 0)), pl.BlockSpec((nh, 3 * rb, HEAD_D), lambda i: (0, i, 0)),
                  pl.BlockSpec((nh, cpb, CHUNK, CHUNK), lambda i: (0, i, 0, 0)), pl.BlockSpec((rb, LANE), lambda i: (i, 0)),
                  pl.BlockSpec(memory_space=pl.ANY)],
        out_specs=[pl.BlockSpec((rb, nh * HEAD_D), lambda i: (i, 0)), pl.BlockSpec((rb, nh * HEAD_D), lambda i: (i, 0)),
                   pl.BlockSpec((nh, cpb, HEAD_D, HEAD_D), lambda i: (0, i, 0, 0))],
        out_shape=[sds((S, nh * HEAD_D), F32), sds((S, nh * HEAD_D), F32), sds((nh, nc, HEAD_D, HEAD_D), F32)],
        scratch_shapes=[pltpu.VMEM((nh, HEAD_D, HEAD_D), F32)],
        compiler_params=_params(("arbitrary",)), name="gdn_scan_fwd")(u, wqk, attn, gb, dep)


def _gdn_scan_bwd(wqk, attn, gb, states, vn, do, nh, dep):
    S = vn.shape[0]
    nc = S // CHUNK
    rb = min(GDN_ROWS, S)
    cpb = rb // CHUNK
    last = S // rb - 1
    sds = jax.ShapeDtypeStruct

    def body(wqk_ref, attn_ref, gb_ref, st_ref, vn_ref, do_ref, dep_ref, dvn_ref, dw_ref, dqd_ref, dkd_ref, da_ref, dgl_ref,
             ds_ref):
        @pl.when(pl.program_id(0) == 0)
        def _():
            ds_ref[...] = jnp.zeros_like(ds_ref)

        row = lax.broadcasted_iota(jnp.int32, (CHUNK, CHUNK), 0)
        col = lax.broadcasted_iota(jnp.int32, (CHUNK, CHUNK), 1)
        for ci in reversed(range(cpb)):
            rows = slice(ci * CHUNK, (ci + 1) * CHUNK)
            glv = jnp.exp(jnp.sum(gb_ref[rows, :], axis=0, keepdims=True))
            base = ci * WQK
            heads = range(nh)
            cols = [slice(h * HEAD_D, (h + 1) * HEAD_D) for h in heads]
            states = [st_ref[h, ci] for h in heads]
            dSs = [ds_ref[h] for h in heads]
            vbs = [vn_ref[rows, cols[h]] for h in heads]
            dobs = [do_ref[rows, cols[h]] for h in heads]
            dv1 = [_gdot(attn_ref[h, ci], dobs[h], TN) for h in heads]
            dv2 = [_gdot(wqk_ref[h, base + 2 * CHUNK:base + WQK, :], dSs[h], NN) for h in heads]
            das = [_gdot(dobs[h], vbs[h], NT) for h in heads]
            dkds = [_gdot(vbs[h], dSs[h], NT) for h in heads]
            dvbs = [dv1[h] + dv2[h] for h in heads]
            xs = [_gdot(jnp.concatenate([dobs[h], dvbs[h]], axis=0), states[h], NT) for h in heads]
            dss = [_gdot(wqk_ref[h, base:base + 2 * CHUNK, :], jnp.concatenate([-dvbs[h], dobs[h]], axis=0), TN)
                   for h in heads]
            for h in heads:
                dqd_ref[rows, cols[h]] = xs[h][:CHUNK]
                dw_ref[rows, cols[h]] = -xs[h][CHUNK:]
                dvn_ref[rows, cols[h]] = dvbs[h]
                da_ref[h, ci] = jnp.where(col <= row, das[h], 0.0)
                dkd_ref[rows, cols[h]] = dkds[h]
                gl = glv[:, h:h + 1]
                dgl = jnp.sum(jnp.sum(states[h] * dSs[h], axis=0, keepdims=True), axis=1, keepdims=True)
                dgl_ref[h, ci] = jnp.broadcast_to(dgl * gl, (1, LANE))
                ds_ref[h] = dSs[h] * gl + dss[h]

    rv = lambda i: last - i
    wide = pl.BlockSpec((rb, nh * HEAD_D), lambda i: (rv(i), 0))
    return pl.pallas_call(
        body, grid=(S // rb,),
        in_specs=[pl.BlockSpec((nh, 3 * rb, HEAD_D), lambda i: (0, rv(i), 0)),
                  pl.BlockSpec((nh, cpb, CHUNK, CHUNK), lambda i: (0, rv(i), 0, 0)),
                  pl.BlockSpec((rb, LANE), lambda i: (rv(i), 0)),
                  pl.BlockSpec((nh, cpb, HEAD_D, HEAD_D), lambda i: (0, rv(i), 0, 0)), wide, wide,
                  pl.BlockSpec(memory_space=pl.ANY)],
        out_specs=[wide, wide, wide, wide, pl.BlockSpec((nh, cpb, CHUNK, CHUNK), lambda i: (0, rv(i), 0, 0)),
                   pl.BlockSpec((nh, cpb, 1, LANE), lambda i: (0, rv(i), 0, 0))],
        out_shape=[sds((S, nh * HEAD_D), F32), sds((S, nh * HEAD_D), F32), sds((S, nh * HEAD_D), F32),
                   sds((S, nh * HEAD_D), F32), sds((nh, nc, CHUNK, CHUNK), F32), sds((nh, nc, 1, LANE), F32)],
        scratch_shapes=[pltpu.VMEM((nh, HEAD_D, HEAD_D), F32)],
        compiler_params=_params(("arbitrary",)), name="gdn_scan_bwd")(wqk, attn, gb, states, vn, do, dep)


def _gdn_local_bwd(qkv, gb, T, u, wqk, dvn, dw, dqd, dkd, dattn, dgl, nh):
    S = qkv.shape[0]
    rb = min(GDN_LOCAL_ROWS, S)
    cpb = rb // CHUNK
    sds = jax.ShapeDtypeStruct

    def body(q_ref, k_ref, v_ref, gb_ref, t_ref, u_ref, wqk_ref, dvn_ref, dw_ref, dqd_ref, dkd_ref, da_ref, dgl_ref,
             dqkv_ref, dg_ref, db_ref):
        h = pl.program_id(0)
        n = range(cpb)
        rows_of = [slice(ci * CHUNK, (ci + 1) * CHUNK) for ci in n]
        qs, ks, vs = ([r[rows, :] for rows in rows_of] for r in (q_ref, k_ref, v_ref))
        gbts = [gb_ref[rows, :] for rows in rows_of]
        bcols = [_col_of(t, nh + h) for t in gbts]
        Ts = [t_ref[ci] for ci in n]
        Ls = _chunks_local(qs, ks, vs, [_col_of(t, h) for t in gbts], bcols, Ts=Ts)
        drvs = [_gdot(Ts[ci], dvn_ref[rows_of[ci], :], TN) for ci in n]
        drks = [_gdot(Ts[ci], dw_ref[rows_of[ci], :], TN) for ci in n]
        dAs = [jnp.where(Ls[ci]["strict"], -(_gdot(drvs[ci], u_ref[rows_of[ci], :], NT)
                                             + _gdot(drks[ci], wqk_ref[ci * WQK:ci * WQK + CHUNK, :], NT)), 0.0) for ci in n]
        dMs = [dAs[ci] * Ls[ci]["decay"] for ci in n]
        dNs = [da_ref[ci] * Ls[ci]["decay"] for ci in n]
        dkbs = [_gdot(dMs[ci], ks[ci], NN) for ci in n]
        dq1 = [_gdot(dNs[ci], ks[ci], NN) for ci in n]
        dk1 = [_gdot(dMs[ci], Ls[ci]["kb"], TN) for ci in n]
        dk2 = [_gdot(dNs[ci], qs[ci], TN) for ci in n]
        for ci in n:
            rows, L, q, k, v, bcol = rows_of[ci], Ls[ci], qs[ci], ks[ci], vs[ci], bcols[ci]
            eye, eG, eGr = L["eye"], L["eG"], L["eGr"]
            drv, drk, dkb = drvs[ci], drks[ci], dkbs[ci]
            dq_dec, dk_dec, dattn_c = dqd_ref[rows, :], dkd_ref[rows, :], da_ref[ci]
            dqkv_ref[rows, :HEAD_D] = dq1[ci] + dq_dec * eG
            dqkv_ref[rows, HEAD_D:2 * HEAD_D] = drk * (bcol * eG) + dk1[ci] + dkb * bcol + dk2[ci] + dk_dec * eGr
            dqkv_ref[rows, 2 * HEAD_D:] = drv * bcol
            db_ref[rows, :] = (jnp.sum(drv * v, axis=1, keepdims=True) + jnp.sum(drk * k, axis=1, keepdims=True) * eG
                               + jnp.sum(dkb * k, axis=1, keepdims=True))
            E = dAs[ci] * L["A"] + dattn_c * L["attn"]
            kd = jnp.sum(dk_dec * L["k_dec"], axis=1, keepdims=True)
            dG = (jnp.sum(dq_dec * L["q_dec"], axis=1, keepdims=True) - kd
                  + jnp.sum(drk * L["rhs_k"], axis=1, keepdims=True)
                  + jnp.sum(E, axis=1, keepdims=True) - _to_col(jnp.sum(E, axis=0, keepdims=True), eye))
            d_last = jnp.sum(kd, axis=0, keepdims=True) + dgl_ref[ci][:, :1]
            dG = dG + jnp.where(L["row"][:, :1] == CHUNK - 1, d_last, 0.0)
            dg_ref[rows, :] = jnp.sum(jnp.where(L["col"] >= L["row"], _to_row(dG, eye), 0.0), axis=1, keepdims=True)

    hd = pl.BlockSpec((rb, HEAD_D), lambda h, i: (i, h))
    cc = pl.BlockSpec((None, cpb, CHUNK, CHUNK), lambda h, i: (h, i, 0, 0))
    col1 = pl.BlockSpec((None, rb, 1), lambda h, i: (h, i, 0))
    return pl.pallas_call(
        body, grid=(nh, S // rb),
        in_specs=[pl.BlockSpec((rb, HEAD_D), lambda h, i: (i, 3 * h)), pl.BlockSpec((rb, HEAD_D), lambda h, i: (i, 3 * h + 1)),
                  pl.BlockSpec((rb, HEAD_D), lambda h, i: (i, 3 * h + 2)), pl.BlockSpec((rb, LANE), lambda h, i: (i, 0)),
                  cc, hd, pl.BlockSpec((None, 3 * rb, HEAD_D), lambda h, i: (h, i, 0)), hd, hd, hd, hd, cc,
                  pl.BlockSpec((None, cpb, 1, LANE), lambda h, i: (h, i, 0, 0))],
        out_specs=[pl.BlockSpec((rb, 3 * HEAD_D), lambda h, i: (i, h)), col1, col1],
        out_shape=[sds((S, 3 * nh * HEAD_D), F32)] + [sds((nh, S, 1), F32)] * 2,
        compiler_params=_params(("parallel", "parallel")), name="gdn_local_bwd")(
            qkv, qkv, qkv, gb, T, u, wqk, dvn, dw, dqd, dkd, dattn, dgl)


def _gated_norm_fwd(o, proj, norm_w, nh, z_blk0, mix, m_blk0):
    S = o.shape[0]

    def body(o_ref, z_ref, w_ref, mix_ref, y_ref):
        ov, z = o_ref[...], z_ref[...]
        r = lax.rsqrt(jnp.mean(ov * ov, axis=1, keepdims=True) + RMS_EPS)
        y_ref[...] = ov * r * w_ref[...] * (z * _sigmoid(z))

    return pl.pallas_call(
        body, grid=(nh,),
        in_specs=[pl.BlockSpec((S, HEAD_D), lambda h: (0, h)), pl.BlockSpec((S, HEAD_D), lambda h: (0, z_blk0 + h)),
                  pl.BlockSpec((1, HEAD_D), lambda h: (0, 0)), pl.BlockSpec(memory_space=pl.ANY)],
        out_specs=pl.BlockSpec((S, HEAD_D), lambda h: (0, m_blk0 + h)),
        out_shape=jax.ShapeDtypeStruct(mix.shape, F32), input_output_aliases={3: 0},
        compiler_params=_params(("parallel",)), name="gated_norm_fwd")(o, proj, norm_w, mix)


def _gated_norm_bwd(o, proj, norm_w, dmix, nh, z_blk0, d_blk0):
    S = o.shape[0]
    sds = jax.ShapeDtypeStruct

    def body(o_ref, z_ref, w_ref, dy_ref, do_ref, dz_ref, dw_ref):
        ov, z, w, dy = o_ref[...], z_ref[...], w_ref[...], dy_ref[...]
        r = lax.rsqrt(jnp.mean(ov * ov, axis=1, keepdims=True) + RMS_EPS)
        oh = ov * r
        sg = _sigmoid(z)
        dz_ref[...] = dy * (oh * w) * (sg * (1.0 + z * (1.0 - sg)))
        don = dy * (z * sg)
        @pl.when(pl.program_id(0) == 0)
        def _():
            dw_ref[...] = jnp.zeros_like(dw_ref)

        dw_ref[...] += jnp.sum(don * oh, axis=0, keepdims=True)
        doh = don * w
        do_ref[...] = r * (doh - oh * jnp.mean(doh * oh, axis=1, keepdims=True))

    return pl.pallas_call(
        body, grid=(nh,),
        in_specs=[pl.BlockSpec((S, HEAD_D), lambda h: (0, h)), pl.BlockSpec((S, HEAD_D), lambda h: (0, z_blk0 + h)),
                  pl.BlockSpec((1, HEAD_D), lambda h: (0, 0)), pl.BlockSpec((S, HEAD_D), lambda h: (0, d_blk0 + h))],
        out_specs=[pl.BlockSpec((S, HEAD_D), lambda h: (0, h)), pl.BlockSpec((S, HEAD_D), lambda h: (0, h)),
                   pl.BlockSpec((1, HEAD_D), lambda h: (0, 0))],
        out_shape=[sds((S, nh * HEAD_D), F32), sds((S, nh * HEAD_D), F32), sds((1, HEAD_D), F32)],
        compiler_params=_params(("arbitrary",)), name="gated_norm_bwd")(o, proj, norm_w, dmix)


def _adamw_math(w, g, m, v):
    m = ADAM_B1 * m + (1.0 - ADAM_B1) * g
    v = ADAM_B2 * v + (1.0 - ADAM_B2) * (g * g)
    m_hat = m / (1.0 - ADAM_B1 ** ADAM_STEP)
    v_hat = v / (1.0 - ADAM_B2 ** ADAM_STEP)
    delta = -ADAM_LR * (m_hat / (jnp.sqrt(v_hat) + ADAM_EPS) + ADAM_WD * w)
    return delta, m, v


def _slab_tiles(R, C, rows=256, cols=256):
    if R % rows == 0:
        return (rows, C), R // rows, lambda i: (i, 0)
    tc = _tile(C, cols)
    return (R, tc), C // tc, lambda i: (0, i)


def _adamw_big(parts, terms, chip, w, m, v, name):
    R, C = w.shape
    blk, steps, at = _slab_tiles(R, C)
    sds = jax.ShapeDtypeStruct

    def body(q_ref, p_ref, t_ref, w_ref, m_ref, v_ref, g_ref, d_ref, nm_ref, nv_ref):
        g = ((p_ref[...].astype(F32) + t_ref[0].astype(F32)) + t_ref[1].astype(F32)) + t_ref[2].astype(F32)
        g_ref[...] = g
        d_ref[...], nm_ref[...], nv_ref[...] = _adamw_math(w_ref[...], g, m_ref[...], v_ref[...])

    spec = pl.BlockSpec(blk, lambda i, q_ref: at(i))
    grid_spec = pltpu.PrefetchScalarGridSpec(
        num_scalar_prefetch=1, grid=(steps,),
        in_specs=[pl.BlockSpec((None,) + blk, lambda i, q_ref: (q_ref[0],) + at(i)),
                  pl.BlockSpec((3,) + blk, lambda i, q_ref: (0,) + at(i)), spec, spec, spec],
        out_specs=[spec] * 4)
    return pl.pallas_call(body, grid_spec=grid_spec, out_shape=[sds((R, C), F32)] * 4,
                          compiler_params=_params(("parallel",)), name=name)(chip, parts, terms, w, m, v)


def _adamw_small(ws, gs, ms, vs):
    n = len(ws)

    def body(*refs):
        for i in range(n):
            w, g, m, v = (refs[k * n + i][...] for k in range(4))
            d, nm, nv = _adamw_math(w, g, m, v)
            refs[4 * n + i][...] = d
            refs[5 * n + i][...] = nm
            refs[6 * n + i][...] = nv

    shapes = [jax.ShapeDtypeStruct(w.shape, F32) for w in ws]
    vm = pl.BlockSpec(memory_space=pltpu.VMEM)
    outs = pl.pallas_call(body, in_specs=[vm] * (4 * n), out_specs=[vm] * (3 * n), out_shape=shapes * 3,
                          name="adamw_small")(*ws, *gs, *ms, *vs)
    return outs[:n], outs[n:2 * n], outs[2 * n:]


MESH = pl.DeviceIdType.MESH
ANY = pl.BlockSpec(memory_space=pl.ANY)


def _place():
    x, y, c = lax.axis_index("x"), lax.axis_index("y"), lax.axis_index("c")
    return x, y, c, [(1 - x, y), (x, 1 - y), (1 - x, 1 - y)]


def _chip_sum(grad, recv, core, name):
    _, R, C = grad.shape
    blk, steps, at = _slab_tiles(R, C, rows=512 if R % 512 == 0 else 256, cols=512)

    def body(c_ref, g_ref, r_ref, o_ref):
        o_ref[...] = (g_ref[...].astype(F32) + r_ref[...].astype(F32)).astype(o_ref.dtype)

    grid_spec = pltpu.PrefetchScalarGridSpec(
        num_scalar_prefetch=1, grid=(4, steps),
        in_specs=[pl.BlockSpec((None,) + blk, lambda q, i, c_ref: (2 * q + c_ref[0],) + at(i)),
                  pl.BlockSpec((None,) + blk, lambda q, i, c_ref: (q,) + at(i))],
        out_specs=pl.BlockSpec((None,) + blk, lambda q, i, c_ref: (q,) + at(i)))
    return pl.pallas_call(body, grid_spec=grid_spec, out_shape=jax.ShapeDtypeStruct((4, R, C), BF16),
                          compiler_params=_params(("parallel", "parallel")), name=name)(core, grad, recv)


HBM_SPEC = pl.BlockSpec(memory_space=pltpu.HBM)
SEM_SPEC = pl.BlockSpec(memory_space=pltpu.SEMAPHORE)
DATAFLOW = pltpu.SideEffectType.DATAFLOW_SIDE_EFFECTING


def _split_start(name, bufs, plan, counts, after=None):
    nb, ng = len(bufs), len(counts)
    extra = [] if after is None else [after]
    place = [(g, k) for g, cnt in enumerate(counts) for k in range(cnt)]

    def body(*refs):
        sems, token = refs[nb + len(extra):nb + len(extra) + 2 * ng], refs[-1]
        for (g, k), (src, dst, to) in zip(place, plan(refs[:nb])):
            pltpu.make_async_remote_copy(src_ref=src, dst_ref=dst, send_sem=sems[2 * g].at[k], recv_sem=sems[2 * g + 1].at[k],
                                         device_id=to, device_id_type=MESH).start()
        token[...] = jnp.zeros_like(token)

    outs = pl.pallas_call(
        body, name=name,
        out_shape=(*[pltpu.SemaphoreType.DMA((cnt,)) for cnt in counts for _ in range(2)],
                   *[pltpu.HBM(b.shape, b.dtype) for b in bufs], jax.ShapeDtypeStruct((8, LANE), F32)),
        in_specs=[HBM_SPEC] * nb + [ANY] * len(extra),
        out_specs=(*[SEM_SPEC] * (2 * ng), *[HBM_SPEC] * nb, pl.BlockSpec(memory_space=pltpu.VMEM)),
        input_output_aliases={i: 2 * ng + i for i in range(nb)},
        compiler_params=pltpu.CompilerParams(has_side_effects=DATAFLOW))(
            *[pltpu.with_memory_space_constraint(b, pltpu.HBM) for b in bufs], *extra)
    return [(outs[2 * g], outs[2 * g + 1]) for g in range(ng)], list(outs[2 * ng:2 * ng + nb]), outs[-1]


def _split_wait(name, sems, bufs, plan, after):
    nb = len(bufs)
    send_sems, recv_sems = sems

    def body(*refs):
        send_s, recv_s = refs[nb], refs[nb + 1]
        for k, (src, dst, to) in enumerate(plan(refs[:nb])):
            cp = pltpu.make_async_remote_copy(src_ref=src, dst_ref=dst, send_sem=send_s.at[k], recv_sem=recv_s.at[k],
                                              device_id=to, device_id_type=MESH)
            cp.wait_send()
            cp.wait_recv()

    after = tuple(after) if isinstance(after, (tuple, list)) else (after,)
    outs = pl.pallas_call(
        body, name=name, out_shape=tuple(pltpu.HBM(b.shape, b.dtype) for b in bufs),
        in_specs=[HBM_SPEC] * nb + [SEM_SPEC, SEM_SPEC] + [ANY] * len(after), out_specs=tuple([HBM_SPEC] * nb),
        input_output_aliases={i: i for i in range(nb)},
        compiler_params=pltpu.CompilerParams(has_side_effects=DATAFLOW))(*bufs, send_sems, recv_sems, *after)
    return list(outs)


def _slot(px, py, pc):
    return 4 * px + 2 * py + pc


class _Gather:
    def __init__(self, shards, groups, dev):
        self.shards, self.groups, self.dev = shards, groups, dev
        self.second = {}

    @staticmethod
    def _plan1(pairs, refs):
        x, y, c, chips = _place()
        out = []
        for s, l in pairs:
            dst = refs[l].at[_slot(x, y, c)]
            out.append((refs[s], dst, (x, y, 1 - c)))
            out += [(refs[s], dst, (px, py, c)) for px, py in chips]
            out.append((refs[s], dst, (x, y, c)))
        return out

    @staticmethod
    def _plan2(refs):
        x, y, c, chips = _place()
        return [(r.at[_slot(px, py, c)],) * 2 + ((x, y, 1 - c),) for r in refs for px, py in chips]

    def start(self):
        n = len(self.shards)
        lands = [lax.empty((N_DEV,) + s.shape, s.dtype) for s in self.shards]
        pairs = [(w, n + w) for g in self.groups for w in g]
        sems, bufs, token = _split_start("gather_start_1", list(self.shards) + lands, functools.partial(self._plan1, pairs),
                                         tuple(5 * len(g) for g in self.groups))
        self.first = [(sems[i], [bufs[w] for w in g], [bufs[n + w] for w in g]) for i, g in enumerate(self.groups)]
        return token

    def mid(self, gi, after):
        sems, srcs, lands = self.first[gi]
        m = len(srcs)
        plan = functools.partial(self._plan1, [(w, m + w) for w in range(m)])
        lands = _split_wait("gather_%d_wait_1" % gi, sems, srcs + lands, plan, after)[m:]
        sems, lands, token = _split_start("gather_%d_start_2" % gi, lands, self._plan2, (3 * m,))
        self.second[gi] = (sems[0], lands)
        return token

    def finish(self, gi, after):
        sems, lands = self.second[gi]
        return _split_wait("gather_%d_wait_2" % gi, sems, lands, self._plan2, after)


class _Exchanges:
    def __init__(self, tag, n, core, gather=None):
        self.tag, self.n, self.core, self.gather = tag, n, core, gather

    def weights_mid(self, group, after):
        return self.gather.mid(group, after)

    def weights_finish(self, group, after):
        return self.gather.finish(group, after)

    def _reduce_plan1(self, refs):
        n = self.n
        x, y, c, _ = _place()
        return [(refs[w].at[2 * q + (1 - c)], refs[n + w].at[q], (x, y, 1 - c)) for w in range(n) for q in range(4)]

    def _reduce_plan2(self, refs):
        n = self.n
        x, y, c, chips = _place()
        return [(refs[w].at[2 * px + py], refs[n + w].at[j], (px, py, c))
                for w in range(n) for j, (px, py) in enumerate(chips)]

    def grads_start(self, grads):
        lands = [lax.empty((4,) + g.shape[1:], g.dtype) for g in grads]
        self.g1 = _split_start(self.tag + "reduce_start_1", list(grads) + lands, self._reduce_plan1, (4 * self.n,))
        return self.g1[2]

    def grads_mid(self, after):
        n = self.n
        sems, bufs, _ = self.g1
        bufs = _split_wait(self.tag + "reduce_wait_1", sems[0], bufs, self._reduce_plan1, after)
        core = self.core.reshape(1).astype(jnp.int32)
        self.parts = [_chip_sum(bufs[w], bufs[n + w], core, self.tag + "reduce_chip_sum_%d" % w) for w in range(n)]
        lands = [lax.empty((3,) + p.shape[1:], p.dtype) for p in self.parts]
        self.g2 = _split_start(self.tag + "reduce_start_2", self.parts + lands, self._reduce_plan2, (3 * n,))
        return self.g2[2]

    def grads_finish(self, after):
        n = self.n
        sems, bufs, _ = self.g2
        bufs = _split_wait(self.tag + "reduce_wait_2", sems[0], bufs, self._reduce_plan2, after)
        self.parts, self.terms = bufs[:n], bufs[n:]


def _all_reduce_small(buf):
    R = buf.shape[0]

    def body(x_ref, o_ref, g_ref, send_sems, recv_sems):
        x, y, c, chips = _place()
        me, sibling = (x, y, c), (x, y, 1 - c)

        def slot(px, py, pc):
            return 4 * px + 2 * py + pc

        def copy(k, block, to, src=None):
            dst = g_ref.at[slot(*block)]
            return pltpu.make_async_remote_copy(src_ref=dst if src is None else src, dst_ref=dst,
                                                send_sem=send_sems.at[k], recv_sem=recv_sems.at[k],
                                                device_id=to, device_id_type=MESH)

        first = [copy(0, me, sibling, src=x_ref)]
        first += [copy(1 + j, me, (*chip, c), src=x_ref) for j, chip in enumerate(chips)]
        for cp in first:
            cp.start()
        g_ref[slot(*me)] = x_ref[...]
        passed = [copy(4 + j, (*chip, c), sibling) for j, chip in enumerate(chips)]
        for j, chip in enumerate(chips):
            copy(1 + j, (*chip, c), me).wait_recv()
            passed[j].start()
        copy(0, sibling, me).wait_recv()
        for j, chip in enumerate(chips):
            copy(4 + j, (*chip, 1 - c), me).wait_recv()
        for cp in first + passed:
            cp.wait_send()
        acc = g_ref[0]
        for s in range(1, N_DEV):
            acc = acc + g_ref[s]
        o_ref[...] = acc

    vm = pl.BlockSpec(memory_space=pltpu.VMEM)
    return pl.pallas_call(
        body, in_specs=[vm], out_specs=vm, out_shape=jax.ShapeDtypeStruct((R, LANE), F32),
        scratch_shapes=[pltpu.VMEM((N_DEV, R, LANE), F32), pltpu.SemaphoreType.DMA((7,)), pltpu.SemaphoreType.DMA((7,))],
        name="all_reduce_small")(buf)


def _pad_cols(a, width):
    return jnp.pad(a, ((0, 0), (0, width - a.shape[1])))


def _local_step(x, target, w_in_g, conv_w, a_log, dt_bias, delta_norm_w, sinks, bias, ln1_g, ln1_b, ln2_g, ln2_b,
                ex, ex_in):
    S, D = x.shape
    aw = D // 2
    hq, hkv, nh = aw // HEAD_A, aw // HEAD_A // GQA, aw // HEAD_D
    kvw = hkv * HEAD_A
    c_q, c_k, c_v, c_d = 0, aw, aw + kvw, aw + 2 * kvw
    c_ab = c_d + 3 * aw
    c_z = c_ab + 2 * nh
    n_in = c_z + aw
    n_slab = w_in_g.shape[1]
    assert w_in_g.shape == (N_DEV, n_in // N_DEV, D), (w_in_g.shape, n_in)

    w_in_t = w_in_g.reshape(n_in, D)
    w_pt = jnp.concatenate([w_in_t[:c_ab], w_in_t[c_z:], jnp.pad(w_in_t[c_ab:c_z], ((0, LANE - 2 * nh), (0, 0)))], axis=0)
    p_z, p_ab = c_ab, c_ab + aw
    n_p = p_ab + LANE

    xb = x.astype(BF16)
    proj = _matmul(xb, w_pt, NT, name="proj", tn=1152)
    attn_out = _attn_fwd(proj, bias, sinks.reshape(-1), hq, 0, c_k // kvw, c_v // kvw, D)
    conv2 = conv_w.reshape(CONV_W, 3 * aw)
    qkv = _gdn_prep_fwd(proj, conv2, nh, c_d // HEAD_D)
    ab = proj[:, p_ab:]
    al, dt = _pad_cols(a_log, LANE), _pad_cols(dt_bias, LANE)
    gb = _gates_fwd(ab, al, dt, nh)
    u_d, wqk, attn_d, t_d = _gdn_local_fwd(qkv, gb, nh)
    o_d, vn, states = _gdn_scan_fwd(u_d, wqk, attn_d, gb, nh, ex.weights_mid(1, u_d))
    mix = _gated_norm_fwd(o_d, proj, delta_norm_w, nh, p_z // HEAD_D, attn_out, aw // HEAD_D)
    w_o_g, w_up_g = ex.weights_finish(1, mix)
    w_o = w_o_g.reshape(D, D)
    mixed = _matmul(mix, w_o, NN, name="out_proj")
    h1 = _ln1_fwd(x, mixed, ln1_g, ln1_b)
    u = _matmul(h1, w_up_g, NN, name="mlp_up", b_groups=True, out_dtype=BF16, deps=(ex.weights_mid(2, h1),))
    (w_down_g,) = ex.weights_finish(2, u)
    w_down = w_down_g.reshape(-1, D)
    mlp = _matmul(u, w_down, NN, name="mlp_down", a_fn=_relu_sq, tk=4096)
    dr2, loss_row, dln2_g, dln2_b = _ln2_loss(h1, mlp, ln2_g, ln2_b, target)

    du = _matmul(dr2, w_down, NT, name="d_mlp_act", epi=_relu_sq_grad, epi_in=(u,), out_dtype=BF16)
    dw_down = _matmul(u, dr2, TN, name="dw_down", a_fn=_relu_sq, out_dtype=BF16)
    dw_up = _matmul(h1, du, TN, name="dw_up", out_dtype=BF16, out_groups=N_DEV)
    dh_mlp = _matmul(du, w_up_g, NT, name="d_h1", b_groups=True, tk=4096)
    dr1, dln1_g, dln1_b = _ln1_bwd(x, mixed, ln1_g, dr2, dh_mlp)
    dw_o = _matmul(mix, dr1, TN, name="dw_o", out_dtype=BF16)
    tok = ex.grads_start([dw_o.reshape(N_DEV, -1, D), dw_up, dw_down.reshape(N_DEV, -1, D)])
    dmix = _matmul(dr1, w_o, NT, name="d_mix", deps=(tok,))
    dq_a, dk_a, dv_a, dbias, dsink = _attn_bwd(proj, bias, sinks.reshape(-1), mix, dmix, hq, 0, c_k // kvw, c_v // kvw)
    drel = _rel_bias_grad(dbias, hq)
    do_d, dz, dnw = _gated_norm_bwd(o_d, proj, delta_norm_w, dmix, nh, p_z // HEAD_D, aw // HEAD_D)
    dvn_s, dw_s, dqd, dkd, dattn_d, dgl = _gdn_scan_bwd(wqk, attn_d, gb, states, vn, do_d, nh, ex.grads_mid(dq_a))
    dqkv_n, dg, dbeta = _gdn_local_bwd(qkv, gb, t_d, u_d, wqk, dvn_s, dw_s, dqd, dkd, dattn_d, dgl, nh)
    dgb = _pad_cols(jnp.concatenate([dg.reshape(nh, S).T, dbeta.reshape(nh, S).T], axis=1), LANE)
    dab, da_log, ddt_bias = _gates_bwd(ab, al, dt, dgb, nh)
    dqkv_d, dconv = _gdn_prep_bwd(proj, conv2, dqkv_n, nh, c_d // HEAD_D)
    dproj = jnp.concatenate([dq_a, dk_a, dv_a, dqkv_d, dz, dab], axis=1).astype(BF16)
    dw_pt = _matmul(dproj, xb, TN, name="dw_in", out_dtype=BF16, tm=1152)

    def grad_rows(lo, hi):
        cuts = sorted({lo, hi, *[c for c in (c_ab, c_z) if lo < c < hi]})
        place = lambda r: r if r < c_ab else (p_ab + r - c_ab if r < c_z else p_z + r - c_z)
        return [dw_pt[place(a):place(a) + b - a] for a, b in zip(cuts[:-1], cuts[1:])]

    dw_in_g = jnp.stack([jnp.concatenate(grad_rows(g * n_slab, (g + 1) * n_slab), axis=0) for g in range(N_DEV)])
    tok = ex_in.grads_mid(ex_in.grads_start([dw_in_g]))
    grad_x = _matmul(dproj, w_pt, NN, name="d_x", tm=512, tk=n_p, deps=(tok,), epi=_residual_grad, epi_in=(dr1,))
    ex.grads_finish(grad_x)

    small = dict(conv_w=dconv, a_log=da_log[:, :nh], dt_bias=ddt_bias[:, :nh], delta_norm_w=dnw,
                 attn_sinks=dsink[:, 0].reshape(1, hq), rel_bias=drel[:, :, 0].T,
                 ln1_g=dln1_g, ln1_b=dln1_b, ln2_g=dln2_g, ln2_b=dln2_b)
    return loss_row, grad_x, small


SMALL_ORDER = ("conv_w", "a_log", "dt_bias", "delta_norm_w", "attn_sinks", "rel_bias", "ln1_g", "ln1_b", "ln2_g", "ln2_b")


def _pack_small(loss_row, small):
    parts = [loss_row.reshape(-1)]
    for k in SMALL_ORDER:
        flat = small[k].reshape(-1)
        parts.append(jnp.pad(flat, (0, (-flat.shape[0]) % LANE)))
    flat = jnp.concatenate(parts)
    flat = jnp.pad(flat, (0, (-flat.shape[0]) % (8 * LANE)))
    return flat.reshape(-1, LANE)


def _unpack_small(buf, small_shapes):
    flat = buf.reshape(-1)
    loss = flat[0]
    off = LANE
    out = {}
    for k in SMALL_ORDER:
        n = int(np.prod(small_shapes[k]))
        out[k] = flat[off:off + n].reshape(small_shapes[k])
        off += n + (-n) % LANE
    return loss, out


def kernel(x, w_in, conv_w, a_log, dt_bias, delta_norm_w, attn_sinks, rel_bias, w_o, ln1_g, ln1_b, w_up, w_down, ln2_g, ln2_b, loss_target, m_w_in, m_conv_w, m_a_log, m_dt_bias, m_delta_norm_w, m_attn_sinks, m_rel_bias, m_w_o, m_ln1_g, m_ln1_b, m_w_up, m_w_down, m_ln2_g, m_ln2_b, v_w_in, v_conv_w, v_a_log, v_dt_bias, v_delta_norm_w, v_attn_sinks, v_rel_bias, v_w_o, v_ln1_g, v_ln1_b, v_w_up, v_w_down, v_ln2_g, v_ln2_b):
    S, D = x.shape[1], x.shape[2]
    core = lax.axis_index("c")
    dev = 4 * lax.axis_index("x") + 2 * lax.axis_index("y") + core

    gather = _Gather([conv_w[0, :, 0, :], w_in[0].T.astype(BF16), w_o[0].astype(BF16), w_up[0].astype(BF16),
                      w_down[0].astype(BF16)], [[0, 1], [2, 3], [4]], dev)
    token = gather.start()
    bias = _attn_bias(rel_bias.T)
    w_t, m_t, v_t = w_in[0].T, m_w_in[0].T, v_w_in[0].T
    conv_g, w_in_g = gather.finish(0, gather.mid(0, (token, bias, w_t, m_t, v_t)))
    ex = _Exchanges("", 3, core, gather)
    ex_in = _Exchanges("in_", 1, core)

    cw_sh = conv_w.shape[3]
    conv_full = jnp.transpose(conv_g, (1, 0, 2)).reshape(CONV_W, N_DEV * cw_sh)

    loss_row, grad_x, small = _local_step(
        x[0], loss_target[0], w_in_g, conv_full, a_log, dt_bias, delta_norm_w, attn_sinks, bias,
        ln1_g, ln1_b, ln2_g, ln2_b, ex, ex_in)

    chip_arr = (dev // 2).reshape(1).astype(jnp.int32)
    big = {}
    for i, (name, w, m, v) in enumerate((("w_o", w_o, m_w_o, v_w_o), ("w_up", w_up, m_w_up, v_w_up),
                                         ("w_down", w_down, m_w_down, v_w_down))):
        big[name] = [o[None] for o in _adamw_big(ex.parts[i], ex.terms[i], chip_arr, w[0], m[0], v[0], "adamw_" + name)]
    ex_in.grads_finish(big["w_down"][0])
    outs = _adamw_big(ex_in.parts[0], ex_in.terms[0], chip_arr, w_t, m_t, v_t, "adamw_w_in")
    big["w_in"] = [o.T[None] for o in outs]

    small_shapes = {k: v.shape for k, v in small.items()}
    loss, small = _unpack_small(_all_reduce_small(_pack_small(loss_row, small)), small_shapes)
    small["conv_w"] = lax.dynamic_slice(small["conv_w"], (0, dev * cw_sh), (CONV_W, cw_sh))
    small["rel_bias"] = small["rel_bias"].reshape(rel_bias.shape)
    p2 = dict(conv_w=(conv_w, m_conv_w, v_conv_w), a_log=(a_log, m_a_log, v_a_log), dt_bias=(dt_bias, m_dt_bias, v_dt_bias),
              delta_norm_w=(delta_norm_w, m_delta_norm_w, v_delta_norm_w), attn_sinks=(attn_sinks, m_attn_sinks, v_attn_sinks),
              rel_bias=(rel_bias, m_rel_bias, v_rel_bias), ln1_g=(ln1_g, m_ln1_g, v_ln1_g), ln1_b=(ln1_b, m_ln1_b, v_ln1_b),
              ln2_g=(ln2_g, m_ln2_g, v_ln2_g), ln2_b=(ln2_b, m_ln2_b, v_ln2_b))
    two_d = lambda a: a.reshape(-1, a.shape[-1])
    ws = [two_d(p2[k][0]) for k in SMALL_ORDER]
    gs = [two_d(small[k]) for k in SMALL_ORDER]
    ms = [two_d(p2[k][1]) for k in SMALL_ORDER]
    vs = [two_d(p2[k][2]) for k in SMALL_ORDER]
    ds, nms, nvs = _adamw_small(ws, gs, ms, vs)
    res = {}
    for i, k in enumerate(SMALL_ORDER):
        shp = p2[k][0].shape
        res[k] = [gs[i].reshape(shp), ds[i].reshape(shp), nms[i].reshape(shp), nvs[i].reshape(shp)]
    res.update(big)
    order = ("w_in", "conv_w", "a_log", "dt_bias", "delta_norm_w", "attn_sinks", "rel_bias", "w_o", "ln1_g", "ln1_b",
             "w_up", "w_down", "ln2_g", "ln2_b")
    return (loss, grad_x[None], *[res[k][0] for k in order], *[res[k][1] for k in order],
            *[res[k][2] for k in order], *[res[k][3] for k in order])
```
